```python
import math
import jax, jax.numpy as jnp
from jax import lax
import numpy as np

D_MODEL = 2048
BATCH = 16
SEQ = 2048
DEPTH = 1

CHUNK = 64
PLE_DIM = 256
EPS = 1e-6
GLA_HEADS = 4
GLA_DK = D_MODEL // (2 * GLA_HEADS)
GLA_DV = D_MODEL // GLA_HEADS
GLA_LOWRANK = 16
GLA_TAU = 16.0
DN_HEADS = 16
DN_DK = D_MODEL // DN_HEADS
DN_DV = D_MODEL // DN_HEADS
DN_CONV = 4
D_FF = 4 * D_MODEL

GLA_QK = GLA_HEADS * GLA_DK
GLA_V = GLA_HEADS * GLA_DV
DN_QK = DN_HEADS * DN_DK
DN_V = DN_HEADS * DN_DV
DN_QKV = 2 * DN_QK + DN_V
IN_SPLITS = (GLA_QK, GLA_QK, GLA_V, GLA_V, GLA_LOWRANK, DN_QKV, DN_V, DN_HEADS, DN_HEADS, D_MODEL, D_MODEL)
D_IN = 2 * GLA_QK + 2 * GLA_V + GLA_LOWRANK + DN_QKV + DN_V + 2 * DN_HEADS + 2 * D_MODEL

kernel_name = "hybrid_gla_gated_deltanet_block"


def rms_norm(x, g):
    xf = x.astype(jnp.float32)
    y = xf * lax.rsqrt(jnp.mean(xf * xf, axis=-1, keepdims=True) + EPS)
    return (y * g.astype(jnp.float32)).astype(x.dtype)


def head_rms_norm(o, g):
    return o * lax.rsqrt(jnp.mean(o * o, axis=-1, keepdims=True) + EPS) * g.astype(jnp.float32)


def l2_normalize(t):
    return t * lax.rsqrt(jnp.sum(t * t, axis=-1, keepdims=True) + EPS)


def split_cols(z, sizes):
    out, off = [], 0
    for s in sizes:
        out.append(z[..., off:off + s])
        off += s
    return out


def to_chunks(t, n_heads):
    b, s, _ = t.shape
    return t.astype(jnp.float32).reshape(b, s // CHUNK, CHUNK, n_heads, -1).transpose(0, 3, 1, 2, 4)


def heads_to_chunks(t):
    b, s, h = t.shape
    return t.astype(jnp.float32).reshape(b, s // CHUNK, CHUNK, h).transpose(0, 3, 1, 2)


def from_chunks(t):
    b, h, nc, c, d = t.shape
    return t.transpose(0, 2, 3, 1, 4).reshape(b, nc * c, h * d)


def causal_depthwise_conv(x, w):
    k, c = w.shape
    return lax.conv_general_dilated(x, w[:, None, :].astype(x.dtype), window_strides=(1,),
                                    padding=[(k - 1, 0)], dimension_numbers=('NWC', 'WIO', 'NWC'),
                                    feature_group_count=c)


def chunk_major(t):
    return jnp.moveaxis(t, 2, 0)


def gla_mixer(q, k, v, log_f):
    b, h, nc, c, dk = q.shape
    dv = v.shape[-1]
    bcum = jnp.cumsum(log_f, axis=3)
    q_in = q * jnp.exp(bcum)
    k_in = k * jnp.exp(-bcum)
    causal = jnp.tril(jnp.ones((c, c), dtype=bool))
    a = jnp.where(causal, jnp.einsum('bhncd,bhnsd->bhncs', q_in, k_in), 0.0)
    o_intra = jnp.einsum('bhncs,bhnsv->bhncv', a, v)
    b_last = bcum[:, :, :, -1, :]
    k_dec = k * jnp.exp(b_last[:, :, :, None, :] - bcum)

    def step(state, xs):
        q_c, k_c, v_c, f_last = xs
        o = jnp.einsum('bhcd,bhdv->bhcv', q_c, state)
        state = state * f_last[..., None] + jnp.einsum('bhcd,bhcv->bhdv', k_c, v_c)
        return state, o

    s0 = jnp.zeros((b, h, dk, dv), jnp.float32)
    _, o_inter = lax.scan(step, s0, (chunk_major(q_in), chunk_major(k_dec), chunk_major(v),
                                     chunk_major(jnp.exp(b_last))))
    return o_intra + jnp.moveaxis(o_inter, 0, 2)


def gated_delta_mixer(q, k, v, g, beta):
    b, h, nc, c, dk = q.shape
    dv = v.shape[-1]
    gcum = jnp.cumsum(g, axis=-1)
    incl = jnp.tril(jnp.ones((c, c), dtype=bool))
    strict = jnp.tril(jnp.ones((c, c), dtype=bool), -1)
    decay = jnp.exp(jnp.where(incl, gcum[..., :, None] - gcum[..., None, :], -jnp.inf))
    k_beta = k * beta[..., None]
    a = jnp.where(strict, jnp.einsum('bhncd,bhnsd->bhncs', k_beta, k) * decay, 0.0)
    eye = jnp.eye(c, dtype=jnp.float32)
    t_mat = lax.linalg.triangular_solve(eye + a, jnp.broadcast_to(eye, a.shape), left_side=True,
                                        lower=True, unit_diagonal=True)
    u = jnp.einsum('bhncs,bhnsv->bhncv', t_mat, v * beta[..., None])
    w = jnp.einsum('bhncs,bhnsd->bhncd', t_mat, k_beta * jnp.exp(gcum)[..., None])
    attn = jnp.where(incl, jnp.einsum('bhncd,bhnsd->bhncs', q, k) * decay, 0.0)
    q_dec = q * jnp.exp(gcum)[..., None]
    g_last = gcum[..., -1]
    k_dec = k * jnp.exp(g_last[..., None] - gcum)[..., None]

    def step(state, xs):
        q_c, k_c, u_c, w_c, attn_c, f_last = xs
        v_new = u_c - jnp.einsum('bhcd,bhdv->bhcv', w_c, state)
        o = jnp.einsum('bhcd,bhdv->bhcv', q_c, state) + jnp.einsum('bhcs,bhsv->bhcv', attn_c, v_new)
        state = state * f_last[..., None, None] + jnp.einsum('bhcd,bhcv->bhdv', k_c, v_new)
        return state, o

    s0 = jnp.zeros((b, h, dk, dv), jnp.float32)
    _, o = lax.scan(step, s0, (chunk_major(q_dec), chunk_major(k_dec), chunk_major(u), chunk_major(w),
                               chunk_major(attn), chunk_major(jnp.exp(g_last))))
    return jnp.moveaxis(o, 0, 2)


def hybrid_layer(x, p_i, g_mix, w_in, gla_w2, gla_b, gla_norm, dn_conv, dn_a_log, dn_dt_bias, dn_norm,
                 w_out, g_mlp, w_up, w_down, g_ple, w_ple_gate, w_ple_proj):
    f32 = jnp.float32
    h = rms_norm(x, g_mix)
    z = h @ w_in
    (gla_q, gla_k, gla_v, gla_g, gla_lr, dn_qkv, dn_z, dn_a, dn_b, gate_a, gate_b) = split_cols(z, IN_SPLITS)

    log_f = jax.nn.log_sigmoid((gla_lr @ gla_w2 + gla_b).astype(f32)) / GLA_TAU
    o_gla = gla_mixer(to_chunks(gla_q, GLA_HEADS) * (GLA_DK ** -0.5), to_chunks(gla_k, GLA_HEADS),
                      to_chunks(gla_v, GLA_HEADS), to_chunks(log_f, GLA_HEADS))
    o_gla = from_chunks(head_rms_norm(o_gla, gla_norm)) * jax.nn.silu(gla_g.astype(f32))

    qkv = jax.nn.silu(causal_depthwise_conv(dn_qkv, dn_conv))
    dq, dk, dv = split_cols(qkv, (DN_QK, DN_QK, DN_V))
    dq = l2_normalize(to_chunks(dq, DN_HEADS)) * (DN_DK ** -0.5)
    dk = l2_normalize(to_chunks(dk, DN_HEADS))
    a_neg = -jnp.exp(dn_a_log.astype(f32))[:, None, None]
    g = a_neg * jax.nn.softplus(heads_to_chunks(dn_a) + dn_dt_bias.astype(f32)[:, None, None])
    beta = jax.nn.sigmoid(heads_to_chunks(dn_b))
    o_dn = gated_delta_mixer(dq, dk, to_chunks(dv, DN_HEADS), g, beta)
    o_dn = from_chunks(head_rms_norm(o_dn, dn_norm)) * jax.nn.silu(dn_z.astype(f32))

    mixed = (jax.nn.sigmoid(gate_a.astype(f32)) * o_gla + jax.nn.sigmoid(gate_b.astype(f32)) * o_dn).astype(x.dtype)
    x = x + mixed @ w_out

    h2 = rms_norm(x, g_mlp)
    x = x + jnp.square(jax.nn.relu(h2 @ w_up)) @ w_down

    h3 = rms_norm(x, g_ple)
    x = x + jax.nn.sigmoid(h3 @ w_ple_gate) * (p_i @ w_ple_proj)
    return x


def _fwd_setup_inputs(seed: int = 0) -> dict:
    key = jax.random.key(seed)
    ks = jax.random.split(key, 24)
    f32 = jnp.float32

    def nrm(k, shape, scale):
        return jax.random.normal(k, shape, f32) * scale

    def gain(k, shape):
        return 1.0 + 0.02 * jax.random.normal(k, shape, f32)

    dt = jnp.exp(jax.random.uniform(ks[10], (DEPTH, DN_HEADS), f32, math.log(1e-3), math.log(1e-1)))
    return {
        "x": nrm(ks[0], (BATCH, SEQ, D_MODEL), 1.0),
        "p": nrm(ks[1], (DEPTH, BATCH, SEQ, PLE_DIM), 1.0),
        "g_mix": gain(ks[2], (DEPTH, D_MODEL)),
        "w_in": nrm(ks[3], (DEPTH, D_MODEL, D_IN), D_MODEL ** -0.5),
        "gla_w2": nrm(ks[4], (DEPTH, GLA_LOWRANK, GLA_QK), GLA_LOWRANK ** -0.5),
        "gla_b": nrm(ks[5], (DEPTH, GLA_QK), 0.1),
        "gla_norm": gain(ks[6], (DEPTH, GLA_DV)),
        "dn_conv": nrm(ks[7], (DEPTH, DN_CONV, DN_QKV), DN_CONV ** -0.5),
        "dn_a_log": jnp.log(jax.random.uniform(ks[8], (DEPTH, DN_HEADS), f32, 1.0, 16.0)),
        "dn_dt_bias": dt + jnp.log(-jnp.expm1(-dt)),
        "dn_norm": gain(ks[9], (DEPTH, DN_DV)),
        "w_out": nrm(ks[11], (DEPTH, D_MODEL, D_MODEL), D_MODEL ** -0.5),
        "g_mlp": gain(ks[12], (DEPTH, D_MODEL)),
        "w_up": nrm(ks[13], (DEPTH, D_MODEL, D_FF), D_MODEL ** -0.5),
        "w_down": nrm(ks[14], (DEPTH, D_FF, D_MODEL), D_FF ** -0.5),
        "g_ple": gain(ks[15], (DEPTH, D_MODEL)),
        "w_ple_gate": nrm(ks[16], (DEPTH, D_MODEL, D_MODEL), D_MODEL ** -0.5),
        "w_ple_proj": nrm(ks[17], (DEPTH, PLE_DIM, D_MODEL), PLE_DIM ** -0.5),
        "g_final": gain(ks[18], (D_MODEL,)),
    }


def _fwd_reference(x, p, g_mix, w_in, gla_w2, gla_b, gla_norm, dn_conv, dn_a_log, dn_dt_bias, dn_norm,
              w_out, g_mlp, w_up, w_down, g_ple, w_ple_gate, w_ple_proj, g_final):
    for i in range(DEPTH):
        x = hybrid_layer(x, p[i], g_mix[i], w_in[i], gla_w2[i], gla_b[i], gla_norm[i], dn_conv[i],
                         dn_a_log[i], dn_dt_bias[i], dn_norm[i], w_out[i], g_mlp[i], w_up[i], w_down[i],
                         g_ple[i], w_ple_gate[i], w_ple_proj[i])
    return rms_norm(x, g_final)


import jax as _jax
import jax.numpy as _jnp

TWIN_FORMAT = 'train_step'
FWD_PARAMS = ['x', 'p', 'g_mix', 'w_in', 'gla_w2', 'gla_b', 'gla_norm', 'dn_conv', 'dn_a_log', 'dn_dt_bias', 'dn_norm', 'w_out', 'g_mlp', 'w_up', 'w_down', 'g_ple', 'w_ple_gate', 'w_ple_proj', 'g_final']
TWIN_WEIGHTS = ['g_mix', 'w_in', 'gla_w2', 'gla_b', 'gla_norm', 'dn_conv', 'dn_a_log', 'dn_dt_bias', 'dn_norm', 'w_out', 'g_mlp', 'w_up', 'w_down', 'g_ple', 'w_ple_gate', 'w_ple_proj', 'g_final']
TWIN_DIFF_INPUT = 'x'
TWIN_INPUTS = ['x', 'p', 'g_mix', 'w_in', 'gla_w2', 'gla_b', 'gla_norm', 'dn_conv', 'dn_a_log', 'dn_dt_bias', 'dn_norm', 'w_out', 'g_mlp', 'w_up', 'w_down', 'g_ple', 'w_ple_gate', 'w_ple_proj', 'g_final', 'loss_target', 'm_g_mix', 'm_w_in', 'm_gla_w2', 'm_gla_b', 'm_gla_norm', 'm_dn_conv', 'm_dn_a_log', 'm_dn_dt_bias', 'm_dn_norm', 'm_w_out', 'm_g_mlp', 'm_w_up', 'm_w_down', 'm_g_ple', 'm_w_ple_gate', 'm_w_ple_proj', 'm_g_final', 'v_g_mix', 'v_w_in', 'v_gla_w2', 'v_gla_b', 'v_gla_norm', 'v_dn_conv', 'v_dn_a_log', 'v_dn_dt_bias', 'v_dn_norm', 'v_w_out', 'v_g_mlp', 'v_w_up', 'v_w_down', 'v_g_ple', 'v_w_ple_gate', 'v_w_ple_proj', 'v_g_final']
TWIN_OUTPUTS = ['loss', 'grad_x', 'grad_g_mix', 'grad_w_in', 'grad_gla_w2', 'grad_gla_b', 'grad_gla_norm', 'grad_dn_conv', 'grad_dn_a_log', 'grad_dn_dt_bias', 'grad_dn_norm', 'grad_w_out', 'grad_g_mlp', 'grad_w_up', 'grad_w_down', 'grad_g_ple', 'grad_w_ple_gate', 'grad_w_ple_proj', 'grad_g_final', 'delta_g_mix', 'delta_w_in', 'delta_gla_w2', 'delta_gla_b', 'delta_gla_norm', 'delta_dn_conv', 'delta_dn_a_log', 'delta_dn_dt_bias', 'delta_dn_norm', 'delta_w_out', 'delta_g_mlp', 'delta_w_up', 'delta_w_down', 'delta_g_ple', 'delta_w_ple_gate', 'delta_w_ple_proj', 'delta_g_final', 'new_m_g_mix', 'new_m_w_in', 'new_m_gla_w2', 'new_m_gla_b', 'new_m_gla_norm', 'new_m_dn_conv', 'new_m_dn_a_log', 'new_m_dn_dt_bias', 'new_m_dn_norm', 'new_m_w_out', 'new_m_g_mlp', 'new_m_w_up', 'new_m_w_down', 'new_m_g_ple', 'new_m_w_ple_gate', 'new_m_w_ple_proj', 'new_m_g_final', 'new_v_g_mix', 'new_v_w_in', 'new_v_gla_w2', 'new_v_gla_b', 'new_v_gla_norm', 'new_v_dn_conv', 'new_v_dn_a_log', 'new_v_dn_dt_bias', 'new_v_dn_norm', 'new_v_w_out', 'new_v_g_mlp', 'new_v_w_up', 'new_v_w_down', 'new_v_g_ple', 'new_v_w_ple_gate', 'new_v_w_ple_proj', 'new_v_g_final']
TWIN_LEAF_KINDS = {'loss': 'loss', 'grad_x': 'grad_x', 'grad_g_mix': 'grad_w', 'grad_w_in': 'grad_w', 'grad_gla_w2': 'grad_w', 'grad_gla_b': 'grad_w', 'grad_gla_norm': 'grad_w', 'grad_dn_conv': 'grad_w', 'grad_dn_a_log': 'grad_w', 'grad_dn_dt_bias': 'grad_w', 'grad_dn_norm': 'grad_w', 'grad_w_out': 'grad_w', 'grad_g_mlp': 'grad_w', 'grad_w_up': 'grad_w', 'grad_w_down': 'grad_w', 'grad_g_ple': 'grad_w', 'grad_w_ple_gate': 'grad_w', 'grad_w_ple_proj': 'grad_w', 'grad_g_final': 'grad_w', 'delta_g_mix': 'delta_w', 'delta_w_in': 'delta_w', 'delta_gla_w2': 'delta_w', 'delta_gla_b': 'delta_w', 'delta_gla_norm': 'delta_w', 'delta_dn_conv': 'delta_w', 'delta_dn_a_log': 'delta_w', 'delta_dn_dt_bias': 'delta_w', 'delta_dn_norm': 'delta_w', 'delta_w_out': 'delta_w', 'delta_g_mlp': 'delta_w', 'delta_w_up': 'delta_w', 'delta_w_down': 'delta_w', 'delta_g_ple': 'delta_w', 'delta_w_ple_gate': 'delta_w', 'delta_w_ple_proj': 'delta_w', 'delta_g_final': 'delta_w', 'new_m_g_mix': 'new_m', 'new_m_w_in': 'new_m', 'new_m_gla_w2': 'new_m', 'new_m_gla_b': 'new_m', 'new_m_gla_norm': 'new_m', 'new_m_dn_conv': 'new_m', 'new_m_dn_a_log': 'new_m', 'new_m_dn_dt_bias': 'new_m', 'new_m_dn_norm': 'new_m', 'new_m_w_out': 'new_m', 'new_m_g_mlp': 'new_m', 'new_m_w_up': 'new_m', 'new_m_w_down': 'new_m', 'new_m_g_ple': 'new_m', 'new_m_w_ple_gate': 'new_m', 'new_m_w_ple_proj': 'new_m', 'new_m_g_final': 'new_m', 'new_v_g_mix': 'new_v', 'new_v_w_in': 'new_v', 'new_v_gla_w2': 'new_v', 'new_v_gla_b': 'new_v', 'new_v_gla_norm': 'new_v', 'new_v_dn_conv': 'new_v', 'new_v_dn_a_log': 'new_v', 'new_v_dn_dt_bias': 'new_v', 'new_v_dn_norm': 'new_v', 'new_v_w_out': 'new_v', 'new_v_g_mlp': 'new_v', 'new_v_w_up': 'new_v', 'new_v_w_down': 'new_v', 'new_v_g_ple': 'new_v', 'new_v_w_ple_gate': 'new_v', 'new_v_w_ple_proj': 'new_v', 'new_v_g_final': 'new_v'}


def _forward(args):
    return _fwd_reference(*[args[k] for k in FWD_PARAMS])


def _output_shape():
    out = _jax.eval_shape(lambda: _forward(_fwd_setup_inputs(0)))
    return out.shape, out.dtype

N_MICROBATCH = 1
ADAM_LR = 0.001
ADAM_B1 = 0.9
ADAM_B2 = 0.999
ADAM_EPS = 1e-08
ADAM_WD = 0.01
ADAM_STEP = 10
PER_EXAMPLE_BATCH_AXIS = {'x': 0, 'p': 1, 'loss_target': 0}
SHARED_INPUTS = []
_WEIGHT_DTYPES = {'g_mix': _jnp.float32, 'w_in': _jnp.float32, 'gla_w2': _jnp.float32, 'gla_b': _jnp.float32, 'gla_norm': _jnp.float32, 'dn_conv': _jnp.float32, 'dn_a_log': _jnp.float32, 'dn_dt_bias': _jnp.float32, 'dn_norm': _jnp.float32, 'w_out': _jnp.float32, 'g_mlp': _jnp.float32, 'w_up': _jnp.float32, 'w_down': _jnp.float32, 'g_ple': _jnp.float32, 'w_ple_gate': _jnp.float32, 'w_ple_proj': _jnp.float32, 'g_final': _jnp.float32}
MOMENT_SCALE = {'g_mix': 7.663167e-02, 'w_in': 2.437895e-02, 'gla_w2': 4.540180e-03, 'gla_b': 2.047048e-02, 'gla_norm': 5.419908e-02, 'dn_conv': 2.108155e-02, 'dn_a_log': 1.751817e-01, 'dn_dt_bias': 1.619012e-01, 'dn_norm': 1.284447e-01, 'w_out': 3.827484e-02, 'g_mlp': 7.625290e-02, 'w_up': 3.659236e-02, 'w_down': 7.054048e-02, 'g_ple': 1.152271e-02, 'w_ple_gate': 1.134254e-02, 'w_ple_proj': 2.780941e-02, 'g_final': 1.611837e+01}


def _to_microbatches(a, axis):
    t = _jnp.moveaxis(a, axis, 0)
    t = t.reshape((N_MICROBATCH, t.shape[0] // N_MICROBATCH) + t.shape[1:])
    return _jnp.moveaxis(t, 1, axis + 1)


def setup_inputs(seed: int = 0) -> dict:
    inp = _fwd_setup_inputs(seed)
    key = _jax.random.fold_in(_jax.random.key(seed), 7919)
    shape, _ = _output_shape()
    out = dict(inp)
    out["loss_target"] = _jax.random.normal(_jax.random.fold_in(key, 0), shape, _jnp.float32)
    for i, name in enumerate(TWIN_WEIGHTS):
        w = inp[name].astype(_jnp.float32)
        if MOMENT_SCALE is None:
            s = _jnp.sqrt(_jnp.mean(_jnp.square(w)) + 1e-30)
        else:
            s = MOMENT_SCALE[name]
        km, kv = _jax.random.split(_jax.random.fold_in(key, i + 1))
        out[name] = w
        out["m_" + name] = s * _jax.random.normal(km, w.shape, _jnp.float32)
        out["v_" + name] = (s * s) * _jax.random.uniform(kv, w.shape, _jnp.float32, 0.5, 1.5)
    if N_MICROBATCH > 1:
        for name, axis in PER_EXAMPLE_BATCH_AXIS.items():
            out[name] = _to_microbatches(out[name], axis)
    return {'x': out['x'], 'p': out['p'], 'g_mix': out['g_mix'], 'w_in': out['w_in'], 'gla_w2': out['gla_w2'], 'gla_b': out['gla_b'], 'gla_norm': out['gla_norm'], 'dn_conv': out['dn_conv'], 'dn_a_log': out['dn_a_log'], 'dn_dt_bias': out['dn_dt_bias'], 'dn_norm': out['dn_norm'], 'w_out': out['w_out'], 'g_mlp': out['g_mlp'], 'w_up': out['w_up'], 'w_down': out['w_down'], 'g_ple': out['g_ple'], 'w_ple_gate': out['w_ple_gate'], 'w_ple_proj': out['w_ple_proj'], 'g_final': out['g_final'], 'loss_target': out['loss_target'], 'm_g_mix': out['m_g_mix'], 'm_w_in': out['m_w_in'], 'm_gla_w2': out['m_gla_w2'], 'm_gla_b': out['m_gla_b'], 'm_gla_norm': out['m_gla_norm'], 'm_dn_conv': out['m_dn_conv'], 'm_dn_a_log': out['m_dn_a_log'], 'm_dn_dt_bias': out['m_dn_dt_bias'], 'm_dn_norm': out['m_dn_norm'], 'm_w_out': out['m_w_out'], 'm_g_mlp': out['m_g_mlp'], 'm_w_up': out['m_w_up'], 'm_w_down': out['m_w_down'], 'm_g_ple': out['m_g_ple'], 'm_w_ple_gate': out['m_w_ple_gate'], 'm_w_ple_proj': out['m_w_ple_proj'], 'm_g_final': out['m_g_final'], 'v_g_mix': out['v_g_mix'], 'v_w_in': out['v_w_in'], 'v_gla_w2': out['v_gla_w2'], 'v_gla_b': out['v_gla_b'], 'v_gla_norm': out['v_gla_norm'], 'v_dn_conv': out['v_dn_conv'], 'v_dn_a_log': out['v_dn_a_log'], 'v_dn_dt_bias': out['v_dn_dt_bias'], 'v_dn_norm': out['v_dn_norm'], 'v_w_out': out['v_w_out'], 'v_g_mlp': out['v_g_mlp'], 'v_w_up': out['v_w_up'], 'v_w_down': out['v_w_down'], 'v_g_ple': out['v_g_ple'], 'v_w_ple_gate': out['v_w_ple_gate'], 'v_w_ple_proj': out['v_w_ple_proj'], 'v_g_final': out['v_g_final']}


def _loss(weights, diff, rest, loss_target):
    with _jax.named_scope("forward"):
        args = {**rest, TWIN_DIFF_INPUT: diff, **{k: w.astype(_WEIGHT_DTYPES[k]) for k, w in weights.items()}}
        y = _forward(args)
    with _jax.named_scope("loss_head"):
        err = _jnp.square(y.astype(_jnp.float32) - loss_target)
        return 0.5 * _jnp.sum(_jnp.mean(err, axis=-1)) if err.ndim else 0.5 * err


def _adamw(w, g, m, v):
    m = ADAM_B1 * m + (1.0 - ADAM_B1) * g
    v = ADAM_B2 * v + (1.0 - ADAM_B2) * _jnp.square(g)
    m_hat = m / (1.0 - ADAM_B1 ** ADAM_STEP)
    v_hat = v / (1.0 - ADAM_B2 ** ADAM_STEP)
    delta = -ADAM_LR * (m_hat / (_jnp.sqrt(v_hat) + ADAM_EPS) + ADAM_WD * w)
    return delta, m, v


def reference(x, p, g_mix, w_in, gla_w2, gla_b, gla_norm, dn_conv, dn_a_log, dn_dt_bias, dn_norm, w_out, g_mlp, w_up, w_down, g_ple, w_ple_gate, w_ple_proj, g_final, loss_target, m_g_mix, m_w_in, m_gla_w2, m_gla_b, m_gla_norm, m_dn_conv, m_dn_a_log, m_dn_dt_bias, m_dn_norm, m_w_out, m_g_mlp, m_w_up, m_w_down, m_g_ple, m_w_ple_gate, m_w_ple_proj, m_g_final, v_g_mix, v_w_in, v_gla_w2, v_gla_b, v_gla_norm, v_dn_conv, v_dn_a_log, v_dn_dt_bias, v_dn_norm, v_w_out, v_g_mlp, v_w_up, v_w_down, v_g_ple, v_w_ple_gate, v_w_ple_proj, v_g_final):
    given = dict(x=x, p=p, g_mix=g_mix, w_in=w_in, gla_w2=gla_w2, gla_b=gla_b, gla_norm=gla_norm, dn_conv=dn_conv, dn_a_log=dn_a_log, dn_dt_bias=dn_dt_bias, dn_norm=dn_norm, w_out=w_out, g_mlp=g_mlp, w_up=w_up, w_down=w_down, g_ple=g_ple, w_ple_gate=w_ple_gate, w_ple_proj=w_ple_proj, g_final=g_final, loss_target=loss_target, m_g_mix=m_g_mix, m_w_in=m_w_in, m_gla_w2=m_gla_w2, m_gla_b=m_gla_b, m_gla_norm=m_gla_norm, m_dn_conv=m_dn_conv, m_dn_a_log=m_dn_a_log, m_dn_dt_bias=m_dn_dt_bias, m_dn_norm=m_dn_norm, m_w_out=m_w_out, m_g_mlp=m_g_mlp, m_w_up=m_w_up, m_w_down=m_w_down, m_g_ple=m_g_ple, m_w_ple_gate=m_w_ple_gate, m_w_ple_proj=m_w_ple_proj, m_g_final=m_g_final, v_g_mix=v_g_mix, v_w_in=v_w_in, v_gla_w2=v_gla_w2, v_gla_b=v_gla_b, v_gla_norm=v_gla_norm, v_dn_conv=v_dn_conv, v_dn_a_log=v_dn_a_log, v_dn_dt_bias=v_dn_dt_bias, v_dn_norm=v_dn_norm, v_w_out=v_w_out, v_g_mlp=v_g_mlp, v_w_up=v_w_up, v_w_down=v_w_down, v_g_ple=v_g_ple, v_w_ple_gate=v_w_ple_gate, v_w_ple_proj=v_w_ple_proj, v_g_final=v_g_final)
    weights = {n: given[n] for n in TWIN_WEIGHTS}
    shared = {n: given[n] for n in SHARED_INPUTS}
    per_example = {n: given[n] for n in ['x', 'p']}
    grad_fn = _jax.value_and_grad(_loss, argnums=(0, 1))

    def one_microbatch(ex, loss_target):
        ex = dict(ex)
        diff = ex.pop(TWIN_DIFF_INPUT)
        return grad_fn(weights, diff, {**shared, **ex}, loss_target)

    if N_MICROBATCH == 1:
        loss, (grad_w, grad_x) = one_microbatch(per_example, given["loss_target"])
    else:
        def body(carry, xs):
            loss_sum, grad_sum = carry
            l_k, (gw_k, gx_k) = one_microbatch(xs[0], xs[1])
            with _jax.named_scope("update"):
                return (loss_sum + l_k, _jax.tree.map(_jnp.add, grad_sum, gw_k)), gx_k

        init = (_jnp.zeros((), _jnp.float32), _jax.tree.map(_jnp.zeros_like, weights))
        (loss, grad_w), grad_x = _jax.lax.scan(body, init, (per_example, given["loss_target"]))
    with _jax.named_scope("update"):
        delta_w, new_m, new_v = {}, {}, {}
        for n in TWIN_WEIGHTS:
            delta_w[n], new_m[n], new_v[n] = _adamw(weights[n], grad_w[n], given["m_" + n], given["v_" + n])
    return (loss, grad_x, *[grad_w[n] for n in TWIN_WEIGHTS], *[delta_w[n] for n in TWIN_WEIGHTS],
            *[new_m[n] for n in TWIN_WEIGHTS], *[new_v[n] for n in TWIN_WEIGHTS])
```

```python
import functools

import jax
import jax.numpy as jnp
from jax import lax
from jax.experimental import pallas as pl
from jax.experimental.pallas import tpu as pltpu

F32 = jnp.float32
BF16 = jnp.bfloat16

CHUNK = 64
GLA_HEADS = 4
DN_HEADS = 16
LOWRANK = 16
GLA_TAU = 16.0
DN_CONV = 4
EPS = 1e-6
ZS = 128
ADAM_LR, ADAM_B1, ADAM_B2, ADAM_EPS, ADAM_WD, ADAM_STEP = 0.001, 0.9, 0.999, 1e-08, 0.01, 10

V7X_VMEM_BYTES = 64 * 1024 * 1024
VMEM_LIMIT_BYTES = V7X_VMEM_BYTES - 8 * 1024 * 1024
LANES = 128
FLAT_COLS = 512
MESH = pl.DeviceIdType.MESH

WEIGHTS = ['g_mix', 'w_in', 'gla_w2', 'gla_b', 'gla_norm', 'dn_conv', 'dn_a_log', 'dn_dt_bias', 'dn_norm', 'w_out',
           'g_mlp', 'w_up', 'w_down', 'g_ple', 'w_ple_gate', 'w_ple_proj', 'g_final']
BIG = [('w_in', 1), ('w_out', 0), ('w_up', 1), ('w_down', 0), ('w_ple_gate', 0), ('w_ple_proj', 1)]
SMALL = [n for n in WEIGHTS if n not in dict(BIG)]

_NN = ((1,), (0,))
_NT = ((1,), (1,))
_TN = ((0,), (0,))


def _params(sem=None):
    return pltpu.CompilerParams(dimension_semantics=sem, vmem_limit_bytes=VMEM_LIMIT_BYTES)


def _dot(a, b, dims, precision=None):
    return lax.dot_general(a, b, (dims, ((), ())), precision=precision, preferred_element_type=F32)


def _make_mm(cast, precision):
    def raw(a, b, dims):
        return _dot(cast(a), cast(b), dims, precision)

    @jax.custom_vjp
    def nn(a, b):
        return raw(a, b, _NN)
    nn.defvjp(lambda a, b: (raw(a, b, _NN), (a, b)), lambda r, g: (raw(g, r[1], _NT), raw(r[0], g, _TN)))

    @jax.custom_vjp
    def nt(a, b):
        return raw(a, b, _NT)
    nt.defvjp(lambda a, b: (raw(a, b, _NT), (a, b)), lambda r, g: (raw(g, r[1], _NN), raw(g, r[0], _TN)))

    @jax.custom_vjp
    def tn(a, b):
        return raw(a, b, _TN)
    tn.defvjp(lambda a, b: (raw(a, b, _TN), (a, b)), lambda r, g: (raw(r[1], g, _NT), raw(r[0], g, _NN)))
    return nn, nt, tn


_bnn, _bnt, _btn = _make_mm(lambda t: t.astype(BF16), None)
_hnn, _hnt, _htn = _make_mm(lambda t: t, lax.Precision.HIGHEST)


def _iota2(n, axis):
    return lax.broadcasted_iota(jnp.int32, (n, n), axis)


def _lower(n, strict=False):
    return (_iota2(n, 0) > _iota2(n, 1)) if strict else (_iota2(n, 0) >= _iota2(n, 1))


@jax.custom_vjp
def _cumsum_rows(x):
    n = x.shape[0]
    return _dot(_lower(n).astype(F32), x, _NN, lax.Precision.HIGHEST)


def _cumsum_rows_bwd(_, g):
    n = g.shape[0]
    upper = (_iota2(n, 0) <= _iota2(n, 1)).astype(F32)
    return (_dot(upper, g, _NN, lax.Precision.HIGHEST),)


_cumsum_rows.defvjp(lambda x: (_cumsum_rows(x), None), _cumsum_rows_bwd)


def _tri_inv_impl(a):
    n = a.shape[0]
    eye = (_iota2(n, 0) == _iota2(n, 1)).astype(F32)
    p = eye - a
    ak = a
    k = 2
    while k < n:
        ak = _dot(ak, ak, _NN, lax.Precision.HIGHEST)
        p = p + _dot(p, ak, _NN, lax.Precision.HIGHEST)
        k *= 2
    return p


@jax.custom_vjp
def _tri_inv(a):
    return _tri_inv_impl(a)


def _tri_inv_fwd(a):
    t = _tri_inv_impl(a)
    return t, t


def _tri_inv_bwd(t, g):
    tg = _dot(t, g, _TN, lax.Precision.HIGHEST)
    return (-_dot(tg, t, _NT, lax.Precision.HIGHEST),)


_tri_inv.defvjp(_tri_inv_fwd, _tri_inv_bwd)


def _shift_rows(x, s, down):
    n = x.shape[0]
    r = lax.broadcasted_iota(jnp.int32, x.shape, 0)
    if down:
        return jnp.where(r >= s, pltpu.roll(x, s, 0), 0.0)
    return jnp.where(r < n - s, pltpu.roll(x, n - s, 0), 0.0)


def _make_shift(s):
    @jax.custom_vjp
    def f(x):
        return _shift_rows(x, s, True)
    f.defvjp(lambda x: (_shift_rows(x, s, True), None), lambda _, g: (_shift_rows(g, s, False),))
    return f


def _sigmoid(x):
    return jax.nn.sigmoid(x)


def _silu(x):
    return x * jax.nn.sigmoid(x)


def _softplus(x):
    return jnp.maximum(x, 0.0) + jnp.log1p(jnp.exp(-jnp.abs(x)))


def _log_sigmoid(x):
    return -_softplus(-x)


def _rms(x, g):
    return x * lax.rsqrt(jnp.mean(x * x, axis=-1, keepdims=True) + EPS) * g


def _gla_chunk(q, k, v, zs, w2, gb, st, *, scale):
    c = q.shape[0]
    logf = _log_sigmoid(_bnn(zs, w2) + gb) * (1.0 / GLA_TAU)
    bcum = _cumsum_rows(logf)
    b_last = jnp.sum(logf, axis=0, keepdims=True)
    q_in = (q * scale) * jnp.exp(bcum)
    k_in = k * jnp.exp(-bcum)
    a = jnp.where(_lower(c), _bnt(q_in, k_in), 0.0)
    o = _bnn(a, v) + _bnt(q_in, st)
    k_dec = k * jnp.exp(b_last - bcum)
    st_new = st * jnp.exp(b_last) + _btn(v, k_dec)
    return o, st_new


def _dn_chunk(q, k, v, aw, bw, alog, dtb, s):
    c = q.shape[0]
    incl, strict = _lower(c), _lower(c, True)
    eye = _iota2(c, 0) == _iota2(c, 1)
    g_w = -jnp.exp(alog) * _softplus(aw + dtb)
    beta_w = _sigmoid(bw)
    gcum_w = _cumsum_rows(g_w)
    lane0 = lax.broadcasted_iota(jnp.int32, gcum_w.shape, 1) == 0
    gcol = jnp.sum(jnp.where(lane0, gcum_w, 0.0), axis=1, keepdims=True)
    d1 = jnp.broadcast_to(gcol, (c, c))
    d2 = _hnn(jnp.ones((c, c), F32), jnp.where(eye, d1, 0.0))
    diff = jnp.where(incl, d1 - d2, 0.0)
    decay = jnp.where(incl, jnp.exp(diff), 0.0)
    k_beta = k * beta_w
    a = jnp.where(strict, _bnt(k_beta, k) * decay, 0.0)
    t = _tri_inv(a)
    egc = jnp.exp(gcum_w)
    u = _bnn(t, v * beta_w)
    w = _bnn(t, k_beta * egc)
    attn = jnp.where(incl, _bnt(q, k) * decay, 0.0)
    q_dec = q * egc
    g_last = jnp.sum(g_w, axis=0, keepdims=True)
    k_dec = k * jnp.exp(g_last - gcum_w)
    v_new = u - _bnn(w, s)
    o = _bnn(q_dec, s) + _bnn(attn, v_new)
    s_new = s * jnp.exp(g_last) + _btn(k_dec, v_new)
    return o, s_new


def _conv_act(x, wrows, *, l2, scale):
    taps = len(wrows)
    y = None
    for j in range(taps):
        s = taps - 1 - j
        xs = x if s == 0 else _make_shift(s)(x)
        y = wrows[j] * xs if y is None else y + wrows[j] * xs
    y = _silu(y)
    if l2:
        y = y * lax.rsqrt(jnp.sum(y * y, axis=-1, keepdims=True) + EPS) * scale
    return y


def _merge_math(og, gg, od, dz, ga, gb, gn, dn):
    nsub = len(og)
    dv = nsub * og[0].shape[1]
    ssq = jnp.sum(og[0] * og[0], axis=-1, keepdims=True)
    for s in range(1, nsub):
        ssq = ssq + jnp.sum(og[s] * og[s], axis=-1, keepdims=True)
    r = lax.rsqrt(ssq * (1.0 / dv) + EPS)
    outs = []
    for s in range(nsub):
        a = og[s] * r * gn[s] * _silu(gg[s])
        b = _rms(od[s], dn) * _silu(dz[s])
        outs.append(_sigmoid(ga[s]) * a + _sigmoid(gb[s]) * b)
    return outs


def _pick(n, target, mult):
    best = None
    for d in range(mult, min(n, target) + 1, mult):
        if n % d == 0:
            best = d
    return best if best is not None else n


def _matmul(a, b, form, out_dtypes, name, epilogue=None, extras=(), bm=1024, bn=1024, bk=512):
    if form == 'nn':
        (M, K), (K2, N) = a.shape, b.shape
    elif form == 'nt':
        (M, K), (N, K2) = a.shape, b.shape
    else:
        (K, M), (K2, N) = a.shape, b.shape
    assert K == K2, (a.shape, b.shape, form)
    bm, bn, bk = _pick(M, bm, 8), _pick(N, bn, LANES), _pick(K, bk, LANES)
    nk = K // bk
    dims = {'nn': _NN, 'nt': _NT, 'tn': _TN}[form]
    a_spec = pl.BlockSpec((bk, bm), lambda i, j, k: (k, i)) if form == 'tn' else pl.BlockSpec((bm, bk), lambda i, j, k: (i, k))
    b_spec = pl.BlockSpec((bn, bk), lambda i, j, k: (j, k)) if form == 'nt' else pl.BlockSpec((bk, bn), lambda i, j, k: (k, j))
    o_spec = pl.BlockSpec((bm, bn), lambda i, j, k: (i, j))
    ne, no = len(extras), len(out_dtypes)

    def body(a_ref, b_ref, *rest):
        extra_refs, out_refs, acc = rest[:ne], rest[ne:ne + no], rest[ne + no]
        k = pl.program_id(2)

        @pl.when(k == 0)
        def _():
            acc[...] = jnp.zeros_like(acc)

        acc[...] += _dot(a_ref[...].astype(BF16), b_ref[...].astype(BF16), dims)

        @pl.when(k == nk - 1)
        def _():
            r = acc[...]
            outs = (r,) if epilogue is None else epilogue(r, *[e[...] for e in extra_refs])
            for ref, o in zip(out_refs, outs):
                ref[...] = o.astype(ref.dtype)

    outs = pl.pallas_call(
        body, name=name, grid=(M // bm, N // bn, nk),
        in_specs=[a_spec, b_spec] + [o_spec] * ne,
        out_specs=[o_spec] * no,
        out_shape=[jax.ShapeDtypeStruct((M, N), d) for d in out_dtypes],
        scratch_shapes=[pltpu.VMEM((bm, bn), F32)],
        compiler_params=_params(("parallel", "parallel", "arbitrary")),
    )(a, b, *extras)
    return outs


def _rowwise(fn, rows, consts, row_outs, acc_outs, name, bt=256):
    T = rows[0].shape[0]
    bt = _pick(T, bt, 8)
    nr, nc, no, na = len(rows), len(consts), len(row_outs), len(acc_outs)

    def body(*refs):
        r_in, c_in = refs[:nr], refs[nr:nr + nc]
        r_out, a_out = refs[nr + nc:nr + nc + no], refs[nr + nc + no:]
        ro, ao = fn([r[...] for r in r_in], [c[...] for c in c_in])
        for ref, o in zip(r_out, ro):
            ref[...] = o.astype(ref.dtype)
        if na:
            @pl.when(pl.program_id(0) == 0)
            def _():
                for ref in a_out:
                    ref[...] = jnp.zeros_like(ref)
            for ref, o in zip(a_out, ao):
                ref[...] += o

    whole = lambda shp: pl.BlockSpec(shp, lambda i: (0,) * len(shp))
    outs = pl.pallas_call(
        body, name=name, grid=(T // bt,),
        in_specs=[pl.BlockSpec((bt, r.shape[1]), lambda i: (i, 0)) for r in rows] + [whole(c.shape) for c in consts],
        out_specs=[pl.BlockSpec((bt, w), lambda i: (i, 0)) for w, _ in row_outs] + [whole(s) for s in acc_outs],
        out_shape=[jax.ShapeDtypeStruct((T, w), d) for w, d in row_outs] + [jax.ShapeDtypeStruct(s, F32) for s in acc_outs],
        compiler_params=_params(("arbitrary",)),
    )(*rows, *consts)
    return outs


def _rmsnorm_fwd(x, g, name):
    return _rowwise(lambda r, c: ([_rms(r[0], c[0])], []), [x], [g], [(x.shape[1], BF16)], [], name)[0]


def _rmsnorm_bwd_add(x, g, dh, dres, name):
    D = x.shape[1]

    def fn(r, c):
        _, vjp = jax.vjp(_rms, r[0], c[0])
        dx, dg = vjp(r[1])
        dx = dx + r[2]
        return [dx, dx], [dg]
    return _rowwise(fn, [x, dh, dres], [g], [(D, F32), (D, BF16)], [(1, D)], name)


def _loss_fwd_bwd(x3, g, target, name):
    D = x3.shape[1]

    def fn(r, c):
        def row_loss(x, gain):
            err = _rms(x, gain) - r[1]
            return 0.5 * jnp.mean(err * err, axis=-1, keepdims=True)
        lrow, vjp = jax.vjp(row_loss, r[0], c[0])
        dx, dg = vjp(jnp.ones_like(lrow))
        tile = jnp.broadcast_to(jnp.sum(lrow, axis=0, keepdims=True), (8, LANES))
        return [dx], [tile, dg]
    return _rowwise(fn, [x3, target], [g], [(D, F32)], [(8, LANES), (1, D)], name)


def _ple_bwd(dx3, gp, pp, name):
    D = dx3.shape[1]

    def fn(r, c):
        s = _sigmoid(r[1])
        return [r[0] * r[2] * s * (1.0 - s), r[0] * s], []
    return _rowwise(fn, [dx3, gp, pp], [], [(D, BF16), (D, BF16)], [], name)


def _adamw(w, g, m, v, name):
    R, C = w.shape
    lanes = -(-C // LANES) * LANES
    bt = R if R % 8 else _pick(R, max(8, (1 << 18) // lanes // 8 * 8), 8)

    def body(w_ref, g_ref, m_ref, v_ref, d_ref, nm_ref, nv_ref):
        gg = g_ref[...]
        nm = ADAM_B1 * m_ref[...] + (1.0 - ADAM_B1) * gg
        nv = ADAM_B2 * v_ref[...] + (1.0 - ADAM_B2) * (gg * gg)
        m_hat = nm / (1.0 - ADAM_B1 ** ADAM_STEP)
        v_hat = nv / (1.0 - ADAM_B2 ** ADAM_STEP)
        d_ref[...] = -ADAM_LR * (m_hat / (jnp.sqrt(v_hat) + ADAM_EPS) + ADAM_WD * w_ref[...])
        nm_ref[...] = nm
        nv_ref[...] = nv

    spec = pl.BlockSpec((bt, C), lambda i: (i, 0))
    return pl.pallas_call(
        body, name=name, grid=(R // bt,), in_specs=[spec] * 4, out_specs=[spec] * 3,
        out_shape=[jax.ShapeDtypeStruct((R, C), F32)] * 3, compiler_params=_params(("parallel",)),
    )(w, g, m, v)


def _gla_fwd(z_all, w2pad, gb, Bl, S, D):
    NC, dk, dv = S // CHUNK, D // (2 * GLA_HEADS), D // GLA_HEADS
    zs_blk = (9 * D) // ZS
    scale = dk ** -0.5

    def body(q, k, v, zs, w2, b, o_ref, stall_ref, st):
        @pl.when(pl.program_id(2) == 0)
        def _():
            st[...] = jnp.zeros_like(st)
        stall_ref[...] = st[...]
        o, st_new = _gla_chunk(q[...], k[...], v[...], zs[...], w2[...], b[...], st[...], scale=scale)
        o_ref[...] = o
        st[...] = st_new

    row = lambda h, b, n: b * NC + n
    return pl.pallas_call(
        body, name="gla_fwd", grid=(GLA_HEADS, Bl, NC),
        in_specs=[pl.BlockSpec((CHUNK, dk), lambda h, b, n: (row(h, b, n), h)),
                  pl.BlockSpec((CHUNK, dk), lambda h, b, n: (row(h, b, n), GLA_HEADS + h)),
                  pl.BlockSpec((CHUNK, dv), lambda h, b, n: (row(h, b, n), GLA_HEADS + h)),
                  pl.BlockSpec((CHUNK, ZS), lambda h, b, n: (row(h, b, n), zs_blk)),
                  pl.BlockSpec((ZS, dk), lambda h, b, n: (0, h)),
                  pl.BlockSpec((1, dk), lambda h, b, n: (0, h))],
        out_specs=[pl.BlockSpec((CHUNK, dv), lambda h, b, n: (row(h, b, n), h)),
                   pl.BlockSpec((None, None, None, dv, dk), lambda h, b, n: (h, b, n, 0, 0))],
        out_shape=[jax.ShapeDtypeStruct((Bl * S, D), F32), jax.ShapeDtypeStruct((GLA_HEADS, Bl, NC, dv, dk), F32)],
        scratch_shapes=[pltpu.VMEM((dv, dk), F32)],
        compiler_params=_params(("arbitrary", "arbitrary", "arbitrary")),
    )(z_all, z_all, z_all, z_all, w2pad, gb)


def _gla_bwd(z_all, w2pad, gb, st_all, do, Bl, S, D):
    NC, dk, dv = S // CHUNK, D // (2 * GLA_HEADS), D // GLA_HEADS
    zs_blk = (9 * D) // ZS
    T = Bl * S
    fn = functools.partial(_gla_chunk, scale=dk ** -0.5)

    def body(q, k, v, zs, w2, b, st0, do_ref, dq_ref, dk_ref, dv_ref, dzs_ref, dw2_ref, db_ref, dst):
        @pl.when(pl.program_id(2) == 0)
        def _():
            dst[...] = jnp.zeros_like(dst)

        @pl.when((pl.program_id(1) == 0) & (pl.program_id(2) == 0))
        def _():
            dw2_ref[...] = jnp.zeros_like(dw2_ref)
            db_ref[...] = jnp.zeros_like(db_ref)

        _, vjp = jax.vjp(fn, q[...], k[...], v[...], zs[...], w2[...], b[...], st0[...])
        dq, dkk, dvv, dzs, dw2, db, dst0 = vjp((do_ref[...], dst[...]))
        dq_ref[...] = dq.astype(dq_ref.dtype)
        dk_ref[...] = dkk.astype(dk_ref.dtype)
        dv_ref[...] = dvv.astype(dv_ref.dtype)
        dzs_ref[...] = dzs
        dw2_ref[...] += dw2
        db_ref[...] += db
        dst[...] = dst0

    row = lambda h, b, n: b * NC + (NC - 1 - n)
    return pl.pallas_call(
        body, name="gla_bwd", grid=(GLA_HEADS, Bl, NC),
        in_specs=[pl.BlockSpec((CHUNK, dk), lambda h, b, n: (row(h, b, n), h)),
                  pl.BlockSpec((CHUNK, dk), lambda h, b, n: (row(h, b, n), GLA_HEADS + h)),
                  pl.BlockSpec((CHUNK, dv), lambda h, b, n: (row(h, b, n), GLA_HEADS + h)),
                  pl.BlockSpec((CHUNK, ZS), lambda h, b, n: (row(h, b, n), zs_blk)),
                  pl.BlockSpec((ZS, dk), lambda h, b, n: (0, h)),
                  pl.BlockSpec((1, dk), lambda h, b, n: (0, h)),
                  pl.BlockSpec((None, None, None, dv, dk), lambda h, b, n: (h, b, NC - 1 - n, 0, 0)),
                  pl.BlockSpec((CHUNK, dv), lambda h, b, n: (row(h, b, n), h))],
        out_specs=[pl.BlockSpec((CHUNK, dk), lambda h, b, n: (row(h, b, n), h)),
                   pl.BlockSpec((CHUNK, dk), lambda h, b, n: (row(h, b, n), h)),
                   pl.BlockSpec((CHUNK, dv), lambda h, b, n: (row(h, b, n), h)),
                   pl.BlockSpec((None, CHUNK, ZS), lambda h, b, n: (h, row(h, b, n), 0)),
                   pl.BlockSpec((ZS, dk), lambda h, b, n: (0, h)),
                   pl.BlockSpec((1, dk), lambda h, b, n: (0, h))],
        out_shape=[jax.ShapeDtypeStruct((T, D // 2), BF16), jax.ShapeDtypeStruct((T, D // 2), BF16),
                   jax.ShapeDtypeStruct((T, D), BF16), jax.ShapeDtypeStruct((GLA_HEADS, T, ZS), F32),
                   jax.ShapeDtypeStruct((ZS, D // 2), F32), jax.ShapeDtypeStruct((1, D // 2), F32)],
        scratch_shapes=[pltpu.VMEM((dv, dk), F32)],
        compiler_params=_params(("arbitrary", "arbitrary", "arbitrary")),
    )(z_all, z_all, z_all, z_all, w2pad, gb, st_all, do)


def _conv_fwd(z_all, conv_w, grp, Bl, S, D):
    d = D // DN_HEADS
    l2, scale = grp < 2, (d ** -0.5 if grp == 0 else 1.0)
    x_blk0 = (3 * D + grp * D) // d

    def body(x_ref, w_ref, o_ref):
        wrows = [w_ref[j:j + 1, :] for j in range(DN_CONV)]
        o_ref[...] = _conv_act(x_ref[...], wrows, l2=l2, scale=scale)

    return pl.pallas_call(
        body, name=f"conv_fwd{grp}", grid=(Bl, DN_HEADS),
        in_specs=[pl.BlockSpec((S, d), lambda b, j: (b, x_blk0 + j)),
                  pl.BlockSpec((DN_CONV, d), lambda b, j: (0, grp * DN_HEADS + j))],
        out_specs=pl.BlockSpec((S, d), lambda b, j: (b, j)),
        out_shape=jax.ShapeDtypeStruct((Bl * S, D), F32),
        compiler_params=_params(("parallel", "parallel")),
    )(z_all, conv_w)


def _conv_bwd(z_all, conv_w, dact, grp, Bl, S, D):
    d = D // DN_HEADS
    l2, scale = grp < 2, (d ** -0.5 if grp == 0 else 1.0)
    x_blk0 = (3 * D + grp * D) // d

    def body(x_ref, w_ref, g_ref, dx_ref, dw_ref):
        @pl.when(pl.program_id(1) == 0)
        def _():
            dw_ref[...] = jnp.zeros_like(dw_ref)
        wrows = [w_ref[j:j + 1, :] for j in range(DN_CONV)]
        _, vjp = jax.vjp(lambda x, wr: _conv_act(x, wr, l2=l2, scale=scale), x_ref[...], wrows)
        dx, dwr = vjp(g_ref[...])
        dx_ref[...] = dx.astype(dx_ref.dtype)
        for j in range(DN_CONV):
            dw_ref[j:j + 1, :] += dwr[j]

    return pl.pallas_call(
        body, name=f"conv_bwd{grp}", grid=(DN_HEADS, Bl),
        in_specs=[pl.BlockSpec((S, d), lambda j, b: (b, x_blk0 + j)),
                  pl.BlockSpec((DN_CONV, d), lambda j, b: (0, grp * DN_HEADS + j)),
                  pl.BlockSpec((S, d), lambda j, b: (b, j))],
        out_specs=[pl.BlockSpec((S, d), lambda j, b: (b, j)), pl.BlockSpec((DN_CONV, d), lambda j, b: (0, j))],
        out_shape=[jax.ShapeDtypeStruct((Bl * S, D), BF16), jax.ShapeDtypeStruct((DN_CONV, D), F32)],
        compiler_params=_params(("arbitrary", "arbitrary")),
    )(z_all, conv_w, dact)


def _dn_fwd(qa, ka, va, aw, bw, alog, dtb, Bl, S, D):
    NC, d = S // CHUNK, D // DN_HEADS

    def body(q, k, v, a, b, al, dt, o_ref, sall_ref, st):
        @pl.when(pl.program_id(2) == 0)
        def _():
            st[...] = jnp.zeros_like(st)
        sall_ref[...] = st[...]
        o, s_new = _dn_chunk(q[...], k[...], v[...], a[...], b[...], al[...], dt[...], st[...])
        o_ref[...] = o
        st[...] = s_new

    row = lambda h, b, n: b * NC + n
    tok = pl.BlockSpec((CHUNK, d), lambda h, b, n: (row(h, b, n), h))
    gate = pl.BlockSpec((None, CHUNK, d), lambda h, b, n: (h, row(h, b, n), 0))
    per_head = pl.BlockSpec((None, 1, d), lambda h, b, n: (h, 0, 0))
    return pl.pallas_call(
        body, name="dn_fwd", grid=(DN_HEADS, Bl, NC),
        in_specs=[tok, tok, tok, gate, gate, per_head, per_head],
        out_specs=[tok, pl.BlockSpec((None, None, None, d, d), lambda h, b, n: (h, b, n, 0, 0))],
        out_shape=[jax.ShapeDtypeStruct((Bl * S, D), F32), jax.ShapeDtypeStruct((DN_HEADS, Bl, NC, d, d), F32)],
        scratch_shapes=[pltpu.VMEM((d, d), F32)],
        compiler_params=_params(("arbitrary", "arbitrary", "arbitrary")),
    )(qa, ka, va, aw, bw, alog, dtb)


def _dn_bwd(qa, ka, va, aw, bw, alog, dtb, s_all, do, Bl, S, D):
    NC, d = S // CHUNK, D // DN_HEADS
    T = Bl * S

    def lanesum(t):
        return jnp.broadcast_to(jnp.sum(t, axis=1, keepdims=True), t.shape)

    def body(q, k, v, a, b, al, dt, s0, do_ref, dq_ref, dk_ref, dv_ref, da_ref, db_ref, dal_ref, ddt_ref, dst):
        @pl.when(pl.program_id(2) == 0)
        def _():
            dst[...] = jnp.zeros_like(dst)

        @pl.when((pl.program_id(1) == 0) & (pl.program_id(2) == 0))
        def _():
            dal_ref[...] = jnp.zeros_like(dal_ref)
            ddt_ref[...] = jnp.zeros_like(ddt_ref)

        _, vjp = jax.vjp(_dn_chunk, q[...], k[...], v[...], a[...], b[...], al[...], dt[...], s0[...])
        dq, dkk, dvv, da, db, dal, ddt, ds0 = vjp((do_ref[...], dst[...]))
        dq_ref[...] = dq
        dk_ref[...] = dkk
        dv_ref[...] = dvv
        da_ref[...] = lanesum(da)
        db_ref[...] = lanesum(db)
        dal_ref[...] += lanesum(dal)
        ddt_ref[...] += lanesum(ddt)
        dst[...] = ds0

    row = lambda h, b, n: b * NC + (NC - 1 - n)
    tok = pl.BlockSpec((CHUNK, d), lambda h, b, n: (row(h, b, n), h))
    gate = pl.BlockSpec((None, CHUNK, d), lambda h, b, n: (h, row(h, b, n), 0))
    per_head = pl.BlockSpec((None, 1, d), lambda h, b, n: (h, 0, 0))
    tok_shape = jax.ShapeDtypeStruct((T, D), F32)
    gate_shape = jax.ShapeDtypeStruct((DN_HEADS, T, d), F32)
    head_shape = jax.ShapeDtypeStruct((DN_HEADS, 1, d), F32)
    return pl.pallas_call(
        body, name="dn_bwd", grid=(DN_HEADS, Bl, NC),
        in_specs=[tok, tok, tok, gate, gate, per_head, per_head,
                  pl.BlockSpec((None, None, None, d, d), lambda h, b, n: (h, b, NC - 1 - n, 0, 0)), tok],
        out_specs=[tok, tok, tok, gate, gate, per_head, per_head],
        out_shape=[tok_shape, tok_shape, tok_shape, gate_shape, gate_shape, head_shape, head_shape],
        scratch_shapes=[pltpu.VMEM((d, d), F32)],
        compiler_params=_params(("arbitrary", "arbitrary", "arbitrary")),
    )(qa, ka, va, aw, bw, alog, dtb, s_all, do)


def _merge_specs(D, bt):
    dv, w = D // GLA_HEADS, D // DN_HEADS
    col = lambda off: pl.BlockSpec((bt, dv), lambda i, h: (i, off // dv + h))
    return dv, w, col


def _merge_load(refs, nsub, w):
    return [[r[:, s * w:(s + 1) * w] for s in range(nsub)] for r in refs]


def _merge_fwd(o_gla, o_dn, z_all, gla_norm, dn_norm, D, bt=256):
    T = o_gla.shape[0]
    bt = _pick(T, bt, 8)
    dv, w, col = _merge_specs(D, bt)
    nsub = dv // w

    def body(og, gg, od, dz, ga, gb, gn, dn, out):
        ogl, ggl, odl, dzl, gal, gbl = _merge_load([og, gg, od, dz, ga, gb], nsub, w)
        gnl = [gn[:, s * w:(s + 1) * w] for s in range(nsub)]
        outs = _merge_math(ogl, ggl, odl, dzl, gal, gbl, gnl, dn[...])
        for s in range(nsub):
            out[:, s * w:(s + 1) * w] = outs[s].astype(out.dtype)

    return pl.pallas_call(
        body, name="merge_fwd", grid=(T // bt, GLA_HEADS),
        in_specs=[col(0), col(2 * D), col(0), col(6 * D), col(7 * D), col(8 * D),
                  pl.BlockSpec((1, dv), lambda i, h: (0, 0)), pl.BlockSpec((1, w), lambda i, h: (0, 0))],
        out_specs=col(0),
        out_shape=jax.ShapeDtypeStruct((T, D), BF16),
        compiler_params=_params(("parallel", "parallel")),
    )(o_gla, z_all, o_dn, z_all, z_all, z_all, gla_norm, dn_norm)


def _merge_bwd(o_gla, o_dn, z_all, gla_norm, dn_norm, dmix, D, bt=256):
    T = o_gla.shape[0]
    bt = _pick(T, bt, 8)
    dv, w, col = _merge_specs(D, bt)
    nsub = dv // w

    def body(og, gg, od, dz, ga, gb, gn, dn, dm, dog, dgg, dod, ddz, dga, dgb, dgn, ddn):
        @pl.when((pl.program_id(0) == 0) & (pl.program_id(1) == 0))
        def _():
            dgn[...] = jnp.zeros_like(dgn)
            ddn[...] = jnp.zeros_like(ddn)

        ogl, ggl, odl, dzl, gal, gbl, dml = _merge_load([og, gg, od, dz, ga, gb, dm], nsub, w)
        gnl = [gn[:, s * w:(s + 1) * w] for s in range(nsub)]
        _, vjp = jax.vjp(_merge_math, ogl, ggl, odl, dzl, gal, gbl, gnl, dn[...])
        g_og, g_gg, g_od, g_dz, g_ga, g_gb, g_gn, g_dn = vjp(dml)
        for s in range(nsub):
            sl = slice(s * w, (s + 1) * w)
            dog[:, sl] = g_og[s]
            dgg[:, sl] = g_gg[s].astype(dgg.dtype)
            dod[:, sl] = g_od[s]
            ddz[:, sl] = g_dz[s].astype(ddz.dtype)
            dga[:, sl] = g_ga[s].astype(dga.dtype)
            dgb[:, sl] = g_gb[s].astype(dgb.dtype)
            dgn[:, sl] += g_gn[s]
        ddn[...] += g_dn

    f32s, bf16s = jax.ShapeDtypeStruct((T, D), F32), jax.ShapeDtypeStruct((T, D), BF16)
    return pl.pallas_call(
        body, name="merge_bwd", grid=(T // bt, GLA_HEADS),
        in_specs=[col(0), col(2 * D), col(0), col(6 * D), col(7 * D), col(8 * D),
                  pl.BlockSpec((1, dv), lambda i, h: (0, 0)), pl.BlockSpec((1, w), lambda i, h: (0, 0)), col(0)],
        out_specs=[col(0)] * 6 + [pl.BlockSpec((1, dv), lambda i, h: (0, 0)), pl.BlockSpec((1, w), lambda i, h: (0, 0))],
        out_shape=[f32s, bf16s, f32s, bf16s, bf16s, bf16s,
                   jax.ShapeDtypeStruct((1, dv), F32), jax.ShapeDtypeStruct((1, w), F32)],
        compiler_params=_params(("arbitrary", "arbitrary")),
    )(o_gla, z_all, o_dn, z_all, z_all, z_all, gla_norm, dn_norm, dmix)


def _place():
    return lax.axis_index("x"), lax.axis_index("y"), lax.axis_index("c")


def _other_chips(x, y):
    return [(1 - x, y), (x, 1 - y), (1 - x, 1 - y)]


def _rcopy(src, dst, send_sem, recv_sem, dev):
    return pltpu.make_async_remote_copy(src_ref=src, dst_ref=dst, send_sem=send_sem, recv_sem=recv_sem,
                                        device_id=dev, device_id_type=MESH)


ANY = pl.BlockSpec(memory_space=pl.ANY)


def _allgather_big(flat):
    R, C = flat.shape
    Rh = R // 2

    def body(src, out, send_sems, recv_sems, local_sem):
        x, y, c = _place()
        me, sib = 2 * x + y, (x, y, 1 - c)
        chips = _other_chips(x, y)

        def half(j, hc):
            return out.at[j, pl.ds(pl.multiple_of(hc * Rh, 16), Rh), :]

        mine = pltpu.make_async_copy(src, out.at[me], local_sem)
        mine.start()
        src_half = src.at[pl.ds(pl.multiple_of(c * Rh, 16), Rh), :]
        sends = [_rcopy(src_half, half(me, c), send_sems.at[k], recv_sems.at[k], (px, py, c))
                 for k, (px, py) in enumerate(chips)]
        for cp in sends:
            cp.start()
        passed = []
        for k, (px, py) in enumerate(chips):
            pj = 2 * px + py
            _rcopy(src_half, half(pj, c), send_sems.at[k], recv_sems.at[k], (px, py, c)).wait_recv()
            f = _rcopy(half(pj, c), half(pj, c), send_sems.at[3 + k], recv_sems.at[3 + k], sib)
            f.start()
            passed.append(f)
        for k, (px, py) in enumerate(chips):
            pj = 2 * px + py
            _rcopy(half(pj, 1 - c), half(pj, 1 - c), send_sems.at[3 + k], recv_sems.at[3 + k], sib).wait_recv()
        for cp in sends + passed:
            cp.wait_send()
        mine.wait()

    return pl.pallas_call(
        body, name="allgather_weights", in_specs=[ANY], out_specs=ANY,
        out_shape=jax.ShapeDtypeStruct((4, R, C), flat.dtype),
        scratch_shapes=[pltpu.SemaphoreType.DMA((6,)), pltpu.SemaphoreType.DMA((6,)), pltpu.SemaphoreType.DMA(())],
    )(flat)


def _pair_exchange(p):
    _, R, C = p.shape
    Rh = R // 2

    def body(src, out, send_sem, recv_sem):
        x, y, c = _place()
        theirs = src.at[:, pl.ds(pl.multiple_of((1 - c) * Rh, 8), Rh), :]
        cp = _rcopy(theirs, out, send_sem, recv_sem, (x, y, 1 - c))
        cp.start()
        cp.wait()

    return pl.pallas_call(
        body, name="grad_pair_exchange", in_specs=[ANY], out_specs=ANY,
        out_shape=jax.ShapeDtypeStruct((4, Rh, C), p.dtype),
        scratch_shapes=[pltpu.SemaphoreType.DMA(()), pltpu.SemaphoreType.DMA(())],
    )(p)


def _pair_sum(p, got, c_idx):
    _, R, C = p.shape
    Rh = R // 2
    bt = _pick(Rh, 512, 16)
    nb = Rh // bt

    def body(c_ref, a, b, of, ob):
        s = a[...] + b[...]
        of[...] = s
        ob[...] = s.astype(BF16)

    spec = pl.BlockSpec((None, bt, C), lambda j, i, c_ref: (j, i, 0))
    return pl.pallas_call(
        body, name="grad_pair_sum",
        grid_spec=pltpu.PrefetchScalarGridSpec(
            num_scalar_prefetch=1, grid=(4, nb),
            in_specs=[pl.BlockSpec((None, bt, C), lambda j, i, c_ref: (j, c_ref[0] * nb + i, 0)), spec],
            out_specs=[spec, spec]),
        out_shape=[jax.ShapeDtypeStruct((4, Rh, C), F32), jax.ShapeDtypeStruct((4, Rh, C), BF16)],
        compiler_params=_params(("parallel", "parallel")),
    )(c_idx, p, got)


def _chip_scatter(qb):
    _, Rh, C = qb.shape

    def body(src, out, send_sems, recv_sems):
        x, y, c = _place()
        cps = [_rcopy(src.at[2 * px + py], out.at[k], send_sems.at[k], recv_sems.at[k], (px, py, c))
               for k, (px, py) in enumerate(_other_chips(x, y))]
        for cp in cps:
            cp.start()
        for cp in cps:
            cp.wait()

    return pl.pallas_call(
        body, name="grad_chip_scatter", in_specs=[ANY], out_specs=ANY,
        out_shape=jax.ShapeDtypeStruct((3, Rh, C), qb.dtype),
        scratch_shapes=[pltpu.SemaphoreType.DMA((3,)), pltpu.SemaphoreType.DMA((3,))],
    )(qb)


def _final_sum(qf, got, me_idx):
    _, Rh, C = qf.shape
    bt = _pick(Rh, 512, 16)

    def body(me_ref, a, b, o):
        o[...] = ((a[...] + b[0].astype(F32)) + b[1].astype(F32)) + b[2].astype(F32)

    return pl.pallas_call(
        body, name="grad_final_sum",
        grid_spec=pltpu.PrefetchScalarGridSpec(
            num_scalar_prefetch=1, grid=(Rh // bt,),
            in_specs=[pl.BlockSpec((None, bt, C), lambda i, me_ref: (me_ref[0], i, 0)),
                      pl.BlockSpec((3, bt, C), lambda i, me_ref: (0, i, 0))],
            out_specs=pl.BlockSpec((bt, C), lambda i, me_ref: (i, 0))),
        out_shape=jax.ShapeDtypeStruct((Rh, C), F32),
        compiler_params=_params(("parallel",)),
    )(me_idx, qf, got)


def _pair_allgather(half_rows):
    Rh, C = half_rows.shape

    def body(src, out, send_sem, recv_sem, local_sem):
        x, y, c = _place()
        rows = lambda hc: out.at[pl.ds(pl.multiple_of(hc * Rh, 8), Rh), :]
        mine = pltpu.make_async_copy(src, rows(c), local_sem)
        mine.start()
        cp = _rcopy(src, rows(c), send_sem, recv_sem, (x, y, 1 - c))
        cp.start()
        _rcopy(src, rows(1 - c), send_sem, recv_sem, (x, y, 1 - c)).wait_recv()
        cp.wait_send()
        mine.wait()

    return pl.pallas_call(
        body, name="grad_pair_allgather", in_specs=[ANY], out_specs=ANY,
        out_shape=jax.ShapeDtypeStruct((2 * Rh, C), half_rows.dtype),
        scratch_shapes=[pltpu.SemaphoreType.DMA(()), pltpu.SemaphoreType.DMA(()), pltpu.SemaphoreType.DMA(())],
    )(half_rows)


def _allgather8(packed, name, with_sum):
    Rs, C = packed.shape
    VMEM = pl.BlockSpec(memory_space=pltpu.VMEM)

    def body(src, out, *rest):
        if with_sum:
            tot, send_sems, recv_sems = rest
        else:
            send_sems, recv_sems = rest
        x, y, c = _place()
        me = 4 * x + 2 * y + c
        flip = lambda v, f: (1 - v) if f else v
        peers = [(flip(x, r >> 2 & 1), flip(y, r >> 1 & 1), flip(c, r & 1)) for r in range(1, 8)]
        out[me] = src[...]
        cps = [_rcopy(src, out.at[me], send_sems.at[k], recv_sems.at[k], dev) for k, dev in enumerate(peers)]
        for cp in cps:
            cp.start()
        for k, (px, py, pc) in enumerate(peers):
            _rcopy(src, out.at[4 * px + 2 * py + pc], send_sems.at[k], recv_sems.at[k], (px, py, pc)).wait_recv()
        for cp in cps:
            cp.wait_send()
        if with_sum:
            s = out[0]
            for d in range(1, 8):
                s = s + out[d]
            tot[...] = s

    out_shape = [jax.ShapeDtypeStruct((8, Rs, C), F32)] + ([jax.ShapeDtypeStruct((Rs, C), F32)] if with_sum else [])
    return pl.pallas_call(
        body, name=name, in_specs=[VMEM], out_specs=[VMEM] * len(out_shape), out_shape=out_shape,
        scratch_shapes=[pltpu.SemaphoreType.DMA((7,)), pltpu.SemaphoreType.DMA((7,))],
        compiler_params=pltpu.CompilerParams(vmem_limit_bytes=VMEM_LIMIT_BYTES),
    )(packed)


def _flat_rows(a):
    v = a.reshape(-1)
    rows = -(-v.shape[0] // FLAT_COLS)
    return jnp.pad(v, (0, rows * FLAT_COLS - v.shape[0])).reshape(rows, FLAT_COLS)


def _pack(arrs, cols, row_mult):
    v = jnp.concatenate([a.reshape(-1) for a in arrs])
    unit = cols * row_mult
    n = -(-v.shape[0] // unit) * unit
    return jnp.pad(v, (0, n - v.shape[0])).reshape(-1, cols)


def _unpack(packed, shapes):
    v = packed.reshape(-1)
    out, off = [], 0
    for s in shapes:
        n = 1
        for d in s:
            n *= d
        out.append(v[off:off + n].reshape(s))
        off += n
    return out


def _big_rows(shard_shapes):
    rows = [-(-(r * c) // FLAT_COLS) for r, c in shard_shapes]
    total = -(-sum(rows) // 32) * 32
    return rows, total


def _split_w_in(w, D):
    pad = jnp.zeros((w.shape[0], ZS - 3 * LOWRANK), w.dtype)
    return jnp.concatenate([w[:, :3 * D], w[:, 3 * D + 16:6 * D + 16], w[:, 6 * D + 16:7 * D + 16], w[:, 7 * D + 48:],
                            w[:, 3 * D:3 * D + 16], w[:, 7 * D + 16:7 * D + 48], pad], axis=1)


def _join_w_in(g, D):
    n = 9 * D
    return jnp.concatenate([g[:, :3 * D], g[:, n:n + 16], g[:, 3 * D:6 * D], g[:, 6 * D:7 * D], g[:, n + 16:n + 48],
                            g[:, 7 * D:9 * D]], axis=1)


def kernel(x, p, g_mix, w_in, gla_w2, gla_b, gla_norm, dn_conv, dn_a_log, dn_dt_bias, dn_norm, w_out, g_mlp, w_up, w_down, g_ple, w_ple_gate, w_ple_proj, g_final, loss_target, m_g_mix, m_w_in, m_gla_w2, m_gla_b, m_gla_norm, m_dn_conv, m_dn_a_log, m_dn_dt_bias, m_dn_norm, m_w_out, m_g_mlp, m_w_up, m_w_down, m_g_ple, m_w_ple_gate, m_w_ple_proj, m_g_final, v_g_mix, v_w_in, v_gla_w2, v_gla_b, v_gla_norm, v_dn_conv, v_dn_a_log, v_dn_dt_bias, v_dn_norm, v_w_out, v_g_mlp, v_w_up, v_w_down, v_g_ple, v_w_ple_gate, v_w_ple_proj, v_g_final):
    wts = dict(zip(WEIGHTS, [g_mix, w_in, gla_w2, gla_b, gla_norm, dn_conv, dn_a_log, dn_dt_bias, dn_norm, w_out, g_mlp,
                             w_up, w_down, g_ple, w_ple_gate, w_ple_proj, g_final]))
    mom = dict(zip(WEIGHTS, [m_g_mix, m_w_in, m_gla_w2, m_gla_b, m_gla_norm, m_dn_conv, m_dn_a_log, m_dn_dt_bias, m_dn_norm,
                             m_w_out, m_g_mlp, m_w_up, m_w_down, m_g_ple, m_w_ple_gate, m_w_ple_proj, m_g_final]))
    var = dict(zip(WEIGHTS, [v_g_mix, v_w_in, v_gla_w2, v_gla_b, v_gla_norm, v_dn_conv, v_dn_a_log, v_dn_dt_bias, v_dn_norm,
                             v_w_out, v_g_mlp, v_w_up, v_w_down, v_g_ple, v_w_ple_gate, v_w_ple_proj, v_g_final]))
    Bl, S, D = x.shape
    T = Bl * S
    PLE = p.shape[-1]
    dn_d = D // DN_HEADS
    ix, iy, ic = _place()
    j_me = 2 * ix + iy

    shard2d = {n: wts[n].reshape(wts[n].shape[-2], wts[n].shape[-1]) for n, _ in BIG}
    shard_shapes = [shard2d[n].shape for n, _ in BIG]
    rows, R = _big_rows(shard_shapes)
    flat = jnp.concatenate([_flat_rows(shard2d[n].astype(BF16)) for n, _ in BIG], axis=0)
    flat = jnp.pad(flat, ((0, R - flat.shape[0]), (0, 0)))
    gathered = _allgather_big(flat)

    def unflatten(slot_rows, want):
        out, off = {}, 0
        for (n, _), (r, c), nr in zip(BIG, shard_shapes, rows):
            if n in want:
                out[n] = slot_rows[off:off + nr].reshape(-1)[:r * c].reshape(r, c)
            off += nr
        return out

    per_slot = [unflatten(gathered[j], dict(BIG)) for j in range(4)]
    full = {n: jnp.concatenate([per_slot[j][n] for j in range(4)], axis=ax) for n, ax in BIG}
    w_all = _split_w_in(full['w_in'], D)

    small_fwd = _allgather8(_pack([gla_w2, dn_conv], LANES, 8), "allgather_small_weights", False)[0]
    w2_sh, conv_sh = gla_w2.shape[1:], dn_conv.shape[1:]
    parts = [_unpack(small_fwd[2 * j], [w2_sh, conv_sh]) for j in range(4)]
    w2_full = jnp.concatenate([q[0] for q in parts], axis=1)
    conv_full = jnp.concatenate([q[1] for q in parts], axis=1)
    w2pad = jnp.pad(w2_full, ((0, ZS - LOWRANK), (0, 0)))

    xt = x.reshape(T, D)
    tgt = loss_target.reshape(T, D)
    pt = p.reshape(T, PLE)
    h = _rmsnorm_fwd(xt, g_mix, "rms1_fwd")
    (z_all,) = _matmul(h, w_all, 'nn', [F32], "proj_in", bm=1024, bn=640)
    o_gla, st_all = _gla_fwd(z_all, w2pad, gla_b, Bl, S, D)
    acts = [_conv_fwd(z_all, conv_full, grp, Bl, S, D) for grp in range(3)]
    n0 = 9 * D

    def widen(cols):
        return jnp.broadcast_to(cols.T[:, :, None], (DN_HEADS, T, dn_d))
    aw, bw = widen(z_all[:, n0 + 16:n0 + 32]), widen(z_all[:, n0 + 32:n0 + 48])
    alog_w = jnp.broadcast_to(dn_a_log.reshape(DN_HEADS, 1, 1), (DN_HEADS, 1, dn_d))
    dtb_w = jnp.broadcast_to(dn_dt_bias.reshape(DN_HEADS, 1, 1), (DN_HEADS, 1, dn_d))
    o_dn, s_all = _dn_fwd(acts[0], acts[1], acts[2], aw, bw, alog_w, dtb_w, Bl, S, D)
    mixed = _merge_fwd(o_gla, o_dn, z_all, gla_norm, dn_norm, D)
    (x1,) = _matmul(mixed, full['w_out'], 'nn', [F32], "proj_out", epilogue=lambda r, e: (e + r,), extras=(xt,), bm=512)
    h2 = _rmsnorm_fwd(x1, g_mlp, "rms2_fwd")
    u, act = _matmul(h2, full['w_up'], 'nn', [F32, BF16], "mlp_up",
                     epilogue=lambda r: (r, jnp.square(jnp.maximum(r, 0.0))), bm=512)
    (x2,) = _matmul(act, full['w_down'], 'nn', [F32], "mlp_down", epilogue=lambda r, e: (e + r,), extras=(x1,), bm=512)
    h3 = _rmsnorm_fwd(x2, g_ple, "rms3_fwd")
    (pp,) = _matmul(pt, full['w_ple_proj'], 'nn', [F32], "ple_proj")
    gp, x3 = _matmul(h3, full['w_ple_gate'], 'nn', [F32, F32], "ple_gate",
                     epilogue=lambda r, e, q: (r, e + _sigmoid(r) * q), extras=(x2, pp), bm=512)
    dx3, loss_tile, d_g_final = _loss_fwd_bwd(x3, g_final.reshape(1, D), tgt, "loss")

    grads = {}
    d_gp, d_pp = _ple_bwd(dx3, gp, pp, "ple_bwd")
    (grads['w_ple_proj'],) = _matmul(pt, d_pp, 'tn', [F32], "ple_proj_dw")
    (grads['w_ple_gate'],) = _matmul(h3, d_gp, 'tn', [F32], "ple_gate_dw")
    (dh3,) = _matmul(d_gp, full['w_ple_gate'], 'nt', [F32], "ple_gate_dx")
    dx2, dx2b, grads['g_ple'] = _rmsnorm_bwd_add(x2, g_ple, dh3, dx3, "rms3_bwd")
    (grads['w_down'],) = _matmul(act, dx2b, 'tn', [F32], "mlp_down_dw")
    (du,) = _matmul(dx2b, full['w_down'], 'nt', [BF16], "mlp_down_dx",
                    epilogue=lambda r, e: (r * 2.0 * jnp.maximum(e, 0.0),), extras=(u,), bm=512)
    (grads['w_up'],) = _matmul(h2, du, 'tn', [F32], "mlp_up_dw")
    (dh2,) = _matmul(du, full['w_up'], 'nt', [F32], "mlp_up_dx")
    dx1, dx1b, grads['g_mlp'] = _rmsnorm_bwd_add(x1, g_mlp, dh2, dx2, "rms2_bwd")
    (grads['w_out'],) = _matmul(mixed, dx1b, 'tn', [F32], "proj_out_dw")
    (dmix,) = _matmul(dx1b, full['w_out'], 'nt', [F32], "proj_out_dx")
    d_ogla, d_gg, d_odn, d_dz, d_ga, d_gb, grads['gla_norm'], grads['dn_norm'] = _merge_bwd(
        o_gla, o_dn, z_all, gla_norm, dn_norm, dmix, D)
    d_q, d_k, d_v, d_zs_heads, d_w2pad, grads['gla_b'] = _gla_bwd(z_all, w2pad, gla_b, st_all, d_ogla, Bl, S, D)
    d_qa, d_ka, d_va, d_aw, d_bw, d_alog_w, d_dtb_w = _dn_bwd(acts[0], acts[1], acts[2], aw, bw, alog_w, dtb_w, s_all,
                                                               d_odn, Bl, S, D)
    conv_b = [_conv_bwd(z_all, conv_full, g, grp, Bl, S, D) for grp, g in enumerate([d_qa, d_ka, d_va])]
    d_conv_full = jnp.concatenate([cb[1] for cb in conv_b], axis=1)
    d_zs = jnp.sum(d_zs_heads, axis=0)
    col = lax.broadcasted_iota(jnp.int32, (1, ZS), 1)
    d_small = jnp.where(col < LOWRANK, d_zs, 0.0)
    d_small = d_small.at[:, 16:32].set(d_aw[:, :, 0].T).at[:, 32:48].set(d_bw[:, :, 0].T)
    dz_all = jnp.concatenate([d_q, d_k, d_v, d_gg, conv_b[0][0], conv_b[1][0], conv_b[2][0], d_dz, d_ga, d_gb,
                              d_small.astype(BF16)], axis=1)
    (d_w_all,) = _matmul(h, dz_all, 'tn', [F32], "proj_in_dw", bn=640)
    (dh,) = _matmul(dz_all, w_all, 'nt', [F32], "proj_in_dx", bk=640)
    grad_x, _, grads['g_mix'] = _rmsnorm_bwd_add(xt, g_mix, dh, dx1, "rms1_bwd")
    grads['w_in'] = _join_w_in(d_w_all, D)
    grads['g_final'] = d_g_final
    grads['dn_a_log'] = d_alog_w[:, 0, 0]
    grads['dn_dt_bias'] = d_dtb_w[:, 0, 0]

    def shard_of(g, ax, j):
        n = g.shape[ax] // 4
        return lax.slice_in_dim(g, j * n, (j + 1) * n, axis=ax)

    send = jnp.stack([
        jnp.pad(jnp.concatenate([_flat_rows(shard_of(grads[n], ax, j)) for n, ax in BIG], axis=0), ((0, R - sum(rows)), (0, 0)))
        for j in range(4)])
    from_sibling = _pair_exchange(send)
    chip_f32, chip_bf16 = _pair_sum(send, from_sibling, ic.reshape(1).astype(jnp.int32))
    from_chips = _chip_scatter(chip_bf16)
    my_half = _final_sum(chip_f32, from_chips, j_me.reshape(1).astype(jnp.int32))
    reduced = unflatten(_pair_allgather(my_half), dict(BIG))

    small_shapes = {'g_mix': (1, D), 'gla_w2': (1, LOWRANK, D // 2), 'gla_b': (1, D // 2), 'gla_norm': (1, D // GLA_HEADS),
                    'dn_conv': (1, DN_CONV, 3 * D), 'dn_a_log': (1, DN_HEADS), 'dn_dt_bias': (1, DN_HEADS),
                    'dn_norm': (1, dn_d), 'g_mlp': (1, D), 'g_ple': (1, D), 'g_final': (D,)}
    grads['gla_w2'] = d_w2pad[:LOWRANK]
    grads['dn_conv'] = d_conv_full
    packed = _pack([loss_tile[0, :1]] + [grads[n] for n in SMALL], LANES, 8)
    _, total = _allgather8(packed, "allreduce_small_grads", True)
    summed = _unpack(total, [(1,)] + [small_shapes[n] for n in SMALL])
    loss = summed[0].reshape(())
    gsmall = dict(zip(SMALL, summed[1:]))
    for n in ('gla_w2', 'dn_conv'):
        width = gsmall[n].shape[-1] // 4
        gsmall[n] = lax.dynamic_slice_in_dim(gsmall[n], j_me * width, width, axis=2)

    g_out, d_out, m_out, v_out = {}, {}, {}, {}
    for n, _ in BIG:
        shp = wts[n].shape
        g2 = reduced[n]
        d2, nm2, nv2 = _adamw(shard2d[n], g2, mom[n].reshape(g2.shape), var[n].reshape(g2.shape), f"adamw_{n}")
        g_out[n], d_out[n], m_out[n], v_out[n] = g2.reshape(shp), d2.reshape(shp), nm2.reshape(shp), nv2.reshape(shp)
    sm_shapes = [wts[n].shape for n in SMALL]
    pk = lambda d: _pack([d[n] for n in SMALL], LANES, 8)
    ds, nms, nvs = _adamw(pk(wts), pk(gsmall), pk(mom), pk(var), "adamw_small")
    for n, dd, mm, vv in zip(SMALL, _unpack(ds, sm_shapes), _unpack(nms, sm_shapes), _unpack(nvs, sm_shapes)):
        g_out[n], d_out[n], m_out[n], v_out[n] = gsmall[n].reshape(wts[n].shape), dd, mm, vv

    return (loss, grad_x.reshape(Bl, S, D), *[g_out[n] for n in WEIGHTS], *[d_out[n] for n in WEIGHTS],
            *[m_out[n] for n in WEIGHTS], *[v_out[n] for n in WEIGHTS])
```

```python
import functools

import jax
import jax.numpy as jnp
from jax import lax
from jax.experimental import pallas as pl
from jax.experimental.pallas import tpu as pltpu

F32 = jnp.float32
BF16 = jnp.bfloat16

CHUNK = 64
GLA_HEADS = 4
DN_HEADS = 16
LOWRANK = 16
GLA_TAU = 16.0
DN_CONV = 4
EPS = 1e-6
ZS = 128
ADAM_LR, ADAM_B1, ADAM_B2, ADAM_EPS, ADAM_WD, ADAM_STEP = 0.001, 0.9, 0.999, 1e-08, 0.01, 10

V7X_VMEM_BYTES = 64 * 1024 * 1024
VMEM_LIMIT_BYTES = V7X_VMEM_BYTES - 8 * 1024 * 1024
LANES = 128
FLAT_COLS = 512
MESH = pl.DeviceIdType.MESH

WEIGHTS = ['g_mix', 'w_in', 'gla_w2', 'gla_b', 'gla_norm', 'dn_conv', 'dn_a_log', 'dn_dt_bias', 'dn_norm', 'w_out',
           'g_mlp', 'w_up', 'w_down', 'g_ple', 'w_ple_gate', 'w_ple_proj', 'g_final']
BIG = [('w_in', 1), ('w_out', 0), ('w_up', 1), ('w_down', 0), ('w_ple_gate', 0), ('w_ple_proj', 1)]
SMALL = [n for n in WEIGHTS if n not in dict(BIG)]

_NN, _NT, _TN = 'nn', 'nt', 'tn'


def _params(sem=None):
    return pltpu.CompilerParams(dimension_semantics=sem, vmem_limit_bytes=VMEM_LIMIT_BYTES)


def _dot(a, b, form, precision=None):
    o = a.ndim - 2
    contract = {_NN: ((1 + o,), (o,)), _NT: ((1 + o,), (1 + o,)), _TN: ((o,), (o,))}[form]
    batch = ((0,), (0,)) if o else ((), ())
    return lax.dot_general(a, b, (contract, batch), precision=precision, preferred_element_type=F32)


def _make_mm(cast, precision):
    def raw(a, b, dims):
        return _dot(cast(a), cast(b), dims, precision)

    @jax.custom_vjp
    def nn(a, b):
        return raw(a, b, _NN)
    nn.defvjp(lambda a, b: (raw(a, b, _NN), (a, b)), lambda r, g: (raw(g, r[1], _NT), raw(r[0], g, _TN)))

    @jax.custom_vjp
    def nt(a, b):
        return raw(a, b, _NT)
    nt.defvjp(lambda a, b: (raw(a, b, _NT), (a, b)), lambda r, g: (raw(g, r[1], _NN), raw(g, r[0], _TN)))

    @jax.custom_vjp
    def tn(a, b):
        return raw(a, b, _TN)
    tn.defvjp(lambda a, b: (raw(a, b, _TN), (a, b)), lambda r, g: (raw(r[1], g, _NT), raw(r[0], g, _NN)))
    return nn, nt, tn


_bnn, _bnt, _btn = _make_mm(lambda t: t.astype(BF16), None)
TRI_PRECISION = lax.Precision.HIGH


def _iota2(n, axis):
    return lax.broadcasted_iota(jnp.int32, (n, n), axis)


def _lower(n, strict=False):
    return (_iota2(n, 0) > _iota2(n, 1)) if strict else (_iota2(n, 0) >= _iota2(n, 1))


def _tri_times(tri, x):
    tri = tri.astype(F32)
    if x.ndim == 3:
        tri = jnp.broadcast_to(tri, (x.shape[0],) + tri.shape)
    return _dot(tri, x, _NN, lax.Precision.HIGHEST)


@jax.custom_vjp
def _cumsum_rows(x):
    return _tri_times(_lower(x.shape[-2]), x)


def _cumsum_rows_bwd(_, g):
    n = g.shape[-2]
    return (_tri_times(_iota2(n, 0) <= _iota2(n, 1), g),)


_cumsum_rows.defvjp(lambda x: (_cumsum_rows(x), None), _cumsum_rows_bwd)


def _tri_inv_impl(a):
    n = a.shape[-1]
    eye = (_iota2(n, 0) == _iota2(n, 1)).astype(F32)
    p = eye - a
    ak = a
    k = 2
    while k < n:
        ak = _dot(ak, ak, _NN, TRI_PRECISION)
        p = p + _dot(p, ak, _NN, TRI_PRECISION)
        k *= 2
    return p


@jax.custom_vjp
def _tri_inv(a):
    return _tri_inv_impl(a)


def _tri_inv_fwd(a):
    t = _tri_inv_impl(a)
    return t, t


def _tri_inv_bwd(t, g):
    tg = _dot(t, g, _TN, TRI_PRECISION)
    return (-_dot(tg, t, _NT, TRI_PRECISION),)


_tri_inv.defvjp(_tri_inv_fwd, _tri_inv_bwd)


def _shift_rows(x, s, down):
    n = x.shape[0]
    r = lax.broadcasted_iota(jnp.int32, x.shape, 0)
    if down:
        return jnp.where(r >= s, pltpu.roll(x, s, 0), 0.0)
    return jnp.where(r < n - s, pltpu.roll(x, n - s, 0), 0.0)


def _make_shift(s):
    @jax.custom_vjp
    def f(x):
        return _shift_rows(x, s, True)
    f.defvjp(lambda x: (_shift_rows(x, s, True), None), lambda _, g: (_shift_rows(g, s, False),))
    return f


def _sigmoid(x):
    return jax.nn.sigmoid(x)


def _silu(x):
    return x * jax.nn.sigmoid(x)


def _softplus(x):
    return jnp.maximum(x, 0.0) + jnp.log1p(jnp.exp(-jnp.abs(x)))


def _log_sigmoid(x):
    return -_softplus(-x)


def _rms(x, g):
    return x * lax.rsqrt(jnp.mean(x * x, axis=-1, keepdims=True) + EPS) * g


def _gla_chunk(q, k, v, zs, w2, gb, st, *, scale):
    c = q.shape[0]
    logf = _log_sigmoid(_bnn(zs, w2) + gb) * (1.0 / GLA_TAU)
    bcum = _cumsum_rows(logf)
    b_last = jnp.sum(logf, axis=0, keepdims=True)
    q_in = (q * scale) * jnp.exp(bcum)
    k_in = k * jnp.exp(-bcum)
    a = jnp.where(_lower(c), _bnt(q_in, k_in), 0.0)
    o = _bnn(a, v) + _bnt(q_in, st)
    k_dec = k * jnp.exp(b_last - bcum)
    st_new = st * jnp.exp(b_last) + _btn(v, k_dec)
    return o, st_new


def _dn_chunk(q, k, v, aw, bw, alog, dtb, s):
    c = q.shape[-2]
    incl, strict = _lower(c), _lower(c, True)
    g_w = -jnp.exp(alog) * _softplus(aw + dtb)
    beta_w = _sigmoid(bw)
    gcum_w = _cumsum_rows(g_w)
    lane0 = lax.broadcasted_iota(jnp.int32, gcum_w.shape, gcum_w.ndim - 1) == 0
    gcol = jnp.sum(jnp.where(lane0, gcum_w, 0.0), axis=-1, keepdims=True)
    d1 = jnp.broadcast_to(gcol, gcol.shape[:-1] + (c,))
    diff = jnp.where(incl, d1 - jnp.swapaxes(d1, -1, -2), 0.0)
    decay = jnp.where(incl, jnp.exp(diff), 0.0)
    k_beta = k * beta_w
    a = jnp.where(strict, _bnt(k_beta, k) * decay, 0.0)
    t = _tri_inv(a)
    egc = jnp.exp(gcum_w)
    u = _bnn(t, v * beta_w)
    w = _bnn(t, k_beta * egc)
    attn = jnp.where(incl, _bnt(q, k) * decay, 0.0)
    q_dec = q * egc
    g_last = jnp.sum(g_w, axis=-2, keepdims=True)
    k_dec = k * jnp.exp(g_last - gcum_w)
    v_new = u - _bnn(w, s)
    o = _bnn(q_dec, s) + _bnn(attn, v_new)
    s_new = s * jnp.exp(g_last) + _btn(k_dec, v_new)
    return o, s_new


def _conv_act(x, wrows, *, l2, scale):
    taps = len(wrows)
    y = None
    for j in range(taps):
        s = taps - 1 - j
        xs = x if s == 0 else _make_shift(s)(x)
        y = wrows[j] * xs if y is None else y + wrows[j] * xs
    y = _silu(y)
    if l2:
        y = y * lax.rsqrt(jnp.sum(y * y, axis=-1, keepdims=True) + EPS) * scale
    return y


def _merge_math(og, gg, od, dz, ga, gb, gn, dn):
    nsub = len(og)
    dv = nsub * og[0].shape[1]
    ssq = jnp.sum(og[0] * og[0], axis=-1, keepdims=True)
    for s in range(1, nsub):
        ssq = ssq + jnp.sum(og[s] * og[s], axis=-1, keepdims=True)
    r = lax.rsqrt(ssq * (1.0 / dv) + EPS)
    outs = []
    for s in range(nsub):
        a = og[s] * r * gn[s] * _silu(gg[s])
        b = _rms(od[s], dn) * _silu(dz[s])
        outs.append(_sigmoid(ga[s]) * a + _sigmoid(gb[s]) * b)
    return outs


def _pick(n, target, mult):
    best = None
    for d in range(mult, min(n, target) + 1, mult):
        if n % d == 0:
            best = d
    return best if best is not None else n


def _matmul(a, b, form, out_dtypes, name, epilogue=None, extras=(), bm=1024, bn=1024, bk=512):
    if form == 'nn':
        (M, K), (K2, N) = a.shape, b.shape
    elif form == 'nt':
        (M, K), (N, K2) = a.shape, b.shape
    else:
        (K, M), (K2, N) = a.shape, b.shape
    assert K == K2, (a.shape, b.shape, form)
    bm, bn, bk = _pick(M, bm, 8), _pick(N, bn, LANES), _pick(K, bk, LANES)
    nk = K // bk
    dims = {'nn': _NN, 'nt': _NT, 'tn': _TN}[form]
    a_spec = pl.BlockSpec((bk, bm), lambda i, j, k: (k, i)) if form == 'tn' else pl.BlockSpec((bm, bk), lambda i, j, k: (i, k))
    b_spec = pl.BlockSpec((bn, bk), lambda i, j, k: (j, k)) if form == 'nt' else pl.BlockSpec((bk, bn), lambda i, j, k: (k, j))
    o_spec = pl.BlockSpec((bm, bn), lambda i, j, k: (i, j))
    ne, no = len(extras), len(out_dtypes)

    def body(a_ref, b_ref, *rest):
        extra_refs, out_refs, acc = rest[:ne], rest[ne:ne + no], rest[ne + no]
        k = pl.program_id(2)

        @pl.when(k == 0)
        def _():
            acc[...] = jnp.zeros_like(acc)

        acc[...] += _dot(a_ref[...].astype(BF16), b_ref[...].astype(BF16), dims)

        @pl.when(k == nk - 1)
        def _():
            r = acc[...]
            outs = (r,) if epilogue is None else epilogue(r, *[e[...] for e in extra_refs])
            for ref, o in zip(out_refs, outs):
                ref[...] = o.astype(ref.dtype)

    outs = pl.pallas_call(
        body, name=name, grid=(M // bm, N // bn, nk),
        in_specs=[a_spec, b_spec] + [o_spec] * ne,
        out_specs=[o_spec] * no,
        out_shape=[jax.ShapeDtypeStruct((M, N), d) for d in out_dtypes],
        scratch_shapes=[pltpu.VMEM((bm, bn), F32)],
        compiler_params=_params(("parallel", "parallel", "arbitrary")),
    )(a, b, *extras)
    return outs


def _rowwise(fn, rows, consts, row_outs, acc_outs, name, bt=256):
    T = rows[0].shape[0]
    bt = _pick(T, bt, 8)
    nr, nc, no, na = len(rows), len(consts), len(row_outs), len(acc_outs)

    def body(*refs):
        r_in, c_in = refs[:nr], refs[nr:nr + nc]
        r_out, a_out = refs[nr + nc:nr + nc + no], refs[nr + nc + no:]
        ro, ao = fn([r[...] for r in r_in], [c[...] for c in c_in])
        for ref, o in zip(r_out, ro):
            ref[...] = o.astype(ref.dtype)
        if na:
            @pl.when(pl.program_id(0) == 0)
            def _():
                for ref in a_out:
                    ref[...] = jnp.zeros_like(ref)
            for ref, o in zip(a_out, ao):
                ref[...] += o

    whole = lambda shp: pl.BlockSpec(shp, lambda i: (0,) * len(shp))
    outs = pl.pallas_call(
        body, name=name, grid=(T // bt,),
        in_specs=[pl.BlockSpec((bt, r.shape[1]), lambda i: (i, 0)) for r in rows] + [whole(c.shape) for c in consts],
        out_specs=[pl.BlockSpec((bt, w), lambda i: (i, 0)) for w, _ in row_outs] + [whole(s) for s in acc_outs],
        out_shape=[jax.ShapeDtypeStruct((T, w), d) for w, d in row_outs] + [jax.ShapeDtypeStruct(s, F32) for s in acc_outs],
        compiler_params=_params(("arbitrary",)),
    )(*rows, *consts)
    return outs


def _rmsnorm_fwd(x, g, name):
    return _rowwise(lambda r, c: ([_rms(r[0], c[0])], []), [x], [g], [(x.shape[1], BF16)], [], name)[0]


def _rmsnorm_bwd_add(x, g, dh, dres, name):
    D = x.shape[1]

    def fn(r, c):
        _, vjp = jax.vjp(_rms, r[0], c[0])
        dx, dg = vjp(r[1])
        dx = dx + r[2]
        return [dx, dx], [dg]
    return _rowwise(fn, [x, dh, dres], [g], [(D, F32), (D, BF16)], [(1, D)], name)


def _loss_fwd_bwd(x3, g, target, name):
    D = x3.shape[1]

    def fn(r, c):
        def row_loss(x, gain):
            err = _rms(x, gain) - r[1]
            return 0.5 * jnp.mean(err * err, axis=-1, keepdims=True)
        lrow, vjp = jax.vjp(row_loss, r[0], c[0])
        dx, dg = vjp(jnp.ones_like(lrow))
        tile = jnp.broadcast_to(jnp.sum(lrow, axis=0, keepdims=True), (8, LANES))
        return [dx], [tile, dg]
    return _rowwise(fn, [x3, target], [g], [(D, F32)], [(8, LANES), (1, D)], name)


def _ple_bwd(dx3, gp, pp, name):
    D = dx3.shape[1]

    def fn(r, c):
        s = _sigmoid(r[1])
        return [r[0] * r[2] * s * (1.0 - s), r[0] * s], []
    return _rowwise(fn, [dx3, gp, pp], [], [(D, BF16), (D, BF16)], [], name)


def _adamw(w, g, m, v, name):
    R, C = w.shape
    lanes = -(-C // LANES) * LANES
    bt = R if R % 8 else _pick(R, max(8, (1 << 18) // lanes // 8 * 8), 8)

    def body(w_ref, g_ref, m_ref, v_ref, d_ref, nm_ref, nv_ref):
        gg = g_ref[...]
        nm = ADAM_B1 * m_ref[...] + (1.0 - ADAM_B1) * gg
        nv = ADAM_B2 * v_ref[...] + (1.0 - ADAM_B2) * (gg * gg)
        m_hat = nm / (1.0 - ADAM_B1 ** ADAM_STEP)
        v_hat = nv / (1.0 - ADAM_B2 ** ADAM_STEP)
        d_ref[...] = -ADAM_LR * (m_hat / (jnp.sqrt(v_hat) + ADAM_EPS) + ADAM_WD * w_ref[...])
        nm_ref[...] = nm
        nv_ref[...] = nv

    spec = pl.BlockSpec((bt, C), lambda i: (i, 0))
    return pl.pallas_call(
        body, name=name, grid=(R // bt,), in_specs=[spec] * 4, out_specs=[spec] * 3,
        out_shape=[jax.ShapeDtypeStruct((R, C), F32)] * 3, compiler_params=_params(("parallel",)),
    )(w, g, m, v)


def _gla_fwd(z_all, w2pad, gb, Bl, S, D):
    NC, dk, dv = S // CHUNK, D // (2 * GLA_HEADS), D // GLA_HEADS
    zs_blk = (9 * D) // ZS
    scale = dk ** -0.5

    def body(q, k, v, zs, w2, b, o_ref, stall_ref, st):
        @pl.when(pl.program_id(2) == 0)
        def _():
            st[...] = jnp.zeros_like(st)
        stall_ref[...] = st[...]
        o, st_new = _gla_chunk(q[...], k[...], v[...], zs[...], w2[...], b[...], st[...], scale=scale)
        o_ref[...] = o
        st[...] = st_new

    row = lambda h, b, n: b * NC + n
    return pl.pallas_call(
        body, name="gla_fwd", grid=(GLA_HEADS, Bl, NC),
        in_specs=[pl.BlockSpec((CHUNK, dk), lambda h, b, n: (row(h, b, n), h)),
                  pl.BlockSpec((CHUNK, dk), lambda h, b, n: (row(h, b, n), GLA_HEADS + h)),
                  pl.BlockSpec((CHUNK, dv), lambda h, b, n: (row(h, b, n), GLA_HEADS + h)),
                  pl.BlockSpec((CHUNK, ZS), lambda h, b, n: (row(h, b, n), zs_blk)),
                  pl.BlockSpec((ZS, dk), lambda h, b, n: (0, h)),
                  pl.BlockSpec((1, dk), lambda h, b, n: (0, h))],
        out_specs=[pl.BlockSpec((CHUNK, dv), lambda h, b, n: (row(h, b, n), h)),
                   pl.BlockSpec((None, None, None, dv, dk), lambda h, b, n: (h, b, n, 0, 0))],
        out_shape=[jax.ShapeDtypeStruct((Bl * S, D), F32), jax.ShapeDtypeStruct((GLA_HEADS, Bl, NC, dv, dk), F32)],
        scratch_shapes=[pltpu.VMEM((dv, dk), F32)],
        compiler_params=_params(("arbitrary", "arbitrary", "arbitrary")),
    )(z_all, z_all, z_all, z_all, w2pad, gb)


def _gla_bwd(z_all, w2pad, gb, st_all, do, Bl, S, D):
    NC, dk, dv = S // CHUNK, D // (2 * GLA_HEADS), D // GLA_HEADS
    zs_blk = (9 * D) // ZS
    T = Bl * S
    fn = functools.partial(_gla_chunk, scale=dk ** -0.5)

    def body(q, k, v, zs, w2, b, st0, do_ref, dq_ref, dk_ref, dv_ref, dzs_ref, dw2_ref, db_ref, dst):
        @pl.when(pl.program_id(2) == 0)
        def _():
            dst[...] = jnp.zeros_like(dst)

        @pl.when((pl.program_id(1) == 0) & (pl.program_id(2) == 0))
        def _():
            dw2_ref[...] = jnp.zeros_like(dw2_ref)
            db_ref[...] = jnp.zeros_like(db_ref)

        _, vjp = jax.vjp(fn, q[...], k[...], v[...], zs[...], w2[...], b[...], st0[...])
        dq, dkk, dvv, dzs, dw2, db, dst0 = vjp((do_ref[...], dst[...]))
        dq_ref[...] = dq.astype(dq_ref.dtype)
        dk_ref[...] = dkk.astype(dk_ref.dtype)
        dv_ref[...] = dvv.astype(dv_ref.dtype)
        dzs_ref[...] = dzs
        dw2_ref[...] += dw2
        db_ref[...] += db
        dst[...] = dst0

    row = lambda h, b, n: b * NC + (NC - 1 - n)
    return pl.pallas_call(
        body, name="gla_bwd", grid=(GLA_HEADS, Bl, NC),
        in_specs=[pl.BlockSpec((CHUNK, dk), lambda h, b, n: (row(h, b, n), h)),
                  pl.BlockSpec((CHUNK, dk), lambda h, b, n: (row(h, b, n), GLA_HEADS + h)),
                  pl.BlockSpec((CHUNK, dv), lambda h, b, n: (row(h, b, n), GLA_HEADS + h)),
                  pl.BlockSpec((CHUNK, ZS), lambda h, b, n: (row(h, b, n), zs_blk)),
                  pl.BlockSpec((ZS, dk), lambda h, b, n: (0, h)),
                  pl.BlockSpec((1, dk), lambda h, b, n: (0, h)),
                  pl.BlockSpec((None, None, None, dv, dk), lambda h, b, n: (h, b, NC - 1 - n, 0, 0)),
                  pl.BlockSpec((CHUNK, dv), lambda h, b, n: (row(h, b, n), h))],
        out_specs=[pl.BlockSpec((CHUNK, dk), lambda h, b, n: (row(h, b, n), h)),
                   pl.BlockSpec((CHUNK, dk), lambda h, b, n: (row(h, b, n), h)),
                   pl.BlockSpec((CHUNK, dv), lambda h, b, n: (row(h, b, n), h)),
                   pl.BlockSpec((None, CHUNK, ZS), lambda h, b, n: (h, row(h, b, n), 0)),
                   pl.BlockSpec((ZS, dk), lambda h, b, n: (0, h)),
                   pl.BlockSpec((1, dk), lambda h, b, n: (0, h))],
        out_shape=[jax.ShapeDtypeStruct((T, D // 2), BF16), jax.ShapeDtypeStruct((T, D // 2), BF16),
                   jax.ShapeDtypeStruct((T, D), BF16), jax.ShapeDtypeStruct((GLA_HEADS, T, ZS), F32),
                   jax.ShapeDtypeStruct((ZS, D // 2), F32), jax.ShapeDtypeStruct((1, D // 2), F32)],
        scratch_shapes=[pltpu.VMEM((dv, dk), F32)],
        compiler_params=_params(("arbitrary", "arbitrary", "arbitrary")),
    )(z_all, z_all, z_all, z_all, w2pad, gb, st_all, do)


def _conv_fwd(z_all, conv_w, grp, Bl, S, D):
    d = D // DN_HEADS
    l2, scale = grp < 2, (d ** -0.5 if grp == 0 else 1.0)
    x_blk0 = (3 * D + grp * D) // d

    def body(x_ref, w_ref, o_ref):
        wrows = [w_ref[j:j + 1, :] for j in range(DN_CONV)]
        o_ref[...] = _conv_act(x_ref[...], wrows, l2=l2, scale=scale)

    return pl.pallas_call(
        body, name=f"conv_fwd{grp}", grid=(Bl, DN_HEADS),
        in_specs=[pl.BlockSpec((S, d), lambda b, j: (b, x_blk0 + j)),
                  pl.BlockSpec((DN_CONV, d), lambda b, j: (0, grp * DN_HEADS + j))],
        out_specs=pl.BlockSpec((S, d), lambda b, j: (b, j)),
        out_shape=jax.ShapeDtypeStruct((Bl * S, D), F32),
        compiler_params=_params(("parallel", "parallel")),
    )(z_all, conv_w)


def _conv_bwd(z_all, conv_w, dact, grp, Bl, S, D):
    d = D // DN_HEADS
    l2, scale = grp < 2, (d ** -0.5 if grp == 0 else 1.0)
    x_blk0 = (3 * D + grp * D) // d

    def body(x_ref, w_ref, g_ref, dx_ref, dw_ref):
        @pl.when(pl.program_id(1) == 0)
        def _():
            dw_ref[...] = jnp.zeros_like(dw_ref)
        wrows = [w_ref[j:j + 1, :] for j in range(DN_CONV)]
        _, vjp = jax.vjp(lambda x, wr: _conv_act(x, wr, l2=l2, scale=scale), x_ref[...], wrows)
        dx, dwr = vjp(g_ref[...])
        dx_ref[...] = dx.astype(dx_ref.dtype)
        for j in range(DN_CONV):
            dw_ref[j:j + 1, :] += dwr[j]

    return pl.pallas_call(
        body, name=f"conv_bwd{grp}", grid=(DN_HEADS, Bl),
        in_specs=[pl.BlockSpec((S, d), lambda j, b: (b, x_blk0 + j)),
                  pl.BlockSpec((DN_CONV, d), lambda j, b: (0, grp * DN_HEADS + j)),
                  pl.BlockSpec((S, d), lambda j, b: (b, j))],
        out_specs=[pl.BlockSpec((S, d), lambda j, b: (b, j)), pl.BlockSpec((DN_CONV, d), lambda j, b: (0, j))],
        out_shape=[jax.ShapeDtypeStruct((Bl * S, D), BF16), jax.ShapeDtypeStruct((DN_CONV, D), F32)],
        compiler_params=_params(("arbitrary", "arbitrary")),
    )(z_all, conv_w, dact)


DN_HEADS_PER_STEP = 4


def _dn_fwd(qa, ka, va, aw, bw, alog, dtb, Bl, S, D):
    NC, d, HB = S // CHUNK, D // DN_HEADS, DN_HEADS_PER_STEP

    chains = [(hh, bb) for hh in range(HB) for bb in range(Bl)]
    G = len(chains)

    def body(q, k, v, a, b, al, dt, o_ref, sall_ref, st):
        @pl.when(pl.program_id(1) == 0)
        def _():
            st[...] = jnp.zeros_like(st)
        tok_in = lambda r: jnp.stack([r[bb, :, hh * d:(hh + 1) * d] for hh, bb in chains])
        head_in = lambda r: jnp.stack([r[hh] for hh, _ in chains])
        s0 = st[...]
        sall_ref[...] = s0.reshape(HB, Bl, d, d)
        o, s_new = _dn_chunk(tok_in(q), tok_in(k), tok_in(v), a[...].reshape(G, CHUNK, d), b[...].reshape(G, CHUNK, d),
                             head_in(al), head_in(dt), s0)
        for g, (hh, bb) in enumerate(chains):
            o_ref[bb, :, hh * d:(hh + 1) * d] = o[g]
        st[...] = s_new

    tok = pl.BlockSpec((Bl, CHUNK, HB * d), lambda g, n: (0, n, g))
    gate = pl.BlockSpec((HB, Bl, CHUNK, d), lambda g, n: (g, 0, n, 0))
    per_head = pl.BlockSpec((HB, 1, d), lambda g, n: (g, 0, 0))
    r3 = lambda t: t.reshape(Bl, S, D)
    r4 = lambda t: t.reshape(DN_HEADS, Bl, S, d)
    o, s_all = pl.pallas_call(
        body, name="dn_fwd", grid=(DN_HEADS // HB, NC),
        in_specs=[tok, tok, tok, gate, gate, per_head, per_head],
        out_specs=[tok, pl.BlockSpec((HB, Bl, None, d, d), lambda g, n: (g, 0, n, 0, 0))],
        out_shape=[jax.ShapeDtypeStruct((Bl, S, D), F32), jax.ShapeDtypeStruct((DN_HEADS, Bl, NC, d, d), F32)],
        scratch_shapes=[pltpu.VMEM((G, d, d), F32)],
        compiler_params=_params(("arbitrary", "arbitrary")),
    )(r3(qa), r3(ka), r3(va), r4(aw), r4(bw), alog, dtb)
    return o.reshape(Bl * S, D), s_all


def _dn_bwd(qa, ka, va, aw, bw, alog, dtb, s_all, do, Bl, S, D):
    NC, d, HB = S // CHUNK, D // DN_HEADS, DN_HEADS_PER_STEP
    T = Bl * S

    def lanesum(t):
        return jnp.broadcast_to(jnp.sum(t, axis=-1, keepdims=True), t.shape)

    chains = [(hh, bb) for hh in range(HB) for bb in range(Bl)]
    G = len(chains)

    def body(q, k, v, a, b, al, dt, s0_ref, do_ref, dq_ref, dk_ref, dv_ref, da_ref, db_ref, dal_ref, ddt_ref, dst):
        @pl.when(pl.program_id(1) == 0)
        def _():
            dst[...] = jnp.zeros_like(dst)
            dal_ref[...] = jnp.zeros_like(dal_ref)
            ddt_ref[...] = jnp.zeros_like(ddt_ref)
        tok_in = lambda r: jnp.stack([r[bb, :, hh * d:(hh + 1) * d] for hh, bb in chains])
        head_in = lambda r: jnp.stack([r[hh] for hh, _ in chains])
        _, vjp = jax.vjp(_dn_chunk, tok_in(q), tok_in(k), tok_in(v), a[...].reshape(G, CHUNK, d), b[...].reshape(G, CHUNK, d),
                         head_in(al), head_in(dt), s0_ref[...].reshape(G, d, d))
        dq, dkk, dvv, da, db, dal, ddt, ds0 = vjp((tok_in(do_ref), dst[...]))
        da_ref[...] = lanesum(da).reshape(HB, Bl, CHUNK, d)
        db_ref[...] = lanesum(db).reshape(HB, Bl, CHUNK, d)
        dal, ddt = lanesum(dal), lanesum(ddt)
        for g, (hh, bb) in enumerate(chains):
            cols = slice(hh * d, (hh + 1) * d)
            dq_ref[bb, :, cols] = dq[g]
            dk_ref[bb, :, cols] = dkk[g]
            dv_ref[bb, :, cols] = dvv[g]
            dal_ref[hh] += dal[g]
            ddt_ref[hh] += ddt[g]
        dst[...] = ds0

    tok = pl.BlockSpec((Bl, CHUNK, HB * d), lambda g, n: (0, NC - 1 - n, g))
    gate = pl.BlockSpec((HB, Bl, CHUNK, d), lambda g, n: (g, 0, NC - 1 - n, 0))
    per_head = pl.BlockSpec((HB, 1, d), lambda g, n: (g, 0, 0))
    r3 = lambda t: t.reshape(Bl, S, D)
    r4 = lambda t: t.reshape(DN_HEADS, Bl, S, d)
    tok_shape = jax.ShapeDtypeStruct((Bl, S, D), F32)
    gate_shape = jax.ShapeDtypeStruct((DN_HEADS, Bl, S, d), F32)
    head_shape = jax.ShapeDtypeStruct((DN_HEADS, 1, d), F32)
    dq, dk, dv, da, db, dal, ddt = pl.pallas_call(
        body, name="dn_bwd", grid=(DN_HEADS // HB, NC),
        in_specs=[tok, tok, tok, gate, gate, per_head, per_head,
                  pl.BlockSpec((HB, Bl, None, d, d), lambda g, n: (g, 0, NC - 1 - n, 0, 0)), tok],
        out_specs=[tok, tok, tok, gate, gate, per_head, per_head],
        out_shape=[tok_shape, tok_shape, tok_shape, gate_shape, gate_shape, head_shape, head_shape],
        scratch_shapes=[pltpu.VMEM((G, d, d), F32)],
        compiler_params=_params(("arbitrary", "arbitrary")),
    )(r3(qa), r3(ka), r3(va), r4(aw), r4(bw), alog, dtb, s_all, r3(do))
    flat = lambda t: t.reshape(T, D)
    return flat(dq), flat(dk), flat(dv), da.reshape(DN_HEADS, T, d), db.reshape(DN_HEADS, T, d), dal, ddt


def _merge_specs(D, bt):
    dv, w = D // GLA_HEADS, D // DN_HEADS
    col = lambda off: pl.BlockSpec((bt, dv), lambda i, h: (i, off // dv + h))
    return dv, w, col


def _merge_load(refs, nsub, w):
    return [[r[:, s * w:(s + 1) * w] for s in range(nsub)] for r in refs]


def _merge_fwd(o_gla, o_dn, z_all, gla_norm, dn_norm, D, bt=256):
    T = o_gla.shape[0]
    bt = _pick(T, bt, 8)
    dv, w, col = _merge_specs(D, bt)
    nsub = dv // w

    def body(og, gg, od, dz, ga, gb, gn, dn, out):
        ogl, ggl, odl, dzl, gal, gbl = _merge_load([og, gg, od, dz, ga, gb], nsub, w)
        gnl = [gn[:, s * w:(s + 1) * w] for s in range(nsub)]
        outs = _merge_math(ogl, ggl, odl, dzl, gal, gbl, gnl, dn[...])
        for s in range(nsub):
            out[:, s * w:(s + 1) * w] = outs[s].astype(out.dtype)

    return pl.pallas_call(
        body, name="merge_fwd", grid=(T // bt, GLA_HEADS),
        in_specs=[col(0), col(2 * D), col(0), col(6 * D), col(7 * D), col(8 * D),
                  pl.BlockSpec((1, dv), lambda i, h: (0, 0)), pl.BlockSpec((1, w), lambda i, h: (0, 0))],
        out_specs=col(0),
        out_shape=jax.ShapeDtypeStruct((T, D), BF16),
        compiler_params=_params(("parallel", "parallel")),
    )(o_gla, z_all, o_dn, z_all, z_all, z_all, gla_norm, dn_norm)


def _merge_bwd(o_gla, o_dn, z_all, gla_norm, dn_norm, dmix, D, bt=256):
    T = o_gla.shape[0]
    bt = _pick(T, bt, 8)
    dv, w, col = _merge_specs(D, bt)
    nsub = dv // w

    def body(og, gg, od, dz, ga, gb, gn, dn, dm, dog, dgg, dod, ddz, dga, dgb, dgn, ddn):
        @pl.when((pl.program_id(0) == 0) & (pl.program_id(1) == 0))
        def _():
            dgn[...] = jnp.zeros_like(dgn)
            ddn[...] = jnp.zeros_like(ddn)

        ogl, ggl, odl, dzl, gal, gbl, dml = _merge_load([og, gg, od, dz, ga, gb, dm], nsub, w)
        gnl = [gn[:, s * w:(s + 1) * w] for s in range(nsub)]
        _, vjp = jax.vjp(_merge_math, ogl, ggl, odl, dzl, gal, gbl, gnl, dn[...])
        g_og, g_gg, g_od, g_dz, g_ga, g_gb, g_gn, g_dn = vjp(dml)
        for s in range(nsub):
            sl = slice(s * w, (s + 1) * w)
            dog[:, sl] = g_og[s]
            dgg[:, sl] = g_gg[s].astype(dgg.dtype)
            dod[:, sl] = g_od[s]
            ddz[:, sl] = g_dz[s].astype(ddz.dtype)
            dga[:, sl] = g_ga[s].astype(dga.dtype)
            dgb[:, sl] = g_gb[s].astype(dgb.dtype)
            dgn[:, sl] += g_gn[s]
        ddn[...] += g_dn

    f32s, bf16s = jax.ShapeDtypeStruct((T, D), F32), jax.ShapeDtypeStruct((T, D), BF16)
    return pl.pallas_call(
        body, name="merge_bwd", grid=(T // bt, GLA_HEADS),
        in_specs=[col(0), col(2 * D), col(0), col(6 * D), col(7 * D), col(8 * D),
                  pl.BlockSpec((1, dv), lambda i, h: (0, 0)), pl.BlockSpec((1, w), lambda i, h: (0, 0)), col(0)],
        out_specs=[col(0)] * 6 + [pl.BlockSpec((1, dv), lambda i, h: (0, 0)), pl.BlockSpec((1, w), lambda i, h: (0, 0))],
        out_shape=[f32s, bf16s, f32s, bf16s, bf16s, bf16s,
                   jax.ShapeDtypeStruct((1, dv), F32), jax.ShapeDtypeStruct((1, w), F32)],
        compiler_params=_params(("arbitrary", "arbitrary")),
    )(o_gla, z_all, o_dn, z_all, z_all, z_all, gla_norm, dn_norm, dmix)


def _place():
    return lax.axis_index("x"), lax.axis_index("y"), lax.axis_index("c")


def _other_chips(x, y):
    return [(1 - x, y), (x, 1 - y), (1 - x, 1 - y)]


def _rcopy(src, dst, send_sem, recv_sem, dev):
    return pltpu.make_async_remote_copy(src_ref=src, dst_ref=dst, send_sem=send_sem, recv_sem=recv_sem,
                                        device_id=dev, device_id_type=MESH)


ANY = pl.BlockSpec(memory_space=pl.ANY)


def _allgather_big(flat):
    R, C = flat.shape
    Rh = R // 2

    def body(src, out, send_sems, recv_sems):
        x, y, c = _place()
        me, sib = 2 * x + y, (x, y, 1 - c)
        chips = _other_chips(x, y)

        def half(j, hc):
            return out.at[j, pl.ds(pl.multiple_of(hc * Rh, 16), Rh), :]

        src_half = src.at[pl.ds(pl.multiple_of(c * Rh, 16), Rh), :]
        sends = [_rcopy(src_half, half(me, c), send_sems.at[k], recv_sems.at[k], (px, py, c))
                 for k, (px, py) in enumerate(chips)]
        for cp in sends:
            cp.start()
        passed = []
        for k, (px, py) in enumerate(chips):
            pj = 2 * px + py
            _rcopy(src_half, half(pj, c), send_sems.at[k], recv_sems.at[k], (px, py, c)).wait_recv()
            f = _rcopy(half(pj, c), half(pj, c), send_sems.at[3 + k], recv_sems.at[3 + k], sib)
            f.start()
            passed.append(f)
        for k, (px, py) in enumerate(chips):
            pj = 2 * px + py
            _rcopy(half(pj, 1 - c), half(pj, 1 - c), send_sems.at[3 + k], recv_sems.at[3 + k], sib).wait_recv()
        for cp in sends + passed:
            cp.wait_send()

    return pl.pallas_call(
        body, name="allgather_weights", in_specs=[ANY], out_specs=ANY,
        out_shape=jax.ShapeDtypeStruct((4, R, C), flat.dtype),
        scratch_shapes=[pltpu.SemaphoreType.DMA((6,)), pltpu.SemaphoreType.DMA((6,))],
    )(flat)


def _pair_exchange(p):
    _, R, C = p.shape
    Rh = R // 2

    def body(src, out, send_sem, recv_sem):
        x, y, c = _place()
        theirs = src.at[:, pl.ds(pl.multiple_of((1 - c) * Rh, 8), Rh), :]
        cp = _rcopy(theirs, out, send_sem, recv_sem, (x, y, 1 - c))
        cp.start()
        cp.wait()

    return pl.pallas_call(
        body, name="grad_pair_exchange", in_specs=[ANY], out_specs=ANY,
        out_shape=jax.ShapeDtypeStruct((4, Rh, C), p.dtype),
        scratch_shapes=[pltpu.SemaphoreType.DMA(()), pltpu.SemaphoreType.DMA(())],
    )(p)


def _pair_sum(p, got, c_idx):
    _, R, C = p.shape
    Rh = R // 2
    bt = _pick(Rh, 512, 16)
    nb = Rh // bt

    def body(c_ref, a, b, of, ob):
        s = a[...] + b[...]
        of[...] = s
        ob[...] = s.astype(BF16)

    spec = pl.BlockSpec((None, bt, C), lambda j, i, c_ref: (j, i, 0))
    return pl.pallas_call(
        body, name="grad_pair_sum",
        grid_spec=pltpu.PrefetchScalarGridSpec(
            num_scalar_prefetch=1, grid=(4, nb),
            in_specs=[pl.BlockSpec((None, bt, C), lambda j, i, c_ref: (j, c_ref[0] * nb + i, 0)), spec],
            out_specs=[spec, spec]),
        out_shape=[jax.ShapeDtypeStruct((4, Rh, C), F32), jax.ShapeDtypeStruct((4, Rh, C), BF16)],
        compiler_params=_params(("parallel", "parallel")),
    )(c_idx, p, got)


def _chip_scatter(qb):
    _, Rh, C = qb.shape

    def body(src, out, send_sems, recv_sems):
        x, y, c = _place()
        cps = [_rcopy(src.at[2 * px + py], out.at[k], send_sems.at[k], recv_sems.at[k], (px, py, c))
               for k, (px, py) in enumerate(_other_chips(x, y))]
        for cp in cps:
            cp.start()
        for cp in cps:
            cp.wait()

    return pl.pallas_call(
        body, name="grad_chip_scatter", in_specs=[ANY], out_specs=ANY,
        out_shape=jax.ShapeDtypeStruct((3, Rh, C), qb.dtype),
        scratch_shapes=[pltpu.SemaphoreType.DMA((3,)), pltpu.SemaphoreType.DMA((3,))],
    )(qb)


def _final_sum(qf, got, me_idx):
    _, Rh, C = qf.shape
    bt = _pick(Rh, 512, 16)

    def body(me_ref, a, b, o):
        o[...] = ((a[...] + b[0].astype(F32)) + b[1].astype(F32)) + b[2].astype(F32)

    return pl.pallas_call(
        body, name="grad_final_sum",
        grid_spec=pltpu.PrefetchScalarGridSpec(
            num_scalar_prefetch=1, grid=(Rh // bt,),
            in_specs=[pl.BlockSpec((None, bt, C), lambda i, me_ref: (me_ref[0], i, 0)),
                      pl.BlockSpec((3, bt, C), lambda i, me_ref: (0, i, 0))],
            out_specs=pl.BlockSpec((bt, C), lambda i, me_ref: (i, 0))),
        out_shape=jax.ShapeDtypeStruct((Rh, C), F32),
        compiler_params=_params(("parallel",)),
    )(me_idx, qf, got)


def _pair_allgather(half_rows):
    Rh, C = half_rows.shape

    def body(src, out, send_sem, recv_sem):
        x, y, c = _place()
        rows = lambda hc: out.at[pl.ds(pl.multiple_of(hc * Rh, 8), Rh), :]
        cp = _rcopy(src, rows(c), send_sem, recv_sem, (x, y, 1 - c))
        cp.start()
        _rcopy(src, rows(1 - c), send_sem, recv_sem, (x, y, 1 - c)).wait_recv()
        cp.wait_send()

    return pl.pallas_call(
        body, name="grad_pair_allgather", in_specs=[ANY], out_specs=ANY,
        out_shape=jax.ShapeDtypeStruct((2 * Rh, C), half_rows.dtype),
        scratch_shapes=[pltpu.SemaphoreType.DMA(()), pltpu.SemaphoreType.DMA(())],
    )(half_rows)


def _allgather8(packed, name, with_sum):
    Rs, C = packed.shape
    VMEM = pl.BlockSpec(memory_space=pltpu.VMEM)

    def body(src, out, *rest):
        if with_sum:
            tot, send_sems, recv_sems = rest
        else:
            send_sems, recv_sems = rest
        x, y, c = _place()
        me = 4 * x + 2 * y + c
        flip = lambda v, f: (1 - v) if f else v
        peers = [(flip(x, r >> 2 & 1), flip(y, r >> 1 & 1), flip(c, r & 1)) for r in range(1, 8)]
        out[me] = src[...]
        cps = [_rcopy(src, out.at[me], send_sems.at[k], recv_sems.at[k], dev) for k, dev in enumerate(peers)]
        for cp in cps:
            cp.start()
        for k, (px, py, pc) in enumerate(peers):
            _rcopy(src, out.at[4 * px + 2 * py + pc], send_sems.at[k], recv_sems.at[k], (px, py, pc)).wait_recv()
        for cp in cps:
            cp.wait_send()
        if with_sum:
            s = out[0]
            for d in range(1, 8):
                s = s + out[d]
            tot[...] = s

    out_shape = [jax.ShapeDtypeStruct((8, Rs, C), F32)] + ([jax.ShapeDtypeStruct((Rs, C), F32)] if with_sum else [])
    return pl.pallas_call(
        body, name=name, in_specs=[VMEM], out_specs=[VMEM] * len(out_shape), out_shape=out_shape,
        scratch_shapes=[pltpu.SemaphoreType.DMA((7,)), pltpu.SemaphoreType.DMA((7,))],
        compiler_params=pltpu.CompilerParams(vmem_limit_bytes=VMEM_LIMIT_BYTES),
    )(packed)


def _flat_rows(a):
    v = a.reshape(-1)
    rows = -(-v.shape[0] // FLAT_COLS)
    return jnp.pad(v, (0, rows * FLAT_COLS - v.shape[0])).reshape(rows, FLAT_COLS)


def _pack(arrs, cols, row_mult):
    v = jnp.concatenate([a.reshape(-1) for a in arrs])
    unit = cols * row_mult
    n = -(-v.shape[0] // unit) * unit
    return jnp.pad(v, (0, n - v.shape[0])).reshape(-1, cols)


def _unpack(packed, shapes):
    v = packed.reshape(-1)
    out, off = [], 0
    for s in shapes:
        n = 1
        for d in s:
            n *= d
        out.append(v[off:off + n].reshape(s))
        off += n
    return out


def _big_rows(shard_shapes):
    rows = [-(-(r * c) // FLAT_COLS) for r, c in shard_shapes]
    total = -(-sum(rows) // 1024) * 1024
    return rows, total


def _split_w_in(w, D):
    pad = jnp.zeros((w.shape[0], ZS - 3 * LOWRANK), w.dtype)
    return jnp.concatenate([w[:, :3 * D], w[:, 3 * D + 16:6 * D + 16], w[:, 6 * D + 16:7 * D + 16], w[:, 7 * D + 48:],
                            w[:, 3 * D:3 * D + 16], w[:, 7 * D + 16:7 * D + 48], pad], axis=1)


def _join_w_in(g, D):
    n = 9 * D
    return jnp.concatenate([g[:, :3 * D], g[:, n:n + 16], g[:, 3 * D:6 * D], g[:, 6 * D:7 * D], g[:, n + 16:n + 48],
                            g[:, 7 * D:9 * D]], axis=1)


def kernel(x, p, g_mix, w_in, gla_w2, gla_b, gla_norm, dn_conv, dn_a_log, dn_dt_bias, dn_norm, w_out, g_mlp, w_up, w_down, g_ple, w_ple_gate, w_ple_proj, g_final, loss_target, m_g_mix, m_w_in, m_gla_w2, m_gla_b, m_gla_norm, m_dn_conv, m_dn_a_log, m_dn_dt_bias, m_dn_norm, m_w_out, m_g_mlp, m_w_up, m_w_down, m_g_ple, m_w_ple_gate, m_w_ple_proj, m_g_final, v_g_mix, v_w_in, v_gla_w2, v_gla_b, v_gla_norm, v_dn_conv, v_dn_a_log, v_dn_dt_bias, v_dn_norm, v_w_out, v_g_mlp, v_w_up, v_w_down, v_g_ple, v_w_ple_gate, v_w_ple_proj, v_g_final):
    wts = dict(zip(WEIGHTS, [g_mix, w_in, gla_w2, gla_b, gla_norm, dn_conv, dn_a_log, dn_dt_bias, dn_norm, w_out, g_mlp,
                             w_up, w_down, g_ple, w_ple_gate, w_ple_proj, g_final]))
    mom = dict(zip(WEIGHTS, [m_g_mix, m_w_in, m_gla_w2, m_gla_b, m_gla_norm, m_dn_conv, m_dn_a_log, m_dn_dt_bias, m_dn_norm,
                             m_w_out, m_g_mlp, m_w_up, m_w_down, m_g_ple, m_w_ple_gate, m_w_ple_proj, m_g_final]))
    var = dict(zip(WEIGHTS, [v_g_mix, v_w_in, v_gla_w2, v_gla_b, v_gla_norm, v_dn_conv, v_dn_a_log, v_dn_dt_bias, v_dn_norm,
                             v_w_out, v_g_mlp, v_w_up, v_w_down, v_g_ple, v_w_ple_gate, v_w_ple_proj, v_g_final]))
    Bl, S, D = x.shape
    T = Bl * S
    PLE = p.shape[-1]
    dn_d = D // DN_HEADS
    ix, iy, ic = _place()
    j_me = 2 * ix + iy

    shard2d = {n: wts[n].reshape(wts[n].shape[-2], wts[n].shape[-1]) for n, _ in BIG}
    shard_shapes = [shard2d[n].shape for n, _ in BIG]
    rows, R = _big_rows(shard_shapes)
    flat = jnp.concatenate([_flat_rows(shard2d[n].astype(BF16)) for n, _ in BIG], axis=0)
    flat = jnp.pad(flat, ((0, R - flat.shape[0]), (0, 0)))
    gathered = lax.dynamic_update_slice(_allgather_big(flat), flat[None], (j_me, 0, 0))

    def unflatten(slot_rows, want):
        out, off = {}, 0
        for (n, _), (r, c), nr in zip(BIG, shard_shapes, rows):
            if n in want:
                out[n] = slot_rows[off:off + nr].reshape(-1)[:r * c].reshape(r, c)
            off += nr
        return out

    per_slot = [unflatten(gathered[j], dict(BIG)) for j in range(4)]
    full = {n: jnp.concatenate([per_slot[j][n] for j in range(4)], axis=ax) for n, ax in BIG}
    w_all = _split_w_in(full['w_in'], D)

    small_fwd = _allgather8(_pack([gla_w2, dn_conv], LANES, 8), "allgather_small_weights", False)[0]
    w2_sh, conv_sh = gla_w2.shape[1:], dn_conv.shape[1:]
    parts = [_unpack(small_fwd[2 * j], [w2_sh, conv_sh]) for j in range(4)]
    w2_full = jnp.concatenate([q[0] for q in parts], axis=1)
    conv_full = jnp.concatenate([q[1] for q in parts], axis=1)
    w2pad = jnp.pad(w2_full, ((0, ZS - LOWRANK), (0, 0)))

    xt = x.reshape(T, D)
    tgt = loss_target.reshape(T, D)
    pt = p.reshape(T, PLE)
    h = _rmsnorm_fwd(xt, g_mix, "rms1_fwd")
    (z_all,) = _matmul(h, w_all, 'nn', [F32], "proj_in", bm=1024, bn=640)
    o_gla, st_all = _gla_fwd(z_all, w2pad, gla_b, Bl, S, D)
    acts = [_conv_fwd(z_all, conv_full, grp, Bl, S, D) for grp in range(3)]
    n0 = 9 * D

    def widen(cols):
        return jnp.broadcast_to(cols.T[:, :, None], (DN_HEADS, T, dn_d))
    aw, bw = widen(z_all[:, n0 + 16:n0 + 32]), widen(z_all[:, n0 + 32:n0 + 48])
    alog_w = jnp.broadcast_to(dn_a_log.reshape(DN_HEADS, 1, 1), (DN_HEADS, 1, dn_d))
    dtb_w = jnp.broadcast_to(dn_dt_bias.reshape(DN_HEADS, 1, 1), (DN_HEADS, 1, dn_d))
    o_dn, s_all = _dn_fwd(acts[0], acts[1], acts[2], aw, bw, alog_w, dtb_w, Bl, S, D)
    mixed = _merge_fwd(o_gla, o_dn, z_all, gla_norm, dn_norm, D)
    (x1,) = _matmul(mixed, full['w_out'], 'nn', [F32], "proj_out", epilogue=lambda r, e: (e + r,), extras=(xt,), bm=512)
    h2 = _rmsnorm_fwd(x1, g_mlp, "rms2_fwd")
    u, act = _matmul(h2, full['w_up'], 'nn', [F32, BF16], "mlp_up",
                     epilogue=lambda r: (r, jnp.square(jnp.maximum(r, 0.0))), bm=512)
    (x2,) = _matmul(act, full['w_down'], 'nn', [F32], "mlp_down", epilogue=lambda r, e: (e + r,), extras=(x1,), bm=512)
    h3 = _rmsnorm_fwd(x2, g_ple, "rms3_fwd")
    (pp,) = _matmul(pt, full['w_ple_proj'], 'nn', [F32], "ple_proj")
    gp, x3 = _matmul(h3, full['w_ple_gate'], 'nn', [F32, F32], "ple_gate",
                     epilogue=lambda r, e, q: (r, e + _sigmoid(r) * q), extras=(x2, pp), bm=512)
    dx3, loss_tile, d_g_final = _loss_fwd_bwd(x3, g_final.reshape(1, D), tgt, "loss")

    grads = {}
    d_gp, d_pp = _ple_bwd(dx3, gp, pp, "ple_bwd")
    (grads['w_ple_proj'],) = _matmul(pt, d_pp, 'tn', [F32], "ple_proj_dw")
    (grads['w_ple_gate'],) = _matmul(h3, d_gp, 'tn', [F32], "ple_gate_dw")
    (dh3,) = _matmul(d_gp, full['w_ple_gate'], 'nt', [F32], "ple_gate_dx")
    dx2, dx2b, grads['g_ple'] = _rmsnorm_bwd_add(x2, g_ple, dh3, dx3, "rms3_bwd")
    (grads['w_down'],) = _matmul(act, dx2b, 'tn', [F32], "mlp_down_dw")
    (du,) = _matmul(dx2b, full['w_down'], 'nt', [BF16], "mlp_down_dx",
                    epilogue=lambda r, e: (r * 2.0 * jnp.maximum(e, 0.0),), extras=(u,), bm=512)
    (grads['w_up'],) = _matmul(h2, du, 'tn', [F32], "mlp_up_dw")
    (dh2,) = _matmul(du, full['w_up'], 'nt', [F32], "mlp_up_dx")
    dx1, dx1b, grads['g_mlp'] = _rmsnorm_bwd_add(x1, g_mlp, dh2, dx2, "rms2_bwd")
    (grads['w_out'],) = _matmul(mixed, dx1b, 'tn', [F32], "proj_out_dw")
    (dmix,) = _matmul(dx1b, full['w_out'], 'nt', [F32], "proj_out_dx")
    d_ogla, d_gg, d_odn, d_dz, d_ga, d_gb, grads['gla_norm'], grads['dn_norm'] = _merge_bwd(
        o_gla, o_dn, z_all, gla_norm, dn_norm, dmix, D)
    d_q, d_k, d_v, d_zs_heads, d_w2pad, grads['gla_b'] = _gla_bwd(z_all, w2pad, gla_b, st_all, d_ogla, Bl, S, D)
    d_qa, d_ka, d_va, d_aw, d_bw, d_alog_w, d_dtb_w = _dn_bwd(acts[0], acts[1], acts[2], aw, bw, alog_w, dtb_w, s_all,
                                                               d_odn, Bl, S, D)
    conv_b = [_conv_bwd(z_all, conv_full, g, grp, Bl, S, D) for grp, g in enumerate([d_qa, d_ka, d_va])]
    d_conv_full = jnp.concatenate([cb[1] for cb in conv_b], axis=1)
    d_zs = jnp.sum(d_zs_heads, axis=0)
    col = lax.broadcasted_iota(jnp.int32, (1, ZS), 1)
    d_small = jnp.where(col < LOWRANK, d_zs, 0.0)
    d_small = d_small.at[:, 16:32].set(d_aw[:, :, 0].T).at[:, 32:48].set(d_bw[:, :, 0].T)
    dz_all = jnp.concatenate([d_q, d_k, d_v, d_gg, conv_b[0][0], conv_b[1][0], conv_b[2][0], d_dz, d_ga, d_gb,
                              d_small.astype(BF16)], axis=1)
    (d_w_all,) = _matmul(h, dz_all, 'tn', [F32], "proj_in_dw", bn=640)
    (dh,) = _matmul(dz_all, w_all, 'nt', [F32], "proj_in_dx", bk=640)
    grad_x, _, grads['g_mix'] = _rmsnorm_bwd_add(xt, g_mix, dh, dx1, "rms1_bwd")
    grads['w_in'] = _join_w_in(d_w_all, D)
    grads['g_final'] = d_g_final
    grads['dn_a_log'] = d_alog_w[:, 0, 0]
    grads['dn_dt_bias'] = d_dtb_w[:, 0, 0]

    def shard_of(g, ax, j):
        n = g.shape[ax] // 4
        return lax.slice_in_dim(g, j * n, (j + 1) * n, axis=ax)

    send = jnp.stack([
        jnp.pad(jnp.concatenate([_flat_rows(shard_of(grads[n], ax, j)) for n, ax in BIG], axis=0), ((0, R - sum(rows)), (0, 0)))
        for j in range(4)])
    from_sibling = _pair_exchange(send)
    chip_f32, chip_bf16 = _pair_sum(send, from_sibling, ic.reshape(1).astype(jnp.int32))
    from_chips = _chip_scatter(chip_bf16)
    my_half = _final_sum(chip_f32, from_chips, j_me.reshape(1).astype(jnp.int32))
    both_halves = lax.dynamic_update_slice(_pair_allgather(my_half), my_half, (ic * (R // 2), 0))
    reduced = unflatten(both_halves, dict(BIG))

    small_shapes = {'g_mix': (1, D), 'gla_w2': (1, LOWRANK, D // 2), 'gla_b': (1, D // 2), 'gla_norm': (1, D // GLA_HEADS),
                    'dn_conv': (1, DN_CONV, 3 * D), 'dn_a_log': (1, DN_HEADS), 'dn_dt_bias': (1, DN_HEADS),
                    'dn_norm': (1, dn_d), 'g_mlp': (1, D), 'g_ple': (1, D), 'g_final': (D,)}
    grads['gla_w2'] = d_w2pad[:LOWRANK]
    grads['dn_conv'] = d_conv_full
    packed = _pack([loss_tile[0, :1]] + [grads[n] for n in SMALL], LANES, 8)
    _, total = _allgather8(packed, "allreduce_small_grads", True)
    summed = _unpack(total, [(1,)] + [small_shapes[n] for n in SMALL])
    loss = summed[0].reshape(())
    gsmall = dict(zip(SMALL, summed[1:]))
    for n in ('gla_w2', 'dn_conv'):
        width = gsmall[n].shape[-1] // 4
        gsmall[n] = lax.dynamic_slice_in_dim(gsmall[n], j_me * width, width, axis=2)

    g_out, d_out, m_out, v_out = {}, {}, {}, {}
    for n, _ in BIG:
        shp = wts[n].shape
        g2 = reduced[n]
        d2, nm2, nv2 = _adamw(shard2d[n], g2, mom[n].reshape(g2.shape), var[n].reshape(g2.shape), f"adamw_{n}")
        g_out[n], d_out[n], m_out[n], v_out[n] = g2.reshape(shp), d2.reshape(shp), nm2.reshape(shp), nv2.reshape(shp)
    sm_shapes = [wts[n].shape for n in SMALL]
    pk = lambda d: _pack([d[n] for n in SMALL], LANES, 8)
    ds, nms, nvs = _adamw(pk(wts), pk(gsmall), pk(mom), pk(var), "adamw_small")
    for n, dd, mm, vv in zip(SMALL, _unpack(ds, sm_shapes), _unpack(nms, sm_shapes), _unpack(nvs, sm_shapes)):
        g_out[n], d_out[n], m_out[n], v_out[n] = gsmall[n].reshape(wts[n].shape), dd, mm, vv

    return (loss, grad_x.reshape(Bl, S, D), *[g_out[n] for n in WEIGHTS], *[d_out[n] for n in WEIGHTS],
            *[m_out[n] for n in WEIGHTS], *[v_out[n] for n in WEIGHTS])
```

```python
import functools

import jax
import jax.numpy as jnp
from jax import lax
from jax.experimental import pallas as pl
from jax.experimental.pallas import tpu as pltpu

F32 = jnp.float32
BF16 = jnp.bfloat16

CHUNK = 64
GLA_HEADS = 4
DN_HEADS = 16
LOWRANK = 16
GLA_TAU = 16.0
DN_CONV = 4
EPS = 1e-6
ZS = 128
A_LANE, B_LANE = LOWRANK, LOWRANK + DN_HEADS
ADAM_LR, ADAM_B1, ADAM_B2, ADAM_EPS, ADAM_WD, ADAM_STEP = 0.001, 0.9, 0.999, 1e-08, 0.01, 10

V7X_VMEM_BYTES = 64 * 1024 * 1024
VMEM_LIMIT_BYTES = V7X_VMEM_BYTES - 8 * 1024 * 1024
LANES = 128
SUBLANES = 8
MESH = pl.DeviceIdType.MESH
DN_HEADS_PER_STEP = 4
GLA_HEADS_PER_STEP = 2

WEIGHTS = ['g_mix', 'w_in', 'gla_w2', 'gla_b', 'gla_norm', 'dn_conv', 'dn_a_log', 'dn_dt_bias', 'dn_norm', 'w_out',
           'g_mlp', 'w_up', 'w_down', 'g_ple', 'w_ple_gate', 'w_ple_proj', 'g_final']
BIG = [('w_in', 1), ('w_out', 0), ('w_up', 1), ('w_down', 0), ('w_ple_gate', 0), ('w_ple_proj', 1)]
SMALL = [n for n in WEIGHTS if n not in dict(BIG)]

_NN, _NT, _TN = 'nn', 'nt', 'tn'


def _params(sem=None):
    return pltpu.CompilerParams(dimension_semantics=sem, vmem_limit_bytes=VMEM_LIMIT_BYTES)


def _dot(a, b, form, precision=None):
    o = a.ndim - 2
    contract = {_NN: ((1 + o,), (o,)), _NT: ((1 + o,), (1 + o,)), _TN: ((o,), (o,))}[form]
    batch = ((0,), (0,)) if o else ((), ())
    return lax.dot_general(a, b, (contract, batch), precision=precision, preferred_element_type=F32)


def _make_mm(cast, precision):
    def raw(a, b, dims):
        return _dot(cast(a), cast(b), dims, precision)

    @jax.custom_vjp
    def nn(a, b):
        return raw(a, b, _NN)
    nn.defvjp(lambda a, b: (raw(a, b, _NN), (a, b)), lambda r, g: (raw(g, r[1], _NT), raw(r[0], g, _TN)))

    @jax.custom_vjp
    def nt(a, b):
        return raw(a, b, _NT)
    nt.defvjp(lambda a, b: (raw(a, b, _NT), (a, b)), lambda r, g: (raw(g, r[1], _NN), raw(g, r[0], _TN)))

    @jax.custom_vjp
    def tn(a, b):
        return raw(a, b, _TN)
    tn.defvjp(lambda a, b: (raw(a, b, _TN), (a, b)), lambda r, g: (raw(r[1], g, _NT), raw(r[0], g, _NN)))
    return nn, nt, tn


_bnn, _bnt, _btn = _make_mm(lambda t: t.astype(BF16), None)
TRI_PRECISION = lax.Precision.HIGH


def _iota2(n, axis):
    return lax.broadcasted_iota(jnp.int32, (n, n), axis)


def _lower(n, strict=False):
    return (_iota2(n, 0) > _iota2(n, 1)) if strict else (_iota2(n, 0) >= _iota2(n, 1))


def _tri_times(tri, x):
    tri = tri.astype(F32)
    if x.ndim == 3:
        tri = jnp.broadcast_to(tri, (x.shape[0],) + tri.shape)
    return _dot(tri, x, _NN, lax.Precision.HIGHEST)


@jax.custom_vjp
def _cumsum_rows(x):
    return _tri_times(_lower(x.shape[-2]), x)


def _cumsum_rows_bwd(_, g):
    n = g.shape[-2]
    return (_tri_times(_iota2(n, 0) <= _iota2(n, 1), g),)


_cumsum_rows.defvjp(lambda x: (_cumsum_rows(x), None), _cumsum_rows_bwd)


def _tri_inv_impl(a):
    n = a.shape[-1]
    eye = (_iota2(n, 0) == _iota2(n, 1)).astype(F32)
    p = eye - a
    ak = a
    k = 2
    while k < n:
        ak = _dot(ak, ak, _NN, TRI_PRECISION)
        p = p + _dot(p, ak, _NN, TRI_PRECISION)
        k *= 2
    return p


@jax.custom_vjp
def _tri_inv(a):
    return _tri_inv_impl(a)


def _tri_inv_fwd(a):
    t = _tri_inv_impl(a)
    return t, t


def _tri_inv_bwd(t, g):
    tg = _dot(t, g, _TN, TRI_PRECISION)
    return (-_dot(tg, t, _NT, TRI_PRECISION),)


_tri_inv.defvjp(_tri_inv_fwd, _tri_inv_bwd)


def _shift_rows(x, s, down):
    n = x.shape[0]
    r = lax.broadcasted_iota(jnp.int32, x.shape, 0)
    if down:
        return jnp.where(r >= s, pltpu.roll(x, s, 0), 0.0)
    return jnp.where(r < n - s, pltpu.roll(x, n - s, 0), 0.0)


def _make_shift(s):
    @jax.custom_vjp
    def f(x):
        return _shift_rows(x, s, True)
    f.defvjp(lambda x: (_shift_rows(x, s, True), None), lambda _, g: (_shift_rows(g, s, False),))
    return f


def _sigmoid(x):
    return jax.nn.sigmoid(x)


def _silu(x):
    return x * jax.nn.sigmoid(x)


def _softplus(x):
    return jnp.maximum(x, 0.0) + jnp.log1p(jnp.exp(-jnp.abs(x)))


def _log_sigmoid(x):
    return -_softplus(-x)


def _rms(x, g):
    return x * lax.rsqrt(jnp.mean(x * x, axis=-1, keepdims=True) + EPS) * g


def _gla_chunk(q, k, v, zs, w2, gb, st, *, scale):
    c = q.shape[-2]
    logf = _log_sigmoid(_bnn(zs, w2) + gb) * (1.0 / GLA_TAU)
    bcum = _cumsum_rows(logf)
    b_last = jnp.sum(logf, axis=-2, keepdims=True)
    q_in = (q * scale) * jnp.exp(bcum)
    k_in = k * jnp.exp(-bcum)
    a = jnp.where(_lower(c), _bnt(q_in, k_in), 0.0)
    o = _bnn(a, v) + _bnt(q_in, st)
    k_dec = k * jnp.exp(b_last - bcum)
    st_new = st * jnp.exp(b_last) + _btn(v, k_dec)
    return o, st_new


def _dn_chunk(q, k, v, aw, bw, alog, dtb, s):
    c = q.shape[-2]
    incl, strict = _lower(c), _lower(c, True)
    g_w = -jnp.exp(alog) * _softplus(aw + dtb)
    beta_w = _sigmoid(bw)
    gcum_w = _cumsum_rows(g_w)
    lane0 = lax.broadcasted_iota(jnp.int32, gcum_w.shape, gcum_w.ndim - 1) == 0
    gcol = jnp.sum(jnp.where(lane0, gcum_w, 0.0), axis=-1, keepdims=True)
    d1 = jnp.broadcast_to(gcol, gcol.shape[:-1] + (c,))
    diff = jnp.where(incl, d1 - jnp.swapaxes(d1, -1, -2), 0.0)
    decay = jnp.where(incl, jnp.exp(diff), 0.0)
    k_beta = k * beta_w
    a = jnp.where(strict, _bnt(k_beta, k) * decay, 0.0)
    t = _tri_inv(a)
    egc = jnp.exp(gcum_w)
    u = _bnn(t, v * beta_w)
    w = _bnn(t, k_beta * egc)
    attn = jnp.where(incl, _bnt(q, k) * decay, 0.0)
    q_dec = q * egc
    g_last = jnp.sum(g_w, axis=-2, keepdims=True)
    k_dec = k * jnp.exp(g_last - gcum_w)
    v_new = u - _bnn(w, s)
    o = _bnn(q_dec, s) + _bnn(attn, v_new)
    s_new = s * jnp.exp(g_last) + _btn(k_dec, v_new)
    return o, s_new


def _conv_act(x, wrows, *, l2, scale):
    taps = len(wrows)
    y = None
    for j in range(taps):
        s = taps - 1 - j
        xs = x if s == 0 else _make_shift(s)(x)
        y = wrows[j] * xs if y is None else y + wrows[j] * xs
    y = _silu(y)
    if l2:
        y = y * lax.rsqrt(jnp.sum(y * y, axis=-1, keepdims=True) + EPS) * scale
    return y


def _merge_math(og, gg, od, dz, ga, gb, gn, dn):
    nsub = len(og)
    dv = nsub * og[0].shape[1]
    ssq = jnp.sum(og[0] * og[0], axis=-1, keepdims=True)
    for s in range(1, nsub):
        ssq = ssq + jnp.sum(og[s] * og[s], axis=-1, keepdims=True)
    r = lax.rsqrt(ssq * (1.0 / dv) + EPS)
    outs = []
    for s in range(nsub):
        a = og[s] * r * gn[s] * _silu(gg[s])
        b = _rms(od[s], dn) * _silu(dz[s])
        outs.append(_sigmoid(ga[s]) * a + _sigmoid(gb[s]) * b)
    return outs


def _pick(n, target, mult):
    best = None
    for d in range(mult, min(n, target) + 1, mult):
        if n % d == 0:
            best = d
    return best if best is not None else n


def _matmul(a, b, form, out_dtypes, name, epilogue=None, extras=(), bm=1024, bn=1024, bk=2048,
            b_slots=False, out_slots=False):
    ns, c = (b.shape[0], b.shape[2]) if b_slots else (1, None)
    b2 = b.shape[1:] if b_slots else b.shape
    if form == 'nn':
        (M, K), (K2, N) = a.shape, (b2[0], b2[1] * ns)
    elif form == 'nt':
        (M, K), (N, K2) = a.shape, (b2[0], b2[1] * ns)
    else:
        (K, M), (K2, N) = a.shape, b2
    assert K == K2 and not (b_slots and form == 'tn'), (a.shape, b.shape, form)
    bm, bn, bk = _pick(M, bm, SUBLANES), _pick(N, bn, LANES), _pick(K, bk, LANES)
    if b_slots:
        bn, bk = (_pick(c, bn, LANES), bk) if form == 'nn' else (bn, _pick(c, bk, LANES))
    if out_slots:
        oc = N // 4
        bn = _pick(oc, bn, LANES)
    nk = K // bk
    a_spec = pl.BlockSpec((bk, bm), lambda i, j, k: (k, i)) if form == 'tn' else pl.BlockSpec((bm, bk), lambda i, j, k: (i, k))
    if b_slots and form == 'nn':
        per = c // bn
        b_spec = pl.BlockSpec((None, bk, bn), lambda i, j, k: (j // per, k, j % per))
    elif b_slots:
        per = c // bk
        b_spec = pl.BlockSpec((None, bn, bk), lambda i, j, k: (k // per, j, k % per))
    elif form == 'nt':
        b_spec = pl.BlockSpec((bn, bk), lambda i, j, k: (j, k))
    else:
        b_spec = pl.BlockSpec((bk, bn), lambda i, j, k: (k, j))
    o_spec = pl.BlockSpec((bm, bn), lambda i, j, k: (i, j))
    if out_slots:
        oper = oc // bn
        out_spec = pl.BlockSpec((None, bm, bn), lambda i, j, k: (j // oper, i, j % oper))
        out_shape = [jax.ShapeDtypeStruct((4, M, oc), d) for d in out_dtypes]
    else:
        out_spec = o_spec
        out_shape = [jax.ShapeDtypeStruct((M, N), d) for d in out_dtypes]
    ne, no = len(extras), len(out_dtypes)

    def finish(r, extra_refs, out_refs):
        outs = (r,) if epilogue is None else epilogue(r, *[e[...] for e in extra_refs])
        for ref, o in zip(out_refs, outs):
            ref[...] = o.astype(ref.dtype)

    def body_one(a_ref, b_ref, *rest):
        finish(_dot(a_ref[...].astype(BF16), b_ref[...].astype(BF16), form), rest[:ne], rest[ne:ne + no])

    def body_acc(a_ref, b_ref, *rest):
        extra_refs, out_refs, acc = rest[:ne], rest[ne:ne + no], rest[ne + no]
        k = pl.program_id(2)
        part = _dot(a_ref[...].astype(BF16), b_ref[...].astype(BF16), form)

        @pl.when(k == 0)
        def _():
            acc[...] = part

        @pl.when((k > 0) & (k < nk - 1))
        def _():
            acc[...] += part

        @pl.when(k == nk - 1)
        def _():
            finish(acc[...] + part, extra_refs, out_refs)

    return pl.pallas_call(
        body_one if nk == 1 else body_acc, name=name, grid=(M // bm, N // bn, nk),
        in_specs=[a_spec, b_spec] + [o_spec] * ne,
        out_specs=[out_spec] * no, out_shape=out_shape,
        scratch_shapes=[] if nk == 1 else [pltpu.VMEM((bm, bn), F32)],
        compiler_params=_params(("parallel", "parallel", "arbitrary")),
    )(a, b, *extras)


def _rowwise(fn, rows, consts, row_outs, acc_outs, name, bt=256):
    T = rows[0].shape[0]
    bt = _pick(T, bt, SUBLANES)
    nr, nc, no, na = len(rows), len(consts), len(row_outs), len(acc_outs)

    def body(*refs):
        r_in, c_in = refs[:nr], refs[nr:nr + nc]
        r_out, a_out = refs[nr + nc:nr + nc + no], refs[nr + nc + no:]
        ro, ao = fn([r[...] for r in r_in], [c[...] for c in c_in])
        for ref, o in zip(r_out, ro):
            ref[...] = o.astype(ref.dtype)
        if na:
            @pl.when(pl.program_id(0) == 0)
            def _():
                for ref in a_out:
                    ref[...] = jnp.zeros_like(ref)
            for ref, o in zip(a_out, ao):
                ref[...] += o

    whole = lambda shp: pl.BlockSpec(shp, lambda i: (0,) * len(shp))
    return pl.pallas_call(
        body, name=name, grid=(T // bt,),
        in_specs=[pl.BlockSpec((bt, r.shape[1]), lambda i: (i, 0)) for r in rows] + [whole(c.shape) for c in consts],
        out_specs=[pl.BlockSpec((bt, w), lambda i: (i, 0)) for w, _ in row_outs] + [whole(s) for s in acc_outs],
        out_shape=[jax.ShapeDtypeStruct((T, w), d) for w, d in row_outs] + [jax.ShapeDtypeStruct(s, F32) for s in acc_outs],
        compiler_params=_params(("arbitrary",)),
    )(*rows, *consts)


def _rmsnorm_fwd(x, g, name):
    return _rowwise(lambda r, c: ([_rms(r[0], c[0])], []), [x], [g], [(x.shape[1], BF16)], [], name)[0]


def _rmsnorm_bwd_add(x, g, dh, dres, name):
    D = x.shape[1]

    def fn(r, c):
        _, vjp = jax.vjp(_rms, r[0], c[0])
        dx, dg = vjp(r[1])
        dx = dx + r[2]
        return [dx, dx], [dg]
    return _rowwise(fn, [x, dh, dres], [g], [(D, F32), (D, BF16)], [(1, D)], name)


def _loss_fwd_bwd(x3, g, target, name):
    D = x3.shape[1]

    def fn(r, c):
        def row_loss(x, gain):
            err = _rms(x, gain) - r[1]
            return 0.5 * jnp.mean(err * err, axis=-1, keepdims=True)
        lrow, vjp = jax.vjp(row_loss, r[0], c[0])
        dx, dg = vjp(jnp.ones_like(lrow))
        tile = jnp.broadcast_to(jnp.sum(lrow, axis=0, keepdims=True), (SUBLANES, LANES))
        return [dx], [tile, dg]
    return _rowwise(fn, [x3, target], [g], [(D, F32)], [(SUBLANES, LANES), (1, D)], name)


def _ple_bwd(dx3, gp, pp, name):
    D = dx3.shape[1]

    def fn(r, c):
        s = _sigmoid(r[1])
        return [r[0] * r[2] * s * (1.0 - s), r[0] * s], []
    return _rowwise(fn, [dx3, gp, pp], [], [(D, BF16), (D, BF16)], [], name)


def _adamw_math(w, g, m, v):
    nm = ADAM_B1 * m + (1.0 - ADAM_B1) * g
    nv = ADAM_B2 * v + (1.0 - ADAM_B2) * (g * g)
    m_hat = nm / (1.0 - ADAM_B1 ** ADAM_STEP)
    v_hat = nv / (1.0 - ADAM_B2 ** ADAM_STEP)
    return -ADAM_LR * (m_hat / (jnp.sqrt(v_hat) + ADAM_EPS) + ADAM_WD * w), nm, nv


def _adamw(w, g, m, v, name):
    R, C = w.shape
    lanes = -(-C // LANES) * LANES
    bt = _pick(R, max(SUBLANES, (1 << 18) // lanes // SUBLANES * SUBLANES), SUBLANES)

    def body(w_ref, g_ref, m_ref, v_ref, d_ref, nm_ref, nv_ref):
        d_ref[...], nm_ref[...], nv_ref[...] = _adamw_math(w_ref[...], g_ref[...], m_ref[...], v_ref[...])

    spec = pl.BlockSpec((bt, C), lambda i: (i, 0))
    return pl.pallas_call(
        body, name=name, grid=(R // bt,), in_specs=[spec] * 4, out_specs=[spec] * 3,
        out_shape=[jax.ShapeDtypeStruct((R, C), F32)] * 3, compiler_params=_params(("parallel",)),
    )(w, g, m, v)


def _adamw_small(ws, gs, ms, vs):
    n = len(ws)

    def body(*refs):
        for i in range(n):
            d, nm, nv = _adamw_math(refs[i][...], refs[n + i][...], refs[2 * n + i][...], refs[3 * n + i][...])
            refs[4 * n + i][...], refs[5 * n + i][...], refs[6 * n + i][...] = d, nm, nv

    VMEM = pl.BlockSpec(memory_space=pltpu.VMEM)
    shapes = [jax.ShapeDtypeStruct(w.shape, F32) for w in ws]
    outs = pl.pallas_call(body, name="adamw_small", in_specs=[VMEM] * (4 * n), out_specs=[VMEM] * (3 * n),
                          out_shape=shapes * 3)(*ws, *gs, *ms, *vs)
    return outs[:n], outs[n:2 * n], outs[2 * n:]


def _gla_fwd(z_big, z_small, w2h, gbh, Bl, S, D):
    NC, dk, dv, HB = S // CHUNK, D // (2 * GLA_HEADS), D // GLA_HEADS, GLA_HEADS_PER_STEP
    HG = GLA_HEADS // HB
    chains = [(hh, bb) for hh in range(HB) for bb in range(Bl)]
    G = len(chains)
    fn = functools.partial(_gla_chunk, scale=dk ** -0.5)

    def body(q, k, v, z, w2, gb, o_ref, stall_ref, st):
        n, g = pl.program_id(0), pl.program_id(1)

        @pl.when(n == 0)
        def _():
            st[g] = jnp.zeros((G, dv, dk), F32)
        s0 = st[g]
        stall_ref[...] = s0.reshape(HB, Bl, dv, dk)
        qk = lambda r: jnp.stack([r[bb, :, hh * dk:(hh + 1) * dk] for hh, bb in chains])
        o, s_new = fn(qk(q), qk(k), jnp.stack([v[bb, :, hh * dv:(hh + 1) * dv] for hh, bb in chains]),
                      jnp.stack([z[bb] for _, bb in chains]), jnp.stack([w2[hh] for hh, _ in chains]),
                      jnp.stack([gb[hh] for hh, _ in chains]), s0)
        for i, (hh, bb) in enumerate(chains):
            o_ref[bb, :, hh * dv:(hh + 1) * dv] = o[i]
        st[g] = s_new

    return pl.pallas_call(
        body, name="gla_fwd", grid=(NC, HG),
        in_specs=[pl.BlockSpec((Bl, CHUNK, HB * dk), lambda n, g: (0, n, g)),
                  pl.BlockSpec((Bl, CHUNK, HB * dk), lambda n, g: (0, n, HG + g)),
                  pl.BlockSpec((Bl, CHUNK, HB * dv), lambda n, g: (0, n, HG + g)),
                  pl.BlockSpec((Bl, CHUNK, ZS), lambda n, g: (0, n, 0)),
                  pl.BlockSpec((HB, ZS, dk), lambda n, g: (g, 0, 0)),
                  pl.BlockSpec((HB, 1, dk), lambda n, g: (g, 0, 0))],
        out_specs=[pl.BlockSpec((Bl, CHUNK, HB * dv), lambda n, g: (0, n, g)),
                   pl.BlockSpec((HB, Bl, None, dv, dk), lambda n, g: (g, 0, n, 0, 0))],
        out_shape=[jax.ShapeDtypeStruct((Bl, S, D), F32), jax.ShapeDtypeStruct((GLA_HEADS, Bl, NC, dv, dk), F32)],
        scratch_shapes=[pltpu.VMEM((HG, G, dv, dk), F32)],
        compiler_params=_params(("arbitrary", "arbitrary")),
    )(z_big, z_big, z_big, z_small, w2h, gbh)


def _gla_bwd(z_big, z_small, w2h, gbh, st_all, do, Bl, S, D):
    NC, dk, dv, HB = S // CHUNK, D // (2 * GLA_HEADS), D // GLA_HEADS, GLA_HEADS_PER_STEP
    HG = GLA_HEADS // HB
    chains = [(hh, bb) for hh in range(HB) for bb in range(Bl)]
    G = len(chains)
    fn = functools.partial(_gla_chunk, scale=dk ** -0.5)

    def body(q, k, v, z, w2, gb, st0, do_ref, dq_ref, dk_ref, dv_ref, dzs_ref, dw2_ref, dgb_ref, dst):
        n, g = pl.program_id(0), pl.program_id(1)

        @pl.when(n == 0)
        def _():
            dst[g] = jnp.zeros((G, dv, dk), F32)

        @pl.when((n == 0) & (g == 0))
        def _():
            dw2_ref[...] = jnp.zeros_like(dw2_ref)
            dgb_ref[...] = jnp.zeros_like(dgb_ref)

        qk = lambda r: jnp.stack([r[bb, :, hh * dk:(hh + 1) * dk] for hh, bb in chains])
        vv = lambda r: jnp.stack([r[bb, :, hh * dv:(hh + 1) * dv] for hh, bb in chains])
        _, vjp = jax.vjp(fn, qk(q), qk(k), vv(v), jnp.stack([z[bb] for _, bb in chains]),
                         jnp.stack([w2[hh] for hh, _ in chains]), jnp.stack([gb[hh] for hh, _ in chains]),
                         st0[...].reshape(G, dv, dk))
        dq, dkk, dvv, dzs, dw2, dgb, dst0 = vjp((vv(do_ref), dst[g]))
        for i, (hh, bb) in enumerate(chains):
            dq_ref[bb, :, hh * dk:(hh + 1) * dk] = dq[i].astype(dq_ref.dtype)
            dk_ref[bb, :, hh * dk:(hh + 1) * dk] = dkk[i].astype(dk_ref.dtype)
            dv_ref[bb, :, hh * dv:(hh + 1) * dv] = dvv[i].astype(dv_ref.dtype)
            dw2_ref[g * HB + hh] += dw2[i]
            dgb_ref[g * HB + hh] += dgb[i]
        for bb in range(Bl):
            tot = sum(dzs[i] for i, (_, b2) in enumerate(chains) if b2 == bb)

            @pl.when(g == 0)
            def _():
                dzs_ref[bb] = tot

            @pl.when(g > 0)
            def _():
                dzs_ref[bb] += tot
        dst[g] = dst0

    rn = lambda n: NC - 1 - n
    return pl.pallas_call(
        body, name="gla_bwd", grid=(NC, HG),
        in_specs=[pl.BlockSpec((Bl, CHUNK, HB * dk), lambda n, g: (0, rn(n), g)),
                  pl.BlockSpec((Bl, CHUNK, HB * dk), lambda n, g: (0, rn(n), HG + g)),
                  pl.BlockSpec((Bl, CHUNK, HB * dv), lambda n, g: (0, rn(n), HG + g)),
                  pl.BlockSpec((Bl, CHUNK, ZS), lambda n, g: (0, rn(n), 0)),
                  pl.BlockSpec((HB, ZS, dk), lambda n, g: (g, 0, 0)),
                  pl.BlockSpec((HB, 1, dk), lambda n, g: (g, 0, 0)),
                  pl.BlockSpec((HB, Bl, None, dv, dk), lambda n, g: (g, 0, rn(n), 0, 0)),
                  pl.BlockSpec((Bl, CHUNK, HB * dv), lambda n, g: (0, rn(n), g))],
        out_specs=[pl.BlockSpec((Bl, CHUNK, HB * dk), lambda n, g: (0, rn(n), g)),
                   pl.BlockSpec((Bl, CHUNK, HB * dk), lambda n, g: (0, rn(n), g)),
                   pl.BlockSpec((Bl, CHUNK, HB * dv), lambda n, g: (0, rn(n), g)),
                   pl.BlockSpec((Bl, CHUNK, ZS), lambda n, g: (0, rn(n), 0)),
                   pl.BlockSpec((GLA_HEADS, ZS, dk), lambda n, g: (0, 0, 0)),
                   pl.BlockSpec((GLA_HEADS, 1, dk), lambda n, g: (0, 0, 0))],
        out_shape=[jax.ShapeDtypeStruct((Bl, S, D // 2), BF16), jax.ShapeDtypeStruct((Bl, S, D // 2), BF16),
                   jax.ShapeDtypeStruct((Bl, S, D), BF16), jax.ShapeDtypeStruct((Bl, S, ZS), F32),
                   jax.ShapeDtypeStruct((GLA_HEADS, ZS, dk), F32), jax.ShapeDtypeStruct((GLA_HEADS, 1, dk), F32)],
        scratch_shapes=[pltpu.VMEM((HG, G, dv, dk), F32)],
        compiler_params=_params(("arbitrary", "arbitrary")),
    )(z_big, z_big, z_big, z_small, w2h, gbh, st_all, do)


def _conv_fwd(z_big, conv_w, grp, Bl, S, D):
    d = D // DN_HEADS
    l2, scale = grp < 2, (d ** -0.5 if grp == 0 else 1.0)
    x_blk0 = (3 * D + grp * D) // d

    def body(x_ref, w_ref, o_ref):
        wrows = [w_ref[j:j + 1, :] for j in range(DN_CONV)]
        o_ref[...] = _conv_act(x_ref[...], wrows, l2=l2, scale=scale)

    return pl.pallas_call(
        body, name=f"conv_fwd{grp}", grid=(Bl, DN_HEADS),
        in_specs=[pl.BlockSpec((S, d), lambda b, j: (b, x_blk0 + j)),
                  pl.BlockSpec((DN_CONV, d), lambda b, j: (0, grp * DN_HEADS + j))],
        out_specs=pl.BlockSpec((S, d), lambda b, j: (b, j)),
        out_shape=jax.ShapeDtypeStruct((Bl * S, D), F32),
        compiler_params=_params(("parallel", "parallel")),
    )(z_big, conv_w)


def _conv_bwd(z_big, conv_w, dact, grp, Bl, S, D):
    d = D // DN_HEADS
    l2, scale = grp < 2, (d ** -0.5 if grp == 0 else 1.0)
    x_blk0 = (3 * D + grp * D) // d

    def body(x_ref, w_ref, g_ref, dx_ref, dw_ref):
        @pl.when(pl.program_id(1) == 0)
        def _():
            dw_ref[...] = jnp.zeros_like(dw_ref)
        wrows = [w_ref[j:j + 1, :] for j in range(DN_CONV)]
        _, vjp = jax.vjp(lambda x, wr: _conv_act(x, wr, l2=l2, scale=scale), x_ref[...], wrows)
        dx, dwr = vjp(g_ref[...])
        dx_ref[...] = dx.astype(dx_ref.dtype)
        for j in range(DN_CONV):
            dw_ref[j:j + 1, :] += dwr[j]

    return pl.pallas_call(
        body, name=f"conv_bwd{grp}", grid=(DN_HEADS, Bl),
        in_specs=[pl.BlockSpec((S, d), lambda j, b: (b, x_blk0 + j)),
                  pl.BlockSpec((DN_CONV, d), lambda j, b: (0, grp * DN_HEADS + j)),
                  pl.BlockSpec((S, d), lambda j, b: (b, j))],
        out_specs=[pl.BlockSpec((S, d), lambda j, b: (b, j)), pl.BlockSpec((DN_CONV, d), lambda j, b: (0, j))],
        out_shape=[jax.ShapeDtypeStruct((Bl * S, D), BF16), jax.ShapeDtypeStruct((DN_CONV, D), F32)],
        compiler_params=_params(("arbitrary", "arbitrary")),
    )(z_big, conv_w, dact)


def _lane_column(zb, lane, width):
    pick = lax.broadcasted_iota(jnp.int32, zb.shape, 1) == lane
    return jnp.broadcast_to(jnp.sum(jnp.where(pick, zb, 0.0), axis=-1, keepdims=True), (zb.shape[0], width))


def _dn_fwd(qa, ka, va, z_small, alog, dtb, Bl, S, D):
    NC, d, HB = S // CHUNK, D // DN_HEADS, DN_HEADS_PER_STEP
    HG = DN_HEADS // HB
    chains = [(hh, bb) for hh in range(HB) for bb in range(Bl)]
    G = len(chains)

    def body(q, k, v, z, al, dt, o_ref, sall_ref, st):
        n, g = pl.program_id(0), pl.program_id(1)

        @pl.when(n == 0)
        def _():
            st[g] = jnp.zeros((G, d, d), F32)
        tok_in = lambda r: jnp.stack([r[bb, :, hh * d:(hh + 1) * d] for hh, bb in chains])
        head_in = lambda r: jnp.stack([r[hh] for hh, _ in chains])
        gate_in = lambda lane0: jnp.stack([_lane_column(z[bb], lane0 + g * HB + hh, d) for hh, bb in chains])
        s0 = st[g]
        sall_ref[...] = s0.reshape(HB, Bl, d, d)
        o, s_new = _dn_chunk(tok_in(q), tok_in(k), tok_in(v), gate_in(A_LANE), gate_in(B_LANE), head_in(al), head_in(dt), s0)
        for i, (hh, bb) in enumerate(chains):
            o_ref[bb, :, hh * d:(hh + 1) * d] = o[i]
        st[g] = s_new

    tok = pl.BlockSpec((Bl, CHUNK, HB * d), lambda n, g: (0, n, g))
    per_head = pl.BlockSpec((HB, 1, d), lambda n, g: (g, 0, 0))
    return pl.pallas_call(
        body, name="dn_fwd", grid=(NC, HG),
        in_specs=[tok, tok, tok, pl.BlockSpec((Bl, CHUNK, ZS), lambda n, g: (0, n, 0)), per_head, per_head],
        out_specs=[tok, pl.BlockSpec((HB, Bl, None, d, d), lambda n, g: (g, 0, n, 0, 0))],
        out_shape=[jax.ShapeDtypeStruct((Bl, S, D), F32), jax.ShapeDtypeStruct((DN_HEADS, Bl, NC, d, d), F32)],
        scratch_shapes=[pltpu.VMEM((HG, G, d, d), F32)],
        compiler_params=_params(("arbitrary", "arbitrary")),
    )(qa, ka, va, z_small, alog, dtb)


def _dn_bwd(qa, ka, va, z_small, alog, dtb, s_all, do, dzs_gla, Bl, S, D):
    NC, d, HB = S // CHUNK, D // DN_HEADS, DN_HEADS_PER_STEP
    HG = DN_HEADS // HB
    chains = [(hh, bb) for hh in range(HB) for bb in range(Bl)]
    G = len(chains)

    def lanesum(t):
        return jnp.sum(t, axis=-1, keepdims=True)

    def body(q, k, v, z, al, dt, s0_ref, do_ref, dzg_ref, dq_ref, dk_ref, dv_ref, dzs_ref, dal_ref, ddt_ref, dst):
        n, g = pl.program_id(0), pl.program_id(1)

        @pl.when(n == 0)
        def _():
            dst[g] = jnp.zeros((G, d, d), F32)

        @pl.when((n == 0) & (g == 0))
        def _():
            dal_ref[...] = jnp.zeros_like(dal_ref)
            ddt_ref[...] = jnp.zeros_like(ddt_ref)

        tok_in = lambda r: jnp.stack([r[bb, :, hh * d:(hh + 1) * d] for hh, bb in chains])
        head_in = lambda r: jnp.stack([r[hh] for hh, _ in chains])
        gate_in = lambda lane0: jnp.stack([_lane_column(z[bb], lane0 + g * HB + hh, d) for hh, bb in chains])
        _, vjp = jax.vjp(_dn_chunk, tok_in(q), tok_in(k), tok_in(v), gate_in(A_LANE), gate_in(B_LANE), head_in(al),
                         head_in(dt), s0_ref[...].reshape(G, d, d))
        dq, dkk, dvv, da, db, dal, ddt, ds0 = vjp((tok_in(do_ref), dst[g]))
        da, db = lanesum(da), lanesum(db)
        dal = jnp.broadcast_to(lanesum(dal), (G, 1, d))
        ddt = jnp.broadcast_to(lanesum(ddt), (G, 1, d))
        lane = lax.broadcasted_iota(jnp.int32, (CHUNK, ZS), 1)
        for bb in range(Bl):
            part = jnp.zeros((CHUNK, ZS), F32)
            for i, (hh, b2) in enumerate(chains):
                if b2 == bb:
                    h = g * HB + hh
                    part = part + jnp.where(lane == A_LANE + h, da[i], 0.0) + jnp.where(lane == B_LANE + h, db[i], 0.0)

            @pl.when(g == 0)
            def _():
                dzs_ref[bb] = jnp.where(lane < LOWRANK, dzg_ref[bb], 0.0) + part

            @pl.when(g > 0)
            def _():
                dzs_ref[bb] += part
        for i, (hh, bb) in enumerate(chains):
            cols = slice(hh * d, (hh + 1) * d)
            dq_ref[bb, :, cols] = dq[i]
            dk_ref[bb, :, cols] = dkk[i]
            dv_ref[bb, :, cols] = dvv[i]
            dal_ref[g * HB + hh] += dal[i]
            ddt_ref[g * HB + hh] += ddt[i]
        dst[g] = ds0

    rn = lambda n: NC - 1 - n
    tok = pl.BlockSpec((Bl, CHUNK, HB * d), lambda n, g: (0, rn(n), g))
    zsb = pl.BlockSpec((Bl, CHUNK, ZS), lambda n, g: (0, rn(n), 0))
    per_head = pl.BlockSpec((HB, 1, d), lambda n, g: (g, 0, 0))
    all_heads = pl.BlockSpec((DN_HEADS, 1, d), lambda n, g: (0, 0, 0))
    tok_shape = jax.ShapeDtypeStruct((Bl, S, D), F32)
    head_shape = jax.ShapeDtypeStruct((DN_HEADS, 1, d), F32)
    return pl.pallas_call(
        body, name="dn_bwd", grid=(NC, HG),
        in_specs=[tok, tok, tok, zsb, per_head, per_head,
                  pl.BlockSpec((HB, Bl, None, d, d), lambda n, g: (g, 0, rn(n), 0, 0)), tok, zsb],
        out_specs=[tok, tok, tok, zsb, all_heads, all_heads],
        out_shape=[tok_shape, tok_shape, tok_shape, jax.ShapeDtypeStruct((Bl, S, ZS), F32), head_shape, head_shape],
        scratch_shapes=[pltpu.VMEM((HG, G, d, d), F32)],
        compiler_params=_params(("arbitrary", "arbitrary")),
    )(qa, ka, va, z_small, alog, dtb, s_all, do, dzs_gla)


def _merge_specs(D, bt):
    dv, w = D // GLA_HEADS, D // DN_HEADS
    col = lambda off: pl.BlockSpec((bt, dv), lambda i, h: (i, off // dv + h))
    return dv, w, col


def _merge_load(refs, nsub, w):
    return [[r[:, s * w:(s + 1) * w] for s in range(nsub)] for r in refs]


def _merge_fwd(o_gla, o_dn, z_big, gla_norm, dn_norm, D, bt=256):
    T = o_gla.shape[0]
    bt = _pick(T, bt, SUBLANES)
    dv, w, col = _merge_specs(D, bt)
    nsub = dv // w

    def body(og, gg, od, dz, ga, gb, gn, dn, out):
        ogl, ggl, odl, dzl, gal, gbl = _merge_load([og, gg, od, dz, ga, gb], nsub, w)
        gnl = [gn[:, s * w:(s + 1) * w] for s in range(nsub)]
        outs = _merge_math(ogl, ggl, odl, dzl, gal, gbl, gnl, dn[...])
        for s in range(nsub):
            out[:, s * w:(s + 1) * w] = outs[s].astype(out.dtype)

    return pl.pallas_call(
        body, name="merge_fwd", grid=(T // bt, GLA_HEADS),
        in_specs=[col(0), col(2 * D), col(0), col(6 * D), col(7 * D), col(8 * D),
                  pl.BlockSpec((1, dv), lambda i, h: (0, 0)), pl.BlockSpec((1, w), lambda i, h: (0, 0))],
        out_specs=col(0),
        out_shape=jax.ShapeDtypeStruct((T, D), BF16),
        compiler_params=_params(("parallel", "parallel")),
    )(o_gla, z_big, o_dn, z_big, z_big, z_big, gla_norm, dn_norm)


def _merge_bwd(o_gla, o_dn, z_big, gla_norm, dn_norm, dmix, D, bt=256):
    T = o_gla.shape[0]
    bt = _pick(T, bt, SUBLANES)
    dv, w, col = _merge_specs(D, bt)
    nsub = dv // w

    def body(og, gg, od, dz, ga, gb, gn, dn, dm, dog, dgg, dod, ddz, dga, dgb, dgn, ddn):
        @pl.when((pl.program_id(0) == 0) & (pl.program_id(1) == 0))
        def _():
            dgn[...] = jnp.zeros_like(dgn)
            ddn[...] = jnp.zeros_like(ddn)

        ogl, ggl, odl, dzl, gal, gbl, dml = _merge_load([og, gg, od, dz, ga, gb, dm], nsub, w)
        gnl = [gn[:, s * w:(s + 1) * w] for s in range(nsub)]
        _, vjp = jax.vjp(_merge_math, ogl, ggl, odl, dzl, gal, gbl, gnl, dn[...])
        g_og, g_gg, g_od, g_dz, g_ga, g_gb, g_gn, g_dn = vjp(dml)
        for s in range(nsub):
            sl = slice(s * w, (s + 1) * w)
            dog[:, sl] = g_og[s]
            dgg[:, sl] = g_gg[s].astype(dgg.dtype)
            dod[:, sl] = g_od[s]
            ddz[:, sl] = g_dz[s].astype(ddz.dtype)
            dga[:, sl] = g_ga[s].astype(dga.dtype)
            dgb[:, sl] = g_gb[s].astype(dgb.dtype)
            dgn[:, sl] += g_gn[s]
        ddn[...] += g_dn

    f32s, bf16s = jax.ShapeDtypeStruct((T, D), F32), jax.ShapeDtypeStruct((T, D), BF16)
    return pl.pallas_call(
        body, name="merge_bwd", grid=(T // bt, GLA_HEADS),
        in_specs=[col(0), col(2 * D), col(0), col(6 * D), col(7 * D), col(8 * D),
                  pl.BlockSpec((1, dv), lambda i, h: (0, 0)), pl.BlockSpec((1, w), lambda i, h: (0, 0)), col(0)],
        out_specs=[col(0)] * 6 + [pl.BlockSpec((1, dv), lambda i, h: (0, 0)), pl.BlockSpec((1, w), lambda i, h: (0, 0))],
        out_shape=[f32s, bf16s, f32s, bf16s, bf16s, bf16s,
                   jax.ShapeDtypeStruct((1, dv), F32), jax.ShapeDtypeStruct((1, w), F32)],
        compiler_params=_params(("arbitrary", "arbitrary")),
    )(o_gla, z_big, o_dn, z_big, z_big, z_big, gla_norm, dn_norm, dmix)


def _place():
    return lax.axis_index("x"), lax.axis_index("y"), lax.axis_index("c")


def _other_chips(x, y):
    return [(1 - x, y), (x, 1 - y), (1 - x, 1 - y)]


def _rcopy(src, dst, send_sem, recv_sem, dev):
    return pltpu.make_async_remote_copy(src_ref=src, dst_ref=dst, send_sem=send_sem, recv_sem=recv_sem,
                                        device_id=dev, device_id_type=MESH)


ANY = pl.BlockSpec(memory_space=pl.ANY)


def _half_rows(ref, rows, hc, lead=()):
    rh = rows // 2
    return ref.at[(*lead, pl.ds(pl.multiple_of(hc * rh, 16), rh), slice(None))]


def _allgather_weights(shards):
    nw = len(shards)

    def body(*refs):
        srcs, outs, send_sems, recv_sems = refs[:nw], refs[nw:2 * nw], refs[2 * nw], refs[2 * nw + 1]
        x, y, c = _place()
        me, sib = 2 * x + y, (x, y, 1 - c)
        chips = _other_chips(x, y)
        sends, passed = [], []
        for w in range(nw):
            r = shards[w].shape[0]
            for k, (px, py) in enumerate(chips):
                s = 6 * w + k
                cp = _rcopy(_half_rows(srcs[w], r, c), _half_rows(outs[w], r, c, (me,)), send_sems.at[s], recv_sems.at[s],
                            (px, py, c))
                cp.start()
                sends.append(cp)
        for w in range(nw):
            r = shards[w].shape[0]
            for k, (px, py) in enumerate(chips):
                s, pj = 6 * w + k, 2 * px + py
                got = _half_rows(outs[w], r, c, (pj,))
                _rcopy(got, got, send_sems.at[s], recv_sems.at[s], (px, py, c)).wait_recv()
                f = _rcopy(got, got, send_sems.at[s + 3], recv_sems.at[s + 3], sib)
                f.start()
                passed.append(f)
        for w in range(nw):
            r = shards[w].shape[0]
            for k, (px, py) in enumerate(chips):
                got = _half_rows(outs[w], r, 1 - c, (2 * px + py,))
                _rcopy(got, got, send_sems.at[6 * w + k + 3], recv_sems.at[6 * w + k + 3], sib).wait_recv()
        for cp in sends + passed:
            cp.wait_send()

    return pl.pallas_call(
        body, name="allgather_weights", in_specs=[ANY] * nw, out_specs=[ANY] * nw,
        out_shape=[jax.ShapeDtypeStruct((4,) + s.shape, s.dtype) for s in shards],
        scratch_shapes=[pltpu.SemaphoreType.DMA((6 * nw,)), pltpu.SemaphoreType.DMA((6 * nw,))],
    )(*shards)


def _pair_exchange(ps):
    nw = len(ps)

    def body(*refs):
        srcs, outs, send_sems, recv_sems = refs[:nw], refs[nw:2 * nw], refs[2 * nw], refs[2 * nw + 1]
        x, y, c = _place()
        cps = []
        for w in range(nw):
            rh = ps[w].shape[1] // 2
            theirs = srcs[w].at[:, pl.ds(pl.multiple_of((1 - c) * rh, 16), rh), :]
            cp = _rcopy(theirs, outs[w], send_sems.at[w], recv_sems.at[w], (x, y, 1 - c))
            cp.start()
            cps.append(cp)
        for cp in cps:
            cp.wait()

    return pl.pallas_call(
        body, name="grad_pair_exchange", in_specs=[ANY] * nw, out_specs=[ANY] * nw,
        out_shape=[jax.ShapeDtypeStruct((4, p.shape[1] // 2, p.shape[2]), p.dtype) for p in ps],
        scratch_shapes=[pltpu.SemaphoreType.DMA((nw,)), pltpu.SemaphoreType.DMA((nw,))],
    )(*ps)


def _sum_block_rows(rh, c):
    lanes = -(-c // LANES) * LANES
    return _pick(rh, max(16, (3 << 18) // lanes // 16 * 16), 16)


def _pair_sum(p, got, c_idx, name):
    _, R, C = p.shape
    Rh = R // 2
    bt = _sum_block_rows(Rh, C)
    nb = Rh // bt

    def body(c_ref, a, b, of, ob):
        s = a[...] + b[...]
        of[...] = s
        ob[...] = s.astype(BF16)

    spec = pl.BlockSpec((None, bt, C), lambda j, i, c_ref: (j, i, 0))
    return pl.pallas_call(
        body, name=name,
        grid_spec=pltpu.PrefetchScalarGridSpec(
            num_scalar_prefetch=1, grid=(4, nb),
            in_specs=[pl.BlockSpec((None, bt, C), lambda j, i, c_ref: (j, c_ref[0] * nb + i, 0)), spec],
            out_specs=[spec, spec]),
        out_shape=[jax.ShapeDtypeStruct((4, Rh, C), F32), jax.ShapeDtypeStruct((4, Rh, C), BF16)],
        compiler_params=_params(("parallel", "parallel")),
    )(c_idx, p, got)


def _chip_scatter(qbs):
    nw = len(qbs)

    def body(*refs):
        srcs, outs, send_sems, recv_sems = refs[:nw], refs[nw:2 * nw], refs[2 * nw], refs[2 * nw + 1]
        x, y, c = _place()
        cps = [_rcopy(srcs[w].at[2 * px + py], outs[w].at[k], send_sems.at[3 * w + k], recv_sems.at[3 * w + k], (px, py, c))
               for w in range(nw) for k, (px, py) in enumerate(_other_chips(x, y))]
        for cp in cps:
            cp.start()
        for cp in cps:
            cp.wait()

    return pl.pallas_call(
        body, name="grad_chip_scatter", in_specs=[ANY] * nw, out_specs=[ANY] * nw,
        out_shape=[jax.ShapeDtypeStruct((3,) + q.shape[1:], q.dtype) for q in qbs],
        scratch_shapes=[pltpu.SemaphoreType.DMA((3 * nw,)), pltpu.SemaphoreType.DMA((3 * nw,))],
    )(*qbs)


def _final_sum(qf, got, me_idx, name):
    _, Rh, C = qf.shape
    bt = _sum_block_rows(Rh, C)

    def body(me_ref, a, b, o):
        o[...] = ((a[...] + b[0].astype(F32)) + b[1].astype(F32)) + b[2].astype(F32)

    return pl.pallas_call(
        body, name=name,
        grid_spec=pltpu.PrefetchScalarGridSpec(
            num_scalar_prefetch=1, grid=(Rh // bt,),
            in_specs=[pl.BlockSpec((None, bt, C), lambda i, me_ref: (me_ref[0], i, 0)),
                      pl.BlockSpec((3, bt, C), lambda i, me_ref: (0, i, 0))],
            out_specs=pl.BlockSpec((bt, C), lambda i, me_ref: (i, 0))),
        out_shape=jax.ShapeDtypeStruct((Rh, C), F32),
        compiler_params=_params(("parallel",)),
    )(me_idx, qf, got)


def _pair_allgather(halves):
    nw = len(halves)

    def body(*refs):
        srcs, outs, send_sems, recv_sems = refs[:nw], refs[nw:2 * nw], refs[2 * nw], refs[2 * nw + 1]
        x, y, c = _place()
        cps = []
        for w in range(nw):
            r = 2 * halves[w].shape[0]
            cp = _rcopy(srcs[w], _half_rows(outs[w], r, c), send_sems.at[w], recv_sems.at[w], (x, y, 1 - c))
            cp.start()
            cps.append(cp)
        for w in range(nw):
            got = _half_rows(outs[w], 2 * halves[w].shape[0], 1 - c)
            _rcopy(got, got, send_sems.at[w], recv_sems.at[w], (x, y, 1 - c)).wait_recv()
        for cp in cps:
            cp.wait_send()

    return pl.pallas_call(
        body, name="grad_pair_allgather", in_specs=[ANY] * nw, out_specs=[ANY] * nw,
        out_shape=[jax.ShapeDtypeStruct((2 * h.shape[0], h.shape[1]), h.dtype) for h in halves],
        scratch_shapes=[pltpu.SemaphoreType.DMA((nw,)), pltpu.SemaphoreType.DMA((nw,))],
    )(*halves)


def _small_exchange(items, out_shapes, finish, name):
    n = len(items)
    offs, rows = [], 0
    for it in items:
        offs.append(rows)
        rows += -(-it.shape[0] // SUBLANES) * SUBLANES
    width = -(-max(it.shape[1] for it in items) // LANES) * LANES
    VMEM = pl.BlockSpec(memory_space=pltpu.VMEM)

    def body(*refs):
        ins, outs = refs[:n], refs[n:n + len(out_shapes)]
        buf, send_sems, recv_sems = refs[n + len(out_shapes):]
        x, y, c = _place()
        me = 4 * x + 2 * y + c
        flip = lambda v, f: (1 - v) if f else v
        peers = [(flip(x, r >> 2 & 1), flip(y, r >> 1 & 1), flip(c, r & 1)) for r in range(1, 8)]
        buf[me] = jnp.zeros((rows, width), F32)
        for it, off, ref in zip(items, offs, ins):
            buf[me, off:off + it.shape[0], 0:it.shape[1]] = ref[...]
        cps = [_rcopy(buf.at[me], buf.at[me], send_sems.at[k], recv_sems.at[k], dev) for k, dev in enumerate(peers)]
        for cp in cps:
            cp.start()
        for k, (px, py, pc) in enumerate(peers):
            slot = buf.at[4 * px + 2 * py + pc]
            _rcopy(slot, slot, send_sems.at[k], recv_sems.at[k], (px, py, pc)).wait_recv()
        for cp in cps:
            cp.wait_send()
        finish(buf, offs, outs)

    return pl.pallas_call(
        body, name=name, in_specs=[VMEM] * n, out_specs=[VMEM] * len(out_shapes),
        out_shape=[jax.ShapeDtypeStruct(s, F32) for s in out_shapes],
        scratch_shapes=[pltpu.VMEM((8, rows, width), F32), pltpu.SemaphoreType.DMA((7,)), pltpu.SemaphoreType.DMA((7,))],
        compiler_params=pltpu.CompilerParams(vmem_limit_bytes=VMEM_LIMIT_BYTES),
    )(*items)


def _allreduce_small(items, name):
    def finish(buf, offs, outs):
        for it, off, out in zip(items, offs, outs):
            region = lambda d: buf[d, off:off + it.shape[0], 0:it.shape[1]]
            s = region(0)
            for d in range(1, 8):
                s = s + region(d)
            out[...] = s
    return _small_exchange(items, [it.shape for it in items], finish, name)


def _allgather_small_shards(items, name):
    def finish(buf, offs, outs):
        for it, off, out in zip(items, offs, outs):
            r, c = it.shape
            for j in range(4):
                out[:, j * c:(j + 1) * c] = buf[2 * j, off:off + r, 0:c]
    return _small_exchange(items, [(it.shape[0], 4 * it.shape[1]) for it in items], finish, name)


def _split_w_in(w, D):
    pad = jnp.zeros((w.shape[0], ZS - 3 * LOWRANK), w.dtype)
    big = jnp.concatenate([w[:, :3 * D], w[:, 3 * D + 16:6 * D + 16], w[:, 6 * D + 16:7 * D + 16], w[:, 7 * D + 48:]], axis=1)
    small = jnp.concatenate([w[:, 3 * D:3 * D + 16], w[:, 7 * D + 16:7 * D + 48], pad], axis=1)
    return big, small


def _join_w_in(gb, gs, D):
    return jnp.concatenate([gb[:, :3 * D], gs[:, :16], gb[:, 3 * D:6 * D], gb[:, 6 * D:7 * D], gs[:, 16:48],
                            gb[:, 7 * D:9 * D]], axis=1)


def kernel(x, p, g_mix, w_in, gla_w2, gla_b, gla_norm, dn_conv, dn_a_log, dn_dt_bias, dn_norm, w_out, g_mlp, w_up, w_down, g_ple, w_ple_gate, w_ple_proj, g_final, loss_target, m_g_mix, m_w_in, m_gla_w2, m_gla_b, m_gla_norm, m_dn_conv, m_dn_a_log, m_dn_dt_bias, m_dn_norm, m_w_out, m_g_mlp, m_w_up, m_w_down, m_g_ple, m_w_ple_gate, m_w_ple_proj, m_g_final, v_g_mix, v_w_in, v_gla_w2, v_gla_b, v_gla_norm, v_dn_conv, v_dn_a_log, v_dn_dt_bias, v_dn_norm, v_w_out, v_g_mlp, v_w_up, v_w_down, v_g_ple, v_w_ple_gate, v_w_ple_proj, v_g_final):
    wts = dict(zip(WEIGHTS, [g_mix, w_in, gla_w2, gla_b, gla_norm, dn_conv, dn_a_log, dn_dt_bias, dn_norm, w_out, g_mlp,
                             w_up, w_down, g_ple, w_ple_gate, w_ple_proj, g_final]))
    mom = dict(zip(WEIGHTS, [m_g_mix, m_w_in, m_gla_w2, m_gla_b, m_gla_norm, m_dn_conv, m_dn_a_log, m_dn_dt_bias, m_dn_norm,
                             m_w_out, m_g_mlp, m_w_up, m_w_down, m_g_ple, m_w_ple_gate, m_w_ple_proj, m_g_final]))
    var = dict(zip(WEIGHTS, [v_g_mix, v_w_in, v_gla_w2, v_gla_b, v_gla_norm, v_dn_conv, v_dn_a_log, v_dn_dt_bias, v_dn_norm,
                             v_w_out, v_g_mlp, v_w_up, v_w_down, v_g_ple, v_w_ple_gate, v_w_ple_proj, v_g_final]))
    Bl, S, D = x.shape
    T = Bl * S
    PLE = p.shape[-1]
    dn_d, gla_dk = D // DN_HEADS, D // (2 * GLA_HEADS)
    ix, iy, ic = _place()
    j_me = 2 * ix + iy
    as2d = lambda a: a.reshape(a.shape[-2], a.shape[-1]) if a.ndim > 1 else a.reshape(1, -1)
    c_idx, me_idx = ic.reshape(1).astype(jnp.int32), j_me.reshape(1).astype(jnp.int32)

    shard2d = {n: as2d(wts[n]) for n, _ in BIG}
    bf16_shards = [shard2d[n].astype(BF16) for n, _ in BIG]
    gathered = _allgather_weights(bf16_shards)
    slots = {n: lax.dynamic_update_slice(g, s[None], (j_me, 0, 0)) for (n, _), g, s in zip(BIG, gathered, bf16_shards)}
    rows_joined = lambda t: t.reshape(4 * t.shape[1], t.shape[2])
    w_big, w_small = _split_w_in(jnp.swapaxes(slots['w_in'], 0, 1).reshape(D, -1), D)
    w_out_f, w_down_f, w_pg_f = rows_joined(slots['w_out']), rows_joined(slots['w_down']), rows_joined(slots['w_ple_gate'])
    w_up_s, w_pp_s = slots['w_up'], slots['w_ple_proj']

    w2_full, conv_full = _allgather_small_shards([as2d(gla_w2), as2d(dn_conv)], "allgather_small_weights")
    w2pad = jnp.pad(w2_full, ((0, ZS - LOWRANK), (0, 0)))
    w2h = jnp.swapaxes(w2pad.reshape(ZS, GLA_HEADS, gla_dk), 0, 1)
    gbh = gla_b.reshape(GLA_HEADS, 1, gla_dk)
    alog_w = jnp.broadcast_to(dn_a_log.reshape(DN_HEADS, 1, 1), (DN_HEADS, 1, dn_d))
    dtb_w = jnp.broadcast_to(dn_dt_bias.reshape(DN_HEADS, 1, 1), (DN_HEADS, 1, dn_d))

    xt = x.reshape(T, D)
    tgt = loss_target.reshape(T, D)
    pt = p.reshape(T, PLE)
    seq = lambda t: t.reshape(Bl, S, t.shape[-1])
    tok = lambda t: t.reshape(T, t.shape[-1])
    h = _rmsnorm_fwd(xt, g_mix, "rms1_fwd")
    (z_big,) = _matmul(h, w_big, 'nn', [F32], "proj_in")
    (z_small,) = _matmul(h, w_small, 'nn', [F32], "proj_in_narrow")
    o_gla, st_all = _gla_fwd(seq(z_big), seq(z_small), w2h, gbh, Bl, S, D)
    acts = [_conv_fwd(z_big, conv_full, grp, Bl, S, D) for grp in range(3)]
    o_dn, s_all = _dn_fwd(seq(acts[0]), seq(acts[1]), seq(acts[2]), seq(z_small), alog_w, dtb_w, Bl, S, D)
    mixed = _merge_fwd(tok(o_gla), tok(o_dn), z_big, gla_norm, dn_norm, D)
    (x1,) = _matmul(mixed, w_out_f, 'nn', [F32], "proj_out", epilogue=lambda r, e: (e + r,), extras=(xt,), bm=512)
    h2 = _rmsnorm_fwd(x1, g_mlp, "rms2_fwd")
    u, act = _matmul(h2, w_up_s, 'nn', [F32, BF16], "mlp_up", b_slots=True,
                     epilogue=lambda r: (r, jnp.square(jnp.maximum(r, 0.0))))
    (x2,) = _matmul(act, w_down_f, 'nn', [F32], "mlp_down", epilogue=lambda r, e: (e + r,), extras=(x1,), bm=512)
    h3 = _rmsnorm_fwd(x2, g_ple, "rms3_fwd")
    (pp,) = _matmul(pt, w_pp_s, 'nn', [F32], "ple_proj", b_slots=True)
    gp, x3 = _matmul(h3, w_pg_f, 'nn', [F32, F32], "ple_gate",
                     epilogue=lambda r, e, q: (r, e + _sigmoid(r) * q), extras=(x2, pp), bm=512)
    dx3, loss_tile, d_g_final = _loss_fwd_bwd(x3, g_final.reshape(1, D), tgt, "loss")

    d_gp, d_pp = _ple_bwd(dx3, gp, pp, "ple_bwd")
    (g_pp,) = _matmul(pt, d_pp, 'tn', [F32], "ple_proj_dw", out_slots=True)
    (g_pg,) = _matmul(h3, d_gp, 'tn', [F32], "ple_gate_dw")
    (dh3,) = _matmul(d_gp, w_pg_f, 'nt', [F32], "ple_gate_dx")
    dx2, dx2b, d_g_ple = _rmsnorm_bwd_add(x2, g_ple, dh3, dx3, "rms3_bwd")
    (g_down,) = _matmul(act, dx2b, 'tn', [F32], "mlp_down_dw")
    (du,) = _matmul(dx2b, w_down_f, 'nt', [BF16], "mlp_down_dx",
                    epilogue=lambda r, e: (r * 2.0 * jnp.maximum(e, 0.0),), extras=(u,))
    (g_up,) = _matmul(h2, du, 'tn', [F32], "mlp_up_dw", out_slots=True)
    (dh2,) = _matmul(du, w_up_s, 'nt', [F32], "mlp_up_dx", b_slots=True)
    dx1, dx1b, d_g_mlp = _rmsnorm_bwd_add(x1, g_mlp, dh2, dx2, "rms2_bwd")
    (g_out,) = _matmul(mixed, dx1b, 'tn', [F32], "proj_out_dw")
    (dmix,) = _matmul(dx1b, w_out_f, 'nt', [F32], "proj_out_dx")
    d_ogla, d_gg, d_odn, d_dz, d_ga, d_gb, d_gla_norm, d_dn_norm = _merge_bwd(
        tok(o_gla), tok(o_dn), z_big, gla_norm, dn_norm, dmix, D)
    d_q, d_k, d_v, dzs_gla, d_w2h, d_gbh = _gla_bwd(seq(z_big), seq(z_small), w2h, gbh, st_all, seq(d_ogla), Bl, S, D)
    d_qa, d_ka, d_va, d_zs, d_alog_w, d_dtb_w = _dn_bwd(seq(acts[0]), seq(acts[1]), seq(acts[2]), seq(z_small), alog_w,
                                                        dtb_w, s_all, seq(d_odn), dzs_gla, Bl, S, D)
    conv_b = [_conv_bwd(z_big, conv_full, tok(g), grp, Bl, S, D) for grp, g in enumerate([d_qa, d_ka, d_va])]
    dz_big = jnp.concatenate([tok(d_q), tok(d_k), tok(d_v), d_gg, conv_b[0][0], conv_b[1][0], conv_b[2][0], d_dz, d_ga,
                              d_gb], axis=1)
    dz_small = tok(d_zs)
    (d_w_big,) = _matmul(h, dz_big, 'tn', [F32], "proj_in_dw")
    (d_w_small,) = _matmul(h, dz_small, 'tn', [F32], "proj_in_narrow_dw")
    (dh_a,) = _matmul(dz_big, w_big, 'nt', [F32], "proj_in_dx")
    (dh,) = _matmul(dz_small, w_small, 'nt', [F32], "proj_in_narrow_dx", epilogue=lambda r, e: (e + r,), extras=(dh_a,))
    grad_x, _, d_g_mix = _rmsnorm_bwd_add(xt, g_mix, dh, dx1, "rms1_bwd")
    g_in = jnp.swapaxes(_join_w_in(d_w_big, d_w_small, D).reshape(D, 4, -1), 0, 1)

    by_rows = lambda g: g.reshape(4, g.shape[0] // 4, g.shape[1])
    send = [g_in, by_rows(g_out), g_up, by_rows(g_down), by_rows(g_pg), g_pp]
    from_sibling = _pair_exchange(send)
    sums = [_pair_sum(s, f, c_idx, f"grad_pair_sum_{n}") for (n, _), s, f in zip(BIG, send, from_sibling)]
    from_chips = _chip_scatter([b for _, b in sums])
    my_halves = [_final_sum(f, got, me_idx, f"grad_final_sum_{n}") for (n, _), (f, _), got in zip(BIG, sums, from_chips)]
    reduced = {n: lax.dynamic_update_slice(o, hlf, (ic * hlf.shape[0], 0))
               for (n, _), o, hlf in zip(BIG, _pair_allgather(my_halves), my_halves)}

    d_w2 = jnp.swapaxes(d_w2h, 0, 1).reshape(ZS, D // 2)[:LOWRANK]
    small_grads = {'g_mix': d_g_mix, 'gla_w2': d_w2, 'gla_b': d_gbh.reshape(1, D // 2), 'gla_norm': d_gla_norm,
                   'dn_a_log': d_alog_w[:, 0, 0].reshape(1, DN_HEADS), 'dn_dt_bias': d_dtb_w[:, 0, 0].reshape(1, DN_HEADS),
                   'dn_norm': d_dn_norm, 'g_mlp': d_g_mlp, 'g_ple': d_g_ple, 'g_final': d_g_final}
    names = [n for n in SMALL if n != 'dn_conv']
    total = _allreduce_small([small_grads[n] for n in names] + [cb[1] for cb in conv_b] + [loss_tile],
                             "allreduce_small_grads")
    gsmall = dict(zip(names, total[:len(names)]))
    loss = total[-1][0, 0]
    my_cols = lambda g: lax.dynamic_slice_in_dim(g, j_me * (g.shape[1] // 4), g.shape[1] // 4, axis=1)
    gsmall['gla_w2'] = my_cols(gsmall['gla_w2'])
    gsmall['dn_conv'] = my_cols(jnp.concatenate(total[len(names):len(names) + 3], axis=1))

    g_o, d_o, m_o, v_o = {}, {}, {}, {}
    for n, _ in BIG:
        shp = wts[n].shape
        d2, nm2, nv2 = _adamw(shard2d[n], reduced[n], as2d(mom[n]), as2d(var[n]), f"adamw_{n}")
        g_o[n], d_o[n], m_o[n], v_o[n] = reduced[n].reshape(shp), d2.reshape(shp), nm2.reshape(shp), nv2.reshape(shp)
    ds, nms, nvs = _adamw_small([as2d(wts[n]) for n in SMALL], [as2d(gsmall[n]) for n in SMALL],
                                [as2d(mom[n]) for n in SMALL], [as2d(var[n]) for n in SMALL])
    for n, dd, mm, vv in zip(SMALL, ds, nms, nvs):
        shp = wts[n].shape
        g_o[n], d_o[n], m_o[n], v_o[n] = gsmall[n].reshape(shp), dd.reshape(shp), mm.reshape(shp), vv.reshape(shp)

    return (loss, grad_x.reshape(Bl, S, D), *[g_o[n] for n in WEIGHTS], *[d_o[n] for n in WEIGHTS],
            *[m_o[n] for n in WEIGHTS], *[v_o[n] for n in WEIGHTS])
```

```python
import functools

import jax
import jax.numpy as jnp
from jax import lax
from jax.experimental import pallas as pl
from jax.experimental.pallas import tpu as pltpu

F32 = jnp.float32
BF16 = jnp.bfloat16

CHUNK = 64
GLA_HEADS = 4
DN_HEADS = 16
LOWRANK = 16
GLA_TAU = 16.0
DN_CONV = 4
EPS = 1e-6
ZS = 128
A_LANE, B_LANE = LOWRANK, LOWRANK + DN_HEADS
ADAM_LR, ADAM_B1, ADAM_B2, ADAM_EPS, ADAM_WD, ADAM_STEP = 0.001, 0.9, 0.999, 1e-08, 0.01, 10

V7X_VMEM_BYTES = 64 * 1024 * 1024
VMEM_LIMIT_BYTES = V7X_VMEM_BYTES - 8 * 1024 * 1024
LANES = 128
SUBLANES = 8
MESH = pl.DeviceIdType.MESH
DN_HEADS_PER_STEP = 4
GLA_HEADS_PER_STEP = 2

WEIGHTS = ['g_mix', 'w_in', 'gla_w2', 'gla_b', 'gla_norm', 'dn_conv', 'dn_a_log', 'dn_dt_bias', 'dn_norm', 'w_out',
           'g_mlp', 'w_up', 'w_down', 'g_ple', 'w_ple_gate', 'w_ple_proj', 'g_final']
BIG = [('w_in', 1), ('w_out', 0), ('w_up', 1), ('w_down', 0), ('w_ple_gate', 0), ('w_ple_proj', 1)]
SMALL = [n for n in WEIGHTS if n not in dict(BIG)]

_NN, _NT, _TN = 'nn', 'nt', 'tn'


def _params(sem=None):
    return pltpu.CompilerParams(dimension_semantics=sem, vmem_limit_bytes=VMEM_LIMIT_BYTES)


def _dot(a, b, form, precision=None):
    o = a.ndim - 2
    contract = {_NN: ((1 + o,), (o,)), _NT: ((1 + o,), (1 + o,)), _TN: ((o,), (o,))}[form]
    batch = ((0,), (0,)) if o else ((), ())
    return lax.dot_general(a, b, (contract, batch), precision=precision, preferred_element_type=F32)


def _make_mm(cast, precision):
    def raw(a, b, dims):
        return _dot(cast(a), cast(b), dims, precision)

    @jax.custom_vjp
    def nn(a, b):
        return raw(a, b, _NN)
    nn.defvjp(lambda a, b: (raw(a, b, _NN), (a, b)), lambda r, g: (raw(g, r[1], _NT), raw(r[0], g, _TN)))

    @jax.custom_vjp
    def nt(a, b):
        return raw(a, b, _NT)
    nt.defvjp(lambda a, b: (raw(a, b, _NT), (a, b)), lambda r, g: (raw(g, r[1], _NN), raw(g, r[0], _TN)))

    @jax.custom_vjp
    def tn(a, b):
        return raw(a, b, _TN)
    tn.defvjp(lambda a, b: (raw(a, b, _TN), (a, b)), lambda r, g: (raw(r[1], g, _NT), raw(r[0], g, _NN)))
    return nn, nt, tn


_bnn, _bnt, _btn = _make_mm(lambda t: t.astype(BF16), None)
TRI_PRECISION = lax.Precision.HIGH


def _iota2(n, axis):
    return lax.broadcasted_iota(jnp.int32, (n, n), axis)


def _lower(n, strict=False):
    return (_iota2(n, 0) > _iota2(n, 1)) if strict else (_iota2(n, 0) >= _iota2(n, 1))


def _tri_times(tri, x):
    tri = tri.astype(F32)
    if x.ndim == 3:
        tri = jnp.broadcast_to(tri, (x.shape[0],) + tri.shape)
    return _dot(tri, x, _NN, lax.Precision.HIGHEST)


@jax.custom_vjp
def _cumsum_rows(x):
    return _tri_times(_lower(x.shape[-2]), x)


def _cumsum_rows_bwd(_, g):
    n = g.shape[-2]
    return (_tri_times(_iota2(n, 0) <= _iota2(n, 1), g),)


_cumsum_rows.defvjp(lambda x: (_cumsum_rows(x), None), _cumsum_rows_bwd)


def _tri_inv_impl(a):
    n = a.shape[-1]
    eye = (_iota2(n, 0) == _iota2(n, 1)).astype(F32)
    p = eye - a
    ak = a
    k = 2
    while k < n:
        ak = _dot(ak, ak, _NN, TRI_PRECISION)
        p = p + _dot(p, ak, _NN, TRI_PRECISION)
        k *= 2
    return p


@jax.custom_vjp
def _tri_inv(a):
    return _tri_inv_impl(a)


def _tri_inv_fwd(a):
    t = _tri_inv_impl(a)
    return t, t


def _tri_inv_bwd(t, g):
    tg = _dot(t, g, _TN, TRI_PRECISION)
    return (-_dot(tg, t, _NT, TRI_PRECISION),)


_tri_inv.defvjp(_tri_inv_fwd, _tri_inv_bwd)


def _shift_rows(x, s, down):
    n = x.shape[0]
    r = lax.broadcasted_iota(jnp.int32, x.shape, 0)
    if down:
        return jnp.where(r >= s, pltpu.roll(x, s, 0), 0.0)
    return jnp.where(r < n - s, pltpu.roll(x, n - s, 0), 0.0)


def _make_shift(s):
    @jax.custom_vjp
    def f(x):
        return _shift_rows(x, s, True)
    f.defvjp(lambda x: (_shift_rows(x, s, True), None), lambda _, g: (_shift_rows(g, s, False),))
    return f


def _sigmoid(x):
    return jax.nn.sigmoid(x)


def _silu(x):
    return x * jax.nn.sigmoid(x)


def _softplus(x):
    return jnp.maximum(x, 0.0) + jnp.log1p(jnp.exp(-jnp.abs(x)))


def _log_sigmoid(x):
    return -_softplus(-x)


def _rms(x, g):
    return x * lax.rsqrt(jnp.mean(x * x, axis=-1, keepdims=True) + EPS) * g


def _gla_chunk(q, k, v, zs, w2, gb, st, *, scale):
    c = q.shape[-2]
    logf = _log_sigmoid(_bnn(zs, w2) + gb) * (1.0 / GLA_TAU)
    bcum = _cumsum_rows(logf)
    b_last = jnp.sum(logf, axis=-2, keepdims=True)
    q_in = (q * scale) * jnp.exp(bcum)
    k_in = k * jnp.exp(-bcum)
    a = jnp.where(_lower(c), _bnt(q_in, k_in), 0.0)
    o = _bnn(a, v) + _bnt(q_in, st)
    k_dec = k * jnp.exp(b_last - bcum)
    st_new = st * jnp.exp(b_last) + _btn(v, k_dec)
    return o, st_new


def _dn_chunk(q, k, v, aw, bw, alog, dtb, s):
    c = q.shape[-2]
    incl, strict = _lower(c), _lower(c, True)
    g_w = -jnp.exp(alog) * _softplus(aw + dtb)
    beta_w = _sigmoid(bw)
    gcum_w = _cumsum_rows(g_w)
    lane0 = lax.broadcasted_iota(jnp.int32, gcum_w.shape, gcum_w.ndim - 1) == 0
    gcol = jnp.sum(jnp.where(lane0, gcum_w, 0.0), axis=-1, keepdims=True)
    d1 = jnp.broadcast_to(gcol, gcol.shape[:-1] + (c,))
    diff = jnp.where(incl, d1 - jnp.swapaxes(d1, -1, -2), 0.0)
    decay = jnp.where(incl, jnp.exp(diff), 0.0)
    k_beta = k * beta_w
    a = jnp.where(strict, _bnt(k_beta, k) * decay, 0.0)
    t = _tri_inv(a)
    egc = jnp.exp(gcum_w)
    u = _bnn(t, v * beta_w)
    w = _bnn(t, k_beta * egc)
    attn = jnp.where(incl, _bnt(q, k) * decay, 0.0)
    q_dec = q * egc
    g_last = jnp.sum(g_w, axis=-2, keepdims=True)
    k_dec = k * jnp.exp(g_last - gcum_w)
    v_new = u - _bnn(w, s)
    o = _bnn(q_dec, s) + _bnn(attn, v_new)
    s_new = s * jnp.exp(g_last) + _btn(k_dec, v_new)
    return o, s_new


def _conv_act(x, wrows, *, l2, scale):
    taps = len(wrows)
    y = None
    for j in range(taps):
        s = taps - 1 - j
        xs = x if s == 0 else _make_shift(s)(x)
        y = wrows[j] * xs if y is None else y + wrows[j] * xs
    y = _silu(y)
    if l2:
        y = y * lax.rsqrt(jnp.sum(y * y, axis=-1, keepdims=True) + EPS) * scale
    return y


def _merge_math(og, gg, od, dz, ga, gb, gn, dn):
    nsub = len(og)
    dv = nsub * og[0].shape[1]
    ssq = jnp.sum(og[0] * og[0], axis=-1, keepdims=True)
    for s in range(1, nsub):
        ssq = ssq + jnp.sum(og[s] * og[s], axis=-1, keepdims=True)
    r = lax.rsqrt(ssq * (1.0 / dv) + EPS)
    outs = []
    for s in range(nsub):
        a = og[s] * r * gn[s] * _silu(gg[s])
        b = _rms(od[s], dn) * _silu(dz[s])
        outs.append(_sigmoid(ga[s]) * a + _sigmoid(gb[s]) * b)
    return outs


def _pick(n, target, mult):
    best = None
    for d in range(mult, min(n, target) + 1, mult):
        if n % d == 0:
            best = d
    return best if best is not None else n


class _Stage:
    def __init__(self, inputs, out_shapes, n_sems, copies):
        self.inputs, self.out_shapes, self.n_sems, self.copies = list(inputs), list(out_shapes), n_sems, copies

    @property
    def sems(self):
        return [pltpu.SemaphoreType.DMA((self.n_sems,)), pltpu.SemaphoreType.DMA((self.n_sems,))]


def _host_stage(body, stage, n_in, n_out, grid):
    ci, co = len(stage.inputs), len(stage.out_shapes)

    def wrapped(*refs):
        ins, cins = refs[:n_in], refs[n_in:n_in + ci]
        outs, couts = refs[n_in + ci:n_in + ci + n_out], refs[n_in + ci + n_out:n_in + ci + n_out + co]
        scratch, sems = refs[n_in + ci + n_out + co:-2], refs[-2:]
        ids = [pl.program_id(d) for d in range(len(grid))]
        first, last = ids[0] == 0, ids[0] == grid[0] - 1
        for i, g in zip(ids[1:], grid[1:]):
            first, last = first & (i == 0), last & (i == g - 1)

        @pl.when(first)
        def _():
            for cp in stage.copies(cins, couts, *sems):
                cp.start()

        body(*ins, *outs, *scratch)

        @pl.when(last)
        def _():
            for cp in stage.copies(cins, couts, *sems):
                cp.wait()

    return wrapped


def _run_stage(stage, name):
    ci = len(stage.inputs)

    def body(*refs):
        cps = stage.copies(refs[:ci], refs[ci:-2], *refs[-2:])
        for cp in cps:
            cp.start()
        for cp in cps:
            cp.wait()

    return pl.pallas_call(body, name=name, in_specs=[ANY] * ci, out_specs=[ANY] * len(stage.out_shapes),
                          out_shape=stage.out_shapes, scratch_shapes=stage.sems)(*stage.inputs)


def _matmul(a, b, form, out_dtypes, name, epilogue=None, extras=(), bm=1024, bn=1024, bk=2048,
            b_slots=False, out_slots=False, stage=None):
    ns, c = (b.shape[0], b.shape[2]) if b_slots else (1, None)
    b2 = b.shape[1:] if b_slots else b.shape
    if form == 'nn':
        (M, K), (K2, N) = a.shape, (b2[0], b2[1] * ns)
    elif form == 'nt':
        (M, K), (N, K2) = a.shape, (b2[0], b2[1] * ns)
    else:
        (K, M), (K2, N) = a.shape, b2
    assert K == K2 and not (b_slots and form == 'tn'), (a.shape, b.shape, form)
    bm, bn, bk = _pick(M, bm, SUBLANES), _pick(N, bn, LANES), _pick(K, bk, LANES)
    if b_slots:
        bn, bk = (_pick(c, bn, LANES), bk) if form == 'nn' else (bn, _pick(c, bk, LANES))
    if out_slots:
        oc = N // 4
        bn = _pick(oc, bn, LANES)
    nk = K // bk
    a_spec = pl.BlockSpec((bk, bm), lambda i, j, k: (k, i)) if form == 'tn' else pl.BlockSpec((bm, bk), lambda i, j, k: (i, k))
    if b_slots and form == 'nn':
        per = c // bn
        b_spec = pl.BlockSpec((None, bk, bn), lambda i, j, k: (j // per, k, j % per))
    elif b_slots:
        per = c // bk
        b_spec = pl.BlockSpec((None, bn, bk), lambda i, j, k: (k // per, j, k % per))
    elif form == 'nt':
        b_spec = pl.BlockSpec((bn, bk), lambda i, j, k: (j, k))
    else:
        b_spec = pl.BlockSpec((bk, bn), lambda i, j, k: (k, j))
    o_spec = pl.BlockSpec((bm, bn), lambda i, j, k: (i, j))
    if out_slots:
        oper = oc // bn
        out_spec = pl.BlockSpec((None, bm, bn), lambda i, j, k: (j // oper, i, j % oper))
        out_shape = [jax.ShapeDtypeStruct((4, M, oc), d) for d in out_dtypes]
    else:
        out_spec = o_spec
        out_shape = [jax.ShapeDtypeStruct((M, N), d) for d in out_dtypes]
    ne, no = len(extras), len(out_dtypes)

    def finish(r, extra_refs, out_refs):
        outs = (r,) if epilogue is None else epilogue(r, *[e[...] for e in extra_refs])
        for ref, o in zip(out_refs, outs):
            ref[...] = o.astype(ref.dtype)

    def body_one(a_ref, b_ref, *rest):
        finish(_dot(a_ref[...].astype(BF16), b_ref[...].astype(BF16), form), rest[:ne], rest[ne:ne + no])

    def body_acc(a_ref, b_ref, *rest):
        extra_refs, out_refs, acc = rest[:ne], rest[ne:ne + no], rest[ne + no]
        k = pl.program_id(2)
        part = _dot(a_ref[...].astype(BF16), b_ref[...].astype(BF16), form)

        @pl.when(k == 0)
        def _():
            acc[...] = part

        @pl.when((k > 0) & (k < nk - 1))
        def _():
            acc[...] += part

        @pl.when(k == nk - 1)
        def _():
            finish(acc[...] + part, extra_refs, out_refs)

    body = body_one if nk == 1 else body_acc
    grid = (M // bm, N // bn, nk)
    scratch = [] if nk == 1 else [pltpu.VMEM((bm, bn), F32)]
    if stage is None:
        return pl.pallas_call(
            body, name=name, grid=grid, in_specs=[a_spec, b_spec] + [o_spec] * ne,
            out_specs=[out_spec] * no, out_shape=out_shape, scratch_shapes=scratch,
            compiler_params=_params(("parallel", "parallel", "arbitrary")),
        )(a, b, *extras)
    ci, co = len(stage.inputs), len(stage.out_shapes)
    return pl.pallas_call(
        _host_stage(body, stage, 2 + ne, no, grid), name=name, grid=grid,
        in_specs=[a_spec, b_spec] + [o_spec] * ne + [ANY] * ci,
        out_specs=[out_spec] * no + [ANY] * co, out_shape=out_shape + stage.out_shapes,
        scratch_shapes=scratch + stage.sems,
        compiler_params=_params(("arbitrary", "arbitrary", "arbitrary")),
    )(a, b, *extras, *stage.inputs)


def _rowwise(fn, rows, consts, row_outs, acc_outs, name, bt=256):
    T = rows[0].shape[0]
    bt = _pick(T, bt, SUBLANES)
    nr, nc, no, na = len(rows), len(consts), len(row_outs), len(acc_outs)

    def body(*refs):
        r_in, c_in = refs[:nr], refs[nr:nr + nc]
        r_out, a_out = refs[nr + nc:nr + nc + no], refs[nr + nc + no:]
        ro, ao = fn([r[...] for r in r_in], [c[...] for c in c_in])
        for ref, o in zip(r_out, ro):
            ref[...] = o.astype(ref.dtype)
        if na:
            @pl.when(pl.program_id(0) == 0)
            def _():
                for ref in a_out:
                    ref[...] = jnp.zeros_like(ref)
            for ref, o in zip(a_out, ao):
                ref[...] += o

    whole = lambda shp: pl.BlockSpec(shp, lambda i: (0,) * len(shp))
    return pl.pallas_call(
        body, name=name, grid=(T // bt,),
        in_specs=[pl.BlockSpec((bt, r.shape[1]), lambda i: (i, 0)) for r in rows] + [whole(c.shape) for c in consts],
        out_specs=[pl.BlockSpec((bt, w), lambda i: (i, 0)) for w, _ in row_outs] + [whole(s) for s in acc_outs],
        out_shape=[jax.ShapeDtypeStruct((T, w), d) for w, d in row_outs] + [jax.ShapeDtypeStruct(s, F32) for s in acc_outs],
        compiler_params=_params(("arbitrary",)),
    )(*rows, *consts)


def _rmsnorm_fwd(x, g, name):
    return _rowwise(lambda r, c: ([_rms(r[0], c[0])], []), [x], [g], [(x.shape[1], BF16)], [], name)[0]


def _rmsnorm_bwd_add(x, g, dh, dres, name):
    D = x.shape[1]

    def fn(r, c):
        _, vjp = jax.vjp(_rms, r[0], c[0])
        dx, dg = vjp(r[1])
        dx = dx + r[2]
        return [dx, dx], [dg]
    return _rowwise(fn, [x, dh, dres], [g], [(D, F32), (D, BF16)], [(1, D)], name)


def _loss_fwd_bwd(x3, g, target, name):
    D = x3.shape[1]

    def fn(r, c):
        def row_loss(x, gain):
            err = _rms(x, gain) - r[1]
            return 0.5 * jnp.mean(err * err, axis=-1, keepdims=True)
        lrow, vjp = jax.vjp(row_loss, r[0], c[0])
        dx, dg = vjp(jnp.ones_like(lrow))
        tile = jnp.broadcast_to(jnp.sum(lrow, axis=0, keepdims=True), (SUBLANES, LANES))
        return [dx], [tile, dg]
    return _rowwise(fn, [x3, target], [g], [(D, F32)], [(SUBLANES, LANES), (1, D)], name)


def _ple_bwd(dx3, gp, pp, name):
    D = dx3.shape[1]

    def fn(r, c):
        s = _sigmoid(r[1])
        return [r[0] * r[2] * s * (1.0 - s), r[0] * s], []
    return _rowwise(fn, [dx3, gp, pp], [], [(D, BF16), (D, BF16)], [], name)


def _adamw_math(w, g, m, v):
    nm = ADAM_B1 * m + (1.0 - ADAM_B1) * g
    nv = ADAM_B2 * v + (1.0 - ADAM_B2) * (g * g)
    m_hat = nm / (1.0 - ADAM_B1 ** ADAM_STEP)
    v_hat = nv / (1.0 - ADAM_B2 ** ADAM_STEP)
    return -ADAM_LR * (m_hat / (jnp.sqrt(v_hat) + ADAM_EPS) + ADAM_WD * w), nm, nv


def _adamw(w, g, m, v, name):
    R, C = w.shape
    lanes = -(-C // LANES) * LANES
    bt = _pick(R, max(SUBLANES, (1 << 18) // lanes // SUBLANES * SUBLANES), SUBLANES)

    def body(w_ref, g_ref, m_ref, v_ref, d_ref, nm_ref, nv_ref):
        d_ref[...], nm_ref[...], nv_ref[...] = _adamw_math(w_ref[...], g_ref[...], m_ref[...], v_ref[...])

    spec = pl.BlockSpec((bt, C), lambda i: (i, 0))
    return pl.pallas_call(
        body, name=name, grid=(R // bt,), in_specs=[spec] * 4, out_specs=[spec] * 3,
        out_shape=[jax.ShapeDtypeStruct((R, C), F32)] * 3, compiler_params=_params(("parallel",)),
    )(w, g, m, v)


def _adamw_small(ws, gs, ms, vs):
    n = len(ws)

    def body(*refs):
        for i in range(n):
            d, nm, nv = _adamw_math(refs[i][...], refs[n + i][...], refs[2 * n + i][...], refs[3 * n + i][...])
            refs[4 * n + i][...], refs[5 * n + i][...], refs[6 * n + i][...] = d, nm, nv

    VMEM = pl.BlockSpec(memory_space=pltpu.VMEM)
    shapes = [jax.ShapeDtypeStruct(w.shape, F32) for w in ws]
    outs = pl.pallas_call(body, name="adamw_small", in_specs=[VMEM] * (4 * n), out_specs=[VMEM] * (3 * n),
                          out_shape=shapes * 3)(*ws, *gs, *ms, *vs)
    return outs[:n], outs[n:2 * n], outs[2 * n:]


def _gla_fwd(z_big, z_small, w2h, gbh, Bl, S, D):
    NC, dk, dv, HB = S // CHUNK, D // (2 * GLA_HEADS), D // GLA_HEADS, GLA_HEADS_PER_STEP
    HG = GLA_HEADS // HB
    chains = [(hh, bb) for hh in range(HB) for bb in range(Bl)]
    G = len(chains)
    fn = functools.partial(_gla_chunk, scale=dk ** -0.5)

    def body(q, k, v, z, w2, gb, o_ref, stall_ref, st):
        n, g = pl.program_id(0), pl.program_id(1)

        @pl.when(n == 0)
        def _():
            st[g] = jnp.zeros((G, dv, dk), F32)
        s0 = st[g]
        stall_ref[...] = s0.reshape(HB, Bl, dv, dk)
        qk = lambda r: jnp.stack([r[bb, :, hh * dk:(hh + 1) * dk] for hh, bb in chains])
        o, s_new = fn(qk(q), qk(k), jnp.stack([v[bb, :, hh * dv:(hh + 1) * dv] for hh, bb in chains]),
                      jnp.stack([z[bb] for _, bb in chains]), jnp.stack([w2[hh] for hh, _ in chains]),
                      jnp.stack([gb[hh] for hh, _ in chains]), s0)
        for i, (hh, bb) in enumerate(chains):
            o_ref[bb, :, hh * dv:(hh + 1) * dv] = o[i]
        st[g] = s_new

    return pl.pallas_call(
        body, name="gla_fwd", grid=(NC, HG),
        in_specs=[pl.BlockSpec((Bl, CHUNK, HB * dk), lambda n, g: (0, n, g)),
                  pl.BlockSpec((Bl, CHUNK, HB * dk), lambda n, g: (0, n, HG + g)),
                  pl.BlockSpec((Bl, CHUNK, HB * dv), lambda n, g: (0, n, HG + g)),
                  pl.BlockSpec((Bl, CHUNK, ZS), lambda n, g: (0, n, 0)),
                  pl.BlockSpec((HB, ZS, dk), lambda n, g: (g, 0, 0)),
                  pl.BlockSpec((HB, 1, dk), lambda n, g: (g, 0, 0))],
        out_specs=[pl.BlockSpec((Bl, CHUNK, HB * dv), lambda n, g: (0, n, g)),
                   pl.BlockSpec((HB, Bl, None, dv, dk), lambda n, g: (g, 0, n, 0, 0))],
        out_shape=[jax.ShapeDtypeStruct((Bl, S, D), F32), jax.ShapeDtypeStruct((GLA_HEADS, Bl, NC, dv, dk), F32)],
        scratch_shapes=[pltpu.VMEM((HG, G, dv, dk), F32)],
        compiler_params=_params(("arbitrary", "arbitrary")),
    )(z_big, z_big, z_big, z_small, w2h, gbh)


def _gla_bwd(z_big, z_small, w2h, gbh, st_all, do, Bl, S, D):
    NC, dk, dv, HB = S // CHUNK, D // (2 * GLA_HEADS), D // GLA_HEADS, GLA_HEADS_PER_STEP
    HG = GLA_HEADS // HB
    chains = [(hh, bb) for hh in range(HB) for bb in range(Bl)]
    G = len(chains)
    fn = functools.partial(_gla_chunk, scale=dk ** -0.5)

    def body(q, k, v, z, w2, gb, st0, do_ref, dq_ref, dk_ref, dv_ref, dzs_ref, dw2_ref, dgb_ref, dst):
        n, g = pl.program_id(0), pl.program_id(1)

        @pl.when(n == 0)
        def _():
            dst[g] = jnp.zeros((G, dv, dk), F32)

        @pl.when((n == 0) & (g == 0))
        def _():
            dw2_ref[...] = jnp.zeros_like(dw2_ref)
            dgb_ref[...] = jnp.zeros_like(dgb_ref)

        qk = lambda r: jnp.stack([r[bb, :, hh * dk:(hh + 1) * dk] for hh, bb in chains])
        vv = lambda r: jnp.stack([r[bb, :, hh * dv:(hh + 1) * dv] for hh, bb in chains])
        _, vjp = jax.vjp(fn, qk(q), qk(k), vv(v), jnp.stack([z[bb] for _, bb in chains]),
                         jnp.stack([w2[hh] for hh, _ in chains]), jnp.stack([gb[hh] for hh, _ in chains]),
                         st0[...].reshape(G, dv, dk))
        dq, dkk, dvv, dzs, dw2, dgb, dst0 = vjp((vv(do_ref), dst[g]))
        for i, (hh, bb) in enumerate(chains):
            dq_ref[bb, :, hh * dk:(hh + 1) * dk] = dq[i].astype(dq_ref.dtype)
            dk_ref[bb, :, hh * dk:(hh + 1) * dk] = dkk[i].astype(dk_ref.dtype)
            dv_ref[bb, :, hh * dv:(hh + 1) * dv] = dvv[i].astype(dv_ref.dtype)
            dw2_ref[g * HB + hh] += dw2[i]
            dgb_ref[g * HB + hh] += dgb[i]
        for bb in range(Bl):
            tot = sum(dzs[i] for i, (_, b2) in enumerate(chains) if b2 == bb)

            @pl.when(g == 0)
            def _():
                dzs_ref[bb] = tot

            @pl.when(g > 0)
            def _():
                dzs_ref[bb] += tot
        dst[g] = dst0

    rn = lambda n: NC - 1 - n
    return pl.pallas_call(
        body, name="gla_bwd", grid=(NC, HG),
        in_specs=[pl.BlockSpec((Bl, CHUNK, HB * dk), lambda n, g: (0, rn(n), g)),
                  pl.BlockSpec((Bl, CHUNK, HB * dk), lambda n, g: (0, rn(n), HG + g)),
                  pl.BlockSpec((Bl, CHUNK, HB * dv), lambda n, g: (0, rn(n), HG + g)),
                  pl.BlockSpec((Bl, CHUNK, ZS), lambda n, g: (0, rn(n), 0)),
                  pl.BlockSpec((HB, ZS, dk), lambda n, g: (g, 0, 0)),
                  pl.BlockSpec((HB, 1, dk), lambda n, g: (g, 0, 0)),
                  pl.BlockSpec((HB, Bl, None, dv, dk), lambda n, g: (g, 0, rn(n), 0, 0)),
                  pl.BlockSpec((Bl, CHUNK, HB * dv), lambda n, g: (0, rn(n), g))],
        out_specs=[pl.BlockSpec((Bl, CHUNK, HB * dk), lambda n, g: (0, rn(n), g)),
                   pl.BlockSpec((Bl, CHUNK, HB * dk), lambda n, g: (0, rn(n), g)),
                   pl.BlockSpec((Bl, CHUNK, HB * dv), lambda n, g: (0, rn(n), g)),
                   pl.BlockSpec((Bl, CHUNK, ZS), lambda n, g: (0, rn(n), 0)),
                   pl.BlockSpec((GLA_HEADS, ZS, dk), lambda n, g: (0, 0, 0)),
                   pl.BlockSpec((GLA_HEADS, 1, dk), lambda n, g: (0, 0, 0))],
        out_shape=[jax.ShapeDtypeStruct((Bl, S, D // 2), BF16), jax.ShapeDtypeStruct((Bl, S, D // 2), BF16),
                   jax.ShapeDtypeStruct((Bl, S, D), BF16), jax.ShapeDtypeStruct((Bl, S, ZS), F32),
                   jax.ShapeDtypeStruct((GLA_HEADS, ZS, dk), F32), jax.ShapeDtypeStruct((GLA_HEADS, 1, dk), F32)],
        scratch_shapes=[pltpu.VMEM((HG, G, dv, dk), F32)],
        compiler_params=_params(("arbitrary", "arbitrary")),
    )(z_big, z_big, z_big, z_small, w2h, gbh, st_all, do)


def _conv_fwd(z_big, conv_w, grp, Bl, S, D):
    d = D // DN_HEADS
    l2, scale = grp < 2, (d ** -0.5 if grp == 0 else 1.0)
    x_blk0 = (3 * D + grp * D) // d

    def body(x_ref, w_ref, o_ref):
        wrows = [w_ref[j:j + 1, :] for j in range(DN_CONV)]
        o_ref[...] = _conv_act(x_ref[...], wrows, l2=l2, scale=scale)

    return pl.pallas_call(
        body, name=f"conv_fwd{grp}", grid=(Bl, DN_HEADS),
        in_specs=[pl.BlockSpec((S, d), lambda b, j: (b, x_blk0 + j)),
                  pl.BlockSpec((DN_CONV, d), lambda b, j: (0, grp * DN_HEADS + j))],
        out_specs=pl.BlockSpec((S, d), lambda b, j: (b, j)),
        out_shape=jax.ShapeDtypeStruct((Bl * S, D), F32),
        compiler_params=_params(("parallel", "parallel")),
    )(z_big, conv_w)


def _conv_bwd(z_big, conv_w, dact, grp, Bl, S, D):
    d = D // DN_HEADS
    l2, scale = grp < 2, (d ** -0.5 if grp == 0 else 1.0)
    x_blk0 = (3 * D + grp * D) // d

    def body(x_ref, w_ref, g_ref, dx_ref, dw_ref):
        @pl.when(pl.program_id(1) == 0)
        def _():
            dw_ref[...] = jnp.zeros_like(dw_ref)
        wrows = [w_ref[j:j + 1, :] for j in range(DN_CONV)]
        _, vjp = jax.vjp(lambda x, wr: _conv_act(x, wr, l2=l2, scale=scale), x_ref[...], wrows)
        dx, dwr = vjp(g_ref[...])
        dx_ref[...] = dx.astype(dx_ref.dtype)
        for j in range(DN_CONV):
            dw_ref[j:j + 1, :] += dwr[j]

    return pl.pallas_call(
        body, name=f"conv_bwd{grp}", grid=(DN_HEADS, Bl),
        in_specs=[pl.BlockSpec((S, d), lambda j, b: (b, x_blk0 + j)),
                  pl.BlockSpec((DN_CONV, d), lambda j, b: (0, grp * DN_HEADS + j)),
                  pl.BlockSpec((S, d), lambda j, b: (b, j))],
        out_specs=[pl.BlockSpec((S, d), lambda j, b: (b, j)), pl.BlockSpec((DN_CONV, d), lambda j, b: (0, j))],
        out_shape=[jax.ShapeDtypeStruct((Bl * S, D), BF16), jax.ShapeDtypeStruct((DN_CONV, D), F32)],
        compiler_params=_params(("arbitrary", "arbitrary")),
    )(z_big, conv_w, dact)


def _lane_column(zb, lane, width):
    pick = lax.broadcasted_iota(jnp.int32, zb.shape, 1) == lane
    return jnp.broadcast_to(jnp.sum(jnp.where(pick, zb, 0.0), axis=-1, keepdims=True), (zb.shape[0], width))


def _dn_fwd(qa, ka, va, z_small, alog, dtb, Bl, S, D):
    NC, d, HB = S // CHUNK, D // DN_HEADS, DN_HEADS_PER_STEP
    HG = DN_HEADS // HB
    chains = [(hh, bb) for hh in range(HB) for bb in range(Bl)]
    G = len(chains)

    def body(q, k, v, z, al, dt, o_ref, sall_ref, st):
        n, g = pl.program_id(0), pl.program_id(1)

        @pl.when(n == 0)
        def _():
            st[g] = jnp.zeros((G, d, d), F32)
        tok_in = lambda r: jnp.stack([r[bb, :, hh * d:(hh + 1) * d] for hh, bb in chains])
        head_in = lambda r: jnp.stack([r[hh] for hh, _ in chains])
        gate_in = lambda lane0: jnp.stack([_lane_column(z[bb], lane0 + g * HB + hh, d) for hh, bb in chains])
        s0 = st[g]
        sall_ref[...] = s0.reshape(HB, Bl, d, d)
        o, s_new = _dn_chunk(tok_in(q), tok_in(k), tok_in(v), gate_in(A_LANE), gate_in(B_LANE), head_in(al), head_in(dt), s0)
        for i, (hh, bb) in enumerate(chains):
            o_ref[bb, :, hh * d:(hh + 1) * d] = o[i]
        st[g] = s_new

    tok = pl.BlockSpec((Bl, CHUNK, HB * d), lambda n, g: (0, n, g))
    per_head = pl.BlockSpec((HB, 1, d), lambda n, g: (g, 0, 0))
    return pl.pallas_call(
        body, name="dn_fwd", grid=(NC, HG),
        in_specs=[tok, tok, tok, pl.BlockSpec((Bl, CHUNK, ZS), lambda n, g: (0, n, 0)), per_head, per_head],
        out_specs=[tok, pl.BlockSpec((HB, Bl, None, d, d), lambda n, g: (g, 0, n, 0, 0))],
        out_shape=[jax.ShapeDtypeStruct((Bl, S, D), F32), jax.ShapeDtypeStruct((DN_HEADS, Bl, NC, d, d), F32)],
        scratch_shapes=[pltpu.VMEM((HG, G, d, d), F32)],
        compiler_params=_params(("arbitrary", "arbitrary")),
    )(qa, ka, va, z_small, alog, dtb)


def _dn_bwd(qa, ka, va, z_small, alog, dtb, s_all, do, dzs_gla, Bl, S, D):
    NC, d, HB = S // CHUNK, D // DN_HEADS, DN_HEADS_PER_STEP
    HG = DN_HEADS // HB
    chains = [(hh, bb) for hh in range(HB) for bb in range(Bl)]
    G = len(chains)

    def lanesum(t):
        return jnp.sum(t, axis=-1, keepdims=True)

    def body(q, k, v, z, al, dt, s0_ref, do_ref, dzg_ref, dq_ref, dk_ref, dv_ref, dzs_ref, dal_ref, ddt_ref, dst):
        n, g = pl.program_id(0), pl.program_id(1)

        @pl.when(n == 0)
        def _():
            dst[g] = jnp.zeros((G, d, d), F32)

        @pl.when((n == 0) & (g == 0))
        def _():
            dal_ref[...] = jnp.zeros_like(dal_ref)
            ddt_ref[...] = jnp.zeros_like(ddt_ref)

        tok_in = lambda r: jnp.stack([r[bb, :, hh * d:(hh + 1) * d] for hh, bb in chains])
        head_in = lambda r: jnp.stack([r[hh] for hh, _ in chains])
        gate_in = lambda lane0: jnp.stack([_lane_column(z[bb], lane0 + g * HB + hh, d) for hh, bb in chains])
        _, vjp = jax.vjp(_dn_chunk, tok_in(q), tok_in(k), tok_in(v), gate_in(A_LANE), gate_in(B_LANE), head_in(al),
                         head_in(dt), s0_ref[...].reshape(G, d, d))
        dq, dkk, dvv, da, db, dal, ddt, ds0 = vjp((tok_in(do_ref), dst[g]))
        da, db = lanesum(da), lanesum(db)
        dal = jnp.broadcast_to(lanesum(dal), (G, 1, d))
        ddt = jnp.broadcast_to(lanesum(ddt), (G, 1, d))
        lane = lax.broadcasted_iota(jnp.int32, (CHUNK, ZS), 1)
        for bb in range(Bl):
            part = jnp.zeros((CHUNK, ZS), F32)
            for i, (hh, b2) in enumerate(chains):
                if b2 == bb:
                    h = g * HB + hh
                    part = part + jnp.where(lane == A_LANE + h, da[i], 0.0) + jnp.where(lane == B_LANE + h, db[i], 0.0)

            @pl.when(g == 0)
            def _():
                dzs_ref[bb] = jnp.where(lane < LOWRANK, dzg_ref[bb], 0.0) + part

            @pl.when(g > 0)
            def _():
                dzs_ref[bb] += part
        for i, (hh, bb) in enumerate(chains):
            cols = slice(hh * d, (hh + 1) * d)
            dq_ref[bb, :, cols] = dq[i]
            dk_ref[bb, :, cols] = dkk[i]
            dv_ref[bb, :, cols] = dvv[i]
            dal_ref[g * HB + hh] += dal[i]
            ddt_ref[g * HB + hh] += ddt[i]
        dst[g] = ds0

    rn = lambda n: NC - 1 - n
    tok = pl.BlockSpec((Bl, CHUNK, HB * d), lambda n, g: (0, rn(n), g))
    zsb = pl.BlockSpec((Bl, CHUNK, ZS), lambda n, g: (0, rn(n), 0))
    per_head = pl.BlockSpec((HB, 1, d), lambda n, g: (g, 0, 0))
    all_heads = pl.BlockSpec((DN_HEADS, 1, d), lambda n, g: (0, 0, 0))
    tok_shape = jax.ShapeDtypeStruct((Bl, S, D), F32)
    head_shape = jax.ShapeDtypeStruct((DN_HEADS, 1, d), F32)
    return pl.pallas_call(
        body, name="dn_bwd", grid=(NC, HG),
        in_specs=[tok, tok, tok, zsb, per_head, per_head,
                  pl.BlockSpec((HB, Bl, None, d, d), lambda n, g: (g, 0, rn(n), 0, 0)), tok, zsb],
        out_specs=[tok, tok, tok, zsb, all_heads, all_heads],
        out_shape=[tok_shape, tok_shape, tok_shape, jax.ShapeDtypeStruct((Bl, S, ZS), F32), head_shape, head_shape],
        scratch_shapes=[pltpu.VMEM((HG, G, d, d), F32)],
        compiler_params=_params(("arbitrary", "arbitrary")),
    )(qa, ka, va, z_small, alog, dtb, s_all, do, dzs_gla)


def _merge_specs(D, bt):
    dv, w = D // GLA_HEADS, D // DN_HEADS
    col = lambda off: pl.BlockSpec((bt, dv), lambda i, h: (i, off // dv + h))
    return dv, w, col


def _merge_load(refs, nsub, w):
    return [[r[:, s * w:(s + 1) * w] for s in range(nsub)] for r in refs]


def _merge_fwd(o_gla, o_dn, z_big, gla_norm, dn_norm, D, bt=256):
    T = o_gla.shape[0]
    bt = _pick(T, bt, SUBLANES)
    dv, w, col = _merge_specs(D, bt)
    nsub = dv // w

    def body(og, gg, od, dz, ga, gb, gn, dn, out):
        ogl, ggl, odl, dzl, gal, gbl = _merge_load([og, gg, od, dz, ga, gb], nsub, w)
        gnl = [gn[:, s * w:(s + 1) * w] for s in range(nsub)]
        outs = _merge_math(ogl, ggl, odl, dzl, gal, gbl, gnl, dn[...])
        for s in range(nsub):
            out[:, s * w:(s + 1) * w] = outs[s].astype(out.dtype)

    return pl.pallas_call(
        body, name="merge_fwd", grid=(T // bt, GLA_HEADS),
        in_specs=[col(0), col(2 * D), col(0), col(6 * D), col(7 * D), col(8 * D),
                  pl.BlockSpec((1, dv), lambda i, h: (0, 0)), pl.BlockSpec((1, w), lambda i, h: (0, 0))],
        out_specs=col(0),
        out_shape=jax.ShapeDtypeStruct((T, D), BF16),
        compiler_params=_params(("parallel", "parallel")),
    )(o_gla, z_big, o_dn, z_big, z_big, z_big, gla_norm, dn_norm)


def _merge_bwd(o_gla, o_dn, z_big, gla_norm, dn_norm, dmix, D, bt=256):
    T = o_gla.shape[0]
    bt = _pick(T, bt, SUBLANES)
    dv, w, col = _merge_specs(D, bt)
    nsub = dv // w

    def body(og, gg, od, dz, ga, gb, gn, dn, dm, dog, dgg, dod, ddz, dga, dgb, dgn, ddn):
        @pl.when((pl.program_id(0) == 0) & (pl.program_id(1) == 0))
        def _():
            dgn[...] = jnp.zeros_like(dgn)
            ddn[...] = jnp.zeros_like(ddn)

        ogl, ggl, odl, dzl, gal, gbl, dml = _merge_load([og, gg, od, dz, ga, gb, dm], nsub, w)
        gnl = [gn[:, s * w:(s + 1) * w] for s in range(nsub)]
        _, vjp = jax.vjp(_merge_math, ogl, ggl, odl, dzl, gal, gbl, gnl, dn[...])
        g_og, g_gg, g_od, g_dz, g_ga, g_gb, g_gn, g_dn = vjp(dml)
        for s in range(nsub):
            sl = slice(s * w, (s + 1) * w)
            dog[:, sl] = g_og[s]
            dgg[:, sl] = g_gg[s].astype(dgg.dtype)
            dod[:, sl] = g_od[s]
            ddz[:, sl] = g_dz[s].astype(ddz.dtype)
            dga[:, sl] = g_ga[s].astype(dga.dtype)
            dgb[:, sl] = g_gb[s].astype(dgb.dtype)
            dgn[:, sl] += g_gn[s]
        ddn[...] += g_dn

    f32s, bf16s = jax.ShapeDtypeStruct((T, D), F32), jax.ShapeDtypeStruct((T, D), BF16)
    return pl.pallas_call(
        body, name="merge_bwd", grid=(T // bt, GLA_HEADS),
        in_specs=[col(0), col(2 * D), col(0), col(6 * D), col(7 * D), col(8 * D),
                  pl.BlockSpec((1, dv), lambda i, h: (0, 0)), pl.BlockSpec((1, w), lambda i, h: (0, 0)), col(0)],
        out_specs=[col(0)] * 6 + [pl.BlockSpec((1, dv), lambda i, h: (0, 0)), pl.BlockSpec((1, w), lambda i, h: (0, 0))],
        out_shape=[f32s, bf16s, f32s, bf16s, bf16s, bf16s,
                   jax.ShapeDtypeStruct((1, dv), F32), jax.ShapeDtypeStruct((1, w), F32)],
        compiler_params=_params(("arbitrary", "arbitrary")),
    )(o_gla, z_big, o_dn, z_big, z_big, z_big, gla_norm, dn_norm, dmix)


def _place():
    return lax.axis_index("x"), lax.axis_index("y"), lax.axis_index("c")


def _other_chips(x, y):
    return [(1 - x, y), (x, 1 - y), (1 - x, 1 - y)]


def _rcopy(src, dst, send_sem, recv_sem, dev):
    return pltpu.make_async_remote_copy(src_ref=src, dst_ref=dst, send_sem=send_sem, recv_sem=recv_sem,
                                        device_id=dev, device_id_type=MESH)


ANY = pl.BlockSpec(memory_space=pl.ANY)


def _half_rows(ref, rows, hc, lead=()):
    rh = rows // 2
    return ref.at[(*lead, pl.ds(pl.multiple_of(hc * rh, 16), rh), slice(None))]


def _allgather_weights(shards):
    nw = len(shards)

    def body(*refs):
        srcs, outs, send_sems, recv_sems = refs[:nw], refs[nw:2 * nw], refs[2 * nw], refs[2 * nw + 1]
        x, y, c = _place()
        me, sib = 2 * x + y, (x, y, 1 - c)
        chips = _other_chips(x, y)
        sends, passed = [], []
        for w in range(nw):
            r = shards[w].shape[0]
            for k, (px, py) in enumerate(chips):
                s = 6 * w + k
                cp = _rcopy(_half_rows(srcs[w], r, c), _half_rows(outs[w], r, c, (me,)), send_sems.at[s], recv_sems.at[s],
                            (px, py, c))
                cp.start()
                sends.append(cp)
        for w in range(nw):
            r = shards[w].shape[0]
            for k, (px, py) in enumerate(chips):
                s, pj = 6 * w + k, 2 * px + py
                got = _half_rows(outs[w], r, c, (pj,))
                _rcopy(got, got, send_sems.at[s], recv_sems.at[s], (px, py, c)).wait_recv()
                f = _rcopy(got, got, send_sems.at[s + 3], recv_sems.at[s + 3], sib)
                f.start()
                passed.append(f)
        for w in range(nw):
            r = shards[w].shape[0]
            for k, (px, py) in enumerate(chips):
                got = _half_rows(outs[w], r, 1 - c, (2 * px + py,))
                _rcopy(got, got, send_sems.at[6 * w + k + 3], recv_sems.at[6 * w + k + 3], sib).wait_recv()
        for cp in sends + passed:
            cp.wait_send()

    return pl.pallas_call(
        body, name="allgather_weights", in_specs=[ANY] * nw, out_specs=[ANY] * nw,
        out_shape=[jax.ShapeDtypeStruct((4,) + s.shape, s.dtype) for s in shards],
        scratch_shapes=[pltpu.SemaphoreType.DMA((6 * nw,)), pltpu.SemaphoreType.DMA((6 * nw,))],
    )(*shards)


def _pair_exchange(ps):
    nw = len(ps)

    def copies(srcs, outs, send_sems, recv_sems):
        x, y, c = _place()
        cps = []
        for w in range(nw):
            rh = ps[w].shape[1] // 2
            theirs = srcs[w].at[:, pl.ds(pl.multiple_of((1 - c) * rh, 16), rh), :]
            cps.append(_rcopy(theirs, outs[w], send_sems.at[w], recv_sems.at[w], (x, y, 1 - c)))
        return cps

    return _Stage(ps, [jax.ShapeDtypeStruct((4, p.shape[1] // 2, p.shape[2]), p.dtype) for p in ps], nw, copies)


def _sum_block_rows(rh, c):
    lanes = -(-c // LANES) * LANES
    return _pick(rh, max(16, (3 << 18) // lanes // 16 * 16), 16)


def _pair_sum(p, got, c_idx, name):
    _, R, C = p.shape
    Rh = R // 2
    bt = _sum_block_rows(Rh, C)
    nb = Rh // bt

    def body(c_ref, a, b, of, ob):
        s = a[...] + b[...]
        of[...] = s
        ob[...] = s.astype(BF16)

    spec = pl.BlockSpec((None, bt, C), lambda j, i, c_ref: (j, i, 0))
    return pl.pallas_call(
        body, name=name,
        grid_spec=pltpu.PrefetchScalarGridSpec(
            num_scalar_prefetch=1, grid=(4, nb),
            in_specs=[pl.BlockSpec((None, bt, C), lambda j, i, c_ref: (j, c_ref[0] * nb + i, 0)), spec],
            out_specs=[spec, spec]),
        out_shape=[jax.ShapeDtypeStruct((4, Rh, C), F32), jax.ShapeDtypeStruct((4, Rh, C), BF16)],
        compiler_params=_params(("parallel", "parallel")),
    )(c_idx, p, got)


def _chip_scatter(qbs):
    nw = len(qbs)

    def copies(srcs, outs, send_sems, recv_sems):
        x, y, c = _place()
        return [_rcopy(srcs[w].at[2 * px + py], outs[w].at[k], send_sems.at[3 * w + k], recv_sems.at[3 * w + k], (px, py, c))
                for w in range(nw) for k, (px, py) in enumerate(_other_chips(x, y))]

    return _Stage(qbs, [jax.ShapeDtypeStruct((3,) + q.shape[1:], q.dtype) for q in qbs], 3 * nw, copies)


def _final_sum(qf, got, me_idx, name):
    _, Rh, C = qf.shape
    bt = _sum_block_rows(Rh, C)

    def body(me_ref, a, b, o):
        o[...] = ((a[...] + b[0].astype(F32)) + b[1].astype(F32)) + b[2].astype(F32)

    return pl.pallas_call(
        body, name=name,
        grid_spec=pltpu.PrefetchScalarGridSpec(
            num_scalar_prefetch=1, grid=(Rh // bt,),
            in_specs=[pl.BlockSpec((None, bt, C), lambda i, me_ref: (me_ref[0], i, 0)),
                      pl.BlockSpec((3, bt, C), lambda i, me_ref: (0, i, 0))],
            out_specs=pl.BlockSpec((bt, C), lambda i, me_ref: (i, 0))),
        out_shape=jax.ShapeDtypeStruct((Rh, C), F32),
        compiler_params=_params(("parallel",)),
    )(me_idx, qf, got)


def _pair_allgather(halves):
    nw = len(halves)

    def body(*refs):
        srcs, outs, send_sems, recv_sems = refs[:nw], refs[nw:2 * nw], refs[2 * nw], refs[2 * nw + 1]
        x, y, c = _place()
        cps = []
        for w in range(nw):
            r = 2 * halves[w].shape[0]
            cp = _rcopy(srcs[w], _half_rows(outs[w], r, c), send_sems.at[w], recv_sems.at[w], (x, y, 1 - c))
            cp.start()
            cps.append(cp)
        for w in range(nw):
            got = _half_rows(outs[w], 2 * halves[w].shape[0], 1 - c)
            _rcopy(got, got, send_sems.at[w], recv_sems.at[w], (x, y, 1 - c)).wait_recv()
        for cp in cps:
            cp.wait_send()

    return pl.pallas_call(
        body, name="grad_pair_allgather", in_specs=[ANY] * nw, out_specs=[ANY] * nw,
        out_shape=[jax.ShapeDtypeStruct((2 * h.shape[0], h.shape[1]), h.dtype) for h in halves],
        scratch_shapes=[pltpu.SemaphoreType.DMA((nw,)), pltpu.SemaphoreType.DMA((nw,))],
    )(*halves)


def _small_exchange(items, out_shapes, finish, name):
    n = len(items)
    offs, rows = [], 0
    for it in items:
        offs.append(rows)
        rows += -(-it.shape[0] // SUBLANES) * SUBLANES
    width = -(-max(it.shape[1] for it in items) // LANES) * LANES
    VMEM = pl.BlockSpec(memory_space=pltpu.VMEM)

    def body(*refs):
        ins, outs = refs[:n], refs[n:n + len(out_shapes)]
        buf, send_sems, recv_sems = refs[n + len(out_shapes):]
        x, y, c = _place()
        me = 4 * x + 2 * y + c
        flip = lambda v, f: (1 - v) if f else v
        peers = [(flip(x, r >> 2 & 1), flip(y, r >> 1 & 1), flip(c, r & 1)) for r in range(1, 8)]
        buf[me] = jnp.zeros((rows, width), F32)
        for it, off, ref in zip(items, offs, ins):
            buf[me, off:off + it.shape[0], 0:it.shape[1]] = ref[...]
        cps = [_rcopy(buf.at[me], buf.at[me], send_sems.at[k], recv_sems.at[k], dev) for k, dev in enumerate(peers)]
        for cp in cps:
            cp.start()
        for k, (px, py, pc) in enumerate(peers):
            slot = buf.at[4 * px + 2 * py + pc]
            _rcopy(slot, slot, send_sems.at[k], recv_sems.at[k], (px, py, pc)).wait_recv()
        for cp in cps:
            cp.wait_send()
        finish(buf, offs, outs)

    return pl.pallas_call(
        body, name=name, in_specs=[VMEM] * n, out_specs=[VMEM] * len(out_shapes),
        out_shape=[jax.ShapeDtypeStruct(s, F32) for s in out_shapes],
        scratch_shapes=[pltpu.VMEM((8, rows, width), F32), pltpu.SemaphoreType.DMA((7,)), pltpu.SemaphoreType.DMA((7,))],
        compiler_params=pltpu.CompilerParams(vmem_limit_bytes=VMEM_LIMIT_BYTES),
    )(*items)


def _allreduce_small(items, name):
    def finish(buf, offs, outs):
        for it, off, out in zip(items, offs, outs):
            region = lambda d: buf[d, off:off + it.shape[0], 0:it.shape[1]]
            s = region(0)
            for d in range(1, 8):
                s = s + region(d)
            out[...] = s
    return _small_exchange(items, [it.shape for it in items], finish, name)


def _allgather_small_shards(items, name):
    def finish(buf, offs, outs):
        for it, off, out in zip(items, offs, outs):
            r, c = it.shape
            for j in range(4):
                out[:, j * c:(j + 1) * c] = buf[2 * j, off:off + r, 0:c]
    return _small_exchange(items, [(it.shape[0], 4 * it.shape[1]) for it in items], finish, name)


def _split_w_in(w, D):
    pad = jnp.zeros((w.shape[0], ZS - 3 * LOWRANK), w.dtype)
    big = jnp.concatenate([w[:, :3 * D], w[:, 3 * D + 16:6 * D + 16], w[:, 6 * D + 16:7 * D + 16], w[:, 7 * D + 48:]], axis=1)
    small = jnp.concatenate([w[:, 3 * D:3 * D + 16], w[:, 7 * D + 16:7 * D + 48], pad], axis=1)
    return big, small


def _join_w_in(gb, gs, D):
    return jnp.concatenate([gb[:, :3 * D], gs[:, :16], gb[:, 3 * D:6 * D], gb[:, 6 * D:7 * D], gs[:, 16:48],
                            gb[:, 7 * D:9 * D]], axis=1)


def kernel(x, p, g_mix, w_in, gla_w2, gla_b, gla_norm, dn_conv, dn_a_log, dn_dt_bias, dn_norm, w_out, g_mlp, w_up, w_down, g_ple, w_ple_gate, w_ple_proj, g_final, loss_target, m_g_mix, m_w_in, m_gla_w2, m_gla_b, m_gla_norm, m_dn_conv, m_dn_a_log, m_dn_dt_bias, m_dn_norm, m_w_out, m_g_mlp, m_w_up, m_w_down, m_g_ple, m_w_ple_gate, m_w_ple_proj, m_g_final, v_g_mix, v_w_in, v_gla_w2, v_gla_b, v_gla_norm, v_dn_conv, v_dn_a_log, v_dn_dt_bias, v_dn_norm, v_w_out, v_g_mlp, v_w_up, v_w_down, v_g_ple, v_w_ple_gate, v_w_ple_proj, v_g_final):
    wts = dict(zip(WEIGHTS, [g_mix, w_in, gla_w2, gla_b, gla_norm, dn_conv, dn_a_log, dn_dt_bias, dn_norm, w_out, g_mlp,
                             w_up, w_down, g_ple, w_ple_gate, w_ple_proj, g_final]))
    mom = dict(zip(WEIGHTS, [m_g_mix, m_w_in, m_gla_w2, m_gla_b, m_gla_norm, m_dn_conv, m_dn_a_log, m_dn_dt_bias, m_dn_norm,
                             m_w_out, m_g_mlp, m_w_up, m_w_down, m_g_ple, m_w_ple_gate, m_w_ple_proj, m_g_final]))
    var = dict(zip(WEIGHTS, [v_g_mix, v_w_in, v_gla_w2, v_gla_b, v_gla_norm, v_dn_conv, v_dn_a_log, v_dn_dt_bias, v_dn_norm,
                             v_w_out, v_g_mlp, v_w_up, v_w_down, v_g_ple, v_w_ple_gate, v_w_ple_proj, v_g_final]))
    Bl, S, D = x.shape
    T = Bl * S
    PLE = p.shape[-1]
    dn_d, gla_dk = D // DN_HEADS, D // (2 * GLA_HEADS)
    ix, iy, ic = _place()
    j_me = 2 * ix + iy
    as2d = lambda a: a.reshape(a.shape[-2], a.shape[-1]) if a.ndim > 1 else a.reshape(1, -1)
    c_idx, me_idx = ic.reshape(1).astype(jnp.int32), j_me.reshape(1).astype(jnp.int32)

    shard2d = {n: as2d(wts[n]) for n, _ in BIG}
    bf16_shards = [shard2d[n].astype(BF16) for n, _ in BIG]
    gathered = _allgather_weights(bf16_shards)
    slots = {n: lax.dynamic_update_slice(g, s[None], (j_me, 0, 0)) for (n, _), g, s in zip(BIG, gathered, bf16_shards)}
    rows_joined = lambda t: t.reshape(4 * t.shape[1], t.shape[2])
    w_big, w_small = _split_w_in(jnp.swapaxes(slots['w_in'], 0, 1).reshape(D, -1), D)
    w_out_f, w_down_f, w_pg_f = rows_joined(slots['w_out']), rows_joined(slots['w_down']), rows_joined(slots['w_ple_gate'])
    w_up_s, w_pp_s = slots['w_up'], slots['w_ple_proj']

    w2_full, conv_full = _allgather_small_shards([as2d(gla_w2), as2d(dn_conv)], "allgather_small_weights")
    w2pad = jnp.pad(w2_full, ((0, ZS - LOWRANK), (0, 0)))
    w2h = jnp.swapaxes(w2pad.reshape(ZS, GLA_HEADS, gla_dk), 0, 1)
    gbh = gla_b.reshape(GLA_HEADS, 1, gla_dk)
    alog_w = jnp.broadcast_to(dn_a_log.reshape(DN_HEADS, 1, 1), (DN_HEADS, 1, dn_d))
    dtb_w = jnp.broadcast_to(dn_dt_bias.reshape(DN_HEADS, 1, 1), (DN_HEADS, 1, dn_d))

    xt = x.reshape(T, D)
    tgt = loss_target.reshape(T, D)
    pt = p.reshape(T, PLE)
    seq = lambda t: t.reshape(Bl, S, t.shape[-1])
    tok = lambda t: t.reshape(T, t.shape[-1])
    h = _rmsnorm_fwd(xt, g_mix, "rms1_fwd")
    (z_big,) = _matmul(h, w_big, 'nn', [F32], "proj_in")
    (z_small,) = _matmul(h, w_small, 'nn', [F32], "proj_in_narrow")
    o_gla, st_all = _gla_fwd(seq(z_big), seq(z_small), w2h, gbh, Bl, S, D)
    acts = [_conv_fwd(z_big, conv_full, grp, Bl, S, D) for grp in range(3)]
    o_dn, s_all = _dn_fwd(seq(acts[0]), seq(acts[1]), seq(acts[2]), seq(z_small), alog_w, dtb_w, Bl, S, D)
    mixed = _merge_fwd(tok(o_gla), tok(o_dn), z_big, gla_norm, dn_norm, D)
    (x1,) = _matmul(mixed, w_out_f, 'nn', [F32], "proj_out", epilogue=lambda r, e: (e + r,), extras=(xt,), bm=512)
    h2 = _rmsnorm_fwd(x1, g_mlp, "rms2_fwd")
    u, act = _matmul(h2, w_up_s, 'nn', [F32, BF16], "mlp_up", b_slots=True,
                     epilogue=lambda r: (r, jnp.square(jnp.maximum(r, 0.0))))
    (x2,) = _matmul(act, w_down_f, 'nn', [F32], "mlp_down", epilogue=lambda r, e: (e + r,), extras=(x1,), bm=512)
    h3 = _rmsnorm_fwd(x2, g_ple, "rms3_fwd")
    (pp,) = _matmul(pt, w_pp_s, 'nn', [F32], "ple_proj", b_slots=True)
    gp, x3 = _matmul(h3, w_pg_f, 'nn', [F32, F32], "ple_gate",
                     epilogue=lambda r, e, q: (r, e + _sigmoid(r) * q), extras=(x2, pp), bm=512)
    dx3, loss_tile, d_g_final = _loss_fwd_bwd(x3, g_final.reshape(1, D), tgt, "loss")

    d_gp, d_pp = _ple_bwd(dx3, gp, pp, "ple_bwd")
    (g_pp,) = _matmul(pt, d_pp, 'tn', [F32], "ple_proj_dw", out_slots=True)
    (g_pg,) = _matmul(h3, d_gp, 'tn', [F32], "ple_gate_dw")
    (dh3,) = _matmul(d_gp, w_pg_f, 'nt', [F32], "ple_gate_dx")
    dx2, dx2b, d_g_ple = _rmsnorm_bwd_add(x2, g_ple, dh3, dx3, "rms3_bwd")
    (g_down,) = _matmul(act, dx2b, 'tn', [F32], "mlp_down_dw")
    (du,) = _matmul(dx2b, w_down_f, 'nt', [BF16], "mlp_down_dx",
                    epilogue=lambda r, e: (r * 2.0 * jnp.maximum(e, 0.0),), extras=(u,))
    (g_up,) = _matmul(h2, du, 'tn', [F32], "mlp_up_dw", out_slots=True)
    (dh2,) = _matmul(du, w_up_s, 'nt', [F32], "mlp_up_dx", b_slots=True)
    dx1, dx1b, d_g_mlp = _rmsnorm_bwd_add(x1, g_mlp, dh2, dx2, "rms2_bwd")
    (g_out,) = _matmul(mixed, dx1b, 'tn', [F32], "proj_out_dw")
    by_rows = lambda g: g.reshape(4, g.shape[0] // 4, g.shape[1])
    rest = [n for n, _ in BIG[1:]]
    send_rest = [by_rows(g_out), g_up, by_rows(g_down), by_rows(g_pg), g_pp]
    dmix, *sib_rest = _matmul(dx1b, w_out_f, 'nt', [F32], "proj_out_dx", stage=_pair_exchange(send_rest))
    sums_rest = [_pair_sum(s, f, c_idx, f"grad_pair_sum_{n}") for n, s, f in zip(rest, send_rest, sib_rest)]
    d_ogla, d_gg, d_odn, d_dz, d_ga, d_gb, d_gla_norm, d_dn_norm = _merge_bwd(
        tok(o_gla), tok(o_dn), z_big, gla_norm, dn_norm, dmix, D)
    d_q, d_k, d_v, dzs_gla, d_w2h, d_gbh = _gla_bwd(seq(z_big), seq(z_small), w2h, gbh, st_all, seq(d_ogla), Bl, S, D)
    d_qa, d_ka, d_va, d_zs, d_alog_w, d_dtb_w = _dn_bwd(seq(acts[0]), seq(acts[1]), seq(acts[2]), seq(z_small), alog_w,
                                                        dtb_w, s_all, seq(d_odn), dzs_gla, Bl, S, D)
    conv_b = [_conv_bwd(z_big, conv_full, tok(g), grp, Bl, S, D) for grp, g in enumerate([d_qa, d_ka, d_va])]
    dz_big = jnp.concatenate([tok(d_q), tok(d_k), tok(d_v), d_gg, conv_b[0][0], conv_b[1][0], conv_b[2][0], d_dz, d_ga,
                              d_gb], axis=1)
    dz_small = tok(d_zs)
    d_w_big, *chips_rest = _matmul(h, dz_big, 'tn', [F32], "proj_in_dw", stage=_chip_scatter([b for _, b in sums_rest]))
    halves_rest = [_final_sum(f, got, me_idx, f"grad_final_sum_{n}") for n, (f, _), got in zip(rest, sums_rest, chips_rest)]
    (d_w_small,) = _matmul(h, dz_small, 'tn', [F32], "proj_in_narrow_dw")
    g_in = jnp.swapaxes(_join_w_in(d_w_big, d_w_small, D).reshape(D, 4, -1), 0, 1)
    (sib_in,) = _run_stage(_pair_exchange([g_in]), "grad_pair_exchange_w_in")
    sum_in_f32, sum_in_bf16 = _pair_sum(g_in, sib_in, c_idx, "grad_pair_sum_w_in")
    dh_a, chips_in = _matmul(dz_big, w_big, 'nt', [F32], "proj_in_dx", stage=_chip_scatter([sum_in_bf16]))
    half_in = _final_sum(sum_in_f32, chips_in, me_idx, "grad_final_sum_w_in")
    (dh,) = _matmul(dz_small, w_small, 'nt', [F32], "proj_in_narrow_dx", epilogue=lambda r, e: (e + r,), extras=(dh_a,))
    grad_x, _, d_g_mix = _rmsnorm_bwd_add(xt, g_mix, dh, dx1, "rms1_bwd")
    my_halves = [half_in] + halves_rest
    reduced = {n: lax.dynamic_update_slice(o, hlf, (ic * hlf.shape[0], 0))
               for (n, _), o, hlf in zip(BIG, _pair_allgather(my_halves), my_halves)}

    d_w2 = jnp.swapaxes(d_w2h, 0, 1).reshape(ZS, D // 2)[:LOWRANK]
    small_grads = {'g_mix': d_g_mix, 'gla_w2': d_w2, 'gla_b': d_gbh.reshape(1, D // 2), 'gla_norm': d_gla_norm,
                   'dn_a_log': d_alog_w[:, 0, 0].reshape(1, DN_HEADS), 'dn_dt_bias': d_dtb_w[:, 0, 0].reshape(1, DN_HEADS),
                   'dn_norm': d_dn_norm, 'g_mlp': d_g_mlp, 'g_ple': d_g_ple, 'g_final': d_g_final}
    names = [n for n in SMALL if n != 'dn_conv']
    total = _allreduce_small([small_grads[n] for n in names] + [cb[1] for cb in conv_b] + [loss_tile],
                             "allreduce_small_grads")
    gsmall = dict(zip(names, total[:len(names)]))
    loss = total[-1][0, 0]
    my_cols = lambda g: lax.dynamic_slice_in_dim(g, j_me * (g.shape[1] // 4), g.shape[1] // 4, axis=1)
    gsmall['gla_w2'] = my_cols(gsmall['gla_w2'])
    gsmall['dn_conv'] = my_cols(jnp.concatenate(total[len(names):len(names) + 3], axis=1))

    g_o, d_o, m_o, v_o = {}, {}, {}, {}
    for n, _ in BIG:
        shp = wts[n].shape
        d2, nm2, nv2 = _adamw(shard2d[n], reduced[n], as2d(mom[n]), as2d(var[n]), f"adamw_{n}")
        g_o[n], d_o[n], m_o[n], v_o[n] = reduced[n].reshape(shp), d2.reshape(shp), nm2.reshape(shp), nv2.reshape(shp)
    ds, nms, nvs = _adamw_small([as2d(wts[n]) for n in SMALL], [as2d(gsmall[n]) for n in SMALL],
                                [as2d(mom[n]) for n in SMALL], [as2d(var[n]) for n in SMALL])
    for n, dd, mm, vv in zip(SMALL, ds, nms, nvs):
        shp = wts[n].shape
        g_o[n], d_o[n], m_o[n], v_o[n] = gsmall[n].reshape(shp), dd.reshape(shp), mm.reshape(shp), vv.reshape(shp)

    return (loss, grad_x.reshape(Bl, S, D), *[g_o[n] for n in WEIGHTS], *[d_o[n] for n in WEIGHTS],
            *[m_o[n] for n in WEIGHTS], *[v_o[n] for n in WEIGHTS])
```

```python
import functools

import jax
import jax.numpy as jnp
from jax import lax
from jax.experimental import pallas as pl
from jax.experimental.pallas import tpu as pltpu

F32 = jnp.float32
BF16 = jnp.bfloat16

CHUNK = 64
GLA_HEADS = 4
DN_HEADS = 16
LOWRANK = 16
GLA_TAU = 16.0
DN_CONV = 4
EPS = 1e-6
ZS = 128
A_LANE, B_LANE = LOWRANK, LOWRANK + DN_HEADS
ADAM_LR, ADAM_B1, ADAM_B2, ADAM_EPS, ADAM_WD, ADAM_STEP = 0.001, 0.9, 0.999, 1e-08, 0.01, 10

V7X_VMEM_BYTES = 64 * 1024 * 1024
VMEM_LIMIT_BYTES = V7X_VMEM_BYTES - 8 * 1024 * 1024
LANES = 128
SUBLANES = 8
MESH = pl.DeviceIdType.MESH
DN_HEADS_PER_STEP = 4
GLA_HEADS_PER_STEP = 2

WEIGHTS = ['g_mix', 'w_in', 'gla_w2', 'gla_b', 'gla_norm', 'dn_conv', 'dn_a_log', 'dn_dt_bias', 'dn_norm', 'w_out',
           'g_mlp', 'w_up', 'w_down', 'g_ple', 'w_ple_gate', 'w_ple_proj', 'g_final']
BIG = [('w_in', 1), ('w_out', 0), ('w_up', 1), ('w_down', 0), ('w_ple_gate', 0), ('w_ple_proj', 1)]
SMALL = [n for n in WEIGHTS if n not in dict(BIG)]

_NN, _NT, _TN = 'nn', 'nt', 'tn'


def _params(sem=None):
    return pltpu.CompilerParams(dimension_semantics=sem, vmem_limit_bytes=VMEM_LIMIT_BYTES)


def _dot(a, b, form, precision=None):
    o = a.ndim - 2
    contract = {_NN: ((1 + o,), (o,)), _NT: ((1 + o,), (1 + o,)), _TN: ((o,), (o,))}[form]
    batch = ((0,), (0,)) if o else ((), ())
    return lax.dot_general(a, b, (contract, batch), precision=precision, preferred_element_type=F32)


def _make_mm(cast, precision):
    def raw(a, b, dims):
        return _dot(cast(a), cast(b), dims, precision)

    @jax.custom_vjp
    def nn(a, b):
        return raw(a, b, _NN)
    nn.defvjp(lambda a, b: (raw(a, b, _NN), (a, b)), lambda r, g: (raw(g, r[1], _NT), raw(r[0], g, _TN)))

    @jax.custom_vjp
    def nt(a, b):
        return raw(a, b, _NT)
    nt.defvjp(lambda a, b: (raw(a, b, _NT), (a, b)), lambda r, g: (raw(g, r[1], _NN), raw(g, r[0], _TN)))

    @jax.custom_vjp
    def tn(a, b):
        return raw(a, b, _TN)
    tn.defvjp(lambda a, b: (raw(a, b, _TN), (a, b)), lambda r, g: (raw(r[1], g, _NT), raw(r[0], g, _NN)))
    return nn, nt, tn


_bnn, _bnt, _btn = _make_mm(lambda t: t.astype(BF16), None)
TRI_PRECISION = lax.Precision.HIGH


def _iota2(n, axis):
    return lax.broadcasted_iota(jnp.int32, (n, n), axis)


def _lower(n, strict=False):
    return (_iota2(n, 0) > _iota2(n, 1)) if strict else (_iota2(n, 0) >= _iota2(n, 1))


def _tri_times(tri, x):
    tri = tri.astype(F32)
    if x.ndim == 3:
        tri = jnp.broadcast_to(tri, (x.shape[0],) + tri.shape)
    return _dot(tri, x, _NN, lax.Precision.HIGHEST)


@jax.custom_vjp
def _cumsum_rows(x):
    return _tri_times(_lower(x.shape[-2]), x)


def _cumsum_rows_bwd(_, g):
    n = g.shape[-2]
    return (_tri_times(_iota2(n, 0) <= _iota2(n, 1), g),)


_cumsum_rows.defvjp(lambda x: (_cumsum_rows(x), None), _cumsum_rows_bwd)


def _tri_inv_impl(a):
    n = a.shape[-1]
    eye = (_iota2(n, 0) == _iota2(n, 1)).astype(F32)
    p = eye - a
    ak = a
    k = 2
    while k < n:
        prec, cast = (TRI_PRECISION, lambda t: t) if k == 2 else (None, lambda t: t.astype(BF16))
        ak = _dot(cast(ak), cast(ak), _NN, prec)
        p = p + _dot(cast(p), cast(ak), _NN, prec)
        k *= 2
    return p


@jax.custom_vjp
def _tri_inv(a):
    return _tri_inv_impl(a)


def _tri_inv_fwd(a):
    t = _tri_inv_impl(a)
    return t, t


def _tri_inv_bwd(t, g):
    tb = t.astype(BF16)
    tg = _dot(tb, g.astype(BF16), _TN)
    return (-_dot(tg.astype(BF16), tb, _NT),)


_tri_inv.defvjp(_tri_inv_fwd, _tri_inv_bwd)


def _shift_rows(x, s, down):
    n = x.shape[0]
    r = lax.broadcasted_iota(jnp.int32, x.shape, 0)
    if down:
        return jnp.where(r >= s, pltpu.roll(x, s, 0), 0.0)
    return jnp.where(r < n - s, pltpu.roll(x, n - s, 0), 0.0)


def _make_shift(s):
    @jax.custom_vjp
    def f(x):
        return _shift_rows(x, s, True)
    f.defvjp(lambda x: (_shift_rows(x, s, True), None), lambda _, g: (_shift_rows(g, s, False),))
    return f


def _sigmoid(x):
    return jax.nn.sigmoid(x)


def _silu(x):
    return x * jax.nn.sigmoid(x)


def _softplus(x):
    return jnp.maximum(x, 0.0) + jnp.log1p(jnp.exp(-jnp.abs(x)))


def _log_sigmoid(x):
    return -_softplus(-x)


def _rms(x, g):
    return x * lax.rsqrt(jnp.mean(x * x, axis=-1, keepdims=True) + EPS) * g


def _gla_chunk(q, k, v, zs, w2, gb, st, *, scale):
    c = q.shape[-2]
    logf = _log_sigmoid(_bnn(zs, w2) + gb) * (1.0 / GLA_TAU)
    bcum = _cumsum_rows(logf)
    b_last = jnp.sum(logf, axis=-2, keepdims=True)
    q_in = (q * scale) * jnp.exp(bcum)
    k_in = k * jnp.exp(-bcum)
    a = jnp.where(_lower(c), _bnt(q_in, k_in), 0.0)
    o = _bnn(a, v) + _bnt(q_in, st)
    k_dec = k * jnp.exp(b_last - bcum)
    st_new = st * jnp.exp(b_last) + _btn(v, k_dec)
    return o, st_new


def _dn_chunk(q, k, v, aw, bw, alog, dtb, s):
    c = q.shape[-2]
    incl, strict = _lower(c), _lower(c, True)
    g_w = -jnp.exp(alog) * _softplus(aw + dtb)
    beta_w = _sigmoid(bw)
    gcum_w = _cumsum_rows(g_w)
    lane0 = lax.broadcasted_iota(jnp.int32, gcum_w.shape, gcum_w.ndim - 1) == 0
    gcol = jnp.sum(jnp.where(lane0, gcum_w, 0.0), axis=-1, keepdims=True)
    d1 = jnp.broadcast_to(gcol, gcol.shape[:-1] + (c,))
    diff = jnp.where(incl, d1 - jnp.swapaxes(d1, -1, -2), 0.0)
    decay = jnp.where(incl, jnp.exp(diff), 0.0)
    k_beta = k * beta_w
    a = jnp.where(strict, _bnt(k_beta, k) * decay, 0.0)
    t = _tri_inv(a)
    egc = jnp.exp(gcum_w)
    u = _bnn(t, v * beta_w)
    w = _bnn(t, k_beta * egc)
    attn = jnp.where(incl, _bnt(q, k) * decay, 0.0)
    q_dec = q * egc
    g_last = jnp.sum(g_w, axis=-2, keepdims=True)
    k_dec = k * jnp.exp(g_last - gcum_w)
    v_new = u - _bnn(w, s)
    o = _bnn(q_dec, s) + _bnn(attn, v_new)
    s_new = s * jnp.exp(g_last) + _btn(k_dec, v_new)
    return o, s_new


def _conv_act(x, wrows, *, l2, scale):
    taps = len(wrows)
    y = None
    for j in range(taps):
        s = taps - 1 - j
        xs = x if s == 0 else _make_shift(s)(x)
        y = wrows[j] * xs if y is None else y + wrows[j] * xs
    y = _silu(y)
    if l2:
        y = y * lax.rsqrt(jnp.sum(y * y, axis=-1, keepdims=True) + EPS) * scale
    return y


def _merge_math(og, gg, od, dz, ga, gb, gn, dn):
    nsub = len(og)
    dv = nsub * og[0].shape[1]
    ssq = jnp.sum(og[0] * og[0], axis=-1, keepdims=True)
    for s in range(1, nsub):
        ssq = ssq + jnp.sum(og[s] * og[s], axis=-1, keepdims=True)
    r = lax.rsqrt(ssq * (1.0 / dv) + EPS)
    outs = []
    for s in range(nsub):
        a = og[s] * r * gn[s] * _silu(gg[s])
        b = _rms(od[s], dn) * _silu(dz[s])
        outs.append(_sigmoid(ga[s]) * a + _sigmoid(gb[s]) * b)
    return outs


def _pick(n, target, mult):
    best = None
    for d in range(mult, min(n, target) + 1, mult):
        if n % d == 0:
            best = d
    return best if best is not None else n


class _Stage:
    def __init__(self, inputs, out_shapes, n_sems, copies, aliases=None):
        self.inputs, self.out_shapes, self.n_sems, self.copies = list(inputs), list(out_shapes), n_sems, copies
        self.aliases = aliases or {}

    @property
    def sems(self):
        return [pltpu.SemaphoreType.DMA((self.n_sems,)), pltpu.SemaphoreType.DMA((self.n_sems,))]


def _host_stage(body, stage, n_in, n_out, grid):
    ci, co = len(stage.inputs), len(stage.out_shapes)

    def wrapped(*refs):
        ins, cins = refs[:n_in], refs[n_in:n_in + ci]
        outs, couts = refs[n_in + ci:n_in + ci + n_out], refs[n_in + ci + n_out:n_in + ci + n_out + co]
        scratch, sems = refs[n_in + ci + n_out + co:-2], refs[-2:]
        ids = [pl.program_id(d) for d in range(len(grid))]
        first, last = ids[0] == 0, ids[0] == grid[0] - 1
        for i, g in zip(ids[1:], grid[1:]):
            first, last = first & (i == 0), last & (i == g - 1)

        @pl.when(first)
        def _():
            for cp in stage.copies(cins, couts, *sems):
                cp.start()

        body(*ins, *outs, *scratch)

        @pl.when(last)
        def _():
            for cp in stage.copies(cins, couts, *sems):
                cp.wait()

    return wrapped


def _run_stage(stage, name):
    ci = len(stage.inputs)

    def body(*refs):
        cps = stage.copies(refs[:ci], refs[ci:-2], *refs[-2:])
        for cp in cps:
            cp.start()
        for cp in cps:
            cp.wait()

    return pl.pallas_call(body, name=name, in_specs=[ANY] * ci, out_specs=[ANY] * len(stage.out_shapes),
                          out_shape=stage.out_shapes, scratch_shapes=stage.sems,
                          input_output_aliases=dict(stage.aliases))(*stage.inputs)


def _matmul(a, b, form, out_dtypes, name, epilogue=None, extras=(), bm=1024, bn=1024, bk=2048,
            b_slots=False, out_slots=False, stage=None):
    ns, c = (b.shape[0], b.shape[2]) if b_slots else (1, None)
    b2 = b.shape[1:] if b_slots else b.shape
    if form == 'nn':
        (M, K), (K2, N) = a.shape, (b2[0], b2[1] * ns)
    elif form == 'nt':
        (M, K), (N, K2) = a.shape, (b2[0], b2[1] * ns)
    else:
        (K, M), (K2, N) = a.shape, b2
    assert K == K2 and not (b_slots and form == 'tn'), (a.shape, b.shape, form)
    bm, bn, bk = _pick(M, bm, SUBLANES), _pick(N, bn, LANES), _pick(K, bk, LANES)
    if b_slots:
        bn, bk = (_pick(c, bn, LANES), bk) if form == 'nn' else (bn, _pick(c, bk, LANES))
    if out_slots:
        oc = N // 4
        bn = _pick(oc, bn, LANES)
    nk = K // bk
    a_spec = pl.BlockSpec((bk, bm), lambda i, j, k: (k, i)) if form == 'tn' else pl.BlockSpec((bm, bk), lambda i, j, k: (i, k))
    if b_slots and form == 'nn':
        per = c // bn
        b_spec = pl.BlockSpec((None, bk, bn), lambda i, j, k: (j // per, k, j % per))
    elif b_slots:
        per = c // bk
        b_spec = pl.BlockSpec((None, bn, bk), lambda i, j, k: (k // per, j, k % per))
    elif form == 'nt':
        b_spec = pl.BlockSpec((bn, bk), lambda i, j, k: (j, k))
    else:
        b_spec = pl.BlockSpec((bk, bn), lambda i, j, k: (k, j))
    o_spec = pl.BlockSpec((bm, bn), lambda i, j, k: (i, j))
    if out_slots:
        oper = oc // bn
        out_spec = pl.BlockSpec((None, bm, bn), lambda i, j, k: (j // oper, i, j % oper))
        out_shape = [jax.ShapeDtypeStruct((4, M, oc), d) for d in out_dtypes]
    else:
        out_spec = o_spec
        out_shape = [jax.ShapeDtypeStruct((M, N), d) for d in out_dtypes]
    ne, no = len(extras), len(out_dtypes)

    def finish(r, extra_refs, out_refs):
        outs = (r,) if epilogue is None else epilogue(r, *[e[...] for e in extra_refs])
        for ref, o in zip(out_refs, outs):
            ref[...] = o.astype(ref.dtype)

    def body_one(a_ref, b_ref, *rest):
        finish(_dot(a_ref[...].astype(BF16), b_ref[...].astype(BF16), form), rest[:ne], rest[ne:ne + no])

    def body_acc(a_ref, b_ref, *rest):
        extra_refs, out_refs, acc = rest[:ne], rest[ne:ne + no], rest[ne + no]
        k = pl.program_id(2)
        part = _dot(a_ref[...].astype(BF16), b_ref[...].astype(BF16), form)

        @pl.when(k == 0)
        def _():
            acc[...] = part

        @pl.when((k > 0) & (k < nk - 1))
        def _():
            acc[...] += part

        @pl.when(k == nk - 1)
        def _():
            finish(acc[...] + part, extra_refs, out_refs)

    body = body_one if nk == 1 else body_acc
    grid = (M // bm, N // bn, nk)
    scratch = [] if nk == 1 else [pltpu.VMEM((bm, bn), F32)]
    if stage is None:
        return pl.pallas_call(
            body, name=name, grid=grid, in_specs=[a_spec, b_spec] + [o_spec] * ne,
            out_specs=[out_spec] * no, out_shape=out_shape, scratch_shapes=scratch,
            compiler_params=_params(("parallel", "parallel", "arbitrary")),
        )(a, b, *extras)
    ci, co = len(stage.inputs), len(stage.out_shapes)
    return pl.pallas_call(
        _host_stage(body, stage, 2 + ne, no, grid), name=name, grid=grid,
        in_specs=[a_spec, b_spec] + [o_spec] * ne + [ANY] * ci,
        out_specs=[out_spec] * no + [ANY] * co, out_shape=out_shape + stage.out_shapes,
        scratch_shapes=scratch + stage.sems,
        input_output_aliases={2 + ne + i: no + o for i, o in stage.aliases.items()},
        compiler_params=_params(("arbitrary", "arbitrary", "arbitrary")),
    )(a, b, *extras, *stage.inputs)


def _rowwise(fn, rows, consts, row_outs, acc_outs, name, bt=256):
    T = rows[0].shape[0]
    bt = _pick(T, bt, SUBLANES)
    nr, nc, no, na = len(rows), len(consts), len(row_outs), len(acc_outs)

    def body(*refs):
        r_in, c_in = refs[:nr], refs[nr:nr + nc]
        r_out, a_out = refs[nr + nc:nr + nc + no], refs[nr + nc + no:]
        ro, ao = fn([r[...] for r in r_in], [c[...] for c in c_in])
        for ref, o in zip(r_out, ro):
            ref[...] = o.astype(ref.dtype)
        if na:
            @pl.when(pl.program_id(0) == 0)
            def _():
                for ref in a_out:
                    ref[...] = jnp.zeros_like(ref)
            for ref, o in zip(a_out, ao):
                ref[...] += o

    whole = lambda shp: pl.BlockSpec(shp, lambda i: (0,) * len(shp))
    return pl.pallas_call(
        body, name=name, grid=(T // bt,),
        in_specs=[pl.BlockSpec((bt, r.shape[1]), lambda i: (i, 0)) for r in rows] + [whole(c.shape) for c in consts],
        out_specs=[pl.BlockSpec((bt, w), lambda i: (i, 0)) for w, _ in row_outs] + [whole(s) for s in acc_outs],
        out_shape=[jax.ShapeDtypeStruct((T, w), d) for w, d in row_outs] + [jax.ShapeDtypeStruct(s, F32) for s in acc_outs],
        compiler_params=_params(("arbitrary",)),
    )(*rows, *consts)


def _rmsnorm_fwd(x, g, name):
    return _rowwise(lambda r, c: ([_rms(r[0], c[0])], []), [x], [g], [(x.shape[1], BF16)], [], name)[0]


def _rmsnorm_bwd_add(x, g, dh, dres, name):
    D = x.shape[1]

    def fn(r, c):
        _, vjp = jax.vjp(_rms, r[0], c[0])
        dx, dg = vjp(r[1])
        dx = dx + r[2]
        return [dx, dx], [dg]
    return _rowwise(fn, [x, dh, dres], [g], [(D, F32), (D, BF16)], [(1, D)], name)


def _loss_fwd_bwd(x3, g, target, name):
    D = x3.shape[1]

    def fn(r, c):
        def row_loss(x, gain):
            err = _rms(x, gain) - r[1]
            return 0.5 * jnp.mean(err * err, axis=-1, keepdims=True)
        lrow, vjp = jax.vjp(row_loss, r[0], c[0])
        dx, dg = vjp(jnp.ones_like(lrow))
        tile = jnp.broadcast_to(jnp.sum(lrow, axis=0, keepdims=True), (SUBLANES, LANES))
        return [dx], [tile, dg]
    return _rowwise(fn, [x3, target], [g], [(D, F32)], [(SUBLANES, LANES), (1, D)], name)


def _ple_bwd(dx3, gp, pp, name):
    D = dx3.shape[1]

    def fn(r, c):
        s = _sigmoid(r[1])
        return [r[0] * r[2] * s * (1.0 - s), r[0] * s], []
    return _rowwise(fn, [dx3, gp, pp], [], [(D, BF16), (D, BF16)], [], name)


def _adamw_math(w, g, m, v):
    nm = ADAM_B1 * m + (1.0 - ADAM_B1) * g
    nv = ADAM_B2 * v + (1.0 - ADAM_B2) * (g * g)
    m_hat = nm / (1.0 - ADAM_B1 ** ADAM_STEP)
    v_hat = nv / (1.0 - ADAM_B2 ** ADAM_STEP)
    return -ADAM_LR * (m_hat / (jnp.sqrt(v_hat) + ADAM_EPS) + ADAM_WD * w), nm, nv


def _adamw(w, g, m, v, name):
    R, C = w.shape
    lanes = -(-C // LANES) * LANES
    bt = _pick(R, max(SUBLANES, (1 << 18) // lanes // SUBLANES * SUBLANES), SUBLANES)

    def body(w_ref, g_ref, m_ref, v_ref, d_ref, nm_ref, nv_ref):
        d_ref[...], nm_ref[...], nv_ref[...] = _adamw_math(w_ref[...], g_ref[...], m_ref[...], v_ref[...])

    spec = pl.BlockSpec((bt, C), lambda i: (i, 0))
    return pl.pallas_call(
        body, name=name, grid=(R // bt,), in_specs=[spec] * 4, out_specs=[spec] * 3,
        out_shape=[jax.ShapeDtypeStruct((R, C), F32)] * 3, compiler_params=_params(("parallel",)),
    )(w, g, m, v)


def _adamw_small(ws, gs, ms, vs):
    n = len(ws)

    def body(*refs):
        for i in range(n):
            d, nm, nv = _adamw_math(refs[i][...], refs[n + i][...], refs[2 * n + i][...], refs[3 * n + i][...])
            refs[4 * n + i][...], refs[5 * n + i][...], refs[6 * n + i][...] = d, nm, nv

    VMEM = pl.BlockSpec(memory_space=pltpu.VMEM)
    shapes = [jax.ShapeDtypeStruct(w.shape, F32) for w in ws]
    outs = pl.pallas_call(body, name="adamw_small", in_specs=[VMEM] * (4 * n), out_specs=[VMEM] * (3 * n),
                          out_shape=shapes * 3)(*ws, *gs, *ms, *vs)
    return outs[:n], outs[n:2 * n], outs[2 * n:]


def _gla_fwd(z_big, z_small, w2h, gbh, Bl, S, D):
    NC, dk, dv, HB = S // CHUNK, D // (2 * GLA_HEADS), D // GLA_HEADS, GLA_HEADS_PER_STEP
    HG = GLA_HEADS // HB
    chains = [(hh, bb) for hh in range(HB) for bb in range(Bl)]
    G = len(chains)
    fn = functools.partial(_gla_chunk, scale=dk ** -0.5)

    def body(q, k, v, z, w2, gb, o_ref, stall_ref, st):
        n, g = pl.program_id(0), pl.program_id(1)

        @pl.when(n == 0)
        def _():
            st[g] = jnp.zeros((G, dv, dk), F32)
        s0 = st[g]
        stall_ref[...] = s0.reshape(HB, Bl, dv, dk)
        qk = lambda r: jnp.stack([r[bb, :, hh * dk:(hh + 1) * dk] for hh, bb in chains])
        o, s_new = fn(qk(q), qk(k), jnp.stack([v[bb, :, hh * dv:(hh + 1) * dv] for hh, bb in chains]),
                      jnp.stack([z[bb] for _, bb in chains]), jnp.stack([w2[hh] for hh, _ in chains]),
                      jnp.stack([gb[hh] for hh, _ in chains]), s0)
        for i, (hh, bb) in enumerate(chains):
            o_ref[bb, :, hh * dv:(hh + 1) * dv] = o[i]
        st[g] = s_new

    return pl.pallas_call(
        body, name="gla_fwd", grid=(NC, HG),
        in_specs=[pl.BlockSpec((Bl, CHUNK, HB * dk), lambda n, g: (0, n, g)),
                  pl.BlockSpec((Bl, CHUNK, HB * dk), lambda n, g: (0, n, HG + g)),
                  pl.BlockSpec((Bl, CHUNK, HB * dv), lambda n, g: (0, n, HG + g)),
                  pl.BlockSpec((Bl, CHUNK, ZS), lambda n, g: (0, n, 0)),
                  pl.BlockSpec((HB, ZS, dk), lambda n, g: (g, 0, 0)),
                  pl.BlockSpec((HB, 1, dk), lambda n, g: (g, 0, 0))],
        out_specs=[pl.BlockSpec((Bl, CHUNK, HB * dv), lambda n, g: (0, n, g)),
                   pl.BlockSpec((HB, Bl, None, dv, dk), lambda n, g: (g, 0, n, 0, 0))],
        out_shape=[jax.ShapeDtypeStruct((Bl, S, D), F32), jax.ShapeDtypeStruct((GLA_HEADS, Bl, NC, dv, dk), F32)],
        scratch_shapes=[pltpu.VMEM((HG, G, dv, dk), F32)],
        compiler_params=_params(("arbitrary", "arbitrary")),
    )(z_big, z_big, z_big, z_small, w2h, gbh)


def _gla_bwd(z_big, z_small, w2h, gbh, st_all, do, Bl, S, D):
    NC, dk, dv, HB = S // CHUNK, D // (2 * GLA_HEADS), D // GLA_HEADS, GLA_HEADS_PER_STEP
    HG = GLA_HEADS // HB
    chains = [(hh, bb) for hh in range(HB) for bb in range(Bl)]
    G = len(chains)
    fn = functools.partial(_gla_chunk, scale=dk ** -0.5)

    def body(q, k, v, z, w2, gb, st0, do_ref, dq_ref, dk_ref, dv_ref, dzs_ref, dw2_ref, dgb_ref, dst):
        n, g = pl.program_id(0), pl.program_id(1)

        @pl.when(n == 0)
        def _():
            dst[g] = jnp.zeros((G, dv, dk), F32)

        @pl.when((n == 0) & (g == 0))
        def _():
            dw2_ref[...] = jnp.zeros_like(dw2_ref)
            dgb_ref[...] = jnp.zeros_like(dgb_ref)

        qk = lambda r: jnp.stack([r[bb, :, hh * dk:(hh + 1) * dk] for hh, bb in chains])
        vv = lambda r: jnp.stack([r[bb, :, hh * dv:(hh + 1) * dv] for hh, bb in chains])
        _, vjp = jax.vjp(fn, qk(q), qk(k), vv(v), jnp.stack([z[bb] for _, bb in chains]),
                         jnp.stack([w2[hh] for hh, _ in chains]), jnp.stack([gb[hh] for hh, _ in chains]),
                         st0[...].reshape(G, dv, dk))
        dq, dkk, dvv, dzs, dw2, dgb, dst0 = vjp((vv(do_ref), dst[g]))
        for i, (hh, bb) in enumerate(chains):
            dq_ref[bb, :, hh * dk:(hh + 1) * dk] = dq[i].astype(dq_ref.dtype)
            dk_ref[bb, :, hh * dk:(hh + 1) * dk] = dkk[i].astype(dk_ref.dtype)
            dv_ref[bb, :, hh * dv:(hh + 1) * dv] = dvv[i].astype(dv_ref.dtype)
            dw2_ref[g * HB + hh] += dw2[i]
            dgb_ref[g * HB + hh] += dgb[i]
        for bb in range(Bl):
            tot = sum(dzs[i] for i, (_, b2) in enumerate(chains) if b2 == bb)

            @pl.when(g == 0)
            def _():
                dzs_ref[bb] = tot

            @pl.when(g > 0)
            def _():
                dzs_ref[bb] += tot
        dst[g] = dst0

    rn = lambda n: NC - 1 - n
    return pl.pallas_call(
        body, name="gla_bwd", grid=(NC, HG),
        in_specs=[pl.BlockSpec((Bl, CHUNK, HB * dk), lambda n, g: (0, rn(n), g)),
                  pl.BlockSpec((Bl, CHUNK, HB * dk), lambda n, g: (0, rn(n), HG + g)),
                  pl.BlockSpec((Bl, CHUNK, HB * dv), lambda n, g: (0, rn(n), HG + g)),
                  pl.BlockSpec((Bl, CHUNK, ZS), lambda n, g: (0, rn(n), 0)),
                  pl.BlockSpec((HB, ZS, dk), lambda n, g: (g, 0, 0)),
                  pl.BlockSpec((HB, 1, dk), lambda n, g: (g, 0, 0)),
                  pl.BlockSpec((HB, Bl, None, dv, dk), lambda n, g: (g, 0, rn(n), 0, 0)),
                  pl.BlockSpec((Bl, CHUNK, HB * dv), lambda n, g: (0, rn(n), g))],
        out_specs=[pl.BlockSpec((Bl, CHUNK, HB * dk), lambda n, g: (0, rn(n), g)),
                   pl.BlockSpec((Bl, CHUNK, HB * dk), lambda n, g: (0, rn(n), g)),
                   pl.BlockSpec((Bl, CHUNK, HB * dv), lambda n, g: (0, rn(n), g)),
                   pl.BlockSpec((Bl, CHUNK, ZS), lambda n, g: (0, rn(n), 0)),
                   pl.BlockSpec((GLA_HEADS, ZS, dk), lambda n, g: (0, 0, 0)),
                   pl.BlockSpec((GLA_HEADS, 1, dk), lambda n, g: (0, 0, 0))],
        out_shape=[jax.ShapeDtypeStruct((Bl, S, D // 2), BF16), jax.ShapeDtypeStruct((Bl, S, D // 2), BF16),
                   jax.ShapeDtypeStruct((Bl, S, D), BF16), jax.ShapeDtypeStruct((Bl, S, ZS), F32),
                   jax.ShapeDtypeStruct((GLA_HEADS, ZS, dk), F32), jax.ShapeDtypeStruct((GLA_HEADS, 1, dk), F32)],
        scratch_shapes=[pltpu.VMEM((HG, G, dv, dk), F32)],
        compiler_params=_params(("arbitrary", "arbitrary")),
    )(z_big, z_big, z_big, z_small, w2h, gbh, st_all, do)


def _conv_fwd(z_big, conv_w, grp, Bl, S, D):
    d = D // DN_HEADS
    l2, scale = grp < 2, (d ** -0.5 if grp == 0 else 1.0)
    x_blk0 = (3 * D + grp * D) // d

    def body(x_ref, w_ref, o_ref):
        wrows = [w_ref[j:j + 1, :] for j in range(DN_CONV)]
        o_ref[...] = _conv_act(x_ref[...], wrows, l2=l2, scale=scale)

    return pl.pallas_call(
        body, name=f"conv_fwd{grp}", grid=(Bl, DN_HEADS),
        in_specs=[pl.BlockSpec((S, d), lambda b, j: (b, x_blk0 + j)),
                  pl.BlockSpec((DN_CONV, d), lambda b, j: (0, grp * DN_HEADS + j))],
        out_specs=pl.BlockSpec((S, d), lambda b, j: (b, j)),
        out_shape=jax.ShapeDtypeStruct((Bl * S, D), F32),
        compiler_params=_params(("parallel", "parallel")),
    )(z_big, conv_w)


def _conv_bwd(z_big, conv_w, dact, grp, Bl, S, D):
    d = D // DN_HEADS
    l2, scale = grp < 2, (d ** -0.5 if grp == 0 else 1.0)
    x_blk0 = (3 * D + grp * D) // d

    def body(x_ref, w_ref, g_ref, dx_ref, dw_ref):
        @pl.when(pl.program_id(1) == 0)
        def _():
            dw_ref[...] = jnp.zeros_like(dw_ref)
        wrows = [w_ref[j:j + 1, :] for j in range(DN_CONV)]
        _, vjp = jax.vjp(lambda x, wr: _conv_act(x, wr, l2=l2, scale=scale), x_ref[...], wrows)
        dx, dwr = vjp(g_ref[...])
        dx_ref[...] = dx.astype(dx_ref.dtype)
        for j in range(DN_CONV):
            dw_ref[j:j + 1, :] += dwr[j]

    return pl.pallas_call(
        body, name=f"conv_bwd{grp}", grid=(DN_HEADS, Bl),
        in_specs=[pl.BlockSpec((S, d), lambda j, b: (b, x_blk0 + j)),
                  pl.BlockSpec((DN_CONV, d), lambda j, b: (0, grp * DN_HEADS + j)),
                  pl.BlockSpec((S, d), lambda j, b: (b, j))],
        out_specs=[pl.BlockSpec((S, d), lambda j, b: (b, j)), pl.BlockSpec((DN_CONV, d), lambda j, b: (0, j))],
        out_shape=[jax.ShapeDtypeStruct((Bl * S, D), BF16), jax.ShapeDtypeStruct((DN_CONV, D), F32)],
        compiler_params=_params(("arbitrary", "arbitrary")),
    )(z_big, conv_w, dact)


def _lane_column(zb, lane, width):
    pick = lax.broadcasted_iota(jnp.int32, zb.shape, 1) == lane
    return jnp.broadcast_to(jnp.sum(jnp.where(pick, zb, 0.0), axis=-1, keepdims=True), (zb.shape[0], width))


def _dn_fwd(qa, ka, va, z_small, alog, dtb, Bl, S, D):
    NC, d, HB = S // CHUNK, D // DN_HEADS, DN_HEADS_PER_STEP
    HG = DN_HEADS // HB
    chains = [(hh, bb) for hh in range(HB) for bb in range(Bl)]
    G = len(chains)

    def body(q, k, v, z, al, dt, o_ref, sall_ref, st):
        n, g = pl.program_id(0), pl.program_id(1)

        @pl.when(n == 0)
        def _():
            st[g] = jnp.zeros((G, d, d), F32)
        tok_in = lambda r: jnp.stack([r[bb, :, hh * d:(hh + 1) * d] for hh, bb in chains])
        head_in = lambda r: jnp.stack([r[hh] for hh, _ in chains])
        gate_in = lambda lane0: jnp.stack([_lane_column(z[bb], lane0 + g * HB + hh, d) for hh, bb in chains])
        s0 = st[g]
        sall_ref[...] = s0.reshape(HB, Bl, d, d)
        o, s_new = _dn_chunk(tok_in(q), tok_in(k), tok_in(v), gate_in(A_LANE), gate_in(B_LANE), head_in(al), head_in(dt), s0)
        for i, (hh, bb) in enumerate(chains):
            o_ref[bb, :, hh * d:(hh + 1) * d] = o[i]
        st[g] = s_new

    tok = pl.BlockSpec((Bl, CHUNK, HB * d), lambda n, g: (0, n, g))
    per_head = pl.BlockSpec((HB, 1, d), lambda n, g: (g, 0, 0))
    return pl.pallas_call(
        body, name="dn_fwd", grid=(NC, HG),
        in_specs=[tok, tok, tok, pl.BlockSpec((Bl, CHUNK, ZS), lambda n, g: (0, n, 0)), per_head, per_head],
        out_specs=[tok, pl.BlockSpec((HB, Bl, None, d, d), lambda n, g: (g, 0, n, 0, 0))],
        out_shape=[jax.ShapeDtypeStruct((Bl, S, D), F32), jax.ShapeDtypeStruct((DN_HEADS, Bl, NC, d, d), F32)],
        scratch_shapes=[pltpu.VMEM((HG, G, d, d), F32)],
        compiler_params=_params(("arbitrary", "arbitrary")),
    )(qa, ka, va, z_small, alog, dtb)


def _dn_bwd(qa, ka, va, z_small, alog, dtb, s_all, do, dzs_gla, Bl, S, D):
    NC, d, HB = S // CHUNK, D // DN_HEADS, DN_HEADS_PER_STEP
    HG = DN_HEADS // HB
    chains = [(hh, bb) for hh in range(HB) for bb in range(Bl)]
    G = len(chains)

    def lanesum(t):
        return jnp.sum(t, axis=-1, keepdims=True)

    def body(q, k, v, z, al, dt, s0_ref, do_ref, dzg_ref, dq_ref, dk_ref, dv_ref, dzs_ref, dal_ref, ddt_ref, dst):
        n, g = pl.program_id(0), pl.program_id(1)

        @pl.when(n == 0)
        def _():
            dst[g] = jnp.zeros((G, d, d), F32)

        @pl.when((n == 0) & (g == 0))
        def _():
            dal_ref[...] = jnp.zeros_like(dal_ref)
            ddt_ref[...] = jnp.zeros_like(ddt_ref)

        tok_in = lambda r: jnp.stack([r[bb, :, hh * d:(hh + 1) * d] for hh, bb in chains])
        head_in = lambda r: jnp.stack([r[hh] for hh, _ in chains])
        gate_in = lambda lane0: jnp.stack([_lane_column(z[bb], lane0 + g * HB + hh, d) for hh, bb in chains])
        _, vjp = jax.vjp(_dn_chunk, tok_in(q), tok_in(k), tok_in(v), gate_in(A_LANE), gate_in(B_LANE), head_in(al),
                         head_in(dt), s0_ref[...].reshape(G, d, d))
        dq, dkk, dvv, da, db, dal, ddt, ds0 = vjp((tok_in(do_ref), dst[g]))
        da, db = lanesum(da), lanesum(db)
        dal = jnp.broadcast_to(lanesum(dal), (G, 1, d))
        ddt = jnp.broadcast_to(lanesum(ddt), (G, 1, d))
        lane = lax.broadcasted_iota(jnp.int32, (CHUNK, ZS), 1)
        for bb in range(Bl):
            part = jnp.zeros((CHUNK, ZS), F32)
            for i, (hh, b2) in enumerate(chains):
                if b2 == bb:
                    h = g * HB + hh
                    part = part + jnp.where(lane == A_LANE + h, da[i], 0.0) + jnp.where(lane == B_LANE + h, db[i], 0.0)

            @pl.when(g == 0)
            def _():
                dzs_ref[bb] = jnp.where(lane < LOWRANK, dzg_ref[bb], 0.0) + part

            @pl.when(g > 0)
            def _():
                dzs_ref[bb] += part
        for i, (hh, bb) in enumerate(chains):
            cols = slice(hh * d, (hh + 1) * d)
            dq_ref[bb, :, cols] = dq[i]
            dk_ref[bb, :, cols] = dkk[i]
            dv_ref[bb, :, cols] = dvv[i]
            dal_ref[g * HB + hh] += dal[i]
            ddt_ref[g * HB + hh] += ddt[i]
        dst[g] = ds0

    rn = lambda n: NC - 1 - n
    tok = pl.BlockSpec((Bl, CHUNK, HB * d), lambda n, g: (0, rn(n), g))
    zsb = pl.BlockSpec((Bl, CHUNK, ZS), lambda n, g: (0, rn(n), 0))
    per_head = pl.BlockSpec((HB, 1, d), lambda n, g: (g, 0, 0))
    all_heads = pl.BlockSpec((DN_HEADS, 1, d), lambda n, g: (0, 0, 0))
    tok_shape = jax.ShapeDtypeStruct((Bl, S, D), F32)
    head_shape = jax.ShapeDtypeStruct((DN_HEADS, 1, d), F32)
    return pl.pallas_call(
        body, name="dn_bwd", grid=(NC, HG),
        in_specs=[tok, tok, tok, zsb, per_head, per_head,
                  pl.BlockSpec((HB, Bl, None, d, d), lambda n, g: (g, 0, rn(n), 0, 0)), tok, zsb],
        out_specs=[tok, tok, tok, zsb, all_heads, all_heads],
        out_shape=[tok_shape, tok_shape, tok_shape, jax.ShapeDtypeStruct((Bl, S, ZS), F32), head_shape, head_shape],
        scratch_shapes=[pltpu.VMEM((HG, G, d, d), F32)],
        compiler_params=_params(("arbitrary", "arbitrary")),
    )(qa, ka, va, z_small, alog, dtb, s_all, do, dzs_gla)


def _merge_specs(D, bt):
    dv, w = D // GLA_HEADS, D // DN_HEADS
    col = lambda off: pl.BlockSpec((bt, dv), lambda i, h: (i, off // dv + h))
    return dv, w, col


def _merge_load(refs, nsub, w):
    return [[r[:, s * w:(s + 1) * w] for s in range(nsub)] for r in refs]


def _merge_fwd(o_gla, o_dn, z_big, gla_norm, dn_norm, D, bt=256):
    T = o_gla.shape[0]
    bt = _pick(T, bt, SUBLANES)
    dv, w, col = _merge_specs(D, bt)
    nsub = dv // w

    def body(og, gg, od, dz, ga, gb, gn, dn, out):
        ogl, ggl, odl, dzl, gal, gbl = _merge_load([og, gg, od, dz, ga, gb], nsub, w)
        gnl = [gn[:, s * w:(s + 1) * w] for s in range(nsub)]
        outs = _merge_math(ogl, ggl, odl, dzl, gal, gbl, gnl, dn[...])
        for s in range(nsub):
            out[:, s * w:(s + 1) * w] = outs[s].astype(out.dtype)

    return pl.pallas_call(
        body, name="merge_fwd", grid=(T // bt, GLA_HEADS),
        in_specs=[col(0), col(2 * D), col(0), col(6 * D), col(7 * D), col(8 * D),
                  pl.BlockSpec((1, dv), lambda i, h: (0, 0)), pl.BlockSpec((1, w), lambda i, h: (0, 0))],
        out_specs=col(0),
        out_shape=jax.ShapeDtypeStruct((T, D), BF16),
        compiler_params=_params(("parallel", "parallel")),
    )(o_gla, z_big, o_dn, z_big, z_big, z_big, gla_norm, dn_norm)


def _merge_bwd(o_gla, o_dn, z_big, gla_norm, dn_norm, dmix, D, bt=256):
    T = o_gla.shape[0]
    bt = _pick(T, bt, SUBLANES)
    dv, w, col = _merge_specs(D, bt)
    nsub = dv // w

    def body(og, gg, od, dz, ga, gb, gn, dn, dm, dog, dgg, dod, ddz, dga, dgb, dgn, ddn):
        @pl.when((pl.program_id(0) == 0) & (pl.program_id(1) == 0))
        def _():
            dgn[...] = jnp.zeros_like(dgn)
            ddn[...] = jnp.zeros_like(ddn)

        ogl, ggl, odl, dzl, gal, gbl, dml = _merge_load([og, gg, od, dz, ga, gb, dm], nsub, w)
        gnl = [gn[:, s * w:(s + 1) * w] for s in range(nsub)]
        _, vjp = jax.vjp(_merge_math, ogl, ggl, odl, dzl, gal, gbl, gnl, dn[...])
        g_og, g_gg, g_od, g_dz, g_ga, g_gb, g_gn, g_dn = vjp(dml)
        for s in range(nsub):
            sl = slice(s * w, (s + 1) * w)
            dog[:, sl] = g_og[s]
            dgg[:, sl] = g_gg[s].astype(dgg.dtype)
            dod[:, sl] = g_od[s]
            ddz[:, sl] = g_dz[s].astype(ddz.dtype)
            dga[:, sl] = g_ga[s].astype(dga.dtype)
            dgb[:, sl] = g_gb[s].astype(dgb.dtype)
            dgn[:, sl] += g_gn[s]
        ddn[...] += g_dn

    f32s, bf16s = jax.ShapeDtypeStruct((T, D), F32), jax.ShapeDtypeStruct((T, D), BF16)
    return pl.pallas_call(
        body, name="merge_bwd", grid=(T // bt, GLA_HEADS),
        in_specs=[col(0), col(2 * D), col(0), col(6 * D), col(7 * D), col(8 * D),
                  pl.BlockSpec((1, dv), lambda i, h: (0, 0)), pl.BlockSpec((1, w), lambda i, h: (0, 0)), col(0)],
        out_specs=[col(0)] * 6 + [pl.BlockSpec((1, dv), lambda i, h: (0, 0)), pl.BlockSpec((1, w), lambda i, h: (0, 0))],
        out_shape=[f32s, bf16s, f32s, bf16s, bf16s, bf16s,
                   jax.ShapeDtypeStruct((1, dv), F32), jax.ShapeDtypeStruct((1, w), F32)],
        compiler_params=_params(("arbitrary", "arbitrary")),
    )(o_gla, z_big, o_dn, z_big, z_big, z_big, gla_norm, dn_norm, dmix)


def _place():
    return lax.axis_index("x"), lax.axis_index("y"), lax.axis_index("c")


def _other_chips(x, y):
    return [(1 - x, y), (x, 1 - y), (1 - x, 1 - y)]


def _rcopy(src, dst, send_sem, recv_sem, dev):
    return pltpu.make_async_remote_copy(src_ref=src, dst_ref=dst, send_sem=send_sem, recv_sem=recv_sem,
                                        device_id=dev, device_id_type=MESH)


ANY = pl.BlockSpec(memory_space=pl.ANY)


def _half_rows(ref, rows, hc, lead=()):
    rh = rows // 2
    return ref.at[(*lead, pl.ds(pl.multiple_of(hc * rh, 16), rh), slice(None))]


def _gather_ici(shards):
    nw = len(shards)

    def copies(srcs, outs, send_sems, recv_sems):
        x, y, c = _place()
        return [_rcopy(_half_rows(srcs[w], shards[w].shape[0], c), _half_rows(outs[w], shards[w].shape[0], c, (2 * x + y,)),
                       send_sems.at[3 * w + k], recv_sems.at[3 * w + k], (px, py, c))
                for w in range(nw) for k, (px, py) in enumerate(_other_chips(x, y))]

    return _Stage(shards, [jax.ShapeDtypeStruct((4,) + s.shape, s.dtype) for s in shards], 3 * nw, copies)


def _gather_pass(gathered):
    nw = len(gathered)

    def copies(srcs, outs, send_sems, recv_sems):
        x, y, c = _place()
        cps = []
        for w in range(nw):
            r = gathered[w].shape[1]
            for k, (px, py) in enumerate(_other_chips(x, y)):
                slot = (2 * px + py,)
                cps.append(_rcopy(_half_rows(srcs[w], r, c, slot), _half_rows(outs[w], r, c, slot),
                                  send_sems.at[3 * w + k], recv_sems.at[3 * w + k], (x, y, 1 - c)))
        return cps

    return _Stage(gathered, [jax.ShapeDtypeStruct(g.shape, g.dtype) for g in gathered], 3 * nw, copies,
                  aliases={w: w for w in range(nw)})


def _pair_exchange(ps):
    nw = len(ps)

    def copies(srcs, outs, send_sems, recv_sems):
        x, y, c = _place()
        cps = []
        for w in range(nw):
            rh = ps[w].shape[1] // 2
            theirs = srcs[w].at[:, pl.ds(pl.multiple_of((1 - c) * rh, 16), rh), :]
            cps.append(_rcopy(theirs, outs[w], send_sems.at[w], recv_sems.at[w], (x, y, 1 - c)))
        return cps

    return _Stage(ps, [jax.ShapeDtypeStruct((4, p.shape[1] // 2, p.shape[2]), p.dtype) for p in ps], nw, copies)


def _sum_block_rows(rh, c):
    lanes = -(-c // LANES) * LANES
    return _pick(rh, max(16, (3 << 18) // lanes // 16 * 16), 16)


def _pair_sum(p, got, c_idx, name):
    _, R, C = p.shape
    Rh = R // 2
    bt = _sum_block_rows(Rh, C)
    nb = Rh // bt

    def body(c_ref, a, b, of, ob):
        s = a[...] + b[...]
        of[...] = s
        ob[...] = s.astype(BF16)

    spec = pl.BlockSpec((None, bt, C), lambda j, i, c_ref: (j, i, 0))
    return pl.pallas_call(
        body, name=name,
        grid_spec=pltpu.PrefetchScalarGridSpec(
            num_scalar_prefetch=1, grid=(4, nb),
            in_specs=[pl.BlockSpec((None, bt, C), lambda j, i, c_ref: (j, c_ref[0] * nb + i, 0)), spec],
            out_specs=[spec, spec]),
        out_shape=[jax.ShapeDtypeStruct((4, Rh, C), F32), jax.ShapeDtypeStruct((4, Rh, C), BF16)],
        compiler_params=_params(("parallel", "parallel")),
    )(c_idx, p, got)


def _chip_scatter(qbs):
    nw = len(qbs)

    def copies(srcs, outs, send_sems, recv_sems):
        x, y, c = _place()
        return [_rcopy(srcs[w].at[2 * px + py], outs[w].at[k], send_sems.at[3 * w + k], recv_sems.at[3 * w + k], (px, py, c))
                for w in range(nw) for k, (px, py) in enumerate(_other_chips(x, y))]

    return _Stage(qbs, [jax.ShapeDtypeStruct((3,) + q.shape[1:], q.dtype) for q in qbs], 3 * nw, copies)


def _final_sum(qf, got, me_idx, name):
    _, Rh, C = qf.shape
    bt = _sum_block_rows(Rh, C)

    def body(me_ref, a, b, o):
        o[...] = ((a[...] + b[0].astype(F32)) + b[1].astype(F32)) + b[2].astype(F32)

    return pl.pallas_call(
        body, name=name,
        grid_spec=pltpu.PrefetchScalarGridSpec(
            num_scalar_prefetch=1, grid=(Rh // bt,),
            in_specs=[pl.BlockSpec((None, bt, C), lambda i, me_ref: (me_ref[0], i, 0)),
                      pl.BlockSpec((3, bt, C), lambda i, me_ref: (0, i, 0))],
            out_specs=pl.BlockSpec((bt, C), lambda i, me_ref: (i, 0))),
        out_shape=jax.ShapeDtypeStruct((Rh, C), F32),
        compiler_params=_params(("parallel",)),
    )(me_idx, qf, got)


def _pair_allgather(halves):
    nw = len(halves)

    def body(*refs):
        srcs, outs, send_sems, recv_sems = refs[:nw], refs[nw:2 * nw], refs[2 * nw], refs[2 * nw + 1]
        x, y, c = _place()
        cps = []
        for w in range(nw):
            r = 2 * halves[w].shape[0]
            cp = _rcopy(srcs[w], _half_rows(outs[w], r, c), send_sems.at[w], recv_sems.at[w], (x, y, 1 - c))
            cp.start()
            cps.append(cp)
        for w in range(nw):
            got = _half_rows(outs[w], 2 * halves[w].shape[0], 1 - c)
            _rcopy(got, got, send_sems.at[w], recv_sems.at[w], (x, y, 1 - c)).wait_recv()
        for cp in cps:
            cp.wait_send()

    return pl.pallas_call(
        body, name="grad_pair_allgather", in_specs=[ANY] * nw, out_specs=[ANY] * nw,
        out_shape=[jax.ShapeDtypeStruct((2 * h.shape[0], h.shape[1]), h.dtype) for h in halves],
        scratch_shapes=[pltpu.SemaphoreType.DMA((nw,)), pltpu.SemaphoreType.DMA((nw,))],
    )(*halves)


def _small_exchange(items, out_shapes, finish, name):
    n = len(items)
    offs, rows = [], 0
    for it in items:
        offs.append(rows)
        rows += -(-it.shape[0] // SUBLANES) * SUBLANES
    width = -(-max(it.shape[1] for it in items) // LANES) * LANES
    VMEM = pl.BlockSpec(memory_space=pltpu.VMEM)

    def body(*refs):
        ins, outs = refs[:n], refs[n:n + len(out_shapes)]
        buf, send_sems, recv_sems = refs[n + len(out_shapes):]
        x, y, c = _place()
        me = 4 * x + 2 * y + c
        flip = lambda v, f: (1 - v) if f else v
        peers = [(flip(x, r >> 2 & 1), flip(y, r >> 1 & 1), flip(c, r & 1)) for r in range(1, 8)]
        buf[me] = jnp.zeros((rows, width), F32)
        for it, off, ref in zip(items, offs, ins):
            buf[me, off:off + it.shape[0], 0:it.shape[1]] = ref[...]
        cps = [_rcopy(buf.at[me], buf.at[me], send_sems.at[k], recv_sems.at[k], dev) for k, dev in enumerate(peers)]
        for cp in cps:
            cp.start()
        for k, (px, py, pc) in enumerate(peers):
            slot = buf.at[4 * px + 2 * py + pc]
            _rcopy(slot, slot, send_sems.at[k], recv_sems.at[k], (px, py, pc)).wait_recv()
        for cp in cps:
            cp.wait_send()
        finish(buf, offs, outs)

    return pl.pallas_call(
        body, name=name, in_specs=[VMEM] * n, out_specs=[VMEM] * len(out_shapes),
        out_shape=[jax.ShapeDtypeStruct(s, F32) for s in out_shapes],
        scratch_shapes=[pltpu.VMEM((8, rows, width), F32), pltpu.SemaphoreType.DMA((7,)), pltpu.SemaphoreType.DMA((7,))],
        compiler_params=pltpu.CompilerParams(vmem_limit_bytes=VMEM_LIMIT_BYTES),
    )(*items)


def _allreduce_small(items, name):
    def finish(buf, offs, outs):
        for it, off, out in zip(items, offs, outs):
            region = lambda d: buf[d, off:off + it.shape[0], 0:it.shape[1]]
            s = region(0)
            for d in range(1, 8):
                s = s + region(d)
            out[...] = s
    return _small_exchange(items, [it.shape for it in items], finish, name)


def _allgather_small_shards(items, name):
    def finish(buf, offs, outs):
        for it, off, out in zip(items, offs, outs):
            r, c = it.shape
            for j in range(4):
                out[:, j * c:(j + 1) * c] = buf[2 * j, off:off + r, 0:c]
    return _small_exchange(items, [(it.shape[0], 4 * it.shape[1]) for it in items], finish, name)


def _split_w_in(w, D):
    pad = jnp.zeros((w.shape[0], ZS - 3 * LOWRANK), w.dtype)
    big = jnp.concatenate([w[:, :3 * D], w[:, 3 * D + 16:6 * D + 16], w[:, 6 * D + 16:7 * D + 16], w[:, 7 * D + 48:]], axis=1)
    small = jnp.concatenate([w[:, 3 * D:3 * D + 16], w[:, 7 * D + 16:7 * D + 48], pad], axis=1)
    return big, small


def _join_w_in(gb, gs, D):
    return jnp.concatenate([gb[:, :3 * D], gs[:, :16], gb[:, 3 * D:6 * D], gb[:, 6 * D:7 * D], gs[:, 16:48],
                            gb[:, 7 * D:9 * D]], axis=1)


def kernel(x, p, g_mix, w_in, gla_w2, gla_b, gla_norm, dn_conv, dn_a_log, dn_dt_bias, dn_norm, w_out, g_mlp, w_up, w_down, g_ple, w_ple_gate, w_ple_proj, g_final, loss_target, m_g_mix, m_w_in, m_gla_w2, m_gla_b, m_gla_norm, m_dn_conv, m_dn_a_log, m_dn_dt_bias, m_dn_norm, m_w_out, m_g_mlp, m_w_up, m_w_down, m_g_ple, m_w_ple_gate, m_w_ple_proj, m_g_final, v_g_mix, v_w_in, v_gla_w2, v_gla_b, v_gla_norm, v_dn_conv, v_dn_a_log, v_dn_dt_bias, v_dn_norm, v_w_out, v_g_mlp, v_w_up, v_w_down, v_g_ple, v_w_ple_gate, v_w_ple_proj, v_g_final):
    wts = dict(zip(WEIGHTS, [g_mix, w_in, gla_w2, gla_b, gla_norm, dn_conv, dn_a_log, dn_dt_bias, dn_norm, w_out, g_mlp,
                             w_up, w_down, g_ple, w_ple_gate, w_ple_proj, g_final]))
    mom = dict(zip(WEIGHTS, [m_g_mix, m_w_in, m_gla_w2, m_gla_b, m_gla_norm, m_dn_conv, m_dn_a_log, m_dn_dt_bias, m_dn_norm,
                             m_w_out, m_g_mlp, m_w_up, m_w_down, m_g_ple, m_w_ple_gate, m_w_ple_proj, m_g_final]))
    var = dict(zip(WEIGHTS, [v_g_mix, v_w_in, v_gla_w2, v_gla_b, v_gla_norm, v_dn_conv, v_dn_a_log, v_dn_dt_bias, v_dn_norm,
                             v_w_out, v_g_mlp, v_w_up, v_w_down, v_g_ple, v_w_ple_gate, v_w_ple_proj, v_g_final]))
    Bl, S, D = x.shape
    T = Bl * S
    PLE = p.shape[-1]
    dn_d, gla_dk = D // DN_HEADS, D // (2 * GLA_HEADS)
    ix, iy, ic = _place()
    j_me = 2 * ix + iy
    as2d = lambda a: a.reshape(a.shape[-2], a.shape[-1]) if a.ndim > 1 else a.reshape(1, -1)
    c_idx, me_idx = ic.reshape(1).astype(jnp.int32), j_me.reshape(1).astype(jnp.int32)

    shard2d = {n: as2d(wts[n]) for n, _ in BIG}
    bf16_shards = [shard2d[n].astype(BF16) for n, _ in BIG]
    own_slot = lambda g, s: lax.dynamic_update_slice(g, s[None], (j_me, 0, 0))
    (w_in_ici,) = _run_stage(_gather_ici(bf16_shards[:1]), "allgather_w_in_ici")
    (w_in_all,) = _run_stage(_gather_pass([w_in_ici]), "allgather_w_in_pass")
    w_in_slots = own_slot(w_in_all, bf16_shards[0])
    w_big, w_small = _split_w_in(jnp.swapaxes(w_in_slots, 0, 1).reshape(D, -1), D)

    w2_full, conv_full = _allgather_small_shards([as2d(gla_w2), as2d(dn_conv)], "allgather_small_weights")
    w2pad = jnp.pad(w2_full, ((0, ZS - LOWRANK), (0, 0)))
    w2h = jnp.swapaxes(w2pad.reshape(ZS, GLA_HEADS, gla_dk), 0, 1)
    gbh = gla_b.reshape(GLA_HEADS, 1, gla_dk)
    alog_w = jnp.broadcast_to(dn_a_log.reshape(DN_HEADS, 1, 1), (DN_HEADS, 1, dn_d))
    dtb_w = jnp.broadcast_to(dn_dt_bias.reshape(DN_HEADS, 1, 1), (DN_HEADS, 1, dn_d))

    xt = x.reshape(T, D)
    tgt = loss_target.reshape(T, D)
    pt = p.reshape(T, PLE)
    seq = lambda t: t.reshape(Bl, S, t.shape[-1])
    tok = lambda t: t.reshape(T, t.shape[-1])
    h = _rmsnorm_fwd(xt, g_mix, "rms1_fwd")
    z_big, *rest_ici = _matmul(h, w_big, 'nn', [F32], "proj_in", stage=_gather_ici(bf16_shards[1:]))
    z_small, *rest_all = _matmul(h, w_small, 'nn', [F32], "proj_in_narrow", stage=_gather_pass(rest_ici))
    slots = {n: own_slot(g, s) for (n, _), g, s in zip(BIG[1:], rest_all, bf16_shards[1:])}
    rows_joined = lambda t: t.reshape(4 * t.shape[1], t.shape[2])
    w_out_f, w_down_f, w_pg_f = rows_joined(slots['w_out']), rows_joined(slots['w_down']), rows_joined(slots['w_ple_gate'])
    w_up_s, w_pp_s = slots['w_up'], slots['w_ple_proj']
    o_gla, st_all = _gla_fwd(seq(z_big), seq(z_small), w2h, gbh, Bl, S, D)
    acts = [_conv_fwd(z_big, conv_full, grp, Bl, S, D) for grp in range(3)]
    o_dn, s_all = _dn_fwd(seq(acts[0]), seq(acts[1]), seq(acts[2]), seq(z_small), alog_w, dtb_w, Bl, S, D)
    mixed = _merge_fwd(tok(o_gla), tok(o_dn), z_big, gla_norm, dn_norm, D)
    (x1,) = _matmul(mixed, w_out_f, 'nn', [F32], "proj_out", epilogue=lambda r, e: (e + r,), extras=(xt,), bm=512)
    h2 = _rmsnorm_fwd(x1, g_mlp, "rms2_fwd")
    u, act = _matmul(h2, w_up_s, 'nn', [F32, BF16], "mlp_up", b_slots=True,
                     epilogue=lambda r: (r, jnp.square(jnp.maximum(r, 0.0))))
    (x2,) = _matmul(act, w_down_f, 'nn', [F32], "mlp_down", epilogue=lambda r, e: (e + r,), extras=(x1,), bm=512)
    h3 = _rmsnorm_fwd(x2, g_ple, "rms3_fwd")
    (pp,) = _matmul(pt, w_pp_s, 'nn', [F32], "ple_proj", b_slots=True)
    gp, x3 = _matmul(h3, w_pg_f, 'nn', [F32, F32], "ple_gate",
                     epilogue=lambda r, e, q: (r, e + _sigmoid(r) * q), extras=(x2, pp), bm=512)
    dx3, loss_tile, d_g_final = _loss_fwd_bwd(x3, g_final.reshape(1, D), tgt, "loss")

    d_gp, d_pp = _ple_bwd(dx3, gp, pp, "ple_bwd")
    (g_pp,) = _matmul(pt, d_pp, 'tn', [F32], "ple_proj_dw", out_slots=True)
    (g_pg,) = _matmul(h3, d_gp, 'tn', [F32], "ple_gate_dw")
    (dh3,) = _matmul(d_gp, w_pg_f, 'nt', [F32], "ple_gate_dx")
    dx2, dx2b, d_g_ple = _rmsnorm_bwd_add(x2, g_ple, dh3, dx3, "rms3_bwd")
    (g_down,) = _matmul(act, dx2b, 'tn', [F32], "mlp_down_dw")
    (du,) = _matmul(dx2b, w_down_f, 'nt', [BF16], "mlp_down_dx",
                    epilogue=lambda r, e: (r * 2.0 * jnp.maximum(e, 0.0),), extras=(u,))
    (g_up,) = _matmul(h2, du, 'tn', [F32], "mlp_up_dw", out_slots=True)
    by_rows = lambda g: g.reshape(4, g.shape[0] // 4, g.shape[1])
    send_mlp = [g_up, by_rows(g_down), by_rows(g_pg), g_pp]
    dh2, *sib_mlp = _matmul(du, w_up_s, 'nt', [F32], "mlp_up_dx", b_slots=True, stage=_pair_exchange(send_mlp))
    dx1, dx1b, d_g_mlp = _rmsnorm_bwd_add(x1, g_mlp, dh2, dx2, "rms2_bwd")
    (g_out,) = _matmul(mixed, dx1b, 'tn', [F32], "proj_out_dw")
    dmix, sib_out = _matmul(dx1b, w_out_f, 'nt', [F32], "proj_out_dx", stage=_pair_exchange([by_rows(g_out)]))
    rest = [n for n, _ in BIG[1:]]
    send_rest, sib_rest = [by_rows(g_out)] + send_mlp, [sib_out] + sib_mlp
    sums_rest = [_pair_sum(s, f, c_idx, f"grad_pair_sum_{n}") for n, s, f in zip(rest, send_rest, sib_rest)]
    d_ogla, d_gg, d_odn, d_dz, d_ga, d_gb, d_gla_norm, d_dn_norm = _merge_bwd(
        tok(o_gla), tok(o_dn), z_big, gla_norm, dn_norm, dmix, D)
    d_q, d_k, d_v, dzs_gla, d_w2h, d_gbh = _gla_bwd(seq(z_big), seq(z_small), w2h, gbh, st_all, seq(d_ogla), Bl, S, D)
    d_qa, d_ka, d_va, d_zs, d_alog_w, d_dtb_w = _dn_bwd(seq(acts[0]), seq(acts[1]), seq(acts[2]), seq(z_small), alog_w,
                                                        dtb_w, s_all, seq(d_odn), dzs_gla, Bl, S, D)
    conv_b = [_conv_bwd(z_big, conv_full, tok(g), grp, Bl, S, D) for grp, g in enumerate([d_qa, d_ka, d_va])]
    dz_big = jnp.concatenate([tok(d_q), tok(d_k), tok(d_v), d_gg, conv_b[0][0], conv_b[1][0], conv_b[2][0], d_dz, d_ga,
                              d_gb], axis=1)
    dz_small = tok(d_zs)
    d_w_big, *chips_rest = _matmul(h, dz_big, 'tn', [F32], "proj_in_dw", stage=_chip_scatter([b for _, b in sums_rest]))
    halves_rest = [_final_sum(f, got, me_idx, f"grad_final_sum_{n}") for n, (f, _), got in zip(rest, sums_rest, chips_rest)]
    (d_w_small,) = _matmul(h, dz_small, 'tn', [F32], "proj_in_narrow_dw")
    g_in = jnp.swapaxes(_join_w_in(d_w_big, d_w_small, D).reshape(D, 4, -1), 0, 1)
    (sib_in,) = _run_stage(_pair_exchange([g_in]), "grad_pair_exchange_w_in")
    sum_in_f32, sum_in_bf16 = _pair_sum(g_in, sib_in, c_idx, "grad_pair_sum_w_in")
    dh_a, chips_in = _matmul(dz_big, w_big, 'nt', [F32], "proj_in_dx", stage=_chip_scatter([sum_in_bf16]))
    half_in = _final_sum(sum_in_f32, chips_in, me_idx, "grad_final_sum_w_in")
    (dh,) = _matmul(dz_small, w_small, 'nt', [F32], "proj_in_narrow_dx", epilogue=lambda r, e: (e + r,), extras=(dh_a,))
    grad_x, _, d_g_mix = _rmsnorm_bwd_add(xt, g_mix, dh, dx1, "rms1_bwd")
    my_halves = [half_in] + halves_rest
    reduced = {n: lax.dynamic_update_slice(o, hlf, (ic * hlf.shape[0], 0))
               for (n, _), o, hlf in zip(BIG, _pair_allgather(my_halves), my_halves)}

    d_w2 = jnp.swapaxes(d_w2h, 0, 1).reshape(ZS, D // 2)[:LOWRANK]
    small_grads = {'g_mix': d_g_mix, 'gla_w2': d_w2, 'gla_b': d_gbh.reshape(1, D // 2), 'gla_norm': d_gla_norm,
                   'dn_a_log': d_alog_w[:, 0, 0].reshape(1, DN_HEADS), 'dn_dt_bias': d_dtb_w[:, 0, 0].reshape(1, DN_HEADS),
                   'dn_norm': d_dn_norm, 'g_mlp': d_g_mlp, 'g_ple': d_g_ple, 'g_final': d_g_final}
    names = [n for n in SMALL if n != 'dn_conv']
    total = _allreduce_small([small_grads[n] for n in names] + [cb[1] for cb in conv_b] + [loss_tile],
                             "allreduce_small_grads")
    gsmall = dict(zip(names, total[:len(names)]))
    loss = total[-1][0, 0]
    my_cols = lambda g: lax.dynamic_slice_in_dim(g, j_me * (g.shape[1] // 4), g.shape[1] // 4, axis=1)
    gsmall['gla_w2'] = my_cols(gsmall['gla_w2'])
    gsmall['dn_conv'] = my_cols(jnp.concatenate(total[len(names):len(names) + 3], axis=1))

    g_o, d_o, m_o, v_o = {}, {}, {}, {}
    for n, _ in BIG:
        shp = wts[n].shape
        d2, nm2, nv2 = _adamw(shard2d[n], reduced[n], as2d(mom[n]), as2d(var[n]), f"adamw_{n}")
        g_o[n], d_o[n], m_o[n], v_o[n] = reduced[n].reshape(shp), d2.reshape(shp), nm2.reshape(shp), nv2.reshape(shp)
    ds, nms, nvs = _adamw_small([as2d(wts[n]) for n in SMALL], [as2d(gsmall[n]) for n in SMALL],
                                [as2d(mom[n]) for n in SMALL], [as2d(var[n]) for n in SMALL])
    for n, dd, mm, vv in zip(SMALL, ds, nms, nvs):
        shp = wts[n].shape
        g_o[n], d_o[n], m_o[n], v_o[n] = gsmall[n].reshape(shp), dd.reshape(shp), mm.reshape(shp), vv.reshape(shp)

    return (loss, grad_x.reshape(Bl, S, D), *[g_o[n] for n in WEIGHTS], *[d_o[n] for n in WEIGHTS],
            *[m_o[n] for n in WEIGHTS], *[v_o[n] for n in WEIGHTS])
```

```python
import functools

import jax
import jax.numpy as jnp
from jax import lax
from jax.experimental import pallas as pl
from jax.experimental.pallas import tpu as pltpu

F32 = jnp.float32
BF16 = jnp.bfloat16

CHUNK = 64
GLA_HEADS = 4
DN_HEADS = 16
LOWRANK = 16
GLA_TAU = 16.0
DN_CONV = 4
EPS = 1e-6
ZS = 128
A_LANE, B_LANE = LOWRANK, LOWRANK + DN_HEADS
ADAM_LR, ADAM_B1, ADAM_B2, ADAM_EPS, ADAM_WD, ADAM_STEP = 0.001, 0.9, 0.999, 1e-08, 0.01, 10

V7X_VMEM_BYTES = 64 * 1024 * 1024
VMEM_LIMIT_BYTES = V7X_VMEM_BYTES - 8 * 1024 * 1024
LANES = 128
SUBLANES = 8
MESH = pl.DeviceIdType.MESH
DN_HEADS_PER_STEP = 4
GLA_HEADS_PER_STEP = 2

WEIGHTS = ['g_mix', 'w_in', 'gla_w2', 'gla_b', 'gla_norm', 'dn_conv', 'dn_a_log', 'dn_dt_bias', 'dn_norm', 'w_out',
           'g_mlp', 'w_up', 'w_down', 'g_ple', 'w_ple_gate', 'w_ple_proj', 'g_final']
BIG = [('w_in', 1), ('w_out', 0), ('w_up', 1), ('w_down', 0), ('w_ple_gate', 0), ('w_ple_proj', 1)]
SMALL = [n for n in WEIGHTS if n not in dict(BIG)]

_NN, _NT, _TN = 'nn', 'nt', 'tn'


def _params(sem=None):
    return pltpu.CompilerParams(dimension_semantics=sem, vmem_limit_bytes=VMEM_LIMIT_BYTES)


def _dot(a, b, form, precision=None):
    o = a.ndim - 2
    contract = {_NN: ((1 + o,), (o,)), _NT: ((1 + o,), (1 + o,)), _TN: ((o,), (o,))}[form]
    batch = ((0,), (0,)) if o else ((), ())
    return lax.dot_general(a, b, (contract, batch), precision=precision, preferred_element_type=F32)


def _make_mm(cast, precision):
    def raw(a, b, dims):
        return _dot(cast(a), cast(b), dims, precision)

    @jax.custom_vjp
    def nn(a, b):
        return raw(a, b, _NN)
    nn.defvjp(lambda a, b: (raw(a, b, _NN), (a, b)), lambda r, g: (raw(g, r[1], _NT), raw(r[0], g, _TN)))

    @jax.custom_vjp
    def nt(a, b):
        return raw(a, b, _NT)
    nt.defvjp(lambda a, b: (raw(a, b, _NT), (a, b)), lambda r, g: (raw(g, r[1], _NN), raw(g, r[0], _TN)))

    @jax.custom_vjp
    def tn(a, b):
        return raw(a, b, _TN)
    tn.defvjp(lambda a, b: (raw(a, b, _TN), (a, b)), lambda r, g: (raw(r[1], g, _NT), raw(r[0], g, _NN)))
    return nn, nt, tn


_bnn, _bnt, _btn = _make_mm(lambda t: t.astype(BF16), None)
TRI_PRECISION = lax.Precision.HIGH


def _iota2(n, axis):
    return lax.broadcasted_iota(jnp.int32, (n, n), axis)


def _lower(n, strict=False):
    return (_iota2(n, 0) > _iota2(n, 1)) if strict else (_iota2(n, 0) >= _iota2(n, 1))


def _tri_times(tri, x):
    tri = tri.astype(F32)
    if x.ndim == 3:
        tri = jnp.broadcast_to(tri, (x.shape[0],) + tri.shape)
    return _dot(tri, x, _NN, lax.Precision.HIGHEST)


@jax.custom_vjp
def _cumsum_rows(x):
    return _tri_times(_lower(x.shape[-2]), x)


def _cumsum_rows_bwd(_, g):
    n = g.shape[-2]
    return (_tri_times(_iota2(n, 0) <= _iota2(n, 1), g),)


_cumsum_rows.defvjp(lambda x: (_cumsum_rows(x), None), _cumsum_rows_bwd)


def _tri_inv_impl(a):
    n = a.shape[-1]
    eye = (_iota2(n, 0) == _iota2(n, 1)).astype(F32)
    p = eye - a
    ak = a
    k = 2
    while k < n:
        prec, cast = (TRI_PRECISION, lambda t: t) if k == 2 else (None, lambda t: t.astype(BF16))
        ak = _dot(cast(ak), cast(ak), _NN, prec)
        p = p + _dot(cast(p), cast(ak), _NN, prec)
        k *= 2
    return p


@jax.custom_vjp
def _tri_inv(a):
    return _tri_inv_impl(a)


def _tri_inv_fwd(a):
    t = _tri_inv_impl(a)
    return t, t


def _tri_inv_bwd(t, g):
    tb = t.astype(BF16)
    tg = _dot(tb, g.astype(BF16), _TN)
    return (-_dot(tg.astype(BF16), tb, _NT),)


_tri_inv.defvjp(_tri_inv_fwd, _tri_inv_bwd)


def _shift_rows(x, s, down):
    n = x.shape[0]
    r = lax.broadcasted_iota(jnp.int32, x.shape, 0)
    if down:
        return jnp.where(r >= s, pltpu.roll(x, s, 0), 0.0)
    return jnp.where(r < n - s, pltpu.roll(x, n - s, 0), 0.0)


def _make_shift(s):
    @jax.custom_vjp
    def f(x):
        return _shift_rows(x, s, True)
    f.defvjp(lambda x: (_shift_rows(x, s, True), None), lambda _, g: (_shift_rows(g, s, False),))
    return f


def _sigmoid(x):
    return jax.nn.sigmoid(x)


def _silu(x):
    return x * jax.nn.sigmoid(x)


def _softplus(x):
    return jnp.maximum(x, 0.0) + jnp.log1p(jnp.exp(-jnp.abs(x)))


def _log_sigmoid(x):
    return -_softplus(-x)


def _rms(x, g):
    return x * lax.rsqrt(jnp.mean(x * x, axis=-1, keepdims=True) + EPS) * g


def _gla_chunk(q, k, v, zs, w2, gb, st, *, scale):
    c = q.shape[-2]
    logf = _log_sigmoid(_bnn(zs, w2) + gb) * (1.0 / GLA_TAU)
    bcum = _cumsum_rows(logf)
    b_last = jnp.sum(logf, axis=-2, keepdims=True)
    q_in = (q * scale) * jnp.exp(bcum)
    k_in = k * jnp.exp(-bcum)
    a = jnp.where(_lower(c), _bnt(q_in, k_in), 0.0)
    o = _bnn(a, v) + _bnt(q_in, st)
    k_dec = k * jnp.exp(b_last - bcum)
    st_new = st * jnp.exp(b_last) + _btn(v, k_dec)
    return o, st_new


def _dn_chunk(q, k, v, aw, bw, alog, dtb, s):
    c = q.shape[-2]
    incl, strict = _lower(c), _lower(c, True)
    g_w = -jnp.exp(alog) * _softplus(aw + dtb)
    beta_w = _sigmoid(bw)
    gcum_w = _cumsum_rows(g_w)
    lane0 = lax.broadcasted_iota(jnp.int32, gcum_w.shape, gcum_w.ndim - 1) == 0
    gcol = jnp.sum(jnp.where(lane0, gcum_w, 0.0), axis=-1, keepdims=True)
    d1 = jnp.broadcast_to(gcol, gcol.shape[:-1] + (c,))
    diff = jnp.where(incl, d1 - jnp.swapaxes(d1, -1, -2), 0.0)
    decay = jnp.where(incl, jnp.exp(diff), 0.0)
    k_beta = k * beta_w
    a = jnp.where(strict, _bnt(k_beta, k) * decay, 0.0)
    t = _tri_inv(a)
    egc = jnp.exp(gcum_w)
    u = _bnn(t, v * beta_w)
    w = _bnn(t, k_beta * egc)
    attn = jnp.where(incl, _bnt(q, k) * decay, 0.0)
    q_dec = q * egc
    g_last = jnp.sum(g_w, axis=-2, keepdims=True)
    k_dec = k * jnp.exp(g_last - gcum_w)
    v_new = u - _bnn(w, s)
    o = _bnn(q_dec, s) + _bnn(attn, v_new)
    s_new = s * jnp.exp(g_last) + _btn(k_dec, v_new)
    return o, s_new


def _conv_act(x, wrows, *, l2, scale):
    taps = len(wrows)
    y = None
    for j in range(taps):
        s = taps - 1 - j
        xs = x if s == 0 else _make_shift(s)(x)
        y = wrows[j] * xs if y is None else y + wrows[j] * xs
    y = _silu(y)
    if l2:
        y = y * lax.rsqrt(jnp.sum(y * y, axis=-1, keepdims=True) + EPS) * scale
    return y


def _merge_math(og, gg, od, dz, ga, gb, gn, dn):
    nsub = len(og)
    dv = nsub * og[0].shape[1]
    ssq = jnp.sum(og[0] * og[0], axis=-1, keepdims=True)
    for s in range(1, nsub):
        ssq = ssq + jnp.sum(og[s] * og[s], axis=-1, keepdims=True)
    r = lax.rsqrt(ssq * (1.0 / dv) + EPS)
    outs = []
    for s in range(nsub):
        a = og[s] * r * gn[s] * _silu(gg[s])
        b = _rms(od[s], dn) * _silu(dz[s])
        outs.append(_sigmoid(ga[s]) * a + _sigmoid(gb[s]) * b)
    return outs


def _pick(n, target, mult):
    best = None
    for d in range(mult, min(n, target) + 1, mult):
        if n % d == 0:
            best = d
    return best if best is not None else n


class _Stage:
    def __init__(self, inputs, out_shapes, n_sems, copies, aliases=None):
        self.inputs, self.out_shapes, self.n_sems, self.copies = list(inputs), list(out_shapes), n_sems, copies
        self.aliases = aliases or {}

    @property
    def sems(self):
        return [pltpu.SemaphoreType.DMA((self.n_sems,)), pltpu.SemaphoreType.DMA((self.n_sems,))]


def _host_stage(body, stage, n_in, n_out, grid):
    ci, co = len(stage.inputs), len(stage.out_shapes)

    def wrapped(*refs):
        ins, cins = refs[:n_in], refs[n_in:n_in + ci]
        outs, couts = refs[n_in + ci:n_in + ci + n_out], refs[n_in + ci + n_out:n_in + ci + n_out + co]
        scratch, sems = refs[n_in + ci + n_out + co:-2], refs[-2:]
        ids = [pl.program_id(d) for d in range(len(grid))]
        first, last = ids[0] == 0, ids[0] == grid[0] - 1
        for i, g in zip(ids[1:], grid[1:]):
            first, last = first & (i == 0), last & (i == g - 1)

        @pl.when(first)
        def _():
            for cp in stage.copies(cins, couts, *sems):
                cp.start()

        body(*ins, *outs, *scratch)

        @pl.when(last)
        def _():
            for cp in stage.copies(cins, couts, *sems):
                cp.wait()

    return wrapped


def _run_stage(stage, name):
    ci = len(stage.inputs)

    def body(*refs):
        cps = stage.copies(refs[:ci], refs[ci:-2], *refs[-2:])
        for cp in cps:
            cp.start()
        for cp in cps:
            cp.wait()

    return pl.pallas_call(body, name=name, in_specs=[ANY] * ci, out_specs=[ANY] * len(stage.out_shapes),
                          out_shape=stage.out_shapes, scratch_shapes=stage.sems,
                          input_output_aliases=dict(stage.aliases))(*stage.inputs)


def _matmul(a, b, form, out_dtypes, name, epilogue=None, extras=(), bm=1024, bn=1024, bk=2048,
            b_slots=False, out_slots=False, stage=None):
    ns, c = (b.shape[0], b.shape[2]) if b_slots else (1, None)
    b2 = b.shape[1:] if b_slots else b.shape
    if form == 'nn':
        (M, K), (K2, N) = a.shape, (b2[0], b2[1] * ns)
    elif form == 'nt':
        (M, K), (N, K2) = a.shape, (b2[0], b2[1] * ns)
    else:
        (K, M), (K2, N) = a.shape, b2
    assert K == K2 and not (b_slots and form == 'tn'), (a.shape, b.shape, form)
    bm, bn, bk = _pick(M, bm, SUBLANES), _pick(N, bn, LANES), _pick(K, bk, LANES)
    if b_slots:
        bn, bk = (_pick(c, bn, LANES), bk) if form == 'nn' else (bn, _pick(c, bk, LANES))
    if out_slots:
        oc = N // 4
        bn = _pick(oc, bn, LANES)
    nk = K // bk
    a_spec = pl.BlockSpec((bk, bm), lambda i, j, k: (k, i)) if form == 'tn' else pl.BlockSpec((bm, bk), lambda i, j, k: (i, k))
    if b_slots and form == 'nn':
        per = c // bn
        b_spec = pl.BlockSpec((None, bk, bn), lambda i, j, k: (j // per, k, j % per))
    elif b_slots:
        per = c // bk
        b_spec = pl.BlockSpec((None, bn, bk), lambda i, j, k: (k // per, j, k % per))
    elif form == 'nt':
        b_spec = pl.BlockSpec((bn, bk), lambda i, j, k: (j, k))
    else:
        b_spec = pl.BlockSpec((bk, bn), lambda i, j, k: (k, j))
    o_spec = pl.BlockSpec((bm, bn), lambda i, j, k: (i, j))
    if out_slots:
        oper = oc // bn
        out_spec = pl.BlockSpec((None, bm, bn), lambda i, j, k: (j // oper, i, j % oper))
        out_shape = [jax.ShapeDtypeStruct((4, M, oc), d) for d in out_dtypes]
    else:
        out_spec = o_spec
        out_shape = [jax.ShapeDtypeStruct((M, N), d) for d in out_dtypes]
    ne, no = len(extras), len(out_dtypes)

    def finish(r, extra_refs, out_refs):
        outs = (r,) if epilogue is None else epilogue(r, *[e[...] for e in extra_refs])
        for ref, o in zip(out_refs, outs):
            ref[...] = o.astype(ref.dtype)

    def body_one(a_ref, b_ref, *rest):
        finish(_dot(a_ref[...].astype(BF16), b_ref[...].astype(BF16), form), rest[:ne], rest[ne:ne + no])

    def body_acc(a_ref, b_ref, *rest):
        extra_refs, out_refs, acc = rest[:ne], rest[ne:ne + no], rest[ne + no]
        k = pl.program_id(2)
        part = _dot(a_ref[...].astype(BF16), b_ref[...].astype(BF16), form)

        @pl.when(k == 0)
        def _():
            acc[...] = part

        @pl.when((k > 0) & (k < nk - 1))
        def _():
            acc[...] += part

        @pl.when(k == nk - 1)
        def _():
            finish(acc[...] + part, extra_refs, out_refs)

    body = body_one if nk == 1 else body_acc
    grid = (M // bm, N // bn, nk)
    scratch = [] if nk == 1 else [pltpu.VMEM((bm, bn), F32)]
    if stage is None:
        return pl.pallas_call(
            body, name=name, grid=grid, in_specs=[a_spec, b_spec] + [o_spec] * ne,
            out_specs=[out_spec] * no, out_shape=out_shape, scratch_shapes=scratch,
            compiler_params=_params(("parallel", "parallel", "arbitrary")),
        )(a, b, *extras)
    ci, co = len(stage.inputs), len(stage.out_shapes)
    return pl.pallas_call(
        _host_stage(body, stage, 2 + ne, no, grid), name=name, grid=grid,
        in_specs=[a_spec, b_spec] + [o_spec] * ne + [ANY] * ci,
        out_specs=[out_spec] * no + [ANY] * co, out_shape=out_shape + stage.out_shapes,
        scratch_shapes=scratch + stage.sems,
        input_output_aliases={2 + ne + i: no + o for i, o in stage.aliases.items()},
        compiler_params=_params(("arbitrary", "arbitrary", "arbitrary")),
    )(a, b, *extras, *stage.inputs)


def _rowwise(fn, rows, consts, row_outs, acc_outs, name, bt=256):
    T = rows[0].shape[0]
    bt = _pick(T, bt, SUBLANES)
    nr, nc, no, na = len(rows), len(consts), len(row_outs), len(acc_outs)

    def body(*refs):
        r_in, c_in = refs[:nr], refs[nr:nr + nc]
        r_out, a_out = refs[nr + nc:nr + nc + no], refs[nr + nc + no:]
        ro, ao = fn([r[...] for r in r_in], [c[...] for c in c_in])
        for ref, o in zip(r_out, ro):
            ref[...] = o.astype(ref.dtype)
        if na:
            @pl.when(pl.program_id(0) == 0)
            def _():
                for ref in a_out:
                    ref[...] = jnp.zeros_like(ref)
            for ref, o in zip(a_out, ao):
                ref[...] += o

    whole = lambda shp: pl.BlockSpec(shp, lambda i: (0,) * len(shp))
    return pl.pallas_call(
        body, name=name, grid=(T // bt,),
        in_specs=[pl.BlockSpec((bt, r.shape[1]), lambda i: (i, 0)) for r in rows] + [whole(c.shape) for c in consts],
        out_specs=[pl.BlockSpec((bt, w), lambda i: (i, 0)) for w, _ in row_outs] + [whole(s) for s in acc_outs],
        out_shape=[jax.ShapeDtypeStruct((T, w), d) for w, d in row_outs] + [jax.ShapeDtypeStruct(s, F32) for s in acc_outs],
        compiler_params=_params(("arbitrary",)),
    )(*rows, *consts)


def _rmsnorm_fwd(x, g, name):
    return _rowwise(lambda r, c: ([_rms(r[0], c[0])], []), [x], [g], [(x.shape[1], BF16)], [], name)[0]


def _rmsnorm_bwd_add(x, g, dh, dres, name):
    D = x.shape[1]

    def fn(r, c):
        _, vjp = jax.vjp(_rms, r[0], c[0])
        dx, dg = vjp(r[1])
        dx = dx + r[2]
        return [dx, dx], [dg]
    return _rowwise(fn, [x, dh, dres], [g], [(D, F32), (D, BF16)], [(1, D)], name)


def _loss_fwd_bwd(x3, g, target, name):
    D = x3.shape[1]

    def fn(r, c):
        def row_loss(x, gain):
            err = _rms(x, gain) - r[1]
            return 0.5 * jnp.mean(err * err, axis=-1, keepdims=True)
        lrow, vjp = jax.vjp(row_loss, r[0], c[0])
        dx, dg = vjp(jnp.ones_like(lrow))
        tile = jnp.broadcast_to(jnp.sum(lrow, axis=0, keepdims=True), (SUBLANES, LANES))
        return [dx], [tile, dg]
    return _rowwise(fn, [x3, target], [g], [(D, F32)], [(SUBLANES, LANES), (1, D)], name)


def _ple_bwd(dx3, gp, pp, name):
    D = dx3.shape[1]

    def fn(r, c):
        s = _sigmoid(r[1])
        return [r[0] * r[2] * s * (1.0 - s), r[0] * s], []
    return _rowwise(fn, [dx3, gp, pp], [], [(D, BF16), (D, BF16)], [], name)


def _adamw_math(w, g, m, v):
    nm = ADAM_B1 * m + (1.0 - ADAM_B1) * g
    nv = ADAM_B2 * v + (1.0 - ADAM_B2) * (g * g)
    m_hat = nm / (1.0 - ADAM_B1 ** ADAM_STEP)
    v_hat = nv / (1.0 - ADAM_B2 ** ADAM_STEP)
    return -ADAM_LR * (m_hat / (jnp.sqrt(v_hat) + ADAM_EPS) + ADAM_WD * w), nm, nv


def _adamw(w, g, m, v, name):
    R, C = w.shape[0], w.shape[-1]
    lanes = -(-C // LANES) * LANES
    if w.ndim == 2:
        bt = _pick(R, max(SUBLANES, (1 << 18) // lanes // SUBLANES * SUBLANES), SUBLANES)
        spec = pl.BlockSpec((bt, C), lambda i: (i, 0))
    else:
        bt = _pick(R, max(1, (1 << 18) // lanes), 1)
        spec = pl.BlockSpec((bt, 1, C), lambda i: (i, 0, 0))

    def body(w_ref, g_ref, m_ref, v_ref, d_ref, nm_ref, nv_ref):
        d_ref[...], nm_ref[...], nv_ref[...] = _adamw_math(w_ref[...], g_ref[...], m_ref[...], v_ref[...])

    return pl.pallas_call(
        body, name=name, grid=(R // bt,), in_specs=[spec] * 4, out_specs=[spec] * 3,
        out_shape=[jax.ShapeDtypeStruct(w.shape, F32)] * 3, compiler_params=_params(("parallel",)),
    )(w, g, m, v)


def _adamw_small(ws, gs, ms, vs):
    n = len(ws)

    def body(*refs):
        for i in range(n):
            d, nm, nv = _adamw_math(refs[i][...], refs[n + i][...], refs[2 * n + i][...], refs[3 * n + i][...])
            refs[4 * n + i][...], refs[5 * n + i][...], refs[6 * n + i][...] = d, nm, nv

    VMEM = pl.BlockSpec(memory_space=pltpu.VMEM)
    shapes = [jax.ShapeDtypeStruct(w.shape, F32) for w in ws]
    outs = pl.pallas_call(body, name="adamw_small", in_specs=[VMEM] * (4 * n), out_specs=[VMEM] * (3 * n),
                          out_shape=shapes * 3)(*ws, *gs, *ms, *vs)
    return outs[:n], outs[n:2 * n], outs[2 * n:]


def _gla_fwd(z_big, z_small, w2h, gbh, Bl, S, D):
    NC, dk, dv, HB = S // CHUNK, D // (2 * GLA_HEADS), D // GLA_HEADS, GLA_HEADS_PER_STEP
    HG = GLA_HEADS // HB
    chains = [(hh, bb) for hh in range(HB) for bb in range(Bl)]
    G = len(chains)
    fn = functools.partial(_gla_chunk, scale=dk ** -0.5)

    def body(q, k, v, z, w2, gb, o_ref, stall_ref, st):
        n, g = pl.program_id(0), pl.program_id(1)

        @pl.when(n == 0)
        def _():
            st[g] = jnp.zeros((G, dv, dk), F32)
        s0 = st[g]
        stall_ref[...] = s0.reshape(HB, Bl, dv, dk)
        qk = lambda r: jnp.stack([r[bb, :, hh * dk:(hh + 1) * dk] for hh, bb in chains])
        o, s_new = fn(qk(q), qk(k), jnp.stack([v[bb, :, hh * dv:(hh + 1) * dv] for hh, bb in chains]),
                      jnp.stack([z[bb] for _, bb in chains]), jnp.stack([w2[hh] for hh, _ in chains]),
                      jnp.stack([gb[hh] for hh, _ in chains]), s0)
        for i, (hh, bb) in enumerate(chains):
            o_ref[bb, :, hh * dv:(hh + 1) * dv] = o[i]
        st[g] = s_new

    return pl.pallas_call(
        body, name="gla_fwd", grid=(NC, HG),
        in_specs=[pl.BlockSpec((Bl, CHUNK, HB * dk), lambda n, g: (0, n, g)),
                  pl.BlockSpec((Bl, CHUNK, HB * dk), lambda n, g: (0, n, HG + g)),
                  pl.BlockSpec((Bl, CHUNK, HB * dv), lambda n, g: (0, n, HG + g)),
                  pl.BlockSpec((Bl, CHUNK, ZS), lambda n, g: (0, n, 0)),
                  pl.BlockSpec((HB, ZS, dk), lambda n, g: (g, 0, 0)),
                  pl.BlockSpec((HB, 1, dk), lambda n, g: (g, 0, 0))],
        out_specs=[pl.BlockSpec((Bl, CHUNK, HB * dv), lambda n, g: (0, n, g)),
                   pl.BlockSpec((HB, Bl, None, dv, dk), lambda n, g: (g, 0, n, 0, 0))],
        out_shape=[jax.ShapeDtypeStruct((Bl, S, D), F32), jax.ShapeDtypeStruct((GLA_HEADS, Bl, NC, dv, dk), F32)],
        scratch_shapes=[pltpu.VMEM((HG, G, dv, dk), F32)],
        compiler_params=_params(("arbitrary", "arbitrary")),
    )(z_big, z_big, z_big, z_small, w2h, gbh)


def _gla_bwd(z_big, z_small, w2h, gbh, st_all, do, Bl, S, D):
    NC, dk, dv, HB = S // CHUNK, D // (2 * GLA_HEADS), D // GLA_HEADS, GLA_HEADS_PER_STEP
    HG = GLA_HEADS // HB
    chains = [(hh, bb) for hh in range(HB) for bb in range(Bl)]
    G = len(chains)
    fn = functools.partial(_gla_chunk, scale=dk ** -0.5)

    def body(q, k, v, z, w2, gb, st0, do_ref, dq_ref, dk_ref, dv_ref, dzs_ref, dw2_ref, dgb_ref, dst):
        n, g = pl.program_id(0), pl.program_id(1)

        @pl.when(n == 0)
        def _():
            dst[g] = jnp.zeros((G, dv, dk), F32)

        @pl.when((n == 0) & (g == 0))
        def _():
            dw2_ref[...] = jnp.zeros_like(dw2_ref)
            dgb_ref[...] = jnp.zeros_like(dgb_ref)

        qk = lambda r: jnp.stack([r[bb, :, hh * dk:(hh + 1) * dk] for hh, bb in chains])
        vv = lambda r: jnp.stack([r[bb, :, hh * dv:(hh + 1) * dv] for hh, bb in chains])
        _, vjp = jax.vjp(fn, qk(q), qk(k), vv(v), jnp.stack([z[bb] for _, bb in chains]),
                         jnp.stack([w2[hh] for hh, _ in chains]), jnp.stack([gb[hh] for hh, _ in chains]),
                         st0[...].reshape(G, dv, dk))
        dq, dkk, dvv, dzs, dw2, dgb, dst0 = vjp((vv(do_ref), dst[g]))
        for i, (hh, bb) in enumerate(chains):
            dq_ref[bb, :, hh * dk:(hh + 1) * dk] = dq[i].astype(dq_ref.dtype)
            dk_ref[bb, :, hh * dk:(hh + 1) * dk] = dkk[i].astype(dk_ref.dtype)
            dv_ref[bb, :, hh * dv:(hh + 1) * dv] = dvv[i].astype(dv_ref.dtype)
            dw2_ref[g * HB + hh] += dw2[i]
            dgb_ref[g * HB + hh] += dgb[i]
        for bb in range(Bl):
            tot = sum(dzs[i] for i, (_, b2) in enumerate(chains) if b2 == bb)

            @pl.when(g == 0)
            def _():
                dzs_ref[bb] = tot

            @pl.when(g > 0)
            def _():
                dzs_ref[bb] += tot
        dst[g] = dst0

    rn = lambda n: NC - 1 - n
    return pl.pallas_call(
        body, name="gla_bwd", grid=(NC, HG),
        in_specs=[pl.BlockSpec((Bl, CHUNK, HB * dk), lambda n, g: (0, rn(n), g)),
                  pl.BlockSpec((Bl, CHUNK, HB * dk), lambda n, g: (0, rn(n), HG + g)),
                  pl.BlockSpec((Bl, CHUNK, HB * dv), lambda n, g: (0, rn(n), HG + g)),
                  pl.BlockSpec((Bl, CHUNK, ZS), lambda n, g: (0, rn(n), 0)),
                  pl.BlockSpec((HB, ZS, dk), lambda n, g: (g, 0, 0)),
                  pl.BlockSpec((HB, 1, dk), lambda n, g: (g, 0, 0)),
                  pl.BlockSpec((HB, Bl, None, dv, dk), lambda n, g: (g, 0, rn(n), 0, 0)),
                  pl.BlockSpec((Bl, CHUNK, HB * dv), lambda n, g: (0, rn(n), g))],
        out_specs=[pl.BlockSpec((Bl, CHUNK, HB * dk), lambda n, g: (0, rn(n), g)),
                   pl.BlockSpec((Bl, CHUNK, HB * dk), lambda n, g: (0, rn(n), g)),
                   pl.BlockSpec((Bl, CHUNK, HB * dv), lambda n, g: (0, rn(n), g)),
                   pl.BlockSpec((Bl, CHUNK, ZS), lambda n, g: (0, rn(n), 0)),
                   pl.BlockSpec((GLA_HEADS, ZS, dk), lambda n, g: (0, 0, 0)),
                   pl.BlockSpec((GLA_HEADS, 1, dk), lambda n, g: (0, 0, 0))],
        out_shape=[jax.ShapeDtypeStruct((Bl, S, D // 2), BF16), jax.ShapeDtypeStruct((Bl, S, D // 2), BF16),
                   jax.ShapeDtypeStruct((Bl, S, D), BF16), jax.ShapeDtypeStruct((Bl, S, ZS), F32),
                   jax.ShapeDtypeStruct((GLA_HEADS, ZS, dk), F32), jax.ShapeDtypeStruct((GLA_HEADS, 1, dk), F32)],
        scratch_shapes=[pltpu.VMEM((HG, G, dv, dk), F32)],
        compiler_params=_params(("arbitrary", "arbitrary")),
    )(z_big, z_big, z_big, z_small, w2h, gbh, st_all, do)


def _conv_fwd(z_big, conv_w, grp, Bl, S, D):
    d = D // DN_HEADS
    l2, scale = grp < 2, (d ** -0.5 if grp == 0 else 1.0)
    x_blk0 = (3 * D + grp * D) // d

    def body(x_ref, w_ref, o_ref):
        wrows = [w_ref[j:j + 1, :] for j in range(DN_CONV)]
        o_ref[...] = _conv_act(x_ref[...], wrows, l2=l2, scale=scale)

    return pl.pallas_call(
        body, name=f"conv_fwd{grp}", grid=(Bl, DN_HEADS),
        in_specs=[pl.BlockSpec((S, d), lambda b, j: (b, x_blk0 + j)),
                  pl.BlockSpec((DN_CONV, d), lambda b, j: (0, grp * DN_HEADS + j))],
        out_specs=pl.BlockSpec((S, d), lambda b, j: (b, j)),
        out_shape=jax.ShapeDtypeStruct((Bl * S, D), F32),
        compiler_params=_params(("parallel", "parallel")),
    )(z_big, conv_w)


def _conv_bwd(z_big, conv_w, dact, grp, Bl, S, D):
    d = D // DN_HEADS
    l2, scale = grp < 2, (d ** -0.5 if grp == 0 else 1.0)
    x_blk0 = (3 * D + grp * D) // d

    def body(x_ref, w_ref, g_ref, dx_ref, dw_ref):
        @pl.when(pl.program_id(1) == 0)
        def _():
            dw_ref[...] = jnp.zeros_like(dw_ref)
        wrows = [w_ref[j:j + 1, :] for j in range(DN_CONV)]
        _, vjp = jax.vjp(lambda x, wr: _conv_act(x, wr, l2=l2, scale=scale), x_ref[...], wrows)
        dx, dwr = vjp(g_ref[...])
        dx_ref[...] = dx.astype(dx_ref.dtype)
        for j in range(DN_CONV):
            dw_ref[j:j + 1, :] += dwr[j]

    return pl.pallas_call(
        body, name=f"conv_bwd{grp}", grid=(DN_HEADS, Bl),
        in_specs=[pl.BlockSpec((S, d), lambda j, b: (b, x_blk0 + j)),
                  pl.BlockSpec((DN_CONV, d), lambda j, b: (0, grp * DN_HEADS + j)),
                  pl.BlockSpec((S, d), lambda j, b: (b, j))],
        out_specs=[pl.BlockSpec((S, d), lambda j, b: (b, j)), pl.BlockSpec((DN_CONV, d), lambda j, b: (0, j))],
        out_shape=[jax.ShapeDtypeStruct((Bl * S, D), BF16), jax.ShapeDtypeStruct((DN_CONV, D), F32)],
        compiler_params=_params(("arbitrary", "arbitrary")),
    )(z_big, conv_w, dact)


def _lane_column(zb, lane, width):
    pick = lax.broadcasted_iota(jnp.int32, zb.shape, 1) == lane
    return jnp.broadcast_to(jnp.sum(jnp.where(pick, zb, 0.0), axis=-1, keepdims=True), (zb.shape[0], width))


def _dn_fwd(qa, ka, va, z_small, alog, dtb, Bl, S, D):
    NC, d, HB = S // CHUNK, D // DN_HEADS, DN_HEADS_PER_STEP
    HG = DN_HEADS // HB
    chains = [(hh, bb) for hh in range(HB) for bb in range(Bl)]
    G = len(chains)

    def body(q, k, v, z, al, dt, o_ref, sall_ref, st):
        n, g = pl.program_id(0), pl.program_id(1)

        @pl.when(n == 0)
        def _():
            st[g] = jnp.zeros((G, d, d), F32)
        tok_in = lambda r: jnp.stack([r[bb, :, hh * d:(hh + 1) * d] for hh, bb in chains])
        head_in = lambda r: jnp.stack([r[hh] for hh, _ in chains])
        gate_in = lambda lane0: jnp.stack([_lane_column(z[bb], lane0 + g * HB + hh, d) for hh, bb in chains])
        s0 = st[g]
        sall_ref[...] = s0.reshape(HB, Bl, d, d)
        o, s_new = _dn_chunk(tok_in(q), tok_in(k), tok_in(v), gate_in(A_LANE), gate_in(B_LANE), head_in(al), head_in(dt), s0)
        for i, (hh, bb) in enumerate(chains):
            o_ref[bb, :, hh * d:(hh + 1) * d] = o[i]
        st[g] = s_new

    tok = pl.BlockSpec((Bl, CHUNK, HB * d), lambda n, g: (0, n, g))
    per_head = pl.BlockSpec((HB, 1, d), lambda n, g: (g, 0, 0))
    return pl.pallas_call(
        body, name="dn_fwd", grid=(NC, HG),
        in_specs=[tok, tok, tok, pl.BlockSpec((Bl, CHUNK, ZS), lambda n, g: (0, n, 0)), per_head, per_head],
        out_specs=[tok, pl.BlockSpec((HB, Bl, None, d, d), lambda n, g: (g, 0, n, 0, 0))],
        out_shape=[jax.ShapeDtypeStruct((Bl, S, D), F32), jax.ShapeDtypeStruct((DN_HEADS, Bl, NC, d, d), F32)],
        scratch_shapes=[pltpu.VMEM((HG, G, d, d), F32)],
        compiler_params=_params(("arbitrary", "arbitrary")),
    )(qa, ka, va, z_small, alog, dtb)


def _dn_bwd(qa, ka, va, z_small, alog, dtb, s_all, do, dzs_gla, Bl, S, D):
    NC, d, HB = S // CHUNK, D // DN_HEADS, DN_HEADS_PER_STEP
    HG = DN_HEADS // HB
    chains = [(hh, bb) for hh in range(HB) for bb in range(Bl)]
    G = len(chains)

    def lanesum(t):
        return jnp.sum(t, axis=-1, keepdims=True)

    def body(q, k, v, z, al, dt, s0_ref, do_ref, dzg_ref, dq_ref, dk_ref, dv_ref, dzs_ref, dal_ref, ddt_ref, dst):
        n, g = pl.program_id(0), pl.program_id(1)

        @pl.when(n == 0)
        def _():
            dst[g] = jnp.zeros((G, d, d), F32)

        @pl.when((n == 0) & (g == 0))
        def _():
            dal_ref[...] = jnp.zeros_like(dal_ref)
            ddt_ref[...] = jnp.zeros_like(ddt_ref)

        tok_in = lambda r: jnp.stack([r[bb, :, hh * d:(hh + 1) * d] for hh, bb in chains])
        head_in = lambda r: jnp.stack([r[hh] for hh, _ in chains])
        gate_in = lambda lane0: jnp.stack([_lane_column(z[bb], lane0 + g * HB + hh, d) for hh, bb in chains])
        _, vjp = jax.vjp(_dn_chunk, tok_in(q), tok_in(k), tok_in(v), gate_in(A_LANE), gate_in(B_LANE), head_in(al),
                         head_in(dt), s0_ref[...].reshape(G, d, d))
        dq, dkk, dvv, da, db, dal, ddt, ds0 = vjp((tok_in(do_ref), dst[g]))
        da, db = lanesum(da), lanesum(db)
        dal = jnp.broadcast_to(lanesum(dal), (G, 1, d))
        ddt = jnp.broadcast_to(lanesum(ddt), (G, 1, d))
        lane = lax.broadcasted_iota(jnp.int32, (CHUNK, ZS), 1)
        for bb in range(Bl):
            part = jnp.zeros((CHUNK, ZS), F32)
            for i, (hh, b2) in enumerate(chains):
                if b2 == bb:
                    h = g * HB + hh
                    part = part + jnp.where(lane == A_LANE + h, da[i], 0.0) + jnp.where(lane == B_LANE + h, db[i], 0.0)

            @pl.when(g == 0)
            def _():
                dzs_ref[bb] = jnp.where(lane < LOWRANK, dzg_ref[bb], 0.0) + part

            @pl.when(g > 0)
            def _():
                dzs_ref[bb] += part
        for i, (hh, bb) in enumerate(chains):
            cols = slice(hh * d, (hh + 1) * d)
            dq_ref[bb, :, cols] = dq[i]
            dk_ref[bb, :, cols] = dkk[i]
            dv_ref[bb, :, cols] = dvv[i]
            dal_ref[g * HB + hh] += dal[i]
            ddt_ref[g * HB + hh] += ddt[i]
        dst[g] = ds0

    rn = lambda n: NC - 1 - n
    tok = pl.BlockSpec((Bl, CHUNK, HB * d), lambda n, g: (0, rn(n), g))
    zsb = pl.BlockSpec((Bl, CHUNK, ZS), lambda n, g: (0, rn(n), 0))
    per_head = pl.BlockSpec((HB, 1, d), lambda n, g: (g, 0, 0))
    all_heads = pl.BlockSpec((DN_HEADS, 1, d), lambda n, g: (0, 0, 0))
    tok_shape = jax.ShapeDtypeStruct((Bl, S, D), F32)
    head_shape = jax.ShapeDtypeStruct((DN_HEADS, 1, d), F32)
    return pl.pallas_call(
        body, name="dn_bwd", grid=(NC, HG),
        in_specs=[tok, tok, tok, zsb, per_head, per_head,
                  pl.BlockSpec((HB, Bl, None, d, d), lambda n, g: (g, 0, rn(n), 0, 0)), tok, zsb],
        out_specs=[tok, tok, tok, zsb, all_heads, all_heads],
        out_shape=[tok_shape, tok_shape, tok_shape, jax.ShapeDtypeStruct((Bl, S, ZS), F32), head_shape, head_shape],
        scratch_shapes=[pltpu.VMEM((HG, G, d, d), F32)],
        compiler_params=_params(("arbitrary", "arbitrary")),
    )(qa, ka, va, z_small, alog, dtb, s_all, do, dzs_gla)


def _merge_specs(D, bt):
    dv, w = D // GLA_HEADS, D // DN_HEADS
    col = lambda off: pl.BlockSpec((bt, dv), lambda i, h: (i, off // dv + h))
    return dv, w, col


def _merge_load(refs, nsub, w):
    return [[r[:, s * w:(s + 1) * w] for s in range(nsub)] for r in refs]


def _merge_fwd(o_gla, o_dn, z_big, gla_norm, dn_norm, D, bt=256):
    T = o_gla.shape[0]
    bt = _pick(T, bt, SUBLANES)
    dv, w, col = _merge_specs(D, bt)
    nsub = dv // w

    def body(og, gg, od, dz, ga, gb, gn, dn, out):
        ogl, ggl, odl, dzl, gal, gbl = _merge_load([og, gg, od, dz, ga, gb], nsub, w)
        gnl = [gn[:, s * w:(s + 1) * w] for s in range(nsub)]
        outs = _merge_math(ogl, ggl, odl, dzl, gal, gbl, gnl, dn[...])
        for s in range(nsub):
            out[:, s * w:(s + 1) * w] = outs[s].astype(out.dtype)

    return pl.pallas_call(
        body, name="merge_fwd", grid=(T // bt, GLA_HEADS),
        in_specs=[col(0), col(2 * D), col(0), col(6 * D), col(7 * D), col(8 * D),
                  pl.BlockSpec((1, dv), lambda i, h: (0, 0)), pl.BlockSpec((1, w), lambda i, h: (0, 0))],
        out_specs=col(0),
        out_shape=jax.ShapeDtypeStruct((T, D), BF16),
        compiler_params=_params(("parallel", "parallel")),
    )(o_gla, z_big, o_dn, z_big, z_big, z_big, gla_norm, dn_norm)


def _merge_bwd(o_gla, o_dn, z_big, gla_norm, dn_norm, dmix, D, bt=256):
    T = o_gla.shape[0]
    bt = _pick(T, bt, SUBLANES)
    dv, w, col = _merge_specs(D, bt)
    nsub = dv // w

    def body(og, gg, od, dz, ga, gb, gn, dn, dm, dog, dgg, dod, ddz, dga, dgb, dgn, ddn):
        @pl.when((pl.program_id(0) == 0) & (pl.program_id(1) == 0))
        def _():
            dgn[...] = jnp.zeros_like(dgn)
            ddn[...] = jnp.zeros_like(ddn)

        ogl, ggl, odl, dzl, gal, gbl, dml = _merge_load([og, gg, od, dz, ga, gb, dm], nsub, w)
        gnl = [gn[:, s * w:(s + 1) * w] for s in range(nsub)]
        _, vjp = jax.vjp(_merge_math, ogl, ggl, odl, dzl, gal, gbl, gnl, dn[...])
        g_og, g_gg, g_od, g_dz, g_ga, g_gb, g_gn, g_dn = vjp(dml)
        for s in range(nsub):
            sl = slice(s * w, (s + 1) * w)
            dog[:, sl] = g_og[s]
            dgg[:, sl] = g_gg[s].astype(dgg.dtype)
            dod[:, sl] = g_od[s]
            ddz[:, sl] = g_dz[s].astype(ddz.dtype)
            dga[:, sl] = g_ga[s].astype(dga.dtype)
            dgb[:, sl] = g_gb[s].astype(dgb.dtype)
            dgn[:, sl] += g_gn[s]
        ddn[...] += g_dn

    f32s, bf16s = jax.ShapeDtypeStruct((T, D), F32), jax.ShapeDtypeStruct((T, D), BF16)
    return pl.pallas_call(
        body, name="merge_bwd", grid=(T // bt, GLA_HEADS),
        in_specs=[col(0), col(2 * D), col(0), col(6 * D), col(7 * D), col(8 * D),
                  pl.BlockSpec((1, dv), lambda i, h: (0, 0)), pl.BlockSpec((1, w), lambda i, h: (0, 0)), col(0)],
        out_specs=[col(0)] * 6 + [pl.BlockSpec((1, dv), lambda i, h: (0, 0)), pl.BlockSpec((1, w), lambda i, h: (0, 0))],
        out_shape=[f32s, bf16s, f32s, bf16s, bf16s, bf16s,
                   jax.ShapeDtypeStruct((1, dv), F32), jax.ShapeDtypeStruct((1, w), F32)],
        compiler_params=_params(("arbitrary", "arbitrary")),
    )(o_gla, z_big, o_dn, z_big, z_big, z_big, gla_norm, dn_norm, dmix)


def _place():
    return lax.axis_index("x"), lax.axis_index("y"), lax.axis_index("c")


def _other_chips(x, y):
    return [(1 - x, y), (x, 1 - y), (1 - x, 1 - y)]


def _rcopy(src, dst, send_sem, recv_sem, dev):
    return pltpu.make_async_remote_copy(src_ref=src, dst_ref=dst, send_sem=send_sem, recv_sem=recv_sem,
                                        device_id=dev, device_id_type=MESH)


ANY = pl.BlockSpec(memory_space=pl.ANY)


ROWS, COLS = 'rows', 'cols'


def _half(ref, hc, by, lead=()):
    shape = ref.shape[len(lead):]
    if by == ROWS:
        rh = shape[0] // 2
        idx = (pl.ds(pl.multiple_of(hc * rh, 16), rh),) + (slice(None),) * (len(shape) - 1)
    else:
        ch = shape[-1] // 2
        idx = (slice(None),) * (len(shape) - 1) + (pl.ds(pl.multiple_of(hc * ch, LANES), ch),)
    return ref.at[(*lead, *idx)]


def _half_shape(shape, by):
    return (shape[0] // 2,) + tuple(shape[1:]) if by == ROWS else tuple(shape[:-1]) + (shape[-1] // 2,)


def _gather_ici(shards, by):
    nw = len(shards)

    def copies(srcs, outs, send_sems, recv_sems):
        x, y, c = _place()
        return [_rcopy(_half(srcs[w], c, by[w]), _half(outs[w], c, by[w], (2 * x + y,)),
                       send_sems.at[3 * w + k], recv_sems.at[3 * w + k], (px, py, c))
                for w in range(nw) for k, (px, py) in enumerate(_other_chips(x, y))]

    return _Stage(shards, [jax.ShapeDtypeStruct((4,) + s.shape, s.dtype) for s in shards], 3 * nw, copies)


def _gather_pass(gathered, by):
    nw = len(gathered)

    def copies(srcs, outs, send_sems, recv_sems):
        x, y, c = _place()
        cps = []
        for w in range(nw):
            for k, (px, py) in enumerate(_other_chips(x, y)):
                slot = (2 * px + py,)
                cps.append(_rcopy(_half(srcs[w], c, by[w], slot), _half(outs[w], c, by[w], slot),
                                  send_sems.at[3 * w + k], recv_sems.at[3 * w + k], (x, y, 1 - c)))
        return cps

    return _Stage(gathered, [jax.ShapeDtypeStruct(g.shape, g.dtype) for g in gathered], 3 * nw, copies,
                  aliases={w: w for w in range(nw)})


def _pair_exchange(ps, by):
    nw = len(ps)

    def copies(srcs, outs, send_sems, recv_sems):
        x, y, c = _place()
        return [_rcopy(_half(srcs[w], 1 - c, by[w], (slice(None),)), outs[w], send_sems.at[w], recv_sems.at[w], (x, y, 1 - c))
                for w in range(nw)]

    return _Stage(ps, [jax.ShapeDtypeStruct((4,) + _half_shape(p.shape[1:], b), p.dtype) for p, b in zip(ps, by)], nw, copies)


def _sum_blocks(half_shape, by):
    rh, ch = half_shape
    if by == ROWS:
        lanes = -(-ch // LANES) * LANES
        bt = _pick(rh, max(16, (3 << 18) // lanes // 16 * 16), 16)
        return (bt, ch), rh // bt, lambda i: (i, 0)
    bc = _pick(ch, max(LANES, (5 << 18) // rh // LANES * LANES), LANES)
    return (rh, bc), ch // bc, lambda i: (0, i)


def _pair_sum(p, got, c_idx, name, by=ROWS):
    hs = got.shape[1:]
    blk, nb, pos = _sum_blocks(hs, by)

    def body(c_ref, a, b, of, ob):
        s = a[...] + b[...]
        of[...] = s
        ob[...] = s.astype(BF16)

    def mine(j, i, c_ref):
        r, cc = pos(c_ref[0] * nb + i)
        return (j, r, cc)

    spec = pl.BlockSpec((None,) + blk, lambda j, i, c_ref: (j,) + pos(i))
    return pl.pallas_call(
        body, name=name,
        grid_spec=pltpu.PrefetchScalarGridSpec(
            num_scalar_prefetch=1, grid=(4, nb),
            in_specs=[pl.BlockSpec((None,) + blk, mine), spec], out_specs=[spec, spec]),
        out_shape=[jax.ShapeDtypeStruct((4,) + hs, F32), jax.ShapeDtypeStruct((4,) + hs, BF16)],
        compiler_params=_params(("parallel", "parallel")),
    )(c_idx, p, got)


def _chip_scatter(qbs):
    nw = len(qbs)

    def copies(srcs, outs, send_sems, recv_sems):
        x, y, c = _place()
        return [_rcopy(srcs[w].at[2 * px + py], outs[w].at[k], send_sems.at[3 * w + k], recv_sems.at[3 * w + k], (px, py, c))
                for w in range(nw) for k, (px, py) in enumerate(_other_chips(x, y))]

    return _Stage(qbs, [jax.ShapeDtypeStruct((3,) + q.shape[1:], q.dtype) for q in qbs], 3 * nw, copies)


def _final_sum(qf, got, me_idx, name, by=ROWS):
    hs = qf.shape[1:]
    blk, nb, pos = _sum_blocks(hs, by)

    def body(me_ref, a, b, o):
        o[...] = ((a[...] + b[0].astype(F32)) + b[1].astype(F32)) + b[2].astype(F32)

    return pl.pallas_call(
        body, name=name,
        grid_spec=pltpu.PrefetchScalarGridSpec(
            num_scalar_prefetch=1, grid=(nb,),
            in_specs=[pl.BlockSpec((None,) + blk, lambda i, me_ref: (me_ref[0],) + pos(i)),
                      pl.BlockSpec((3,) + blk, lambda i, me_ref: (0,) + pos(i))],
            out_specs=pl.BlockSpec(blk, lambda i, me_ref: pos(i))),
        out_shape=jax.ShapeDtypeStruct(hs, F32),
        compiler_params=_params(("parallel",)),
    )(me_idx, qf, got)


def _pair_allgather(halves, by):
    nw = len(halves)
    whole = [(2 * h.shape[0], h.shape[1]) if b == ROWS else (h.shape[0], 2 * h.shape[1]) for h, b in zip(halves, by)]

    def body(*refs):
        srcs, outs, send_sems, recv_sems = refs[:nw], refs[nw:2 * nw], refs[2 * nw], refs[2 * nw + 1]
        x, y, c = _place()
        cps = []
        for w in range(nw):
            cp = _rcopy(srcs[w], _half(outs[w], c, by[w]), send_sems.at[w], recv_sems.at[w], (x, y, 1 - c))
            cp.start()
            cps.append(cp)
        for w in range(nw):
            got = _half(outs[w], 1 - c, by[w])
            _rcopy(got, got, send_sems.at[w], recv_sems.at[w], (x, y, 1 - c)).wait_recv()
        for cp in cps:
            cp.wait_send()

    return pl.pallas_call(
        body, name="grad_pair_allgather", in_specs=[ANY] * nw, out_specs=[ANY] * nw,
        out_shape=[jax.ShapeDtypeStruct(s, h.dtype) for s, h in zip(whole, halves)],
        scratch_shapes=[pltpu.SemaphoreType.DMA((nw,)), pltpu.SemaphoreType.DMA((nw,))],
    )(*halves)


def _small_exchange(items, out_shapes, finish, name):
    n = len(items)
    offs, rows = [], 0
    for it in items:
        offs.append(rows)
        rows += -(-it.shape[0] // SUBLANES) * SUBLANES
    width = -(-max(it.shape[1] for it in items) // LANES) * LANES
    VMEM = pl.BlockSpec(memory_space=pltpu.VMEM)

    def body(*refs):
        ins, outs = refs[:n], refs[n:n + len(out_shapes)]
        buf, send_sems, recv_sems = refs[n + len(out_shapes):]
        x, y, c = _place()
        me = 4 * x + 2 * y + c
        flip = lambda v, f: (1 - v) if f else v
        peers = [(flip(x, r >> 2 & 1), flip(y, r >> 1 & 1), flip(c, r & 1)) for r in range(1, 8)]
        buf[me] = jnp.zeros((rows, width), F32)
        for it, off, ref in zip(items, offs, ins):
            buf[me, off:off + it.shape[0], 0:it.shape[1]] = ref[...]
        cps = [_rcopy(buf.at[me], buf.at[me], send_sems.at[k], recv_sems.at[k], dev) for k, dev in enumerate(peers)]
        for cp in cps:
            cp.start()
        for k, (px, py, pc) in enumerate(peers):
            slot = buf.at[4 * px + 2 * py + pc]
            _rcopy(slot, slot, send_sems.at[k], recv_sems.at[k], (px, py, pc)).wait_recv()
        for cp in cps:
            cp.wait_send()
        finish(buf, offs, outs)

    return pl.pallas_call(
        body, name=name, in_specs=[VMEM] * n, out_specs=[VMEM] * len(out_shapes),
        out_shape=[jax.ShapeDtypeStruct(s, F32) for s in out_shapes],
        scratch_shapes=[pltpu.VMEM((8, rows, width), F32), pltpu.SemaphoreType.DMA((7,)), pltpu.SemaphoreType.DMA((7,))],
        compiler_params=pltpu.CompilerParams(vmem_limit_bytes=VMEM_LIMIT_BYTES),
    )(*items)


def _allreduce_small(items, name):
    def finish(buf, offs, outs):
        for it, off, out in zip(items, offs, outs):
            region = lambda d: buf[d, off:off + it.shape[0], 0:it.shape[1]]
            s = region(0)
            for d in range(1, 8):
                s = s + region(d)
            out[...] = s
    return _small_exchange(items, [it.shape for it in items], finish, name)


def _allgather_small_shards(items, name):
    def finish(buf, offs, outs):
        for it, off, out in zip(items, offs, outs):
            r, c = it.shape
            for j in range(4):
                out[:, j * c:(j + 1) * c] = buf[2 * j, off:off + r, 0:c]
    return _small_exchange(items, [(it.shape[0], 4 * it.shape[1]) for it in items], finish, name)


def _split_w_in(wt, D):
    pad = jnp.zeros((ZS - 3 * LOWRANK, wt.shape[1]), wt.dtype)
    big = jnp.concatenate([wt[:3 * D], wt[3 * D + 16:6 * D + 16], wt[6 * D + 16:7 * D + 16], wt[7 * D + 48:]], axis=0)
    small = jnp.concatenate([wt[3 * D:3 * D + 16], wt[7 * D + 16:7 * D + 48], pad], axis=0)
    return big, small


def _join_w_in(gb, gs, D):
    return jnp.concatenate([gb[:3 * D], gs[:16], gb[3 * D:6 * D], gb[6 * D:7 * D], gs[16:48], gb[7 * D:9 * D]], axis=0)


def kernel(x, p, g_mix, w_in, gla_w2, gla_b, gla_norm, dn_conv, dn_a_log, dn_dt_bias, dn_norm, w_out, g_mlp, w_up, w_down, g_ple, w_ple_gate, w_ple_proj, g_final, loss_target, m_g_mix, m_w_in, m_gla_w2, m_gla_b, m_gla_norm, m_dn_conv, m_dn_a_log, m_dn_dt_bias, m_dn_norm, m_w_out, m_g_mlp, m_w_up, m_w_down, m_g_ple, m_w_ple_gate, m_w_ple_proj, m_g_final, v_g_mix, v_w_in, v_gla_w2, v_gla_b, v_gla_norm, v_dn_conv, v_dn_a_log, v_dn_dt_bias, v_dn_norm, v_w_out, v_g_mlp, v_w_up, v_w_down, v_g_ple, v_w_ple_gate, v_w_ple_proj, v_g_final):
    wts = dict(zip(WEIGHTS, [g_mix, w_in, gla_w2, gla_b, gla_norm, dn_conv, dn_a_log, dn_dt_bias, dn_norm, w_out, g_mlp,
                             w_up, w_down, g_ple, w_ple_gate, w_ple_proj, g_final]))
    mom = dict(zip(WEIGHTS, [m_g_mix, m_w_in, m_gla_w2, m_gla_b, m_gla_norm, m_dn_conv, m_dn_a_log, m_dn_dt_bias, m_dn_norm,
                             m_w_out, m_g_mlp, m_w_up, m_w_down, m_g_ple, m_w_ple_gate, m_w_ple_proj, m_g_final]))
    var = dict(zip(WEIGHTS, [v_g_mix, v_w_in, v_gla_w2, v_gla_b, v_gla_norm, v_dn_conv, v_dn_a_log, v_dn_dt_bias, v_dn_norm,
                             v_w_out, v_g_mlp, v_w_up, v_w_down, v_g_ple, v_w_ple_gate, v_w_ple_proj, v_g_final]))
    Bl, S, D = x.shape
    T = Bl * S
    PLE = p.shape[-1]
    dn_d, gla_dk = D // DN_HEADS, D // (2 * GLA_HEADS)
    ix, iy, ic = _place()
    j_me = 2 * ix + iy
    as2d = lambda a: a.reshape(a.shape[-2], a.shape[-1]) if a.ndim > 1 else a.reshape(1, -1)
    c_idx, me_idx = ic.reshape(1).astype(jnp.int32), j_me.reshape(1).astype(jnp.int32)

    rows_first = lambda a: jnp.transpose(a, (2, 0, 1))
    cols_last = lambda a: jnp.transpose(a, (1, 2, 0))
    w_in_t, m_in_t, v_in_t = rows_first(w_in), rows_first(m_w_in), rows_first(v_w_in)
    n_in = w_in_t.shape[0]
    shard2d = {n: as2d(wts[n]) for n, _ in BIG[1:]}
    bf16_shards = [w_in_t.astype(BF16).reshape(n_in, D)] + [shard2d[n].astype(BF16) for n, _ in BIG[1:]]
    split = [COLS] + [ROWS] * (len(BIG) - 1)
    own_slot = lambda g, s: lax.dynamic_update_slice(g, s[None], (j_me, 0, 0))
    (w_in_ici,) = _run_stage(_gather_ici(bf16_shards[:1], split[:1]), "allgather_w_in_ici")
    (w_in_all,) = _run_stage(_gather_pass([w_in_ici], split[:1]), "allgather_w_in_pass")
    w_in_slots = own_slot(w_in_all, bf16_shards[0])
    w_big, w_small = _split_w_in(w_in_slots.reshape(4 * n_in, D), D)

    w2_full, conv_full = _allgather_small_shards([as2d(gla_w2), as2d(dn_conv)], "allgather_small_weights")
    w2pad = jnp.pad(w2_full, ((0, ZS - LOWRANK), (0, 0)))
    w2h = jnp.swapaxes(w2pad.reshape(ZS, GLA_HEADS, gla_dk), 0, 1)
    gbh = gla_b.reshape(GLA_HEADS, 1, gla_dk)
    alog_w = jnp.broadcast_to(dn_a_log.reshape(DN_HEADS, 1, 1), (DN_HEADS, 1, dn_d))
    dtb_w = jnp.broadcast_to(dn_dt_bias.reshape(DN_HEADS, 1, 1), (DN_HEADS, 1, dn_d))

    xt = x.reshape(T, D)
    tgt = loss_target.reshape(T, D)
    pt = p.reshape(T, PLE)
    seq = lambda t: t.reshape(Bl, S, t.shape[-1])
    tok = lambda t: t.reshape(T, t.shape[-1])
    h = _rmsnorm_fwd(xt, g_mix, "rms1_fwd")
    z_big, *rest_ici = _matmul(h, w_big, 'nt', [F32], "proj_in", stage=_gather_ici(bf16_shards[1:], split[1:]))
    z_small, *rest_all = _matmul(h, w_small, 'nt', [F32], "proj_in_narrow", stage=_gather_pass(rest_ici, split[1:]))
    slots = {n: own_slot(g, s) for (n, _), g, s in zip(BIG[1:], rest_all, bf16_shards[1:])}
    rows_joined = lambda t: t.reshape(4 * t.shape[1], t.shape[2])
    w_out_f, w_down_f, w_pg_f = rows_joined(slots['w_out']), rows_joined(slots['w_down']), rows_joined(slots['w_ple_gate'])
    w_up_s, w_pp_s = slots['w_up'], slots['w_ple_proj']
    o_gla, st_all = _gla_fwd(seq(z_big), seq(z_small), w2h, gbh, Bl, S, D)
    acts = [_conv_fwd(z_big, conv_full, grp, Bl, S, D) for grp in range(3)]
    o_dn, s_all = _dn_fwd(seq(acts[0]), seq(acts[1]), seq(acts[2]), seq(z_small), alog_w, dtb_w, Bl, S, D)
    mixed = _merge_fwd(tok(o_gla), tok(o_dn), z_big, gla_norm, dn_norm, D)
    (x1,) = _matmul(mixed, w_out_f, 'nn', [F32], "proj_out", epilogue=lambda r, e: (e + r,), extras=(xt,), bm=512)
    h2 = _rmsnorm_fwd(x1, g_mlp, "rms2_fwd")
    u, act = _matmul(h2, w_up_s, 'nn', [F32, BF16], "mlp_up", b_slots=True,
                     epilogue=lambda r: (r, jnp.square(jnp.maximum(r, 0.0))))
    (x2,) = _matmul(act, w_down_f, 'nn', [F32], "mlp_down", epilogue=lambda r, e: (e + r,), extras=(x1,), bm=512)
    h3 = _rmsnorm_fwd(x2, g_ple, "rms3_fwd")
    (pp,) = _matmul(pt, w_pp_s, 'nn', [F32], "ple_proj", b_slots=True)
    gp, x3 = _matmul(h3, w_pg_f, 'nn', [F32, F32], "ple_gate",
                     epilogue=lambda r, e, q: (r, e + _sigmoid(r) * q), extras=(x2, pp), bm=512)
    dx3, loss_tile, d_g_final = _loss_fwd_bwd(x3, g_final.reshape(1, D), tgt, "loss")

    d_gp, d_pp = _ple_bwd(dx3, gp, pp, "ple_bwd")
    (g_pp,) = _matmul(pt, d_pp, 'tn', [F32], "ple_proj_dw", out_slots=True)
    (g_pg,) = _matmul(h3, d_gp, 'tn', [F32], "ple_gate_dw")
    (dh3,) = _matmul(d_gp, w_pg_f, 'nt', [F32], "ple_gate_dx")
    dx2, dx2b, d_g_ple = _rmsnorm_bwd_add(x2, g_ple, dh3, dx3, "rms3_bwd")
    (g_down,) = _matmul(act, dx2b, 'tn', [F32], "mlp_down_dw")
    (du,) = _matmul(dx2b, w_down_f, 'nt', [BF16], "mlp_down_dx",
                    epilogue=lambda r, e: (r * 2.0 * jnp.maximum(e, 0.0),), extras=(u,))
    (g_up,) = _matmul(h2, du, 'tn', [F32], "mlp_up_dw", out_slots=True)
    by_rows = lambda g: g.reshape(4, g.shape[0] // 4, g.shape[1])
    send_mlp = [g_up, by_rows(g_down), by_rows(g_pg), g_pp]
    dh2, *sib_mlp = _matmul(du, w_up_s, 'nt', [F32], "mlp_up_dx", b_slots=True, stage=_pair_exchange(send_mlp, split[2:]))
    dx1, dx1b, d_g_mlp = _rmsnorm_bwd_add(x1, g_mlp, dh2, dx2, "rms2_bwd")
    (g_out,) = _matmul(mixed, dx1b, 'tn', [F32], "proj_out_dw")
    dmix, sib_out = _matmul(dx1b, w_out_f, 'nt', [F32], "proj_out_dx", stage=_pair_exchange([by_rows(g_out)], split[1:2]))
    rest = [n for n, _ in BIG[1:]]
    send_rest, sib_rest = [by_rows(g_out)] + send_mlp, [sib_out] + sib_mlp
    sums_rest = [_pair_sum(s, f, c_idx, f"grad_pair_sum_{n}") for n, s, f in zip(rest, send_rest, sib_rest)]
    d_ogla, d_gg, d_odn, d_dz, d_ga, d_gb, d_gla_norm, d_dn_norm = _merge_bwd(
        tok(o_gla), tok(o_dn), z_big, gla_norm, dn_norm, dmix, D)
    d_q, d_k, d_v, dzs_gla, d_w2h, d_gbh = _gla_bwd(seq(z_big), seq(z_small), w2h, gbh, st_all, seq(d_ogla), Bl, S, D)
    d_qa, d_ka, d_va, d_zs, d_alog_w, d_dtb_w = _dn_bwd(seq(acts[0]), seq(acts[1]), seq(acts[2]), seq(z_small), alog_w,
                                                        dtb_w, s_all, seq(d_odn), dzs_gla, Bl, S, D)
    conv_b = [_conv_bwd(z_big, conv_full, tok(g), grp, Bl, S, D) for grp, g in enumerate([d_qa, d_ka, d_va])]
    dz_big = jnp.concatenate([tok(d_q), tok(d_k), tok(d_v), d_gg, conv_b[0][0], conv_b[1][0], conv_b[2][0], d_dz, d_ga,
                              d_gb], axis=1)
    dz_small = tok(d_zs)
    d_w_big, *chips_rest = _matmul(dz_big, h, 'tn', [F32], "proj_in_dw", stage=_chip_scatter([b for _, b in sums_rest]))
    halves_rest = [_final_sum(f, got, me_idx, f"grad_final_sum_{n}") for n, (f, _), got in zip(rest, sums_rest, chips_rest)]
    (d_w_small,) = _matmul(dz_small, h, 'tn', [F32], "proj_in_narrow_dw")
    g_in = _join_w_in(d_w_big, d_w_small, D).reshape(4, n_in, D)
    (sib_in,) = _run_stage(_pair_exchange([g_in], split[:1]), "grad_pair_exchange_w_in")
    sum_in_f32, sum_in_bf16 = _pair_sum(g_in, sib_in, c_idx, "grad_pair_sum_w_in", split[0])
    dh_a, chips_in = _matmul(dz_big, w_big, 'nn', [F32], "proj_in_dx", stage=_chip_scatter([sum_in_bf16]))
    half_in = _final_sum(sum_in_f32, chips_in, me_idx, "grad_final_sum_w_in", split[0])
    (dh,) = _matmul(dz_small, w_small, 'nn', [F32], "proj_in_narrow_dx", epilogue=lambda r, e: (e + r,), extras=(dh_a,))
    grad_x, _, d_g_mix = _rmsnorm_bwd_add(xt, g_mix, dh, dx1, "rms1_bwd")
    my_halves = [half_in] + halves_rest
    own_half = lambda o, hlf, b: lax.dynamic_update_slice(
        o, hlf, (ic * hlf.shape[0], 0) if b == ROWS else (0, ic * hlf.shape[1]))
    reduced = {n: own_half(o, hlf, b) for (n, _), o, hlf, b in zip(BIG, _pair_allgather(my_halves, split), my_halves, split)}

    d_w2 = jnp.swapaxes(d_w2h, 0, 1).reshape(ZS, D // 2)[:LOWRANK]
    small_grads = {'g_mix': d_g_mix, 'gla_w2': d_w2, 'gla_b': d_gbh.reshape(1, D // 2), 'gla_norm': d_gla_norm,
                   'dn_a_log': d_alog_w[:, 0, 0].reshape(1, DN_HEADS), 'dn_dt_bias': d_dtb_w[:, 0, 0].reshape(1, DN_HEADS),
                   'dn_norm': d_dn_norm, 'g_mlp': d_g_mlp, 'g_ple': d_g_ple, 'g_final': d_g_final}
    names = [n for n in SMALL if n != 'dn_conv']
    total = _allreduce_small([small_grads[n] for n in names] + [cb[1] for cb in conv_b] + [loss_tile],
                             "allreduce_small_grads")
    gsmall = dict(zip(names, total[:len(names)]))
    loss = total[-1][0, 0]
    my_cols = lambda g: lax.dynamic_slice_in_dim(g, j_me * (g.shape[1] // 4), g.shape[1] // 4, axis=1)
    gsmall['gla_w2'] = my_cols(gsmall['gla_w2'])
    gsmall['dn_conv'] = my_cols(jnp.concatenate(total[len(names):len(names) + 3], axis=1))

    g_o, d_o, m_o, v_o = {}, {}, {}, {}
    g_in_t = reduced['w_in'].reshape(n_in, 1, D)
    d_in_t, nm_in_t, nv_in_t = _adamw(w_in_t, g_in_t, m_in_t, v_in_t, "adamw_w_in")
    g_o['w_in'], d_o['w_in'], m_o['w_in'], v_o['w_in'] = [cols_last(t) for t in (g_in_t, d_in_t, nm_in_t, nv_in_t)]
    for n, _ in BIG[1:]:
        shp = wts[n].shape
        d2, nm2, nv2 = _adamw(shard2d[n], reduced[n], as2d(mom[n]), as2d(var[n]), f"adamw_{n}")
        g_o[n], d_o[n], m_o[n], v_o[n] = reduced[n].reshape(shp), d2.reshape(shp), nm2.reshape(shp), nv2.reshape(shp)
    ds, nms, nvs = _adamw_small([as2d(wts[n]) for n in SMALL], [as2d(gsmall[n]) for n in SMALL],
                                [as2d(mom[n]) for n in SMALL], [as2d(var[n]) for n in SMALL])
    for n, dd, mm, vv in zip(SMALL, ds, nms, nvs):
        shp = wts[n].shape
        g_o[n], d_o[n], m_o[n], v_o[n] = gsmall[n].reshape(shp), dd.reshape(shp), mm.reshape(shp), vv.reshape(shp)

    return (loss, grad_x.reshape(Bl, S, D), *[g_o[n] for n in WEIGHTS], *[d_o[n] for n in WEIGHTS],
            *[m_o[n] for n in WEIGHTS], *[v_o[n] for n in WEIGHTS])
```

```python
import functools

import jax
import jax.numpy as jnp
from jax import lax
from jax.experimental import pallas as pl
from jax.experimental.pallas import tpu as pltpu

F32 = jnp.float32
BF16 = jnp.bfloat16

CHUNK = 64
GLA_HEADS = 4
DN_HEADS = 16
LOWRANK = 16
GLA_TAU = 16.0
DN_CONV = 4
EPS = 1e-6
ZS = 128
A_LANE, B_LANE = LOWRANK, LOWRANK + DN_HEADS
ADAM_LR, ADAM_B1, ADAM_B2, ADAM_EPS, ADAM_WD, ADAM_STEP = 0.001, 0.9, 0.999, 1e-08, 0.01, 10

V7X_VMEM_BYTES = 64 * 1024 * 1024
VMEM_LIMIT_BYTES = V7X_VMEM_BYTES - 8 * 1024 * 1024
LANES = 128
SUBLANES = 8
MESH = pl.DeviceIdType.MESH
DN_HEADS_PER_STEP = 4
GLA_HEADS_PER_STEP = 2

WEIGHTS = ['g_mix', 'w_in', 'gla_w2', 'gla_b', 'gla_norm', 'dn_conv', 'dn_a_log', 'dn_dt_bias', 'dn_norm', 'w_out',
           'g_mlp', 'w_up', 'w_down', 'g_ple', 'w_ple_gate', 'w_ple_proj', 'g_final']
BIG = [('w_in', 1), ('w_out', 0), ('w_up', 1), ('w_down', 0), ('w_ple_gate', 0), ('w_ple_proj', 1)]
SMALL = [n for n in WEIGHTS if n not in dict(BIG)]

_NN, _NT, _TN = 'nn', 'nt', 'tn'


def _params(sem=None):
    return pltpu.CompilerParams(dimension_semantics=sem, vmem_limit_bytes=VMEM_LIMIT_BYTES)


def _dot(a, b, form, precision=None):
    o = a.ndim - 2
    contract = {_NN: ((1 + o,), (o,)), _NT: ((1 + o,), (1 + o,)), _TN: ((o,), (o,))}[form]
    batch = ((0,), (0,)) if o else ((), ())
    return lax.dot_general(a, b, (contract, batch), precision=precision, preferred_element_type=F32)


def _make_mm(cast, precision):
    def raw(a, b, dims):
        return _dot(cast(a), cast(b), dims, precision)

    @jax.custom_vjp
    def nn(a, b):
        return raw(a, b, _NN)
    nn.defvjp(lambda a, b: (raw(a, b, _NN), (a, b)), lambda r, g: (raw(g, r[1], _NT), raw(r[0], g, _TN)))

    @jax.custom_vjp
    def nt(a, b):
        return raw(a, b, _NT)
    nt.defvjp(lambda a, b: (raw(a, b, _NT), (a, b)), lambda r, g: (raw(g, r[1], _NN), raw(g, r[0], _TN)))

    @jax.custom_vjp
    def tn(a, b):
        return raw(a, b, _TN)
    tn.defvjp(lambda a, b: (raw(a, b, _TN), (a, b)), lambda r, g: (raw(r[1], g, _NT), raw(r[0], g, _NN)))
    return nn, nt, tn


_bnn, _bnt, _btn = _make_mm(lambda t: t.astype(BF16), None)
TRI_PRECISION = lax.Precision.HIGH


def _iota2(n, axis):
    return lax.broadcasted_iota(jnp.int32, (n, n), axis)


def _lower(n, strict=False):
    return (_iota2(n, 0) > _iota2(n, 1)) if strict else (_iota2(n, 0) >= _iota2(n, 1))


def _tri_times(tri, x):
    tri = tri.astype(F32)
    if x.ndim == 3:
        tri = jnp.broadcast_to(tri, (x.shape[0],) + tri.shape)
    return _dot(tri, x, _NN, lax.Precision.HIGHEST)


@jax.custom_vjp
def _cumsum_rows(x):
    return _tri_times(_lower(x.shape[-2]), x)


def _cumsum_rows_bwd(_, g):
    n = g.shape[-2]
    return (_tri_times(_iota2(n, 0) <= _iota2(n, 1), g),)


_cumsum_rows.defvjp(lambda x: (_cumsum_rows(x), None), _cumsum_rows_bwd)


def _tri_inv_impl(a):
    n = a.shape[-1]
    eye = (_iota2(n, 0) == _iota2(n, 1)).astype(F32)
    p = eye - a
    ak = a
    k = 2
    while k < n:
        prec, cast = (TRI_PRECISION, lambda t: t) if k == 2 else (None, lambda t: t.astype(BF16))
        ak = _dot(cast(ak), cast(ak), _NN, prec)
        p = p + _dot(cast(p), cast(ak), _NN, prec)
        k *= 2
    return p


@jax.custom_vjp
def _tri_inv(a):
    return _tri_inv_impl(a)


def _tri_inv_fwd(a):
    t = _tri_inv_impl(a)
    return t, t


def _tri_inv_bwd(t, g):
    tb = t.astype(BF16)
    tg = _dot(tb, g.astype(BF16), _TN)
    return (-_dot(tg.astype(BF16), tb, _NT),)


_tri_inv.defvjp(_tri_inv_fwd, _tri_inv_bwd)


def _shift_rows(x, s, down):
    n = x.shape[0]
    r = lax.broadcasted_iota(jnp.int32, x.shape, 0)
    if down:
        return jnp.where(r >= s, pltpu.roll(x, s, 0), 0.0)
    return jnp.where(r < n - s, pltpu.roll(x, n - s, 0), 0.0)


def _make_shift(s):
    @jax.custom_vjp
    def f(x):
        return _shift_rows(x, s, True)
    f.defvjp(lambda x: (_shift_rows(x, s, True), None), lambda _, g: (_shift_rows(g, s, False),))
    return f


def _sigmoid(x):
    return jax.nn.sigmoid(x)


def _silu(x):
    return x * jax.nn.sigmoid(x)


def _softplus(x):
    return jnp.maximum(x, 0.0) + jnp.log1p(jnp.exp(-jnp.abs(x)))


def _log_sigmoid(x):
    return -_softplus(-x)


def _rms(x, g):
    return x * lax.rsqrt(jnp.mean(x * x, axis=-1, keepdims=True) + EPS) * g


def _gla_chunk(q, k, v, zs, w2, gb, st, *, scale):
    c = q.shape[-2]
    logf = _log_sigmoid(_bnn(zs, w2) + gb) * (1.0 / GLA_TAU)
    bcum = _cumsum_rows(logf)
    b_last = jnp.sum(logf, axis=-2, keepdims=True)
    q_in = (q * scale) * jnp.exp(bcum)
    k_in = k * jnp.exp(-bcum)
    a = jnp.where(_lower(c), _bnt(q_in, k_in), 0.0)
    o = _bnn(a, v) + _bnt(q_in, st)
    k_dec = k * jnp.exp(b_last - bcum)
    st_new = st * jnp.exp(b_last) + _btn(v, k_dec)
    return o, st_new


def _dn_chunk(q, k, v, aw, bw, alog, dtb, s):
    c = q.shape[-2]
    incl, strict = _lower(c), _lower(c, True)
    g_w = -jnp.exp(alog) * _softplus(aw + dtb)
    beta_w = _sigmoid(bw)
    gcum_w = _cumsum_rows(g_w)
    lane0 = lax.broadcasted_iota(jnp.int32, gcum_w.shape, gcum_w.ndim - 1) == 0
    gcol = jnp.sum(jnp.where(lane0, gcum_w, 0.0), axis=-1, keepdims=True)
    d1 = jnp.broadcast_to(gcol, gcol.shape[:-1] + (c,))
    diff = jnp.where(incl, d1 - jnp.swapaxes(d1, -1, -2), 0.0)
    decay = jnp.where(incl, jnp.exp(diff), 0.0)
    k_beta = k * beta_w
    a = jnp.where(strict, _bnt(k_beta, k) * decay, 0.0)
    t = _tri_inv(a)
    egc = jnp.exp(gcum_w)
    u = _bnn(t, v * beta_w)
    w = _bnn(t, k_beta * egc)
    attn = jnp.where(incl, _bnt(q, k) * decay, 0.0)
    q_dec = q * egc
    g_last = jnp.sum(g_w, axis=-2, keepdims=True)
    k_dec = k * jnp.exp(g_last - gcum_w)
    v_new = u - _bnn(w, s)
    o = _bnn(q_dec, s) + _bnn(attn, v_new)
    s_new = s * jnp.exp(g_last) + _btn(k_dec, v_new)
    return o, s_new


def _conv_act(x, wrows, *, l2, scale):
    taps = len(wrows)
    y = None
    for j in range(taps):
        s = taps - 1 - j
        xs = x if s == 0 else _make_shift(s)(x)
        y = wrows[j] * xs if y is None else y + wrows[j] * xs
    y = _silu(y)
    if l2:
        y = y * lax.rsqrt(jnp.sum(y * y, axis=-1, keepdims=True) + EPS) * scale
    return y


def _merge_math(og, gg, od, dz, ga, gb, gn, dn):
    nsub = len(og)
    dv = nsub * og[0].shape[1]
    ssq = jnp.sum(og[0] * og[0], axis=-1, keepdims=True)
    for s in range(1, nsub):
        ssq = ssq + jnp.sum(og[s] * og[s], axis=-1, keepdims=True)
    r = lax.rsqrt(ssq * (1.0 / dv) + EPS)
    outs = []
    for s in range(nsub):
        a = og[s] * r * gn[s] * _silu(gg[s])
        b = _rms(od[s], dn) * _silu(dz[s])
        outs.append(_sigmoid(ga[s]) * a + _sigmoid(gb[s]) * b)
    return outs


def _pick(n, target, mult):
    best = None
    for d in range(mult, min(n, target) + 1, mult):
        if n % d == 0:
            best = d
    return best if best is not None else n


class _Stage:
    def __init__(self, inputs, out_shapes, n_sems, copies, aliases=None):
        self.inputs, self.out_shapes, self.n_sems, self.copies = list(inputs), list(out_shapes), n_sems, copies
        self.aliases = aliases or {}

    @property
    def sems(self):
        return [pltpu.SemaphoreType.DMA((self.n_sems,)), pltpu.SemaphoreType.DMA((self.n_sems,))]


def _host_stage(body, stage, n_in, n_out, grid):
    ci, co = len(stage.inputs), len(stage.out_shapes)

    def wrapped(*refs):
        ins, cins = refs[:n_in], refs[n_in:n_in + ci]
        outs, couts = refs[n_in + ci:n_in + ci + n_out], refs[n_in + ci + n_out:n_in + ci + n_out + co]
        scratch, sems = refs[n_in + ci + n_out + co:-2], refs[-2:]
        ids = [pl.program_id(d) for d in range(len(grid))]
        first, last = ids[0] == 0, ids[0] == grid[0] - 1
        for i, g in zip(ids[1:], grid[1:]):
            first, last = first & (i == 0), last & (i == g - 1)

        @pl.when(first)
        def _():
            for cp in stage.copies(cins, couts, *sems):
                cp.start()

        body(*ins, *outs, *scratch)

        @pl.when(last)
        def _():
            for cp in stage.copies(cins, couts, *sems):
                cp.wait()

    return wrapped


def _run_stage(stage, name):
    ci = len(stage.inputs)

    def body(*refs):
        cps = stage.copies(refs[:ci], refs[ci:-2], *refs[-2:])
        for cp in cps:
            cp.start()
        for cp in cps:
            cp.wait()

    return pl.pallas_call(body, name=name, in_specs=[ANY] * ci, out_specs=[ANY] * len(stage.out_shapes),
                          out_shape=stage.out_shapes, scratch_shapes=stage.sems,
                          input_output_aliases=dict(stage.aliases))(*stage.inputs)


def _matmul(a, b, form, out_dtypes, name, epilogue=None, extras=(), bm=1024, bn=1024, bk=2048,
            b_slots=False, out_slots=False, stage=None):
    ns, c = (b.shape[0], b.shape[2]) if b_slots else (1, None)
    b2 = b.shape[1:] if b_slots else b.shape
    if form == 'nn':
        (M, K), (K2, N) = a.shape, (b2[0], b2[1] * ns)
    elif form == 'nt':
        (M, K), (N, K2) = a.shape, (b2[0], b2[1] * ns)
    else:
        (K, M), (K2, N) = a.shape, b2
    assert K == K2 and not (b_slots and form == 'tn'), (a.shape, b.shape, form)
    bm, bn, bk = _pick(M, bm, SUBLANES), _pick(N, bn, LANES), _pick(K, bk, LANES)
    if b_slots:
        bn, bk = (_pick(c, bn, LANES), bk) if form == 'nn' else (bn, _pick(c, bk, LANES))
    if out_slots:
        oc = N // 4
        bn = _pick(oc, bn, LANES)
    nk = K // bk
    a_spec = pl.BlockSpec((bk, bm), lambda i, j, k: (k, i)) if form == 'tn' else pl.BlockSpec((bm, bk), lambda i, j, k: (i, k))
    if b_slots and form == 'nn':
        per = c // bn
        b_spec = pl.BlockSpec((None, bk, bn), lambda i, j, k: (j // per, k, j % per))
    elif b_slots:
        per = c // bk
        b_spec = pl.BlockSpec((None, bn, bk), lambda i, j, k: (k // per, j, k % per))
    elif form == 'nt':
        b_spec = pl.BlockSpec((bn, bk), lambda i, j, k: (j, k))
    else:
        b_spec = pl.BlockSpec((bk, bn), lambda i, j, k: (k, j))
    o_spec = pl.BlockSpec((bm, bn), lambda i, j, k: (i, j))
    if out_slots:
        oper = oc // bn
        out_spec = pl.BlockSpec((None, bm, bn), lambda i, j, k: (j // oper, i, j % oper))
        out_shape = [jax.ShapeDtypeStruct((4, M, oc), d) for d in out_dtypes]
    else:
        out_spec = o_spec
        out_shape = [jax.ShapeDtypeStruct((M, N), d) for d in out_dtypes]
    ne, no = len(extras), len(out_dtypes)

    def finish(r, extra_refs, out_refs):
        outs = (r,) if epilogue is None else epilogue(r, *[e[...] for e in extra_refs])
        for ref, o in zip(out_refs, outs):
            ref[...] = o.astype(ref.dtype)

    def body_one(a_ref, b_ref, *rest):
        finish(_dot(a_ref[...].astype(BF16), b_ref[...].astype(BF16), form), rest[:ne], rest[ne:ne + no])

    def body_acc(a_ref, b_ref, *rest):
        extra_refs, out_refs, acc = rest[:ne], rest[ne:ne + no], rest[ne + no]
        k = pl.program_id(2)
        part = _dot(a_ref[...].astype(BF16), b_ref[...].astype(BF16), form)

        @pl.when(k == 0)
        def _():
            acc[...] = part

        @pl.when((k > 0) & (k < nk - 1))
        def _():
            acc[...] += part

        @pl.when(k == nk - 1)
        def _():
            finish(acc[...] + part, extra_refs, out_refs)

    body = body_one if nk == 1 else body_acc
    grid = (M // bm, N // bn, nk)
    scratch = [] if nk == 1 else [pltpu.VMEM((bm, bn), F32)]
    if stage is None:
        return pl.pallas_call(
            body, name=name, grid=grid, in_specs=[a_spec, b_spec] + [o_spec] * ne,
            out_specs=[out_spec] * no, out_shape=out_shape, scratch_shapes=scratch,
            compiler_params=_params(("parallel", "parallel", "arbitrary")),
        )(a, b, *extras)
    ci, co = len(stage.inputs), len(stage.out_shapes)
    return pl.pallas_call(
        _host_stage(body, stage, 2 + ne, no, grid), name=name, grid=grid,
        in_specs=[a_spec, b_spec] + [o_spec] * ne + [ANY] * ci,
        out_specs=[out_spec] * no + [ANY] * co, out_shape=out_shape + stage.out_shapes,
        scratch_shapes=scratch + stage.sems,
        input_output_aliases={2 + ne + i: no + o for i, o in stage.aliases.items()},
        compiler_params=_params(("arbitrary", "arbitrary", "arbitrary")),
    )(a, b, *extras, *stage.inputs)


def _rowwise(fn, rows, consts, row_outs, acc_outs, name, bt=256):
    T = rows[0].shape[0]
    bt = _pick(T, bt, SUBLANES)
    nr, nc, no, na = len(rows), len(consts), len(row_outs), len(acc_outs)

    def body(*refs):
        r_in, c_in = refs[:nr], refs[nr:nr + nc]
        r_out, a_out = refs[nr + nc:nr + nc + no], refs[nr + nc + no:]
        ro, ao = fn([r[...] for r in r_in], [c[...] for c in c_in])
        for ref, o in zip(r_out, ro):
            ref[...] = o.astype(ref.dtype)
        if na:
            @pl.when(pl.program_id(0) == 0)
            def _():
                for ref in a_out:
                    ref[...] = jnp.zeros_like(ref)
            for ref, o in zip(a_out, ao):
                ref[...] += o

    whole = lambda shp: pl.BlockSpec(shp, lambda i: (0,) * len(shp))
    return pl.pallas_call(
        body, name=name, grid=(T // bt,),
        in_specs=[pl.BlockSpec((bt, r.shape[1]), lambda i: (i, 0)) for r in rows] + [whole(c.shape) for c in consts],
        out_specs=[pl.BlockSpec((bt, w), lambda i: (i, 0)) for w, _ in row_outs] + [whole(s) for s in acc_outs],
        out_shape=[jax.ShapeDtypeStruct((T, w), d) for w, d in row_outs] + [jax.ShapeDtypeStruct(s, F32) for s in acc_outs],
        compiler_params=_params(("arbitrary",)),
    )(*rows, *consts)


def _rmsnorm_fwd(x, g, name):
    return _rowwise(lambda r, c: ([_rms(r[0], c[0])], []), [x], [g], [(x.shape[1], BF16)], [], name)[0]


def _rmsnorm_bwd_add(x, g, dh, dres, name):
    D = x.shape[1]

    def fn(r, c):
        _, vjp = jax.vjp(_rms, r[0], c[0])
        dx, dg = vjp(r[1])
        dx = dx + r[2]
        return [dx, dx], [dg]
    return _rowwise(fn, [x, dh, dres], [g], [(D, F32), (D, BF16)], [(1, D)], name)


def _loss_fwd_bwd(x3, g, target, name):
    D = x3.shape[1]

    def fn(r, c):
        def row_loss(x, gain):
            err = _rms(x, gain) - r[1]
            return 0.5 * jnp.mean(err * err, axis=-1, keepdims=True)
        lrow, vjp = jax.vjp(row_loss, r[0], c[0])
        dx, dg = vjp(jnp.ones_like(lrow))
        tile = jnp.broadcast_to(jnp.sum(lrow, axis=0, keepdims=True), (SUBLANES, LANES))
        return [dx], [tile, dg]
    return _rowwise(fn, [x3, target], [g], [(D, F32)], [(SUBLANES, LANES), (1, D)], name)


def _ple_bwd(dx3, gp, pp, name):
    D = dx3.shape[1]

    def fn(r, c):
        s = _sigmoid(r[1])
        return [r[0] * r[2] * s * (1.0 - s), r[0] * s], []
    return _rowwise(fn, [dx3, gp, pp], [], [(D, BF16), (D, BF16)], [], name)


def _adamw_math(w, g, m, v):
    nm = ADAM_B1 * m + (1.0 - ADAM_B1) * g
    nv = ADAM_B2 * v + (1.0 - ADAM_B2) * (g * g)
    m_hat = nm / (1.0 - ADAM_B1 ** ADAM_STEP)
    v_hat = nv / (1.0 - ADAM_B2 ** ADAM_STEP)
    return -ADAM_LR * (m_hat / (jnp.sqrt(v_hat) + ADAM_EPS) + ADAM_WD * w), nm, nv


def _adamw(w, g, m, v, name):
    R, C = w.shape[0], w.shape[-1]
    lanes = -(-C // LANES) * LANES
    if w.ndim == 2:
        bt = _pick(R, max(SUBLANES, (1 << 18) // lanes // SUBLANES * SUBLANES), SUBLANES)
        spec = pl.BlockSpec((bt, C), lambda i: (i, 0))
    else:
        bt = _pick(R, max(1, (1 << 18) // lanes), 1)
        spec = pl.BlockSpec((bt, 1, C), lambda i: (i, 0, 0))

    def body(w_ref, g_ref, m_ref, v_ref, d_ref, nm_ref, nv_ref):
        d_ref[...], nm_ref[...], nv_ref[...] = _adamw_math(w_ref[...], g_ref[...], m_ref[...], v_ref[...])

    return pl.pallas_call(
        body, name=name, grid=(R // bt,), in_specs=[spec] * 4, out_specs=[spec] * 3,
        out_shape=[jax.ShapeDtypeStruct(w.shape, F32)] * 3, compiler_params=_params(("parallel",)),
    )(w, g, m, v)


def _adamw_small(ws, gs, ms, vs):
    n = len(ws)

    def body(*refs):
        for i in range(n):
            d, nm, nv = _adamw_math(refs[i][...], refs[n + i][...], refs[2 * n + i][...], refs[3 * n + i][...])
            refs[4 * n + i][...], refs[5 * n + i][...], refs[6 * n + i][...] = d, nm, nv

    VMEM = pl.BlockSpec(memory_space=pltpu.VMEM)
    shapes = [jax.ShapeDtypeStruct(w.shape, F32) for w in ws]
    outs = pl.pallas_call(body, name="adamw_small", in_specs=[VMEM] * (4 * n), out_specs=[VMEM] * (3 * n),
                          out_shape=shapes * 3)(*ws, *gs, *ms, *vs)
    return outs[:n], outs[n:2 * n], outs[2 * n:]


def _gla_fwd(z_big, z_small, w2h, gbh, Bl, S, D):
    NC, dk, dv, HB = S // CHUNK, D // (2 * GLA_HEADS), D // GLA_HEADS, GLA_HEADS_PER_STEP
    HG = GLA_HEADS // HB
    chains = [(hh, bb) for hh in range(HB) for bb in range(Bl)]
    G = len(chains)
    fn = functools.partial(_gla_chunk, scale=dk ** -0.5)

    def body(q, k, v, z, w2, gb, o_ref, stall_ref, st):
        n, g = pl.program_id(0), pl.program_id(1)

        @pl.when(n == 0)
        def _():
            st[g] = jnp.zeros((G, dv, dk), F32)
        s0 = st[g]
        stall_ref[...] = s0.reshape(HB, Bl, dv, dk)
        qk = lambda r: jnp.stack([r[bb, :, hh * dk:(hh + 1) * dk] for hh, bb in chains])
        o, s_new = fn(qk(q), qk(k), jnp.stack([v[bb, :, hh * dv:(hh + 1) * dv] for hh, bb in chains]),
                      jnp.stack([z[bb] for _, bb in chains]), jnp.stack([w2[hh] for hh, _ in chains]),
                      jnp.stack([gb[hh] for hh, _ in chains]), s0)
        for i, (hh, bb) in enumerate(chains):
            o_ref[bb, :, hh * dv:(hh + 1) * dv] = o[i]
        st[g] = s_new

    return pl.pallas_call(
        body, name="gla_fwd", grid=(NC, HG),
        in_specs=[pl.BlockSpec((Bl, CHUNK, HB * dk), lambda n, g: (0, n, g)),
                  pl.BlockSpec((Bl, CHUNK, HB * dk), lambda n, g: (0, n, HG + g)),
                  pl.BlockSpec((Bl, CHUNK, HB * dv), lambda n, g: (0, n, HG + g)),
                  pl.BlockSpec((Bl, CHUNK, ZS), lambda n, g: (0, n, 0)),
                  pl.BlockSpec((HB, ZS, dk), lambda n, g: (g, 0, 0)),
                  pl.BlockSpec((HB, 1, dk), lambda n, g: (g, 0, 0))],
        out_specs=[pl.BlockSpec((Bl, CHUNK, HB * dv), lambda n, g: (0, n, g)),
                   pl.BlockSpec((HB, Bl, None, dv, dk), lambda n, g: (g, 0, n, 0, 0))],
        out_shape=[jax.ShapeDtypeStruct((Bl, S, D), F32), jax.ShapeDtypeStruct((GLA_HEADS, Bl, NC, dv, dk), F32)],
        scratch_shapes=[pltpu.VMEM((HG, G, dv, dk), F32)],
        compiler_params=_params(("arbitrary", "arbitrary")),
    )(z_big, z_big, z_big, z_small, w2h, gbh)


def _gla_bwd(z_big, z_small, w2h, gbh, st_all, do, Bl, S, D):
    NC, dk, dv, HB = S // CHUNK, D // (2 * GLA_HEADS), D // GLA_HEADS, GLA_HEADS_PER_STEP
    HG = GLA_HEADS // HB
    chains = [(hh, bb) for hh in range(HB) for bb in range(Bl)]
    G = len(chains)
    fn = functools.partial(_gla_chunk, scale=dk ** -0.5)

    def body(q, k, v, z, w2, gb, st0, do_ref, dq_ref, dk_ref, dv_ref, dzs_ref, dw2_ref, dgb_ref, dst):
        n, g = pl.program_id(0), pl.program_id(1)

        @pl.when(n == 0)
        def _():
            dst[g] = jnp.zeros((G, dv, dk), F32)

        @pl.when((n == 0) & (g == 0))
        def _():
            dw2_ref[...] = jnp.zeros_like(dw2_ref)
            dgb_ref[...] = jnp.zeros_like(dgb_ref)

        qk = lambda r: jnp.stack([r[bb, :, hh * dk:(hh + 1) * dk] for hh, bb in chains])
        vv = lambda r: jnp.stack([r[bb, :, hh * dv:(hh + 1) * dv] for hh, bb in chains])
        _, vjp = jax.vjp(fn, qk(q), qk(k), vv(v), jnp.stack([z[bb] for _, bb in chains]),
                         jnp.stack([w2[hh] for hh, _ in chains]), jnp.stack([gb[hh] for hh, _ in chains]),
                         st0[...].reshape(G, dv, dk))
        dq, dkk, dvv, dzs, dw2, dgb, dst0 = vjp((vv(do_ref), dst[g]))
        for i, (hh, bb) in enumerate(chains):
            dq_ref[bb, :, hh * dk:(hh + 1) * dk] = dq[i].astype(dq_ref.dtype)
            dk_ref[bb, :, hh * dk:(hh + 1) * dk] = dkk[i].astype(dk_ref.dtype)
            dv_ref[bb, :, hh * dv:(hh + 1) * dv] = dvv[i].astype(dv_ref.dtype)
            dw2_ref[g * HB + hh] += dw2[i]
            dgb_ref[g * HB + hh] += dgb[i]
        for bb in range(Bl):
            tot = sum(dzs[i] for i, (_, b2) in enumerate(chains) if b2 == bb)

            @pl.when(g == 0)
            def _():
                dzs_ref[bb] = tot

            @pl.when(g > 0)
            def _():
                dzs_ref[bb] += tot
        dst[g] = dst0

    rn = lambda n: NC - 1 - n
    return pl.pallas_call(
        body, name="gla_bwd", grid=(NC, HG),
        in_specs=[pl.BlockSpec((Bl, CHUNK, HB * dk), lambda n, g: (0, rn(n), g)),
                  pl.BlockSpec((Bl, CHUNK, HB * dk), lambda n, g: (0, rn(n), HG + g)),
                  pl.BlockSpec((Bl, CHUNK, HB * dv), lambda n, g: (0, rn(n), HG + g)),
                  pl.BlockSpec((Bl, CHUNK, ZS), lambda n, g: (0, rn(n), 0)),
                  pl.BlockSpec((HB, ZS, dk), lambda n, g: (g, 0, 0)),
                  pl.BlockSpec((HB, 1, dk), lambda n, g: (g, 0, 0)),
                  pl.BlockSpec((HB, Bl, None, dv, dk), lambda n, g: (g, 0, rn(n), 0, 0)),
                  pl.BlockSpec((Bl, CHUNK, HB * dv), lambda n, g: (0, rn(n), g))],
        out_specs=[pl.BlockSpec((Bl, CHUNK, HB * dk), lambda n, g: (0, rn(n), g)),
                   pl.BlockSpec((Bl, CHUNK, HB * dk), lambda n, g: (0, rn(n), g)),
                   pl.BlockSpec((Bl, CHUNK, HB * dv), lambda n, g: (0, rn(n), g)),
                   pl.BlockSpec((Bl, CHUNK, ZS), lambda n, g: (0, rn(n), 0)),
                   pl.BlockSpec((GLA_HEADS, ZS, dk), lambda n, g: (0, 0, 0)),
                   pl.BlockSpec((GLA_HEADS, 1, dk), lambda n, g: (0, 0, 0))],
        out_shape=[jax.ShapeDtypeStruct((Bl, S, D // 2), BF16), jax.ShapeDtypeStruct((Bl, S, D // 2), BF16),
                   jax.ShapeDtypeStruct((Bl, S, D), BF16), jax.ShapeDtypeStruct((Bl, S, ZS), F32),
                   jax.ShapeDtypeStruct((GLA_HEADS, ZS, dk), F32), jax.ShapeDtypeStruct((GLA_HEADS, 1, dk), F32)],
        scratch_shapes=[pltpu.VMEM((HG, G, dv, dk), F32)],
        compiler_params=_params(("arbitrary", "arbitrary")),
    )(z_big, z_big, z_big, z_small, w2h, gbh, st_all, do)


def _conv_fwd(z_big, conv_w, grp, Bl, S, D):
    d = D // DN_HEADS
    l2, scale = grp < 2, (d ** -0.5 if grp == 0 else 1.0)
    x_blk0 = (3 * D + grp * D) // d

    def body(x_ref, w_ref, o_ref):
        wrows = [w_ref[j:j + 1, :] for j in range(DN_CONV)]
        o_ref[...] = _conv_act(x_ref[...], wrows, l2=l2, scale=scale)

    return pl.pallas_call(
        body, name=f"conv_fwd{grp}", grid=(Bl, DN_HEADS),
        in_specs=[pl.BlockSpec((S, d), lambda b, j: (b, x_blk0 + j)),
                  pl.BlockSpec((DN_CONV, d), lambda b, j: (0, grp * DN_HEADS + j))],
        out_specs=pl.BlockSpec((S, d), lambda b, j: (b, j)),
        out_shape=jax.ShapeDtypeStruct((Bl * S, D), F32),
        compiler_params=_params(("parallel", "parallel")),
    )(z_big, conv_w)


def _conv_bwd(z_big, conv_w, dact, grp, Bl, S, D):
    d = D // DN_HEADS
    l2, scale = grp < 2, (d ** -0.5 if grp == 0 else 1.0)
    x_blk0 = (3 * D + grp * D) // d

    def body(x_ref, w_ref, g_ref, dx_ref, dw_ref):
        @pl.when(pl.program_id(1) == 0)
        def _():
            dw_ref[...] = jnp.zeros_like(dw_ref)
        wrows = [w_ref[j:j + 1, :] for j in range(DN_CONV)]
        _, vjp = jax.vjp(lambda x, wr: _conv_act(x, wr, l2=l2, scale=scale), x_ref[...], wrows)
        dx, dwr = vjp(g_ref[...])
        dx_ref[...] = dx.astype(dx_ref.dtype)
        for j in range(DN_CONV):
            dw_ref[j:j + 1, :] += dwr[j]

    return pl.pallas_call(
        body, name=f"conv_bwd{grp}", grid=(DN_HEADS, Bl),
        in_specs=[pl.BlockSpec((S, d), lambda j, b: (b, x_blk0 + j)),
                  pl.BlockSpec((DN_CONV, d), lambda j, b: (0, grp * DN_HEADS + j)),
                  pl.BlockSpec((S, d), lambda j, b: (b, j))],
        out_specs=[pl.BlockSpec((S, d), lambda j, b: (b, j)), pl.BlockSpec((DN_CONV, d), lambda j, b: (0, j))],
        out_shape=[jax.ShapeDtypeStruct((Bl * S, D), BF16), jax.ShapeDtypeStruct((DN_CONV, D), F32)],
        compiler_params=_params(("arbitrary", "arbitrary")),
    )(z_big, conv_w, dact)


def _lane_column(zb, lane, width):
    pick = lax.broadcasted_iota(jnp.int32, zb.shape, 1) == lane
    return jnp.broadcast_to(jnp.sum(jnp.where(pick, zb, 0.0), axis=-1, keepdims=True), (zb.shape[0], width))


def _dn_fwd(qa, ka, va, z_small, alog, dtb, Bl, S, D):
    NC, d, HB = S // CHUNK, D // DN_HEADS, DN_HEADS_PER_STEP
    HG = DN_HEADS // HB
    chains = [(hh, bb) for hh in range(HB) for bb in range(Bl)]
    G = len(chains)

    def body(q, k, v, z, al, dt, o_ref, sall_ref, st):
        n, g = pl.program_id(0), pl.program_id(1)

        @pl.when(n == 0)
        def _():
            st[g] = jnp.zeros((G, d, d), F32)
        tok_in = lambda r: jnp.stack([r[bb, :, hh * d:(hh + 1) * d] for hh, bb in chains])
        head_in = lambda r: jnp.stack([r[hh] for hh, _ in chains])
        gate_in = lambda lane0: jnp.stack([_lane_column(z[bb], lane0 + g * HB + hh, d) for hh, bb in chains])
        s0 = st[g]
        sall_ref[...] = s0.reshape(HB, Bl, d, d)
        o, s_new = _dn_chunk(tok_in(q), tok_in(k), tok_in(v), gate_in(A_LANE), gate_in(B_LANE), head_in(al), head_in(dt), s0)
        for i, (hh, bb) in enumerate(chains):
            o_ref[bb, :, hh * d:(hh + 1) * d] = o[i]
        st[g] = s_new

    tok = pl.BlockSpec((Bl, CHUNK, HB * d), lambda n, g: (0, n, g))
    per_head = pl.BlockSpec((HB, 1, d), lambda n, g: (g, 0, 0))
    return pl.pallas_call(
        body, name="dn_fwd", grid=(NC, HG),
        in_specs=[tok, tok, tok, pl.BlockSpec((Bl, CHUNK, ZS), lambda n, g: (0, n, 0)), per_head, per_head],
        out_specs=[tok, pl.BlockSpec((HB, Bl, None, d, d), lambda n, g: (g, 0, n, 0, 0))],
        out_shape=[jax.ShapeDtypeStruct((Bl, S, D), F32), jax.ShapeDtypeStruct((DN_HEADS, Bl, NC, d, d), F32)],
        scratch_shapes=[pltpu.VMEM((HG, G, d, d), F32)],
        compiler_params=_params(("arbitrary", "arbitrary")),
    )(qa, ka, va, z_small, alog, dtb)


def _dn_bwd(qa, ka, va, z_small, alog, dtb, s_all, do, dzs_gla, Bl, S, D):
    NC, d, HB = S // CHUNK, D // DN_HEADS, DN_HEADS_PER_STEP
    HG = DN_HEADS // HB
    chains = [(hh, bb) for hh in range(HB) for bb in range(Bl)]
    G = len(chains)

    def lanesum(t):
        return jnp.sum(t, axis=-1, keepdims=True)

    def body(q, k, v, z, al, dt, s0_ref, do_ref, dzg_ref, dq_ref, dk_ref, dv_ref, dzs_ref, dal_ref, ddt_ref, dst):
        n, g = pl.program_id(0), pl.program_id(1)

        @pl.when(n == 0)
        def _():
            dst[g] = jnp.zeros((G, d, d), F32)

        @pl.when((n == 0) & (g == 0))
        def _():
            dal_ref[...] = jnp.zeros_like(dal_ref)
            ddt_ref[...] = jnp.zeros_like(ddt_ref)

        tok_in = lambda r: jnp.stack([r[bb, :, hh * d:(hh + 1) * d] for hh, bb in chains])
        head_in = lambda r: jnp.stack([r[hh] for hh, _ in chains])
        gate_in = lambda lane0: jnp.stack([_lane_column(z[bb], lane0 + g * HB + hh, d) for hh, bb in chains])
        _, vjp = jax.vjp(_dn_chunk, tok_in(q), tok_in(k), tok_in(v), gate_in(A_LANE), gate_in(B_LANE), head_in(al),
                         head_in(dt), s0_ref[...].reshape(G, d, d))
        dq, dkk, dvv, da, db, dal, ddt, ds0 = vjp((tok_in(do_ref), dst[g]))
        da, db = lanesum(da), lanesum(db)
        dal = jnp.broadcast_to(lanesum(dal), (G, 1, d))
        ddt = jnp.broadcast_to(lanesum(ddt), (G, 1, d))
        lane = lax.broadcasted_iota(jnp.int32, (CHUNK, ZS), 1)
        for bb in range(Bl):
            part = jnp.zeros((CHUNK, ZS), F32)
            for i, (hh, b2) in enumerate(chains):
                if b2 == bb:
                    h = g * HB + hh
                    part = part + jnp.where(lane == A_LANE + h, da[i], 0.0) + jnp.where(lane == B_LANE + h, db[i], 0.0)

            @pl.when(g == 0)
            def _():
                dzs_ref[bb] = jnp.where(lane < LOWRANK, dzg_ref[bb], 0.0) + part

            @pl.when(g > 0)
            def _():
                dzs_ref[bb] += part
        for i, (hh, bb) in enumerate(chains):
            cols = slice(hh * d, (hh + 1) * d)
            dq_ref[bb, :, cols] = dq[i]
            dk_ref[bb, :, cols] = dkk[i]
            dv_ref[bb, :, cols] = dvv[i]
            dal_ref[g * HB + hh] += dal[i]
            ddt_ref[g * HB + hh] += ddt[i]
        dst[g] = ds0

    rn = lambda n: NC - 1 - n
    tok = pl.BlockSpec((Bl, CHUNK, HB * d), lambda n, g: (0, rn(n), g))
    zsb = pl.BlockSpec((Bl, CHUNK, ZS), lambda n, g: (0, rn(n), 0))
    per_head = pl.BlockSpec((HB, 1, d), lambda n, g: (g, 0, 0))
    all_heads = pl.BlockSpec((DN_HEADS, 1, d), lambda n, g: (0, 0, 0))
    tok_shape = jax.ShapeDtypeStruct((Bl, S, D), F32)
    head_shape = jax.ShapeDtypeStruct((DN_HEADS, 1, d), F32)
    return pl.pallas_call(
        body, name="dn_bwd", grid=(NC, HG),
        in_specs=[tok, tok, tok, zsb, per_head, per_head,
                  pl.BlockSpec((HB, Bl, None, d, d), lambda n, g: (g, 0, rn(n), 0, 0)), tok, zsb],
        out_specs=[tok, tok, tok, zsb, all_heads, all_heads],
        out_shape=[tok_shape, tok_shape, tok_shape, jax.ShapeDtypeStruct((Bl, S, ZS), F32), head_shape, head_shape],
        scratch_shapes=[pltpu.VMEM((HG, G, d, d), F32)],
        compiler_params=_params(("arbitrary", "arbitrary")),
    )(qa, ka, va, z_small, alog, dtb, s_all, do, dzs_gla)


def _merge_specs(D, bt):
    dv, w = D // GLA_HEADS, D // DN_HEADS
    col = lambda off: pl.BlockSpec((bt, dv), lambda i, h: (i, off // dv + h))
    return dv, w, col


def _merge_load(refs, nsub, w):
    return [[r[:, s * w:(s + 1) * w] for s in range(nsub)] for r in refs]


def _merge_fwd(o_gla, o_dn, z_big, gla_norm, dn_norm, D, bt=256):
    T = o_gla.shape[0]
    bt = _pick(T, bt, SUBLANES)
    dv, w, col = _merge_specs(D, bt)
    nsub = dv // w

    def body(og, gg, od, dz, ga, gb, gn, dn, out):
        ogl, ggl, odl, dzl, gal, gbl = _merge_load([og, gg, od, dz, ga, gb], nsub, w)
        gnl = [gn[:, s * w:(s + 1) * w] for s in range(nsub)]
        outs = _merge_math(ogl, ggl, odl, dzl, gal, gbl, gnl, dn[...])
        for s in range(nsub):
            out[:, s * w:(s + 1) * w] = outs[s].astype(out.dtype)

    return pl.pallas_call(
        body, name="merge_fwd", grid=(T // bt, GLA_HEADS),
        in_specs=[col(0), col(2 * D), col(0), col(6 * D), col(7 * D), col(8 * D),
                  pl.BlockSpec((1, dv), lambda i, h: (0, 0)), pl.BlockSpec((1, w), lambda i, h: (0, 0))],
        out_specs=col(0),
        out_shape=jax.ShapeDtypeStruct((T, D), BF16),
        compiler_params=_params(("parallel", "parallel")),
    )(o_gla, z_big, o_dn, z_big, z_big, z_big, gla_norm, dn_norm)


def _merge_bwd(o_gla, o_dn, z_big, gla_norm, dn_norm, dmix, D, bt=256):
    T = o_gla.shape[0]
    bt = _pick(T, bt, SUBLANES)
    dv, w, col = _merge_specs(D, bt)
    nsub = dv // w

    def body(og, gg, od, dz, ga, gb, gn, dn, dm, dog, dgg, dod, ddz, dga, dgb, dgn, ddn):
        @pl.when((pl.program_id(0) == 0) & (pl.program_id(1) == 0))
        def _():
            dgn[...] = jnp.zeros_like(dgn)
            ddn[...] = jnp.zeros_like(ddn)

        ogl, ggl, odl, dzl, gal, gbl, dml = _merge_load([og, gg, od, dz, ga, gb, dm], nsub, w)
        gnl = [gn[:, s * w:(s + 1) * w] for s in range(nsub)]
        _, vjp = jax.vjp(_merge_math, ogl, ggl, odl, dzl, gal, gbl, gnl, dn[...])
        g_og, g_gg, g_od, g_dz, g_ga, g_gb, g_gn, g_dn = vjp(dml)
        for s in range(nsub):
            sl = slice(s * w, (s + 1) * w)
            dog[:, sl] = g_og[s]
            dgg[:, sl] = g_gg[s].astype(dgg.dtype)
            dod[:, sl] = g_od[s]
            ddz[:, sl] = g_dz[s].astype(ddz.dtype)
            dga[:, sl] = g_ga[s].astype(dga.dtype)
            dgb[:, sl] = g_gb[s].astype(dgb.dtype)
            dgn[:, sl] += g_gn[s]
        ddn[...] += g_dn

    f32s, bf16s = jax.ShapeDtypeStruct((T, D), F32), jax.ShapeDtypeStruct((T, D), BF16)
    return pl.pallas_call(
        body, name="merge_bwd", grid=(T // bt, GLA_HEADS),
        in_specs=[col(0), col(2 * D), col(0), col(6 * D), col(7 * D), col(8 * D),
                  pl.BlockSpec((1, dv), lambda i, h: (0, 0)), pl.BlockSpec((1, w), lambda i, h: (0, 0)), col(0)],
        out_specs=[col(0)] * 6 + [pl.BlockSpec((1, dv), lambda i, h: (0, 0)), pl.BlockSpec((1, w), lambda i, h: (0, 0))],
        out_shape=[f32s, bf16s, f32s, bf16s, bf16s, bf16s,
                   jax.ShapeDtypeStruct((1, dv), F32), jax.ShapeDtypeStruct((1, w), F32)],
        compiler_params=_params(("arbitrary", "arbitrary")),
    )(o_gla, z_big, o_dn, z_big, z_big, z_big, gla_norm, dn_norm, dmix)


def _place():
    return lax.axis_index("x"), lax.axis_index("y"), lax.axis_index("c")


def _other_chips(x, y):
    return [(1 - x, y), (x, 1 - y), (1 - x, 1 - y)]


def _rcopy(src, dst, send_sem, recv_sem, dev):
    return pltpu.make_async_remote_copy(src_ref=src, dst_ref=dst, send_sem=send_sem, recv_sem=recv_sem,
                                        device_id=dev, device_id_type=MESH)


ANY = pl.BlockSpec(memory_space=pl.ANY)


ROWS, COLS = 'rows', 'cols'


def _half(ref, hc, by, lead=()):
    shape = ref.shape[len(lead):]
    if by == ROWS:
        rh = shape[0] // 2
        idx = (pl.ds(pl.multiple_of(hc * rh, 16), rh),) + (slice(None),) * (len(shape) - 1)
    else:
        ch = shape[-1] // 2
        idx = (slice(None),) * (len(shape) - 1) + (pl.ds(pl.multiple_of(hc * ch, LANES), ch),)
    return ref.at[(*lead, *idx)]


def _half_shape(shape, by):
    return (shape[0] // 2,) + tuple(shape[1:]) if by == ROWS else tuple(shape[:-1]) + (shape[-1] // 2,)


def _gather_ici(shards, by):
    nw = len(shards)

    def copies(srcs, outs, send_sems, recv_sems):
        x, y, c = _place()
        return [_rcopy(_half(srcs[w], c, by[w]), _half(outs[w], c, by[w], (2 * x + y,)),
                       send_sems.at[3 * w + k], recv_sems.at[3 * w + k], (px, py, c))
                for w in range(nw) for k, (px, py) in enumerate(_other_chips(x, y))]

    return _Stage(shards, [jax.ShapeDtypeStruct((4,) + s.shape, s.dtype) for s in shards], 3 * nw, copies)


def _gather_pass(gathered, by):
    nw = len(gathered)

    def copies(srcs, outs, send_sems, recv_sems):
        x, y, c = _place()
        cps = []
        for w in range(nw):
            for k, (px, py) in enumerate(_other_chips(x, y)):
                slot = (2 * px + py,)
                cps.append(_rcopy(_half(srcs[w], c, by[w], slot), _half(outs[w], c, by[w], slot),
                                  send_sems.at[3 * w + k], recv_sems.at[3 * w + k], (x, y, 1 - c)))
        return cps

    return _Stage(gathered, [jax.ShapeDtypeStruct(g.shape, g.dtype) for g in gathered], 3 * nw, copies,
                  aliases={w: w for w in range(nw)})


def _pair_exchange(ps, by):
    nw = len(ps)

    def copies(srcs, outs, send_sems, recv_sems):
        x, y, c = _place()
        return [_rcopy(_half(srcs[w], 1 - c, by[w], (slice(None),)), outs[w], send_sems.at[w], recv_sems.at[w], (x, y, 1 - c))
                for w in range(nw)]

    return _Stage(ps, [jax.ShapeDtypeStruct((4,) + _half_shape(p.shape[1:], b), p.dtype) for p, b in zip(ps, by)], nw, copies)


def _sum_blocks(half_shape, by):
    rh, ch = half_shape
    if by == ROWS:
        lanes = -(-ch // LANES) * LANES
        bt = _pick(rh, max(16, (3 << 18) // lanes // 16 * 16), 16)
        return (bt, ch), rh // bt, lambda i: (i, 0)
    bc = _pick(ch, max(LANES, (5 << 18) // rh // LANES * LANES), LANES)
    return (rh, bc), ch // bc, lambda i: (0, i)


def _pair_sum(p, got, c_idx, name, by=ROWS):
    hs = got.shape[1:]
    blk, nb, pos = _sum_blocks(hs, by)

    def body(c_ref, a, b, of, ob):
        s = a[...] + b[...]
        of[...] = s
        ob[...] = s.astype(BF16)

    def mine(j, i, c_ref):
        r, cc = pos(c_ref[0] * nb + i)
        return (j, r, cc)

    spec = pl.BlockSpec((None,) + blk, lambda j, i, c_ref: (j,) + pos(i))
    return pl.pallas_call(
        body, name=name,
        grid_spec=pltpu.PrefetchScalarGridSpec(
            num_scalar_prefetch=1, grid=(4, nb),
            in_specs=[pl.BlockSpec((None,) + blk, mine), spec], out_specs=[spec, spec]),
        out_shape=[jax.ShapeDtypeStruct((4,) + hs, F32), jax.ShapeDtypeStruct((4,) + hs, BF16)],
        compiler_params=_params(("parallel", "parallel")),
    )(c_idx, p, got)


def _chip_scatter(qbs):
    nw = len(qbs)

    def copies(srcs, outs, send_sems, recv_sems):
        x, y, c = _place()
        return [_rcopy(srcs[w].at[2 * px + py], outs[w].at[k], send_sems.at[3 * w + k], recv_sems.at[3 * w + k], (px, py, c))
                for w in range(nw) for k, (px, py) in enumerate(_other_chips(x, y))]

    return _Stage(qbs, [jax.ShapeDtypeStruct((3,) + q.shape[1:], q.dtype) for q in qbs], 3 * nw, copies)


def _final_sum(qf, got, me_idx, name, by=ROWS):
    hs = qf.shape[1:]
    blk, nb, pos = _sum_blocks(hs, by)

    def body(me_ref, a, b, o):
        o[...] = ((a[...] + b[0].astype(F32)) + b[1].astype(F32)) + b[2].astype(F32)

    return pl.pallas_call(
        body, name=name,
        grid_spec=pltpu.PrefetchScalarGridSpec(
            num_scalar_prefetch=1, grid=(nb,),
            in_specs=[pl.BlockSpec((None,) + blk, lambda i, me_ref: (me_ref[0],) + pos(i)),
                      pl.BlockSpec((3,) + blk, lambda i, me_ref: (0,) + pos(i))],
            out_specs=pl.BlockSpec(blk, lambda i, me_ref: pos(i))),
        out_shape=jax.ShapeDtypeStruct(hs, F32),
        compiler_params=_params(("parallel",)),
    )(me_idx, qf, got)


def _pair_allgather(halves, by):
    nw = len(halves)
    whole = [(2 * h.shape[0], h.shape[1]) if b == ROWS else h.shape for h, b in zip(halves, by)]

    def body(*refs):
        srcs, outs, send_sems, recv_sems = refs[:nw], refs[nw:2 * nw], refs[2 * nw], refs[2 * nw + 1]
        x, y, c = _place()
        mine = lambda w, hc: _half(outs[w], hc, ROWS) if by[w] == ROWS else outs[w]
        cps = []
        for w in range(nw):
            cp = _rcopy(srcs[w], mine(w, c), send_sems.at[w], recv_sems.at[w], (x, y, 1 - c))
            cp.start()
            cps.append(cp)
        for w in range(nw):
            got = mine(w, 1 - c)
            _rcopy(got, got, send_sems.at[w], recv_sems.at[w], (x, y, 1 - c)).wait_recv()
        for cp in cps:
            cp.wait_send()

    return pl.pallas_call(
        body, name="grad_pair_allgather", in_specs=[ANY] * nw, out_specs=[ANY] * nw,
        out_shape=[jax.ShapeDtypeStruct(s, h.dtype) for s, h in zip(whole, halves)],
        scratch_shapes=[pltpu.SemaphoreType.DMA((nw,)), pltpu.SemaphoreType.DMA((nw,))],
    )(*halves)


def _small_exchange(items, out_shapes, finish, name):
    n = len(items)
    offs, rows = [], 0
    for it in items:
        offs.append(rows)
        rows += it.shape[0]
    rows = -(-rows // SUBLANES) * SUBLANES
    width = -(-max(it.shape[1] for it in items) // LANES) * LANES
    VMEM = pl.BlockSpec(memory_space=pltpu.VMEM)

    def body(*refs):
        ins, outs = refs[:n], refs[n:n + len(out_shapes)]
        buf, send_sems, recv_sems = refs[n + len(out_shapes):]
        x, y, c = _place()
        me = 4 * x + 2 * y + c
        flip = lambda v, f: (1 - v) if f else v
        peers = [(flip(x, r >> 2 & 1), flip(y, r >> 1 & 1), flip(c, r & 1)) for r in range(1, 8)]
        buf[me] = jnp.zeros((rows, width), F32)
        for it, off, ref in zip(items, offs, ins):
            buf[me, off:off + it.shape[0], 0:it.shape[1]] = ref[...]
        cps = [_rcopy(buf.at[me], buf.at[me], send_sems.at[k], recv_sems.at[k], dev) for k, dev in enumerate(peers)]
        for cp in cps:
            cp.start()
        for k, (px, py, pc) in enumerate(peers):
            slot = buf.at[4 * px + 2 * py + pc]
            _rcopy(slot, slot, send_sems.at[k], recv_sems.at[k], (px, py, pc)).wait_recv()
        for cp in cps:
            cp.wait_send()
        finish(buf, offs, outs)

    return pl.pallas_call(
        body, name=name, in_specs=[VMEM] * n, out_specs=[VMEM] * len(out_shapes),
        out_shape=[jax.ShapeDtypeStruct(s, F32) for s in out_shapes],
        scratch_shapes=[pltpu.VMEM((8, rows, width), F32), pltpu.SemaphoreType.DMA((7,)), pltpu.SemaphoreType.DMA((7,))],
        compiler_params=pltpu.CompilerParams(vmem_limit_bytes=VMEM_LIMIT_BYTES),
    )(*items)


def _allreduce_small(items, name):
    def finish(buf, offs, outs):
        for it, off, out in zip(items, offs, outs):
            region = lambda d: buf[d, off:off + it.shape[0], 0:it.shape[1]]
            s = region(0)
            for d in range(1, 8):
                s = s + region(d)
            out[...] = s
    return _small_exchange(items, [it.shape for it in items], finish, name)


def _allgather_small_shards(items, name):
    def finish(buf, offs, outs):
        for it, off, out in zip(items, offs, outs):
            r, c = it.shape
            for j in range(4):
                out[:, j * c:(j + 1) * c] = buf[2 * j, off:off + r, 0:c]
    return _small_exchange(items, [(it.shape[0], 4 * it.shape[1]) for it in items], finish, name)


def _split_w_in(wt, D):
    pad = jnp.zeros((ZS - 3 * LOWRANK, wt.shape[1]), wt.dtype)
    big = jnp.concatenate([wt[:3 * D], wt[3 * D + 16:6 * D + 16], wt[6 * D + 16:7 * D + 16], wt[7 * D + 48:]], axis=0)
    small = jnp.concatenate([wt[3 * D:3 * D + 16], wt[7 * D + 16:7 * D + 48], pad], axis=0)
    return big, small


def _join_w_in(gb, gs, D):
    return jnp.concatenate([gb[:3 * D], gs[:16], gb[3 * D:6 * D], gb[6 * D:7 * D], gs[16:48], gb[7 * D:9 * D]], axis=0)


def kernel(x, p, g_mix, w_in, gla_w2, gla_b, gla_norm, dn_conv, dn_a_log, dn_dt_bias, dn_norm, w_out, g_mlp, w_up, w_down, g_ple, w_ple_gate, w_ple_proj, g_final, loss_target, m_g_mix, m_w_in, m_gla_w2, m_gla_b, m_gla_norm, m_dn_conv, m_dn_a_log, m_dn_dt_bias, m_dn_norm, m_w_out, m_g_mlp, m_w_up, m_w_down, m_g_ple, m_w_ple_gate, m_w_ple_proj, m_g_final, v_g_mix, v_w_in, v_gla_w2, v_gla_b, v_gla_norm, v_dn_conv, v_dn_a_log, v_dn_dt_bias, v_dn_norm, v_w_out, v_g_mlp, v_w_up, v_w_down, v_g_ple, v_w_ple_gate, v_w_ple_proj, v_g_final):
    wts = dict(zip(WEIGHTS, [g_mix, w_in, gla_w2, gla_b, gla_norm, dn_conv, dn_a_log, dn_dt_bias, dn_norm, w_out, g_mlp,
                             w_up, w_down, g_ple, w_ple_gate, w_ple_proj, g_final]))
    mom = dict(zip(WEIGHTS, [m_g_mix, m_w_in, m_gla_w2, m_gla_b, m_gla_norm, m_dn_conv, m_dn_a_log, m_dn_dt_bias, m_dn_norm,
                             m_w_out, m_g_mlp, m_w_up, m_w_down, m_g_ple, m_w_ple_gate, m_w_ple_proj, m_g_final]))
    var = dict(zip(WEIGHTS, [v_g_mix, v_w_in, v_gla_w2, v_gla_b, v_gla_norm, v_dn_conv, v_dn_a_log, v_dn_dt_bias, v_dn_norm,
                             v_w_out, v_g_mlp, v_w_up, v_w_down, v_g_ple, v_w_ple_gate, v_w_ple_proj, v_g_final]))
    Bl, S, D = x.shape
    T = Bl * S
    PLE = p.shape[-1]
    dn_d, gla_dk = D // DN_HEADS, D // (2 * GLA_HEADS)
    ix, iy, ic = _place()
    j_me = 2 * ix + iy
    as2d = lambda a: a.reshape(a.shape[-2], a.shape[-1]) if a.ndim > 1 else a.reshape(1, -1)
    c_idx, me_idx = ic.reshape(1).astype(jnp.int32), j_me.reshape(1).astype(jnp.int32)

    rows_first = lambda a: jnp.transpose(a, (2, 0, 1))
    cols_last = lambda a: jnp.transpose(a, (1, 2, 0))
    w_in_t, m_in_t, v_in_t = rows_first(w_in), rows_first(m_w_in), rows_first(v_w_in)
    n_in = w_in_t.shape[0]
    shard2d = {n: as2d(wts[n]) for n, _ in BIG[1:]}
    bf16_shards = [w_in_t.astype(BF16).reshape(n_in, D)] + [shard2d[n].astype(BF16) for n, _ in BIG[1:]]
    split = [COLS] + [ROWS] * (len(BIG) - 1)
    own_slot = lambda g, s: lax.dynamic_update_slice(g, s[None], (j_me, 0, 0))
    (w_in_ici,) = _run_stage(_gather_ici(bf16_shards[:1], split[:1]), "allgather_w_in_ici")
    (w_in_all,) = _run_stage(_gather_pass([w_in_ici], split[:1]), "allgather_w_in_pass")
    w_in_slots = own_slot(w_in_all, bf16_shards[0])
    w_big, w_small = _split_w_in(w_in_slots.reshape(4 * n_in, D), D)

    w2_full, conv_full = _allgather_small_shards([as2d(gla_w2), as2d(dn_conv)], "allgather_small_weights")
    w2pad = jnp.pad(w2_full, ((0, ZS - LOWRANK), (0, 0)))
    w2h = jnp.swapaxes(w2pad.reshape(ZS, GLA_HEADS, gla_dk), 0, 1)
    gbh = gla_b.reshape(GLA_HEADS, 1, gla_dk)
    alog_w = jnp.broadcast_to(dn_a_log.reshape(DN_HEADS, 1, 1), (DN_HEADS, 1, dn_d))
    dtb_w = jnp.broadcast_to(dn_dt_bias.reshape(DN_HEADS, 1, 1), (DN_HEADS, 1, dn_d))

    xt = x.reshape(T, D)
    tgt = loss_target.reshape(T, D)
    pt = p.reshape(T, PLE)
    seq = lambda t: t.reshape(Bl, S, t.shape[-1])
    tok = lambda t: t.reshape(T, t.shape[-1])
    h = _rmsnorm_fwd(xt, g_mix, "rms1_fwd")
    z_big, *rest_ici = _matmul(h, w_big, 'nt', [F32], "proj_in", stage=_gather_ici(bf16_shards[1:], split[1:]))
    z_small, *rest_all = _matmul(h, w_small, 'nt', [F32], "proj_in_narrow", stage=_gather_pass(rest_ici, split[1:]))
    slots = {n: own_slot(g, s) for (n, _), g, s in zip(BIG[1:], rest_all, bf16_shards[1:])}
    rows_joined = lambda t: t.reshape(4 * t.shape[1], t.shape[2])
    w_out_f, w_down_f, w_pg_f = rows_joined(slots['w_out']), rows_joined(slots['w_down']), rows_joined(slots['w_ple_gate'])
    w_up_s, w_pp_s = slots['w_up'], slots['w_ple_proj']
    o_gla, st_all = _gla_fwd(seq(z_big), seq(z_small), w2h, gbh, Bl, S, D)
    acts = [_conv_fwd(z_big, conv_full, grp, Bl, S, D) for grp in range(3)]
    o_dn, s_all = _dn_fwd(seq(acts[0]), seq(acts[1]), seq(acts[2]), seq(z_small), alog_w, dtb_w, Bl, S, D)
    mixed = _merge_fwd(tok(o_gla), tok(o_dn), z_big, gla_norm, dn_norm, D)
    (x1,) = _matmul(mixed, w_out_f, 'nn', [F32], "proj_out", epilogue=lambda r, e: (e + r,), extras=(xt,), bm=512)
    h2 = _rmsnorm_fwd(x1, g_mlp, "rms2_fwd")
    u, act = _matmul(h2, w_up_s, 'nn', [F32, BF16], "mlp_up", b_slots=True,
                     epilogue=lambda r: (r, jnp.square(jnp.maximum(r, 0.0))))
    (x2,) = _matmul(act, w_down_f, 'nn', [F32], "mlp_down", epilogue=lambda r, e: (e + r,), extras=(x1,), bm=512)
    h3 = _rmsnorm_fwd(x2, g_ple, "rms3_fwd")
    (pp,) = _matmul(pt, w_pp_s, 'nn', [F32], "ple_proj", b_slots=True)
    gp, x3 = _matmul(h3, w_pg_f, 'nn', [F32, F32], "ple_gate",
                     epilogue=lambda r, e, q: (r, e + _sigmoid(r) * q), extras=(x2, pp), bm=512)
    dx3, loss_tile, d_g_final = _loss_fwd_bwd(x3, g_final.reshape(1, D), tgt, "loss")

    d_gp, d_pp = _ple_bwd(dx3, gp, pp, "ple_bwd")
    (g_pp,) = _matmul(pt, d_pp, 'tn', [F32], "ple_proj_dw", out_slots=True)
    (g_pg,) = _matmul(h3, d_gp, 'tn', [F32], "ple_gate_dw")
    (dh3,) = _matmul(d_gp, w_pg_f, 'nt', [F32], "ple_gate_dx")
    dx2, dx2b, d_g_ple = _rmsnorm_bwd_add(x2, g_ple, dh3, dx3, "rms3_bwd")
    (g_down,) = _matmul(act, dx2b, 'tn', [F32], "mlp_down_dw")
    (du,) = _matmul(dx2b, w_down_f, 'nt', [BF16], "mlp_down_dx",
                    epilogue=lambda r, e: (r * 2.0 * jnp.maximum(e, 0.0),), extras=(u,))
    (g_up,) = _matmul(h2, du, 'tn', [F32], "mlp_up_dw", out_slots=True)
    by_rows = lambda g: g.reshape(4, g.shape[0] // 4, g.shape[1])
    send_mlp = [g_up, by_rows(g_down), by_rows(g_pg), g_pp]
    dh2, *sib_mlp = _matmul(du, w_up_s, 'nt', [F32], "mlp_up_dx", b_slots=True, stage=_pair_exchange(send_mlp, split[2:]))
    dx1, dx1b, d_g_mlp = _rmsnorm_bwd_add(x1, g_mlp, dh2, dx2, "rms2_bwd")
    (g_out,) = _matmul(mixed, dx1b, 'tn', [F32], "proj_out_dw")
    dmix, sib_out = _matmul(dx1b, w_out_f, 'nt', [F32], "proj_out_dx", stage=_pair_exchange([by_rows(g_out)], split[1:2]))
    rest = [n for n, _ in BIG[1:]]
    send_rest, sib_rest = [by_rows(g_out)] + send_mlp, [sib_out] + sib_mlp
    sums_rest = [_pair_sum(s, f, c_idx, f"grad_pair_sum_{n}") for n, s, f in zip(rest, send_rest, sib_rest)]
    d_ogla, d_gg, d_odn, d_dz, d_ga, d_gb, d_gla_norm, d_dn_norm = _merge_bwd(
        tok(o_gla), tok(o_dn), z_big, gla_norm, dn_norm, dmix, D)
    d_q, d_k, d_v, dzs_gla, d_w2h, d_gbh = _gla_bwd(seq(z_big), seq(z_small), w2h, gbh, st_all, seq(d_ogla), Bl, S, D)
    d_qa, d_ka, d_va, d_zs, d_alog_w, d_dtb_w = _dn_bwd(seq(acts[0]), seq(acts[1]), seq(acts[2]), seq(z_small), alog_w,
                                                        dtb_w, s_all, seq(d_odn), dzs_gla, Bl, S, D)
    conv_b = [_conv_bwd(z_big, conv_full, tok(g), grp, Bl, S, D) for grp, g in enumerate([d_qa, d_ka, d_va])]
    dz_big = jnp.concatenate([tok(d_q), tok(d_k), tok(d_v), d_gg, conv_b[0][0], conv_b[1][0], conv_b[2][0], d_dz, d_ga,
                              d_gb], axis=1)
    dz_small = tok(d_zs)
    d_w_big, *chips_rest = _matmul(dz_big, h, 'tn', [F32], "proj_in_dw", stage=_chip_scatter([b for _, b in sums_rest]))
    halves_rest = [_final_sum(f, got, me_idx, f"grad_final_sum_{n}") for n, (f, _), got in zip(rest, sums_rest, chips_rest)]
    (d_w_small,) = _matmul(dz_small, h, 'tn', [F32], "proj_in_narrow_dw")
    g_in = _join_w_in(d_w_big, d_w_small, D).reshape(4, n_in, D)
    (sib_in,) = _run_stage(_pair_exchange([g_in], split[:1]), "grad_pair_exchange_w_in")
    sum_in_f32, sum_in_bf16 = _pair_sum(g_in, sib_in, c_idx, "grad_pair_sum_w_in", split[0])
    dh_a, chips_in = _matmul(dz_big, w_big, 'nn', [F32], "proj_in_dx", stage=_chip_scatter([sum_in_bf16]))
    half_in = _final_sum(sum_in_f32, chips_in, me_idx, "grad_final_sum_w_in", split[0])
    (dh,) = _matmul(dz_small, w_small, 'nn', [F32], "proj_in_narrow_dx", epilogue=lambda r, e: (e + r,), extras=(dh_a,))
    grad_x, _, d_g_mix = _rmsnorm_bwd_add(xt, g_mix, dh, dx1, "rms1_bwd")
    my_halves = [half_in] + halves_rest
    from_pair = _pair_allgather(my_halves, split)
    reduced = {n: lax.dynamic_update_slice(o, hlf, (ic * hlf.shape[0], 0))
               for (n, _), o, hlf in zip(BIG[1:], from_pair[1:], halves_rest)}
    south = ic == 0
    g_in_t = jnp.concatenate([jnp.where(south, half_in, from_pair[0]), jnp.where(south, from_pair[0], half_in)],
                             axis=1).reshape(n_in, 1, D)

    d_w2 = jnp.swapaxes(d_w2h, 0, 1).reshape(ZS, D // 2)[:LOWRANK]
    small_grads = {'g_mix': d_g_mix, 'gla_w2': d_w2, 'gla_b': d_gbh.reshape(1, D // 2), 'gla_norm': d_gla_norm,
                   'dn_a_log': d_alog_w[:, 0, 0].reshape(1, DN_HEADS), 'dn_dt_bias': d_dtb_w[:, 0, 0].reshape(1, DN_HEADS),
                   'dn_norm': d_dn_norm, 'g_mlp': d_g_mlp, 'g_ple': d_g_ple, 'g_final': d_g_final}
    names = [n for n in SMALL if n != 'dn_conv']
    total = _allreduce_small([small_grads[n] for n in names] + [cb[1] for cb in conv_b] + [loss_tile[:1]],
                             "allreduce_small_grads")
    gsmall = dict(zip(names, total[:len(names)]))
    loss = total[-1][0, 0]
    my_cols = lambda g: lax.dynamic_slice_in_dim(g, j_me * (g.shape[1] // 4), g.shape[1] // 4, axis=1)
    gsmall['gla_w2'] = my_cols(gsmall['gla_w2'])
    gsmall['dn_conv'] = my_cols(jnp.concatenate(total[len(names):len(names) + 3], axis=1))

    g_o, d_o, m_o, v_o = {}, {}, {}, {}
    d_in_t, nm_in_t, nv_in_t = _adamw(w_in_t, g_in_t, m_in_t, v_in_t, "adamw_w_in")
    g_o['w_in'], d_o['w_in'], m_o['w_in'], v_o['w_in'] = [cols_last(t) for t in (g_in_t, d_in_t, nm_in_t, nv_in_t)]
    for n, _ in BIG[1:]:
        shp = wts[n].shape
        d2, nm2, nv2 = _adamw(shard2d[n], reduced[n], as2d(mom[n]), as2d(var[n]), f"adamw_{n}")
        g_o[n], d_o[n], m_o[n], v_o[n] = reduced[n].reshape(shp), d2.reshape(shp), nm2.reshape(shp), nv2.reshape(shp)
    ds, nms, nvs = _adamw_small([as2d(wts[n]) for n in SMALL], [as2d(gsmall[n]) for n in SMALL],
                                [as2d(mom[n]) for n in SMALL], [as2d(var[n]) for n in SMALL])
    for n, dd, mm, vv in zip(SMALL, ds, nms, nvs):
        shp = wts[n].shape
        g_o[n], d_o[n], m_o[n], v_o[n] = gsmall[n].reshape(shp), dd.reshape(shp), mm.reshape(shp), vv.reshape(shp)

    return (loss, grad_x.reshape(Bl, S, D), *[g_o[n] for n in WEIGHTS], *[d_o[n] for n in WEIGHTS],
            *[m_o[n] for n in WEIGHTS], *[v_o[n] for n in WEIGHTS])
```

```python
import functools

import jax
import jax.numpy as jnp
from jax import lax
from jax.experimental import pallas as pl
from jax.experimental.pallas import tpu as pltpu

F32 = jnp.float32
BF16 = jnp.bfloat16

CHUNK = 64
GLA_HEADS = 4
DN_HEADS = 16
LOWRANK = 16
GLA_TAU = 16.0
DN_CONV = 4
EPS = 1e-6
ZS = 128
A_LANE, B_LANE = LOWRANK, LOWRANK + DN_HEADS
ADAM_LR, ADAM_B1, ADAM_B2, ADAM_EPS, ADAM_WD, ADAM_STEP = 0.001, 0.9, 0.999, 1e-08, 0.01, 10

V7X_VMEM_BYTES = 64 * 1024 * 1024
VMEM_LIMIT_BYTES = V7X_VMEM_BYTES - 8 * 1024 * 1024
LANES = 128
SUBLANES = 8
MESH = pl.DeviceIdType.MESH
DN_HEADS_PER_STEP = 4
GLA_HEADS_PER_STEP = 2

WEIGHTS = ['g_mix', 'w_in', 'gla_w2', 'gla_b', 'gla_norm', 'dn_conv', 'dn_a_log', 'dn_dt_bias', 'dn_norm', 'w_out',
           'g_mlp', 'w_up', 'w_down', 'g_ple', 'w_ple_gate', 'w_ple_proj', 'g_final']
BIG = [('w_in', 1), ('w_out', 0), ('w_up', 1), ('w_down', 0), ('w_ple_gate', 0), ('w_ple_proj', 1)]
SMALL = [n for n in WEIGHTS if n not in dict(BIG)]

_NN, _NT, _TN = 'nn', 'nt', 'tn'


def _params(sem=None):
    return pltpu.CompilerParams(dimension_semantics=sem, vmem_limit_bytes=VMEM_LIMIT_BYTES)


def _dot(a, b, form, precision=None):
    o = a.ndim - 2
    contract = {_NN: ((1 + o,), (o,)), _NT: ((1 + o,), (1 + o,)), _TN: ((o,), (o,))}[form]
    batch = ((0,), (0,)) if o else ((), ())
    return lax.dot_general(a, b, (contract, batch), precision=precision, preferred_element_type=F32)


def _make_mm(cast, precision):
    def raw(a, b, dims):
        return _dot(cast(a), cast(b), dims, precision)

    @jax.custom_vjp
    def nn(a, b):
        return raw(a, b, _NN)
    nn.defvjp(lambda a, b: (raw(a, b, _NN), (a, b)), lambda r, g: (raw(g, r[1], _NT), raw(r[0], g, _TN)))

    @jax.custom_vjp
    def nt(a, b):
        return raw(a, b, _NT)
    nt.defvjp(lambda a, b: (raw(a, b, _NT), (a, b)), lambda r, g: (raw(g, r[1], _NN), raw(g, r[0], _TN)))

    @jax.custom_vjp
    def tn(a, b):
        return raw(a, b, _TN)
    tn.defvjp(lambda a, b: (raw(a, b, _TN), (a, b)), lambda r, g: (raw(r[1], g, _NT), raw(r[0], g, _NN)))
    return nn, nt, tn


_bnn, _bnt, _btn = _make_mm(lambda t: t.astype(BF16), None)
TRI_PRECISION = lax.Precision.HIGH


def _iota2(n, axis):
    return lax.broadcasted_iota(jnp.int32, (n, n), axis)


def _lower(n, strict=False):
    return (_iota2(n, 0) > _iota2(n, 1)) if strict else (_iota2(n, 0) >= _iota2(n, 1))


def _tri_times(tri, x):
    tri = tri.astype(F32)
    if x.ndim == 3:
        tri = jnp.broadcast_to(tri, (x.shape[0],) + tri.shape)
    return _dot(tri, x, _NN, lax.Precision.HIGHEST)


@jax.custom_vjp
def _cumsum_rows(x):
    return _tri_times(_lower(x.shape[-2]), x)


def _cumsum_rows_bwd(_, g):
    n = g.shape[-2]
    return (_tri_times(_iota2(n, 0) <= _iota2(n, 1), g),)


_cumsum_rows.defvjp(lambda x: (_cumsum_rows(x), None), _cumsum_rows_bwd)


def _tri_inv_impl(a):
    n = a.shape[-1]
    eye = (_iota2(n, 0) == _iota2(n, 1)).astype(F32)
    p = eye - a
    ak = a
    k = 2
    while k < n:
        prec, cast = (TRI_PRECISION, lambda t: t) if k == 2 else (None, lambda t: t.astype(BF16))
        ak = _dot(cast(ak), cast(ak), _NN, prec)
        p = p + _dot(cast(p), cast(ak), _NN, prec)
        k *= 2
    return p


@jax.custom_vjp
def _tri_inv(a):
    return _tri_inv_impl(a)


def _tri_inv_fwd(a):
    t = _tri_inv_impl(a)
    return t, t


def _tri_inv_bwd(t, g):
    tb = t.astype(BF16)
    tg = _dot(tb, g.astype(BF16), _TN)
    return (-_dot(tg.astype(BF16), tb, _NT),)


_tri_inv.defvjp(_tri_inv_fwd, _tri_inv_bwd)


def _shift_rows(x, s, down):
    n = x.shape[0]
    r = lax.broadcasted_iota(jnp.int32, x.shape, 0)
    if down:
        return jnp.where(r >= s, pltpu.roll(x, s, 0), 0.0)
    return jnp.where(r < n - s, pltpu.roll(x, n - s, 0), 0.0)


def _make_shift(s):
    @jax.custom_vjp
    def f(x):
        return _shift_rows(x, s, True)
    f.defvjp(lambda x: (_shift_rows(x, s, True), None), lambda _, g: (_shift_rows(g, s, False),))
    return f


def _sigmoid(x):
    return jax.nn.sigmoid(x)


def _silu(x):
    return x * jax.nn.sigmoid(x)


def _softplus(x):
    return jnp.maximum(x, 0.0) + jnp.log1p(jnp.exp(-jnp.abs(x)))


def _log_sigmoid(x):
    return -_softplus(-x)


def _rms(x, g):
    return x * lax.rsqrt(jnp.mean(x * x, axis=-1, keepdims=True) + EPS) * g


def _gla_chunk(q, k, v, zs, w2, gb, st, *, scale):
    c = q.shape[-2]
    logf = _log_sigmoid(_bnn(zs, w2) + gb) * (1.0 / GLA_TAU)
    bcum = _cumsum_rows(logf)
    b_last = jnp.sum(logf, axis=-2, keepdims=True)
    q_in = (q * scale) * jnp.exp(bcum)
    k_in = k * jnp.exp(-bcum)
    a = jnp.where(_lower(c), _bnt(q_in, k_in), 0.0)
    o = _bnn(a, v) + _bnt(q_in, st)
    k_dec = k * jnp.exp(b_last - bcum)
    st_new = st * jnp.exp(b_last) + _btn(v, k_dec)
    return o, st_new


def _dn_chunk(q, k, v, aw, bw, alog, dtb, s):
    c = q.shape[-2]
    incl, strict = _lower(c), _lower(c, True)
    g_w = -jnp.exp(alog) * _softplus(aw + dtb)
    beta_w = _sigmoid(bw)
    gcum_w = _cumsum_rows(g_w)
    lane0 = lax.broadcasted_iota(jnp.int32, gcum_w.shape, gcum_w.ndim - 1) == 0
    gcol = jnp.sum(jnp.where(lane0, gcum_w, 0.0), axis=-1, keepdims=True)
    d1 = jnp.broadcast_to(gcol, gcol.shape[:-1] + (c,))
    diff = jnp.where(incl, d1 - jnp.swapaxes(d1, -1, -2), 0.0)
    decay = jnp.where(incl, jnp.exp(diff), 0.0)
    k_beta = k * beta_w
    a = jnp.where(strict, _bnt(k_beta, k) * decay, 0.0)
    t = _tri_inv(a)
    egc = jnp.exp(gcum_w)
    u = _bnn(t, v * beta_w)
    w = _bnn(t, k_beta * egc)
    attn = jnp.where(incl, _bnt(q, k) * decay, 0.0)
    q_dec = q * egc
    g_last = jnp.sum(g_w, axis=-2, keepdims=True)
    k_dec = k * jnp.exp(g_last - gcum_w)
    v_new = u - _bnn(w, s)
    o = _bnn(q_dec, s) + _bnn(attn, v_new)
    s_new = s * jnp.exp(g_last) + _btn(k_dec, v_new)
    return o, s_new


def _conv_act(x, wrows, *, l2, scale):
    taps = len(wrows)
    y = None
    for j in range(taps):
        s = taps - 1 - j
        xs = x if s == 0 else _make_shift(s)(x)
        y = wrows[j] * xs if y is None else y + wrows[j] * xs
    y = _silu(y)
    if l2:
        y = y * lax.rsqrt(jnp.sum(y * y, axis=-1, keepdims=True) + EPS) * scale
    return y


def _merge_math(og, gg, od, dz, ga, gb, gn, dn):
    nsub = len(og)
    dv = nsub * og[0].shape[1]
    ssq = jnp.sum(og[0] * og[0], axis=-1, keepdims=True)
    for s in range(1, nsub):
        ssq = ssq + jnp.sum(og[s] * og[s], axis=-1, keepdims=True)
    r = lax.rsqrt(ssq * (1.0 / dv) + EPS)
    outs = []
    for s in range(nsub):
        a = og[s] * r * gn[s] * _silu(gg[s])
        b = _rms(od[s], dn) * _silu(dz[s])
        outs.append(_sigmoid(ga[s]) * a + _sigmoid(gb[s]) * b)
    return outs


def _pick(n, target, mult):
    best = None
    for d in range(mult, min(n, target) + 1, mult):
        if n % d == 0:
            best = d
    return best if best is not None else n


class _Stage:
    def __init__(self, inputs, out_shapes, n_sems, copies, aliases=None):
        self.inputs, self.out_shapes, self.n_sems, self.copies = list(inputs), list(out_shapes), n_sems, copies
        self.aliases = aliases or {}

    @property
    def sems(self):
        return [pltpu.SemaphoreType.DMA((self.n_sems,)), pltpu.SemaphoreType.DMA((self.n_sems,))]


def _host_stage(body, stage, n_in, n_out, grid):
    ci, co = len(stage.inputs), len(stage.out_shapes)

    def wrapped(*refs):
        ins, cins = refs[:n_in], refs[n_in:n_in + ci]
        outs, couts = refs[n_in + ci:n_in + ci + n_out], refs[n_in + ci + n_out:n_in + ci + n_out + co]
        scratch, sems = refs[n_in + ci + n_out + co:-2], refs[-2:]
        ids = [pl.program_id(d) for d in range(len(grid))]
        first, last = ids[0] == 0, ids[0] == grid[0] - 1
        for i, g in zip(ids[1:], grid[1:]):
            first, last = first & (i == 0), last & (i == g - 1)

        @pl.when(first)
        def _():
            for cp in stage.copies(cins, couts, *sems):
                cp.start()

        body(*ins, *outs, *scratch)

        @pl.when(last)
        def _():
            for cp in stage.copies(cins, couts, *sems):
                cp.wait()

    return wrapped


def _call(body, name, grid, in_specs, out_specs, out_shape, scratch, semantics, args, stage=None):
    if stage is None:
        return pl.pallas_call(body, name=name, grid=grid, in_specs=list(in_specs), out_specs=list(out_specs),
                              out_shape=list(out_shape), scratch_shapes=list(scratch), compiler_params=_params(semantics))(*args)
    n_in, n_out = len(in_specs), len(out_specs)
    return pl.pallas_call(
        _host_stage(body, stage, n_in, n_out, grid), name=name, grid=grid,
        in_specs=list(in_specs) + [ANY] * len(stage.inputs),
        out_specs=list(out_specs) + [ANY] * len(stage.out_shapes), out_shape=list(out_shape) + stage.out_shapes,
        scratch_shapes=list(scratch) + stage.sems,
        input_output_aliases={n_in + i: n_out + o for i, o in stage.aliases.items()},
        compiler_params=_params(("arbitrary",) * len(grid)),
    )(*args, *stage.inputs)


def _run_stage(stage, name):
    ci = len(stage.inputs)

    def body(*refs):
        cps = stage.copies(refs[:ci], refs[ci:-2], *refs[-2:])
        for cp in cps:
            cp.start()
        for cp in cps:
            cp.wait()

    return pl.pallas_call(body, name=name, in_specs=[ANY] * ci, out_specs=[ANY] * len(stage.out_shapes),
                          out_shape=stage.out_shapes, scratch_shapes=stage.sems,
                          input_output_aliases=dict(stage.aliases))(*stage.inputs)


def _matmul(a, b, form, out_dtypes, name, epilogue=None, extras=(), bm=1024, bn=1024, bk=2048,
            b_slots=False, out_slots=False, stage=None):
    ns, c = (b.shape[0], b.shape[2]) if b_slots else (1, None)
    b2 = b.shape[1:] if b_slots else b.shape
    if form == 'nn':
        (M, K), (K2, N) = a.shape, (b2[0], b2[1] * ns)
    elif form == 'nt':
        (M, K), (N, K2) = a.shape, (b2[0], b2[1] * ns)
    else:
        (K, M), (K2, N) = a.shape, b2
    assert K == K2 and not (b_slots and form == 'tn'), (a.shape, b.shape, form)
    bm, bn, bk = _pick(M, bm, SUBLANES), _pick(N, bn, LANES), _pick(K, bk, LANES)
    if b_slots:
        bn, bk = (_pick(c, bn, LANES), bk) if form == 'nn' else (bn, _pick(c, bk, LANES))
    if out_slots:
        oc = N // 4
        bn = _pick(oc, bn, LANES)
    nk = K // bk
    a_spec = pl.BlockSpec((bk, bm), lambda i, j, k: (k, i)) if form == 'tn' else pl.BlockSpec((bm, bk), lambda i, j, k: (i, k))
    if b_slots and form == 'nn':
        per = c // bn
        b_spec = pl.BlockSpec((None, bk, bn), lambda i, j, k: (j // per, k, j % per))
    elif b_slots:
        per = c // bk
        b_spec = pl.BlockSpec((None, bn, bk), lambda i, j, k: (k // per, j, k % per))
    elif form == 'nt':
        b_spec = pl.BlockSpec((bn, bk), lambda i, j, k: (j, k))
    else:
        b_spec = pl.BlockSpec((bk, bn), lambda i, j, k: (k, j))
    o_spec = pl.BlockSpec((bm, bn), lambda i, j, k: (i, j))
    if out_slots:
        oper = oc // bn
        out_spec = pl.BlockSpec((None, bm, bn), lambda i, j, k: (j // oper, i, j % oper))
        out_shape = [jax.ShapeDtypeStruct((4, M, oc), d) for d in out_dtypes]
    else:
        out_spec = o_spec
        out_shape = [jax.ShapeDtypeStruct((M, N), d) for d in out_dtypes]
    ne, no = len(extras), len(out_dtypes)

    def finish(r, extra_refs, out_refs):
        outs = (r,) if epilogue is None else epilogue(r, *[e[...] for e in extra_refs])
        for ref, o in zip(out_refs, outs):
            ref[...] = o.astype(ref.dtype)

    def body_one(a_ref, b_ref, *rest):
        finish(_dot(a_ref[...].astype(BF16), b_ref[...].astype(BF16), form), rest[:ne], rest[ne:ne + no])

    def body_acc(a_ref, b_ref, *rest):
        extra_refs, out_refs, acc = rest[:ne], rest[ne:ne + no], rest[ne + no]
        k = pl.program_id(2)
        part = _dot(a_ref[...].astype(BF16), b_ref[...].astype(BF16), form)

        @pl.when(k == 0)
        def _():
            acc[...] = part

        @pl.when((k > 0) & (k < nk - 1))
        def _():
            acc[...] += part

        @pl.when(k == nk - 1)
        def _():
            finish(acc[...] + part, extra_refs, out_refs)

    return _call(body_one if nk == 1 else body_acc, name, (M // bm, N // bn, nk), [a_spec, b_spec] + [o_spec] * ne,
                 [out_spec] * no, out_shape, [] if nk == 1 else [pltpu.VMEM((bm, bn), F32)],
                 ("parallel", "parallel", "arbitrary"), (a, b, *extras), stage)


def _rowwise(fn, rows, consts, row_outs, acc_outs, name, bt=256):
    T = rows[0].shape[0]
    bt = _pick(T, bt, SUBLANES)
    nr, nc, no, na = len(rows), len(consts), len(row_outs), len(acc_outs)

    def body(*refs):
        r_in, c_in = refs[:nr], refs[nr:nr + nc]
        r_out, a_out = refs[nr + nc:nr + nc + no], refs[nr + nc + no:]
        ro, ao = fn([r[...] for r in r_in], [c[...] for c in c_in])
        for ref, o in zip(r_out, ro):
            ref[...] = o.astype(ref.dtype)
        if na:
            @pl.when(pl.program_id(0) == 0)
            def _():
                for ref in a_out:
                    ref[...] = jnp.zeros_like(ref)
            for ref, o in zip(a_out, ao):
                ref[...] += o

    whole = lambda shp: pl.BlockSpec(shp, lambda i: (0,) * len(shp))
    return pl.pallas_call(
        body, name=name, grid=(T // bt,),
        in_specs=[pl.BlockSpec((bt, r.shape[1]), lambda i: (i, 0)) for r in rows] + [whole(c.shape) for c in consts],
        out_specs=[pl.BlockSpec((bt, w), lambda i: (i, 0)) for w, _ in row_outs] + [whole(s) for s in acc_outs],
        out_shape=[jax.ShapeDtypeStruct((T, w), d) for w, d in row_outs] + [jax.ShapeDtypeStruct(s, F32) for s in acc_outs],
        compiler_params=_params(("arbitrary",)),
    )(*rows, *consts)


def _rmsnorm_fwd(x, g, name):
    return _rowwise(lambda r, c: ([_rms(r[0], c[0])], []), [x], [g], [(x.shape[1], BF16)], [], name)[0]


def _rmsnorm_bwd_add(x, g, dh, dres, name):
    D = x.shape[1]

    def fn(r, c):
        _, vjp = jax.vjp(_rms, r[0], c[0])
        dx, dg = vjp(r[1])
        dx = dx + r[2]
        return [dx, dx], [dg]
    return _rowwise(fn, [x, dh, dres], [g], [(D, F32), (D, BF16)], [(1, D)], name)


def _loss_fwd_bwd(x3, g, target, name):
    D = x3.shape[1]

    def fn(r, c):
        def row_loss(x, gain):
            err = _rms(x, gain) - r[1]
            return 0.5 * jnp.mean(err * err, axis=-1, keepdims=True)
        lrow, vjp = jax.vjp(row_loss, r[0], c[0])
        dx, dg = vjp(jnp.ones_like(lrow))
        tile = jnp.broadcast_to(jnp.sum(lrow, axis=0, keepdims=True), (SUBLANES, LANES))
        return [dx], [tile, dg]
    return _rowwise(fn, [x3, target], [g], [(D, F32)], [(SUBLANES, LANES), (1, D)], name)


def _ple_bwd(dx3, gp, pp, name):
    D = dx3.shape[1]

    def fn(r, c):
        s = _sigmoid(r[1])
        return [r[0] * r[2] * s * (1.0 - s), r[0] * s], []
    return _rowwise(fn, [dx3, gp, pp], [], [(D, BF16), (D, BF16)], [], name)


def _adamw_math(w, g, m, v):
    nm = ADAM_B1 * m + (1.0 - ADAM_B1) * g
    nv = ADAM_B2 * v + (1.0 - ADAM_B2) * (g * g)
    m_hat = nm / (1.0 - ADAM_B1 ** ADAM_STEP)
    v_hat = nv / (1.0 - ADAM_B2 ** ADAM_STEP)
    return -ADAM_LR * (m_hat / (jnp.sqrt(v_hat) + ADAM_EPS) + ADAM_WD * w), nm, nv


def _adamw(w, g, m, v, name):
    R, C = w.shape[0], w.shape[-1]
    lanes = -(-C // LANES) * LANES
    if w.ndim == 2:
        bt = _pick(R, max(SUBLANES, (1 << 18) // lanes // SUBLANES * SUBLANES), SUBLANES)
        spec = pl.BlockSpec((bt, C), lambda i: (i, 0))
    else:
        bt = _pick(R, max(1, (1 << 18) // lanes), 1)
        spec = pl.BlockSpec((bt, 1, C), lambda i: (i, 0, 0))

    def body(w_ref, g_ref, m_ref, v_ref, d_ref, nm_ref, nv_ref):
        d_ref[...], nm_ref[...], nv_ref[...] = _adamw_math(w_ref[...], g_ref[...], m_ref[...], v_ref[...])

    return pl.pallas_call(
        body, name=name, grid=(R // bt,), in_specs=[spec] * 4, out_specs=[spec] * 3,
        out_shape=[jax.ShapeDtypeStruct(w.shape, F32)] * 3, compiler_params=_params(("parallel",)),
    )(w, g, m, v)


def _adamw_small(ws, gs, ms, vs):
    n = len(ws)

    def body(*refs):
        for i in range(n):
            d, nm, nv = _adamw_math(refs[i][...], refs[n + i][...], refs[2 * n + i][...], refs[3 * n + i][...])
            refs[4 * n + i][...], refs[5 * n + i][...], refs[6 * n + i][...] = d, nm, nv

    VMEM = pl.BlockSpec(memory_space=pltpu.VMEM)
    shapes = [jax.ShapeDtypeStruct(w.shape, F32) for w in ws]
    outs = pl.pallas_call(body, name="adamw_small", in_specs=[VMEM] * (4 * n), out_specs=[VMEM] * (3 * n),
                          out_shape=shapes * 3)(*ws, *gs, *ms, *vs)
    return outs[:n], outs[n:2 * n], outs[2 * n:]


def _gla_fwd(z_big, z_small, w2h, gbh, Bl, S, D, stage=None):
    NC, dk, dv, HB = S // CHUNK, D // (2 * GLA_HEADS), D // GLA_HEADS, GLA_HEADS_PER_STEP
    HG = GLA_HEADS // HB
    chains = [(hh, bb) for hh in range(HB) for bb in range(Bl)]
    G = len(chains)
    fn = functools.partial(_gla_chunk, scale=dk ** -0.5)

    def body(q, k, v, z, w2, gb, o_ref, stall_ref, st):
        n, g = pl.program_id(0), pl.program_id(1)

        @pl.when(n == 0)
        def _():
            st[g] = jnp.zeros((G, dv, dk), F32)
        s0 = st[g]
        stall_ref[...] = s0.reshape(HB, Bl, dv, dk)
        qk = lambda r: jnp.stack([r[bb, :, hh * dk:(hh + 1) * dk] for hh, bb in chains])
        o, s_new = fn(qk(q), qk(k), jnp.stack([v[bb, :, hh * dv:(hh + 1) * dv] for hh, bb in chains]),
                      jnp.stack([z[bb] for _, bb in chains]), jnp.stack([w2[hh] for hh, _ in chains]),
                      jnp.stack([gb[hh] for hh, _ in chains]), s0)
        for i, (hh, bb) in enumerate(chains):
            o_ref[bb, :, hh * dv:(hh + 1) * dv] = o[i]
        st[g] = s_new

    return _call(
        body, "gla_fwd", (NC, HG),
        [pl.BlockSpec((Bl, CHUNK, HB * dk), lambda n, g: (0, n, g)),
         pl.BlockSpec((Bl, CHUNK, HB * dk), lambda n, g: (0, n, HG + g)),
         pl.BlockSpec((Bl, CHUNK, HB * dv), lambda n, g: (0, n, HG + g)),
         pl.BlockSpec((Bl, CHUNK, ZS), lambda n, g: (0, n, 0)),
         pl.BlockSpec((HB, ZS, dk), lambda n, g: (g, 0, 0)),
         pl.BlockSpec((HB, 1, dk), lambda n, g: (g, 0, 0))],
        [pl.BlockSpec((Bl, CHUNK, HB * dv), lambda n, g: (0, n, g)),
         pl.BlockSpec((HB, Bl, None, dv, dk), lambda n, g: (g, 0, n, 0, 0))],
        [jax.ShapeDtypeStruct((Bl, S, D), F32), jax.ShapeDtypeStruct((GLA_HEADS, Bl, NC, dv, dk), F32)],
        [pltpu.VMEM((HG, G, dv, dk), F32)], ("arbitrary", "arbitrary"), (z_big, z_big, z_big, z_small, w2h, gbh), stage)


def _gla_bwd(z_big, z_small, w2h, gbh, st_all, do, Bl, S, D):
    NC, dk, dv, HB = S // CHUNK, D // (2 * GLA_HEADS), D // GLA_HEADS, GLA_HEADS_PER_STEP
    HG = GLA_HEADS // HB
    chains = [(hh, bb) for hh in range(HB) for bb in range(Bl)]
    G = len(chains)
    fn = functools.partial(_gla_chunk, scale=dk ** -0.5)

    def body(q, k, v, z, w2, gb, st0, do_ref, dq_ref, dk_ref, dv_ref, dzs_ref, dw2_ref, dgb_ref, dst):
        n, g = pl.program_id(0), pl.program_id(1)

        @pl.when(n == 0)
        def _():
            dst[g] = jnp.zeros((G, dv, dk), F32)

        @pl.when((n == 0) & (g == 0))
        def _():
            dw2_ref[...] = jnp.zeros_like(dw2_ref)
            dgb_ref[...] = jnp.zeros_like(dgb_ref)

        qk = lambda r: jnp.stack([r[bb, :, hh * dk:(hh + 1) * dk] for hh, bb in chains])
        vv = lambda r: jnp.stack([r[bb, :, hh * dv:(hh + 1) * dv] for hh, bb in chains])
        _, vjp = jax.vjp(fn, qk(q), qk(k), vv(v), jnp.stack([z[bb] for _, bb in chains]),
                         jnp.stack([w2[hh] for hh, _ in chains]), jnp.stack([gb[hh] for hh, _ in chains]),
                         st0[...].reshape(G, dv, dk))
        dq, dkk, dvv, dzs, dw2, dgb, dst0 = vjp((vv(do_ref), dst[g]))
        for i, (hh, bb) in enumerate(chains):
            dq_ref[bb, :, hh * dk:(hh + 1) * dk] = dq[i].astype(dq_ref.dtype)
            dk_ref[bb, :, hh * dk:(hh + 1) * dk] = dkk[i].astype(dk_ref.dtype)
            dv_ref[bb, :, hh * dv:(hh + 1) * dv] = dvv[i].astype(dv_ref.dtype)
            dw2_ref[g * HB + hh] += dw2[i]
            dgb_ref[g * HB + hh] += dgb[i]
        for bb in range(Bl):
            tot = sum(dzs[i] for i, (_, b2) in enumerate(chains) if b2 == bb)

            @pl.when(g == 0)
            def _():
                dzs_ref[bb] = tot

            @pl.when(g > 0)
            def _():
                dzs_ref[bb] += tot
        dst[g] = dst0

    rn = lambda n: NC - 1 - n
    return pl.pallas_call(
        body, name="gla_bwd", grid=(NC, HG),
        in_specs=[pl.BlockSpec((Bl, CHUNK, HB * dk), lambda n, g: (0, rn(n), g)),
                  pl.BlockSpec((Bl, CHUNK, HB * dk), lambda n, g: (0, rn(n), HG + g)),
                  pl.BlockSpec((Bl, CHUNK, HB * dv), lambda n, g: (0, rn(n), HG + g)),
                  pl.BlockSpec((Bl, CHUNK, ZS), lambda n, g: (0, rn(n), 0)),
                  pl.BlockSpec((HB, ZS, dk), lambda n, g: (g, 0, 0)),
                  pl.BlockSpec((HB, 1, dk), lambda n, g: (g, 0, 0)),
                  pl.BlockSpec((HB, Bl, None, dv, dk), lambda n, g: (g, 0, rn(n), 0, 0)),
                  pl.BlockSpec((Bl, CHUNK, HB * dv), lambda n, g: (0, rn(n), g))],
        out_specs=[pl.BlockSpec((Bl, CHUNK, HB * dk), lambda n, g: (0, rn(n), g)),
                   pl.BlockSpec((Bl, CHUNK, HB * dk), lambda n, g: (0, rn(n), g)),
                   pl.BlockSpec((Bl, CHUNK, HB * dv), lambda n, g: (0, rn(n), g)),
                   pl.BlockSpec((Bl, CHUNK, ZS), lambda n, g: (0, rn(n), 0)),
                   pl.BlockSpec((GLA_HEADS, ZS, dk), lambda n, g: (0, 0, 0)),
                   pl.BlockSpec((GLA_HEADS, 1, dk), lambda n, g: (0, 0, 0))],
        out_shape=[jax.ShapeDtypeStruct((Bl, S, D // 2), BF16), jax.ShapeDtypeStruct((Bl, S, D // 2), BF16),
                   jax.ShapeDtypeStruct((Bl, S, D), BF16), jax.ShapeDtypeStruct((Bl, S, ZS), F32),
                   jax.ShapeDtypeStruct((GLA_HEADS, ZS, dk), F32), jax.ShapeDtypeStruct((GLA_HEADS, 1, dk), F32)],
        scratch_shapes=[pltpu.VMEM((HG, G, dv, dk), F32)],
        compiler_params=_params(("arbitrary", "arbitrary")),
    )(z_big, z_big, z_big, z_small, w2h, gbh, st_all, do)


def _conv_fwd(z_big, conv_w, grp, Bl, S, D):
    d = D // DN_HEADS
    l2, scale = grp < 2, (d ** -0.5 if grp == 0 else 1.0)
    x_blk0 = (3 * D + grp * D) // d

    def body(x_ref, w_ref, o_ref):
        wrows = [w_ref[j:j + 1, :] for j in range(DN_CONV)]
        o_ref[...] = _conv_act(x_ref[...], wrows, l2=l2, scale=scale)

    return pl.pallas_call(
        body, name=f"conv_fwd{grp}", grid=(Bl, DN_HEADS),
        in_specs=[pl.BlockSpec((S, d), lambda b, j: (b, x_blk0 + j)),
                  pl.BlockSpec((DN_CONV, d), lambda b, j: (0, grp * DN_HEADS + j))],
        out_specs=pl.BlockSpec((S, d), lambda b, j: (b, j)),
        out_shape=jax.ShapeDtypeStruct((Bl * S, D), F32),
        compiler_params=_params(("parallel", "parallel")),
    )(z_big, conv_w)


def _conv_bwd(z_big, conv_w, dact, grp, Bl, S, D):
    d = D // DN_HEADS
    l2, scale = grp < 2, (d ** -0.5 if grp == 0 else 1.0)
    x_blk0 = (3 * D + grp * D) // d

    def body(x_ref, w_ref, g_ref, dx_ref, dw_ref):
        @pl.when(pl.program_id(1) == 0)
        def _():
            dw_ref[...] = jnp.zeros_like(dw_ref)
        wrows = [w_ref[j:j + 1, :] for j in range(DN_CONV)]
        _, vjp = jax.vjp(lambda x, wr: _conv_act(x, wr, l2=l2, scale=scale), x_ref[...], wrows)
        dx, dwr = vjp(g_ref[...])
        dx_ref[...] = dx.astype(dx_ref.dtype)
        for j in range(DN_CONV):
            dw_ref[j:j + 1, :] += dwr[j]

    return pl.pallas_call(
        body, name=f"conv_bwd{grp}", grid=(DN_HEADS, Bl),
        in_specs=[pl.BlockSpec((S, d), lambda j, b: (b, x_blk0 + j)),
                  pl.BlockSpec((DN_CONV, d), lambda j, b: (0, grp * DN_HEADS + j)),
                  pl.BlockSpec((S, d), lambda j, b: (b, j))],
        out_specs=[pl.BlockSpec((S, d), lambda j, b: (b, j)), pl.BlockSpec((DN_CONV, d), lambda j, b: (0, j))],
        out_shape=[jax.ShapeDtypeStruct((Bl * S, D), BF16), jax.ShapeDtypeStruct((DN_CONV, D), F32)],
        compiler_params=_params(("arbitrary", "arbitrary")),
    )(z_big, conv_w, dact)


def _lane_column(zb, lane, width):
    pick = lax.broadcasted_iota(jnp.int32, zb.shape, 1) == lane
    return jnp.broadcast_to(jnp.sum(jnp.where(pick, zb, 0.0), axis=-1, keepdims=True), (zb.shape[0], width))


def _dn_fwd(qa, ka, va, z_small, alog, dtb, Bl, S, D, stage=None):
    NC, d, HB = S // CHUNK, D // DN_HEADS, DN_HEADS_PER_STEP
    HG = DN_HEADS // HB
    chains = [(hh, bb) for hh in range(HB) for bb in range(Bl)]
    G = len(chains)

    def body(q, k, v, z, al, dt, o_ref, sall_ref, st):
        n, g = pl.program_id(0), pl.program_id(1)

        @pl.when(n == 0)
        def _():
            st[g] = jnp.zeros((G, d, d), F32)
        tok_in = lambda r: jnp.stack([r[bb, :, hh * d:(hh + 1) * d] for hh, bb in chains])
        head_in = lambda r: jnp.stack([r[hh] for hh, _ in chains])
        gate_in = lambda lane0: jnp.stack([_lane_column(z[bb], lane0 + g * HB + hh, d) for hh, bb in chains])
        s0 = st[g]
        sall_ref[...] = s0.reshape(HB, Bl, d, d)
        o, s_new = _dn_chunk(tok_in(q), tok_in(k), tok_in(v), gate_in(A_LANE), gate_in(B_LANE), head_in(al), head_in(dt), s0)
        for i, (hh, bb) in enumerate(chains):
            o_ref[bb, :, hh * d:(hh + 1) * d] = o[i]
        st[g] = s_new

    tok = pl.BlockSpec((Bl, CHUNK, HB * d), lambda n, g: (0, n, g))
    per_head = pl.BlockSpec((HB, 1, d), lambda n, g: (g, 0, 0))
    return _call(
        body, "dn_fwd", (NC, HG),
        [tok, tok, tok, pl.BlockSpec((Bl, CHUNK, ZS), lambda n, g: (0, n, 0)), per_head, per_head],
        [tok, pl.BlockSpec((HB, Bl, None, d, d), lambda n, g: (g, 0, n, 0, 0))],
        [jax.ShapeDtypeStruct((Bl, S, D), F32), jax.ShapeDtypeStruct((DN_HEADS, Bl, NC, d, d), F32)],
        [pltpu.VMEM((HG, G, d, d), F32)], ("arbitrary", "arbitrary"), (qa, ka, va, z_small, alog, dtb), stage)


def _dn_bwd(qa, ka, va, z_small, alog, dtb, s_all, do, dzs_gla, Bl, S, D, stage=None):
    NC, d, HB = S // CHUNK, D // DN_HEADS, DN_HEADS_PER_STEP
    HG = DN_HEADS // HB
    chains = [(hh, bb) for hh in range(HB) for bb in range(Bl)]
    G = len(chains)

    def lanesum(t):
        return jnp.sum(t, axis=-1, keepdims=True)

    def body(q, k, v, z, al, dt, s0_ref, do_ref, dzg_ref, dq_ref, dk_ref, dv_ref, dzs_ref, dal_ref, ddt_ref, dst):
        n, g = pl.program_id(0), pl.program_id(1)

        @pl.when(n == 0)
        def _():
            dst[g] = jnp.zeros((G, d, d), F32)

        @pl.when((n == 0) & (g == 0))
        def _():
            dal_ref[...] = jnp.zeros_like(dal_ref)
            ddt_ref[...] = jnp.zeros_like(ddt_ref)

        tok_in = lambda r: jnp.stack([r[bb, :, hh * d:(hh + 1) * d] for hh, bb in chains])
        head_in = lambda r: jnp.stack([r[hh] for hh, _ in chains])
        gate_in = lambda lane0: jnp.stack([_lane_column(z[bb], lane0 + g * HB + hh, d) for hh, bb in chains])
        _, vjp = jax.vjp(_dn_chunk, tok_in(q), tok_in(k), tok_in(v), gate_in(A_LANE), gate_in(B_LANE), head_in(al),
                         head_in(dt), s0_ref[...].reshape(G, d, d))
        dq, dkk, dvv, da, db, dal, ddt, ds0 = vjp((tok_in(do_ref), dst[g]))
        da, db = lanesum(da), lanesum(db)
        dal = jnp.broadcast_to(lanesum(dal), (G, 1, d))
        ddt = jnp.broadcast_to(lanesum(ddt), (G, 1, d))
        lane = lax.broadcasted_iota(jnp.int32, (CHUNK, ZS), 1)
        for bb in range(Bl):
            part = jnp.zeros((CHUNK, ZS), F32)
            for i, (hh, b2) in enumerate(chains):
                if b2 == bb:
                    h = g * HB + hh
                    part = part + jnp.where(lane == A_LANE + h, da[i], 0.0) + jnp.where(lane == B_LANE + h, db[i], 0.0)

            @pl.when(g == 0)
            def _():
                dzs_ref[bb] = jnp.where(lane < LOWRANK, dzg_ref[bb], 0.0) + part

            @pl.when(g > 0)
            def _():
                dzs_ref[bb] += part
        for i, (hh, bb) in enumerate(chains):
            cols = slice(hh * d, (hh + 1) * d)
            dq_ref[bb, :, cols] = dq[i]
            dk_ref[bb, :, cols] = dkk[i]
            dv_ref[bb, :, cols] = dvv[i]
            dal_ref[g * HB + hh] += dal[i]
            ddt_ref[g * HB + hh] += ddt[i]
        dst[g] = ds0

    rn = lambda n: NC - 1 - n
    tok = pl.BlockSpec((Bl, CHUNK, HB * d), lambda n, g: (0, rn(n), g))
    zsb = pl.BlockSpec((Bl, CHUNK, ZS), lambda n, g: (0, rn(n), 0))
    per_head = pl.BlockSpec((HB, 1, d), lambda n, g: (g, 0, 0))
    all_heads = pl.BlockSpec((DN_HEADS, 1, d), lambda n, g: (0, 0, 0))
    tok_shape = jax.ShapeDtypeStruct((Bl, S, D), F32)
    head_shape = jax.ShapeDtypeStruct((DN_HEADS, 1, d), F32)
    return _call(
        body, "dn_bwd", (NC, HG),
        [tok, tok, tok, zsb, per_head, per_head,
         pl.BlockSpec((HB, Bl, None, d, d), lambda n, g: (g, 0, rn(n), 0, 0)), tok, zsb],
        [tok, tok, tok, zsb, all_heads, all_heads],
        [tok_shape, tok_shape, tok_shape, jax.ShapeDtypeStruct((Bl, S, ZS), F32), head_shape, head_shape],
        [pltpu.VMEM((HG, G, d, d), F32)], ("arbitrary", "arbitrary"),
        (qa, ka, va, z_small, alog, dtb, s_all, do, dzs_gla), stage)


def _merge_specs(D, bt):
    dv, w = D // GLA_HEADS, D // DN_HEADS
    col = lambda off: pl.BlockSpec((bt, dv), lambda i, h: (i, off // dv + h))
    return dv, w, col


def _merge_load(refs, nsub, w):
    return [[r[:, s * w:(s + 1) * w] for s in range(nsub)] for r in refs]


def _merge_fwd(o_gla, o_dn, z_big, gla_norm, dn_norm, D, bt=256, stage=None):
    T = o_gla.shape[0]
    bt = _pick(T, bt, SUBLANES)
    dv, w, col = _merge_specs(D, bt)
    nsub = dv // w

    def body(og, gg, od, dz, ga, gb, gn, dn, out):
        ogl, ggl, odl, dzl, gal, gbl = _merge_load([og, gg, od, dz, ga, gb], nsub, w)
        gnl = [gn[:, s * w:(s + 1) * w] for s in range(nsub)]
        outs = _merge_math(ogl, ggl, odl, dzl, gal, gbl, gnl, dn[...])
        for s in range(nsub):
            out[:, s * w:(s + 1) * w] = outs[s].astype(out.dtype)

    return _call(
        body, "merge_fwd", (T // bt, GLA_HEADS),
        [col(0), col(2 * D), col(0), col(6 * D), col(7 * D), col(8 * D),
         pl.BlockSpec((1, dv), lambda i, h: (0, 0)), pl.BlockSpec((1, w), lambda i, h: (0, 0))],
        [col(0)], [jax.ShapeDtypeStruct((T, D), BF16)], [], ("parallel", "parallel"),
        (o_gla, z_big, o_dn, z_big, z_big, z_big, gla_norm, dn_norm), stage)


def _merge_bwd(o_gla, o_dn, z_big, gla_norm, dn_norm, dmix, D, bt=256):
    T = o_gla.shape[0]
    bt = _pick(T, bt, SUBLANES)
    dv, w, col = _merge_specs(D, bt)
    nsub = dv // w

    def body(og, gg, od, dz, ga, gb, gn, dn, dm, dog, dgg, dod, ddz, dga, dgb, dgn, ddn):
        @pl.when((pl.program_id(0) == 0) & (pl.program_id(1) == 0))
        def _():
            dgn[...] = jnp.zeros_like(dgn)
            ddn[...] = jnp.zeros_like(ddn)

        ogl, ggl, odl, dzl, gal, gbl, dml = _merge_load([og, gg, od, dz, ga, gb, dm], nsub, w)
        gnl = [gn[:, s * w:(s + 1) * w] for s in range(nsub)]
        _, vjp = jax.vjp(_merge_math, ogl, ggl, odl, dzl, gal, gbl, gnl, dn[...])
        g_og, g_gg, g_od, g_dz, g_ga, g_gb, g_gn, g_dn = vjp(dml)
        for s in range(nsub):
            sl = slice(s * w, (s + 1) * w)
            dog[:, sl] = g_og[s]
            dgg[:, sl] = g_gg[s].astype(dgg.dtype)
            dod[:, sl] = g_od[s]
            ddz[:, sl] = g_dz[s].astype(ddz.dtype)
            dga[:, sl] = g_ga[s].astype(dga.dtype)
            dgb[:, sl] = g_gb[s].astype(dgb.dtype)
            dgn[:, sl] += g_gn[s]
        ddn[...] += g_dn

    f32s, bf16s = jax.ShapeDtypeStruct((T, D), F32), jax.ShapeDtypeStruct((T, D), BF16)
    return pl.pallas_call(
        body, name="merge_bwd", grid=(T // bt, GLA_HEADS),
        in_specs=[col(0), col(2 * D), col(0), col(6 * D), col(7 * D), col(8 * D),
                  pl.BlockSpec((1, dv), lambda i, h: (0, 0)), pl.BlockSpec((1, w), lambda i, h: (0, 0)), col(0)],
        out_specs=[col(0)] * 6 + [pl.BlockSpec((1, dv), lambda i, h: (0, 0)), pl.BlockSpec((1, w), lambda i, h: (0, 0))],
        out_shape=[f32s, bf16s, f32s, bf16s, bf16s, bf16s,
                   jax.ShapeDtypeStruct((1, dv), F32), jax.ShapeDtypeStruct((1, w), F32)],
        compiler_params=_params(("arbitrary", "arbitrary")),
    )(o_gla, z_big, o_dn, z_big, z_big, z_big, gla_norm, dn_norm, dmix)


def _place():
    return lax.axis_index("x"), lax.axis_index("y"), lax.axis_index("c")


def _other_chips(x, y):
    return [(1 - x, y), (x, 1 - y), (1 - x, 1 - y)]


def _rcopy(src, dst, send_sem, recv_sem, dev):
    return pltpu.make_async_remote_copy(src_ref=src, dst_ref=dst, send_sem=send_sem, recv_sem=recv_sem,
                                        device_id=dev, device_id_type=MESH)


ANY = pl.BlockSpec(memory_space=pl.ANY)


ROWS, COLS = 'rows', 'cols'


def _half(ref, hc, by, lead=()):
    shape = ref.shape[len(lead):]
    if by == ROWS:
        rh = shape[0] // 2
        idx = (pl.ds(pl.multiple_of(hc * rh, 16), rh),) + (slice(None),) * (len(shape) - 1)
    else:
        ch = shape[-1] // 2
        idx = (slice(None),) * (len(shape) - 1) + (pl.ds(pl.multiple_of(hc * ch, LANES), ch),)
    return ref.at[(*lead, *idx)]


def _half_shape(shape, by):
    return (shape[0] // 2,) + tuple(shape[1:]) if by == ROWS else tuple(shape[:-1]) + (shape[-1] // 2,)


def _gather_ici(shards, by):
    nw = len(shards)

    def copies(srcs, outs, send_sems, recv_sems):
        x, y, c = _place()
        return [_rcopy(_half(srcs[w], c, by[w]), _half(outs[w], c, by[w], (2 * x + y,)),
                       send_sems.at[3 * w + k], recv_sems.at[3 * w + k], (px, py, c))
                for w in range(nw) for k, (px, py) in enumerate(_other_chips(x, y))]

    return _Stage(shards, [jax.ShapeDtypeStruct((4,) + s.shape, s.dtype) for s in shards], 3 * nw, copies)


def _gather_pass(gathered, by):
    nw = len(gathered)

    def copies(srcs, outs, send_sems, recv_sems):
        x, y, c = _place()
        cps = []
        for w in range(nw):
            for k, (px, py) in enumerate(_other_chips(x, y)):
                slot = (2 * px + py,)
                cps.append(_rcopy(_half(srcs[w], c, by[w], slot), _half(outs[w], c, by[w], slot),
                                  send_sems.at[3 * w + k], recv_sems.at[3 * w + k], (x, y, 1 - c)))
        return cps

    return _Stage(gathered, [jax.ShapeDtypeStruct(g.shape, g.dtype) for g in gathered], 3 * nw, copies,
                  aliases={w: w for w in range(nw)})


def _pair_exchange(ps, by):
    nw = len(ps)

    def copies(srcs, outs, send_sems, recv_sems):
        x, y, c = _place()
        return [_rcopy(_half(srcs[w], 1 - c, by[w], (slice(None),)), outs[w], send_sems.at[w], recv_sems.at[w], (x, y, 1 - c))
                for w in range(nw)]

    return _Stage(ps, [jax.ShapeDtypeStruct((4,) + _half_shape(p.shape[1:], b), p.dtype) for p, b in zip(ps, by)], nw, copies)


def _sum_blocks(half_shape, by):
    rh, ch = half_shape
    if by == ROWS:
        lanes = -(-ch // LANES) * LANES
        bt = _pick(rh, max(16, (3 << 18) // lanes // 16 * 16), 16)
        return (bt, ch), rh // bt, lambda i: (i, 0)
    bc = _pick(ch, max(LANES, (5 << 18) // rh // LANES * LANES), LANES)
    return (rh, bc), ch // bc, lambda i: (0, i)


def _pair_sum(p, got, c_idx, name, by=ROWS):
    hs = got.shape[1:]
    blk, nb, pos = _sum_blocks(hs, by)

    def body(c_ref, a, b, of, ob):
        s = a[...] + b[...]
        of[...] = s
        ob[...] = s.astype(BF16)

    def mine(j, i, c_ref):
        r, cc = pos(c_ref[0] * nb + i)
        return (j, r, cc)

    spec = pl.BlockSpec((None,) + blk, lambda j, i, c_ref: (j,) + pos(i))
    return pl.pallas_call(
        body, name=name,
        grid_spec=pltpu.PrefetchScalarGridSpec(
            num_scalar_prefetch=1, grid=(4, nb),
            in_specs=[pl.BlockSpec((None,) + blk, mine), spec], out_specs=[spec, spec]),
        out_shape=[jax.ShapeDtypeStruct((4,) + hs, F32), jax.ShapeDtypeStruct((4,) + hs, BF16)],
        compiler_params=_params(("parallel", "parallel")),
    )(c_idx, p, got)


def _chip_scatter(qbs):
    nw = len(qbs)

    def copies(srcs, outs, send_sems, recv_sems):
        x, y, c = _place()
        return [_rcopy(srcs[w].at[2 * px + py], outs[w].at[k], send_sems.at[3 * w + k], recv_sems.at[3 * w + k], (px, py, c))
                for w in range(nw) for k, (px, py) in enumerate(_other_chips(x, y))]

    return _Stage(qbs, [jax.ShapeDtypeStruct((3,) + q.shape[1:], q.dtype) for q in qbs], 3 * nw, copies)


def _final_sum(qf, got, me_idx, name, by=ROWS):
    hs = qf.shape[1:]
    blk, nb, pos = _sum_blocks(hs, by)

    def body(me_ref, a, b, o):
        o[...] = ((a[...] + b[0].astype(F32)) + b[1].astype(F32)) + b[2].astype(F32)

    return pl.pallas_call(
        body, name=name,
        grid_spec=pltpu.PrefetchScalarGridSpec(
            num_scalar_prefetch=1, grid=(nb,),
            in_specs=[pl.BlockSpec((None,) + blk, lambda i, me_ref: (me_ref[0],) + pos(i)),
                      pl.BlockSpec((3,) + blk, lambda i, me_ref: (0,) + pos(i))],
            out_specs=pl.BlockSpec(blk, lambda i, me_ref: pos(i))),
        out_shape=jax.ShapeDtypeStruct(hs, F32),
        compiler_params=_params(("parallel",)),
    )(me_idx, qf, got)


def _pair_allgather(halves, by):
    nw = len(halves)
    whole = [(2 * h.shape[0], h.shape[1]) if b == ROWS else h.shape for h, b in zip(halves, by)]

    def body(*refs):
        srcs, outs, send_sems, recv_sems = refs[:nw], refs[nw:2 * nw], refs[2 * nw], refs[2 * nw + 1]
        x, y, c = _place()
        mine = lambda w, hc: _half(outs[w], hc, ROWS) if by[w] == ROWS else outs[w]
        cps = []
        for w in range(nw):
            cp = _rcopy(srcs[w], mine(w, c), send_sems.at[w], recv_sems.at[w], (x, y, 1 - c))
            cp.start()
            cps.append(cp)
        for w in range(nw):
            got = mine(w, 1 - c)
            _rcopy(got, got, send_sems.at[w], recv_sems.at[w], (x, y, 1 - c)).wait_recv()
        for cp in cps:
            cp.wait_send()

    return pl.pallas_call(
        body, name="grad_pair_allgather", in_specs=[ANY] * nw, out_specs=[ANY] * nw,
        out_shape=[jax.ShapeDtypeStruct(s, h.dtype) for s, h in zip(whole, halves)],
        scratch_shapes=[pltpu.SemaphoreType.DMA((nw,)), pltpu.SemaphoreType.DMA((nw,))],
    )(*halves)


def _small_exchange(items, out_shapes, finish, name):
    n = len(items)
    offs, rows = [], 0
    for it in items:
        offs.append(rows)
        rows += it.shape[0]
    rows = -(-rows // SUBLANES) * SUBLANES
    width = -(-max(it.shape[1] for it in items) // LANES) * LANES
    VMEM = pl.BlockSpec(memory_space=pltpu.VMEM)

    def body(*refs):
        ins, outs = refs[:n], refs[n:n + len(out_shapes)]
        buf, send_sems, recv_sems = refs[n + len(out_shapes):]
        x, y, c = _place()
        me = 4 * x + 2 * y + c
        flip = lambda v, f: (1 - v) if f else v
        peers = [(flip(x, r >> 2 & 1), flip(y, r >> 1 & 1), flip(c, r & 1)) for r in range(1, 8)]
        buf[me] = jnp.zeros((rows, width), F32)
        for it, off, ref in zip(items, offs, ins):
            buf[me, off:off + it.shape[0], 0:it.shape[1]] = ref[...]
        cps = [_rcopy(buf.at[me], buf.at[me], send_sems.at[k], recv_sems.at[k], dev) for k, dev in enumerate(peers)]
        for cp in cps:
            cp.start()
        for k, (px, py, pc) in enumerate(peers):
            slot = buf.at[4 * px + 2 * py + pc]
            _rcopy(slot, slot, send_sems.at[k], recv_sems.at[k], (px, py, pc)).wait_recv()
        for cp in cps:
            cp.wait_send()
        finish(buf, offs, outs)

    return pl.pallas_call(
        body, name=name, in_specs=[VMEM] * n, out_specs=[VMEM] * len(out_shapes),
        out_shape=[jax.ShapeDtypeStruct(s, F32) for s in out_shapes],
        scratch_shapes=[pltpu.VMEM((8, rows, width), F32), pltpu.SemaphoreType.DMA((7,)), pltpu.SemaphoreType.DMA((7,))],
        compiler_params=pltpu.CompilerParams(vmem_limit_bytes=VMEM_LIMIT_BYTES),
    )(*items)


def _allreduce_small(items, name):
    def finish(buf, offs, outs):
        for it, off, out in zip(items, offs, outs):
            region = lambda d: buf[d, off:off + it.shape[0], 0:it.shape[1]]
            s = region(0)
            for d in range(1, 8):
                s = s + region(d)
            out[...] = s
    return _small_exchange(items, [it.shape for it in items], finish, name)


def _allgather_small_shards(items, name):
    def finish(buf, offs, outs):
        for it, off, out in zip(items, offs, outs):
            r, c = it.shape
            for j in range(4):
                out[:, j * c:(j + 1) * c] = buf[2 * j, off:off + r, 0:c]
    return _small_exchange(items, [(it.shape[0], 4 * it.shape[1]) for it in items], finish, name)


def _split_w_in(wt, D):
    pad = jnp.zeros((ZS - 3 * LOWRANK, wt.shape[1]), wt.dtype)
    big = jnp.concatenate([wt[:3 * D], wt[3 * D + 16:6 * D + 16], wt[6 * D + 16:7 * D + 16], wt[7 * D + 48:]], axis=0)
    small = jnp.concatenate([wt[3 * D:3 * D + 16], wt[7 * D + 16:7 * D + 48], pad], axis=0)
    return big, small


def _join_w_in(gb, gs, D):
    return jnp.concatenate([gb[:3 * D], gs[:16], gb[3 * D:6 * D], gb[6 * D:7 * D], gs[16:48], gb[7 * D:9 * D]], axis=0)


def kernel(x, p, g_mix, w_in, gla_w2, gla_b, gla_norm, dn_conv, dn_a_log, dn_dt_bias, dn_norm, w_out, g_mlp, w_up, w_down, g_ple, w_ple_gate, w_ple_proj, g_final, loss_target, m_g_mix, m_w_in, m_gla_w2, m_gla_b, m_gla_norm, m_dn_conv, m_dn_a_log, m_dn_dt_bias, m_dn_norm, m_w_out, m_g_mlp, m_w_up, m_w_down, m_g_ple, m_w_ple_gate, m_w_ple_proj, m_g_final, v_g_mix, v_w_in, v_gla_w2, v_gla_b, v_gla_norm, v_dn_conv, v_dn_a_log, v_dn_dt_bias, v_dn_norm, v_w_out, v_g_mlp, v_w_up, v_w_down, v_g_ple, v_w_ple_gate, v_w_ple_proj, v_g_final):
    wts = dict(zip(WEIGHTS, [g_mix, w_in, gla_w2, gla_b, gla_norm, dn_conv, dn_a_log, dn_dt_bias, dn_norm, w_out, g_mlp,
                             w_up, w_down, g_ple, w_ple_gate, w_ple_proj, g_final]))
    mom = dict(zip(WEIGHTS, [m_g_mix, m_w_in, m_gla_w2, m_gla_b, m_gla_norm, m_dn_conv, m_dn_a_log, m_dn_dt_bias, m_dn_norm,
                             m_w_out, m_g_mlp, m_w_up, m_w_down, m_g_ple, m_w_ple_gate, m_w_ple_proj, m_g_final]))
    var = dict(zip(WEIGHTS, [v_g_mix, v_w_in, v_gla_w2, v_gla_b, v_gla_norm, v_dn_conv, v_dn_a_log, v_dn_dt_bias, v_dn_norm,
                             v_w_out, v_g_mlp, v_w_up, v_w_down, v_g_ple, v_w_ple_gate, v_w_ple_proj, v_g_final]))
    Bl, S, D = x.shape
    T = Bl * S
    PLE = p.shape[-1]
    dn_d, gla_dk = D // DN_HEADS, D // (2 * GLA_HEADS)
    ix, iy, ic = _place()
    j_me = 2 * ix + iy
    as2d = lambda a: a.reshape(a.shape[-2], a.shape[-1]) if a.ndim > 1 else a.reshape(1, -1)
    c_idx, me_idx = ic.reshape(1).astype(jnp.int32), j_me.reshape(1).astype(jnp.int32)

    rows_first = lambda a: jnp.transpose(a, (2, 0, 1))
    cols_last = lambda a: jnp.transpose(a, (1, 2, 0))
    w_in_t, m_in_t, v_in_t = rows_first(w_in), rows_first(m_w_in), rows_first(v_w_in)
    n_in = w_in_t.shape[0]
    shard2d = {n: as2d(wts[n]) for n, _ in BIG[1:]}
    bf16_shards = [w_in_t.astype(BF16).reshape(n_in, D)] + [shard2d[n].astype(BF16) for n, _ in BIG[1:]]
    split = [COLS] + [ROWS] * (len(BIG) - 1)
    own_slot = lambda g, s: lax.dynamic_update_slice(g, s[None], (j_me, 0, 0))
    (w_in_ici,) = _run_stage(_gather_ici(bf16_shards[:1], split[:1]), "allgather_w_in_ici")
    (w_in_all,) = _run_stage(_gather_pass([w_in_ici], split[:1]), "allgather_w_in_pass")
    w_in_slots = own_slot(w_in_all, bf16_shards[0])
    w_big, w_small = _split_w_in(w_in_slots.reshape(4 * n_in, D), D)

    w2_full, conv_full = _allgather_small_shards([as2d(gla_w2), as2d(dn_conv)], "allgather_small_weights")
    w2pad = jnp.pad(w2_full, ((0, ZS - LOWRANK), (0, 0)))
    w2h = jnp.swapaxes(w2pad.reshape(ZS, GLA_HEADS, gla_dk), 0, 1)
    gbh = gla_b.reshape(GLA_HEADS, 1, gla_dk)
    alog_w = jnp.broadcast_to(dn_a_log.reshape(DN_HEADS, 1, 1), (DN_HEADS, 1, dn_d))
    dtb_w = jnp.broadcast_to(dn_dt_bias.reshape(DN_HEADS, 1, 1), (DN_HEADS, 1, dn_d))

    xt = x.reshape(T, D)
    tgt = loss_target.reshape(T, D)
    pt = p.reshape(T, PLE)
    seq = lambda t: t.reshape(Bl, S, t.shape[-1])
    tok = lambda t: t.reshape(T, t.shape[-1])
    h = _rmsnorm_fwd(xt, g_mix, "rms1_fwd")
    first, second = [1, 2, 5], [3, 4]
    sh, sp = (lambda idx: [bf16_shards[i] for i in idx]), (lambda idx: [split[i] for i in idx])
    z_big, *first_ici = _matmul(h, w_big, 'nt', [F32], "proj_in", stage=_gather_ici(sh(first), sp(first)))
    (z_small,) = _matmul(h, w_small, 'nt', [F32], "proj_in_narrow")
    o_gla, st_all, *first_all = _gla_fwd(seq(z_big), seq(z_small), w2h, gbh, Bl, S, D, stage=_gather_pass(first_ici, sp(first)))
    acts = [_conv_fwd(z_big, conv_full, grp, Bl, S, D) for grp in range(3)]
    o_dn, s_all, *second_ici = _dn_fwd(seq(acts[0]), seq(acts[1]), seq(acts[2]), seq(z_small), alog_w, dtb_w, Bl, S, D,
                                       stage=_gather_ici(sh(second), sp(second)))
    mixed, *second_all = _merge_fwd(tok(o_gla), tok(o_dn), z_big, gla_norm, dn_norm, D,
                                    stage=_gather_pass(second_ici, sp(second)))
    slots = {BIG[i][0]: own_slot(g, bf16_shards[i]) for i, g in zip(first + second, first_all + second_all)}
    rows_joined = lambda t: t.reshape(4 * t.shape[1], t.shape[2])
    w_out_f, w_down_f, w_pg_f = rows_joined(slots['w_out']), rows_joined(slots['w_down']), rows_joined(slots['w_ple_gate'])
    w_up_s, w_pp_s = slots['w_up'], slots['w_ple_proj']
    (x1,) = _matmul(mixed, w_out_f, 'nn', [F32], "proj_out", epilogue=lambda r, e: (e + r,), extras=(xt,), bm=512)
    h2 = _rmsnorm_fwd(x1, g_mlp, "rms2_fwd")
    u, act = _matmul(h2, w_up_s, 'nn', [F32, BF16], "mlp_up", b_slots=True,
                     epilogue=lambda r: (r, jnp.square(jnp.maximum(r, 0.0))))
    (x2,) = _matmul(act, w_down_f, 'nn', [F32], "mlp_down", epilogue=lambda r, e: (e + r,), extras=(x1,), bm=512)
    h3 = _rmsnorm_fwd(x2, g_ple, "rms3_fwd")
    (pp,) = _matmul(pt, w_pp_s, 'nn', [F32], "ple_proj", b_slots=True)
    gp, x3 = _matmul(h3, w_pg_f, 'nn', [F32, F32], "ple_gate",
                     epilogue=lambda r, e, q: (r, e + _sigmoid(r) * q), extras=(x2, pp), bm=512)
    dx3, loss_tile, d_g_final = _loss_fwd_bwd(x3, g_final.reshape(1, D), tgt, "loss")

    d_gp, d_pp = _ple_bwd(dx3, gp, pp, "ple_bwd")
    (g_pp,) = _matmul(pt, d_pp, 'tn', [F32], "ple_proj_dw", out_slots=True)
    (g_pg,) = _matmul(h3, d_gp, 'tn', [F32], "ple_gate_dw")
    (dh3,) = _matmul(d_gp, w_pg_f, 'nt', [F32], "ple_gate_dx")
    dx2, dx2b, d_g_ple = _rmsnorm_bwd_add(x2, g_ple, dh3, dx3, "rms3_bwd")
    (g_down,) = _matmul(act, dx2b, 'tn', [F32], "mlp_down_dw")
    (du,) = _matmul(dx2b, w_down_f, 'nt', [BF16], "mlp_down_dx",
                    epilogue=lambda r, e: (r * 2.0 * jnp.maximum(e, 0.0),), extras=(u,))
    (g_up,) = _matmul(h2, du, 'tn', [F32], "mlp_up_dw", out_slots=True)
    by_rows = lambda g: g.reshape(4, g.shape[0] // 4, g.shape[1])
    send_mlp = [g_up, by_rows(g_down), by_rows(g_pg), g_pp]
    dh2, *sib_mlp = _matmul(du, w_up_s, 'nt', [F32], "mlp_up_dx", b_slots=True, stage=_pair_exchange(send_mlp, split[2:]))
    dx1, dx1b, d_g_mlp = _rmsnorm_bwd_add(x1, g_mlp, dh2, dx2, "rms2_bwd")
    (g_out,) = _matmul(mixed, dx1b, 'tn', [F32], "proj_out_dw")
    dmix, sib_out = _matmul(dx1b, w_out_f, 'nt', [F32], "proj_out_dx", stage=_pair_exchange([by_rows(g_out)], split[1:2]))
    rest = [n for n, _ in BIG[1:]]
    send_rest, sib_rest = [by_rows(g_out)] + send_mlp, [sib_out] + sib_mlp
    sums_rest = [_pair_sum(s, f, c_idx, f"grad_pair_sum_{n}") for n, s, f in zip(rest, send_rest, sib_rest)]
    d_ogla, d_gg, d_odn, d_dz, d_ga, d_gb, d_gla_norm, d_dn_norm = _merge_bwd(
        tok(o_gla), tok(o_dn), z_big, gla_norm, dn_norm, dmix, D)
    d_q, d_k, d_v, dzs_gla, d_w2h, d_gbh = _gla_bwd(seq(z_big), seq(z_small), w2h, gbh, st_all, seq(d_ogla), Bl, S, D)
    d_qa, d_ka, d_va, d_zs, d_alog_w, d_dtb_w, *chips_rest = _dn_bwd(
        seq(acts[0]), seq(acts[1]), seq(acts[2]), seq(z_small), alog_w, dtb_w, s_all, seq(d_odn), dzs_gla, Bl, S, D,
        stage=_chip_scatter([b for _, b in sums_rest]))
    conv_b = [_conv_bwd(z_big, conv_full, tok(g), grp, Bl, S, D) for grp, g in enumerate([d_qa, d_ka, d_va])]
    dz_big = jnp.concatenate([tok(d_q), tok(d_k), tok(d_v), d_gg, conv_b[0][0], conv_b[1][0], conv_b[2][0], d_dz, d_ga,
                              d_gb], axis=1)
    dz_small = tok(d_zs)
    (d_w_big,) = _matmul(dz_big, h, 'tn', [F32], "proj_in_dw")
    halves_rest = [_final_sum(f, got, me_idx, f"grad_final_sum_{n}") for n, (f, _), got in zip(rest, sums_rest, chips_rest)]
    (d_w_small,) = _matmul(dz_small, h, 'tn', [F32], "proj_in_narrow_dw")
    g_in = _join_w_in(d_w_big, d_w_small, D).reshape(4, n_in, D)
    (sib_in,) = _run_stage(_pair_exchange([g_in], split[:1]), "grad_pair_exchange_w_in")
    sum_in_f32, sum_in_bf16 = _pair_sum(g_in, sib_in, c_idx, "grad_pair_sum_w_in", split[0])
    dh_a, chips_in = _matmul(dz_big, w_big, 'nn', [F32], "proj_in_dx", stage=_chip_scatter([sum_in_bf16]))
    half_in = _final_sum(sum_in_f32, chips_in, me_idx, "grad_final_sum_w_in", split[0])
    (dh,) = _matmul(dz_small, w_small, 'nn', [F32], "proj_in_narrow_dx", epilogue=lambda r, e: (e + r,), extras=(dh_a,))
    grad_x, _, d_g_mix = _rmsnorm_bwd_add(xt, g_mix, dh, dx1, "rms1_bwd")
    my_halves = [half_in] + halves_rest
    from_pair = _pair_allgather(my_halves, split)
    reduced = {n: lax.dynamic_update_slice(o, hlf, (ic * hlf.shape[0], 0))
               for (n, _), o, hlf in zip(BIG[1:], from_pair[1:], halves_rest)}
    south = ic == 0
    g_in_t = jnp.concatenate([jnp.where(south, half_in, from_pair[0]), jnp.where(south, from_pair[0], half_in)],
                             axis=1).reshape(n_in, 1, D)

    d_w2 = jnp.swapaxes(d_w2h, 0, 1).reshape(ZS, D // 2)[:LOWRANK]
    small_grads = {'g_mix': d_g_mix, 'gla_w2': d_w2, 'gla_b': d_gbh.reshape(1, D // 2), 'gla_norm': d_gla_norm,
                   'dn_a_log': d_alog_w[:, 0, 0].reshape(1, DN_HEADS), 'dn_dt_bias': d_dtb_w[:, 0, 0].reshape(1, DN_HEADS),
                   'dn_norm': d_dn_norm, 'g_mlp': d_g_mlp, 'g_ple': d_g_ple, 'g_final': d_g_final}
    names = [n for n in SMALL if n != 'dn_conv']
    total = _allreduce_small([small_grads[n] for n in names] + [cb[1] for cb in conv_b] + [loss_tile[:1]],
                             "allreduce_small_grads")
    gsmall = dict(zip(names, total[:len(names)]))
    loss = total[-1][0, 0]
    my_cols = lambda g: lax.dynamic_slice_in_dim(g, j_me * (g.shape[1] // 4), g.shape[1] // 4, axis=1)
    gsmall['gla_w2'] = my_cols(gsmall['gla_w2'])
    gsmall['dn_conv'] = my_cols(jnp.concatenate(total[len(names):len(names) + 3], axis=1))

    g_o, d_o, m_o, v_o = {}, {}, {}, {}
    d_in_t, nm_in_t, nv_in_t = _adamw(w_in_t, g_in_t, m_in_t, v_in_t, "adamw_w_in")
    g_o['w_in'], d_o['w_in'], m_o['w_in'], v_o['w_in'] = [cols_last(t) for t in (g_in_t, d_in_t, nm_in_t, nv_in_t)]
    for n, _ in BIG[1:]:
        shp = wts[n].shape
        d2, nm2, nv2 = _adamw(shard2d[n], reduced[n], as2d(mom[n]), as2d(var[n]), f"adamw_{n}")
        g_o[n], d_o[n], m_o[n], v_o[n] = reduced[n].reshape(shp), d2.reshape(shp), nm2.reshape(shp), nv2.reshape(shp)
    ds, nms, nvs = _adamw_small([as2d(wts[n]) for n in SMALL], [as2d(gsmall[n]) for n in SMALL],
                                [as2d(mom[n]) for n in SMALL], [as2d(var[n]) for n in SMALL])
    for n, dd, mm, vv in zip(SMALL, ds, nms, nvs):
        shp = wts[n].shape
        g_o[n], d_o[n], m_o[n], v_o[n] = gsmall[n].reshape(shp), dd.reshape(shp), mm.reshape(shp), vv.reshape(shp)

    return (loss, grad_x.reshape(Bl, S, D), *[g_o[n] for n in WEIGHTS], *[d_o[n] for n in WEIGHTS],
            *[m_o[n] for n in WEIGHTS], *[v_o[n] for n in WEIGHTS])
```

```python
import functools

import jax
import jax.numpy as jnp
from jax import lax
from jax.experimental import pallas as pl
from jax.experimental.pallas import tpu as pltpu

F32 = jnp.float32
BF16 = jnp.bfloat16

CHUNK = 64
GLA_HEADS = 4
DN_HEADS = 16
LOWRANK = 16
GLA_TAU = 16.0
DN_CONV = 4
EPS = 1e-6
ZS = 128
A_LANE, B_LANE = LOWRANK, LOWRANK + DN_HEADS
ADAM_LR, ADAM_B1, ADAM_B2, ADAM_EPS, ADAM_WD, ADAM_STEP = 0.001, 0.9, 0.999, 1e-08, 0.01, 10

V7X_VMEM_BYTES = 64 * 1024 * 1024
VMEM_LIMIT_BYTES = V7X_VMEM_BYTES - 8 * 1024 * 1024
LANES = 128
SUBLANES = 8
MESH = pl.DeviceIdType.MESH
DN_HEADS_PER_STEP = 8
GLA_HEADS_PER_STEP = 4

WEIGHTS = ['g_mix', 'w_in', 'gla_w2', 'gla_b', 'gla_norm', 'dn_conv', 'dn_a_log', 'dn_dt_bias', 'dn_norm', 'w_out',
           'g_mlp', 'w_up', 'w_down', 'g_ple', 'w_ple_gate', 'w_ple_proj', 'g_final']
BIG = [('w_in', 1), ('w_out', 0), ('w_up', 1), ('w_down', 0), ('w_ple_gate', 0), ('w_ple_proj', 1)]
SMALL = [n for n in WEIGHTS if n not in dict(BIG)]

_NN, _NT, _TN = 'nn', 'nt', 'tn'


def _params(sem=None):
    return pltpu.CompilerParams(dimension_semantics=sem, vmem_limit_bytes=VMEM_LIMIT_BYTES)


def _dot(a, b, form, precision=None):
    o = a.ndim - 2
    contract = {_NN: ((1 + o,), (o,)), _NT: ((1 + o,), (1 + o,)), _TN: ((o,), (o,))}[form]
    batch = ((0,), (0,)) if o else ((), ())
    return lax.dot_general(a, b, (contract, batch), precision=precision, preferred_element_type=F32)


def _make_mm(cast, precision):
    def raw(a, b, dims):
        return _dot(cast(a), cast(b), dims, precision)

    @jax.custom_vjp
    def nn(a, b):
        return raw(a, b, _NN)
    nn.defvjp(lambda a, b: (raw(a, b, _NN), (a, b)), lambda r, g: (raw(g, r[1], _NT), raw(r[0], g, _TN)))

    @jax.custom_vjp
    def nt(a, b):
        return raw(a, b, _NT)
    nt.defvjp(lambda a, b: (raw(a, b, _NT), (a, b)), lambda r, g: (raw(g, r[1], _NN), raw(g, r[0], _TN)))

    @jax.custom_vjp
    def tn(a, b):
        return raw(a, b, _TN)
    tn.defvjp(lambda a, b: (raw(a, b, _TN), (a, b)), lambda r, g: (raw(r[1], g, _NT), raw(r[0], g, _NN)))
    return nn, nt, tn


_bnn, _bnt, _btn = _make_mm(lambda t: t.astype(BF16), None)
TRI_PRECISION = lax.Precision.HIGH


def _iota2(n, axis):
    return lax.broadcasted_iota(jnp.int32, (n, n), axis)


def _lower(n, strict=False):
    return (_iota2(n, 0) > _iota2(n, 1)) if strict else (_iota2(n, 0) >= _iota2(n, 1))


def _tri_times(tri, x):
    tri = tri.astype(F32)
    if x.ndim == 3:
        tri = jnp.broadcast_to(tri, (x.shape[0],) + tri.shape)
    return _dot(tri, x, _NN, lax.Precision.HIGHEST)


@jax.custom_vjp
def _cumsum_rows(x):
    return _tri_times(_lower(x.shape[-2]), x)


def _cumsum_rows_bwd(_, g):
    n = g.shape[-2]
    return (_tri_times(_iota2(n, 0) <= _iota2(n, 1), g),)


_cumsum_rows.defvjp(lambda x: (_cumsum_rows(x), None), _cumsum_rows_bwd)


def _tri_inv_impl(a):
    n = a.shape[-1]
    eye = (_iota2(n, 0) == _iota2(n, 1)).astype(F32)
    p = eye - a
    ak = a
    k = 2
    while k < n:
        prec, cast = (TRI_PRECISION, lambda t: t) if k == 2 else (None, lambda t: t.astype(BF16))
        ak = _dot(cast(ak), cast(ak), _NN, prec)
        p = p + _dot(cast(p), cast(ak), _NN, prec)
        k *= 2
    return p


@jax.custom_vjp
def _tri_inv(a):
    return _tri_inv_impl(a)


def _tri_inv_fwd(a):
    t = _tri_inv_impl(a)
    return t, t


def _tri_inv_bwd(t, g):
    tb = t.astype(BF16)
    tg = _dot(tb, g.astype(BF16), _TN)
    return (-_dot(tg.astype(BF16), tb, _NT),)


_tri_inv.defvjp(_tri_inv_fwd, _tri_inv_bwd)


def _shift_rows(x, s, down):
    n = x.shape[0]
    r = lax.broadcasted_iota(jnp.int32, x.shape, 0)
    if down:
        return jnp.where(r >= s, pltpu.roll(x, s, 0), 0.0)
    return jnp.where(r < n - s, pltpu.roll(x, n - s, 0), 0.0)


def _make_shift(s):
    @jax.custom_vjp
    def f(x):
        return _shift_rows(x, s, True)
    f.defvjp(lambda x: (_shift_rows(x, s, True), None), lambda _, g: (_shift_rows(g, s, False),))
    return f


def _sigmoid(x):
    return jax.nn.sigmoid(x)


def _silu(x):
    return x * jax.nn.sigmoid(x)


def _softplus(x):
    return jnp.maximum(x, 0.0) + jnp.log1p(jnp.exp(-jnp.abs(x)))


def _log_sigmoid(x):
    return -_softplus(-x)


def _rms(x, g):
    return x * lax.rsqrt(jnp.mean(x * x, axis=-1, keepdims=True) + EPS) * g


def _gla_chunk(q, k, v, zs, w2, gb, st, *, scale):
    c = q.shape[-2]
    logf = _log_sigmoid(_bnn(zs, w2) + gb) * (1.0 / GLA_TAU)
    bcum = _cumsum_rows(logf)
    b_last = jnp.sum(logf, axis=-2, keepdims=True)
    q_in = (q * scale) * jnp.exp(bcum)
    k_in = k * jnp.exp(-bcum)
    a = jnp.where(_lower(c), _bnt(q_in, k_in), 0.0)
    o = _bnn(a, v) + _bnt(q_in, st)
    k_dec = k * jnp.exp(b_last - bcum)
    st_new = st * jnp.exp(b_last) + _btn(v, k_dec)
    return o, st_new


def _dn_chunk(q, k, v, aw, bw, alog, dtb, s):
    c = q.shape[-2]
    incl, strict = _lower(c), _lower(c, True)
    g_w = -jnp.exp(alog) * _softplus(aw + dtb)
    beta_w = _sigmoid(bw)
    gcum_w = _cumsum_rows(g_w)
    lane0 = lax.broadcasted_iota(jnp.int32, gcum_w.shape, gcum_w.ndim - 1) == 0
    gcol = jnp.sum(jnp.where(lane0, gcum_w, 0.0), axis=-1, keepdims=True)
    d1 = jnp.broadcast_to(gcol, gcol.shape[:-1] + (c,))
    diff = jnp.where(incl, d1 - jnp.swapaxes(d1, -1, -2), 0.0)
    decay = jnp.where(incl, jnp.exp(diff), 0.0)
    k_beta = k * beta_w
    a = jnp.where(strict, _bnt(k_beta, k) * decay, 0.0)
    t = _tri_inv(a)
    egc = jnp.exp(gcum_w)
    u = _bnn(t, v * beta_w)
    w = _bnn(t, k_beta * egc)
    attn = jnp.where(incl, _bnt(q, k) * decay, 0.0)
    q_dec = q * egc
    g_last = jnp.sum(g_w, axis=-2, keepdims=True)
    k_dec = k * jnp.exp(g_last - gcum_w)
    v_new = u - _bnn(w, s)
    o = _bnn(q_dec, s) + _bnn(attn, v_new)
    s_new = s * jnp.exp(g_last) + _btn(k_dec, v_new)
    return o, s_new


def _conv_act(x, wrows, *, l2, scale):
    taps = len(wrows)
    y = None
    for j in range(taps):
        s = taps - 1 - j
        xs = x if s == 0 else _make_shift(s)(x)
        y = wrows[j] * xs if y is None else y + wrows[j] * xs
    y = _silu(y)
    if l2:
        y = y * lax.rsqrt(jnp.sum(y * y, axis=-1, keepdims=True) + EPS) * scale
    return y


def _merge_math(og, gg, od, dz, ga, gb, gn, dn):
    nsub = len(og)
    dv = nsub * og[0].shape[1]
    ssq = jnp.sum(og[0] * og[0], axis=-1, keepdims=True)
    for s in range(1, nsub):
        ssq = ssq + jnp.sum(og[s] * og[s], axis=-1, keepdims=True)
    r = lax.rsqrt(ssq * (1.0 / dv) + EPS)
    outs = []
    for s in range(nsub):
        a = og[s] * r * gn[s] * _silu(gg[s])
        b = _rms(od[s], dn) * _silu(dz[s])
        outs.append(_sigmoid(ga[s]) * a + _sigmoid(gb[s]) * b)
    return outs


def _pick(n, target, mult):
    best = None
    for d in range(mult, min(n, target) + 1, mult):
        if n % d == 0:
            best = d
    return best if best is not None else n


class _Stage:
    def __init__(self, inputs, out_shapes, n_sems, copies, aliases=None):
        self.inputs, self.out_shapes, self.n_sems, self.copies = list(inputs), list(out_shapes), n_sems, copies
        self.aliases = aliases or {}

    @property
    def sems(self):
        return [pltpu.SemaphoreType.DMA((self.n_sems,)), pltpu.SemaphoreType.DMA((self.n_sems,))]


def _host_stage(body, stage, n_in, n_out, grid):
    ci, co = len(stage.inputs), len(stage.out_shapes)

    def wrapped(*refs):
        ins, cins = refs[:n_in], refs[n_in:n_in + ci]
        outs, couts = refs[n_in + ci:n_in + ci + n_out], refs[n_in + ci + n_out:n_in + ci + n_out + co]
        scratch, sems = refs[n_in + ci + n_out + co:-2], refs[-2:]
        ids = [pl.program_id(d) for d in range(len(grid))]
        first, last = ids[0] == 0, ids[0] == grid[0] - 1
        for i, g in zip(ids[1:], grid[1:]):
            first, last = first & (i == 0), last & (i == g - 1)

        @pl.when(first)
        def _():
            for cp in stage.copies(cins, couts, *sems):
                cp.start()

        body(*ins, *outs, *scratch)

        @pl.when(last)
        def _():
            for cp in stage.copies(cins, couts, *sems):
                cp.wait()

    return wrapped


def _call(body, name, grid, in_specs, out_specs, out_shape, scratch, semantics, args, stage=None):
    if stage is None:
        return pl.pallas_call(body, name=name, grid=grid, in_specs=list(in_specs), out_specs=list(out_specs),
                              out_shape=list(out_shape), scratch_shapes=list(scratch), compiler_params=_params(semantics))(*args)
    n_in, n_out = len(in_specs), len(out_specs)
    return pl.pallas_call(
        _host_stage(body, stage, n_in, n_out, grid), name=name, grid=grid,
        in_specs=list(in_specs) + [ANY] * len(stage.inputs),
        out_specs=list(out_specs) + [ANY] * len(stage.out_shapes), out_shape=list(out_shape) + stage.out_shapes,
        scratch_shapes=list(scratch) + stage.sems,
        input_output_aliases={n_in + i: n_out + o for i, o in stage.aliases.items()},
        compiler_params=_params(("arbitrary",) * len(grid)),
    )(*args, *stage.inputs)


def _run_stage(stage, name):
    ci = len(stage.inputs)

    def body(*refs):
        cps = stage.copies(refs[:ci], refs[ci:-2], *refs[-2:])
        for cp in cps:
            cp.start()
        for cp in cps:
            cp.wait()

    return pl.pallas_call(body, name=name, in_specs=[ANY] * ci, out_specs=[ANY] * len(stage.out_shapes),
                          out_shape=stage.out_shapes, scratch_shapes=stage.sems,
                          input_output_aliases=dict(stage.aliases))(*stage.inputs)


def _matmul(a, b, form, out_dtypes, name, epilogue=None, extras=(), bm=1024, bn=1024, bk=2048,
            b_slots=False, out_slots=False, stage=None):
    ns, c = (b.shape[0], b.shape[2]) if b_slots else (1, None)
    b2 = b.shape[1:] if b_slots else b.shape
    if form == 'nn':
        (M, K), (K2, N) = a.shape, (b2[0], b2[1] * ns)
    elif form == 'nt':
        (M, K), (N, K2) = a.shape, (b2[0], b2[1] * ns)
    else:
        (K, M), (K2, N) = a.shape, b2
    assert K == K2 and not (b_slots and form == 'tn'), (a.shape, b.shape, form)
    bm, bn, bk = _pick(M, bm, SUBLANES), _pick(N, bn, LANES), _pick(K, bk, LANES)
    if b_slots:
        bn, bk = (_pick(c, bn, LANES), bk) if form == 'nn' else (bn, _pick(c, bk, LANES))
    if out_slots:
        oc = N // 4
        bn = _pick(oc, bn, LANES)
    nk = K // bk
    a_spec = pl.BlockSpec((bk, bm), lambda i, j, k: (k, i)) if form == 'tn' else pl.BlockSpec((bm, bk), lambda i, j, k: (i, k))
    if b_slots and form == 'nn':
        per = c // bn
        b_spec = pl.BlockSpec((None, bk, bn), lambda i, j, k: (j // per, k, j % per))
    elif b_slots:
        per = c // bk
        b_spec = pl.BlockSpec((None, bn, bk), lambda i, j, k: (k // per, j, k % per))
    elif form == 'nt':
        b_spec = pl.BlockSpec((bn, bk), lambda i, j, k: (j, k))
    else:
        b_spec = pl.BlockSpec((bk, bn), lambda i, j, k: (k, j))
    o_spec = pl.BlockSpec((bm, bn), lambda i, j, k: (i, j))
    if out_slots:
        oper = oc // bn
        out_spec = pl.BlockSpec((None, bm, bn), lambda i, j, k: (j // oper, i, j % oper))
        out_shape = [jax.ShapeDtypeStruct((4, M, oc), d) for d in out_dtypes]
    else:
        out_spec = o_spec
        out_shape = [jax.ShapeDtypeStruct((M, N), d) for d in out_dtypes]
    ne, no = len(extras), len(out_dtypes)

    def finish(r, extra_refs, out_refs):
        outs = (r,) if epilogue is None else epilogue(r, *[e[...] for e in extra_refs])
        for ref, o in zip(out_refs, outs):
            ref[...] = o.astype(ref.dtype)

    def body_one(a_ref, b_ref, *rest):
        finish(_dot(a_ref[...].astype(BF16), b_ref[...].astype(BF16), form), rest[:ne], rest[ne:ne + no])

    def body_acc(a_ref, b_ref, *rest):
        extra_refs, out_refs, acc = rest[:ne], rest[ne:ne + no], rest[ne + no]
        k = pl.program_id(2)
        part = _dot(a_ref[...].astype(BF16), b_ref[...].astype(BF16), form)

        @pl.when(k == 0)
        def _():
            acc[...] = part

        @pl.when((k > 0) & (k < nk - 1))
        def _():
            acc[...] += part

        @pl.when(k == nk - 1)
        def _():
            finish(acc[...] + part, extra_refs, out_refs)

    return _call(body_one if nk == 1 else body_acc, name, (M // bm, N // bn, nk), [a_spec, b_spec] + [o_spec] * ne,
                 [out_spec] * no, out_shape, [] if nk == 1 else [pltpu.VMEM((bm, bn), F32)],
                 ("parallel", "parallel", "arbitrary"), (a, b, *extras), stage)


def _rowwise(fn, rows, consts, row_outs, acc_outs, name, bt=256, stage=None):
    T = rows[0].shape[0]
    bt = _pick(T, bt, SUBLANES)
    nr, nc, no, na = len(rows), len(consts), len(row_outs), len(acc_outs)

    def body(*refs):
        r_in, c_in = refs[:nr], refs[nr:nr + nc]
        r_out, a_out = refs[nr + nc:nr + nc + no], refs[nr + nc + no:]
        ro, ao = fn([r[...] for r in r_in], [c[...] for c in c_in])
        for ref, o in zip(r_out, ro):
            ref[...] = o.astype(ref.dtype)
        if na:
            @pl.when(pl.program_id(0) == 0)
            def _():
                for ref in a_out:
                    ref[...] = jnp.zeros_like(ref)
            for ref, o in zip(a_out, ao):
                ref[...] += o

    whole = lambda shp: pl.BlockSpec(shp, lambda i: (0,) * len(shp))
    return _call(
        body, name, (T // bt,),
        [pl.BlockSpec((bt, r.shape[1]), lambda i: (i, 0)) for r in rows] + [whole(c.shape) for c in consts],
        [pl.BlockSpec((bt, w), lambda i: (i, 0)) for w, _ in row_outs] + [whole(s) for s in acc_outs],
        [jax.ShapeDtypeStruct((T, w), d) for w, d in row_outs] + [jax.ShapeDtypeStruct(s, F32) for s in acc_outs],
        [], ("arbitrary",), (*rows, *consts), stage)


def _rmsnorm_fwd(x, g, name, stage=None):
    return _rowwise(lambda r, c: ([_rms(r[0], c[0])], []), [x], [g], [(x.shape[1], BF16)], [], name, stage=stage)


def _rmsnorm_bwd_add(x, g, dh, dres, name):
    D = x.shape[1]

    def fn(r, c):
        _, vjp = jax.vjp(_rms, r[0], c[0])
        dx, dg = vjp(r[1])
        dx = dx + r[2]
        return [dx, dx], [dg]
    return _rowwise(fn, [x, dh, dres], [g], [(D, F32), (D, BF16)], [(1, D)], name)


def _loss_fwd_bwd(x3, g, target, name):
    D = x3.shape[1]

    def fn(r, c):
        def row_loss(x, gain):
            err = _rms(x, gain) - r[1]
            return 0.5 * jnp.mean(err * err, axis=-1, keepdims=True)
        lrow, vjp = jax.vjp(row_loss, r[0], c[0])
        dx, dg = vjp(jnp.ones_like(lrow))
        tile = jnp.broadcast_to(jnp.sum(lrow, axis=0, keepdims=True), (SUBLANES, LANES))
        return [dx], [tile, dg]
    return _rowwise(fn, [x3, target], [g], [(D, F32)], [(SUBLANES, LANES), (1, D)], name)


def _ple_bwd(dx3, gp, pp, name):
    D = dx3.shape[1]

    def fn(r, c):
        s = _sigmoid(r[1])
        return [r[0] * r[2] * s * (1.0 - s), r[0] * s], []
    return _rowwise(fn, [dx3, gp, pp], [], [(D, BF16), (D, BF16)], [], name)


def _adamw_math(w, g, m, v):
    nm = ADAM_B1 * m + (1.0 - ADAM_B1) * g
    nv = ADAM_B2 * v + (1.0 - ADAM_B2) * (g * g)
    m_hat = nm / (1.0 - ADAM_B1 ** ADAM_STEP)
    v_hat = nv / (1.0 - ADAM_B2 ** ADAM_STEP)
    return -ADAM_LR * (m_hat / (jnp.sqrt(v_hat) + ADAM_EPS) + ADAM_WD * w), nm, nv


def _adamw(w, g, m, v, name):
    R, C = w.shape[0], w.shape[-1]
    lanes = -(-C // LANES) * LANES
    if w.ndim == 2:
        bt = _pick(R, max(SUBLANES, (1 << 18) // lanes // SUBLANES * SUBLANES), SUBLANES)
        spec = pl.BlockSpec((bt, C), lambda i: (i, 0))
    else:
        bt = _pick(R, max(1, (1 << 18) // lanes), 1)
        spec = pl.BlockSpec((bt, 1, C), lambda i: (i, 0, 0))

    def body(w_ref, g_ref, m_ref, v_ref, d_ref, nm_ref, nv_ref):
        d_ref[...], nm_ref[...], nv_ref[...] = _adamw_math(w_ref[...], g_ref[...], m_ref[...], v_ref[...])

    return pl.pallas_call(
        body, name=name, grid=(R // bt,), in_specs=[spec] * 4, out_specs=[spec] * 3,
        out_shape=[jax.ShapeDtypeStruct(w.shape, F32)] * 3, compiler_params=_params(("parallel",)),
    )(w, g, m, v)


def _adamw_small(ws, gs, ms, vs):
    n = len(ws)

    def body(*refs):
        for i in range(n):
            d, nm, nv = _adamw_math(refs[i][...], refs[n + i][...], refs[2 * n + i][...], refs[3 * n + i][...])
            refs[4 * n + i][...], refs[5 * n + i][...], refs[6 * n + i][...] = d, nm, nv

    VMEM = pl.BlockSpec(memory_space=pltpu.VMEM)
    shapes = [jax.ShapeDtypeStruct(w.shape, F32) for w in ws]
    outs = pl.pallas_call(body, name="adamw_small", in_specs=[VMEM] * (4 * n), out_specs=[VMEM] * (3 * n),
                          out_shape=shapes * 3)(*ws, *gs, *ms, *vs)
    return outs[:n], outs[n:2 * n], outs[2 * n:]


def _gla_fwd(z_big, z_small, w2h, gbh, Bl, S, D, stage=None):
    NC, dk, dv, HB = S // CHUNK, D // (2 * GLA_HEADS), D // GLA_HEADS, GLA_HEADS_PER_STEP
    HG = GLA_HEADS // HB
    chains = [(hh, bb) for hh in range(HB) for bb in range(Bl)]
    G = len(chains)
    fn = functools.partial(_gla_chunk, scale=dk ** -0.5)

    def body(q, k, v, z, w2, gb, o_ref, stall_ref, st):
        n, g = pl.program_id(0), pl.program_id(1)

        @pl.when(n == 0)
        def _():
            st[g] = jnp.zeros((G, dv, dk), F32)
        s0 = st[g]
        stall_ref[...] = s0.reshape(HB, Bl, dv, dk)
        qk = lambda r: jnp.stack([r[bb, :, hh * dk:(hh + 1) * dk] for hh, bb in chains])
        o, s_new = fn(qk(q), qk(k), jnp.stack([v[bb, :, hh * dv:(hh + 1) * dv] for hh, bb in chains]),
                      jnp.stack([z[bb] for _, bb in chains]), jnp.stack([w2[hh] for hh, _ in chains]),
                      jnp.stack([gb[hh] for hh, _ in chains]), s0)
        for i, (hh, bb) in enumerate(chains):
            o_ref[bb, :, hh * dv:(hh + 1) * dv] = o[i]
        st[g] = s_new

    return _call(
        body, "gla_fwd", (NC, HG),
        [pl.BlockSpec((Bl, CHUNK, HB * dk), lambda n, g: (0, n, g)),
         pl.BlockSpec((Bl, CHUNK, HB * dk), lambda n, g: (0, n, HG + g)),
         pl.BlockSpec((Bl, CHUNK, HB * dv), lambda n, g: (0, n, HG + g)),
         pl.BlockSpec((Bl, CHUNK, ZS), lambda n, g: (0, n, 0)),
         pl.BlockSpec((HB, ZS, dk), lambda n, g: (g, 0, 0)),
         pl.BlockSpec((HB, 1, dk), lambda n, g: (g, 0, 0))],
        [pl.BlockSpec((Bl, CHUNK, HB * dv), lambda n, g: (0, n, g)),
         pl.BlockSpec((HB, Bl, None, dv, dk), lambda n, g: (g, 0, n, 0, 0))],
        [jax.ShapeDtypeStruct((Bl, S, D), F32), jax.ShapeDtypeStruct((GLA_HEADS, Bl, NC, dv, dk), F32)],
        [pltpu.VMEM((HG, G, dv, dk), F32)], ("arbitrary", "arbitrary"), (z_big, z_big, z_big, z_small, w2h, gbh), stage)


def _gla_bwd(z_big, z_small, w2h, gbh, st_all, do, Bl, S, D):
    NC, dk, dv, HB = S // CHUNK, D // (2 * GLA_HEADS), D // GLA_HEADS, GLA_HEADS_PER_STEP
    HG = GLA_HEADS // HB
    chains = [(hh, bb) for hh in range(HB) for bb in range(Bl)]
    G = len(chains)
    fn = functools.partial(_gla_chunk, scale=dk ** -0.5)

    def body(q, k, v, z, w2, gb, st0, do_ref, dq_ref, dk_ref, dv_ref, dzs_ref, dw2_ref, dgb_ref, dst):
        n, g = pl.program_id(0), pl.program_id(1)

        @pl.when(n == 0)
        def _():
            dst[g] = jnp.zeros((G, dv, dk), F32)

        @pl.when((n == 0) & (g == 0))
        def _():
            dw2_ref[...] = jnp.zeros_like(dw2_ref)
            dgb_ref[...] = jnp.zeros_like(dgb_ref)

        qk = lambda r: jnp.stack([r[bb, :, hh * dk:(hh + 1) * dk] for hh, bb in chains])
        vv = lambda r: jnp.stack([r[bb, :, hh * dv:(hh + 1) * dv] for hh, bb in chains])
        _, vjp = jax.vjp(fn, qk(q), qk(k), vv(v), jnp.stack([z[bb] for _, bb in chains]),
                         jnp.stack([w2[hh] for hh, _ in chains]), jnp.stack([gb[hh] for hh, _ in chains]),
                         st0[...].reshape(G, dv, dk))
        dq, dkk, dvv, dzs, dw2, dgb, dst0 = vjp((vv(do_ref), dst[g]))
        for i, (hh, bb) in enumerate(chains):
            dq_ref[bb, :, hh * dk:(hh + 1) * dk] = dq[i].astype(dq_ref.dtype)
            dk_ref[bb, :, hh * dk:(hh + 1) * dk] = dkk[i].astype(dk_ref.dtype)
            dv_ref[bb, :, hh * dv:(hh + 1) * dv] = dvv[i].astype(dv_ref.dtype)
            dw2_ref[g * HB + hh] += dw2[i]
            dgb_ref[g * HB + hh] += dgb[i]
        for bb in range(Bl):
            tot = sum(dzs[i] for i, (_, b2) in enumerate(chains) if b2 == bb)

            @pl.when(g == 0)
            def _():
                dzs_ref[bb] = tot

            @pl.when(g > 0)
            def _():
                dzs_ref[bb] += tot
        dst[g] = dst0

    rn = lambda n: NC - 1 - n
    return pl.pallas_call(
        body, name="gla_bwd", grid=(NC, HG),
        in_specs=[pl.BlockSpec((Bl, CHUNK, HB * dk), lambda n, g: (0, rn(n), g)),
                  pl.BlockSpec((Bl, CHUNK, HB * dk), lambda n, g: (0, rn(n), HG + g)),
                  pl.BlockSpec((Bl, CHUNK, HB * dv), lambda n, g: (0, rn(n), HG + g)),
                  pl.BlockSpec((Bl, CHUNK, ZS), lambda n, g: (0, rn(n), 0)),
                  pl.BlockSpec((HB, ZS, dk), lambda n, g: (g, 0, 0)),
                  pl.BlockSpec((HB, 1, dk), lambda n, g: (g, 0, 0)),
                  pl.BlockSpec((HB, Bl, None, dv, dk), lambda n, g: (g, 0, rn(n), 0, 0)),
                  pl.BlockSpec((Bl, CHUNK, HB * dv), lambda n, g: (0, rn(n), g))],
        out_specs=[pl.BlockSpec((Bl, CHUNK, HB * dk), lambda n, g: (0, rn(n), g)),
                   pl.BlockSpec((Bl, CHUNK, HB * dk), lambda n, g: (0, rn(n), g)),
                   pl.BlockSpec((Bl, CHUNK, HB * dv), lambda n, g: (0, rn(n), g)),
                   pl.BlockSpec((Bl, CHUNK, ZS), lambda n, g: (0, rn(n), 0)),
                   pl.BlockSpec((GLA_HEADS, ZS, dk), lambda n, g: (0, 0, 0)),
                   pl.BlockSpec((GLA_HEADS, 1, dk), lambda n, g: (0, 0, 0))],
        out_shape=[jax.ShapeDtypeStruct((Bl, S, D // 2), BF16), jax.ShapeDtypeStruct((Bl, S, D // 2), BF16),
                   jax.ShapeDtypeStruct((Bl, S, D), BF16), jax.ShapeDtypeStruct((Bl, S, ZS), F32),
                   jax.ShapeDtypeStruct((GLA_HEADS, ZS, dk), F32), jax.ShapeDtypeStruct((GLA_HEADS, 1, dk), F32)],
        scratch_shapes=[pltpu.VMEM((HG, G, dv, dk), F32)],
        compiler_params=_params(("arbitrary", "arbitrary")),
    )(z_big, z_big, z_big, z_small, w2h, gbh, st_all, do)


def _conv_fwd(z_big, conv_w, grp, Bl, S, D):
    d = D // DN_HEADS
    l2, scale = grp < 2, (d ** -0.5 if grp == 0 else 1.0)
    x_blk0 = (3 * D + grp * D) // d

    def body(x_ref, w_ref, o_ref):
        wrows = [w_ref[j:j + 1, :] for j in range(DN_CONV)]
        o_ref[...] = _conv_act(x_ref[...], wrows, l2=l2, scale=scale)

    return pl.pallas_call(
        body, name=f"conv_fwd{grp}", grid=(Bl, DN_HEADS),
        in_specs=[pl.BlockSpec((S, d), lambda b, j: (b, x_blk0 + j)),
                  pl.BlockSpec((DN_CONV, d), lambda b, j: (0, grp * DN_HEADS + j))],
        out_specs=pl.BlockSpec((S, d), lambda b, j: (b, j)),
        out_shape=jax.ShapeDtypeStruct((Bl * S, D), F32),
        compiler_params=_params(("parallel", "parallel")),
    )(z_big, conv_w)


def _conv_bwd(z_big, conv_w, dact, grp, Bl, S, D):
    d = D // DN_HEADS
    l2, scale = grp < 2, (d ** -0.5 if grp == 0 else 1.0)
    x_blk0 = (3 * D + grp * D) // d

    def body(x_ref, w_ref, g_ref, dx_ref, dw_ref):
        @pl.when(pl.program_id(1) == 0)
        def _():
            dw_ref[...] = jnp.zeros_like(dw_ref)
        wrows = [w_ref[j:j + 1, :] for j in range(DN_CONV)]
        _, vjp = jax.vjp(lambda x, wr: _conv_act(x, wr, l2=l2, scale=scale), x_ref[...], wrows)
        dx, dwr = vjp(g_ref[...])
        dx_ref[...] = dx.astype(dx_ref.dtype)
        for j in range(DN_CONV):
            dw_ref[j:j + 1, :] += dwr[j]

    return pl.pallas_call(
        body, name=f"conv_bwd{grp}", grid=(DN_HEADS, Bl),
        in_specs=[pl.BlockSpec((S, d), lambda j, b: (b, x_blk0 + j)),
                  pl.BlockSpec((DN_CONV, d), lambda j, b: (0, grp * DN_HEADS + j)),
                  pl.BlockSpec((S, d), lambda j, b: (b, j))],
        out_specs=[pl.BlockSpec((S, d), lambda j, b: (b, j)), pl.BlockSpec((DN_CONV, d), lambda j, b: (0, j))],
        out_shape=[jax.ShapeDtypeStruct((Bl * S, D), BF16), jax.ShapeDtypeStruct((DN_CONV, D), F32)],
        compiler_params=_params(("arbitrary", "arbitrary")),
    )(z_big, conv_w, dact)


def _lane_column(zb, lane, width):
    pick = lax.broadcasted_iota(jnp.int32, zb.shape, 1) == lane
    return jnp.broadcast_to(jnp.sum(jnp.where(pick, zb, 0.0), axis=-1, keepdims=True), (zb.shape[0], width))


def _dn_fwd(qa, ka, va, z_small, alog, dtb, Bl, S, D, stage=None):
    NC, d, HB = S // CHUNK, D // DN_HEADS, DN_HEADS_PER_STEP
    HG = DN_HEADS // HB
    chains = [(hh, bb) for hh in range(HB) for bb in range(Bl)]
    G = len(chains)

    def body(q, k, v, z, al, dt, o_ref, sall_ref, st):
        n, g = pl.program_id(0), pl.program_id(1)

        @pl.when(n == 0)
        def _():
            st[g] = jnp.zeros((G, d, d), F32)
        tok_in = lambda r: jnp.stack([r[bb, :, hh * d:(hh + 1) * d] for hh, bb in chains])
        head_in = lambda r: jnp.stack([r[hh] for hh, _ in chains])
        gate_in = lambda lane0: jnp.stack([_lane_column(z[bb], lane0 + g * HB + hh, d) for hh, bb in chains])
        s0 = st[g]
        sall_ref[...] = s0.reshape(HB, Bl, d, d)
        o, s_new = _dn_chunk(tok_in(q), tok_in(k), tok_in(v), gate_in(A_LANE), gate_in(B_LANE), head_in(al), head_in(dt), s0)
        for i, (hh, bb) in enumerate(chains):
            o_ref[bb, :, hh * d:(hh + 1) * d] = o[i]
        st[g] = s_new

    tok = pl.BlockSpec((Bl, CHUNK, HB * d), lambda n, g: (0, n, g))
    per_head = pl.BlockSpec((HB, 1, d), lambda n, g: (g, 0, 0))
    return _call(
        body, "dn_fwd", (NC, HG),
        [tok, tok, tok, pl.BlockSpec((Bl, CHUNK, ZS), lambda n, g: (0, n, 0)), per_head, per_head],
        [tok, pl.BlockSpec((HB, Bl, None, d, d), lambda n, g: (g, 0, n, 0, 0))],
        [jax.ShapeDtypeStruct((Bl, S, D), F32), jax.ShapeDtypeStruct((DN_HEADS, Bl, NC, d, d), F32)],
        [pltpu.VMEM((HG, G, d, d), F32)], ("arbitrary", "arbitrary"), (qa, ka, va, z_small, alog, dtb), stage)


def _dn_bwd(qa, ka, va, z_small, alog, dtb, s_all, do, dzs_gla, Bl, S, D, stage=None):
    NC, d, HB = S // CHUNK, D // DN_HEADS, DN_HEADS_PER_STEP
    HG = DN_HEADS // HB
    chains = [(hh, bb) for hh in range(HB) for bb in range(Bl)]
    G = len(chains)

    def lanesum(t):
        return jnp.sum(t, axis=-1, keepdims=True)

    def body(q, k, v, z, al, dt, s0_ref, do_ref, dzg_ref, dq_ref, dk_ref, dv_ref, dzs_ref, dal_ref, ddt_ref, dst):
        n, g = pl.program_id(0), pl.program_id(1)

        @pl.when(n == 0)
        def _():
            dst[g] = jnp.zeros((G, d, d), F32)

        @pl.when((n == 0) & (g == 0))
        def _():
            dal_ref[...] = jnp.zeros_like(dal_ref)
            ddt_ref[...] = jnp.zeros_like(ddt_ref)

        tok_in = lambda r: jnp.stack([r[bb, :, hh * d:(hh + 1) * d] for hh, bb in chains])
        head_in = lambda r: jnp.stack([r[hh] for hh, _ in chains])
        gate_in = lambda lane0: jnp.stack([_lane_column(z[bb], lane0 + g * HB + hh, d) for hh, bb in chains])
        _, vjp = jax.vjp(_dn_chunk, tok_in(q), tok_in(k), tok_in(v), gate_in(A_LANE), gate_in(B_LANE), head_in(al),
                         head_in(dt), s0_ref[...].reshape(G, d, d))
        dq, dkk, dvv, da, db, dal, ddt, ds0 = vjp((tok_in(do_ref), dst[g]))
        da, db = lanesum(da), lanesum(db)
        dal = jnp.broadcast_to(lanesum(dal), (G, 1, d))
        ddt = jnp.broadcast_to(lanesum(ddt), (G, 1, d))
        lane = lax.broadcasted_iota(jnp.int32, (CHUNK, ZS), 1)
        for bb in range(Bl):
            part = jnp.zeros((CHUNK, ZS), F32)
            for i, (hh, b2) in enumerate(chains):
                if b2 == bb:
                    h = g * HB + hh
                    part = part + jnp.where(lane == A_LANE + h, da[i], 0.0) + jnp.where(lane == B_LANE + h, db[i], 0.0)

            @pl.when(g == 0)
            def _():
                dzs_ref[bb] = jnp.where(lane < LOWRANK, dzg_ref[bb], 0.0) + part

            @pl.when(g > 0)
            def _():
                dzs_ref[bb] += part
        for i, (hh, bb) in enumerate(chains):
            cols = slice(hh * d, (hh + 1) * d)
            dq_ref[bb, :, cols] = dq[i]
            dk_ref[bb, :, cols] = dkk[i]
            dv_ref[bb, :, cols] = dvv[i]
            dal_ref[g * HB + hh] += dal[i]
            ddt_ref[g * HB + hh] += ddt[i]
        dst[g] = ds0

    rn = lambda n: NC - 1 - n
    tok = pl.BlockSpec((Bl, CHUNK, HB * d), lambda n, g: (0, rn(n), g))
    zsb = pl.BlockSpec((Bl, CHUNK, ZS), lambda n, g: (0, rn(n), 0))
    per_head = pl.BlockSpec((HB, 1, d), lambda n, g: (g, 0, 0))
    all_heads = pl.BlockSpec((DN_HEADS, 1, d), lambda n, g: (0, 0, 0))
    tok_shape = jax.ShapeDtypeStruct((Bl, S, D), F32)
    head_shape = jax.ShapeDtypeStruct((DN_HEADS, 1, d), F32)
    return _call(
        body, "dn_bwd", (NC, HG),
        [tok, tok, tok, zsb, per_head, per_head,
         pl.BlockSpec((HB, Bl, None, d, d), lambda n, g: (g, 0, rn(n), 0, 0)), tok, zsb],
        [tok, tok, tok, zsb, all_heads, all_heads],
        [tok_shape, tok_shape, tok_shape, jax.ShapeDtypeStruct((Bl, S, ZS), F32), head_shape, head_shape],
        [pltpu.VMEM((HG, G, d, d), F32)], ("arbitrary", "arbitrary"),
        (qa, ka, va, z_small, alog, dtb, s_all, do, dzs_gla), stage)


def _merge_specs(D, bt):
    dv, w = D // GLA_HEADS, D // DN_HEADS
    col = lambda off: pl.BlockSpec((bt, dv), lambda i, h: (i, off // dv + h))
    return dv, w, col


def _merge_load(refs, nsub, w):
    return [[r[:, s * w:(s + 1) * w] for s in range(nsub)] for r in refs]


def _merge_fwd(o_gla, o_dn, z_big, gla_norm, dn_norm, D, bt=256, stage=None):
    T = o_gla.shape[0]
    bt = _pick(T, bt, SUBLANES)
    dv, w, col = _merge_specs(D, bt)
    nsub = dv // w

    def body(og, gg, od, dz, ga, gb, gn, dn, out):
        ogl, ggl, odl, dzl, gal, gbl = _merge_load([og, gg, od, dz, ga, gb], nsub, w)
        gnl = [gn[:, s * w:(s + 1) * w] for s in range(nsub)]
        outs = _merge_math(ogl, ggl, odl, dzl, gal, gbl, gnl, dn[...])
        for s in range(nsub):
            out[:, s * w:(s + 1) * w] = outs[s].astype(out.dtype)

    return _call(
        body, "merge_fwd", (T // bt, GLA_HEADS),
        [col(0), col(2 * D), col(0), col(6 * D), col(7 * D), col(8 * D),
         pl.BlockSpec((1, dv), lambda i, h: (0, 0)), pl.BlockSpec((1, w), lambda i, h: (0, 0))],
        [col(0)], [jax.ShapeDtypeStruct((T, D), BF16)], [], ("parallel", "parallel"),
        (o_gla, z_big, o_dn, z_big, z_big, z_big, gla_norm, dn_norm), stage)


def _merge_bwd(o_gla, o_dn, z_big, gla_norm, dn_norm, dmix, D, bt=256):
    T = o_gla.shape[0]
    bt = _pick(T, bt, SUBLANES)
    dv, w, col = _merge_specs(D, bt)
    nsub = dv // w

    def body(og, gg, od, dz, ga, gb, gn, dn, dm, dog, dgg, dod, ddz, dga, dgb, dgn, ddn):
        @pl.when((pl.program_id(0) == 0) & (pl.program_id(1) == 0))
        def _():
            dgn[...] = jnp.zeros_like(dgn)
            ddn[...] = jnp.zeros_like(ddn)

        ogl, ggl, odl, dzl, gal, gbl, dml = _merge_load([og, gg, od, dz, ga, gb, dm], nsub, w)
        gnl = [gn[:, s * w:(s + 1) * w] for s in range(nsub)]
        _, vjp = jax.vjp(_merge_math, ogl, ggl, odl, dzl, gal, gbl, gnl, dn[...])
        g_og, g_gg, g_od, g_dz, g_ga, g_gb, g_gn, g_dn = vjp(dml)
        for s in range(nsub):
            sl = slice(s * w, (s + 1) * w)
            dog[:, sl] = g_og[s]
            dgg[:, sl] = g_gg[s].astype(dgg.dtype)
            dod[:, sl] = g_od[s]
            ddz[:, sl] = g_dz[s].astype(ddz.dtype)
            dga[:, sl] = g_ga[s].astype(dga.dtype)
            dgb[:, sl] = g_gb[s].astype(dgb.dtype)
            dgn[:, sl] += g_gn[s]
        ddn[...] += g_dn

    f32s, bf16s = jax.ShapeDtypeStruct((T, D), F32), jax.ShapeDtypeStruct((T, D), BF16)
    return pl.pallas_call(
        body, name="merge_bwd", grid=(T // bt, GLA_HEADS),
        in_specs=[col(0), col(2 * D), col(0), col(6 * D), col(7 * D), col(8 * D),
                  pl.BlockSpec((1, dv), lambda i, h: (0, 0)), pl.BlockSpec((1, w), lambda i, h: (0, 0)), col(0)],
        out_specs=[col(0)] * 6 + [pl.BlockSpec((1, dv), lambda i, h: (0, 0)), pl.BlockSpec((1, w), lambda i, h: (0, 0))],
        out_shape=[f32s, bf16s, f32s, bf16s, bf16s, bf16s,
                   jax.ShapeDtypeStruct((1, dv), F32), jax.ShapeDtypeStruct((1, w), F32)],
        compiler_params=_params(("arbitrary", "arbitrary")),
    )(o_gla, z_big, o_dn, z_big, z_big, z_big, gla_norm, dn_norm, dmix)


def _place():
    return lax.axis_index("x"), lax.axis_index("y"), lax.axis_index("c")


def _other_chips(x, y):
    return [(1 - x, y), (x, 1 - y), (1 - x, 1 - y)]


def _rcopy(src, dst, send_sem, recv_sem, dev):
    return pltpu.make_async_remote_copy(src_ref=src, dst_ref=dst, send_sem=send_sem, recv_sem=recv_sem,
                                        device_id=dev, device_id_type=MESH)


ANY = pl.BlockSpec(memory_space=pl.ANY)


ROWS, COLS = 'rows', 'cols'


def _half(ref, hc, by, lead=()):
    shape = ref.shape[len(lead):]
    if by == ROWS:
        rh = shape[0] // 2
        idx = (pl.ds(pl.multiple_of(hc * rh, 16), rh),) + (slice(None),) * (len(shape) - 1)
    else:
        ch = shape[-1] // 2
        idx = (slice(None),) * (len(shape) - 1) + (pl.ds(pl.multiple_of(hc * ch, LANES), ch),)
    return ref.at[(*lead, *idx)]


def _half_shape(shape, by):
    return (shape[0] // 2,) + tuple(shape[1:]) if by == ROWS else tuple(shape[:-1]) + (shape[-1] // 2,)


def _gather_ici(shards, by):
    nw = len(shards)

    def copies(srcs, outs, send_sems, recv_sems):
        x, y, c = _place()
        return [_rcopy(_half(srcs[w], c, by[w]), _half(outs[w], c, by[w], (2 * x + y,)),
                       send_sems.at[3 * w + k], recv_sems.at[3 * w + k], (px, py, c))
                for w in range(nw) for k, (px, py) in enumerate(_other_chips(x, y))]

    return _Stage(shards, [jax.ShapeDtypeStruct((4,) + s.shape, s.dtype) for s in shards], 3 * nw, copies)


def _gather_pass(gathered, by):
    nw = len(gathered)

    def copies(srcs, outs, send_sems, recv_sems):
        x, y, c = _place()
        cps = []
        for w in range(nw):
            for k, (px, py) in enumerate(_other_chips(x, y)):
                slot = (2 * px + py,)
                cps.append(_rcopy(_half(srcs[w], c, by[w], slot), _half(outs[w], c, by[w], slot),
                                  send_sems.at[3 * w + k], recv_sems.at[3 * w + k], (x, y, 1 - c)))
        return cps

    return _Stage(gathered, [jax.ShapeDtypeStruct(g.shape, g.dtype) for g in gathered], 3 * nw, copies,
                  aliases={w: w for w in range(nw)})


def _pair_exchange(ps, by):
    nw = len(ps)

    def copies(srcs, outs, send_sems, recv_sems):
        x, y, c = _place()
        return [_rcopy(_half(srcs[w], 1 - c, by[w], (slice(None),)), outs[w], send_sems.at[w], recv_sems.at[w], (x, y, 1 - c))
                for w in range(nw)]

    return _Stage(ps, [jax.ShapeDtypeStruct((4,) + _half_shape(p.shape[1:], b), p.dtype) for p, b in zip(ps, by)], nw, copies)


def _sum_blocks(half_shape, by):
    rh, ch = half_shape
    if by == ROWS:
        lanes = -(-ch // LANES) * LANES
        bt = _pick(rh, max(16, (3 << 18) // lanes // 16 * 16), 16)
        return (bt, ch), rh // bt, lambda i: (i, 0)
    bc = _pick(ch, max(LANES, (5 << 18) // rh // LANES * LANES), LANES)
    return (rh, bc), ch // bc, lambda i: (0, i)


def _pair_sum(p, got, c_idx, name, by=ROWS):
    hs = got.shape[1:]
    blk, nb, pos = _sum_blocks(hs, by)

    def body(c_ref, a, b, of, ob):
        s = a[...] + b[...]
        of[...] = s
        ob[...] = s.astype(BF16)

    def mine(j, i, c_ref):
        r, cc = pos(c_ref[0] * nb + i)
        return (j, r, cc)

    spec = pl.BlockSpec((None,) + blk, lambda j, i, c_ref: (j,) + pos(i))
    return pl.pallas_call(
        body, name=name,
        grid_spec=pltpu.PrefetchScalarGridSpec(
            num_scalar_prefetch=1, grid=(4, nb),
            in_specs=[pl.BlockSpec((None,) + blk, mine), spec], out_specs=[spec, spec]),
        out_shape=[jax.ShapeDtypeStruct((4,) + hs, F32), jax.ShapeDtypeStruct((4,) + hs, BF16)],
        compiler_params=_params(("parallel", "parallel")),
    )(c_idx, p, got)


def _chip_scatter(qbs):
    nw = len(qbs)

    def copies(srcs, outs, send_sems, recv_sems):
        x, y, c = _place()
        return [_rcopy(srcs[w].at[2 * px + py], outs[w].at[k], send_sems.at[3 * w + k], recv_sems.at[3 * w + k], (px, py, c))
                for w in range(nw) for k, (px, py) in enumerate(_other_chips(x, y))]

    return _Stage(qbs, [jax.ShapeDtypeStruct((3,) + q.shape[1:], q.dtype) for q in qbs], 3 * nw, copies)


def _final_sum(qf, got, me_idx, name, by=ROWS):
    hs = qf.shape[1:]
    blk, nb, pos = _sum_blocks(hs, by)

    def body(me_ref, a, b, o):
        o[...] = ((a[...] + b[0].astype(F32)) + b[1].astype(F32)) + b[2].astype(F32)

    return pl.pallas_call(
        body, name=name,
        grid_spec=pltpu.PrefetchScalarGridSpec(
            num_scalar_prefetch=1, grid=(nb,),
            in_specs=[pl.BlockSpec((None,) + blk, lambda i, me_ref: (me_ref[0],) + pos(i)),
                      pl.BlockSpec((3,) + blk, lambda i, me_ref: (0,) + pos(i))],
            out_specs=pl.BlockSpec(blk, lambda i, me_ref: pos(i))),
        out_shape=jax.ShapeDtypeStruct(hs, F32),
        compiler_params=_params(("parallel",)),
    )(me_idx, qf, got)


def _pair_allgather(halves, by):
    nw = len(halves)
    whole = [(2 * h.shape[0], h.shape[1]) if b == ROWS else h.shape for h, b in zip(halves, by)]

    def body(*refs):
        srcs, outs, send_sems, recv_sems = refs[:nw], refs[nw:2 * nw], refs[2 * nw], refs[2 * nw + 1]
        x, y, c = _place()
        mine = lambda w, hc: _half(outs[w], hc, ROWS) if by[w] == ROWS else outs[w]
        cps = []
        for w in range(nw):
            cp = _rcopy(srcs[w], mine(w, c), send_sems.at[w], recv_sems.at[w], (x, y, 1 - c))
            cp.start()
            cps.append(cp)
        for w in range(nw):
            got = mine(w, 1 - c)
            _rcopy(got, got, send_sems.at[w], recv_sems.at[w], (x, y, 1 - c)).wait_recv()
        for cp in cps:
            cp.wait_send()

    return pl.pallas_call(
        body, name="grad_pair_allgather", in_specs=[ANY] * nw, out_specs=[ANY] * nw,
        out_shape=[jax.ShapeDtypeStruct(s, h.dtype) for s, h in zip(whole, halves)],
        scratch_shapes=[pltpu.SemaphoreType.DMA((nw,)), pltpu.SemaphoreType.DMA((nw,))],
    )(*halves)


def _small_exchange(items, out_shapes, finish, name):
    n = len(items)
    offs, rows = [], 0
    for it in items:
        offs.append(rows)
        rows += it.shape[0]
    rows = -(-rows // SUBLANES) * SUBLANES
    width = -(-max(it.shape[1] for it in items) // LANES) * LANES
    VMEM = pl.BlockSpec(memory_space=pltpu.VMEM)

    def body(*refs):
        ins, outs = refs[:n], refs[n:n + len(out_shapes)]
        buf, send_sems, recv_sems = refs[n + len(out_shapes):]
        x, y, c = _place()
        me = 4 * x + 2 * y + c
        flip = lambda v, f: (1 - v) if f else v
        peers = [(flip(x, r >> 2 & 1), flip(y, r >> 1 & 1), flip(c, r & 1)) for r in range(1, 8)]
        buf[me] = jnp.zeros((rows, width), F32)
        for it, off, ref in zip(items, offs, ins):
            buf[me, off:off + it.shape[0], 0:it.shape[1]] = ref[...]
        cps = [_rcopy(buf.at[me], buf.at[me], send_sems.at[k], recv_sems.at[k], dev) for k, dev in enumerate(peers)]
        for cp in cps:
            cp.start()
        for k, (px, py, pc) in enumerate(peers):
            slot = buf.at[4 * px + 2 * py + pc]
            _rcopy(slot, slot, send_sems.at[k], recv_sems.at[k], (px, py, pc)).wait_recv()
        for cp in cps:
            cp.wait_send()
        finish(buf, offs, outs)

    return pl.pallas_call(
        body, name=name, in_specs=[VMEM] * n, out_specs=[VMEM] * len(out_shapes),
        out_shape=[jax.ShapeDtypeStruct(s, F32) for s in out_shapes],
        scratch_shapes=[pltpu.VMEM((8, rows, width), F32), pltpu.SemaphoreType.DMA((7,)), pltpu.SemaphoreType.DMA((7,))],
        compiler_params=pltpu.CompilerParams(vmem_limit_bytes=VMEM_LIMIT_BYTES),
    )(*items)


def _allreduce_small(items, name):
    def finish(buf, offs, outs):
        for it, off, out in zip(items, offs, outs):
            region = lambda d: buf[d, off:off + it.shape[0], 0:it.shape[1]]
            s = region(0)
            for d in range(1, 8):
                s = s + region(d)
            out[...] = s
    return _small_exchange(items, [it.shape for it in items], finish, name)


def _allgather_small_shards(items, name):
    def finish(buf, offs, outs):
        for it, off, out in zip(items, offs, outs):
            r, c = it.shape
            for j in range(4):
                out[:, j * c:(j + 1) * c] = buf[2 * j, off:off + r, 0:c]
    return _small_exchange(items, [(it.shape[0], 4 * it.shape[1]) for it in items], finish, name)


def _split_w_in(wt, D):
    pad = jnp.zeros((ZS - 3 * LOWRANK, wt.shape[1]), wt.dtype)
    big = jnp.concatenate([wt[:3 * D], wt[3 * D + 16:6 * D + 16], wt[6 * D + 16:7 * D + 16], wt[7 * D + 48:]], axis=0)
    small = jnp.concatenate([wt[3 * D:3 * D + 16], wt[7 * D + 16:7 * D + 48], pad], axis=0)
    return big, small


def _join_w_in(gb, gs, D):
    return jnp.concatenate([gb[:3 * D], gs[:16], gb[3 * D:6 * D], gb[6 * D:7 * D], gs[16:48], gb[7 * D:9 * D]], axis=0)


def kernel(x, p, g_mix, w_in, gla_w2, gla_b, gla_norm, dn_conv, dn_a_log, dn_dt_bias, dn_norm, w_out, g_mlp, w_up, w_down, g_ple, w_ple_gate, w_ple_proj, g_final, loss_target, m_g_mix, m_w_in, m_gla_w2, m_gla_b, m_gla_norm, m_dn_conv, m_dn_a_log, m_dn_dt_bias, m_dn_norm, m_w_out, m_g_mlp, m_w_up, m_w_down, m_g_ple, m_w_ple_gate, m_w_ple_proj, m_g_final, v_g_mix, v_w_in, v_gla_w2, v_gla_b, v_gla_norm, v_dn_conv, v_dn_a_log, v_dn_dt_bias, v_dn_norm, v_w_out, v_g_mlp, v_w_up, v_w_down, v_g_ple, v_w_ple_gate, v_w_ple_proj, v_g_final):
    wts = dict(zip(WEIGHTS, [g_mix, w_in, gla_w2, gla_b, gla_norm, dn_conv, dn_a_log, dn_dt_bias, dn_norm, w_out, g_mlp,
                             w_up, w_down, g_ple, w_ple_gate, w_ple_proj, g_final]))
    mom = dict(zip(WEIGHTS, [m_g_mix, m_w_in, m_gla_w2, m_gla_b, m_gla_norm, m_dn_conv, m_dn_a_log, m_dn_dt_bias, m_dn_norm,
                             m_w_out, m_g_mlp, m_w_up, m_w_down, m_g_ple, m_w_ple_gate, m_w_ple_proj, m_g_final]))
    var = dict(zip(WEIGHTS, [v_g_mix, v_w_in, v_gla_w2, v_gla_b, v_gla_norm, v_dn_conv, v_dn_a_log, v_dn_dt_bias, v_dn_norm,
                             v_w_out, v_g_mlp, v_w_up, v_w_down, v_g_ple, v_w_ple_gate, v_w_ple_proj, v_g_final]))
    Bl, S, D = x.shape
    T = Bl * S
    PLE = p.shape[-1]
    dn_d, gla_dk = D // DN_HEADS, D // (2 * GLA_HEADS)
    ix, iy, ic = _place()
    j_me = 2 * ix + iy
    as2d = lambda a: a.reshape(a.shape[-2], a.shape[-1]) if a.ndim > 1 else a.reshape(1, -1)
    c_idx, me_idx = ic.reshape(1).astype(jnp.int32), j_me.reshape(1).astype(jnp.int32)

    rows_first = lambda a: jnp.transpose(a, (2, 0, 1))
    cols_last = lambda a: jnp.transpose(a, (1, 2, 0))
    w_in_t, m_in_t, v_in_t = rows_first(w_in), rows_first(m_w_in), rows_first(v_w_in)
    n_in = w_in_t.shape[0]
    shard2d = {n: as2d(wts[n]) for n, _ in BIG[1:]}
    bf16_shards = [w_in_t.astype(BF16).reshape(n_in, D)] + [shard2d[n].astype(BF16) for n, _ in BIG[1:]]
    split = [COLS] + [ROWS] * (len(BIG) - 1)
    own_slot = lambda g, s: lax.dynamic_update_slice(g, s[None], (j_me, 0, 0))
    xt = x.reshape(T, D)
    (w_in_ici,) = _run_stage(_gather_ici(bf16_shards[:1], split[:1]), "allgather_w_in_ici")
    h, w_in_all = _rmsnorm_fwd(xt, g_mix, "rms1_fwd", stage=_gather_pass([w_in_ici], split[:1]))
    w_in_slots = own_slot(w_in_all, bf16_shards[0])
    w_big, w_small = _split_w_in(w_in_slots.reshape(4 * n_in, D), D)

    w2_full, conv_full = _allgather_small_shards([as2d(gla_w2), as2d(dn_conv)], "allgather_small_weights")
    w2pad = jnp.pad(w2_full, ((0, ZS - LOWRANK), (0, 0)))
    w2h = jnp.swapaxes(w2pad.reshape(ZS, GLA_HEADS, gla_dk), 0, 1)
    gbh = gla_b.reshape(GLA_HEADS, 1, gla_dk)
    alog_w = jnp.broadcast_to(dn_a_log.reshape(DN_HEADS, 1, 1), (DN_HEADS, 1, dn_d))
    dtb_w = jnp.broadcast_to(dn_dt_bias.reshape(DN_HEADS, 1, 1), (DN_HEADS, 1, dn_d))

    tgt = loss_target.reshape(T, D)
    pt = p.reshape(T, PLE)
    seq = lambda t: t.reshape(Bl, S, t.shape[-1])
    tok = lambda t: t.reshape(T, t.shape[-1])
    first, second = [1, 2, 5], [3, 4]
    sh, sp = (lambda idx: [bf16_shards[i] for i in idx]), (lambda idx: [split[i] for i in idx])
    z_big, *first_ici = _matmul(h, w_big, 'nt', [F32], "proj_in", stage=_gather_ici(sh(first), sp(first)))
    (z_small,) = _matmul(h, w_small, 'nt', [F32], "proj_in_narrow")
    o_gla, st_all, *first_all = _gla_fwd(seq(z_big), seq(z_small), w2h, gbh, Bl, S, D, stage=_gather_pass(first_ici, sp(first)))
    acts = [_conv_fwd(z_big, conv_full, grp, Bl, S, D) for grp in range(3)]
    o_dn, s_all, *second_ici = _dn_fwd(seq(acts[0]), seq(acts[1]), seq(acts[2]), seq(z_small), alog_w, dtb_w, Bl, S, D,
                                       stage=_gather_ici(sh(second), sp(second)))
    mixed, *second_all = _merge_fwd(tok(o_gla), tok(o_dn), z_big, gla_norm, dn_norm, D,
                                    stage=_gather_pass(second_ici, sp(second)))
    slots = {BIG[i][0]: own_slot(g, bf16_shards[i]) for i, g in zip(first + second, first_all + second_all)}
    rows_joined = lambda t: t.reshape(4 * t.shape[1], t.shape[2])
    w_out_f, w_down_f, w_pg_f = rows_joined(slots['w_out']), rows_joined(slots['w_down']), rows_joined(slots['w_ple_gate'])
    w_up_s, w_pp_s = slots['w_up'], slots['w_ple_proj']
    (x1,) = _matmul(mixed, w_out_f, 'nn', [F32], "proj_out", epilogue=lambda r, e: (e + r,), extras=(xt,), bm=512)
    (h2,) = _rmsnorm_fwd(x1, g_mlp, "rms2_fwd")
    u, act = _matmul(h2, w_up_s, 'nn', [F32, BF16], "mlp_up", b_slots=True,
                     epilogue=lambda r: (r, jnp.square(jnp.maximum(r, 0.0))))
    (x2,) = _matmul(act, w_down_f, 'nn', [F32], "mlp_down", epilogue=lambda r, e: (e + r,), extras=(x1,), bm=512)
    (h3,) = _rmsnorm_fwd(x2, g_ple, "rms3_fwd")
    (pp,) = _matmul(pt, w_pp_s, 'nn', [F32], "ple_proj", b_slots=True)
    gp, x3 = _matmul(h3, w_pg_f, 'nn', [F32, F32], "ple_gate",
                     epilogue=lambda r, e, q: (r, e + _sigmoid(r) * q), extras=(x2, pp), bm=512)
    dx3, loss_tile, d_g_final = _loss_fwd_bwd(x3, g_final.reshape(1, D), tgt, "loss")

    d_gp, d_pp = _ple_bwd(dx3, gp, pp, "ple_bwd")
    (g_pp,) = _matmul(pt, d_pp, 'tn', [F32], "ple_proj_dw", out_slots=True)
    (g_pg,) = _matmul(h3, d_gp, 'tn', [F32], "ple_gate_dw")
    (dh3,) = _matmul(d_gp, w_pg_f, 'nt', [F32], "ple_gate_dx")
    dx2, dx2b, d_g_ple = _rmsnorm_bwd_add(x2, g_ple, dh3, dx3, "rms3_bwd")
    (g_down,) = _matmul(act, dx2b, 'tn', [F32], "mlp_down_dw")
    (du,) = _matmul(dx2b, w_down_f, 'nt', [BF16], "mlp_down_dx",
                    epilogue=lambda r, e: (r * 2.0 * jnp.maximum(e, 0.0),), extras=(u,))
    (g_up,) = _matmul(h2, du, 'tn', [F32], "mlp_up_dw", out_slots=True)
    by_rows = lambda g: g.reshape(4, g.shape[0] // 4, g.shape[1])
    send_mlp = [g_up, by_rows(g_down), by_rows(g_pg), g_pp]
    dh2, *sib_mlp = _matmul(du, w_up_s, 'nt', [F32], "mlp_up_dx", b_slots=True, stage=_pair_exchange(send_mlp, split[2:]))
    dx1, dx1b, d_g_mlp = _rmsnorm_bwd_add(x1, g_mlp, dh2, dx2, "rms2_bwd")
    (g_out,) = _matmul(mixed, dx1b, 'tn', [F32], "proj_out_dw")
    dmix, sib_out = _matmul(dx1b, w_out_f, 'nt', [F32], "proj_out_dx", stage=_pair_exchange([by_rows(g_out)], split[1:2]))
    rest = [n for n, _ in BIG[1:]]
    send_rest, sib_rest = [by_rows(g_out)] + send_mlp, [sib_out] + sib_mlp
    sums_rest = [_pair_sum(s, f, c_idx, f"grad_pair_sum_{n}") for n, s, f in zip(rest, send_rest, sib_rest)]
    d_ogla, d_gg, d_odn, d_dz, d_ga, d_gb, d_gla_norm, d_dn_norm = _merge_bwd(
        tok(o_gla), tok(o_dn), z_big, gla_norm, dn_norm, dmix, D)
    d_q, d_k, d_v, dzs_gla, d_w2h, d_gbh = _gla_bwd(seq(z_big), seq(z_small), w2h, gbh, st_all, seq(d_ogla), Bl, S, D)
    d_qa, d_ka, d_va, d_zs, d_alog_w, d_dtb_w, *chips_rest = _dn_bwd(
        seq(acts[0]), seq(acts[1]), seq(acts[2]), seq(z_small), alog_w, dtb_w, s_all, seq(d_odn), dzs_gla, Bl, S, D,
        stage=_chip_scatter([b for _, b in sums_rest]))
    conv_b = [_conv_bwd(z_big, conv_full, tok(g), grp, Bl, S, D) for grp, g in enumerate([d_qa, d_ka, d_va])]
    dz_big = jnp.concatenate([tok(d_q), tok(d_k), tok(d_v), d_gg, conv_b[0][0], conv_b[1][0], conv_b[2][0], d_dz, d_ga,
                              d_gb], axis=1)
    dz_small = tok(d_zs)
    (d_w_big,) = _matmul(dz_big, h, 'tn', [F32], "proj_in_dw")
    halves_rest = [_final_sum(f, got, me_idx, f"grad_final_sum_{n}") for n, (f, _), got in zip(rest, sums_rest, chips_rest)]
    (d_w_small,) = _matmul(dz_small, h, 'tn', [F32], "proj_in_narrow_dw")
    g_in = _join_w_in(d_w_big, d_w_small, D).reshape(4, n_in, D)
    (sib_in,) = _run_stage(_pair_exchange([g_in], split[:1]), "grad_pair_exchange_w_in")
    sum_in_f32, sum_in_bf16 = _pair_sum(g_in, sib_in, c_idx, "grad_pair_sum_w_in", split[0])
    dh_a, chips_in = _matmul(dz_big, w_big, 'nn', [F32], "proj_in_dx", stage=_chip_scatter([sum_in_bf16]))
    half_in = _final_sum(sum_in_f32, chips_in, me_idx, "grad_final_sum_w_in", split[0])
    (dh,) = _matmul(dz_small, w_small, 'nn', [F32], "proj_in_narrow_dx", epilogue=lambda r, e: (e + r,), extras=(dh_a,))
    grad_x, _, d_g_mix = _rmsnorm_bwd_add(xt, g_mix, dh, dx1, "rms1_bwd")
    my_halves = [half_in] + halves_rest
    from_pair = _pair_allgather(my_halves, split)
    reduced = {n: lax.dynamic_update_slice(o, hlf, (ic * hlf.shape[0], 0))
               for (n, _), o, hlf in zip(BIG[1:], from_pair[1:], halves_rest)}
    south = ic == 0
    g_in_t = jnp.concatenate([jnp.where(south, half_in, from_pair[0]), jnp.where(south, from_pair[0], half_in)],
                             axis=1).reshape(n_in, 1, D)

    d_w2 = jnp.swapaxes(d_w2h, 0, 1).reshape(ZS, D // 2)[:LOWRANK]
    small_grads = {'g_mix': d_g_mix, 'gla_w2': d_w2, 'gla_b': d_gbh.reshape(1, D // 2), 'gla_norm': d_gla_norm,
                   'dn_a_log': d_alog_w[:, 0, 0].reshape(1, DN_HEADS), 'dn_dt_bias': d_dtb_w[:, 0, 0].reshape(1, DN_HEADS),
                   'dn_norm': d_dn_norm, 'g_mlp': d_g_mlp, 'g_ple': d_g_ple, 'g_final': d_g_final}
    names = [n for n in SMALL if n != 'dn_conv']
    total = _allreduce_small([small_grads[n] for n in names] + [cb[1] for cb in conv_b] + [loss_tile[:1]],
                             "allreduce_small_grads")
    gsmall = dict(zip(names, total[:len(names)]))
    loss = total[-1][0, 0]
    my_cols = lambda g: lax.dynamic_slice_in_dim(g, j_me * (g.shape[1] // 4), g.shape[1] // 4, axis=1)
    gsmall['gla_w2'] = my_cols(gsmall['gla_w2'])
    gsmall['dn_conv'] = my_cols(jnp.concatenate(total[len(names):len(names) + 3], axis=1))

    g_o, d_o, m_o, v_o = {}, {}, {}, {}
    d_in_t, nm_in_t, nv_in_t = _adamw(w_in_t, g_in_t, m_in_t, v_in_t, "adamw_w_in")
    g_o['w_in'], d_o['w_in'], m_o['w_in'], v_o['w_in'] = [cols_last(t) for t in (g_in_t, d_in_t, nm_in_t, nv_in_t)]
    for n, _ in BIG[1:]:
        shp = wts[n].shape
        d2, nm2, nv2 = _adamw(shard2d[n], reduced[n], as2d(mom[n]), as2d(var[n]), f"adamw_{n}")
        g_o[n], d_o[n], m_o[n], v_o[n] = reduced[n].reshape(shp), d2.reshape(shp), nm2.reshape(shp), nv2.reshape(shp)
    ds, nms, nvs = _adamw_small([as2d(wts[n]) for n in SMALL], [as2d(gsmall[n]) for n in SMALL],
                                [as2d(mom[n]) for n in SMALL], [as2d(var[n]) for n in SMALL])
    for n, dd, mm, vv in zip(SMALL, ds, nms, nvs):
        shp = wts[n].shape
        g_o[n], d_o[n], m_o[n], v_o[n] = gsmall[n].reshape(shp), dd.reshape(shp), mm.reshape(shp), vv.reshape(shp)

    return (loss, grad_x.reshape(Bl, S, D), *[g_o[n] for n in WEIGHTS], *[d_o[n] for n in WEIGHTS],
            *[m_o[n] for n in WEIGHTS], *[v_o[n] for n in WEIGHTS])
```

```python
import functools

import jax
import jax.numpy as jnp
from jax import lax
from jax.experimental import pallas as pl
from jax.experimental.pallas import tpu as pltpu

F32 = jnp.float32
BF16 = jnp.bfloat16

CHUNK = 64
GLA_HEADS = 4
DN_HEADS = 16
LOWRANK = 16
GLA_TAU = 16.0
DN_CONV = 4
EPS = 1e-6
ZS = 128
A_LANE, B_LANE = LOWRANK, LOWRANK + DN_HEADS
ADAM_LR, ADAM_B1, ADAM_B2, ADAM_EPS, ADAM_WD, ADAM_STEP = 0.001, 0.9, 0.999, 1e-08, 0.01, 10

V7X_VMEM_BYTES = 64 * 1024 * 1024
VMEM_LIMIT_BYTES = V7X_VMEM_BYTES - 8 * 1024 * 1024
LANES = 128
SUBLANES = 8
MESH = pl.DeviceIdType.MESH
DN_HEADS_PER_STEP = 16
GLA_HEADS_PER_STEP = 4

WEIGHTS = ['g_mix', 'w_in', 'gla_w2', 'gla_b', 'gla_norm', 'dn_conv', 'dn_a_log', 'dn_dt_bias', 'dn_norm', 'w_out',
           'g_mlp', 'w_up', 'w_down', 'g_ple', 'w_ple_gate', 'w_ple_proj', 'g_final']
BIG = [('w_in', 1), ('w_out', 0), ('w_up', 1), ('w_down', 0), ('w_ple_gate', 0), ('w_ple_proj', 1)]
SMALL = [n for n in WEIGHTS if n not in dict(BIG)]

_NN, _NT, _TN = 'nn', 'nt', 'tn'


def _params(sem=None):
    return pltpu.CompilerParams(dimension_semantics=sem, vmem_limit_bytes=VMEM_LIMIT_BYTES)


def _dot(a, b, form, precision=None):
    o = a.ndim - 2
    contract = {_NN: ((1 + o,), (o,)), _NT: ((1 + o,), (1 + o,)), _TN: ((o,), (o,))}[form]
    batch = ((0,), (0,)) if o else ((), ())
    return lax.dot_general(a, b, (contract, batch), precision=precision, preferred_element_type=F32)


def _make_mm(cast, precision):
    def raw(a, b, dims):
        return _dot(cast(a), cast(b), dims, precision)

    @jax.custom_vjp
    def nn(a, b):
        return raw(a, b, _NN)
    nn.defvjp(lambda a, b: (raw(a, b, _NN), (a, b)), lambda r, g: (raw(g, r[1], _NT), raw(r[0], g, _TN)))

    @jax.custom_vjp
    def nt(a, b):
        return raw(a, b, _NT)
    nt.defvjp(lambda a, b: (raw(a, b, _NT), (a, b)), lambda r, g: (raw(g, r[1], _NN), raw(g, r[0], _TN)))

    @jax.custom_vjp
    def tn(a, b):
        return raw(a, b, _TN)
    tn.defvjp(lambda a, b: (raw(a, b, _TN), (a, b)), lambda r, g: (raw(r[1], g, _NT), raw(r[0], g, _NN)))
    return nn, nt, tn


_bnn, _bnt, _btn = _make_mm(lambda t: t.astype(BF16), None)
TRI_PRECISION = lax.Precision.HIGH


def _iota2(n, axis):
    return lax.broadcasted_iota(jnp.int32, (n, n), axis)


def _lower(n, strict=False):
    return (_iota2(n, 0) > _iota2(n, 1)) if strict else (_iota2(n, 0) >= _iota2(n, 1))


def _tri_times(tri, x):
    tri = tri.astype(F32)
    if x.ndim == 3:
        tri = jnp.broadcast_to(tri, (x.shape[0],) + tri.shape)
    return _dot(tri, x, _NN, lax.Precision.HIGHEST)


@jax.custom_vjp
def _cumsum_rows(x):
    return _tri_times(_lower(x.shape[-2]), x)


def _cumsum_rows_bwd(_, g):
    n = g.shape[-2]
    return (_tri_times(_iota2(n, 0) <= _iota2(n, 1), g),)


_cumsum_rows.defvjp(lambda x: (_cumsum_rows(x), None), _cumsum_rows_bwd)


def _tri_inv_impl(a):
    n = a.shape[-1]
    eye = (_iota2(n, 0) == _iota2(n, 1)).astype(F32)
    p = eye - a
    ak = a
    k = 2
    while k < n:
        prec, cast = (TRI_PRECISION, lambda t: t) if k == 2 else (None, lambda t: t.astype(BF16))
        ak = _dot(cast(ak), cast(ak), _NN, prec)
        p = p + _dot(cast(p), cast(ak), _NN, prec)
        k *= 2
    return p


@jax.custom_vjp
def _tri_inv(a):
    return _tri_inv_impl(a)


def _tri_inv_fwd(a):
    t = _tri_inv_impl(a)
    return t, t


def _tri_inv_bwd(t, g):
    tb = t.astype(BF16)
    tg = _dot(tb, g.astype(BF16), _TN)
    return (-_dot(tg.astype(BF16), tb, _NT),)


_tri_inv.defvjp(_tri_inv_fwd, _tri_inv_bwd)


def _shift_rows(x, s, down):
    n = x.shape[0]
    r = lax.broadcasted_iota(jnp.int32, x.shape, 0)
    if down:
        return jnp.where(r >= s, pltpu.roll(x, s, 0), 0.0)
    return jnp.where(r < n - s, pltpu.roll(x, n - s, 0), 0.0)


def _make_shift(s):
    @jax.custom_vjp
    def f(x):
        return _shift_rows(x, s, True)
    f.defvjp(lambda x: (_shift_rows(x, s, True), None), lambda _, g: (_shift_rows(g, s, False),))
    return f


def _sigmoid(x):
    return jax.nn.sigmoid(x)


def _silu(x):
    return x * jax.nn.sigmoid(x)


def _softplus(x):
    return jnp.maximum(x, 0.0) + jnp.log1p(jnp.exp(-jnp.abs(x)))


def _log_sigmoid(x):
    return -_softplus(-x)


def _rms(x, g):
    return x * lax.rsqrt(jnp.mean(x * x, axis=-1, keepdims=True) + EPS) * g


def _gla_chunk(q, k, v, zs, w2, gb, st, *, scale):
    c = q.shape[-2]
    logf = _log_sigmoid(_bnn(zs, w2) + gb) * (1.0 / GLA_TAU)
    bcum = _cumsum_rows(logf)
    b_last = jnp.sum(logf, axis=-2, keepdims=True)
    q_in = (q * scale) * jnp.exp(bcum)
    k_in = k * jnp.exp(-bcum)
    a = jnp.where(_lower(c), _bnt(q_in, k_in), 0.0)
    o = _bnn(a, v) + _bnt(q_in, st)
    k_dec = k * jnp.exp(b_last - bcum)
    st_new = st * jnp.exp(b_last) + _btn(v, k_dec)
    return o, st_new


def _dn_chunk(q, k, v, aw, bw, alog, dtb, s):
    c = q.shape[-2]
    incl, strict = _lower(c), _lower(c, True)
    g_w = -jnp.exp(alog) * _softplus(aw + dtb)
    beta_w = _sigmoid(bw)
    gcum_w = _cumsum_rows(g_w)
    lane0 = lax.broadcasted_iota(jnp.int32, gcum_w.shape, gcum_w.ndim - 1) == 0
    gcol = jnp.sum(jnp.where(lane0, gcum_w, 0.0), axis=-1, keepdims=True)
    d1 = jnp.broadcast_to(gcol, gcol.shape[:-1] + (c,))
    diff = jnp.where(incl, d1 - jnp.swapaxes(d1, -1, -2), 0.0)
    decay = jnp.where(incl, jnp.exp(diff), 0.0)
    k_beta = k * beta_w
    a = jnp.where(strict, _bnt(k_beta, k) * decay, 0.0)
    t = _tri_inv(a)
    egc = jnp.exp(gcum_w)
    u = _bnn(t, v * beta_w)
    w = _bnn(t, k_beta * egc)
    attn = jnp.where(incl, _bnt(q, k) * decay, 0.0)
    q_dec = q * egc
    g_last = jnp.sum(g_w, axis=-2, keepdims=True)
    k_dec = k * jnp.exp(g_last - gcum_w)
    v_new = u - _bnn(w, s)
    o = _bnn(q_dec, s) + _bnn(attn, v_new)
    s_new = s * jnp.exp(g_last) + _btn(k_dec, v_new)
    return o, s_new


def _conv_act(x, wrows, *, l2, scale):
    taps = len(wrows)
    y = None
    for j in range(taps):
        s = taps - 1 - j
        xs = x if s == 0 else _make_shift(s)(x)
        y = wrows[j] * xs if y is None else y + wrows[j] * xs
    y = _silu(y)
    if l2:
        y = y * lax.rsqrt(jnp.sum(y * y, axis=-1, keepdims=True) + EPS) * scale
    return y


def _merge_math(og, gg, od, dz, ga, gb, gn, dn):
    nsub = len(og)
    dv = nsub * og[0].shape[1]
    ssq = jnp.sum(og[0] * og[0], axis=-1, keepdims=True)
    for s in range(1, nsub):
        ssq = ssq + jnp.sum(og[s] * og[s], axis=-1, keepdims=True)
    r = lax.rsqrt(ssq * (1.0 / dv) + EPS)
    outs = []
    for s in range(nsub):
        a = og[s] * r * gn[s] * _silu(gg[s])
        b = _rms(od[s], dn) * _silu(dz[s])
        outs.append(_sigmoid(ga[s]) * a + _sigmoid(gb[s]) * b)
    return outs


def _pick(n, target, mult):
    best = None
    for d in range(mult, min(n, target) + 1, mult):
        if n % d == 0:
            best = d
    return best if best is not None else n


class _Stage:
    def __init__(self, inputs, out_shapes, n_sems, copies, aliases=None):
        self.inputs, self.out_shapes, self.n_sems, self.copies = list(inputs), list(out_shapes), n_sems, copies
        self.aliases = aliases or {}

    @property
    def sems(self):
        return [pltpu.SemaphoreType.DMA((self.n_sems,)), pltpu.SemaphoreType.DMA((self.n_sems,))]


def _host_stage(body, stage, n_in, n_out, grid):
    ci, co = len(stage.inputs), len(stage.out_shapes)

    def wrapped(*refs):
        ins, cins = refs[:n_in], refs[n_in:n_in + ci]
        outs, couts = refs[n_in + ci:n_in + ci + n_out], refs[n_in + ci + n_out:n_in + ci + n_out + co]
        scratch, sems = refs[n_in + ci + n_out + co:-2], refs[-2:]
        ids = [pl.program_id(d) for d in range(len(grid))]
        first, last = ids[0] == 0, ids[0] == grid[0] - 1
        for i, g in zip(ids[1:], grid[1:]):
            first, last = first & (i == 0), last & (i == g - 1)

        @pl.when(first)
        def _():
            for cp in stage.copies(cins, couts, *sems):
                cp.start()

        body(*ins, *outs, *scratch)

        @pl.when(last)
        def _():
            for cp in stage.copies(cins, couts, *sems):
                cp.wait()

    return wrapped


def _call(body, name, grid, in_specs, out_specs, out_shape, scratch, semantics, args, stage=None):
    if stage is None:
        return pl.pallas_call(body, name=name, grid=grid, in_specs=list(in_specs), out_specs=list(out_specs),
                              out_shape=list(out_shape), scratch_shapes=list(scratch), compiler_params=_params(semantics))(*args)
    n_in, n_out = len(in_specs), len(out_specs)
    return pl.pallas_call(
        _host_stage(body, stage, n_in, n_out, grid), name=name, grid=grid,
        in_specs=list(in_specs) + [ANY] * len(stage.inputs),
        out_specs=list(out_specs) + [ANY] * len(stage.out_shapes), out_shape=list(out_shape) + stage.out_shapes,
        scratch_shapes=list(scratch) + stage.sems,
        input_output_aliases={n_in + i: n_out + o for i, o in stage.aliases.items()},
        compiler_params=_params(("arbitrary",) * len(grid)),
    )(*args, *stage.inputs)


def _run_stage(stage, name):
    ci = len(stage.inputs)

    def body(*refs):
        cps = stage.copies(refs[:ci], refs[ci:-2], *refs[-2:])
        for cp in cps:
            cp.start()
        for cp in cps:
            cp.wait()

    return pl.pallas_call(body, name=name, in_specs=[ANY] * ci, out_specs=[ANY] * len(stage.out_shapes),
                          out_shape=stage.out_shapes, scratch_shapes=stage.sems,
                          input_output_aliases=dict(stage.aliases))(*stage.inputs)


def _matmul(a, b, form, out_dtypes, name, epilogue=None, extras=(), bm=1024, bn=1024, bk=2048,
            b_slots=False, out_slots=False, stage=None):
    ns, c = (b.shape[0], b.shape[2]) if b_slots else (1, None)
    b2 = b.shape[1:] if b_slots else b.shape
    if form == 'nn':
        (M, K), (K2, N) = a.shape, (b2[0], b2[1] * ns)
    elif form == 'nt':
        (M, K), (N, K2) = a.shape, (b2[0], b2[1] * ns)
    else:
        (K, M), (K2, N) = a.shape, b2
    assert K == K2 and not (b_slots and form == 'tn'), (a.shape, b.shape, form)
    bm, bn, bk = _pick(M, bm, SUBLANES), _pick(N, bn, LANES), _pick(K, bk, LANES)
    if b_slots:
        bn, bk = (_pick(c, bn, LANES), bk) if form == 'nn' else (bn, _pick(c, bk, LANES))
    if out_slots:
        oc = N // 4
        bn = _pick(oc, bn, LANES)
    nk = K // bk
    a_spec = pl.BlockSpec((bk, bm), lambda i, j, k: (k, i)) if form == 'tn' else pl.BlockSpec((bm, bk), lambda i, j, k: (i, k))
    if b_slots and form == 'nn':
        per = c // bn
        b_spec = pl.BlockSpec((None, bk, bn), lambda i, j, k: (j // per, k, j % per))
    elif b_slots:
        per = c // bk
        b_spec = pl.BlockSpec((None, bn, bk), lambda i, j, k: (k // per, j, k % per))
    elif form == 'nt':
        b_spec = pl.BlockSpec((bn, bk), lambda i, j, k: (j, k))
    else:
        b_spec = pl.BlockSpec((bk, bn), lambda i, j, k: (k, j))
    o_spec = pl.BlockSpec((bm, bn), lambda i, j, k: (i, j))
    if out_slots:
        oper = oc // bn
        out_spec = pl.BlockSpec((None, bm, bn), lambda i, j, k: (j // oper, i, j % oper))
        out_shape = [jax.ShapeDtypeStruct((4, M, oc), d) for d in out_dtypes]
    else:
        out_spec = o_spec
        out_shape = [jax.ShapeDtypeStruct((M, N), d) for d in out_dtypes]
    ne, no = len(extras), len(out_dtypes)

    def finish(r, extra_refs, out_refs):
        outs = (r,) if epilogue is None else epilogue(r, *[e[...] for e in extra_refs])
        for ref, o in zip(out_refs, outs):
            ref[...] = o.astype(ref.dtype)

    def body_one(a_ref, b_ref, *rest):
        finish(_dot(a_ref[...].astype(BF16), b_ref[...].astype(BF16), form), rest[:ne], rest[ne:ne + no])

    def body_acc(a_ref, b_ref, *rest):
        extra_refs, out_refs, acc = rest[:ne], rest[ne:ne + no], rest[ne + no]
        k = pl.program_id(2)
        part = _dot(a_ref[...].astype(BF16), b_ref[...].astype(BF16), form)

        @pl.when(k == 0)
        def _():
            acc[...] = part

        @pl.when((k > 0) & (k < nk - 1))
        def _():
            acc[...] += part

        @pl.when(k == nk - 1)
        def _():
            finish(acc[...] + part, extra_refs, out_refs)

    return _call(body_one if nk == 1 else body_acc, name, (M // bm, N // bn, nk), [a_spec, b_spec] + [o_spec] * ne,
                 [out_spec] * no, out_shape, [] if nk == 1 else [pltpu.VMEM((bm, bn), F32)],
                 ("parallel", "parallel", "arbitrary"), (a, b, *extras), stage)


def _rowwise(fn, rows, consts, row_outs, acc_outs, name, bt=256, stage=None):
    T = rows[0].shape[0]
    bt = _pick(T, bt, SUBLANES)
    nr, nc, no, na = len(rows), len(consts), len(row_outs), len(acc_outs)

    def body(*refs):
        r_in, c_in = refs[:nr], refs[nr:nr + nc]
        r_out, a_out = refs[nr + nc:nr + nc + no], refs[nr + nc + no:]
        ro, ao = fn([r[...] for r in r_in], [c[...] for c in c_in])
        for ref, o in zip(r_out, ro):
            ref[...] = o.astype(ref.dtype)
        if na:
            @pl.when(pl.program_id(0) == 0)
            def _():
                for ref in a_out:
                    ref[...] = jnp.zeros_like(ref)
            for ref, o in zip(a_out, ao):
                ref[...] += o

    whole = lambda shp: pl.BlockSpec(shp, lambda i: (0,) * len(shp))
    return _call(
        body, name, (T // bt,),
        [pl.BlockSpec((bt, r.shape[1]), lambda i: (i, 0)) for r in rows] + [whole(c.shape) for c in consts],
        [pl.BlockSpec((bt, w), lambda i: (i, 0)) for w, _ in row_outs] + [whole(s) for s in acc_outs],
        [jax.ShapeDtypeStruct((T, w), d) for w, d in row_outs] + [jax.ShapeDtypeStruct(s, F32) for s in acc_outs],
        [], ("arbitrary",), (*rows, *consts), stage)


def _rmsnorm_fwd(x, g, name, stage=None):
    return _rowwise(lambda r, c: ([_rms(r[0], c[0])], []), [x], [g], [(x.shape[1], BF16)], [], name, stage=stage)


def _rmsnorm_bwd_add(x, g, dh, dres, name):
    D = x.shape[1]

    def fn(r, c):
        _, vjp = jax.vjp(_rms, r[0], c[0])
        dx, dg = vjp(r[1])
        dx = dx + r[2]
        return [dx, dx], [dg]
    return _rowwise(fn, [x, dh, dres], [g], [(D, F32), (D, BF16)], [(1, D)], name)


def _loss_fwd_bwd(x3, g, target, name):
    D = x3.shape[1]

    def fn(r, c):
        def row_loss(x, gain):
            err = _rms(x, gain) - r[1]
            return 0.5 * jnp.mean(err * err, axis=-1, keepdims=True)
        lrow, vjp = jax.vjp(row_loss, r[0], c[0])
        dx, dg = vjp(jnp.ones_like(lrow))
        tile = jnp.broadcast_to(jnp.sum(lrow, axis=0, keepdims=True), (SUBLANES, LANES))
        return [dx], [tile, dg]
    return _rowwise(fn, [x3, target], [g], [(D, F32)], [(SUBLANES, LANES), (1, D)], name)


def _ple_bwd(dx3, gp, pp, name):
    D = dx3.shape[1]

    def fn(r, c):
        s = _sigmoid(r[1])
        return [r[0] * r[2] * s * (1.0 - s), r[0] * s], []
    return _rowwise(fn, [dx3, gp, pp], [], [(D, BF16), (D, BF16)], [], name)


def _adamw_math(w, g, m, v):
    nm = ADAM_B1 * m + (1.0 - ADAM_B1) * g
    nv = ADAM_B2 * v + (1.0 - ADAM_B2) * (g * g)
    m_hat = nm / (1.0 - ADAM_B1 ** ADAM_STEP)
    v_hat = nv / (1.0 - ADAM_B2 ** ADAM_STEP)
    return -ADAM_LR * (m_hat / (jnp.sqrt(v_hat) + ADAM_EPS) + ADAM_WD * w), nm, nv


def _adamw(w, g, m, v, name, with_grad=False):
    R, C = w.shape[0], w.shape[-1]
    lanes = -(-C // LANES) * LANES
    if w.ndim == 2:
        bt = _pick(R, max(SUBLANES, (1 << 18) // lanes // SUBLANES * SUBLANES), SUBLANES)
        spec = pl.BlockSpec((bt, C), lambda i: (i, 0))
    else:
        bt = _pick(R, max(1, (1 << 18) // lanes), 1)
        spec = pl.BlockSpec((bt, 1, C), lambda i: (i, 0, 0))

    def body(w_ref, g_ref, m_ref, v_ref, d_ref, nm_ref, nv_ref, *g_out):
        d_ref[...], nm_ref[...], nv_ref[...] = _adamw_math(w_ref[...], g_ref[...], m_ref[...], v_ref[...])
        for ref in g_out:
            ref[...] = g_ref[...]

    n_out = 4 if with_grad else 3
    return pl.pallas_call(
        body, name=name, grid=(R // bt,), in_specs=[spec] * 4, out_specs=[spec] * n_out,
        out_shape=[jax.ShapeDtypeStruct(w.shape, F32)] * n_out, compiler_params=_params(("parallel",)),
    )(w, g, m, v)


def _adamw_small(ws, gs, ms, vs):
    n = len(ws)

    def body(*refs):
        for i in range(n):
            d, nm, nv = _adamw_math(refs[i][...], refs[n + i][...], refs[2 * n + i][...], refs[3 * n + i][...])
            refs[4 * n + i][...], refs[5 * n + i][...], refs[6 * n + i][...] = d, nm, nv

    VMEM = pl.BlockSpec(memory_space=pltpu.VMEM)
    shapes = [jax.ShapeDtypeStruct(w.shape, F32) for w in ws]
    outs = pl.pallas_call(body, name="adamw_small", in_specs=[VMEM] * (4 * n), out_specs=[VMEM] * (3 * n),
                          out_shape=shapes * 3)(*ws, *gs, *ms, *vs)
    return outs[:n], outs[n:2 * n], outs[2 * n:]


def _gla_fwd(z_big, z_small, w2h, gbh, Bl, S, D, stage=None):
    NC, dk, dv, HB = S // CHUNK, D // (2 * GLA_HEADS), D // GLA_HEADS, GLA_HEADS_PER_STEP
    HG = GLA_HEADS // HB
    chains = [(hh, bb) for hh in range(HB) for bb in range(Bl)]
    G = len(chains)
    fn = functools.partial(_gla_chunk, scale=dk ** -0.5)

    def body(q, k, v, z, w2, gb, o_ref, stall_ref, st):
        n, g = pl.program_id(0), pl.program_id(1)

        @pl.when(n == 0)
        def _():
            st[g] = jnp.zeros((G, dv, dk), F32)
        s0 = st[g]
        stall_ref[...] = s0.reshape(HB, Bl, dv, dk)
        qk = lambda r: jnp.stack([r[bb, :, hh * dk:(hh + 1) * dk] for hh, bb in chains])
        o, s_new = fn(qk(q), qk(k), jnp.stack([v[bb, :, hh * dv:(hh + 1) * dv] for hh, bb in chains]),
                      jnp.stack([z[bb] for _, bb in chains]), jnp.stack([w2[hh] for hh, _ in chains]),
                      jnp.stack([gb[hh] for hh, _ in chains]), s0)
        for i, (hh, bb) in enumerate(chains):
            o_ref[bb, :, hh * dv:(hh + 1) * dv] = o[i]
        st[g] = s_new

    return _call(
        body, "gla_fwd", (NC, HG),
        [pl.BlockSpec((Bl, CHUNK, HB * dk), lambda n, g: (0, n, g)),
         pl.BlockSpec((Bl, CHUNK, HB * dk), lambda n, g: (0, n, HG + g)),
         pl.BlockSpec((Bl, CHUNK, HB * dv), lambda n, g: (0, n, HG + g)),
         pl.BlockSpec((Bl, CHUNK, ZS), lambda n, g: (0, n, 0)),
         pl.BlockSpec((HB, ZS, dk), lambda n, g: (g, 0, 0)),
         pl.BlockSpec((HB, 1, dk), lambda n, g: (g, 0, 0))],
        [pl.BlockSpec((Bl, CHUNK, HB * dv), lambda n, g: (0, n, g)),
         pl.BlockSpec((HB, Bl, None, dv, dk), lambda n, g: (g, 0, n, 0, 0))],
        [jax.ShapeDtypeStruct((Bl, S, D), F32), jax.ShapeDtypeStruct((GLA_HEADS, Bl, NC, dv, dk), F32)],
        [pltpu.VMEM((HG, G, dv, dk), F32)], ("arbitrary", "arbitrary"), (z_big, z_big, z_big, z_small, w2h, gbh), stage)


def _gla_bwd(z_big, z_small, w2h, gbh, st_all, do, Bl, S, D):
    NC, dk, dv, HB = S // CHUNK, D // (2 * GLA_HEADS), D // GLA_HEADS, GLA_HEADS_PER_STEP
    HG = GLA_HEADS // HB
    chains = [(hh, bb) for hh in range(HB) for bb in range(Bl)]
    G = len(chains)
    fn = functools.partial(_gla_chunk, scale=dk ** -0.5)

    def body(q, k, v, z, w2, gb, st0, do_ref, dq_ref, dk_ref, dv_ref, dzs_ref, dw2_ref, dgb_ref, dst):
        n, g = pl.program_id(0), pl.program_id(1)

        @pl.when(n == 0)
        def _():
            dst[g] = jnp.zeros((G, dv, dk), F32)

        @pl.when((n == 0) & (g == 0))
        def _():
            dw2_ref[...] = jnp.zeros_like(dw2_ref)
            dgb_ref[...] = jnp.zeros_like(dgb_ref)

        qk = lambda r: jnp.stack([r[bb, :, hh * dk:(hh + 1) * dk] for hh, bb in chains])
        vv = lambda r: jnp.stack([r[bb, :, hh * dv:(hh + 1) * dv] for hh, bb in chains])
        _, vjp = jax.vjp(fn, qk(q), qk(k), vv(v), jnp.stack([z[bb] for _, bb in chains]),
                         jnp.stack([w2[hh] for hh, _ in chains]), jnp.stack([gb[hh] for hh, _ in chains]),
                         st0[...].reshape(G, dv, dk))
        dq, dkk, dvv, dzs, dw2, dgb, dst0 = vjp((vv(do_ref), dst[g]))
        for i, (hh, bb) in enumerate(chains):
            dq_ref[bb, :, hh * dk:(hh + 1) * dk] = dq[i].astype(dq_ref.dtype)
            dk_ref[bb, :, hh * dk:(hh + 1) * dk] = dkk[i].astype(dk_ref.dtype)
            dv_ref[bb, :, hh * dv:(hh + 1) * dv] = dvv[i].astype(dv_ref.dtype)
            dw2_ref[g * HB + hh] += dw2[i]
            dgb_ref[g * HB + hh] += dgb[i]
        for bb in range(Bl):
            tot = sum(dzs[i] for i, (_, b2) in enumerate(chains) if b2 == bb)

            @pl.when(g == 0)
            def _():
                dzs_ref[bb] = tot

            @pl.when(g > 0)
            def _():
                dzs_ref[bb] += tot
        dst[g] = dst0

    rn = lambda n: NC - 1 - n
    return pl.pallas_call(
        body, name="gla_bwd", grid=(NC, HG),
        in_specs=[pl.BlockSpec((Bl, CHUNK, HB * dk), lambda n, g: (0, rn(n), g)),
                  pl.BlockSpec((Bl, CHUNK, HB * dk), lambda n, g: (0, rn(n), HG + g)),
                  pl.BlockSpec((Bl, CHUNK, HB * dv), lambda n, g: (0, rn(n), HG + g)),
                  pl.BlockSpec((Bl, CHUNK, ZS), lambda n, g: (0, rn(n), 0)),
                  pl.BlockSpec((HB, ZS, dk), lambda n, g: (g, 0, 0)),
                  pl.BlockSpec((HB, 1, dk), lambda n, g: (g, 0, 0)),
                  pl.BlockSpec((HB, Bl, None, dv, dk), lambda n, g: (g, 0, rn(n), 0, 0)),
                  pl.BlockSpec((Bl, CHUNK, HB * dv), lambda n, g: (0, rn(n), g))],
        out_specs=[pl.BlockSpec((Bl, CHUNK, HB * dk), lambda n, g: (0, rn(n), g)),
                   pl.BlockSpec((Bl, CHUNK, HB * dk), lambda n, g: (0, rn(n), g)),
                   pl.BlockSpec((Bl, CHUNK, HB * dv), lambda n, g: (0, rn(n), g)),
                   pl.BlockSpec((Bl, CHUNK, ZS), lambda n, g: (0, rn(n), 0)),
                   pl.BlockSpec((GLA_HEADS, ZS, dk), lambda n, g: (0, 0, 0)),
                   pl.BlockSpec((GLA_HEADS, 1, dk), lambda n, g: (0, 0, 0))],
        out_shape=[jax.ShapeDtypeStruct((Bl, S, D // 2), BF16), jax.ShapeDtypeStruct((Bl, S, D // 2), BF16),
                   jax.ShapeDtypeStruct((Bl, S, D), BF16), jax.ShapeDtypeStruct((Bl, S, ZS), F32),
                   jax.ShapeDtypeStruct((GLA_HEADS, ZS, dk), F32), jax.ShapeDtypeStruct((GLA_HEADS, 1, dk), F32)],
        scratch_shapes=[pltpu.VMEM((HG, G, dv, dk), F32)],
        compiler_params=_params(("arbitrary", "arbitrary")),
    )(z_big, z_big, z_big, z_small, w2h, gbh, st_all, do)


def _conv_fwd(z_big, conv_w, grp, Bl, S, D):
    d = D // DN_HEADS
    l2, scale = grp < 2, (d ** -0.5 if grp == 0 else 1.0)
    x_blk0 = (3 * D + grp * D) // d

    def body(x_ref, w_ref, o_ref):
        wrows = [w_ref[j:j + 1, :] for j in range(DN_CONV)]
        o_ref[...] = _conv_act(x_ref[...], wrows, l2=l2, scale=scale)

    return pl.pallas_call(
        body, name=f"conv_fwd{grp}", grid=(Bl, DN_HEADS),
        in_specs=[pl.BlockSpec((S, d), lambda b, j: (b, x_blk0 + j)),
                  pl.BlockSpec((DN_CONV, d), lambda b, j: (0, grp * DN_HEADS + j))],
        out_specs=pl.BlockSpec((S, d), lambda b, j: (b, j)),
        out_shape=jax.ShapeDtypeStruct((Bl * S, D), F32),
        compiler_params=_params(("parallel", "parallel")),
    )(z_big, conv_w)


def _conv_bwd(z_big, conv_w, dact, grp, Bl, S, D):
    d = D // DN_HEADS
    l2, scale = grp < 2, (d ** -0.5 if grp == 0 else 1.0)
    x_blk0 = (3 * D + grp * D) // d

    def body(x_ref, w_ref, g_ref, dx_ref, dw_ref):
        @pl.when(pl.program_id(1) == 0)
        def _():
            dw_ref[...] = jnp.zeros_like(dw_ref)
        wrows = [w_ref[j:j + 1, :] for j in range(DN_CONV)]
        _, vjp = jax.vjp(lambda x, wr: _conv_act(x, wr, l2=l2, scale=scale), x_ref[...], wrows)
        dx, dwr = vjp(g_ref[...])
        dx_ref[...] = dx.astype(dx_ref.dtype)
        for j in range(DN_CONV):
            dw_ref[j:j + 1, :] += dwr[j]

    return pl.pallas_call(
        body, name=f"conv_bwd{grp}", grid=(DN_HEADS, Bl),
        in_specs=[pl.BlockSpec((S, d), lambda j, b: (b, x_blk0 + j)),
                  pl.BlockSpec((DN_CONV, d), lambda j, b: (0, grp * DN_HEADS + j)),
                  pl.BlockSpec((S, d), lambda j, b: (b, j))],
        out_specs=[pl.BlockSpec((S, d), lambda j, b: (b, j)), pl.BlockSpec((DN_CONV, d), lambda j, b: (0, j))],
        out_shape=[jax.ShapeDtypeStruct((Bl * S, D), BF16), jax.ShapeDtypeStruct((DN_CONV, D), F32)],
        compiler_params=_params(("arbitrary", "arbitrary")),
    )(z_big, conv_w, dact)


def _lane_column(zb, lane, width):
    pick = lax.broadcasted_iota(jnp.int32, zb.shape, 1) == lane
    return jnp.broadcast_to(jnp.sum(jnp.where(pick, zb, 0.0), axis=-1, keepdims=True), (zb.shape[0], width))


def _dn_fwd(qa, ka, va, z_small, alog, dtb, Bl, S, D, stage=None):
    NC, d, HB = S // CHUNK, D // DN_HEADS, DN_HEADS_PER_STEP
    HG = DN_HEADS // HB
    chains = [(hh, bb) for hh in range(HB) for bb in range(Bl)]
    G = len(chains)

    def body(q, k, v, z, al, dt, o_ref, sall_ref, st):
        n, g = pl.program_id(0), pl.program_id(1)

        @pl.when(n == 0)
        def _():
            st[g] = jnp.zeros((G, d, d), F32)
        tok_in = lambda r: jnp.stack([r[bb, :, hh * d:(hh + 1) * d] for hh, bb in chains])
        head_in = lambda r: jnp.stack([r[hh] for hh, _ in chains])
        gate_in = lambda lane0: jnp.stack([_lane_column(z[bb], lane0 + g * HB + hh, d) for hh, bb in chains])
        s0 = st[g]
        sall_ref[...] = s0.reshape(HB, Bl, d, d)
        o, s_new = _dn_chunk(tok_in(q), tok_in(k), tok_in(v), gate_in(A_LANE), gate_in(B_LANE), head_in(al), head_in(dt), s0)
        for i, (hh, bb) in enumerate(chains):
            o_ref[bb, :, hh * d:(hh + 1) * d] = o[i]
        st[g] = s_new

    tok = pl.BlockSpec((Bl, CHUNK, HB * d), lambda n, g: (0, n, g))
    per_head = pl.BlockSpec((HB, 1, d), lambda n, g: (g, 0, 0))
    return _call(
        body, "dn_fwd", (NC, HG),
        [tok, tok, tok, pl.BlockSpec((Bl, CHUNK, ZS), lambda n, g: (0, n, 0)), per_head, per_head],
        [tok, pl.BlockSpec((HB, Bl, None, d, d), lambda n, g: (g, 0, n, 0, 0))],
        [jax.ShapeDtypeStruct((Bl, S, D), F32), jax.ShapeDtypeStruct((DN_HEADS, Bl, NC, d, d), F32)],
        [pltpu.VMEM((HG, G, d, d), F32)], ("arbitrary", "arbitrary"), (qa, ka, va, z_small, alog, dtb), stage)


def _dn_bwd(qa, ka, va, z_small, alog, dtb, s_all, do, dzs_gla, Bl, S, D, stage=None):
    NC, d, HB = S // CHUNK, D // DN_HEADS, DN_HEADS_PER_STEP
    HG = DN_HEADS // HB
    chains = [(hh, bb) for hh in range(HB) for bb in range(Bl)]
    G = len(chains)

    def lanesum(t):
        return jnp.sum(t, axis=-1, keepdims=True)

    def body(q, k, v, z, al, dt, s0_ref, do_ref, dzg_ref, dq_ref, dk_ref, dv_ref, dzs_ref, dal_ref, ddt_ref, dst):
        n, g = pl.program_id(0), pl.program_id(1)

        @pl.when(n == 0)
        def _():
            dst[g] = jnp.zeros((G, d, d), F32)

        @pl.when((n == 0) & (g == 0))
        def _():
            dal_ref[...] = jnp.zeros_like(dal_ref)
            ddt_ref[...] = jnp.zeros_like(ddt_ref)

        tok_in = lambda r: jnp.stack([r[bb, :, hh * d:(hh + 1) * d] for hh, bb in chains])
        head_in = lambda r: jnp.stack([r[hh] for hh, _ in chains])
        gate_in = lambda lane0: jnp.stack([_lane_column(z[bb], lane0 + g * HB + hh, d) for hh, bb in chains])
        _, vjp = jax.vjp(_dn_chunk, tok_in(q), tok_in(k), tok_in(v), gate_in(A_LANE), gate_in(B_LANE), head_in(al),
                         head_in(dt), s0_ref[...].reshape(G, d, d))
        dq, dkk, dvv, da, db, dal, ddt, ds0 = vjp((tok_in(do_ref), dst[g]))
        da, db = lanesum(da), lanesum(db)
        dal = jnp.broadcast_to(lanesum(dal), (G, 1, d))
        ddt = jnp.broadcast_to(lanesum(ddt), (G, 1, d))
        lane = lax.broadcasted_iota(jnp.int32, (CHUNK, ZS), 1)
        for bb in range(Bl):
            part = jnp.zeros((CHUNK, ZS), F32)
            for i, (hh, b2) in enumerate(chains):
                if b2 == bb:
                    h = g * HB + hh
                    part = part + jnp.where(lane == A_LANE + h, da[i], 0.0) + jnp.where(lane == B_LANE + h, db[i], 0.0)

            @pl.when(g == 0)
            def _():
                dzs_ref[bb] = jnp.where(lane < LOWRANK, dzg_ref[bb], 0.0) + part

            @pl.when(g > 0)
            def _():
                dzs_ref[bb] += part
        for i, (hh, bb) in enumerate(chains):
            cols = slice(hh * d, (hh + 1) * d)
            dq_ref[bb, :, cols] = dq[i]
            dk_ref[bb, :, cols] = dkk[i]
            dv_ref[bb, :, cols] = dvv[i]
            dal_ref[g * HB + hh] += dal[i]
            ddt_ref[g * HB + hh] += ddt[i]
        dst[g] = ds0

    rn = lambda n: NC - 1 - n
    tok = pl.BlockSpec((Bl, CHUNK, HB * d), lambda n, g: (0, rn(n), g))
    zsb = pl.BlockSpec((Bl, CHUNK, ZS), lambda n, g: (0, rn(n), 0))
    per_head = pl.BlockSpec((HB, 1, d), lambda n, g: (g, 0, 0))
    all_heads = pl.BlockSpec((DN_HEADS, 1, d), lambda n, g: (0, 0, 0))
    tok_shape = jax.ShapeDtypeStruct((Bl, S, D), F32)
    head_shape = jax.ShapeDtypeStruct((DN_HEADS, 1, d), F32)
    return _call(
        body, "dn_bwd", (NC, HG),
        [tok, tok, tok, zsb, per_head, per_head,
         pl.BlockSpec((HB, Bl, None, d, d), lambda n, g: (g, 0, rn(n), 0, 0)), tok, zsb],
        [tok, tok, tok, zsb, all_heads, all_heads],
        [tok_shape, tok_shape, tok_shape, jax.ShapeDtypeStruct((Bl, S, ZS), F32), head_shape, head_shape],
        [pltpu.VMEM((HG, G, d, d), F32)], ("arbitrary", "arbitrary"),
        (qa, ka, va, z_small, alog, dtb, s_all, do, dzs_gla), stage)


def _merge_specs(D, bt):
    dv, w = D // GLA_HEADS, D // DN_HEADS
    col = lambda off: pl.BlockSpec((bt, dv), lambda i, h: (i, off // dv + h))
    return dv, w, col


def _merge_load(refs, nsub, w):
    return [[r[:, s * w:(s + 1) * w] for s in range(nsub)] for r in refs]


def _merge_fwd(o_gla, o_dn, z_big, gla_norm, dn_norm, D, bt=256, stage=None):
    T = o_gla.shape[0]
    bt = _pick(T, bt, SUBLANES)
    dv, w, col = _merge_specs(D, bt)
    nsub = dv // w

    def body(og, gg, od, dz, ga, gb, gn, dn, out):
        ogl, ggl, odl, dzl, gal, gbl = _merge_load([og, gg, od, dz, ga, gb], nsub, w)
        gnl = [gn[:, s * w:(s + 1) * w] for s in range(nsub)]
        outs = _merge_math(ogl, ggl, odl, dzl, gal, gbl, gnl, dn[...])
        for s in range(nsub):
            out[:, s * w:(s + 1) * w] = outs[s].astype(out.dtype)

    return _call(
        body, "merge_fwd", (T // bt, GLA_HEADS),
        [col(0), col(2 * D), col(0), col(6 * D), col(7 * D), col(8 * D),
         pl.BlockSpec((1, dv), lambda i, h: (0, 0)), pl.BlockSpec((1, w), lambda i, h: (0, 0))],
        [col(0)], [jax.ShapeDtypeStruct((T, D), BF16)], [], ("parallel", "parallel"),
        (o_gla, z_big, o_dn, z_big, z_big, z_big, gla_norm, dn_norm), stage)


def _merge_bwd(o_gla, o_dn, z_big, gla_norm, dn_norm, dmix, D, bt=256):
    T = o_gla.shape[0]
    bt = _pick(T, bt, SUBLANES)
    dv, w, col = _merge_specs(D, bt)
    nsub = dv // w

    def body(og, gg, od, dz, ga, gb, gn, dn, dm, dog, dgg, dod, ddz, dga, dgb, dgn, ddn):
        @pl.when((pl.program_id(0) == 0) & (pl.program_id(1) == 0))
        def _():
            dgn[...] = jnp.zeros_like(dgn)
            ddn[...] = jnp.zeros_like(ddn)

        ogl, ggl, odl, dzl, gal, gbl, dml = _merge_load([og, gg, od, dz, ga, gb, dm], nsub, w)
        gnl = [gn[:, s * w:(s + 1) * w] for s in range(nsub)]
        _, vjp = jax.vjp(_merge_math, ogl, ggl, odl, dzl, gal, gbl, gnl, dn[...])
        g_og, g_gg, g_od, g_dz, g_ga, g_gb, g_gn, g_dn = vjp(dml)
        for s in range(nsub):
            sl = slice(s * w, (s + 1) * w)
            dog[:, sl] = g_og[s]
            dgg[:, sl] = g_gg[s].astype(dgg.dtype)
            dod[:, sl] = g_od[s]
            ddz[:, sl] = g_dz[s].astype(ddz.dtype)
            dga[:, sl] = g_ga[s].astype(dga.dtype)
            dgb[:, sl] = g_gb[s].astype(dgb.dtype)
            dgn[:, sl] += g_gn[s]
        ddn[...] += g_dn

    f32s, bf16s = jax.ShapeDtypeStruct((T, D), F32), jax.ShapeDtypeStruct((T, D), BF16)
    return pl.pallas_call(
        body, name="merge_bwd", grid=(T // bt, GLA_HEADS),
        in_specs=[col(0), col(2 * D), col(0), col(6 * D), col(7 * D), col(8 * D),
                  pl.BlockSpec((1, dv), lambda i, h: (0, 0)), pl.BlockSpec((1, w), lambda i, h: (0, 0)), col(0)],
        out_specs=[col(0)] * 6 + [pl.BlockSpec((1, dv), lambda i, h: (0, 0)), pl.BlockSpec((1, w), lambda i, h: (0, 0))],
        out_shape=[f32s, bf16s, f32s, bf16s, bf16s, bf16s,
                   jax.ShapeDtypeStruct((1, dv), F32), jax.ShapeDtypeStruct((1, w), F32)],
        compiler_params=_params(("arbitrary", "arbitrary")),
    )(o_gla, z_big, o_dn, z_big, z_big, z_big, gla_norm, dn_norm, dmix)


def _place():
    return lax.axis_index("x"), lax.axis_index("y"), lax.axis_index("c")


def _other_chips(x, y):
    return [(1 - x, y), (x, 1 - y), (1 - x, 1 - y)]


def _rcopy(src, dst, send_sem, recv_sem, dev):
    return pltpu.make_async_remote_copy(src_ref=src, dst_ref=dst, send_sem=send_sem, recv_sem=recv_sem,
                                        device_id=dev, device_id_type=MESH)


ANY = pl.BlockSpec(memory_space=pl.ANY)


ROWS, COLS = 'rows', 'cols'


def _half(ref, hc, by, lead=()):
    shape = ref.shape[len(lead):]
    if by == ROWS:
        rh = shape[0] // 2
        idx = (pl.ds(pl.multiple_of(hc * rh, 16), rh),) + (slice(None),) * (len(shape) - 1)
    else:
        ch = shape[-1] // 2
        idx = (slice(None),) * (len(shape) - 1) + (pl.ds(pl.multiple_of(hc * ch, LANES), ch),)
    return ref.at[(*lead, *idx)]


def _half_shape(shape, by):
    return (shape[0] // 2,) + tuple(shape[1:]) if by == ROWS else tuple(shape[:-1]) + (shape[-1] // 2,)


def _gather_ici(shards, by):
    nw = len(shards)

    def copies(srcs, outs, send_sems, recv_sems):
        x, y, c = _place()
        return [_rcopy(_half(srcs[w], c, by[w]), _half(outs[w], c, by[w], (2 * x + y,)),
                       send_sems.at[3 * w + k], recv_sems.at[3 * w + k], (px, py, c))
                for w in range(nw) for k, (px, py) in enumerate(_other_chips(x, y))]

    return _Stage(shards, [jax.ShapeDtypeStruct((4,) + s.shape, s.dtype) for s in shards], 3 * nw, copies)


def _gather_pass(gathered, by):
    nw = len(gathered)

    def copies(srcs, outs, send_sems, recv_sems):
        x, y, c = _place()
        cps = []
        for w in range(nw):
            for k, (px, py) in enumerate(_other_chips(x, y)):
                slot = (2 * px + py,)
                cps.append(_rcopy(_half(srcs[w], c, by[w], slot), _half(outs[w], c, by[w], slot),
                                  send_sems.at[3 * w + k], recv_sems.at[3 * w + k], (x, y, 1 - c)))
        return cps

    return _Stage(gathered, [jax.ShapeDtypeStruct(g.shape, g.dtype) for g in gathered], 3 * nw, copies,
                  aliases={w: w for w in range(nw)})


def _pair_exchange(ps, by):
    nw = len(ps)

    def copies(srcs, outs, send_sems, recv_sems):
        x, y, c = _place()
        return [_rcopy(_half(srcs[w], 1 - c, by[w], (slice(None),)), outs[w], send_sems.at[w], recv_sems.at[w], (x, y, 1 - c))
                for w in range(nw)]

    return _Stage(ps, [jax.ShapeDtypeStruct((4,) + _half_shape(p.shape[1:], b), p.dtype) for p, b in zip(ps, by)], nw, copies)


def _sum_blocks(half_shape, by):
    rh, ch = half_shape
    if by == ROWS:
        lanes = -(-ch // LANES) * LANES
        bt = _pick(rh, max(16, (3 << 18) // lanes // 16 * 16), 16)
        return (bt, ch), rh // bt, lambda i: (i, 0)
    bc = _pick(ch, max(LANES, (5 << 18) // rh // LANES * LANES), LANES)
    return (rh, bc), ch // bc, lambda i: (0, i)


def _pair_sum(p, got, c_idx, name, by=ROWS):
    hs = got.shape[1:]
    blk, nb, pos = _sum_blocks(hs, by)

    def body(c_ref, a, b, of, ob):
        s = a[...] + b[...]
        of[...] = s
        ob[...] = s.astype(BF16)

    def mine(j, i, c_ref):
        r, cc = pos(c_ref[0] * nb + i)
        return (j, r, cc)

    spec = pl.BlockSpec((None,) + blk, lambda j, i, c_ref: (j,) + pos(i))
    return pl.pallas_call(
        body, name=name,
        grid_spec=pltpu.PrefetchScalarGridSpec(
            num_scalar_prefetch=1, grid=(4, nb),
            in_specs=[pl.BlockSpec((None,) + blk, mine), spec], out_specs=[spec, spec]),
        out_shape=[jax.ShapeDtypeStruct((4,) + hs, F32), jax.ShapeDtypeStruct((4,) + hs, BF16)],
        compiler_params=_params(("parallel", "parallel")),
    )(c_idx, p, got)


def _chip_scatter(qbs):
    nw = len(qbs)

    def copies(srcs, outs, send_sems, recv_sems):
        x, y, c = _place()
        return [_rcopy(srcs[w].at[2 * px + py], outs[w].at[k], send_sems.at[3 * w + k], recv_sems.at[3 * w + k], (px, py, c))
                for w in range(nw) for k, (px, py) in enumerate(_other_chips(x, y))]

    return _Stage(qbs, [jax.ShapeDtypeStruct((3,) + q.shape[1:], q.dtype) for q in qbs], 3 * nw, copies)


def _final_sum(qf, got, me_idx, name, by=ROWS):
    hs = qf.shape[1:]
    blk, nb, pos = _sum_blocks(hs, by)

    def body(me_ref, a, b, o):
        o[...] = ((a[...] + b[0].astype(F32)) + b[1].astype(F32)) + b[2].astype(F32)

    return pl.pallas_call(
        body, name=name,
        grid_spec=pltpu.PrefetchScalarGridSpec(
            num_scalar_prefetch=1, grid=(nb,),
            in_specs=[pl.BlockSpec((None,) + blk, lambda i, me_ref: (me_ref[0],) + pos(i)),
                      pl.BlockSpec((3,) + blk, lambda i, me_ref: (0,) + pos(i))],
            out_specs=pl.BlockSpec(blk, lambda i, me_ref: pos(i))),
        out_shape=jax.ShapeDtypeStruct(hs, F32),
        compiler_params=_params(("parallel",)),
    )(me_idx, qf, got)


def _pair_allgather(halves, by):
    nw = len(halves)
    whole = [(2 * h.shape[0], h.shape[1]) if b == ROWS else h.shape for h, b in zip(halves, by)]

    def copies(srcs, outs, send_sems, recv_sems):
        x, y, c = _place()
        there = lambda w: _half(outs[w], c, ROWS) if by[w] == ROWS else outs[w]
        return [_rcopy(srcs[w], there(w), send_sems.at[w], recv_sems.at[w], (x, y, 1 - c)) for w in range(nw)]

    return _Stage(halves, [jax.ShapeDtypeStruct(s, h.dtype) for s, h in zip(whole, halves)], nw, copies)


class _SemaphoreWindow:
    def __init__(self, ref, off):
        self.ref, self.off = ref, off

    @property
    def at(self):
        return self

    def __getitem__(self, i):
        return self.ref.at[self.off + i]


def _both(a, b):
    na, ma = len(a.inputs), len(a.out_shapes)

    def copies(ins, outs, send_sems, recv_sems):
        return (a.copies(ins[:na], outs[:ma], send_sems, recv_sems) +
                b.copies(ins[na:], outs[ma:], _SemaphoreWindow(send_sems, a.n_sems), _SemaphoreWindow(recv_sems, a.n_sems)))

    return _Stage(a.inputs + b.inputs, a.out_shapes + b.out_shapes, a.n_sems + b.n_sems, copies,
                  aliases={**a.aliases, **{na + i: ma + o for i, o in b.aliases.items()}})


def _small_exchange(items, out_shapes, finish, name):
    n = len(items)
    offs, rows = [], 0
    for it in items:
        offs.append(rows)
        rows += it.shape[0]
    rows = -(-rows // SUBLANES) * SUBLANES
    width = -(-max(it.shape[1] for it in items) // LANES) * LANES
    VMEM = pl.BlockSpec(memory_space=pltpu.VMEM)

    def body(*refs):
        ins, outs = refs[:n], refs[n:n + len(out_shapes)]
        buf, send_sems, recv_sems = refs[n + len(out_shapes):]
        x, y, c = _place()
        me = 4 * x + 2 * y + c
        flip = lambda v, f: (1 - v) if f else v
        peers = [(flip(x, r >> 2 & 1), flip(y, r >> 1 & 1), flip(c, r & 1)) for r in range(1, 8)]
        buf[me] = jnp.zeros((rows, width), F32)
        for it, off, ref in zip(items, offs, ins):
            buf[me, off:off + it.shape[0], 0:it.shape[1]] = ref[...]
        cps = [_rcopy(buf.at[me], buf.at[me], send_sems.at[k], recv_sems.at[k], dev) for k, dev in enumerate(peers)]
        for cp in cps:
            cp.start()
        for k, (px, py, pc) in enumerate(peers):
            slot = buf.at[4 * px + 2 * py + pc]
            _rcopy(slot, slot, send_sems.at[k], recv_sems.at[k], (px, py, pc)).wait_recv()
        for cp in cps:
            cp.wait_send()
        finish(buf, offs, outs)

    return pl.pallas_call(
        body, name=name, in_specs=[VMEM] * n, out_specs=[VMEM] * len(out_shapes),
        out_shape=[jax.ShapeDtypeStruct(s, F32) for s in out_shapes],
        scratch_shapes=[pltpu.VMEM((8, rows, width), F32), pltpu.SemaphoreType.DMA((7,)), pltpu.SemaphoreType.DMA((7,))],
        compiler_params=pltpu.CompilerParams(vmem_limit_bytes=VMEM_LIMIT_BYTES),
    )(*items)


def _allreduce_small(items, name):
    def finish(buf, offs, outs):
        for it, off, out in zip(items, offs, outs):
            region = lambda d: buf[d, off:off + it.shape[0], 0:it.shape[1]]
            s = region(0)
            for d in range(1, 8):
                s = s + region(d)
            out[...] = s
    return _small_exchange(items, [it.shape for it in items], finish, name)


def _allgather_small_shards(items, name):
    def finish(buf, offs, outs):
        for it, off, out in zip(items, offs, outs):
            r, c = it.shape
            for j in range(4):
                out[:, j * c:(j + 1) * c] = buf[2 * j, off:off + r, 0:c]
    return _small_exchange(items, [(it.shape[0], 4 * it.shape[1]) for it in items], finish, name)


def _split_w_in(wt, D):
    pad = jnp.zeros((ZS - 3 * LOWRANK, wt.shape[1]), wt.dtype)
    big = jnp.concatenate([wt[:3 * D], wt[3 * D + 16:6 * D + 16], wt[6 * D + 16:7 * D + 16], wt[7 * D + 48:]], axis=0)
    small = jnp.concatenate([wt[3 * D:3 * D + 16], wt[7 * D + 16:7 * D + 48], pad], axis=0)
    return big, small


def _join_w_in(gb, gs, D):
    return jnp.concatenate([gb[:3 * D], gs[:16], gb[3 * D:6 * D], gb[6 * D:7 * D], gs[16:48], gb[7 * D:9 * D]], axis=0)


def kernel(x, p, g_mix, w_in, gla_w2, gla_b, gla_norm, dn_conv, dn_a_log, dn_dt_bias, dn_norm, w_out, g_mlp, w_up, w_down, g_ple, w_ple_gate, w_ple_proj, g_final, loss_target, m_g_mix, m_w_in, m_gla_w2, m_gla_b, m_gla_norm, m_dn_conv, m_dn_a_log, m_dn_dt_bias, m_dn_norm, m_w_out, m_g_mlp, m_w_up, m_w_down, m_g_ple, m_w_ple_gate, m_w_ple_proj, m_g_final, v_g_mix, v_w_in, v_gla_w2, v_gla_b, v_gla_norm, v_dn_conv, v_dn_a_log, v_dn_dt_bias, v_dn_norm, v_w_out, v_g_mlp, v_w_up, v_w_down, v_g_ple, v_w_ple_gate, v_w_ple_proj, v_g_final):
    wts = dict(zip(WEIGHTS, [g_mix, w_in, gla_w2, gla_b, gla_norm, dn_conv, dn_a_log, dn_dt_bias, dn_norm, w_out, g_mlp,
                             w_up, w_down, g_ple, w_ple_gate, w_ple_proj, g_final]))
    mom = dict(zip(WEIGHTS, [m_g_mix, m_w_in, m_gla_w2, m_gla_b, m_gla_norm, m_dn_conv, m_dn_a_log, m_dn_dt_bias, m_dn_norm,
                             m_w_out, m_g_mlp, m_w_up, m_w_down, m_g_ple, m_w_ple_gate, m_w_ple_proj, m_g_final]))
    var = dict(zip(WEIGHTS, [v_g_mix, v_w_in, v_gla_w2, v_gla_b, v_gla_norm, v_dn_conv, v_dn_a_log, v_dn_dt_bias, v_dn_norm,
                             v_w_out, v_g_mlp, v_w_up, v_w_down, v_g_ple, v_w_ple_gate, v_w_ple_proj, v_g_final]))
    Bl, S, D = x.shape
    T = Bl * S
    PLE = p.shape[-1]
    dn_d, gla_dk = D // DN_HEADS, D // (2 * GLA_HEADS)
    ix, iy, ic = _place()
    j_me = 2 * ix + iy
    as2d = lambda a: a.reshape(a.shape[-2], a.shape[-1]) if a.ndim > 1 else a.reshape(1, -1)
    c_idx, me_idx = ic.reshape(1).astype(jnp.int32), j_me.reshape(1).astype(jnp.int32)

    rows_first = lambda a: jnp.transpose(a, (2, 0, 1))
    cols_last = lambda a: jnp.transpose(a, (1, 2, 0))
    w_in_t, m_in_t, v_in_t = rows_first(w_in), rows_first(m_w_in), rows_first(v_w_in)
    n_in = w_in_t.shape[0]
    shard2d = {n: as2d(wts[n]) for n, _ in BIG[1:]}
    bf16_shards = [w_in_t.astype(BF16).reshape(n_in, D)] + [shard2d[n].astype(BF16) for n, _ in BIG[1:]]
    split = [COLS] + [ROWS] * (len(BIG) - 1)
    own_slot = lambda g, s: lax.dynamic_update_slice(g, s[None], (j_me, 0, 0))
    xt = x.reshape(T, D)
    (w_in_ici,) = _run_stage(_gather_ici(bf16_shards[:1], split[:1]), "allgather_w_in_ici")
    h, w_in_all = _rmsnorm_fwd(xt, g_mix, "rms1_fwd", stage=_gather_pass([w_in_ici], split[:1]))
    w_in_slots = own_slot(w_in_all, bf16_shards[0])
    w_big, w_small = _split_w_in(w_in_slots.reshape(4 * n_in, D), D)

    w2_full, conv_full = _allgather_small_shards([as2d(gla_w2), as2d(dn_conv)], "allgather_small_weights")
    w2pad = jnp.pad(w2_full, ((0, ZS - LOWRANK), (0, 0)))
    w2h = jnp.swapaxes(w2pad.reshape(ZS, GLA_HEADS, gla_dk), 0, 1)
    gbh = gla_b.reshape(GLA_HEADS, 1, gla_dk)
    alog_w = jnp.broadcast_to(dn_a_log.reshape(DN_HEADS, 1, 1), (DN_HEADS, 1, dn_d))
    dtb_w = jnp.broadcast_to(dn_dt_bias.reshape(DN_HEADS, 1, 1), (DN_HEADS, 1, dn_d))

    tgt = loss_target.reshape(T, D)
    pt = p.reshape(T, PLE)
    seq = lambda t: t.reshape(Bl, S, t.shape[-1])
    tok = lambda t: t.reshape(T, t.shape[-1])
    first, second = [1, 2, 5], [3, 4]
    sh, sp = (lambda idx: [bf16_shards[i] for i in idx]), (lambda idx: [split[i] for i in idx])
    z_big, *first_ici = _matmul(h, w_big, 'nt', [F32], "proj_in", stage=_gather_ici(sh(first), sp(first)))
    (z_small,) = _matmul(h, w_small, 'nt', [F32], "proj_in_narrow")
    o_gla, st_all, *first_all = _gla_fwd(seq(z_big), seq(z_small), w2h, gbh, Bl, S, D, stage=_gather_pass(first_ici, sp(first)))
    acts = [_conv_fwd(z_big, conv_full, grp, Bl, S, D) for grp in range(3)]
    o_dn, s_all, *second_ici = _dn_fwd(seq(acts[0]), seq(acts[1]), seq(acts[2]), seq(z_small), alog_w, dtb_w, Bl, S, D,
                                       stage=_gather_ici(sh(second), sp(second)))
    mixed, *second_all = _merge_fwd(tok(o_gla), tok(o_dn), z_big, gla_norm, dn_norm, D,
                                    stage=_gather_pass(second_ici, sp(second)))
    slots = {BIG[i][0]: own_slot(g, bf16_shards[i]) for i, g in zip(first + second, first_all + second_all)}
    rows_joined = lambda t: t.reshape(4 * t.shape[1], t.shape[2])
    w_out_f, w_down_f, w_pg_f = rows_joined(slots['w_out']), rows_joined(slots['w_down']), rows_joined(slots['w_ple_gate'])
    w_up_s, w_pp_s = slots['w_up'], slots['w_ple_proj']
    (x1,) = _matmul(mixed, w_out_f, 'nn', [F32], "proj_out", epilogue=lambda r, e: (e + r,), extras=(xt,), bm=512)
    (h2,) = _rmsnorm_fwd(x1, g_mlp, "rms2_fwd")
    u, act = _matmul(h2, w_up_s, 'nn', [F32, BF16], "mlp_up", b_slots=True,
                     epilogue=lambda r: (r, jnp.square(jnp.maximum(r, 0.0))))
    (x2,) = _matmul(act, w_down_f, 'nn', [F32], "mlp_down", epilogue=lambda r, e: (e + r,), extras=(x1,), bm=512)
    (h3,) = _rmsnorm_fwd(x2, g_ple, "rms3_fwd")
    (pp,) = _matmul(pt, w_pp_s, 'nn', [F32], "ple_proj", b_slots=True)
    gp, x3 = _matmul(h3, w_pg_f, 'nn', [F32, F32], "ple_gate",
                     epilogue=lambda r, e, q: (r, e + _sigmoid(r) * q), extras=(x2, pp), bm=512)
    dx3, loss_tile, d_g_final = _loss_fwd_bwd(x3, g_final.reshape(1, D), tgt, "loss")

    d_gp, d_pp = _ple_bwd(dx3, gp, pp, "ple_bwd")
    (g_pp,) = _matmul(pt, d_pp, 'tn', [F32], "ple_proj_dw", out_slots=True)
    (g_pg,) = _matmul(h3, d_gp, 'tn', [F32], "ple_gate_dw")
    (dh3,) = _matmul(d_gp, w_pg_f, 'nt', [F32], "ple_gate_dx")
    dx2, dx2b, d_g_ple = _rmsnorm_bwd_add(x2, g_ple, dh3, dx3, "rms3_bwd")
    (g_down,) = _matmul(act, dx2b, 'tn', [F32], "mlp_down_dw")
    (du,) = _matmul(dx2b, w_down_f, 'nt', [BF16], "mlp_down_dx",
                    epilogue=lambda r, e: (r * 2.0 * jnp.maximum(e, 0.0),), extras=(u,))
    (g_up,) = _matmul(h2, du, 'tn', [F32], "mlp_up_dw", out_slots=True)
    by_rows = lambda g: g.reshape(4, g.shape[0] // 4, g.shape[1])
    send_mlp = [g_up, by_rows(g_down), by_rows(g_pg), g_pp]
    dh2, *sib_mlp = _matmul(du, w_up_s, 'nt', [F32], "mlp_up_dx", b_slots=True, stage=_pair_exchange(send_mlp, split[2:]))
    dx1, dx1b, d_g_mlp = _rmsnorm_bwd_add(x1, g_mlp, dh2, dx2, "rms2_bwd")
    (g_out,) = _matmul(mixed, dx1b, 'tn', [F32], "proj_out_dw")
    dmix, sib_out = _matmul(dx1b, w_out_f, 'nt', [F32], "proj_out_dx", stage=_pair_exchange([by_rows(g_out)], split[1:2]))
    rest = [n for n, _ in BIG[1:]]
    send_rest, sib_rest = [by_rows(g_out)] + send_mlp, [sib_out] + sib_mlp
    sums_rest = [_pair_sum(s, f, c_idx, f"grad_pair_sum_{n}") for n, s, f in zip(rest, send_rest, sib_rest)]
    d_ogla, d_gg, d_odn, d_dz, d_ga, d_gb, d_gla_norm, d_dn_norm = _merge_bwd(
        tok(o_gla), tok(o_dn), z_big, gla_norm, dn_norm, dmix, D)
    d_q, d_k, d_v, dzs_gla, d_w2h, d_gbh = _gla_bwd(seq(z_big), seq(z_small), w2h, gbh, st_all, seq(d_ogla), Bl, S, D)
    d_qa, d_ka, d_va, d_zs, d_alog_w, d_dtb_w, *chips_rest = _dn_bwd(
        seq(acts[0]), seq(acts[1]), seq(acts[2]), seq(z_small), alog_w, dtb_w, s_all, seq(d_odn), dzs_gla, Bl, S, D,
        stage=_chip_scatter([b for _, b in sums_rest]))
    conv_b = [_conv_bwd(z_big, conv_full, tok(g), grp, Bl, S, D) for grp, g in enumerate([d_qa, d_ka, d_va])]
    dz_big = jnp.concatenate([tok(d_q), tok(d_k), tok(d_v), d_gg, conv_b[0][0], conv_b[1][0], conv_b[2][0], d_dz, d_ga,
                              d_gb], axis=1)
    dz_small = tok(d_zs)
    (d_w_big,) = _matmul(dz_big, h, 'tn', [F32], "proj_in_dw")
    halves_rest = [_final_sum(f, got, me_idx, f"grad_final_sum_{n}") for n, (f, _), got in zip(rest, sums_rest, chips_rest)]
    (d_w_small,) = _matmul(dz_small, h, 'tn', [F32], "proj_in_narrow_dw")
    g_in = _join_w_in(d_w_big, d_w_small, D).reshape(4, n_in, D)
    (sib_in,) = _run_stage(_pair_exchange([g_in], split[:1]), "grad_pair_exchange_w_in")
    sum_in_f32, sum_in_bf16 = _pair_sum(g_in, sib_in, c_idx, "grad_pair_sum_w_in", split[0])
    dh_a, chips_in, *pair_rest = _matmul(dz_big, w_big, 'nn', [F32], "proj_in_dx",
                                         stage=_both(_chip_scatter([sum_in_bf16]), _pair_allgather(halves_rest, split[1:])))
    half_in = _final_sum(sum_in_f32, chips_in, me_idx, "grad_final_sum_w_in", split[0])
    (dh,) = _matmul(dz_small, w_small, 'nn', [F32], "proj_in_narrow_dx", epilogue=lambda r, e: (e + r,), extras=(dh_a,))
    grad_x, _, d_g_mix = _rmsnorm_bwd_add(xt, g_mix, dh, dx1, "rms1_bwd")
    (pair_in,) = _run_stage(_pair_allgather([half_in], split[:1]), "grad_pair_allgather_w_in")
    reduced = {n: lax.dynamic_update_slice(o, hlf, (ic * hlf.shape[0], 0)) for n, o, hlf in zip(rest, pair_rest, halves_rest)}
    south = ic == 0
    g_in_t = jnp.concatenate([jnp.where(south, half_in, pair_in), jnp.where(south, pair_in, half_in)],
                             axis=1).reshape(n_in, 1, D)

    d_w2 = jnp.swapaxes(d_w2h, 0, 1).reshape(ZS, D // 2)[:LOWRANK]
    small_grads = {'g_mix': d_g_mix, 'gla_w2': d_w2, 'gla_b': d_gbh.reshape(1, D // 2), 'gla_norm': d_gla_norm,
                   'dn_a_log': d_alog_w[:, 0, 0].reshape(1, DN_HEADS), 'dn_dt_bias': d_dtb_w[:, 0, 0].reshape(1, DN_HEADS),
                   'dn_norm': d_dn_norm, 'g_mlp': d_g_mlp, 'g_ple': d_g_ple, 'g_final': d_g_final}
    names = [n for n in SMALL if n != 'dn_conv']
    total = _allreduce_small([small_grads[n] for n in names] + [cb[1] for cb in conv_b] + [loss_tile[:1]],
                             "allreduce_small_grads")
    gsmall = dict(zip(names, total[:len(names)]))
    loss = total[-1][0, 0]
    my_cols = lambda g: lax.dynamic_slice_in_dim(g, j_me * (g.shape[1] // 4), g.shape[1] // 4, axis=1)
    gsmall['gla_w2'] = my_cols(gsmall['gla_w2'])
    gsmall['dn_conv'] = my_cols(jnp.concatenate(total[len(names):len(names) + 3], axis=1))

    g_o, d_o, m_o, v_o = {}, {}, {}, {}
    d_in_t, nm_in_t, nv_in_t, g_out_t = _adamw(w_in_t, g_in_t, m_in_t, v_in_t, "adamw_w_in", with_grad=True)
    g_o['w_in'], d_o['w_in'], m_o['w_in'], v_o['w_in'] = [cols_last(t) for t in (g_out_t, d_in_t, nm_in_t, nv_in_t)]
    for n, _ in BIG[1:]:
        shp = wts[n].shape
        d2, nm2, nv2 = _adamw(shard2d[n], reduced[n], as2d(mom[n]), as2d(var[n]), f"adamw_{n}")
        g_o[n], d_o[n], m_o[n], v_o[n] = reduced[n].reshape(shp), d2.reshape(shp), nm2.reshape(shp), nv2.reshape(shp)
    ds, nms, nvs = _adamw_small([as2d(wts[n]) for n in SMALL], [as2d(gsmall[n]) for n in SMALL],
                                [as2d(mom[n]) for n in SMALL], [as2d(var[n]) for n in SMALL])
    for n, dd, mm, vv in zip(SMALL, ds, nms, nvs):
        shp = wts[n].shape
        g_o[n], d_o[n], m_o[n], v_o[n] = gsmall[n].reshape(shp), dd.reshape(shp), mm.reshape(shp), vv.reshape(shp)

    return (loss, grad_x.reshape(Bl, S, D), *[g_o[n] for n in WEIGHTS], *[d_o[n] for n in WEIGHTS],
            *[m_o[n] for n in WEIGHTS], *[v_o[n] for n in WEIGHTS])
```

```python
import functools

import jax
import jax.numpy as jnp
from jax import lax
from jax.experimental import pallas as pl
from jax.experimental.pallas import tpu as pltpu

F32 = jnp.float32
BF16 = jnp.bfloat16

CHUNK = 64
GLA_HEADS = 4
DN_HEADS = 16
LOWRANK = 16
GLA_TAU = 16.0
DN_CONV = 4
EPS = 1e-6
ZS = 128
A_LANE, B_LANE = LOWRANK, LOWRANK + DN_HEADS
ADAM_LR, ADAM_B1, ADAM_B2, ADAM_EPS, ADAM_WD, ADAM_STEP = 0.001, 0.9, 0.999, 1e-08, 0.01, 10

V7X_VMEM_BYTES = 64 * 1024 * 1024
VMEM_LIMIT_BYTES = V7X_VMEM_BYTES - 8 * 1024 * 1024
LANES = 128
SUBLANES = 8
MESH = pl.DeviceIdType.MESH
DN_HEADS_PER_STEP = 16
GLA_HEADS_PER_STEP = 4

WEIGHTS = ['g_mix', 'w_in', 'gla_w2', 'gla_b', 'gla_norm', 'dn_conv', 'dn_a_log', 'dn_dt_bias', 'dn_norm', 'w_out',
           'g_mlp', 'w_up', 'w_down', 'g_ple', 'w_ple_gate', 'w_ple_proj', 'g_final']
BIG = [('w_in', 1), ('w_out', 0), ('w_up', 1), ('w_down', 0), ('w_ple_gate', 0), ('w_ple_proj', 1)]
SMALL = [n for n in WEIGHTS if n not in dict(BIG)]

_NN, _NT, _TN = 'nn', 'nt', 'tn'


def _params(sem=None):
    return pltpu.CompilerParams(dimension_semantics=sem, vmem_limit_bytes=VMEM_LIMIT_BYTES)


def _dot(a, b, form, precision=None):
    o = a.ndim - 2
    contract = {_NN: ((1 + o,), (o,)), _NT: ((1 + o,), (1 + o,)), _TN: ((o,), (o,))}[form]
    batch = ((0,), (0,)) if o else ((), ())
    return lax.dot_general(a, b, (contract, batch), precision=precision, preferred_element_type=F32)


def _make_mm(cast, precision):
    def raw(a, b, dims):
        return _dot(cast(a), cast(b), dims, precision)

    @jax.custom_vjp
    def nn(a, b):
        return raw(a, b, _NN)
    nn.defvjp(lambda a, b: (raw(a, b, _NN), (a, b)), lambda r, g: (raw(g, r[1], _NT), raw(r[0], g, _TN)))

    @jax.custom_vjp
    def nt(a, b):
        return raw(a, b, _NT)
    nt.defvjp(lambda a, b: (raw(a, b, _NT), (a, b)), lambda r, g: (raw(g, r[1], _NN), raw(g, r[0], _TN)))

    @jax.custom_vjp
    def tn(a, b):
        return raw(a, b, _TN)
    tn.defvjp(lambda a, b: (raw(a, b, _TN), (a, b)), lambda r, g: (raw(r[1], g, _NT), raw(r[0], g, _NN)))
    return nn, nt, tn


_bnn, _bnt, _btn = _make_mm(lambda t: t.astype(BF16), None)
TRI_PRECISION = lax.Precision.HIGH


def _iota2(n, axis):
    return lax.broadcasted_iota(jnp.int32, (n, n), axis)


def _lower(n, strict=False):
    return (_iota2(n, 0) > _iota2(n, 1)) if strict else (_iota2(n, 0) >= _iota2(n, 1))


def _tri_times(tri, x):
    tri = tri.astype(F32)
    if x.ndim == 3:
        tri = jnp.broadcast_to(tri, (x.shape[0],) + tri.shape)
    return _dot(tri, x, _NN, lax.Precision.HIGH)


@jax.custom_vjp
def _cumsum_rows(x):
    return _tri_times(_lower(x.shape[-2]), x)


def _cumsum_rows_bwd(_, g):
    n = g.shape[-2]
    return (_tri_times(_iota2(n, 0) <= _iota2(n, 1), g),)


_cumsum_rows.defvjp(lambda x: (_cumsum_rows(x), None), _cumsum_rows_bwd)


def _tri_inv_impl(a):
    n = a.shape[-1]
    eye = (_iota2(n, 0) == _iota2(n, 1)).astype(F32)
    p = eye - a
    ak = a
    k = 2
    while k < n:
        prec, cast = (TRI_PRECISION, lambda t: t) if k == 2 else (None, lambda t: t.astype(BF16))
        ak = _dot(cast(ak), cast(ak), _NN, prec)
        p = p + _dot(cast(p), cast(ak), _NN, prec)
        k *= 2
    return p


@jax.custom_vjp
def _tri_inv(a):
    return _tri_inv_impl(a)


def _tri_inv_fwd(a):
    t = _tri_inv_impl(a)
    return t, t


def _tri_inv_bwd(t, g):
    tb = t.astype(BF16)
    tg = _dot(tb, g.astype(BF16), _TN)
    return (-_dot(tg.astype(BF16), tb, _NT),)


_tri_inv.defvjp(_tri_inv_fwd, _tri_inv_bwd)


def _shift_rows(x, s, down):
    n = x.shape[0]
    r = lax.broadcasted_iota(jnp.int32, x.shape, 0)
    if down:
        return jnp.where(r >= s, pltpu.roll(x, s, 0), 0.0)
    return jnp.where(r < n - s, pltpu.roll(x, n - s, 0), 0.0)


def _make_shift(s):
    @jax.custom_vjp
    def f(x):
        return _shift_rows(x, s, True)
    f.defvjp(lambda x: (_shift_rows(x, s, True), None), lambda _, g: (_shift_rows(g, s, False),))
    return f


def _sigmoid(x):
    return jax.nn.sigmoid(x)


def _silu(x):
    return x * jax.nn.sigmoid(x)


def _softplus(x):
    return jnp.maximum(x, 0.0) + jnp.log1p(jnp.exp(-jnp.abs(x)))


def _log_sigmoid(x):
    return -_softplus(-x)


def _rms(x, g):
    return x * lax.rsqrt(jnp.mean(x * x, axis=-1, keepdims=True) + EPS) * g


def _gla_chunk(q, k, v, zs, w2, gb, st, *, scale):
    c = q.shape[-2]
    logf = _log_sigmoid(_bnn(zs, w2) + gb) * (1.0 / GLA_TAU)
    bcum = _cumsum_rows(logf)
    b_last = jnp.sum(logf, axis=-2, keepdims=True)
    q_in = (q * scale) * jnp.exp(bcum)
    k_in = k * jnp.exp(-bcum)
    a = jnp.where(_lower(c), _bnt(q_in, k_in), 0.0)
    o = _bnn(a, v) + _bnt(q_in, st)
    k_dec = k * jnp.exp(b_last - bcum)
    st_new = st * jnp.exp(b_last) + _btn(v, k_dec)
    return o, st_new


def _dn_chunk(q, k, v, aw, bw, alog, dtb, s):
    c = q.shape[-2]
    incl, strict = _lower(c), _lower(c, True)
    g_w = -jnp.exp(alog) * _softplus(aw + dtb)
    beta_w = _sigmoid(bw)
    gcum_w = _cumsum_rows(g_w)
    lane0 = lax.broadcasted_iota(jnp.int32, gcum_w.shape, gcum_w.ndim - 1) == 0
    gcol = jnp.sum(jnp.where(lane0, gcum_w, 0.0), axis=-1, keepdims=True)
    d1 = jnp.broadcast_to(gcol, gcol.shape[:-1] + (c,))
    diff = jnp.where(incl, d1 - jnp.swapaxes(d1, -1, -2), 0.0)
    decay = jnp.where(incl, jnp.exp(diff), 0.0)
    k_beta = k * beta_w
    a = jnp.where(strict, _bnt(k_beta, k) * decay, 0.0)
    t = _tri_inv(a)
    egc = jnp.exp(gcum_w)
    u = _bnn(t, v * beta_w)
    w = _bnn(t, k_beta * egc)
    attn = jnp.where(incl, _bnt(q, k) * decay, 0.0)
    q_dec = q * egc
    g_last = jnp.sum(g_w, axis=-2, keepdims=True)
    k_dec = k * jnp.exp(g_last - gcum_w)
    v_new = u - _bnn(w, s)
    o = _bnn(q_dec, s) + _bnn(attn, v_new)
    s_new = s * jnp.exp(g_last) + _btn(k_dec, v_new)
    return o, s_new


def _conv_act(x, wrows, *, l2, scale):
    taps = len(wrows)
    y = None
    for j in range(taps):
        s = taps - 1 - j
        xs = x if s == 0 else _make_shift(s)(x)
        y = wrows[j] * xs if y is None else y + wrows[j] * xs
    y = _silu(y)
    if l2:
        y = y * lax.rsqrt(jnp.sum(y * y, axis=-1, keepdims=True) + EPS) * scale
    return y


def _merge_math(og, gg, od, dz, ga, gb, gn, dn):
    nsub = len(og)
    dv = nsub * og[0].shape[1]
    ssq = jnp.sum(og[0] * og[0], axis=-1, keepdims=True)
    for s in range(1, nsub):
        ssq = ssq + jnp.sum(og[s] * og[s], axis=-1, keepdims=True)
    r = lax.rsqrt(ssq * (1.0 / dv) + EPS)
    outs = []
    for s in range(nsub):
        a = og[s] * r * gn[s] * _silu(gg[s])
        b = _rms(od[s], dn) * _silu(dz[s])
        outs.append(_sigmoid(ga[s]) * a + _sigmoid(gb[s]) * b)
    return outs


def _pick(n, target, mult):
    best = None
    for d in range(mult, min(n, target) + 1, mult):
        if n % d == 0:
            best = d
    return best if best is not None else n


class _Stage:
    def __init__(self, inputs, out_shapes, n_sems, copies, aliases=None):
        self.inputs, self.out_shapes, self.n_sems, self.copies = list(inputs), list(out_shapes), n_sems, copies
        self.aliases = aliases or {}

    @property
    def sems(self):
        return [pltpu.SemaphoreType.DMA((self.n_sems,)), pltpu.SemaphoreType.DMA((self.n_sems,))]


def _host_stage(body, stage, n_in, n_out, grid):
    ci, co = len(stage.inputs), len(stage.out_shapes)

    def wrapped(*refs):
        ins, cins = refs[:n_in], refs[n_in:n_in + ci]
        outs, couts = refs[n_in + ci:n_in + ci + n_out], refs[n_in + ci + n_out:n_in + ci + n_out + co]
        scratch, sems = refs[n_in + ci + n_out + co:-2], refs[-2:]
        ids = [pl.program_id(d) for d in range(len(grid))]
        first, last = ids[0] == 0, ids[0] == grid[0] - 1
        for i, g in zip(ids[1:], grid[1:]):
            first, last = first & (i == 0), last & (i == g - 1)

        @pl.when(first)
        def _():
            for cp in stage.copies(cins, couts, *sems):
                cp.start()

        body(*ins, *outs, *scratch)

        @pl.when(last)
        def _():
            for cp in stage.copies(cins, couts, *sems):
                cp.wait()

    return wrapped


def _call(body, name, grid, in_specs, out_specs, out_shape, scratch, semantics, args, stage=None):
    if stage is None:
        return pl.pallas_call(body, name=name, grid=grid, in_specs=list(in_specs), out_specs=list(out_specs),
                              out_shape=list(out_shape), scratch_shapes=list(scratch), compiler_params=_params(semantics))(*args)
    n_in, n_out = len(in_specs), len(out_specs)
    return pl.pallas_call(
        _host_stage(body, stage, n_in, n_out, grid), name=name, grid=grid,
        in_specs=list(in_specs) + [ANY] * len(stage.inputs),
        out_specs=list(out_specs) + [ANY] * len(stage.out_shapes), out_shape=list(out_shape) + stage.out_shapes,
        scratch_shapes=list(scratch) + stage.sems,
        input_output_aliases={n_in + i: n_out + o for i, o in stage.aliases.items()},
        compiler_params=_params(("arbitrary",) * len(grid)),
    )(*args, *stage.inputs)


def _run_stage(stage, name):
    ci = len(stage.inputs)

    def body(*refs):
        cps = stage.copies(refs[:ci], refs[ci:-2], *refs[-2:])
        for cp in cps:
            cp.start()
        for cp in cps:
            cp.wait()

    return pl.pallas_call(body, name=name, in_specs=[ANY] * ci, out_specs=[ANY] * len(stage.out_shapes),
                          out_shape=stage.out_shapes, scratch_shapes=stage.sems,
                          input_output_aliases=dict(stage.aliases))(*stage.inputs)


def _matmul(a, b, form, out_dtypes, name, epilogue=None, extras=(), bm=1024, bn=1024, bk=2048,
            b_slots=False, out_slots=False, stage=None):
    ns, c = (b.shape[0], b.shape[2]) if b_slots else (1, None)
    b2 = b.shape[1:] if b_slots else b.shape
    if form == 'nn':
        (M, K), (K2, N) = a.shape, (b2[0], b2[1] * ns)
    elif form == 'nt':
        (M, K), (N, K2) = a.shape, (b2[0], b2[1] * ns)
    else:
        (K, M), (K2, N) = a.shape, b2
    assert K == K2 and not (b_slots and form == 'tn'), (a.shape, b.shape, form)
    bm, bn, bk = _pick(M, bm, SUBLANES), _pick(N, bn, LANES), _pick(K, bk, LANES)
    if b_slots:
        bn, bk = (_pick(c, bn, LANES), bk) if form == 'nn' else (bn, _pick(c, bk, LANES))
    if out_slots:
        oc = N // 4
        bn = _pick(oc, bn, LANES)
    nk = K // bk
    a_spec = pl.BlockSpec((bk, bm), lambda i, j, k: (k, i)) if form == 'tn' else pl.BlockSpec((bm, bk), lambda i, j, k: (i, k))
    if b_slots and form == 'nn':
        per = c // bn
        b_spec = pl.BlockSpec((None, bk, bn), lambda i, j, k: (j // per, k, j % per))
    elif b_slots:
        per = c // bk
        b_spec = pl.BlockSpec((None, bn, bk), lambda i, j, k: (k // per, j, k % per))
    elif form == 'nt':
        b_spec = pl.BlockSpec((bn, bk), lambda i, j, k: (j, k))
    else:
        b_spec = pl.BlockSpec((bk, bn), lambda i, j, k: (k, j))
    o_spec = pl.BlockSpec((bm, bn), lambda i, j, k: (i, j))
    if out_slots:
        oper = oc // bn
        out_spec = pl.BlockSpec((None, bm, bn), lambda i, j, k: (j // oper, i, j % oper))
        out_shape = [jax.ShapeDtypeStruct((4, M, oc), d) for d in out_dtypes]
    else:
        out_spec = o_spec
        out_shape = [jax.ShapeDtypeStruct((M, N), d) for d in out_dtypes]
    ne, no = len(extras), len(out_dtypes)

    def finish(r, extra_refs, out_refs):
        outs = (r,) if epilogue is None else epilogue(r, *[e[...] for e in extra_refs])
        for ref, o in zip(out_refs, outs):
            ref[...] = o.astype(ref.dtype)

    def body_one(a_ref, b_ref, *rest):
        finish(_dot(a_ref[...].astype(BF16), b_ref[...].astype(BF16), form), rest[:ne], rest[ne:ne + no])

    def body_acc(a_ref, b_ref, *rest):
        extra_refs, out_refs, acc = rest[:ne], rest[ne:ne + no], rest[ne + no]
        k = pl.program_id(2)
        part = _dot(a_ref[...].astype(BF16), b_ref[...].astype(BF16), form)

        @pl.when(k == 0)
        def _():
            acc[...] = part

        @pl.when((k > 0) & (k < nk - 1))
        def _():
            acc[...] += part

        @pl.when(k == nk - 1)
        def _():
            finish(acc[...] + part, extra_refs, out_refs)

    return _call(body_one if nk == 1 else body_acc, name, (M // bm, N // bn, nk), [a_spec, b_spec] + [o_spec] * ne,
                 [out_spec] * no, out_shape, [] if nk == 1 else [pltpu.VMEM((bm, bn), F32)],
                 ("parallel", "parallel", "arbitrary"), (a, b, *extras), stage)


def _rowwise(fn, rows, consts, row_outs, acc_outs, name, bt=256, stage=None):
    T = rows[0].shape[0]
    bt = _pick(T, bt, SUBLANES)
    nr, nc, no, na = len(rows), len(consts), len(row_outs), len(acc_outs)

    def body(*refs):
        r_in, c_in = refs[:nr], refs[nr:nr + nc]
        r_out, a_out = refs[nr + nc:nr + nc + no], refs[nr + nc + no:]
        ro, ao = fn([r[...] for r in r_in], [c[...] for c in c_in])
        for ref, o in zip(r_out, ro):
            ref[...] = o.astype(ref.dtype)
        if na:
            @pl.when(pl.program_id(0) == 0)
            def _():
                for ref in a_out:
                    ref[...] = jnp.zeros_like(ref)
            for ref, o in zip(a_out, ao):
                ref[...] += o

    whole = lambda shp: pl.BlockSpec(shp, lambda i: (0,) * len(shp))
    return _call(
        body, name, (T // bt,),
        [pl.BlockSpec((bt, r.shape[1]), lambda i: (i, 0)) for r in rows] + [whole(c.shape) for c in consts],
        [pl.BlockSpec((bt, w), lambda i: (i, 0)) for w, _ in row_outs] + [whole(s) for s in acc_outs],
        [jax.ShapeDtypeStruct((T, w), d) for w, d in row_outs] + [jax.ShapeDtypeStruct(s, F32) for s in acc_outs],
        [], ("arbitrary",), (*rows, *consts), stage)


def _rmsnorm_fwd(x, g, name, stage=None):
    return _rowwise(lambda r, c: ([_rms(r[0], c[0])], []), [x], [g], [(x.shape[1], BF16)], [], name, stage=stage)


def _rmsnorm_bwd_add(x, g, dh, dres, name):
    D = x.shape[1]

    def fn(r, c):
        _, vjp = jax.vjp(_rms, r[0], c[0])
        dx, dg = vjp(r[1])
        dx = dx + r[2]
        return [dx, dx], [dg]
    return _rowwise(fn, [x, dh, dres], [g], [(D, F32), (D, BF16)], [(1, D)], name)


def _loss_fwd_bwd(x3, g, target, name):
    D = x3.shape[1]

    def fn(r, c):
        def row_loss(x, gain):
            err = _rms(x, gain) - r[1]
            return 0.5 * jnp.mean(err * err, axis=-1, keepdims=True)
        lrow, vjp = jax.vjp(row_loss, r[0], c[0])
        dx, dg = vjp(jnp.ones_like(lrow))
        tile = jnp.broadcast_to(jnp.sum(lrow, axis=0, keepdims=True), (SUBLANES, LANES))
        return [dx], [tile, dg]
    return _rowwise(fn, [x3, target], [g], [(D, F32)], [(SUBLANES, LANES), (1, D)], name)


def _ple_bwd(dx3, gp, pp, name):
    D = dx3.shape[1]

    def fn(r, c):
        s = _sigmoid(r[1])
        return [r[0] * r[2] * s * (1.0 - s), r[0] * s], []
    return _rowwise(fn, [dx3, gp, pp], [], [(D, BF16), (D, BF16)], [], name)


def _adamw_math(w, g, m, v):
    nm = ADAM_B1 * m + (1.0 - ADAM_B1) * g
    nv = ADAM_B2 * v + (1.0 - ADAM_B2) * (g * g)
    m_hat = nm / (1.0 - ADAM_B1 ** ADAM_STEP)
    v_hat = nv / (1.0 - ADAM_B2 ** ADAM_STEP)
    return -ADAM_LR * (m_hat / (jnp.sqrt(v_hat) + ADAM_EPS) + ADAM_WD * w), nm, nv


def _adamw(w, g, m, v, name, with_grad=False):
    R, C = w.shape[0], w.shape[-1]
    lanes = -(-C // LANES) * LANES
    if w.ndim == 2:
        bt = _pick(R, max(SUBLANES, (1 << 18) // lanes // SUBLANES * SUBLANES), SUBLANES)
        spec = pl.BlockSpec((bt, C), lambda i: (i, 0))
    else:
        bt = _pick(R, max(1, (1 << 18) // lanes), 1)
        spec = pl.BlockSpec((bt, 1, C), lambda i: (i, 0, 0))

    def body(w_ref, g_ref, m_ref, v_ref, d_ref, nm_ref, nv_ref, *g_out):
        d_ref[...], nm_ref[...], nv_ref[...] = _adamw_math(w_ref[...], g_ref[...], m_ref[...], v_ref[...])
        for ref in g_out:
            ref[...] = g_ref[...]

    n_out = 4 if with_grad else 3
    return pl.pallas_call(
        body, name=name, grid=(R // bt,), in_specs=[spec] * 4, out_specs=[spec] * n_out,
        out_shape=[jax.ShapeDtypeStruct(w.shape, F32)] * n_out, compiler_params=_params(("parallel",)),
    )(w, g, m, v)


def _adamw_small(ws, gs, ms, vs):
    n = len(ws)

    def body(*refs):
        for i in range(n):
            d, nm, nv = _adamw_math(refs[i][...], refs[n + i][...], refs[2 * n + i][...], refs[3 * n + i][...])
            refs[4 * n + i][...], refs[5 * n + i][...], refs[6 * n + i][...] = d, nm, nv

    VMEM = pl.BlockSpec(memory_space=pltpu.VMEM)
    shapes = [jax.ShapeDtypeStruct(w.shape, F32) for w in ws]
    outs = pl.pallas_call(body, name="adamw_small", in_specs=[VMEM] * (4 * n), out_specs=[VMEM] * (3 * n),
                          out_shape=shapes * 3)(*ws, *gs, *ms, *vs)
    return outs[:n], outs[n:2 * n], outs[2 * n:]


def _gla_fwd(z_big, z_small, w2h, gbh, Bl, S, D, stage=None):
    NC, dk, dv, HB = S // CHUNK, D // (2 * GLA_HEADS), D // GLA_HEADS, GLA_HEADS_PER_STEP
    HG = GLA_HEADS // HB
    chains = [(hh, bb) for hh in range(HB) for bb in range(Bl)]
    G = len(chains)
    fn = functools.partial(_gla_chunk, scale=dk ** -0.5)

    def body(q, k, v, z, w2, gb, o_ref, stall_ref, st):
        n, g = pl.program_id(0), pl.program_id(1)

        @pl.when(n == 0)
        def _():
            st[g] = jnp.zeros((G, dv, dk), F32)
        s0 = st[g]
        stall_ref[...] = s0.reshape(HB, Bl, dv, dk)
        qk = lambda r: jnp.stack([r[bb, :, hh * dk:(hh + 1) * dk] for hh, bb in chains])
        o, s_new = fn(qk(q), qk(k), jnp.stack([v[bb, :, hh * dv:(hh + 1) * dv] for hh, bb in chains]),
                      jnp.stack([z[bb] for _, bb in chains]), jnp.stack([w2[hh] for hh, _ in chains]),
                      jnp.stack([gb[hh] for hh, _ in chains]), s0)
        for i, (hh, bb) in enumerate(chains):
            o_ref[bb, :, hh * dv:(hh + 1) * dv] = o[i]
        st[g] = s_new

    return _call(
        body, "gla_fwd", (NC, HG),
        [pl.BlockSpec((Bl, CHUNK, HB * dk), lambda n, g: (0, n, g)),
         pl.BlockSpec((Bl, CHUNK, HB * dk), lambda n, g: (0, n, HG + g)),
         pl.BlockSpec((Bl, CHUNK, HB * dv), lambda n, g: (0, n, HG + g)),
         pl.BlockSpec((Bl, CHUNK, ZS), lambda n, g: (0, n, 0)),
         pl.BlockSpec((HB, ZS, dk), lambda n, g: (g, 0, 0)),
         pl.BlockSpec((HB, 1, dk), lambda n, g: (g, 0, 0))],
        [pl.BlockSpec((Bl, CHUNK, HB * dv), lambda n, g: (0, n, g)),
         pl.BlockSpec((HB, Bl, None, dv, dk), lambda n, g: (g, 0, n, 0, 0))],
        [jax.ShapeDtypeStruct((Bl, S, D), F32), jax.ShapeDtypeStruct((GLA_HEADS, Bl, NC, dv, dk), F32)],
        [pltpu.VMEM((HG, G, dv, dk), F32)], ("arbitrary", "arbitrary"), (z_big, z_big, z_big, z_small, w2h, gbh), stage)


def _gla_bwd(z_big, z_small, w2h, gbh, st_all, do, Bl, S, D):
    NC, dk, dv, HB = S // CHUNK, D // (2 * GLA_HEADS), D // GLA_HEADS, GLA_HEADS_PER_STEP
    HG = GLA_HEADS // HB
    chains = [(hh, bb) for hh in range(HB) for bb in range(Bl)]
    G = len(chains)
    fn = functools.partial(_gla_chunk, scale=dk ** -0.5)

    def body(q, k, v, z, w2, gb, st0, do_ref, dq_ref, dk_ref, dv_ref, dzs_ref, dw2_ref, dgb_ref, dst):
        n, g = pl.program_id(0), pl.program_id(1)

        @pl.when(n == 0)
        def _():
            dst[g] = jnp.zeros((G, dv, dk), F32)

        @pl.when((n == 0) & (g == 0))
        def _():
            dw2_ref[...] = jnp.zeros_like(dw2_ref)
            dgb_ref[...] = jnp.zeros_like(dgb_ref)

        qk = lambda r: jnp.stack([r[bb, :, hh * dk:(hh + 1) * dk] for hh, bb in chains])
        vv = lambda r: jnp.stack([r[bb, :, hh * dv:(hh + 1) * dv] for hh, bb in chains])
        _, vjp = jax.vjp(fn, qk(q), qk(k), vv(v), jnp.stack([z[bb] for _, bb in chains]),
                         jnp.stack([w2[hh] for hh, _ in chains]), jnp.stack([gb[hh] for hh, _ in chains]),
                         st0[...].reshape(G, dv, dk))
        dq, dkk, dvv, dzs, dw2, dgb, dst0 = vjp((vv(do_ref), dst[g]))
        for i, (hh, bb) in enumerate(chains):
            dq_ref[bb, :, hh * dk:(hh + 1) * dk] = dq[i].astype(dq_ref.dtype)
            dk_ref[bb, :, hh * dk:(hh + 1) * dk] = dkk[i].astype(dk_ref.dtype)
            dv_ref[bb, :, hh * dv:(hh + 1) * dv] = dvv[i].astype(dv_ref.dtype)
            dw2_ref[g * HB + hh] += dw2[i]
            dgb_ref[g * HB + hh] += dgb[i]
        for bb in range(Bl):
            tot = sum(dzs[i] for i, (_, b2) in enumerate(chains) if b2 == bb)

            @pl.when(g == 0)
            def _():
                dzs_ref[bb] = tot

            @pl.when(g > 0)
            def _():
                dzs_ref[bb] += tot
        dst[g] = dst0

    rn = lambda n: NC - 1 - n
    return pl.pallas_call(
        body, name="gla_bwd", grid=(NC, HG),
        in_specs=[pl.BlockSpec((Bl, CHUNK, HB * dk), lambda n, g: (0, rn(n), g)),
                  pl.BlockSpec((Bl, CHUNK, HB * dk), lambda n, g: (0, rn(n), HG + g)),
                  pl.BlockSpec((Bl, CHUNK, HB * dv), lambda n, g: (0, rn(n), HG + g)),
                  pl.BlockSpec((Bl, CHUNK, ZS), lambda n, g: (0, rn(n), 0)),
                  pl.BlockSpec((HB, ZS, dk), lambda n, g: (g, 0, 0)),
                  pl.BlockSpec((HB, 1, dk), lambda n, g: (g, 0, 0)),
                  pl.BlockSpec((HB, Bl, None, dv, dk), lambda n, g: (g, 0, rn(n), 0, 0)),
                  pl.BlockSpec((Bl, CHUNK, HB * dv), lambda n, g: (0, rn(n), g))],
        out_specs=[pl.BlockSpec((Bl, CHUNK, HB * dk), lambda n, g: (0, rn(n), g)),
                   pl.BlockSpec((Bl, CHUNK, HB * dk), lambda n, g: (0, rn(n), g)),
                   pl.BlockSpec((Bl, CHUNK, HB * dv), lambda n, g: (0, rn(n), g)),
                   pl.BlockSpec((Bl, CHUNK, ZS), lambda n, g: (0, rn(n), 0)),
                   pl.BlockSpec((GLA_HEADS, ZS, dk), lambda n, g: (0, 0, 0)),
                   pl.BlockSpec((GLA_HEADS, 1, dk), lambda n, g: (0, 0, 0))],
        out_shape=[jax.ShapeDtypeStruct((Bl, S, D // 2), BF16), jax.ShapeDtypeStruct((Bl, S, D // 2), BF16),
                   jax.ShapeDtypeStruct((Bl, S, D), BF16), jax.ShapeDtypeStruct((Bl, S, ZS), F32),
                   jax.ShapeDtypeStruct((GLA_HEADS, ZS, dk), F32), jax.ShapeDtypeStruct((GLA_HEADS, 1, dk), F32)],
        scratch_shapes=[pltpu.VMEM((HG, G, dv, dk), F32)],
        compiler_params=_params(("arbitrary", "arbitrary")),
    )(z_big, z_big, z_big, z_small, w2h, gbh, st_all, do)


def _conv_fwd(z_big, conv_w, grp, Bl, S, D):
    d = D // DN_HEADS
    l2, scale = grp < 2, (d ** -0.5 if grp == 0 else 1.0)
    x_blk0 = (3 * D + grp * D) // d

    def body(x_ref, w_ref, o_ref):
        wrows = [w_ref[j:j + 1, :] for j in range(DN_CONV)]
        o_ref[...] = _conv_act(x_ref[...], wrows, l2=l2, scale=scale)

    return pl.pallas_call(
        body, name=f"conv_fwd{grp}", grid=(Bl, DN_HEADS),
        in_specs=[pl.BlockSpec((S, d), lambda b, j: (b, x_blk0 + j)),
                  pl.BlockSpec((DN_CONV, d), lambda b, j: (0, grp * DN_HEADS + j))],
        out_specs=pl.BlockSpec((S, d), lambda b, j: (b, j)),
        out_shape=jax.ShapeDtypeStruct((Bl * S, D), F32),
        compiler_params=_params(("parallel", "parallel")),
    )(z_big, conv_w)


def _conv_bwd(z_big, conv_w, dact, grp, Bl, S, D):
    d = D // DN_HEADS
    l2, scale = grp < 2, (d ** -0.5 if grp == 0 else 1.0)
    x_blk0 = (3 * D + grp * D) // d

    def body(x_ref, w_ref, g_ref, dx_ref, dw_ref):
        @pl.when(pl.program_id(1) == 0)
        def _():
            dw_ref[...] = jnp.zeros_like(dw_ref)
        wrows = [w_ref[j:j + 1, :] for j in range(DN_CONV)]
        _, vjp = jax.vjp(lambda x, wr: _conv_act(x, wr, l2=l2, scale=scale), x_ref[...], wrows)
        dx, dwr = vjp(g_ref[...])
        dx_ref[...] = dx.astype(dx_ref.dtype)
        for j in range(DN_CONV):
            dw_ref[j:j + 1, :] += dwr[j]

    return pl.pallas_call(
        body, name=f"conv_bwd{grp}", grid=(DN_HEADS, Bl),
        in_specs=[pl.BlockSpec((S, d), lambda j, b: (b, x_blk0 + j)),
                  pl.BlockSpec((DN_CONV, d), lambda j, b: (0, grp * DN_HEADS + j)),
                  pl.BlockSpec((S, d), lambda j, b: (b, j))],
        out_specs=[pl.BlockSpec((S, d), lambda j, b: (b, j)), pl.BlockSpec((DN_CONV, d), lambda j, b: (0, j))],
        out_shape=[jax.ShapeDtypeStruct((Bl * S, D), BF16), jax.ShapeDtypeStruct((DN_CONV, D), F32)],
        compiler_params=_params(("arbitrary", "arbitrary")),
    )(z_big, conv_w, dact)


def _lane_column(zb, lane, width):
    pick = lax.broadcasted_iota(jnp.int32, zb.shape, 1) == lane
    return jnp.broadcast_to(jnp.sum(jnp.where(pick, zb, 0.0), axis=-1, keepdims=True), (zb.shape[0], width))


def _dn_fwd(qa, ka, va, z_small, alog, dtb, Bl, S, D, stage=None):
    NC, d, HB = S // CHUNK, D // DN_HEADS, DN_HEADS_PER_STEP
    HG = DN_HEADS // HB
    chains = [(hh, bb) for hh in range(HB) for bb in range(Bl)]
    G = len(chains)

    def body(q, k, v, z, al, dt, o_ref, sall_ref, st):
        n, g = pl.program_id(0), pl.program_id(1)

        @pl.when(n == 0)
        def _():
            st[g] = jnp.zeros((G, d, d), F32)
        tok_in = lambda r: jnp.stack([r[bb, :, hh * d:(hh + 1) * d] for hh, bb in chains])
        head_in = lambda r: jnp.stack([r[hh] for hh, _ in chains])
        gate_in = lambda lane0: jnp.stack([_lane_column(z[bb], lane0 + g * HB + hh, d) for hh, bb in chains])
        s0 = st[g]
        sall_ref[...] = s0.reshape(HB, Bl, d, d)
        o, s_new = _dn_chunk(tok_in(q), tok_in(k), tok_in(v), gate_in(A_LANE), gate_in(B_LANE), head_in(al), head_in(dt), s0)
        for i, (hh, bb) in enumerate(chains):
            o_ref[bb, :, hh * d:(hh + 1) * d] = o[i]
        st[g] = s_new

    tok = pl.BlockSpec((Bl, CHUNK, HB * d), lambda n, g: (0, n, g))
    per_head = pl.BlockSpec((HB, 1, d), lambda n, g: (g, 0, 0))
    return _call(
        body, "dn_fwd", (NC, HG),
        [tok, tok, tok, pl.BlockSpec((Bl, CHUNK, ZS), lambda n, g: (0, n, 0)), per_head, per_head],
        [tok, pl.BlockSpec((HB, Bl, None, d, d), lambda n, g: (g, 0, n, 0, 0))],
        [jax.ShapeDtypeStruct((Bl, S, D), F32), jax.ShapeDtypeStruct((DN_HEADS, Bl, NC, d, d), F32)],
        [pltpu.VMEM((HG, G, d, d), F32)], ("arbitrary", "arbitrary"), (qa, ka, va, z_small, alog, dtb), stage)


def _dn_bwd(qa, ka, va, z_small, alog, dtb, s_all, do, dzs_gla, Bl, S, D, stage=None):
    NC, d, HB = S // CHUNK, D // DN_HEADS, DN_HEADS_PER_STEP
    HG = DN_HEADS // HB
    chains = [(hh, bb) for hh in range(HB) for bb in range(Bl)]
    G = len(chains)

    def lanesum(t):
        return jnp.sum(t, axis=-1, keepdims=True)

    def body(q, k, v, z, al, dt, s0_ref, do_ref, dzg_ref, dq_ref, dk_ref, dv_ref, dzs_ref, dal_ref, ddt_ref, dst):
        n, g = pl.program_id(0), pl.program_id(1)

        @pl.when(n == 0)
        def _():
            dst[g] = jnp.zeros((G, d, d), F32)

        @pl.when((n == 0) & (g == 0))
        def _():
            dal_ref[...] = jnp.zeros_like(dal_ref)
            ddt_ref[...] = jnp.zeros_like(ddt_ref)

        tok_in = lambda r: jnp.stack([r[bb, :, hh * d:(hh + 1) * d] for hh, bb in chains])
        head_in = lambda r: jnp.stack([r[hh] for hh, _ in chains])
        gate_in = lambda lane0: jnp.stack([_lane_column(z[bb], lane0 + g * HB + hh, d) for hh, bb in chains])
        _, vjp = jax.vjp(_dn_chunk, tok_in(q), tok_in(k), tok_in(v), gate_in(A_LANE), gate_in(B_LANE), head_in(al),
                         head_in(dt), s0_ref[...].reshape(G, d, d))
        dq, dkk, dvv, da, db, dal, ddt, ds0 = vjp((tok_in(do_ref), dst[g]))
        da, db = lanesum(da), lanesum(db)
        dal = jnp.broadcast_to(lanesum(dal), (G, 1, d))
        ddt = jnp.broadcast_to(lanesum(ddt), (G, 1, d))
        lane = lax.broadcasted_iota(jnp.int32, (CHUNK, ZS), 1)
        for bb in range(Bl):
            part = jnp.zeros((CHUNK, ZS), F32)
            for i, (hh, b2) in enumerate(chains):
                if b2 == bb:
                    h = g * HB + hh
                    part = part + jnp.where(lane == A_LANE + h, da[i], 0.0) + jnp.where(lane == B_LANE + h, db[i], 0.0)

            @pl.when(g == 0)
            def _():
                dzs_ref[bb] = jnp.where(lane < LOWRANK, dzg_ref[bb], 0.0) + part

            @pl.when(g > 0)
            def _():
                dzs_ref[bb] += part
        for i, (hh, bb) in enumerate(chains):
            cols = slice(hh * d, (hh + 1) * d)
            dq_ref[bb, :, cols] = dq[i]
            dk_ref[bb, :, cols] = dkk[i]
            dv_ref[bb, :, cols] = dvv[i]
            dal_ref[g * HB + hh] += dal[i]
            ddt_ref[g * HB + hh] += ddt[i]
        dst[g] = ds0

    rn = lambda n: NC - 1 - n
    tok = pl.BlockSpec((Bl, CHUNK, HB * d), lambda n, g: (0, rn(n), g))
    zsb = pl.BlockSpec((Bl, CHUNK, ZS), lambda n, g: (0, rn(n), 0))
    per_head = pl.BlockSpec((HB, 1, d), lambda n, g: (g, 0, 0))
    all_heads = pl.BlockSpec((DN_HEADS, 1, d), lambda n, g: (0, 0, 0))
    tok_shape = jax.ShapeDtypeStruct((Bl, S, D), F32)
    head_shape = jax.ShapeDtypeStruct((DN_HEADS, 1, d), F32)
    return _call(
        body, "dn_bwd", (NC, HG),
        [tok, tok, tok, zsb, per_head, per_head,
         pl.BlockSpec((HB, Bl, None, d, d), lambda n, g: (g, 0, rn(n), 0, 0)), tok, zsb],
        [tok, tok, tok, zsb, all_heads, all_heads],
        [tok_shape, tok_shape, tok_shape, jax.ShapeDtypeStruct((Bl, S, ZS), F32), head_shape, head_shape],
        [pltpu.VMEM((HG, G, d, d), F32)], ("arbitrary", "arbitrary"),
        (qa, ka, va, z_small, alog, dtb, s_all, do, dzs_gla), stage)


def _merge_specs(D, bt):
    dv, w = D // GLA_HEADS, D // DN_HEADS
    col = lambda off: pl.BlockSpec((bt, dv), lambda i, h: (i, off // dv + h))
    return dv, w, col


def _merge_load(refs, nsub, w):
    return [[r[:, s * w:(s + 1) * w] for s in range(nsub)] for r in refs]


def _merge_fwd(o_gla, o_dn, z_big, gla_norm, dn_norm, D, bt=256, stage=None):
    T = o_gla.shape[0]
    bt = _pick(T, bt, SUBLANES)
    dv, w, col = _merge_specs(D, bt)
    nsub = dv // w

    def body(og, gg, od, dz, ga, gb, gn, dn, out):
        ogl, ggl, odl, dzl, gal, gbl = _merge_load([og, gg, od, dz, ga, gb], nsub, w)
        gnl = [gn[:, s * w:(s + 1) * w] for s in range(nsub)]
        outs = _merge_math(ogl, ggl, odl, dzl, gal, gbl, gnl, dn[...])
        for s in range(nsub):
            out[:, s * w:(s + 1) * w] = outs[s].astype(out.dtype)

    return _call(
        body, "merge_fwd", (T // bt, GLA_HEADS),
        [col(0), col(2 * D), col(0), col(6 * D), col(7 * D), col(8 * D),
         pl.BlockSpec((1, dv), lambda i, h: (0, 0)), pl.BlockSpec((1, w), lambda i, h: (0, 0))],
        [col(0)], [jax.ShapeDtypeStruct((T, D), BF16)], [], ("parallel", "parallel"),
        (o_gla, z_big, o_dn, z_big, z_big, z_big, gla_norm, dn_norm), stage)


def _merge_bwd(o_gla, o_dn, z_big, gla_norm, dn_norm, dmix, D, bt=256):
    T = o_gla.shape[0]
    bt = _pick(T, bt, SUBLANES)
    dv, w, col = _merge_specs(D, bt)
    nsub = dv // w

    def body(og, gg, od, dz, ga, gb, gn, dn, dm, dog, dgg, dod, ddz, dga, dgb, dgn, ddn):
        @pl.when((pl.program_id(0) == 0) & (pl.program_id(1) == 0))
        def _():
            dgn[...] = jnp.zeros_like(dgn)
            ddn[...] = jnp.zeros_like(ddn)

        ogl, ggl, odl, dzl, gal, gbl, dml = _merge_load([og, gg, od, dz, ga, gb, dm], nsub, w)
        gnl = [gn[:, s * w:(s + 1) * w] for s in range(nsub)]
        _, vjp = jax.vjp(_merge_math, ogl, ggl, odl, dzl, gal, gbl, gnl, dn[...])
        g_og, g_gg, g_od, g_dz, g_ga, g_gb, g_gn, g_dn = vjp(dml)
        for s in range(nsub):
            sl = slice(s * w, (s + 1) * w)
            dog[:, sl] = g_og[s]
            dgg[:, sl] = g_gg[s].astype(dgg.dtype)
            dod[:, sl] = g_od[s]
            ddz[:, sl] = g_dz[s].astype(ddz.dtype)
            dga[:, sl] = g_ga[s].astype(dga.dtype)
            dgb[:, sl] = g_gb[s].astype(dgb.dtype)
            dgn[:, sl] += g_gn[s]
        ddn[...] += g_dn

    f32s, bf16s = jax.ShapeDtypeStruct((T, D), F32), jax.ShapeDtypeStruct((T, D), BF16)
    return pl.pallas_call(
        body, name="merge_bwd", grid=(T // bt, GLA_HEADS),
        in_specs=[col(0), col(2 * D), col(0), col(6 * D), col(7 * D), col(8 * D),
                  pl.BlockSpec((1, dv), lambda i, h: (0, 0)), pl.BlockSpec((1, w), lambda i, h: (0, 0)), col(0)],
        out_specs=[col(0)] * 6 + [pl.BlockSpec((1, dv), lambda i, h: (0, 0)), pl.BlockSpec((1, w), lambda i, h: (0, 0))],
        out_shape=[f32s, bf16s, f32s, bf16s, bf16s, bf16s,
                   jax.ShapeDtypeStruct((1, dv), F32), jax.ShapeDtypeStruct((1, w), F32)],
        compiler_params=_params(("arbitrary", "arbitrary")),
    )(o_gla, z_big, o_dn, z_big, z_big, z_big, gla_norm, dn_norm, dmix)


def _place():
    return lax.axis_index("x"), lax.axis_index("y"), lax.axis_index("c")


def _other_chips(x, y):
    return [(1 - x, y), (x, 1 - y), (1 - x, 1 - y)]


def _rcopy(src, dst, send_sem, recv_sem, dev):
    return pltpu.make_async_remote_copy(src_ref=src, dst_ref=dst, send_sem=send_sem, recv_sem=recv_sem,
                                        device_id=dev, device_id_type=MESH)


ANY = pl.BlockSpec(memory_space=pl.ANY)


ROWS, COLS = 'rows', 'cols'


def _half(ref, hc, by, lead=()):
    shape = ref.shape[len(lead):]
    if by == ROWS:
        rh = shape[0] // 2
        idx = (pl.ds(pl.multiple_of(hc * rh, 16), rh),) + (slice(None),) * (len(shape) - 1)
    else:
        ch = shape[-1] // 2
        idx = (slice(None),) * (len(shape) - 1) + (pl.ds(pl.multiple_of(hc * ch, LANES), ch),)
    return ref.at[(*lead, *idx)]


def _half_shape(shape, by):
    return (shape[0] // 2,) + tuple(shape[1:]) if by == ROWS else tuple(shape[:-1]) + (shape[-1] // 2,)


def _gather_ici(shards, by):
    nw = len(shards)

    def copies(srcs, outs, send_sems, recv_sems):
        x, y, c = _place()
        return [_rcopy(_half(srcs[w], c, by[w]), _half(outs[w], c, by[w], (2 * x + y,)),
                       send_sems.at[3 * w + k], recv_sems.at[3 * w + k], (px, py, c))
                for w in range(nw) for k, (px, py) in enumerate(_other_chips(x, y))]

    return _Stage(shards, [jax.ShapeDtypeStruct((4,) + s.shape, s.dtype) for s in shards], 3 * nw, copies)


def _gather_neighbours(shards, by):
    nw = len(shards)

    def copies(srcs, outs, send_sems, recv_sems):
        x, y, c = _place()
        return [_rcopy(_half(srcs[w], c, by[w]), _half(outs[w], c, by[w], (2 * x + y,)),
                       send_sems.at[2 * w + k], recv_sems.at[2 * w + k], (px, py, c))
                for w in range(nw) for k, (px, py) in enumerate(_other_chips(x, y)[:2])]

    return _Stage(shards, [jax.ShapeDtypeStruct((4,) + s.shape, s.dtype) for s in shards], 2 * nw, copies)


def _gather_relay(gathered):
    nw = len(gathered)

    def copies(srcs, outs, send_sems, recv_sems):
        x, y, c = _place()
        cps = []
        for w in range(nw):
            _, n, cols = gathered[w].shape
            cut, ch = n // 2 // 16 * 16, cols // 2
            lanes = pl.ds(pl.multiple_of(c * ch, LANES), ch)
            via = [(2 * (1 - x) + y, pl.ds(0, cut), (x, 1 - y, c)),
                   (2 * x + (1 - y), pl.ds(cut, n - cut), (1 - x, y, c))]
            for k, (slot, rows, dev) in enumerate(via):
                cps.append(_rcopy(srcs[w].at[slot, rows, lanes], outs[w].at[slot, rows, lanes],
                                  send_sems.at[2 * w + k], recv_sems.at[2 * w + k], dev))
        return cps

    return _Stage(gathered, [jax.ShapeDtypeStruct(g.shape, g.dtype) for g in gathered], 2 * nw, copies,
                  aliases={w: w for w in range(nw)})


def _gather_pass(gathered, by):
    nw = len(gathered)

    def copies(srcs, outs, send_sems, recv_sems):
        x, y, c = _place()
        cps = []
        for w in range(nw):
            for k, (px, py) in enumerate(_other_chips(x, y)):
                slot = (2 * px + py,)
                cps.append(_rcopy(_half(srcs[w], c, by[w], slot), _half(outs[w], c, by[w], slot),
                                  send_sems.at[3 * w + k], recv_sems.at[3 * w + k], (x, y, 1 - c)))
        return cps

    return _Stage(gathered, [jax.ShapeDtypeStruct(g.shape, g.dtype) for g in gathered], 3 * nw, copies,
                  aliases={w: w for w in range(nw)})


def _pair_exchange(ps, by):
    nw = len(ps)

    def copies(srcs, outs, send_sems, recv_sems):
        x, y, c = _place()
        return [_rcopy(_half(srcs[w], 1 - c, by[w], (slice(None),)), outs[w], send_sems.at[w], recv_sems.at[w], (x, y, 1 - c))
                for w in range(nw)]

    return _Stage(ps, [jax.ShapeDtypeStruct((4,) + _half_shape(p.shape[1:], b), p.dtype) for p, b in zip(ps, by)], nw, copies)


def _sum_blocks(half_shape, by):
    rh, ch = half_shape
    if by == ROWS:
        lanes = -(-ch // LANES) * LANES
        bt = _pick(rh, max(16, (3 << 18) // lanes // 16 * 16), 16)
        return (bt, ch), rh // bt, lambda i: (i, 0)
    bc = _pick(ch, max(LANES, (5 << 18) // rh // LANES * LANES), LANES)
    return (rh, bc), ch // bc, lambda i: (0, i)


def _pair_sum(p, got, c_idx, name, by=ROWS):
    hs = got.shape[1:]
    blk, nb, pos = _sum_blocks(hs, by)

    def body(c_ref, a, b, of, ob):
        s = a[...] + b[...]
        of[...] = s
        ob[...] = s.astype(BF16)

    def mine(j, i, c_ref):
        r, cc = pos(c_ref[0] * nb + i)
        return (j, r, cc)

    spec = pl.BlockSpec((None,) + blk, lambda j, i, c_ref: (j,) + pos(i))
    return pl.pallas_call(
        body, name=name,
        grid_spec=pltpu.PrefetchScalarGridSpec(
            num_scalar_prefetch=1, grid=(4, nb),
            in_specs=[pl.BlockSpec((None,) + blk, mine), spec], out_specs=[spec, spec]),
        out_shape=[jax.ShapeDtypeStruct((4,) + hs, F32), jax.ShapeDtypeStruct((4,) + hs, BF16)],
        compiler_params=_params(("parallel", "parallel")),
    )(c_idx, p, got)


def _chip_scatter(qbs):
    nw = len(qbs)

    def copies(srcs, outs, send_sems, recv_sems):
        x, y, c = _place()
        return [_rcopy(srcs[w].at[2 * px + py], outs[w].at[k], send_sems.at[3 * w + k], recv_sems.at[3 * w + k], (px, py, c))
                for w in range(nw) for k, (px, py) in enumerate(_other_chips(x, y))]

    return _Stage(qbs, [jax.ShapeDtypeStruct((3,) + q.shape[1:], q.dtype) for q in qbs], 3 * nw, copies)


def _final_sum(qf, got, me_idx, name, by=ROWS):
    hs = qf.shape[1:]
    blk, nb, pos = _sum_blocks(hs, by)

    def body(me_ref, a, b, o):
        o[...] = ((a[...] + b[0].astype(F32)) + b[1].astype(F32)) + b[2].astype(F32)

    return pl.pallas_call(
        body, name=name,
        grid_spec=pltpu.PrefetchScalarGridSpec(
            num_scalar_prefetch=1, grid=(nb,),
            in_specs=[pl.BlockSpec((None,) + blk, lambda i, me_ref: (me_ref[0],) + pos(i)),
                      pl.BlockSpec((3,) + blk, lambda i, me_ref: (0,) + pos(i))],
            out_specs=pl.BlockSpec(blk, lambda i, me_ref: pos(i))),
        out_shape=jax.ShapeDtypeStruct(hs, F32),
        compiler_params=_params(("parallel",)),
    )(me_idx, qf, got)


def _pair_allgather(halves, by):
    nw = len(halves)
    whole = [(2 * h.shape[0], h.shape[1]) if b == ROWS else h.shape for h, b in zip(halves, by)]

    def copies(srcs, outs, send_sems, recv_sems):
        x, y, c = _place()
        there = lambda w: _half(outs[w], c, ROWS) if by[w] == ROWS else outs[w]
        return [_rcopy(srcs[w], there(w), send_sems.at[w], recv_sems.at[w], (x, y, 1 - c)) for w in range(nw)]

    return _Stage(halves, [jax.ShapeDtypeStruct(s, h.dtype) for s, h in zip(whole, halves)], nw, copies)


class _SemaphoreWindow:
    def __init__(self, ref, off):
        self.ref, self.off = ref, off

    @property
    def at(self):
        return self

    def __getitem__(self, i):
        return self.ref.at[self.off + i]


def _both(a, b):
    na, ma = len(a.inputs), len(a.out_shapes)

    def copies(ins, outs, send_sems, recv_sems):
        return (a.copies(ins[:na], outs[:ma], send_sems, recv_sems) +
                b.copies(ins[na:], outs[ma:], _SemaphoreWindow(send_sems, a.n_sems), _SemaphoreWindow(recv_sems, a.n_sems)))

    return _Stage(a.inputs + b.inputs, a.out_shapes + b.out_shapes, a.n_sems + b.n_sems, copies,
                  aliases={**a.aliases, **{na + i: ma + o for i, o in b.aliases.items()}})


def _small_exchange(items, out_shapes, finish, name):
    n = len(items)
    offs, rows = [], 0
    for it in items:
        offs.append(rows)
        rows += it.shape[0]
    rows = -(-rows // SUBLANES) * SUBLANES
    width = -(-max(it.shape[1] for it in items) // LANES) * LANES
    VMEM = pl.BlockSpec(memory_space=pltpu.VMEM)

    def body(*refs):
        ins, outs = refs[:n], refs[n:n + len(out_shapes)]
        buf, send_sems, recv_sems = refs[n + len(out_shapes):]
        x, y, c = _place()
        me = 4 * x + 2 * y + c
        flip = lambda v, f: (1 - v) if f else v
        peers = [(flip(x, r >> 2 & 1), flip(y, r >> 1 & 1), flip(c, r & 1)) for r in range(1, 8)]
        buf[me] = jnp.zeros((rows, width), F32)
        for it, off, ref in zip(items, offs, ins):
            buf[me, off:off + it.shape[0], 0:it.shape[1]] = ref[...]
        cps = [_rcopy(buf.at[me], buf.at[me], send_sems.at[k], recv_sems.at[k], dev) for k, dev in enumerate(peers)]
        for cp in cps:
            cp.start()
        for k, (px, py, pc) in enumerate(peers):
            slot = buf.at[4 * px + 2 * py + pc]
            _rcopy(slot, slot, send_sems.at[k], recv_sems.at[k], (px, py, pc)).wait_recv()
        for cp in cps:
            cp.wait_send()
        finish(buf, offs, outs)

    return pl.pallas_call(
        body, name=name, in_specs=[VMEM] * n, out_specs=[VMEM] * len(out_shapes),
        out_shape=[jax.ShapeDtypeStruct(s, F32) for s in out_shapes],
        scratch_shapes=[pltpu.VMEM((8, rows, width), F32), pltpu.SemaphoreType.DMA((7,)), pltpu.SemaphoreType.DMA((7,))],
        compiler_params=pltpu.CompilerParams(vmem_limit_bytes=VMEM_LIMIT_BYTES),
    )(*items)


def _allreduce_small(items, name):
    def finish(buf, offs, outs):
        for it, off, out in zip(items, offs, outs):
            region = lambda d: buf[d, off:off + it.shape[0], 0:it.shape[1]]
            s = region(0)
            for d in range(1, 8):
                s = s + region(d)
            out[...] = s
    return _small_exchange(items, [it.shape for it in items], finish, name)


def _allgather_small_shards(items, name):
    def finish(buf, offs, outs):
        for it, off, out in zip(items, offs, outs):
            r, c = it.shape
            for j in range(4):
                out[:, j * c:(j + 1) * c] = buf[2 * j, off:off + r, 0:c]
    return _small_exchange(items, [(it.shape[0], 4 * it.shape[1]) for it in items], finish, name)


def _split_w_in(wt, D):
    pad = jnp.zeros((ZS - 3 * LOWRANK, wt.shape[1]), wt.dtype)
    big = jnp.concatenate([wt[:3 * D], wt[3 * D + 16:6 * D + 16], wt[6 * D + 16:7 * D + 16], wt[7 * D + 48:]], axis=0)
    small = jnp.concatenate([wt[3 * D:3 * D + 16], wt[7 * D + 16:7 * D + 48], pad], axis=0)
    return big, small


def _join_w_in(gb, gs, D):
    return jnp.concatenate([gb[:3 * D], gs[:16], gb[3 * D:6 * D], gb[6 * D:7 * D], gs[16:48], gb[7 * D:9 * D]], axis=0)


def kernel(x, p, g_mix, w_in, gla_w2, gla_b, gla_norm, dn_conv, dn_a_log, dn_dt_bias, dn_norm, w_out, g_mlp, w_up, w_down, g_ple, w_ple_gate, w_ple_proj, g_final, loss_target, m_g_mix, m_w_in, m_gla_w2, m_gla_b, m_gla_norm, m_dn_conv, m_dn_a_log, m_dn_dt_bias, m_dn_norm, m_w_out, m_g_mlp, m_w_up, m_w_down, m_g_ple, m_w_ple_gate, m_w_ple_proj, m_g_final, v_g_mix, v_w_in, v_gla_w2, v_gla_b, v_gla_norm, v_dn_conv, v_dn_a_log, v_dn_dt_bias, v_dn_norm, v_w_out, v_g_mlp, v_w_up, v_w_down, v_g_ple, v_w_ple_gate, v_w_ple_proj, v_g_final):
    wts = dict(zip(WEIGHTS, [g_mix, w_in, gla_w2, gla_b, gla_norm, dn_conv, dn_a_log, dn_dt_bias, dn_norm, w_out, g_mlp,
                             w_up, w_down, g_ple, w_ple_gate, w_ple_proj, g_final]))
    mom = dict(zip(WEIGHTS, [m_g_mix, m_w_in, m_gla_w2, m_gla_b, m_gla_norm, m_dn_conv, m_dn_a_log, m_dn_dt_bias, m_dn_norm,
                             m_w_out, m_g_mlp, m_w_up, m_w_down, m_g_ple, m_w_ple_gate, m_w_ple_proj, m_g_final]))
    var = dict(zip(WEIGHTS, [v_g_mix, v_w_in, v_gla_w2, v_gla_b, v_gla_norm, v_dn_conv, v_dn_a_log, v_dn_dt_bias, v_dn_norm,
                             v_w_out, v_g_mlp, v_w_up, v_w_down, v_g_ple, v_w_ple_gate, v_w_ple_proj, v_g_final]))
    Bl, S, D = x.shape
    T = Bl * S
    PLE = p.shape[-1]
    dn_d, gla_dk = D // DN_HEADS, D // (2 * GLA_HEADS)
    ix, iy, ic = _place()
    j_me = 2 * ix + iy
    as2d = lambda a: a.reshape(a.shape[-2], a.shape[-1]) if a.ndim > 1 else a.reshape(1, -1)
    c_idx, me_idx = ic.reshape(1).astype(jnp.int32), j_me.reshape(1).astype(jnp.int32)

    rows_first = lambda a: jnp.transpose(a, (2, 0, 1))
    cols_last = lambda a: jnp.transpose(a, (1, 2, 0))
    w_in_t, m_in_t, v_in_t = rows_first(w_in), rows_first(m_w_in), rows_first(v_w_in)
    n_in = w_in_t.shape[0]
    shard2d = {n: as2d(wts[n]) for n, _ in BIG[1:]}
    bf16_shards = [w_in_t.astype(BF16).reshape(n_in, D)] + [shard2d[n].astype(BF16) for n, _ in BIG[1:]]
    split = [COLS] + [ROWS] * (len(BIG) - 1)
    own_slot = lambda g, s: lax.dynamic_update_slice(g, s[None], (j_me, 0, 0))
    xt = x.reshape(T, D)
    (w_in_near,) = _run_stage(_gather_neighbours(bf16_shards[:1], split[:1]), "allgather_w_in_neighbours")
    (w_in_ici,) = _run_stage(_gather_relay([w_in_near]), "allgather_w_in_relay")
    h, w_in_all = _rmsnorm_fwd(xt, g_mix, "rms1_fwd", stage=_gather_pass([w_in_ici], split[:1]))
    w_in_slots = own_slot(w_in_all, bf16_shards[0])
    w_big, w_small = _split_w_in(w_in_slots.reshape(4 * n_in, D), D)

    w2_full, conv_full = _allgather_small_shards([as2d(gla_w2), as2d(dn_conv)], "allgather_small_weights")
    w2pad = jnp.pad(w2_full, ((0, ZS - LOWRANK), (0, 0)))
    w2h = jnp.swapaxes(w2pad.reshape(ZS, GLA_HEADS, gla_dk), 0, 1)
    gbh = gla_b.reshape(GLA_HEADS, 1, gla_dk)
    alog_w = jnp.broadcast_to(dn_a_log.reshape(DN_HEADS, 1, 1), (DN_HEADS, 1, dn_d))
    dtb_w = jnp.broadcast_to(dn_dt_bias.reshape(DN_HEADS, 1, 1), (DN_HEADS, 1, dn_d))

    tgt = loss_target.reshape(T, D)
    pt = p.reshape(T, PLE)
    seq = lambda t: t.reshape(Bl, S, t.shape[-1])
    tok = lambda t: t.reshape(T, t.shape[-1])
    first, second = [1, 2, 5], [3, 4]
    sh, sp = (lambda idx: [bf16_shards[i] for i in idx]), (lambda idx: [split[i] for i in idx])
    z_big, *first_ici = _matmul(h, w_big, 'nt', [F32], "proj_in", stage=_gather_ici(sh(first), sp(first)))
    (z_small,) = _matmul(h, w_small, 'nt', [F32], "proj_in_narrow")
    o_gla, st_all, *first_all = _gla_fwd(seq(z_big), seq(z_small), w2h, gbh, Bl, S, D, stage=_gather_pass(first_ici, sp(first)))
    acts = [_conv_fwd(z_big, conv_full, grp, Bl, S, D) for grp in range(3)]
    o_dn, s_all, *second_ici = _dn_fwd(seq(acts[0]), seq(acts[1]), seq(acts[2]), seq(z_small), alog_w, dtb_w, Bl, S, D,
                                       stage=_gather_ici(sh(second), sp(second)))
    mixed, *second_all = _merge_fwd(tok(o_gla), tok(o_dn), z_big, gla_norm, dn_norm, D,
                                    stage=_gather_pass(second_ici, sp(second)))
    slots = {BIG[i][0]: own_slot(g, bf16_shards[i]) for i, g in zip(first + second, first_all + second_all)}
    rows_joined = lambda t: t.reshape(4 * t.shape[1], t.shape[2])
    w_out_f, w_down_f, w_pg_f = rows_joined(slots['w_out']), rows_joined(slots['w_down']), rows_joined(slots['w_ple_gate'])
    w_up_s, w_pp_s = slots['w_up'], slots['w_ple_proj']
    (x1,) = _matmul(mixed, w_out_f, 'nn', [F32], "proj_out", epilogue=lambda r, e: (e + r,), extras=(xt,), bm=512)
    (h2,) = _rmsnorm_fwd(x1, g_mlp, "rms2_fwd")
    u, act = _matmul(h2, w_up_s, 'nn', [F32, BF16], "mlp_up", b_slots=True,
                     epilogue=lambda r: (r, jnp.square(jnp.maximum(r, 0.0))))
    (x2,) = _matmul(act, w_down_f, 'nn', [F32], "mlp_down", epilogue=lambda r, e: (e + r,), extras=(x1,), bm=512)
    (h3,) = _rmsnorm_fwd(x2, g_ple, "rms3_fwd")
    (pp,) = _matmul(pt, w_pp_s, 'nn', [F32], "ple_proj", b_slots=True)
    gp, x3 = _matmul(h3, w_pg_f, 'nn', [F32, F32], "ple_gate",
                     epilogue=lambda r, e, q: (r, e + _sigmoid(r) * q), extras=(x2, pp), bm=512)
    dx3, loss_tile, d_g_final = _loss_fwd_bwd(x3, g_final.reshape(1, D), tgt, "loss")

    d_gp, d_pp = _ple_bwd(dx3, gp, pp, "ple_bwd")
    (g_pp,) = _matmul(pt, d_pp, 'tn', [F32], "ple_proj_dw", out_slots=True)
    (g_pg,) = _matmul(h3, d_gp, 'tn', [F32], "ple_gate_dw")
    (dh3,) = _matmul(d_gp, w_pg_f, 'nt', [F32], "ple_gate_dx")
    dx2, dx2b, d_g_ple = _rmsnorm_bwd_add(x2, g_ple, dh3, dx3, "rms3_bwd")
    (g_down,) = _matmul(act, dx2b, 'tn', [F32], "mlp_down_dw")
    (du,) = _matmul(dx2b, w_down_f, 'nt', [BF16], "mlp_down_dx",
                    epilogue=lambda r, e: (r * 2.0 * jnp.maximum(e, 0.0),), extras=(u,))
    (g_up,) = _matmul(h2, du, 'tn', [F32], "mlp_up_dw", out_slots=True)
    by_rows = lambda g: g.reshape(4, g.shape[0] // 4, g.shape[1])
    send_mlp = [g_up, by_rows(g_down), by_rows(g_pg), g_pp]
    dh2, *sib_mlp = _matmul(du, w_up_s, 'nt', [F32], "mlp_up_dx", b_slots=True, stage=_pair_exchange(send_mlp, split[2:]))
    dx1, dx1b, d_g_mlp = _rmsnorm_bwd_add(x1, g_mlp, dh2, dx2, "rms2_bwd")
    (g_out,) = _matmul(mixed, dx1b, 'tn', [F32], "proj_out_dw")
    dmix, sib_out = _matmul(dx1b, w_out_f, 'nt', [F32], "proj_out_dx", stage=_pair_exchange([by_rows(g_out)], split[1:2]))
    rest = [n for n, _ in BIG[1:]]
    send_rest, sib_rest = [by_rows(g_out)] + send_mlp, [sib_out] + sib_mlp
    sums_rest = [_pair_sum(s, f, c_idx, f"grad_pair_sum_{n}") for n, s, f in zip(rest, send_rest, sib_rest)]
    d_ogla, d_gg, d_odn, d_dz, d_ga, d_gb, d_gla_norm, d_dn_norm = _merge_bwd(
        tok(o_gla), tok(o_dn), z_big, gla_norm, dn_norm, dmix, D)
    d_q, d_k, d_v, dzs_gla, d_w2h, d_gbh = _gla_bwd(seq(z_big), seq(z_small), w2h, gbh, st_all, seq(d_ogla), Bl, S, D)
    d_qa, d_ka, d_va, d_zs, d_alog_w, d_dtb_w, *chips_rest = _dn_bwd(
        seq(acts[0]), seq(acts[1]), seq(acts[2]), seq(z_small), alog_w, dtb_w, s_all, seq(d_odn), dzs_gla, Bl, S, D,
        stage=_chip_scatter([b for _, b in sums_rest]))
    conv_b = [_conv_bwd(z_big, conv_full, tok(g), grp, Bl, S, D) for grp, g in enumerate([d_qa, d_ka, d_va])]
    dz_big = jnp.concatenate([tok(d_q), tok(d_k), tok(d_v), d_gg, conv_b[0][0], conv_b[1][0], conv_b[2][0], d_dz, d_ga,
                              d_gb], axis=1)
    dz_small = tok(d_zs)
    (d_w_big,) = _matmul(dz_big, h, 'tn', [F32], "proj_in_dw")
    halves_rest = [_final_sum(f, got, me_idx, f"grad_final_sum_{n}") for n, (f, _), got in zip(rest, sums_rest, chips_rest)]
    (d_w_small,) = _matmul(dz_small, h, 'tn', [F32], "proj_in_narrow_dw")
    g_in = _join_w_in(d_w_big, d_w_small, D).reshape(4, n_in, D)
    (sib_in,) = _run_stage(_pair_exchange([g_in], split[:1]), "grad_pair_exchange_w_in")
    sum_in_f32, sum_in_bf16 = _pair_sum(g_in, sib_in, c_idx, "grad_pair_sum_w_in", split[0])
    dh_a, chips_in, *pair_rest = _matmul(dz_big, w_big, 'nn', [F32], "proj_in_dx",
                                         stage=_both(_chip_scatter([sum_in_bf16]), _pair_allgather(halves_rest, split[1:])))
    half_in = _final_sum(sum_in_f32, chips_in, me_idx, "grad_final_sum_w_in", split[0])
    (dh,) = _matmul(dz_small, w_small, 'nn', [F32], "proj_in_narrow_dx", epilogue=lambda r, e: (e + r,), extras=(dh_a,))
    grad_x, _, d_g_mix = _rmsnorm_bwd_add(xt, g_mix, dh, dx1, "rms1_bwd")
    (pair_in,) = _run_stage(_pair_allgather([half_in], split[:1]), "grad_pair_allgather_w_in")
    reduced = {n: lax.dynamic_update_slice(o, hlf, (ic * hlf.shape[0], 0)) for n, o, hlf in zip(rest, pair_rest, halves_rest)}
    south = ic == 0
    g_in_t = jnp.concatenate([jnp.where(south, half_in, pair_in), jnp.where(south, pair_in, half_in)],
                             axis=1).reshape(n_in, 1, D)

    d_w2 = jnp.swapaxes(d_w2h, 0, 1).reshape(ZS, D // 2)[:LOWRANK]
    small_grads = {'g_mix': d_g_mix, 'gla_w2': d_w2, 'gla_b': d_gbh.reshape(1, D // 2), 'gla_norm': d_gla_norm,
                   'dn_a_log': d_alog_w[:, 0, 0].reshape(1, DN_HEADS), 'dn_dt_bias': d_dtb_w[:, 0, 0].reshape(1, DN_HEADS),
                   'dn_norm': d_dn_norm, 'g_mlp': d_g_mlp, 'g_ple': d_g_ple, 'g_final': d_g_final}
    names = [n for n in SMALL if n != 'dn_conv']
    total = _allreduce_small([small_grads[n] for n in names] + [cb[1] for cb in conv_b] + [loss_tile[:1]],
                             "allreduce_small_grads")
    gsmall = dict(zip(names, total[:len(names)]))
    loss = total[-1][0, 0]
    my_cols = lambda g: lax.dynamic_slice_in_dim(g, j_me * (g.shape[1] // 4), g.shape[1] // 4, axis=1)
    gsmall['gla_w2'] = my_cols(gsmall['gla_w2'])
    gsmall['dn_conv'] = my_cols(jnp.concatenate(total[len(names):len(names) + 3], axis=1))

    g_o, d_o, m_o, v_o = {}, {}, {}, {}
    d_in_t, nm_in_t, nv_in_t, g_out_t = _adamw(w_in_t, g_in_t, m_in_t, v_in_t, "adamw_w_in", with_grad=True)
    g_o['w_in'], d_o['w_in'], m_o['w_in'], v_o['w_in'] = [cols_last(t) for t in (g_out_t, d_in_t, nm_in_t, nv_in_t)]
    for n, _ in BIG[1:]:
        shp = wts[n].shape
        d2, nm2, nv2 = _adamw(shard2d[n], reduced[n], as2d(mom[n]), as2d(var[n]), f"adamw_{n}")
        g_o[n], d_o[n], m_o[n], v_o[n] = reduced[n].reshape(shp), d2.reshape(shp), nm2.reshape(shp), nv2.reshape(shp)
    ds, nms, nvs = _adamw_small([as2d(wts[n]) for n in SMALL], [as2d(gsmall[n]) for n in SMALL],
                                [as2d(mom[n]) for n in SMALL], [as2d(var[n]) for n in SMALL])
    for n, dd, mm, vv in zip(SMALL, ds, nms, nvs):
        shp = wts[n].shape
        g_o[n], d_o[n], m_o[n], v_o[n] = gsmall[n].reshape(shp), dd.reshape(shp), mm.reshape(shp), vv.reshape(shp)

    return (loss, grad_x.reshape(Bl, S, D), *[g_o[n] for n in WEIGHTS], *[d_o[n] for n in WEIGHTS],
            *[m_o[n] for n in WEIGHTS], *[v_o[n] for n in WEIGHTS])
```

```python
import functools

import jax
import jax.numpy as jnp
from jax import lax
from jax.experimental import pallas as pl
from jax.experimental.pallas import tpu as pltpu

F32 = jnp.float32
BF16 = jnp.bfloat16

CHUNK = 64
GLA_HEADS = 4
DN_HEADS = 16
LOWRANK = 16
GLA_TAU = 16.0
DN_CONV = 4
EPS = 1e-6
ZS = 128
A_LANE, B_LANE = LOWRANK, LOWRANK + DN_HEADS
ADAM_LR, ADAM_B1, ADAM_B2, ADAM_EPS, ADAM_WD, ADAM_STEP = 0.001, 0.9, 0.999, 1e-08, 0.01, 10

V7X_VMEM_BYTES = 64 * 1024 * 1024
VMEM_LIMIT_BYTES = V7X_VMEM_BYTES - 8 * 1024 * 1024
LANES = 128
SUBLANES = 8
MESH = pl.DeviceIdType.MESH
DN_HEADS_PER_STEP = 16
GLA_HEADS_PER_STEP = 4

WEIGHTS = ['g_mix', 'w_in', 'gla_w2', 'gla_b', 'gla_norm', 'dn_conv', 'dn_a_log', 'dn_dt_bias', 'dn_norm', 'w_out',
           'g_mlp', 'w_up', 'w_down', 'g_ple', 'w_ple_gate', 'w_ple_proj', 'g_final']
BIG = [('w_in', 1), ('w_out', 0), ('w_up', 1), ('w_down', 0), ('w_ple_gate', 0), ('w_ple_proj', 1)]
SMALL = [n for n in WEIGHTS if n not in dict(BIG)]

_NN, _NT, _TN = 'nn', 'nt', 'tn'


def _params(sem=None):
    return pltpu.CompilerParams(dimension_semantics=sem, vmem_limit_bytes=VMEM_LIMIT_BYTES)


def _dot(a, b, form, precision=None):
    o = a.ndim - 2
    contract = {_NN: ((1 + o,), (o,)), _NT: ((1 + o,), (1 + o,)), _TN: ((o,), (o,))}[form]
    batch = ((0,), (0,)) if o else ((), ())
    return lax.dot_general(a, b, (contract, batch), precision=precision, preferred_element_type=F32)


def _make_mm(cast, precision):
    def raw(a, b, dims):
        return _dot(cast(a), cast(b), dims, precision)

    @jax.custom_vjp
    def nn(a, b):
        return raw(a, b, _NN)
    nn.defvjp(lambda a, b: (raw(a, b, _NN), (a, b)), lambda r, g: (raw(g, r[1], _NT), raw(r[0], g, _TN)))

    @jax.custom_vjp
    def nt(a, b):
        return raw(a, b, _NT)
    nt.defvjp(lambda a, b: (raw(a, b, _NT), (a, b)), lambda r, g: (raw(g, r[1], _NN), raw(g, r[0], _TN)))

    @jax.custom_vjp
    def tn(a, b):
        return raw(a, b, _TN)
    tn.defvjp(lambda a, b: (raw(a, b, _TN), (a, b)), lambda r, g: (raw(r[1], g, _NT), raw(r[0], g, _NN)))
    return nn, nt, tn


_bnn, _bnt, _btn = _make_mm(lambda t: t.astype(BF16), None)
TRI_PRECISION = lax.Precision.HIGH


def _iota2(n, axis):
    return lax.broadcasted_iota(jnp.int32, (n, n), axis)


def _lower(n, strict=False):
    return (_iota2(n, 0) > _iota2(n, 1)) if strict else (_iota2(n, 0) >= _iota2(n, 1))


def _tri_times(tri, x):
    tri = tri.astype(F32)
    if x.ndim == 3:
        tri = jnp.broadcast_to(tri, (x.shape[0],) + tri.shape)
    return _dot(tri, x, _NN, lax.Precision.HIGH)


@jax.custom_vjp
def _cumsum_rows(x):
    return _tri_times(_lower(x.shape[-2]), x)


def _cumsum_rows_bwd(_, g):
    n = g.shape[-2]
    return (_tri_times(_iota2(n, 0) <= _iota2(n, 1), g),)


_cumsum_rows.defvjp(lambda x: (_cumsum_rows(x), None), _cumsum_rows_bwd)


def _tri_inv_impl(a):
    n = a.shape[-1]
    eye = (_iota2(n, 0) == _iota2(n, 1)).astype(F32)
    p = eye - a
    ak = a
    k = 2
    while k < n:
        prec, cast = (TRI_PRECISION, lambda t: t) if k == 2 else (None, lambda t: t.astype(BF16))
        ak = _dot(cast(ak), cast(ak), _NN, prec)
        p = p + _dot(cast(p), cast(ak), _NN, prec)
        k *= 2
    return p


@jax.custom_vjp
def _tri_inv(a):
    return _tri_inv_impl(a)


def _tri_inv_fwd(a):
    t = _tri_inv_impl(a)
    return t, t


def _tri_inv_bwd(t, g):
    tb = t.astype(BF16)
    tg = _dot(tb, g.astype(BF16), _TN)
    return (-_dot(tg.astype(BF16), tb, _NT),)


_tri_inv.defvjp(_tri_inv_fwd, _tri_inv_bwd)


def _shift_rows(x, s, down):
    n = x.shape[0]
    r = lax.broadcasted_iota(jnp.int32, x.shape, 0)
    if down:
        return jnp.where(r >= s, pltpu.roll(x, s, 0), 0.0)
    return jnp.where(r < n - s, pltpu.roll(x, n - s, 0), 0.0)


def _make_shift(s):
    @jax.custom_vjp
    def f(x):
        return _shift_rows(x, s, True)
    f.defvjp(lambda x: (_shift_rows(x, s, True), None), lambda _, g: (_shift_rows(g, s, False),))
    return f


def _sigmoid(x):
    return jax.nn.sigmoid(x)


def _silu(x):
    return x * jax.nn.sigmoid(x)


def _softplus(x):
    return jnp.maximum(x, 0.0) + jnp.log1p(jnp.exp(-jnp.abs(x)))


def _log_sigmoid(x):
    return -_softplus(-x)


def _rms(x, g):
    return x * lax.rsqrt(jnp.mean(x * x, axis=-1, keepdims=True) + EPS) * g


def _gla_chunk(q, k, v, zs, w2, gb, st, *, scale):
    c = q.shape[-2]
    logf = _log_sigmoid(_bnn(zs, w2) + gb) * (1.0 / GLA_TAU)
    bcum = _cumsum_rows(logf)
    b_last = jnp.sum(logf, axis=-2, keepdims=True)
    q_in = (q * scale) * jnp.exp(bcum)
    k_in = k * jnp.exp(-bcum)
    a = jnp.where(_lower(c), _bnt(q_in, k_in), 0.0)
    o = _bnn(a, v) + _bnt(q_in, st)
    k_dec = k * jnp.exp(b_last - bcum)
    st_new = st * jnp.exp(b_last) + _btn(v, k_dec)
    return o, st_new


def _dn_chunk(q, k, v, aw, bw, alog, dtb, s):
    c = q.shape[-2]
    incl, strict = _lower(c), _lower(c, True)
    g_w = -jnp.exp(alog) * _softplus(aw + dtb)
    beta_w = _sigmoid(bw)
    gcum_w = _cumsum_rows(g_w)
    lane0 = lax.broadcasted_iota(jnp.int32, gcum_w.shape, gcum_w.ndim - 1) == 0
    gcol = jnp.sum(jnp.where(lane0, gcum_w, 0.0), axis=-1, keepdims=True)
    d1 = jnp.broadcast_to(gcol, gcol.shape[:-1] + (c,))
    diff = jnp.where(incl, d1 - jnp.swapaxes(d1, -1, -2), 0.0)
    decay = jnp.where(incl, jnp.exp(diff), 0.0)
    k_beta = k * beta_w
    a = jnp.where(strict, _bnt(k_beta, k) * decay, 0.0)
    t = _tri_inv(a)
    egc = jnp.exp(gcum_w)
    u = _bnn(t, v * beta_w)
    w = _bnn(t, k_beta * egc)
    attn = jnp.where(incl, _bnt(q, k) * decay, 0.0)
    q_dec = q * egc
    g_last = jnp.sum(g_w, axis=-2, keepdims=True)
    k_dec = k * jnp.exp(g_last - gcum_w)
    v_new = u - _bnn(w, s)
    o = _bnn(q_dec, s) + _bnn(attn, v_new)
    s_new = s * jnp.exp(g_last) + _btn(k_dec, v_new)
    return o, s_new


def _conv_act(x, wrows, *, l2, scale):
    taps = len(wrows)
    y = None
    for j in range(taps):
        s = taps - 1 - j
        xs = x if s == 0 else _make_shift(s)(x)
        y = wrows[j] * xs if y is None else y + wrows[j] * xs
    y = _silu(y)
    if l2:
        y = y * lax.rsqrt(jnp.sum(y * y, axis=-1, keepdims=True) + EPS) * scale
    return y


def _merge_math(og, gg, od, dz, ga, gb, gn, dn):
    nsub = len(og)
    dv = nsub * og[0].shape[1]
    ssq = jnp.sum(og[0] * og[0], axis=-1, keepdims=True)
    for s in range(1, nsub):
        ssq = ssq + jnp.sum(og[s] * og[s], axis=-1, keepdims=True)
    r = lax.rsqrt(ssq * (1.0 / dv) + EPS)
    outs = []
    for s in range(nsub):
        a = og[s] * r * gn[s] * _silu(gg[s])
        b = _rms(od[s], dn) * _silu(dz[s])
        outs.append(_sigmoid(ga[s]) * a + _sigmoid(gb[s]) * b)
    return outs


def _pick(n, target, mult):
    best = None
    for d in range(mult, min(n, target) + 1, mult):
        if n % d == 0:
            best = d
    return best if best is not None else n


class _Stage:
    def __init__(self, inputs, out_shapes, n_sems, copies, aliases=None):
        self.inputs, self.out_shapes, self.n_sems, self.copies = list(inputs), list(out_shapes), n_sems, copies
        self.aliases = aliases or {}

    @property
    def sems(self):
        return [pltpu.SemaphoreType.DMA((self.n_sems,)), pltpu.SemaphoreType.DMA((self.n_sems,))]


def _host_stage(body, stage, n_in, n_out, grid):
    ci, co = len(stage.inputs), len(stage.out_shapes)

    def wrapped(*refs):
        ins, cins = refs[:n_in], refs[n_in:n_in + ci]
        outs, couts = refs[n_in + ci:n_in + ci + n_out], refs[n_in + ci + n_out:n_in + ci + n_out + co]
        scratch, sems = refs[n_in + ci + n_out + co:-2], refs[-2:]
        ids = [pl.program_id(d) for d in range(len(grid))]
        first, last = ids[0] == 0, ids[0] == grid[0] - 1
        for i, g in zip(ids[1:], grid[1:]):
            first, last = first & (i == 0), last & (i == g - 1)

        @pl.when(first)
        def _():
            for cp in stage.copies(cins, couts, *sems):
                cp.start()

        body(*ins, *outs, *scratch)

        @pl.when(last)
        def _():
            for cp in stage.copies(cins, couts, *sems):
                cp.wait()

    return wrapped


def _call(body, name, grid, in_specs, out_specs, out_shape, scratch, semantics, args, stage=None):
    if stage is None:
        return pl.pallas_call(body, name=name, grid=grid, in_specs=list(in_specs), out_specs=list(out_specs),
                              out_shape=list(out_shape), scratch_shapes=list(scratch), compiler_params=_params(semantics))(*args)
    n_in, n_out = len(in_specs), len(out_specs)
    return pl.pallas_call(
        _host_stage(body, stage, n_in, n_out, grid), name=name, grid=grid,
        in_specs=list(in_specs) + [ANY] * len(stage.inputs),
        out_specs=list(out_specs) + [ANY] * len(stage.out_shapes), out_shape=list(out_shape) + stage.out_shapes,
        scratch_shapes=list(scratch) + stage.sems,
        input_output_aliases={n_in + i: n_out + o for i, o in stage.aliases.items()},
        compiler_params=_params(("arbitrary",) * len(grid)),
    )(*args, *stage.inputs)


def _run_stage(stage, name):
    ci = len(stage.inputs)

    def body(*refs):
        cps = stage.copies(refs[:ci], refs[ci:-2], *refs[-2:])
        for cp in cps:
            cp.start()
        for cp in cps:
            cp.wait()

    return pl.pallas_call(body, name=name, in_specs=[ANY] * ci, out_specs=[ANY] * len(stage.out_shapes),
                          out_shape=stage.out_shapes, scratch_shapes=stage.sems,
                          input_output_aliases=dict(stage.aliases))(*stage.inputs)


def _matmul(a, b, form, out_dtypes, name, epilogue=None, extras=(), bm=1024, bn=1024, bk=2048,
            b_slots=False, out_slots=False, stage=None, m_cols=None):
    ns, c = (b.shape[0], b.shape[2]) if b_slots else (1, None)
    b2 = b.shape[1:] if b_slots else b.shape
    if form == 'nn':
        (M, K), (K2, N) = a.shape, (b2[0], b2[1] * ns)
    elif form == 'nt':
        (M, K), (N, K2) = a.shape, (b2[0], b2[1] * ns)
    else:
        (K, M), (K2, N) = a.shape, b2
    m0 = 0
    if m_cols is not None:
        m0, M = m_cols
    assert K == K2 and not (b_slots and form == 'tn') and (m_cols is None or form == 'tn'), (a.shape, b.shape, form)
    bm, bn, bk = _pick(M, bm, SUBLANES), _pick(N, bn, LANES), _pick(K, bk, LANES)
    assert m0 % bm == 0
    if b_slots:
        bn, bk = (_pick(c, bn, LANES), bk) if form == 'nn' else (bn, _pick(c, bk, LANES))
    if out_slots:
        oc = N // 4
        bn = _pick(oc, bn, LANES)
    nk = K // bk
    a_spec = pl.BlockSpec((bk, bm), lambda i, j, k: (k, m0 // bm + i)) if form == 'tn' else pl.BlockSpec((bm, bk), lambda i, j, k: (i, k))
    if b_slots and form == 'nn':
        per = c // bn
        b_spec = pl.BlockSpec((None, bk, bn), lambda i, j, k: (j // per, k, j % per))
    elif b_slots:
        per = c // bk
        b_spec = pl.BlockSpec((None, bn, bk), lambda i, j, k: (k // per, j, k % per))
    elif form == 'nt':
        b_spec = pl.BlockSpec((bn, bk), lambda i, j, k: (j, k))
    else:
        b_spec = pl.BlockSpec((bk, bn), lambda i, j, k: (k, j))
    o_spec = pl.BlockSpec((bm, bn), lambda i, j, k: (i, j))
    if out_slots:
        oper = oc // bn
        out_spec = pl.BlockSpec((None, bm, bn), lambda i, j, k: (j // oper, i, j % oper))
        out_shape = [jax.ShapeDtypeStruct((4, M, oc), d) for d in out_dtypes]
    else:
        out_spec = o_spec
        out_shape = [jax.ShapeDtypeStruct((M, N), d) for d in out_dtypes]
    ne, no = len(extras), len(out_dtypes)

    def finish(r, extra_refs, out_refs):
        outs = (r,) if epilogue is None else epilogue(r, *[e[...] for e in extra_refs])
        for ref, o in zip(out_refs, outs):
            ref[...] = o.astype(ref.dtype)

    def body_one(a_ref, b_ref, *rest):
        finish(_dot(a_ref[...].astype(BF16), b_ref[...].astype(BF16), form), rest[:ne], rest[ne:ne + no])

    def body_acc(a_ref, b_ref, *rest):
        extra_refs, out_refs, acc = rest[:ne], rest[ne:ne + no], rest[ne + no]
        k = pl.program_id(2)
        part = _dot(a_ref[...].astype(BF16), b_ref[...].astype(BF16), form)

        @pl.when(k == 0)
        def _():
            acc[...] = part

        @pl.when((k > 0) & (k < nk - 1))
        def _():
            acc[...] += part

        @pl.when(k == nk - 1)
        def _():
            finish(acc[...] + part, extra_refs, out_refs)

    return _call(body_one if nk == 1 else body_acc, name, (M // bm, N // bn, nk), [a_spec, b_spec] + [o_spec] * ne,
                 [out_spec] * no, out_shape, [] if nk == 1 else [pltpu.VMEM((bm, bn), F32)],
                 ("parallel", "parallel", "arbitrary"), (a, b, *extras), stage)


def _rowwise(fn, rows, consts, row_outs, acc_outs, name, bt=256, stage=None):
    T = rows[0].shape[0]
    bt = _pick(T, bt, SUBLANES)
    nr, nc, no, na = len(rows), len(consts), len(row_outs), len(acc_outs)

    def body(*refs):
        r_in, c_in = refs[:nr], refs[nr:nr + nc]
        r_out, a_out = refs[nr + nc:nr + nc + no], refs[nr + nc + no:]
        ro, ao = fn([r[...] for r in r_in], [c[...] for c in c_in])
        for ref, o in zip(r_out, ro):
            ref[...] = o.astype(ref.dtype)
        if na:
            @pl.when(pl.program_id(0) == 0)
            def _():
                for ref in a_out:
                    ref[...] = jnp.zeros_like(ref)
            for ref, o in zip(a_out, ao):
                ref[...] += o

    whole = lambda shp: pl.BlockSpec(shp, lambda i: (0,) * len(shp))
    return _call(
        body, name, (T // bt,),
        [pl.BlockSpec((bt, r.shape[1]), lambda i: (i, 0)) for r in rows] + [whole(c.shape) for c in consts],
        [pl.BlockSpec((bt, w), lambda i: (i, 0)) for w, _ in row_outs] + [whole(s) for s in acc_outs],
        [jax.ShapeDtypeStruct((T, w), d) for w, d in row_outs] + [jax.ShapeDtypeStruct(s, F32) for s in acc_outs],
        [], ("arbitrary",), (*rows, *consts), stage)


def _rmsnorm_fwd(x, g, name, stage=None):
    return _rowwise(lambda r, c: ([_rms(r[0], c[0])], []), [x], [g], [(x.shape[1], BF16)], [], name, stage=stage)


def _rmsnorm_bwd_add(x, g, dh, dres, name):
    D = x.shape[1]

    def fn(r, c):
        _, vjp = jax.vjp(_rms, r[0], c[0])
        dx, dg = vjp(r[1])
        dx = dx + r[2]
        return [dx, dx], [dg]
    return _rowwise(fn, [x, dh, dres], [g], [(D, F32), (D, BF16)], [(1, D)], name)


def _loss_fwd_bwd(x3, g, target, name):
    D = x3.shape[1]

    def fn(r, c):
        def row_loss(x, gain):
            err = _rms(x, gain) - r[1]
            return 0.5 * jnp.mean(err * err, axis=-1, keepdims=True)
        lrow, vjp = jax.vjp(row_loss, r[0], c[0])
        dx, dg = vjp(jnp.ones_like(lrow))
        tile = jnp.broadcast_to(jnp.sum(lrow, axis=0, keepdims=True), (SUBLANES, LANES))
        return [dx], [tile, dg]
    return _rowwise(fn, [x3, target], [g], [(D, F32)], [(SUBLANES, LANES), (1, D)], name)


def _ple_bwd(dx3, gp, pp, name):
    D = dx3.shape[1]

    def fn(r, c):
        s = _sigmoid(r[1])
        return [r[0] * r[2] * s * (1.0 - s), r[0] * s], []
    return _rowwise(fn, [dx3, gp, pp], [], [(D, BF16), (D, BF16)], [], name)


def _adamw_math(w, g, m, v):
    nm = ADAM_B1 * m + (1.0 - ADAM_B1) * g
    nv = ADAM_B2 * v + (1.0 - ADAM_B2) * (g * g)
    m_hat = nm / (1.0 - ADAM_B1 ** ADAM_STEP)
    v_hat = nv / (1.0 - ADAM_B2 ** ADAM_STEP)
    return -ADAM_LR * (m_hat / (jnp.sqrt(v_hat) + ADAM_EPS) + ADAM_WD * w), nm, nv


def _adamw(w, g, m, v, name, with_grad=False):
    R, C = w.shape[0], w.shape[-1]
    lanes = -(-C // LANES) * LANES
    if w.ndim == 2:
        bt = _pick(R, max(SUBLANES, (1 << 18) // lanes // SUBLANES * SUBLANES), SUBLANES)
        spec = pl.BlockSpec((bt, C), lambda i: (i, 0))
    else:
        bt = _pick(R, max(1, (1 << 18) // lanes), 1)
        spec = pl.BlockSpec((bt, 1, C), lambda i: (i, 0, 0))

    def body(w_ref, g_ref, m_ref, v_ref, d_ref, nm_ref, nv_ref, *g_out):
        d_ref[...], nm_ref[...], nv_ref[...] = _adamw_math(w_ref[...], g_ref[...], m_ref[...], v_ref[...])
        for ref in g_out:
            ref[...] = g_ref[...]

    n_out = 4 if with_grad else 3
    return pl.pallas_call(
        body, name=name, grid=(R // bt,), in_specs=[spec] * 4, out_specs=[spec] * n_out,
        out_shape=[jax.ShapeDtypeStruct(w.shape, F32)] * n_out, compiler_params=_params(("parallel",)),
    )(w, g, m, v)


def _adamw_small(ws, gs, ms, vs):
    n = len(ws)

    def body(*refs):
        for i in range(n):
            d, nm, nv = _adamw_math(refs[i][...], refs[n + i][...], refs[2 * n + i][...], refs[3 * n + i][...])
            refs[4 * n + i][...], refs[5 * n + i][...], refs[6 * n + i][...] = d, nm, nv

    VMEM = pl.BlockSpec(memory_space=pltpu.VMEM)
    shapes = [jax.ShapeDtypeStruct(w.shape, F32) for w in ws]
    outs = pl.pallas_call(body, name="adamw_small", in_specs=[VMEM] * (4 * n), out_specs=[VMEM] * (3 * n),
                          out_shape=shapes * 3)(*ws, *gs, *ms, *vs)
    return outs[:n], outs[n:2 * n], outs[2 * n:]


def _gla_fwd(z_big, z_small, w2h, gbh, Bl, S, D, stage=None):
    NC, dk, dv, HB = S // CHUNK, D // (2 * GLA_HEADS), D // GLA_HEADS, GLA_HEADS_PER_STEP
    HG = GLA_HEADS // HB
    chains = [(hh, bb) for hh in range(HB) for bb in range(Bl)]
    G = len(chains)
    fn = functools.partial(_gla_chunk, scale=dk ** -0.5)

    def body(q, k, v, z, w2, gb, o_ref, stall_ref, st):
        n, g = pl.program_id(0), pl.program_id(1)

        @pl.when(n == 0)
        def _():
            st[g] = jnp.zeros((G, dv, dk), F32)
        s0 = st[g]
        stall_ref[...] = s0.reshape(HB, Bl, dv, dk)
        qk = lambda r: jnp.stack([r[bb, :, hh * dk:(hh + 1) * dk] for hh, bb in chains])
        o, s_new = fn(qk(q), qk(k), jnp.stack([v[bb, :, hh * dv:(hh + 1) * dv] for hh, bb in chains]),
                      jnp.stack([z[bb] for _, bb in chains]), jnp.stack([w2[hh] for hh, _ in chains]),
                      jnp.stack([gb[hh] for hh, _ in chains]), s0)
        for i, (hh, bb) in enumerate(chains):
            o_ref[bb, :, hh * dv:(hh + 1) * dv] = o[i]
        st[g] = s_new

    return _call(
        body, "gla_fwd", (NC, HG),
        [pl.BlockSpec((Bl, CHUNK, HB * dk), lambda n, g: (0, n, g)),
         pl.BlockSpec((Bl, CHUNK, HB * dk), lambda n, g: (0, n, HG + g)),
         pl.BlockSpec((Bl, CHUNK, HB * dv), lambda n, g: (0, n, HG + g)),
         pl.BlockSpec((Bl, CHUNK, ZS), lambda n, g: (0, n, 0)),
         pl.BlockSpec((HB, ZS, dk), lambda n, g: (g, 0, 0)),
         pl.BlockSpec((HB, 1, dk), lambda n, g: (g, 0, 0))],
        [pl.BlockSpec((Bl, CHUNK, HB * dv), lambda n, g: (0, n, g)),
         pl.BlockSpec((HB, Bl, None, dv, dk), lambda n, g: (g, 0, n, 0, 0))],
        [jax.ShapeDtypeStruct((Bl, S, D), F32), jax.ShapeDtypeStruct((GLA_HEADS, Bl, NC, dv, dk), F32)],
        [pltpu.VMEM((HG, G, dv, dk), F32)], ("arbitrary", "arbitrary"), (z_big, z_big, z_big, z_small, w2h, gbh), stage)


def _gla_bwd(z_big, z_small, w2h, gbh, st_all, do, Bl, S, D):
    NC, dk, dv, HB = S // CHUNK, D // (2 * GLA_HEADS), D // GLA_HEADS, GLA_HEADS_PER_STEP
    HG = GLA_HEADS // HB
    chains = [(hh, bb) for hh in range(HB) for bb in range(Bl)]
    G = len(chains)
    fn = functools.partial(_gla_chunk, scale=dk ** -0.5)

    def body(q, k, v, z, w2, gb, st0, do_ref, dq_ref, dk_ref, dv_ref, dzs_ref, dw2_ref, dgb_ref, dst):
        n, g = pl.program_id(0), pl.program_id(1)

        @pl.when(n == 0)
        def _():
            dst[g] = jnp.zeros((G, dv, dk), F32)

        @pl.when((n == 0) & (g == 0))
        def _():
            dw2_ref[...] = jnp.zeros_like(dw2_ref)
            dgb_ref[...] = jnp.zeros_like(dgb_ref)

        qk = lambda r: jnp.stack([r[bb, :, hh * dk:(hh + 1) * dk] for hh, bb in chains])
        vv = lambda r: jnp.stack([r[bb, :, hh * dv:(hh + 1) * dv] for hh, bb in chains])
        _, vjp = jax.vjp(fn, qk(q), qk(k), vv(v), jnp.stack([z[bb] for _, bb in chains]),
                         jnp.stack([w2[hh] for hh, _ in chains]), jnp.stack([gb[hh] for hh, _ in chains]),
                         st0[...].reshape(G, dv, dk))
        dq, dkk, dvv, dzs, dw2, dgb, dst0 = vjp((vv(do_ref), dst[g]))
        for i, (hh, bb) in enumerate(chains):
            dq_ref[bb, :, hh * dk:(hh + 1) * dk] = dq[i].astype(dq_ref.dtype)
            dk_ref[bb, :, hh * dk:(hh + 1) * dk] = dkk[i].astype(dk_ref.dtype)
            dv_ref[bb, :, hh * dv:(hh + 1) * dv] = dvv[i].astype(dv_ref.dtype)
            dw2_ref[g * HB + hh] += dw2[i]
            dgb_ref[g * HB + hh] += dgb[i]
        for bb in range(Bl):
            tot = sum(dzs[i] for i, (_, b2) in enumerate(chains) if b2 == bb)

            @pl.when(g == 0)
            def _():
                dzs_ref[bb] = tot

            @pl.when(g > 0)
            def _():
                dzs_ref[bb] += tot
        dst[g] = dst0

    rn = lambda n: NC - 1 - n
    return pl.pallas_call(
        body, name="gla_bwd", grid=(NC, HG),
        in_specs=[pl.BlockSpec((Bl, CHUNK, HB * dk), lambda n, g: (0, rn(n), g)),
                  pl.BlockSpec((Bl, CHUNK, HB * dk), lambda n, g: (0, rn(n), HG + g)),
                  pl.BlockSpec((Bl, CHUNK, HB * dv), lambda n, g: (0, rn(n), HG + g)),
                  pl.BlockSpec((Bl, CHUNK, ZS), lambda n, g: (0, rn(n), 0)),
                  pl.BlockSpec((HB, ZS, dk), lambda n, g: (g, 0, 0)),
                  pl.BlockSpec((HB, 1, dk), lambda n, g: (g, 0, 0)),
                  pl.BlockSpec((HB, Bl, None, dv, dk), lambda n, g: (g, 0, rn(n), 0, 0)),
                  pl.BlockSpec((Bl, CHUNK, HB * dv), lambda n, g: (0, rn(n), g))],
        out_specs=[pl.BlockSpec((Bl, CHUNK, HB * dk), lambda n, g: (0, rn(n), g)),
                   pl.BlockSpec((Bl, CHUNK, HB * dk), lambda n, g: (0, rn(n), g)),
                   pl.BlockSpec((Bl, CHUNK, HB * dv), lambda n, g: (0, rn(n), g)),
                   pl.BlockSpec((Bl, CHUNK, ZS), lambda n, g: (0, rn(n), 0)),
                   pl.BlockSpec((GLA_HEADS, ZS, dk), lambda n, g: (0, 0, 0)),
                   pl.BlockSpec((GLA_HEADS, 1, dk), lambda n, g: (0, 0, 0))],
        out_shape=[jax.ShapeDtypeStruct((Bl, S, D // 2), BF16), jax.ShapeDtypeStruct((Bl, S, D // 2), BF16),
                   jax.ShapeDtypeStruct((Bl, S, D), BF16), jax.ShapeDtypeStruct((Bl, S, ZS), F32),
                   jax.ShapeDtypeStruct((GLA_HEADS, ZS, dk), F32), jax.ShapeDtypeStruct((GLA_HEADS, 1, dk), F32)],
        scratch_shapes=[pltpu.VMEM((HG, G, dv, dk), F32)],
        compiler_params=_params(("arbitrary", "arbitrary")),
    )(z_big, z_big, z_big, z_small, w2h, gbh, st_all, do)


def _conv_fwd(z_big, conv_w, grp, Bl, S, D):
    d = D // DN_HEADS
    l2, scale = grp < 2, (d ** -0.5 if grp == 0 else 1.0)
    x_blk0 = (3 * D + grp * D) // d

    def body(x_ref, w_ref, o_ref):
        wrows = [w_ref[j:j + 1, :] for j in range(DN_CONV)]
        o_ref[...] = _conv_act(x_ref[...], wrows, l2=l2, scale=scale)

    return pl.pallas_call(
        body, name=f"conv_fwd{grp}", grid=(Bl, DN_HEADS),
        in_specs=[pl.BlockSpec((S, d), lambda b, j: (b, x_blk0 + j)),
                  pl.BlockSpec((DN_CONV, d), lambda b, j: (0, grp * DN_HEADS + j))],
        out_specs=pl.BlockSpec((S, d), lambda b, j: (b, j)),
        out_shape=jax.ShapeDtypeStruct((Bl * S, D), F32),
        compiler_params=_params(("parallel", "parallel")),
    )(z_big, conv_w)


def _conv_bwd(z_big, conv_w, dact, grp, Bl, S, D):
    d = D // DN_HEADS
    l2, scale = grp < 2, (d ** -0.5 if grp == 0 else 1.0)
    x_blk0 = (3 * D + grp * D) // d

    def body(x_ref, w_ref, g_ref, dx_ref, dw_ref):
        @pl.when(pl.program_id(1) == 0)
        def _():
            dw_ref[...] = jnp.zeros_like(dw_ref)
        wrows = [w_ref[j:j + 1, :] for j in range(DN_CONV)]
        _, vjp = jax.vjp(lambda x, wr: _conv_act(x, wr, l2=l2, scale=scale), x_ref[...], wrows)
        dx, dwr = vjp(g_ref[...])
        dx_ref[...] = dx.astype(dx_ref.dtype)
        for j in range(DN_CONV):
            dw_ref[j:j + 1, :] += dwr[j]

    return pl.pallas_call(
        body, name=f"conv_bwd{grp}", grid=(DN_HEADS, Bl),
        in_specs=[pl.BlockSpec((S, d), lambda j, b: (b, x_blk0 + j)),
                  pl.BlockSpec((DN_CONV, d), lambda j, b: (0, grp * DN_HEADS + j)),
                  pl.BlockSpec((S, d), lambda j, b: (b, j))],
        out_specs=[pl.BlockSpec((S, d), lambda j, b: (b, j)), pl.BlockSpec((DN_CONV, d), lambda j, b: (0, j))],
        out_shape=[jax.ShapeDtypeStruct((Bl * S, D), BF16), jax.ShapeDtypeStruct((DN_CONV, D), F32)],
        compiler_params=_params(("arbitrary", "arbitrary")),
    )(z_big, conv_w, dact)


def _lane_column(zb, lane, width):
    pick = lax.broadcasted_iota(jnp.int32, zb.shape, 1) == lane
    return jnp.broadcast_to(jnp.sum(jnp.where(pick, zb, 0.0), axis=-1, keepdims=True), (zb.shape[0], width))


def _dn_fwd(qa, ka, va, z_small, alog, dtb, Bl, S, D, stage=None):
    NC, d, HB = S // CHUNK, D // DN_HEADS, DN_HEADS_PER_STEP
    HG = DN_HEADS // HB
    chains = [(hh, bb) for hh in range(HB) for bb in range(Bl)]
    G = len(chains)

    def body(q, k, v, z, al, dt, o_ref, sall_ref, st):
        n, g = pl.program_id(0), pl.program_id(1)

        @pl.when(n == 0)
        def _():
            st[g] = jnp.zeros((G, d, d), F32)
        tok_in = lambda r: jnp.stack([r[bb, :, hh * d:(hh + 1) * d] for hh, bb in chains])
        head_in = lambda r: jnp.stack([r[hh] for hh, _ in chains])
        gate_in = lambda lane0: jnp.stack([_lane_column(z[bb], lane0 + g * HB + hh, d) for hh, bb in chains])
        s0 = st[g]
        sall_ref[...] = s0.reshape(HB, Bl, d, d)
        o, s_new = _dn_chunk(tok_in(q), tok_in(k), tok_in(v), gate_in(A_LANE), gate_in(B_LANE), head_in(al), head_in(dt), s0)
        for i, (hh, bb) in enumerate(chains):
            o_ref[bb, :, hh * d:(hh + 1) * d] = o[i]
        st[g] = s_new

    tok = pl.BlockSpec((Bl, CHUNK, HB * d), lambda n, g: (0, n, g))
    per_head = pl.BlockSpec((HB, 1, d), lambda n, g: (g, 0, 0))
    return _call(
        body, "dn_fwd", (NC, HG),
        [tok, tok, tok, pl.BlockSpec((Bl, CHUNK, ZS), lambda n, g: (0, n, 0)), per_head, per_head],
        [tok, pl.BlockSpec((HB, Bl, None, d, d), lambda n, g: (g, 0, n, 0, 0))],
        [jax.ShapeDtypeStruct((Bl, S, D), F32), jax.ShapeDtypeStruct((DN_HEADS, Bl, NC, d, d), F32)],
        [pltpu.VMEM((HG, G, d, d), F32)], ("arbitrary", "arbitrary"), (qa, ka, va, z_small, alog, dtb), stage)


def _dn_bwd(qa, ka, va, z_small, alog, dtb, s_all, do, dzs_gla, Bl, S, D, stage=None):
    NC, d, HB = S // CHUNK, D // DN_HEADS, DN_HEADS_PER_STEP
    HG = DN_HEADS // HB
    chains = [(hh, bb) for hh in range(HB) for bb in range(Bl)]
    G = len(chains)

    def lanesum(t):
        return jnp.sum(t, axis=-1, keepdims=True)

    def body(q, k, v, z, al, dt, s0_ref, do_ref, dzg_ref, dq_ref, dk_ref, dv_ref, dzs_ref, dal_ref, ddt_ref, dst):
        n, g = pl.program_id(0), pl.program_id(1)

        @pl.when(n == 0)
        def _():
            dst[g] = jnp.zeros((G, d, d), F32)

        @pl.when((n == 0) & (g == 0))
        def _():
            dal_ref[...] = jnp.zeros_like(dal_ref)
            ddt_ref[...] = jnp.zeros_like(ddt_ref)

        tok_in = lambda r: jnp.stack([r[bb, :, hh * d:(hh + 1) * d] for hh, bb in chains])
        head_in = lambda r: jnp.stack([r[hh] for hh, _ in chains])
        gate_in = lambda lane0: jnp.stack([_lane_column(z[bb], lane0 + g * HB + hh, d) for hh, bb in chains])
        _, vjp = jax.vjp(_dn_chunk, tok_in(q), tok_in(k), tok_in(v), gate_in(A_LANE), gate_in(B_LANE), head_in(al),
                         head_in(dt), s0_ref[...].reshape(G, d, d))
        dq, dkk, dvv, da, db, dal, ddt, ds0 = vjp((tok_in(do_ref), dst[g]))
        da, db = lanesum(da), lanesum(db)
        dal = jnp.broadcast_to(lanesum(dal), (G, 1, d))
        ddt = jnp.broadcast_to(lanesum(ddt), (G, 1, d))
        lane = lax.broadcasted_iota(jnp.int32, (CHUNK, ZS), 1)
        for bb in range(Bl):
            part = jnp.zeros((CHUNK, ZS), F32)
            for i, (hh, b2) in enumerate(chains):
                if b2 == bb:
                    h = g * HB + hh
                    part = part + jnp.where(lane == A_LANE + h, da[i], 0.0) + jnp.where(lane == B_LANE + h, db[i], 0.0)

            @pl.when(g == 0)
            def _():
                dzs_ref[bb] = jnp.where(lane < LOWRANK, dzg_ref[bb], 0.0) + part

            @pl.when(g > 0)
            def _():
                dzs_ref[bb] += part
        for i, (hh, bb) in enumerate(chains):
            cols = slice(hh * d, (hh + 1) * d)
            dq_ref[bb, :, cols] = dq[i]
            dk_ref[bb, :, cols] = dkk[i]
            dv_ref[bb, :, cols] = dvv[i]
            dal_ref[g * HB + hh] += dal[i]
            ddt_ref[g * HB + hh] += ddt[i]
        dst[g] = ds0

    rn = lambda n: NC - 1 - n
    tok = pl.BlockSpec((Bl, CHUNK, HB * d), lambda n, g: (0, rn(n), g))
    zsb = pl.BlockSpec((Bl, CHUNK, ZS), lambda n, g: (0, rn(n), 0))
    per_head = pl.BlockSpec((HB, 1, d), lambda n, g: (g, 0, 0))
    all_heads = pl.BlockSpec((DN_HEADS, 1, d), lambda n, g: (0, 0, 0))
    tok_shape = jax.ShapeDtypeStruct((Bl, S, D), F32)
    head_shape = jax.ShapeDtypeStruct((DN_HEADS, 1, d), F32)
    return _call(
        body, "dn_bwd", (NC, HG),
        [tok, tok, tok, zsb, per_head, per_head,
         pl.BlockSpec((HB, Bl, None, d, d), lambda n, g: (g, 0, rn(n), 0, 0)), tok, zsb],
        [tok, tok, tok, zsb, all_heads, all_heads],
        [tok_shape, tok_shape, tok_shape, jax.ShapeDtypeStruct((Bl, S, ZS), F32), head_shape, head_shape],
        [pltpu.VMEM((HG, G, d, d), F32)], ("arbitrary", "arbitrary"),
        (qa, ka, va, z_small, alog, dtb, s_all, do, dzs_gla), stage)


def _merge_specs(D, bt):
    dv, w = D // GLA_HEADS, D // DN_HEADS
    col = lambda off: pl.BlockSpec((bt, dv), lambda i, h: (i, off // dv + h))
    return dv, w, col


def _merge_load(refs, nsub, w):
    return [[r[:, s * w:(s + 1) * w] for s in range(nsub)] for r in refs]


def _merge_fwd(o_gla, o_dn, z_big, gla_norm, dn_norm, D, bt=256, stage=None):
    T = o_gla.shape[0]
    bt = _pick(T, bt, SUBLANES)
    dv, w, col = _merge_specs(D, bt)
    nsub = dv // w

    def body(og, gg, od, dz, ga, gb, gn, dn, out):
        ogl, ggl, odl, dzl, gal, gbl = _merge_load([og, gg, od, dz, ga, gb], nsub, w)
        gnl = [gn[:, s * w:(s + 1) * w] for s in range(nsub)]
        outs = _merge_math(ogl, ggl, odl, dzl, gal, gbl, gnl, dn[...])
        for s in range(nsub):
            out[:, s * w:(s + 1) * w] = outs[s].astype(out.dtype)

    return _call(
        body, "merge_fwd", (T // bt, GLA_HEADS),
        [col(0), col(2 * D), col(0), col(6 * D), col(7 * D), col(8 * D),
         pl.BlockSpec((1, dv), lambda i, h: (0, 0)), pl.BlockSpec((1, w), lambda i, h: (0, 0))],
        [col(0)], [jax.ShapeDtypeStruct((T, D), BF16)], [], ("parallel", "parallel"),
        (o_gla, z_big, o_dn, z_big, z_big, z_big, gla_norm, dn_norm), stage)


def _merge_bwd(o_gla, o_dn, z_big, gla_norm, dn_norm, dmix, D, bt=256):
    T = o_gla.shape[0]
    bt = _pick(T, bt, SUBLANES)
    dv, w, col = _merge_specs(D, bt)
    nsub = dv // w

    def body(og, gg, od, dz, ga, gb, gn, dn, dm, dog, dgg, dod, ddz, dga, dgb, dgn, ddn):
        @pl.when((pl.program_id(0) == 0) & (pl.program_id(1) == 0))
        def _():
            dgn[...] = jnp.zeros_like(dgn)
            ddn[...] = jnp.zeros_like(ddn)

        ogl, ggl, odl, dzl, gal, gbl, dml = _merge_load([og, gg, od, dz, ga, gb, dm], nsub, w)
        gnl = [gn[:, s * w:(s + 1) * w] for s in range(nsub)]
        _, vjp = jax.vjp(_merge_math, ogl, ggl, odl, dzl, gal, gbl, gnl, dn[...])
        g_og, g_gg, g_od, g_dz, g_ga, g_gb, g_gn, g_dn = vjp(dml)
        for s in range(nsub):
            sl = slice(s * w, (s + 1) * w)
            dog[:, sl] = g_og[s]
            dgg[:, sl] = g_gg[s].astype(dgg.dtype)
            dod[:, sl] = g_od[s]
            ddz[:, sl] = g_dz[s].astype(ddz.dtype)
            dga[:, sl] = g_ga[s].astype(dga.dtype)
            dgb[:, sl] = g_gb[s].astype(dgb.dtype)
            dgn[:, sl] += g_gn[s]
        ddn[...] += g_dn

    f32s, bf16s = jax.ShapeDtypeStruct((T, D), F32), jax.ShapeDtypeStruct((T, D), BF16)
    return pl.pallas_call(
        body, name="merge_bwd", grid=(T // bt, GLA_HEADS),
        in_specs=[col(0), col(2 * D), col(0), col(6 * D), col(7 * D), col(8 * D),
                  pl.BlockSpec((1, dv), lambda i, h: (0, 0)), pl.BlockSpec((1, w), lambda i, h: (0, 0)), col(0)],
        out_specs=[col(0)] * 6 + [pl.BlockSpec((1, dv), lambda i, h: (0, 0)), pl.BlockSpec((1, w), lambda i, h: (0, 0))],
        out_shape=[f32s, bf16s, f32s, bf16s, bf16s, bf16s,
                   jax.ShapeDtypeStruct((1, dv), F32), jax.ShapeDtypeStruct((1, w), F32)],
        compiler_params=_params(("arbitrary", "arbitrary")),
    )(o_gla, z_big, o_dn, z_big, z_big, z_big, gla_norm, dn_norm, dmix)


def _place():
    return lax.axis_index("x"), lax.axis_index("y"), lax.axis_index("c")


def _other_chips(x, y):
    return [(1 - x, y), (x, 1 - y), (1 - x, 1 - y)]


def _rcopy(src, dst, send_sem, recv_sem, dev):
    return pltpu.make_async_remote_copy(src_ref=src, dst_ref=dst, send_sem=send_sem, recv_sem=recv_sem,
                                        device_id=dev, device_id_type=MESH)


ANY = pl.BlockSpec(memory_space=pl.ANY)


ROWS, COLS = 'rows', 'cols'


def _half(ref, hc, by, lead=()):
    shape = ref.shape[len(lead):]
    if by == ROWS:
        rh = shape[0] // 2
        idx = (pl.ds(pl.multiple_of(hc * rh, 16), rh),) + (slice(None),) * (len(shape) - 1)
    else:
        ch = shape[-1] // 2
        idx = (slice(None),) * (len(shape) - 1) + (pl.ds(pl.multiple_of(hc * ch, LANES), ch),)
    return ref.at[(*lead, *idx)]


def _half_shape(shape, by):
    return (shape[0] // 2,) + tuple(shape[1:]) if by == ROWS else tuple(shape[:-1]) + (shape[-1] // 2,)


def _gather_ici(shards, by):
    nw = len(shards)

    def copies(srcs, outs, send_sems, recv_sems):
        x, y, c = _place()
        return [_rcopy(_half(srcs[w], c, by[w]), _half(outs[w], c, by[w], (2 * x + y,)),
                       send_sems.at[3 * w + k], recv_sems.at[3 * w + k], (px, py, c))
                for w in range(nw) for k, (px, py) in enumerate(_other_chips(x, y))]

    return _Stage(shards, [jax.ShapeDtypeStruct((4,) + s.shape, s.dtype) for s in shards], 3 * nw, copies)


def _gather_neighbours(shards, by):
    nw = len(shards)

    def copies(srcs, outs, send_sems, recv_sems):
        x, y, c = _place()
        return [_rcopy(_half(srcs[w], c, by[w]), _half(outs[w], c, by[w], (2 * x + y,)),
                       send_sems.at[2 * w + k], recv_sems.at[2 * w + k], (px, py, c))
                for w in range(nw) for k, (px, py) in enumerate(_other_chips(x, y)[:2])]

    return _Stage(shards, [jax.ShapeDtypeStruct((4,) + s.shape, s.dtype) for s in shards], 2 * nw, copies)


def _gather_relay(gathered):
    nw = len(gathered)

    def copies(srcs, outs, send_sems, recv_sems):
        x, y, c = _place()
        cps = []
        for w in range(nw):
            _, n, cols = gathered[w].shape
            cut, ch = n // 2 // 16 * 16, cols // 2
            lanes = pl.ds(pl.multiple_of(c * ch, LANES), ch)
            via = [(2 * (1 - x) + y, pl.ds(0, cut), (x, 1 - y, c)),
                   (2 * x + (1 - y), pl.ds(cut, n - cut), (1 - x, y, c))]
            for k, (slot, rows, dev) in enumerate(via):
                cps.append(_rcopy(srcs[w].at[slot, rows, lanes], outs[w].at[slot, rows, lanes],
                                  send_sems.at[2 * w + k], recv_sems.at[2 * w + k], dev))
        return cps

    return _Stage(gathered, [jax.ShapeDtypeStruct(g.shape, g.dtype) for g in gathered], 2 * nw, copies,
                  aliases={w: w for w in range(nw)})


def _gather_pass(gathered, by):
    nw = len(gathered)

    def copies(srcs, outs, send_sems, recv_sems):
        x, y, c = _place()
        cps = []
        for w in range(nw):
            for k, (px, py) in enumerate(_other_chips(x, y)):
                slot = (2 * px + py,)
                cps.append(_rcopy(_half(srcs[w], c, by[w], slot), _half(outs[w], c, by[w], slot),
                                  send_sems.at[3 * w + k], recv_sems.at[3 * w + k], (x, y, 1 - c)))
        return cps

    return _Stage(gathered, [jax.ShapeDtypeStruct(g.shape, g.dtype) for g in gathered], 3 * nw, copies,
                  aliases={w: w for w in range(nw)})


def _pair_exchange(ps, by):
    nw = len(ps)

    def copies(srcs, outs, send_sems, recv_sems):
        x, y, c = _place()
        return [_rcopy(_half(srcs[w], 1 - c, by[w], (slice(None),) * (ps[w].ndim - 2)), outs[w], send_sems.at[w], recv_sems.at[w],
                       (x, y, 1 - c)) for w in range(nw)]

    return _Stage(ps, [jax.ShapeDtypeStruct(p.shape[:-2] + _half_shape(p.shape[-2:], b), p.dtype) for p, b in zip(ps, by)],
                  nw, copies)


def _sum_blocks(half_shape, by):
    rh, ch = half_shape
    lanes = -(-ch // LANES) * LANES
    bt = _pick(rh, max(16, (3 << 18) // lanes // 16 * 16), 16)
    if by == ROWS:
        nb = rh // bt
        return (bt, ch), nb, (lambda i: (i, 0)), (lambda i, c: (c * nb + i, 0))
    if rh % bt == 0 and bt % 16 == 0:
        return (bt, ch), rh // bt, (lambda i: (i, 0)), (lambda i, c: (i, c))
    bc = _pick(ch, max(LANES, (5 << 18) // rh // LANES * LANES), LANES)
    nb = ch // bc
    return (rh, bc), nb, (lambda i: (0, i)), (lambda i, c: (0, c * nb + i))


def _pair_sum(p, got, c_idx, name, by=ROWS):
    lead, hs = got.shape[0], got.shape[1:]
    blk, nb, pos, pos_whole = _sum_blocks(hs, by)

    def body(c_ref, a, b, of, ob):
        s = a[...] + b[...]
        of[...] = s
        ob[...] = s.astype(BF16)

    spec = pl.BlockSpec((None,) + blk, lambda j, i, c_ref: (j,) + pos(i))
    return pl.pallas_call(
        body, name=name,
        grid_spec=pltpu.PrefetchScalarGridSpec(
            num_scalar_prefetch=1, grid=(lead, nb),
            in_specs=[pl.BlockSpec((None,) + blk, lambda j, i, c_ref: (j,) + pos_whole(i, c_ref[0])), spec],
            out_specs=[spec, spec]),
        out_shape=[jax.ShapeDtypeStruct((lead,) + hs, F32), jax.ShapeDtypeStruct((lead,) + hs, BF16)],
        compiler_params=_params(("parallel", "parallel")),
    )(c_idx, p, got)


def _chip_scatter(qbs):
    nw = len(qbs)

    def copies(srcs, outs, send_sems, recv_sems):
        x, y, c = _place()
        return [_rcopy(srcs[w].at[2 * px + py], outs[w].at[k], send_sems.at[3 * w + k], recv_sems.at[3 * w + k], (px, py, c))
                for w in range(nw) for k, (px, py) in enumerate(_other_chips(x, y))]

    return _Stage(qbs, [jax.ShapeDtypeStruct((3,) + q.shape[1:], q.dtype) for q in qbs], 3 * nw, copies)


def _final_sum(qf, got, me_idx, name, by=ROWS):
    hs = qf.shape[1:]
    blk, nb, pos, _ = _sum_blocks(hs, by)

    def body(me_ref, a, b, o):
        o[...] = ((a[...] + b[0].astype(F32)) + b[1].astype(F32)) + b[2].astype(F32)

    return pl.pallas_call(
        body, name=name,
        grid_spec=pltpu.PrefetchScalarGridSpec(
            num_scalar_prefetch=1, grid=(nb,),
            in_specs=[pl.BlockSpec((None,) + blk, lambda i, me_ref: (me_ref[0],) + pos(i)),
                      pl.BlockSpec((3,) + blk, lambda i, me_ref: (0,) + pos(i))],
            out_specs=pl.BlockSpec(blk, lambda i, me_ref: pos(i))),
        out_shape=jax.ShapeDtypeStruct(hs, F32),
        compiler_params=_params(("parallel",)),
    )(me_idx, qf, got)


def _pair_allgather(halves, by):
    nw = len(halves)
    whole = [(2 * h.shape[0], h.shape[1]) if b == ROWS else h.shape for h, b in zip(halves, by)]

    def copies(srcs, outs, send_sems, recv_sems):
        x, y, c = _place()
        there = lambda w: _half(outs[w], c, ROWS) if by[w] == ROWS else outs[w]
        return [_rcopy(srcs[w], there(w), send_sems.at[w], recv_sems.at[w], (x, y, 1 - c)) for w in range(nw)]

    return _Stage(halves, [jax.ShapeDtypeStruct(s, h.dtype) for s, h in zip(whole, halves)], nw, copies)


class _SemaphoreWindow:
    def __init__(self, ref, off):
        self.ref, self.off = ref, off

    @property
    def at(self):
        return self

    def __getitem__(self, i):
        return self.ref.at[self.off + i]


def _both(a, b):
    na, ma = len(a.inputs), len(a.out_shapes)

    def copies(ins, outs, send_sems, recv_sems):
        return (a.copies(ins[:na], outs[:ma], send_sems, recv_sems) +
                b.copies(ins[na:], outs[ma:], _SemaphoreWindow(send_sems, a.n_sems), _SemaphoreWindow(recv_sems, a.n_sems)))

    return _Stage(a.inputs + b.inputs, a.out_shapes + b.out_shapes, a.n_sems + b.n_sems, copies,
                  aliases={**a.aliases, **{na + i: ma + o for i, o in b.aliases.items()}})


def _small_exchange(items, out_shapes, finish, name):
    n = len(items)
    offs, rows = [], 0
    for it in items:
        offs.append(rows)
        rows += it.shape[0]
    rows = -(-rows // SUBLANES) * SUBLANES
    width = -(-max(it.shape[1] for it in items) // LANES) * LANES
    VMEM = pl.BlockSpec(memory_space=pltpu.VMEM)

    def body(*refs):
        ins, outs = refs[:n], refs[n:n + len(out_shapes)]
        buf, send_sems, recv_sems = refs[n + len(out_shapes):]
        x, y, c = _place()
        me = 4 * x + 2 * y + c
        flip = lambda v, f: (1 - v) if f else v
        peers = [(flip(x, r >> 2 & 1), flip(y, r >> 1 & 1), flip(c, r & 1)) for r in range(1, 8)]
        buf[me] = jnp.zeros((rows, width), F32)
        for it, off, ref in zip(items, offs, ins):
            buf[me, off:off + it.shape[0], 0:it.shape[1]] = ref[...]
        cps = [_rcopy(buf.at[me], buf.at[me], send_sems.at[k], recv_sems.at[k], dev) for k, dev in enumerate(peers)]
        for cp in cps:
            cp.start()
        for k, (px, py, pc) in enumerate(peers):
            slot = buf.at[4 * px + 2 * py + pc]
            _rcopy(slot, slot, send_sems.at[k], recv_sems.at[k], (px, py, pc)).wait_recv()
        for cp in cps:
            cp.wait_send()
        finish(buf, offs, outs)

    return pl.pallas_call(
        body, name=name, in_specs=[VMEM] * n, out_specs=[VMEM] * len(out_shapes),
        out_shape=[jax.ShapeDtypeStruct(s, F32) for s in out_shapes],
        scratch_shapes=[pltpu.VMEM((8, rows, width), F32), pltpu.SemaphoreType.DMA((7,)), pltpu.SemaphoreType.DMA((7,))],
        compiler_params=pltpu.CompilerParams(vmem_limit_bytes=VMEM_LIMIT_BYTES),
    )(*items)


def _allreduce_small(items, name):
    def finish(buf, offs, outs):
        for it, off, out in zip(items, offs, outs):
            region = lambda d: buf[d, off:off + it.shape[0], 0:it.shape[1]]
            s = region(0)
            for d in range(1, 8):
                s = s + region(d)
            out[...] = s
    return _small_exchange(items, [it.shape for it in items], finish, name)


def _allgather_small_shards(items, name):
    def finish(buf, offs, outs):
        for it, off, out in zip(items, offs, outs):
            r, c = it.shape
            for j in range(4):
                out[:, j * c:(j + 1) * c] = buf[2 * j, off:off + r, 0:c]
    return _small_exchange(items, [(it.shape[0], 4 * it.shape[1]) for it in items], finish, name)


def _split_w_in(wt, D):
    pad = jnp.zeros((ZS - 3 * LOWRANK, wt.shape[1]), wt.dtype)
    big = jnp.concatenate([wt[:3 * D], wt[3 * D + 16:6 * D + 16], wt[6 * D + 16:7 * D + 16], wt[7 * D + 48:]], axis=0)
    small = jnp.concatenate([wt[3 * D:3 * D + 16], wt[7 * D + 16:7 * D + 48], pad], axis=0)
    return big, small


def _join_w_in(ga, gb, gs, D):
    return jnp.concatenate([ga[:3 * D], gs[:16], ga[3 * D:], gb[:D], gs[16:48], gb[D:]], axis=0)


def kernel(x, p, g_mix, w_in, gla_w2, gla_b, gla_norm, dn_conv, dn_a_log, dn_dt_bias, dn_norm, w_out, g_mlp, w_up, w_down, g_ple, w_ple_gate, w_ple_proj, g_final, loss_target, m_g_mix, m_w_in, m_gla_w2, m_gla_b, m_gla_norm, m_dn_conv, m_dn_a_log, m_dn_dt_bias, m_dn_norm, m_w_out, m_g_mlp, m_w_up, m_w_down, m_g_ple, m_w_ple_gate, m_w_ple_proj, m_g_final, v_g_mix, v_w_in, v_gla_w2, v_gla_b, v_gla_norm, v_dn_conv, v_dn_a_log, v_dn_dt_bias, v_dn_norm, v_w_out, v_g_mlp, v_w_up, v_w_down, v_g_ple, v_w_ple_gate, v_w_ple_proj, v_g_final):
    wts = dict(zip(WEIGHTS, [g_mix, w_in, gla_w2, gla_b, gla_norm, dn_conv, dn_a_log, dn_dt_bias, dn_norm, w_out, g_mlp,
                             w_up, w_down, g_ple, w_ple_gate, w_ple_proj, g_final]))
    mom = dict(zip(WEIGHTS, [m_g_mix, m_w_in, m_gla_w2, m_gla_b, m_gla_norm, m_dn_conv, m_dn_a_log, m_dn_dt_bias, m_dn_norm,
                             m_w_out, m_g_mlp, m_w_up, m_w_down, m_g_ple, m_w_ple_gate, m_w_ple_proj, m_g_final]))
    var = dict(zip(WEIGHTS, [v_g_mix, v_w_in, v_gla_w2, v_gla_b, v_gla_norm, v_dn_conv, v_dn_a_log, v_dn_dt_bias, v_dn_norm,
                             v_w_out, v_g_mlp, v_w_up, v_w_down, v_g_ple, v_w_ple_gate, v_w_ple_proj, v_g_final]))
    Bl, S, D = x.shape
    T = Bl * S
    PLE = p.shape[-1]
    dn_d, gla_dk = D // DN_HEADS, D // (2 * GLA_HEADS)
    ix, iy, ic = _place()
    j_me = 2 * ix + iy
    as2d = lambda a: a.reshape(a.shape[-2], a.shape[-1]) if a.ndim > 1 else a.reshape(1, -1)
    c_idx, me_idx = ic.reshape(1).astype(jnp.int32), j_me.reshape(1).astype(jnp.int32)

    rows_first = lambda a: jnp.transpose(a, (2, 0, 1))
    cols_last = lambda a: jnp.transpose(a, (1, 2, 0))
    w_in_t, m_in_t, v_in_t = rows_first(w_in), rows_first(m_w_in), rows_first(v_w_in)
    n_in = w_in_t.shape[0]
    shard2d = {n: as2d(wts[n]) for n, _ in BIG[1:]}
    bf16_shards = [w_in_t.astype(BF16).reshape(n_in, D)] + [shard2d[n].astype(BF16) for n, _ in BIG[1:]]
    split = [COLS] + [ROWS] * (len(BIG) - 1)
    own_slot = lambda g, s: lax.dynamic_update_slice(g, s[None], (j_me, 0, 0))
    xt = x.reshape(T, D)
    (w_in_near,) = _run_stage(_gather_neighbours(bf16_shards[:1], split[:1]), "allgather_w_in_neighbours")
    (w_in_ici,) = _run_stage(_gather_relay([w_in_near]), "allgather_w_in_relay")
    h, w_in_all = _rmsnorm_fwd(xt, g_mix, "rms1_fwd", stage=_gather_pass([w_in_ici], split[:1]))
    w_in_slots = own_slot(w_in_all, bf16_shards[0])
    w_big, w_small = _split_w_in(w_in_slots.reshape(4 * n_in, D), D)

    w2_full, conv_full = _allgather_small_shards([as2d(gla_w2), as2d(dn_conv)], "allgather_small_weights")
    w2pad = jnp.pad(w2_full, ((0, ZS - LOWRANK), (0, 0)))
    w2h = jnp.swapaxes(w2pad.reshape(ZS, GLA_HEADS, gla_dk), 0, 1)
    gbh = gla_b.reshape(GLA_HEADS, 1, gla_dk)
    alog_w = jnp.broadcast_to(dn_a_log.reshape(DN_HEADS, 1, 1), (DN_HEADS, 1, dn_d))
    dtb_w = jnp.broadcast_to(dn_dt_bias.reshape(DN_HEADS, 1, 1), (DN_HEADS, 1, dn_d))

    tgt = loss_target.reshape(T, D)
    pt = p.reshape(T, PLE)
    seq = lambda t: t.reshape(Bl, S, t.shape[-1])
    tok = lambda t: t.reshape(T, t.shape[-1])
    first, second = [1, 2, 5], [3, 4]
    sh, sp = (lambda idx: [bf16_shards[i] for i in idx]), (lambda idx: [split[i] for i in idx])
    z_big, *first_ici = _matmul(h, w_big, 'nt', [F32], "proj_in", stage=_gather_ici(sh(first), sp(first)))
    (z_small,) = _matmul(h, w_small, 'nt', [F32], "proj_in_narrow")
    o_gla, st_all, *first_all = _gla_fwd(seq(z_big), seq(z_small), w2h, gbh, Bl, S, D, stage=_gather_pass(first_ici, sp(first)))
    acts = [_conv_fwd(z_big, conv_full, grp, Bl, S, D) for grp in range(3)]
    o_dn, s_all, *second_ici = _dn_fwd(seq(acts[0]), seq(acts[1]), seq(acts[2]), seq(z_small), alog_w, dtb_w, Bl, S, D,
                                       stage=_gather_ici(sh(second), sp(second)))
    mixed, *second_all = _merge_fwd(tok(o_gla), tok(o_dn), z_big, gla_norm, dn_norm, D,
                                    stage=_gather_pass(second_ici, sp(second)))
    slots = {BIG[i][0]: own_slot(g, bf16_shards[i]) for i, g in zip(first + second, first_all + second_all)}
    rows_joined = lambda t: t.reshape(4 * t.shape[1], t.shape[2])
    w_out_f, w_down_f, w_pg_f = rows_joined(slots['w_out']), rows_joined(slots['w_down']), rows_joined(slots['w_ple_gate'])
    w_up_s, w_pp_s = slots['w_up'], slots['w_ple_proj']
    (x1,) = _matmul(mixed, w_out_f, 'nn', [F32], "proj_out", epilogue=lambda r, e: (e + r,), extras=(xt,), bm=512)
    (h2,) = _rmsnorm_fwd(x1, g_mlp, "rms2_fwd")
    u, act = _matmul(h2, w_up_s, 'nn', [F32, BF16], "mlp_up", b_slots=True,
                     epilogue=lambda r: (r, jnp.square(jnp.maximum(r, 0.0))))
    (x2,) = _matmul(act, w_down_f, 'nn', [F32], "mlp_down", epilogue=lambda r, e: (e + r,), extras=(x1,), bm=512)
    (h3,) = _rmsnorm_fwd(x2, g_ple, "rms3_fwd")
    (pp,) = _matmul(pt, w_pp_s, 'nn', [F32], "ple_proj", b_slots=True)
    gp, x3 = _matmul(h3, w_pg_f, 'nn', [F32, F32], "ple_gate",
                     epilogue=lambda r, e, q: (r, e + _sigmoid(r) * q), extras=(x2, pp), bm=512)
    dx3, loss_tile, d_g_final = _loss_fwd_bwd(x3, g_final.reshape(1, D), tgt, "loss")

    d_gp, d_pp = _ple_bwd(dx3, gp, pp, "ple_bwd")
    (g_pp,) = _matmul(pt, d_pp, 'tn', [F32], "ple_proj_dw", out_slots=True)
    (g_pg,) = _matmul(h3, d_gp, 'tn', [F32], "ple_gate_dw")
    (dh3,) = _matmul(d_gp, w_pg_f, 'nt', [F32], "ple_gate_dx")
    dx2, dx2b, d_g_ple = _rmsnorm_bwd_add(x2, g_ple, dh3, dx3, "rms3_bwd")
    (g_down,) = _matmul(act, dx2b, 'tn', [F32], "mlp_down_dw")
    (du,) = _matmul(dx2b, w_down_f, 'nt', [BF16], "mlp_down_dx",
                    epilogue=lambda r, e: (r * 2.0 * jnp.maximum(e, 0.0),), extras=(u,))
    (g_up,) = _matmul(h2, du, 'tn', [F32], "mlp_up_dw", out_slots=True)
    by_rows = lambda g: g.reshape(4, g.shape[0] // 4, g.shape[1])
    send_mlp = [g_up, by_rows(g_down), by_rows(g_pg), g_pp]
    dh2, *sib_mlp = _matmul(du, w_up_s, 'nt', [F32], "mlp_up_dx", b_slots=True, stage=_pair_exchange(send_mlp, split[2:]))
    dx1, dx1b, d_g_mlp = _rmsnorm_bwd_add(x1, g_mlp, dh2, dx2, "rms2_bwd")
    (g_out,) = _matmul(mixed, dx1b, 'tn', [F32], "proj_out_dw")
    dmix, sib_out = _matmul(dx1b, w_out_f, 'nt', [F32], "proj_out_dx", stage=_pair_exchange([by_rows(g_out)], split[1:2]))
    rest = [n for n, _ in BIG[1:]]
    send_rest, sib_rest = [by_rows(g_out)] + send_mlp, [sib_out] + sib_mlp
    sums_rest = [_pair_sum(s, f, c_idx, f"grad_pair_sum_{n}") for n, s, f in zip(rest, send_rest, sib_rest)]
    d_ogla, d_gg, d_odn, d_dz, d_ga, d_gb, d_gla_norm, d_dn_norm = _merge_bwd(
        tok(o_gla), tok(o_dn), z_big, gla_norm, dn_norm, dmix, D)
    d_q, d_k, d_v, dzs_gla, d_w2h, d_gbh = _gla_bwd(seq(z_big), seq(z_small), w2h, gbh, st_all, seq(d_ogla), Bl, S, D)
    d_qa, d_ka, d_va, d_zs, d_alog_w, d_dtb_w, *chips_rest = _dn_bwd(
        seq(acts[0]), seq(acts[1]), seq(acts[2]), seq(z_small), alog_w, dtb_w, s_all, seq(d_odn), dzs_gla, Bl, S, D,
        stage=_chip_scatter([b for _, b in sums_rest]))
    conv_b = [_conv_bwd(z_big, conv_full, tok(g), grp, Bl, S, D) for grp, g in enumerate([d_qa, d_ka, d_va])]
    dz_big = jnp.concatenate([tok(d_q), tok(d_k), tok(d_v), d_gg, conv_b[0][0], conv_b[1][0], conv_b[2][0], d_dz, d_ga,
                              d_gb], axis=1)
    dz_small = tok(d_zs)
    cut = 6 * D
    (d_w_a,) = _matmul(dz_big, h, 'tn', [F32], "proj_in_dw_a", m_cols=(0, cut))
    d_w_b, sib_a = _matmul(dz_big, h, 'tn', [F32], "proj_in_dw_b", m_cols=(cut, 3 * D), stage=_pair_exchange([d_w_a], [COLS]))
    halves_rest = [_final_sum(f, got, me_idx, f"grad_final_sum_{n}") for n, (f, _), got in zip(rest, sums_rest, chips_rest)]
    d_w_small, sib_b = _matmul(dz_small, h, 'tn', [F32], "proj_in_narrow_dw", stage=_pair_exchange([d_w_b], [COLS]))
    (sib_s,) = _run_stage(_pair_exchange([d_w_small], [COLS]), "grad_pair_exchange_narrow")
    parts = [_pair_sum(mine[None], theirs[None], c_idx, f"grad_pair_sum_w_in_{tag}", COLS)
             for tag, mine, theirs in (("a", d_w_a, sib_a), ("b", d_w_b, sib_b), ("narrow", d_w_small, sib_s))]
    joined = lambda k: _join_w_in(parts[0][k][0], parts[1][k][0], parts[2][k][0], D).reshape(4, n_in, D // 2)
    sum_in_f32, sum_in_bf16 = joined(0), joined(1)
    dh_a, chips_in, *pair_rest = _matmul(dz_big, w_big, 'nn', [F32], "proj_in_dx",
                                         stage=_both(_chip_scatter([sum_in_bf16]), _pair_allgather(halves_rest, split[1:])))
    half_in = _final_sum(sum_in_f32, chips_in, me_idx, "grad_final_sum_w_in", split[0])
    (dh,) = _matmul(dz_small, w_small, 'nn', [F32], "proj_in_narrow_dx", epilogue=lambda r, e: (e + r,), extras=(dh_a,))
    grad_x, _, d_g_mix = _rmsnorm_bwd_add(xt, g_mix, dh, dx1, "rms1_bwd")
    (pair_in,) = _run_stage(_pair_allgather([half_in], split[:1]), "grad_pair_allgather_w_in")
    reduced = {n: lax.dynamic_update_slice(o, hlf, (ic * hlf.shape[0], 0)) for n, o, hlf in zip(rest, pair_rest, halves_rest)}
    south = ic == 0
    g_in_t = jnp.concatenate([jnp.where(south, half_in, pair_in), jnp.where(south, pair_in, half_in)],
                             axis=1).reshape(n_in, 1, D)

    d_w2 = jnp.swapaxes(d_w2h, 0, 1).reshape(ZS, D // 2)[:LOWRANK]
    small_grads = {'g_mix': d_g_mix, 'gla_w2': d_w2, 'gla_b': d_gbh.reshape(1, D // 2), 'gla_norm': d_gla_norm,
                   'dn_a_log': d_alog_w[:, 0, 0].reshape(1, DN_HEADS), 'dn_dt_bias': d_dtb_w[:, 0, 0].reshape(1, DN_HEADS),
                   'dn_norm': d_dn_norm, 'g_mlp': d_g_mlp, 'g_ple': d_g_ple, 'g_final': d_g_final}
    names = [n for n in SMALL if n != 'dn_conv']
    total = _allreduce_small([small_grads[n] for n in names] + [cb[1] for cb in conv_b] + [loss_tile[:1]],
                             "allreduce_small_grads")
    gsmall = dict(zip(names, total[:len(names)]))
    loss = total[-1][0, 0]
    my_cols = lambda g: lax.dynamic_slice_in_dim(g, j_me * (g.shape[1] // 4), g.shape[1] // 4, axis=1)
    gsmall['gla_w2'] = my_cols(gsmall['gla_w2'])
    gsmall['dn_conv'] = my_cols(jnp.concatenate(total[len(names):len(names) + 3], axis=1))

    g_o, d_o, m_o, v_o = {}, {}, {}, {}
    d_in_t, nm_in_t, nv_in_t, g_out_t = _adamw(w_in_t, g_in_t, m_in_t, v_in_t, "adamw_w_in", with_grad=True)
    g_o['w_in'], d_o['w_in'], m_o['w_in'], v_o['w_in'] = [cols_last(t) for t in (g_out_t, d_in_t, nm_in_t, nv_in_t)]
    for n, _ in BIG[1:]:
        shp = wts[n].shape
        d2, nm2, nv2 = _adamw(shard2d[n], reduced[n], as2d(mom[n]), as2d(var[n]), f"adamw_{n}")
        g_o[n], d_o[n], m_o[n], v_o[n] = reduced[n].reshape(shp), d2.reshape(shp), nm2.reshape(shp), nv2.reshape(shp)
    ds, nms, nvs = _adamw_small([as2d(wts[n]) for n in SMALL], [as2d(gsmall[n]) for n in SMALL],
                                [as2d(mom[n]) for n in SMALL], [as2d(var[n]) for n in SMALL])
    for n, dd, mm, vv in zip(SMALL, ds, nms, nvs):
        shp = wts[n].shape
        g_o[n], d_o[n], m_o[n], v_o[n] = gsmall[n].reshape(shp), dd.reshape(shp), mm.reshape(shp), vv.reshape(shp)

    return (loss, grad_x.reshape(Bl, S, D), *[g_o[n] for n in WEIGHTS], *[d_o[n] for n in WEIGHTS],
            *[m_o[n] for n in WEIGHTS], *[v_o[n] for n in WEIGHTS])
```

```python
import functools

import jax
import jax.numpy as jnp
from jax import lax
from jax.experimental import pallas as pl
from jax.experimental.pallas import tpu as pltpu

F32 = jnp.float32
BF16 = jnp.bfloat16

CHUNK = 64
GLA_HEADS = 4
DN_HEADS = 16
LOWRANK = 16
GLA_TAU = 16.0
DN_CONV = 4
EPS = 1e-6
ZS = 128
A_LANE, B_LANE = LOWRANK, LOWRANK + DN_HEADS
ADAM_LR, ADAM_B1, ADAM_B2, ADAM_EPS, ADAM_WD, ADAM_STEP = 0.001, 0.9, 0.999, 1e-08, 0.01, 10

V7X_VMEM_BYTES = 64 * 1024 * 1024
VMEM_LIMIT_BYTES = V7X_VMEM_BYTES - 8 * 1024 * 1024
LANES = 128
SUBLANES = 8
MESH = pl.DeviceIdType.MESH
DN_HEADS_PER_STEP = 16
GLA_HEADS_PER_STEP = 4

WEIGHTS = ['g_mix', 'w_in', 'gla_w2', 'gla_b', 'gla_norm', 'dn_conv', 'dn_a_log', 'dn_dt_bias', 'dn_norm', 'w_out',
           'g_mlp', 'w_up', 'w_down', 'g_ple', 'w_ple_gate', 'w_ple_proj', 'g_final']
BIG = [('w_in', 1), ('w_out', 0), ('w_up', 1), ('w_down', 0), ('w_ple_gate', 0), ('w_ple_proj', 1)]
SMALL = [n for n in WEIGHTS if n not in dict(BIG)]

_NN, _NT, _TN = 'nn', 'nt', 'tn'


def _params(sem=None):
    return pltpu.CompilerParams(dimension_semantics=sem, vmem_limit_bytes=VMEM_LIMIT_BYTES)


def _dot(a, b, form, precision=None):
    o = a.ndim - 2
    contract = {_NN: ((1 + o,), (o,)), _NT: ((1 + o,), (1 + o,)), _TN: ((o,), (o,))}[form]
    batch = ((0,), (0,)) if o else ((), ())
    return lax.dot_general(a, b, (contract, batch), precision=precision, preferred_element_type=F32)


def _make_mm(cast, precision):
    def raw(a, b, dims):
        return _dot(cast(a), cast(b), dims, precision)

    @jax.custom_vjp
    def nn(a, b):
        return raw(a, b, _NN)
    nn.defvjp(lambda a, b: (raw(a, b, _NN), (a, b)), lambda r, g: (raw(g, r[1], _NT), raw(r[0], g, _TN)))

    @jax.custom_vjp
    def nt(a, b):
        return raw(a, b, _NT)
    nt.defvjp(lambda a, b: (raw(a, b, _NT), (a, b)), lambda r, g: (raw(g, r[1], _NN), raw(g, r[0], _TN)))

    @jax.custom_vjp
    def tn(a, b):
        return raw(a, b, _TN)
    tn.defvjp(lambda a, b: (raw(a, b, _TN), (a, b)), lambda r, g: (raw(r[1], g, _NT), raw(r[0], g, _NN)))
    return nn, nt, tn


_bnn, _bnt, _btn = _make_mm(lambda t: t.astype(BF16), None)
TRI_PRECISION = lax.Precision.HIGH


def _iota2(n, axis):
    return lax.broadcasted_iota(jnp.int32, (n, n), axis)


def _lower(n, strict=False):
    return (_iota2(n, 0) > _iota2(n, 1)) if strict else (_iota2(n, 0) >= _iota2(n, 1))


def _tri_times(tri, x):
    tri = tri.astype(F32)
    if x.ndim == 3:
        tri = jnp.broadcast_to(tri, (x.shape[0],) + tri.shape)
    return _dot(tri, x, _NN, lax.Precision.HIGH)


@jax.custom_vjp
def _cumsum_rows(x):
    return _tri_times(_lower(x.shape[-2]), x)


def _cumsum_rows_bwd(_, g):
    n = g.shape[-2]
    return (_tri_times(_iota2(n, 0) <= _iota2(n, 1), g),)


_cumsum_rows.defvjp(lambda x: (_cumsum_rows(x), None), _cumsum_rows_bwd)


def _tri_inv_impl(a):
    n = a.shape[-1]
    eye = (_iota2(n, 0) == _iota2(n, 1)).astype(F32)
    p = eye - a
    ak = a
    k = 2
    while k < n:
        prec, cast = (TRI_PRECISION, lambda t: t) if k == 2 else (None, lambda t: t.astype(BF16))
        ak = _dot(cast(ak), cast(ak), _NN, prec)
        p = p + _dot(cast(p), cast(ak), _NN, prec)
        k *= 2
    return p


@jax.custom_vjp
def _tri_inv(a):
    return _tri_inv_impl(a)


def _tri_inv_fwd(a):
    t = _tri_inv_impl(a)
    return t, t


def _tri_inv_bwd(t, g):
    tb = t.astype(BF16)
    tg = _dot(tb, g.astype(BF16), _TN)
    return (-_dot(tg.astype(BF16), tb, _NT),)


_tri_inv.defvjp(_tri_inv_fwd, _tri_inv_bwd)


def _shift_rows(x, s, down):
    n = x.shape[0]
    r = lax.broadcasted_iota(jnp.int32, x.shape, 0)
    if down:
        return jnp.where(r >= s, pltpu.roll(x, s, 0), 0.0)
    return jnp.where(r < n - s, pltpu.roll(x, n - s, 0), 0.0)


def _make_shift(s):
    @jax.custom_vjp
    def f(x):
        return _shift_rows(x, s, True)
    f.defvjp(lambda x: (_shift_rows(x, s, True), None), lambda _, g: (_shift_rows(g, s, False),))
    return f


def _sigmoid(x):
    return jax.nn.sigmoid(x)


def _silu(x):
    return x * jax.nn.sigmoid(x)


def _softplus(x):
    return jnp.maximum(x, 0.0) + jnp.log1p(jnp.exp(-jnp.abs(x)))


def _log_sigmoid(x):
    return -_softplus(-x)


def _rms(x, g):
    return x * lax.rsqrt(jnp.mean(x * x, axis=-1, keepdims=True) + EPS) * g


def _gla_chunk(q, k, v, zs, w2, gb, st, *, scale):
    c = q.shape[-2]
    logf = _log_sigmoid(_bnn(zs, w2) + gb) * (1.0 / GLA_TAU)
    bcum = _cumsum_rows(logf)
    b_last = jnp.sum(logf, axis=-2, keepdims=True)
    q_in = (q * scale) * jnp.exp(bcum)
    k_in = k * jnp.exp(-bcum)
    a = jnp.where(_lower(c), _bnt(q_in, k_in), 0.0)
    o = _bnn(a, v) + _bnt(q_in, st)
    k_dec = k * jnp.exp(b_last - bcum)
    st_new = st * jnp.exp(b_last) + _btn(v, k_dec)
    return o, st_new


def _dn_chunk(q, k, v, aw, bw, alog, dtb, s):
    c = q.shape[-2]
    incl, strict = _lower(c), _lower(c, True)
    g_w = -jnp.exp(alog) * _softplus(aw + dtb)
    beta_w = _sigmoid(bw)
    gcum_w = _cumsum_rows(g_w)
    lane0 = lax.broadcasted_iota(jnp.int32, gcum_w.shape, gcum_w.ndim - 1) == 0
    gcol = jnp.sum(jnp.where(lane0, gcum_w, 0.0), axis=-1, keepdims=True)
    d1 = jnp.broadcast_to(gcol, gcol.shape[:-1] + (c,))
    diff = jnp.where(incl, d1 - jnp.swapaxes(d1, -1, -2), 0.0)
    decay = jnp.where(incl, jnp.exp(diff), 0.0)
    k_beta = k * beta_w
    a = jnp.where(strict, _bnt(k_beta, k) * decay, 0.0)
    t = _tri_inv(a)
    egc = jnp.exp(gcum_w)
    u = _bnn(t, v * beta_w)
    w = _bnn(t, k_beta * egc)
    attn = jnp.where(incl, _bnt(q, k) * decay, 0.0)
    q_dec = q * egc
    g_last = jnp.sum(g_w, axis=-2, keepdims=True)
    k_dec = k * jnp.exp(g_last - gcum_w)
    v_new = u - _bnn(w, s)
    o = _bnn(q_dec, s) + _bnn(attn, v_new)
    s_new = s * jnp.exp(g_last) + _btn(k_dec, v_new)
    return o, s_new


def _conv_act(x, wrows, *, l2, scale):
    taps = len(wrows)
    y = None
    for j in range(taps):
        s = taps - 1 - j
        xs = x if s == 0 else _make_shift(s)(x)
        y = wrows[j] * xs if y is None else y + wrows[j] * xs
    y = _silu(y)
    if l2:
        y = y * lax.rsqrt(jnp.sum(y * y, axis=-1, keepdims=True) + EPS) * scale
    return y


def _merge_math(og, gg, od, dz, ga, gb, gn, dn):
    nsub = len(og)
    dv = nsub * og[0].shape[1]
    ssq = jnp.sum(og[0] * og[0], axis=-1, keepdims=True)
    for s in range(1, nsub):
        ssq = ssq + jnp.sum(og[s] * og[s], axis=-1, keepdims=True)
    r = lax.rsqrt(ssq * (1.0 / dv) + EPS)
    outs = []
    for s in range(nsub):
        a = og[s] * r * gn[s] * _silu(gg[s])
        b = _rms(od[s], dn) * _silu(dz[s])
        outs.append(_sigmoid(ga[s]) * a + _sigmoid(gb[s]) * b)
    return outs


def _pick(n, target, mult):
    best = None
    for d in range(mult, min(n, target) + 1, mult):
        if n % d == 0:
            best = d
    return best if best is not None else n


class _Stage:
    def __init__(self, inputs, out_shapes, n_sems, copies, aliases=None):
        self.inputs, self.out_shapes, self.n_sems, self.copies = list(inputs), list(out_shapes), n_sems, copies
        self.aliases = aliases or {}

    @property
    def sems(self):
        return [pltpu.SemaphoreType.DMA((self.n_sems,)), pltpu.SemaphoreType.DMA((self.n_sems,))]


def _host_stage(body, stage, n_in, n_out, grid):
    ci, co = len(stage.inputs), len(stage.out_shapes)

    def wrapped(*refs):
        ins, cins = refs[:n_in], refs[n_in:n_in + ci]
        outs, couts = refs[n_in + ci:n_in + ci + n_out], refs[n_in + ci + n_out:n_in + ci + n_out + co]
        scratch, sems = refs[n_in + ci + n_out + co:-2], refs[-2:]
        ids = [pl.program_id(d) for d in range(len(grid))]
        first, last = ids[0] == 0, ids[0] == grid[0] - 1
        for i, g in zip(ids[1:], grid[1:]):
            first, last = first & (i == 0), last & (i == g - 1)

        @pl.when(first)
        def _():
            for cp in stage.copies(cins, couts, *sems):
                cp.start()

        body(*ins, *outs, *scratch)

        @pl.when(last)
        def _():
            for cp in stage.copies(cins, couts, *sems):
                cp.wait()

    return wrapped


def _call(body, name, grid, in_specs, out_specs, out_shape, scratch, semantics, args, stage=None):
    if stage is None:
        return pl.pallas_call(body, name=name, grid=grid, in_specs=list(in_specs), out_specs=list(out_specs),
                              out_shape=list(out_shape), scratch_shapes=list(scratch), compiler_params=_params(semantics))(*args)
    n_in, n_out = len(in_specs), len(out_specs)
    return pl.pallas_call(
        _host_stage(body, stage, n_in, n_out, grid), name=name, grid=grid,
        in_specs=list(in_specs) + [ANY] * len(stage.inputs),
        out_specs=list(out_specs) + [ANY] * len(stage.out_shapes), out_shape=list(out_shape) + stage.out_shapes,
        scratch_shapes=list(scratch) + stage.sems,
        input_output_aliases={n_in + i: n_out + o for i, o in stage.aliases.items()},
        compiler_params=_params(("arbitrary",) * len(grid)),
    )(*args, *stage.inputs)


def _run_stage(stage, name):
    ci = len(stage.inputs)

    def body(*refs):
        cps = stage.copies(refs[:ci], refs[ci:-2], *refs[-2:])
        for cp in cps:
            cp.start()
        for cp in cps:
            cp.wait()

    return pl.pallas_call(body, name=name, in_specs=[ANY] * ci, out_specs=[ANY] * len(stage.out_shapes),
                          out_shape=stage.out_shapes, scratch_shapes=stage.sems,
                          input_output_aliases=dict(stage.aliases))(*stage.inputs)


def _matmul(a, b, form, out_dtypes, name, epilogue=None, extras=(), bm=1024, bn=1024, bk=2048,
            b_slots=False, out_slots=False, stage=None, m_cols=None):
    ns, c = (b.shape[0], b.shape[2]) if b_slots else (1, None)
    b2 = b.shape[1:] if b_slots else b.shape
    if form == 'nn':
        (M, K), (K2, N) = a.shape, (b2[0], b2[1] * ns)
    elif form == 'nt':
        (M, K), (N, K2) = a.shape, (b2[0], b2[1] * ns)
    else:
        (K, M), (K2, N) = a.shape, b2
    m0 = 0
    if m_cols is not None:
        m0, M = m_cols
    assert K == K2 and not (b_slots and form == 'tn') and (m_cols is None or form == 'tn'), (a.shape, b.shape, form)
    bm, bn, bk = _pick(M, bm, SUBLANES), _pick(N, bn, LANES), _pick(K, bk, LANES)
    assert m0 % bm == 0
    if b_slots:
        bn, bk = (_pick(c, bn, LANES), bk) if form == 'nn' else (bn, _pick(c, bk, LANES))
    if out_slots:
        oc = N // 4
        bn = _pick(oc, bn, LANES)
    nk = K // bk
    a_spec = pl.BlockSpec((bk, bm), lambda i, j, k: (k, m0 // bm + i)) if form == 'tn' else pl.BlockSpec((bm, bk), lambda i, j, k: (i, k))
    if b_slots and form == 'nn':
        per = c // bn
        b_spec = pl.BlockSpec((None, bk, bn), lambda i, j, k: (j // per, k, j % per))
    elif b_slots:
        per = c // bk
        b_spec = pl.BlockSpec((None, bn, bk), lambda i, j, k: (k // per, j, k % per))
    elif form == 'nt':
        b_spec = pl.BlockSpec((bn, bk), lambda i, j, k: (j, k))
    else:
        b_spec = pl.BlockSpec((bk, bn), lambda i, j, k: (k, j))
    o_spec = pl.BlockSpec((bm, bn), lambda i, j, k: (i, j))
    if out_slots:
        oper = oc // bn
        out_spec = pl.BlockSpec((None, bm, bn), lambda i, j, k: (j // oper, i, j % oper))
        out_shape = [jax.ShapeDtypeStruct((4, M, oc), d) for d in out_dtypes]
    else:
        out_spec = o_spec
        out_shape = [jax.ShapeDtypeStruct((M, N), d) for d in out_dtypes]
    ne, no = len(extras), len(out_dtypes)

    def finish(r, extra_refs, out_refs):
        outs = (r,) if epilogue is None else epilogue(r, *[e[...] for e in extra_refs])
        for ref, o in zip(out_refs, outs):
            ref[...] = o.astype(ref.dtype)

    def body_one(a_ref, b_ref, *rest):
        finish(_dot(a_ref[...].astype(BF16), b_ref[...].astype(BF16), form), rest[:ne], rest[ne:ne + no])

    def body_acc(a_ref, b_ref, *rest):
        extra_refs, out_refs, acc = rest[:ne], rest[ne:ne + no], rest[ne + no]
        k = pl.program_id(2)
        part = _dot(a_ref[...].astype(BF16), b_ref[...].astype(BF16), form)

        @pl.when(k == 0)
        def _():
            acc[...] = part

        @pl.when((k > 0) & (k < nk - 1))
        def _():
            acc[...] += part

        @pl.when(k == nk - 1)
        def _():
            finish(acc[...] + part, extra_refs, out_refs)

    row_spec = pl.BlockSpec((1, bn), lambda i, j, k: (0, j))
    extra_specs = [o_spec if e.shape[0] == M else row_spec for e in extras]
    return _call(body_one if nk == 1 else body_acc, name, (M // bm, N // bn, nk), [a_spec, b_spec] + extra_specs,
                 [out_spec] * no, out_shape, [] if nk == 1 else [pltpu.VMEM((bm, bn), F32)],
                 ("parallel", "parallel", "arbitrary"), (a, b, *extras), stage)


def _rowwise(fn, rows, consts, row_outs, acc_outs, name, bt=256, stage=None):
    T = rows[0].shape[0]
    bt = _pick(T, bt, SUBLANES)
    nr, nc, no, na = len(rows), len(consts), len(row_outs), len(acc_outs)

    def body(*refs):
        r_in, c_in = refs[:nr], refs[nr:nr + nc]
        r_out, a_out = refs[nr + nc:nr + nc + no], refs[nr + nc + no:]
        ro, ao = fn([r[...] for r in r_in], [c[...] for c in c_in])
        for ref, o in zip(r_out, ro):
            ref[...] = o.astype(ref.dtype)
        if na:
            @pl.when(pl.program_id(0) == 0)
            def _():
                for ref in a_out:
                    ref[...] = jnp.zeros_like(ref)
            for ref, o in zip(a_out, ao):
                ref[...] += o

    whole = lambda shp: pl.BlockSpec(shp, lambda i: (0,) * len(shp))
    return _call(
        body, name, (T // bt,),
        [pl.BlockSpec((bt, r.shape[1]), lambda i: (i, 0)) for r in rows] + [whole(c.shape) for c in consts],
        [pl.BlockSpec((bt, w), lambda i: (i, 0)) for w, _ in row_outs] + [whole(s) for s in acc_outs],
        [jax.ShapeDtypeStruct((T, w), d) for w, d in row_outs] + [jax.ShapeDtypeStruct(s, F32) for s in acc_outs],
        [], ("arbitrary",), (*rows, *consts), stage)


def _rmsnorm_fwd(x, g, name, stage=None):
    return _rowwise(lambda r, c: ([_rms(r[0], c[0])], []), [x], [g], [(x.shape[1], BF16)], [], name, stage=stage)


def _rmsnorm_bwd_add(x, g, dh, dres, name):
    D = x.shape[1]

    def fn(r, c):
        _, vjp = jax.vjp(_rms, r[0], c[0])
        dx, dg = vjp(r[1])
        dx = dx + r[2]
        return [dx, dx], [dg]
    return _rowwise(fn, [x, dh, dres], [g], [(D, F32), (D, BF16)], [(1, D)], name)


def _loss_fwd_bwd(x3, g, target, name):
    D = x3.shape[1]

    def fn(r, c):
        def row_loss(x, gain):
            err = _rms(x, gain) - r[1]
            return 0.5 * jnp.mean(err * err, axis=-1, keepdims=True)
        lrow, vjp = jax.vjp(row_loss, r[0], c[0])
        dx, dg = vjp(jnp.ones_like(lrow))
        tile = jnp.broadcast_to(jnp.sum(lrow, axis=0, keepdims=True), (SUBLANES, LANES))
        return [dx], [tile, dg]
    return _rowwise(fn, [x3, target], [g], [(D, F32)], [(SUBLANES, LANES), (1, D)], name)


def _ple_bwd(dx3, gp, pp, name):
    D = dx3.shape[1]

    def fn(r, c):
        s = _sigmoid(r[1])
        return [r[0] * r[2] * s * (1.0 - s), r[0] * s], []
    return _rowwise(fn, [dx3, gp, pp], [], [(D, BF16), (D, BF16)], [], name)


def _adamw_math(w, g, m, v):
    nm = ADAM_B1 * m + (1.0 - ADAM_B1) * g
    nv = ADAM_B2 * v + (1.0 - ADAM_B2) * (g * g)
    m_hat = nm / (1.0 - ADAM_B1 ** ADAM_STEP)
    v_hat = nv / (1.0 - ADAM_B2 ** ADAM_STEP)
    return -ADAM_LR * (m_hat / (jnp.sqrt(v_hat) + ADAM_EPS) + ADAM_WD * w), nm, nv


def _adamw(w, g, m, v, name, with_grad=False):
    R, C = w.shape[0], w.shape[-1]
    lanes = -(-C // LANES) * LANES
    if w.ndim == 2:
        bt = _pick(R, max(SUBLANES, (1 << 18) // lanes // SUBLANES * SUBLANES), SUBLANES)
        spec = pl.BlockSpec((bt, C), lambda i: (i, 0))
    else:
        bt = _pick(R, max(1, (1 << 18) // lanes), 1)
        spec = pl.BlockSpec((bt, 1, C), lambda i: (i, 0, 0))

    def body(w_ref, g_ref, m_ref, v_ref, d_ref, nm_ref, nv_ref, *g_out):
        d_ref[...], nm_ref[...], nv_ref[...] = _adamw_math(w_ref[...], g_ref[...], m_ref[...], v_ref[...])
        for ref in g_out:
            ref[...] = g_ref[...]

    n_out = 4 if with_grad else 3
    return pl.pallas_call(
        body, name=name, grid=(R // bt,), in_specs=[spec] * 4, out_specs=[spec] * n_out,
        out_shape=[jax.ShapeDtypeStruct(w.shape, F32)] * n_out, compiler_params=_params(("parallel",)),
    )(w, g, m, v)


def _adamw_small(ws, gs, ms, vs):
    n = len(ws)

    def body(*refs):
        for i in range(n):
            d, nm, nv = _adamw_math(refs[i][...], refs[n + i][...], refs[2 * n + i][...], refs[3 * n + i][...])
            refs[4 * n + i][...], refs[5 * n + i][...], refs[6 * n + i][...] = d, nm, nv

    VMEM = pl.BlockSpec(memory_space=pltpu.VMEM)
    shapes = [jax.ShapeDtypeStruct(w.shape, F32) for w in ws]
    outs = pl.pallas_call(body, name="adamw_small", in_specs=[VMEM] * (4 * n), out_specs=[VMEM] * (3 * n),
                          out_shape=shapes * 3)(*ws, *gs, *ms, *vs)
    return outs[:n], outs[n:2 * n], outs[2 * n:]


def _gla_fwd(z_big, z_small, w2h, gbh, Bl, S, D, stage=None):
    NC, dk, dv, HB = S // CHUNK, D // (2 * GLA_HEADS), D // GLA_HEADS, GLA_HEADS_PER_STEP
    HG = GLA_HEADS // HB
    chains = [(hh, bb) for hh in range(HB) for bb in range(Bl)]
    G = len(chains)
    fn = functools.partial(_gla_chunk, scale=dk ** -0.5)

    def body(q, k, v, z, w2, gb, o_ref, stall_ref, st):
        n, g = pl.program_id(0), pl.program_id(1)

        @pl.when(n == 0)
        def _():
            st[g] = jnp.zeros((G, dv, dk), F32)
        s0 = st[g]
        stall_ref[...] = s0.reshape(HB, Bl, dv, dk)
        qk = lambda r: jnp.stack([r[bb, :, hh * dk:(hh + 1) * dk] for hh, bb in chains])
        o, s_new = fn(qk(q), qk(k), jnp.stack([v[bb, :, hh * dv:(hh + 1) * dv] for hh, bb in chains]),
                      jnp.stack([z[bb] for _, bb in chains]), jnp.stack([w2[hh] for hh, _ in chains]),
                      jnp.stack([gb[hh] for hh, _ in chains]), s0)
        for i, (hh, bb) in enumerate(chains):
            o_ref[bb, :, hh * dv:(hh + 1) * dv] = o[i]
        st[g] = s_new

    return _call(
        body, "gla_fwd", (NC, HG),
        [pl.BlockSpec((Bl, CHUNK, HB * dk), lambda n, g: (0, n, g)),
         pl.BlockSpec((Bl, CHUNK, HB * dk), lambda n, g: (0, n, HG + g)),
         pl.BlockSpec((Bl, CHUNK, HB * dv), lambda n, g: (0, n, HG + g)),
         pl.BlockSpec((Bl, CHUNK, ZS), lambda n, g: (0, n, 0)),
         pl.BlockSpec((HB, ZS, dk), lambda n, g: (g, 0, 0)),
         pl.BlockSpec((HB, 1, dk), lambda n, g: (g, 0, 0))],
        [pl.BlockSpec((Bl, CHUNK, HB * dv), lambda n, g: (0, n, g)),
         pl.BlockSpec((HB, Bl, None, dv, dk), lambda n, g: (g, 0, n, 0, 0))],
        [jax.ShapeDtypeStruct((Bl, S, D), F32), jax.ShapeDtypeStruct((GLA_HEADS, Bl, NC, dv, dk), F32)],
        [pltpu.VMEM((HG, G, dv, dk), F32)], ("arbitrary", "arbitrary"), (z_big, z_big, z_big, z_small, w2h, gbh), stage)


def _gla_bwd(z_big, z_small, w2h, gbh, st_all, do, Bl, S, D):
    NC, dk, dv, HB = S // CHUNK, D // (2 * GLA_HEADS), D // GLA_HEADS, GLA_HEADS_PER_STEP
    HG = GLA_HEADS // HB
    chains = [(hh, bb) for hh in range(HB) for bb in range(Bl)]
    G = len(chains)
    fn = functools.partial(_gla_chunk, scale=dk ** -0.5)

    def body(q, k, v, z, w2, gb, st0, do_ref, dq_ref, dk_ref, dv_ref, dzs_ref, dw2_ref, dgb_ref, dst):
        n, g = pl.program_id(0), pl.program_id(1)

        @pl.when(n == 0)
        def _():
            dst[g] = jnp.zeros((G, dv, dk), F32)

        @pl.when((n == 0) & (g == 0))
        def _():
            dw2_ref[...] = jnp.zeros_like(dw2_ref)
            dgb_ref[...] = jnp.zeros_like(dgb_ref)

        qk = lambda r: jnp.stack([r[bb, :, hh * dk:(hh + 1) * dk] for hh, bb in chains])
        vv = lambda r: jnp.stack([r[bb, :, hh * dv:(hh + 1) * dv] for hh, bb in chains])
        _, vjp = jax.vjp(fn, qk(q), qk(k), vv(v), jnp.stack([z[bb] for _, bb in chains]),
                         jnp.stack([w2[hh] for hh, _ in chains]), jnp.stack([gb[hh] for hh, _ in chains]),
                         st0[...].reshape(G, dv, dk))
        dq, dkk, dvv, dzs, dw2, dgb, dst0 = vjp((vv(do_ref), dst[g]))
        for i, (hh, bb) in enumerate(chains):
            dq_ref[bb, :, hh * dk:(hh + 1) * dk] = dq[i].astype(dq_ref.dtype)
            dk_ref[bb, :, hh * dk:(hh + 1) * dk] = dkk[i].astype(dk_ref.dtype)
            dv_ref[bb, :, hh * dv:(hh + 1) * dv] = dvv[i].astype(dv_ref.dtype)
            dw2_ref[g * HB + hh] += dw2[i]
            dgb_ref[g * HB + hh] += dgb[i]
        for bb in range(Bl):
            tot = sum(dzs[i] for i, (_, b2) in enumerate(chains) if b2 == bb)

            @pl.when(g == 0)
            def _():
                dzs_ref[bb] = tot

            @pl.when(g > 0)
            def _():
                dzs_ref[bb] += tot
        dst[g] = dst0

    rn = lambda n: NC - 1 - n
    return pl.pallas_call(
        body, name="gla_bwd", grid=(NC, HG),
        in_specs=[pl.BlockSpec((Bl, CHUNK, HB * dk), lambda n, g: (0, rn(n), g)),
                  pl.BlockSpec((Bl, CHUNK, HB * dk), lambda n, g: (0, rn(n), HG + g)),
                  pl.BlockSpec((Bl, CHUNK, HB * dv), lambda n, g: (0, rn(n), HG + g)),
                  pl.BlockSpec((Bl, CHUNK, ZS), lambda n, g: (0, rn(n), 0)),
                  pl.BlockSpec((HB, ZS, dk), lambda n, g: (g, 0, 0)),
                  pl.BlockSpec((HB, 1, dk), lambda n, g: (g, 0, 0)),
                  pl.BlockSpec((HB, Bl, None, dv, dk), lambda n, g: (g, 0, rn(n), 0, 0)),
                  pl.BlockSpec((Bl, CHUNK, HB * dv), lambda n, g: (0, rn(n), g))],
        out_specs=[pl.BlockSpec((Bl, CHUNK, HB * dk), lambda n, g: (0, rn(n), g)),
                   pl.BlockSpec((Bl, CHUNK, HB * dk), lambda n, g: (0, rn(n), g)),
                   pl.BlockSpec((Bl, CHUNK, HB * dv), lambda n, g: (0, rn(n), g)),
                   pl.BlockSpec((Bl, CHUNK, ZS), lambda n, g: (0, rn(n), 0)),
                   pl.BlockSpec((GLA_HEADS, ZS, dk), lambda n, g: (0, 0, 0)),
                   pl.BlockSpec((GLA_HEADS, 1, dk), lambda n, g: (0, 0, 0))],
        out_shape=[jax.ShapeDtypeStruct((Bl, S, D // 2), BF16), jax.ShapeDtypeStruct((Bl, S, D // 2), BF16),
                   jax.ShapeDtypeStruct((Bl, S, D), BF16), jax.ShapeDtypeStruct((Bl, S, ZS), F32),
                   jax.ShapeDtypeStruct((GLA_HEADS, ZS, dk), F32), jax.ShapeDtypeStruct((GLA_HEADS, 1, dk), F32)],
        scratch_shapes=[pltpu.VMEM((HG, G, dv, dk), F32)],
        compiler_params=_params(("arbitrary", "arbitrary")),
    )(z_big, z_big, z_big, z_small, w2h, gbh, st_all, do)


def _conv_fwd(z_big, conv_w, grp, Bl, S, D):
    d = D // DN_HEADS
    l2, scale = grp < 2, (d ** -0.5 if grp == 0 else 1.0)
    x_blk0 = (3 * D + grp * D) // d

    def body(x_ref, w_ref, o_ref):
        wrows = [w_ref[j:j + 1, :] for j in range(DN_CONV)]
        o_ref[...] = _conv_act(x_ref[...], wrows, l2=l2, scale=scale)

    return pl.pallas_call(
        body, name=f"conv_fwd{grp}", grid=(Bl, DN_HEADS),
        in_specs=[pl.BlockSpec((S, d), lambda b, j: (b, x_blk0 + j)),
                  pl.BlockSpec((DN_CONV, d), lambda b, j: (0, grp * DN_HEADS + j))],
        out_specs=pl.BlockSpec((S, d), lambda b, j: (b, j)),
        out_shape=jax.ShapeDtypeStruct((Bl * S, D), F32),
        compiler_params=_params(("parallel", "parallel")),
    )(z_big, conv_w)


def _conv_bwd(z_big, conv_w, dact, grp, Bl, S, D):
    d = D // DN_HEADS
    l2, scale = grp < 2, (d ** -0.5 if grp == 0 else 1.0)
    x_blk0 = (3 * D + grp * D) // d

    def body(x_ref, w_ref, g_ref, dx_ref, dw_ref):
        @pl.when(pl.program_id(1) == 0)
        def _():
            dw_ref[...] = jnp.zeros_like(dw_ref)
        wrows = [w_ref[j:j + 1, :] for j in range(DN_CONV)]
        _, vjp = jax.vjp(lambda x, wr: _conv_act(x, wr, l2=l2, scale=scale), x_ref[...], wrows)
        dx, dwr = vjp(g_ref[...])
        dx_ref[...] = dx.astype(dx_ref.dtype)
        for j in range(DN_CONV):
            dw_ref[j:j + 1, :] += dwr[j]

    return pl.pallas_call(
        body, name=f"conv_bwd{grp}", grid=(DN_HEADS, Bl),
        in_specs=[pl.BlockSpec((S, d), lambda j, b: (b, x_blk0 + j)),
                  pl.BlockSpec((DN_CONV, d), lambda j, b: (0, grp * DN_HEADS + j)),
                  pl.BlockSpec((S, d), lambda j, b: (b, j))],
        out_specs=[pl.BlockSpec((S, d), lambda j, b: (b, j)), pl.BlockSpec((DN_CONV, d), lambda j, b: (0, j))],
        out_shape=[jax.ShapeDtypeStruct((Bl * S, D), BF16), jax.ShapeDtypeStruct((DN_CONV, D), F32)],
        compiler_params=_params(("arbitrary", "arbitrary")),
    )(z_big, conv_w, dact)


def _lane_column(zb, lane, width):
    pick = lax.broadcasted_iota(jnp.int32, zb.shape, 1) == lane
    return jnp.broadcast_to(jnp.sum(jnp.where(pick, zb, 0.0), axis=-1, keepdims=True), (zb.shape[0], width))


def _dn_fwd(qa, ka, va, z_small, alog, dtb, Bl, S, D, stage=None):
    NC, d, HB = S // CHUNK, D // DN_HEADS, DN_HEADS_PER_STEP
    HG = DN_HEADS // HB
    chains = [(hh, bb) for hh in range(HB) for bb in range(Bl)]
    G = len(chains)

    def body(q, k, v, z, al, dt, o_ref, sall_ref, st):
        n, g = pl.program_id(0), pl.program_id(1)

        @pl.when(n == 0)
        def _():
            st[g] = jnp.zeros((G, d, d), F32)
        tok_in = lambda r: jnp.stack([r[bb, :, hh * d:(hh + 1) * d] for hh, bb in chains])
        head_in = lambda r: jnp.stack([r[hh] for hh, _ in chains])
        gate_in = lambda lane0: jnp.stack([_lane_column(z[bb], lane0 + g * HB + hh, d) for hh, bb in chains])
        s0 = st[g]
        sall_ref[...] = s0.reshape(HB, Bl, d, d)
        o, s_new = _dn_chunk(tok_in(q), tok_in(k), tok_in(v), gate_in(A_LANE), gate_in(B_LANE), head_in(al), head_in(dt), s0)
        for i, (hh, bb) in enumerate(chains):
            o_ref[bb, :, hh * d:(hh + 1) * d] = o[i]
        st[g] = s_new

    tok = pl.BlockSpec((Bl, CHUNK, HB * d), lambda n, g: (0, n, g))
    per_head = pl.BlockSpec((HB, 1, d), lambda n, g: (g, 0, 0))
    return _call(
        body, "dn_fwd", (NC, HG),
        [tok, tok, tok, pl.BlockSpec((Bl, CHUNK, ZS), lambda n, g: (0, n, 0)), per_head, per_head],
        [tok, pl.BlockSpec((HB, Bl, None, d, d), lambda n, g: (g, 0, n, 0, 0))],
        [jax.ShapeDtypeStruct((Bl, S, D), F32), jax.ShapeDtypeStruct((DN_HEADS, Bl, NC, d, d), F32)],
        [pltpu.VMEM((HG, G, d, d), F32)], ("arbitrary", "arbitrary"), (qa, ka, va, z_small, alog, dtb), stage)


def _dn_bwd(qa, ka, va, z_small, alog, dtb, s_all, do, dzs_gla, Bl, S, D, stage=None):
    NC, d, HB = S // CHUNK, D // DN_HEADS, DN_HEADS_PER_STEP
    HG = DN_HEADS // HB
    chains = [(hh, bb) for hh in range(HB) for bb in range(Bl)]
    G = len(chains)

    def lanesum(t):
        return jnp.sum(t, axis=-1, keepdims=True)

    def body(q, k, v, z, al, dt, s0_ref, do_ref, dzg_ref, dq_ref, dk_ref, dv_ref, dzs_ref, dal_ref, ddt_ref, dst):
        n, g = pl.program_id(0), pl.program_id(1)

        @pl.when(n == 0)
        def _():
            dst[g] = jnp.zeros((G, d, d), F32)

        @pl.when((n == 0) & (g == 0))
        def _():
            dal_ref[...] = jnp.zeros_like(dal_ref)
            ddt_ref[...] = jnp.zeros_like(ddt_ref)

        tok_in = lambda r: jnp.stack([r[bb, :, hh * d:(hh + 1) * d] for hh, bb in chains])
        head_in = lambda r: jnp.stack([r[hh] for hh, _ in chains])
        gate_in = lambda lane0: jnp.stack([_lane_column(z[bb], lane0 + g * HB + hh, d) for hh, bb in chains])
        _, vjp = jax.vjp(_dn_chunk, tok_in(q), tok_in(k), tok_in(v), gate_in(A_LANE), gate_in(B_LANE), head_in(al),
                         head_in(dt), s0_ref[...].reshape(G, d, d))
        dq, dkk, dvv, da, db, dal, ddt, ds0 = vjp((tok_in(do_ref), dst[g]))
        da, db = lanesum(da), lanesum(db)
        dal = jnp.broadcast_to(lanesum(dal), (G, 1, d))
        ddt = jnp.broadcast_to(lanesum(ddt), (G, 1, d))
        lane = lax.broadcasted_iota(jnp.int32, (CHUNK, ZS), 1)
        for bb in range(Bl):
            part = jnp.zeros((CHUNK, ZS), F32)
            for i, (hh, b2) in enumerate(chains):
                if b2 == bb:
                    h = g * HB + hh
                    part = part + jnp.where(lane == A_LANE + h, da[i], 0.0) + jnp.where(lane == B_LANE + h, db[i], 0.0)

            @pl.when(g == 0)
            def _():
                dzs_ref[bb] = jnp.where(lane < LOWRANK, dzg_ref[bb], 0.0) + part

            @pl.when(g > 0)
            def _():
                dzs_ref[bb] += part
        for i, (hh, bb) in enumerate(chains):
            cols = slice(hh * d, (hh + 1) * d)
            dq_ref[bb, :, cols] = dq[i]
            dk_ref[bb, :, cols] = dkk[i]
            dv_ref[bb, :, cols] = dvv[i]
            dal_ref[g * HB + hh] += dal[i]
            ddt_ref[g * HB + hh] += ddt[i]
        dst[g] = ds0

    rn = lambda n: NC - 1 - n
    tok = pl.BlockSpec((Bl, CHUNK, HB * d), lambda n, g: (0, rn(n), g))
    zsb = pl.BlockSpec((Bl, CHUNK, ZS), lambda n, g: (0, rn(n), 0))
    per_head = pl.BlockSpec((HB, 1, d), lambda n, g: (g, 0, 0))
    all_heads = pl.BlockSpec((DN_HEADS, 1, d), lambda n, g: (0, 0, 0))
    tok_shape = jax.ShapeDtypeStruct((Bl, S, D), F32)
    head_shape = jax.ShapeDtypeStruct((DN_HEADS, 1, d), F32)
    return _call(
        body, "dn_bwd", (NC, HG),
        [tok, tok, tok, zsb, per_head, per_head,
         pl.BlockSpec((HB, Bl, None, d, d), lambda n, g: (g, 0, rn(n), 0, 0)), tok, zsb],
        [tok, tok, tok, zsb, all_heads, all_heads],
        [tok_shape, tok_shape, tok_shape, jax.ShapeDtypeStruct((Bl, S, ZS), F32), head_shape, head_shape],
        [pltpu.VMEM((HG, G, d, d), F32)], ("arbitrary", "arbitrary"),
        (qa, ka, va, z_small, alog, dtb, s_all, do, dzs_gla), stage)


def _merge_specs(D, bt):
    dv, w = D // GLA_HEADS, D // DN_HEADS
    col = lambda off: pl.BlockSpec((bt, dv), lambda i, h: (i, off // dv + h))
    return dv, w, col


def _merge_load(refs, nsub, w):
    return [[r[:, s * w:(s + 1) * w] for s in range(nsub)] for r in refs]


def _merge_fwd(o_gla, o_dn, z_big, gla_norm, dn_norm, D, bt=256, stage=None):
    T = o_gla.shape[0]
    bt = _pick(T, bt, SUBLANES)
    dv, w, col = _merge_specs(D, bt)
    nsub = dv // w

    def body(og, gg, od, dz, ga, gb, gn, dn, out):
        ogl, ggl, odl, dzl, gal, gbl = _merge_load([og, gg, od, dz, ga, gb], nsub, w)
        gnl = [gn[:, s * w:(s + 1) * w] for s in range(nsub)]
        outs = _merge_math(ogl, ggl, odl, dzl, gal, gbl, gnl, dn[...])
        for s in range(nsub):
            out[:, s * w:(s + 1) * w] = outs[s].astype(out.dtype)

    return _call(
        body, "merge_fwd", (T // bt, GLA_HEADS),
        [col(0), col(2 * D), col(0), col(6 * D), col(7 * D), col(8 * D),
         pl.BlockSpec((1, dv), lambda i, h: (0, 0)), pl.BlockSpec((1, w), lambda i, h: (0, 0))],
        [col(0)], [jax.ShapeDtypeStruct((T, D), BF16)], [], ("parallel", "parallel"),
        (o_gla, z_big, o_dn, z_big, z_big, z_big, gla_norm, dn_norm), stage)


def _merge_bwd(o_gla, o_dn, z_big, gla_norm, dn_norm, dmix, D, bt=256):
    T = o_gla.shape[0]
    bt = _pick(T, bt, SUBLANES)
    dv, w, col = _merge_specs(D, bt)
    nsub = dv // w

    def body(og, gg, od, dz, ga, gb, gn, dn, dm, dog, dgg, dod, ddz, dga, dgb, dgn, ddn):
        @pl.when((pl.program_id(0) == 0) & (pl.program_id(1) == 0))
        def _():
            dgn[...] = jnp.zeros_like(dgn)
            ddn[...] = jnp.zeros_like(ddn)

        ogl, ggl, odl, dzl, gal, gbl, dml = _merge_load([og, gg, od, dz, ga, gb, dm], nsub, w)
        gnl = [gn[:, s * w:(s + 1) * w] for s in range(nsub)]
        _, vjp = jax.vjp(_merge_math, ogl, ggl, odl, dzl, gal, gbl, gnl, dn[...])
        g_og, g_gg, g_od, g_dz, g_ga, g_gb, g_gn, g_dn = vjp(dml)
        for s in range(nsub):
            sl = slice(s * w, (s + 1) * w)
            dog[:, sl] = g_og[s]
            dgg[:, sl] = g_gg[s].astype(dgg.dtype)
            dod[:, sl] = g_od[s]
            ddz[:, sl] = g_dz[s].astype(ddz.dtype)
            dga[:, sl] = g_ga[s].astype(dga.dtype)
            dgb[:, sl] = g_gb[s].astype(dgb.dtype)
            dgn[:, sl] += g_gn[s]
        ddn[...] += g_dn

    f32s, bf16s = jax.ShapeDtypeStruct((T, D), F32), jax.ShapeDtypeStruct((T, D), BF16)
    return pl.pallas_call(
        body, name="merge_bwd", grid=(T // bt, GLA_HEADS),
        in_specs=[col(0), col(2 * D), col(0), col(6 * D), col(7 * D), col(8 * D),
                  pl.BlockSpec((1, dv), lambda i, h: (0, 0)), pl.BlockSpec((1, w), lambda i, h: (0, 0)), col(0)],
        out_specs=[col(0)] * 6 + [pl.BlockSpec((1, dv), lambda i, h: (0, 0)), pl.BlockSpec((1, w), lambda i, h: (0, 0))],
        out_shape=[f32s, bf16s, f32s, bf16s, bf16s, bf16s,
                   jax.ShapeDtypeStruct((1, dv), F32), jax.ShapeDtypeStruct((1, w), F32)],
        compiler_params=_params(("arbitrary", "arbitrary")),
    )(o_gla, z_big, o_dn, z_big, z_big, z_big, gla_norm, dn_norm, dmix)


def _place():
    return lax.axis_index("x"), lax.axis_index("y"), lax.axis_index("c")


def _other_chips(x, y):
    return [(1 - x, y), (x, 1 - y), (1 - x, 1 - y)]


def _rcopy(src, dst, send_sem, recv_sem, dev):
    return pltpu.make_async_remote_copy(src_ref=src, dst_ref=dst, send_sem=send_sem, recv_sem=recv_sem,
                                        device_id=dev, device_id_type=MESH)


ANY = pl.BlockSpec(memory_space=pl.ANY)


ROWS, COLS = 'rows', 'cols'


def _half(ref, hc, by, lead=()):
    shape = ref.shape[len(lead):]
    if by == ROWS:
        rh = shape[0] // 2
        idx = (pl.ds(pl.multiple_of(hc * rh, 16), rh),) + (slice(None),) * (len(shape) - 1)
    else:
        ch = shape[-1] // 2
        idx = (slice(None),) * (len(shape) - 1) + (pl.ds(pl.multiple_of(hc * ch, LANES), ch),)
    return ref.at[(*lead, *idx)]


def _half_shape(shape, by):
    return (shape[0] // 2,) + tuple(shape[1:]) if by == ROWS else tuple(shape[:-1]) + (shape[-1] // 2,)


def _gather_ici(shards, by):
    nw = len(shards)

    def copies(srcs, outs, send_sems, recv_sems):
        x, y, c = _place()
        return [_rcopy(_half(srcs[w], c, by[w]), _half(outs[w], c, by[w], (2 * x + y,)),
                       send_sems.at[3 * w + k], recv_sems.at[3 * w + k], (px, py, c))
                for w in range(nw) for k, (px, py) in enumerate(_other_chips(x, y))]

    return _Stage(shards, [jax.ShapeDtypeStruct((4,) + s.shape, s.dtype) for s in shards], 3 * nw, copies)


def _gather_neighbours(shards, by):
    nw = len(shards)

    def copies(srcs, outs, send_sems, recv_sems):
        x, y, c = _place()
        return [_rcopy(_half(srcs[w], c, by[w]), _half(outs[w], c, by[w], (2 * x + y,)),
                       send_sems.at[2 * w + k], recv_sems.at[2 * w + k], (px, py, c))
                for w in range(nw) for k, (px, py) in enumerate(_other_chips(x, y)[:2])]

    return _Stage(shards, [jax.ShapeDtypeStruct((4,) + s.shape, s.dtype) for s in shards], 2 * nw, copies)


def _gather_relay(gathered):
    nw = len(gathered)

    def copies(srcs, outs, send_sems, recv_sems):
        x, y, c = _place()
        cps = []
        for w in range(nw):
            _, n, cols = gathered[w].shape
            cut, ch = n // 2 // 16 * 16, cols // 2
            lanes = pl.ds(pl.multiple_of(c * ch, LANES), ch)
            via = [(2 * (1 - x) + y, pl.ds(0, cut), (x, 1 - y, c)),
                   (2 * x + (1 - y), pl.ds(cut, n - cut), (1 - x, y, c))]
            for k, (slot, rows, dev) in enumerate(via):
                cps.append(_rcopy(srcs[w].at[slot, rows, lanes], outs[w].at[slot, rows, lanes],
                                  send_sems.at[2 * w + k], recv_sems.at[2 * w + k], dev))
        return cps

    return _Stage(gathered, [jax.ShapeDtypeStruct(g.shape, g.dtype) for g in gathered], 2 * nw, copies,
                  aliases={w: w for w in range(nw)})


def _gather_pass(gathered, by):
    nw = len(gathered)

    def copies(srcs, outs, send_sems, recv_sems):
        x, y, c = _place()
        cps = []
        for w in range(nw):
            for k, (px, py) in enumerate(_other_chips(x, y)):
                slot = (2 * px + py,)
                cps.append(_rcopy(_half(srcs[w], c, by[w], slot), _half(outs[w], c, by[w], slot),
                                  send_sems.at[3 * w + k], recv_sems.at[3 * w + k], (x, y, 1 - c)))
        return cps

    return _Stage(gathered, [jax.ShapeDtypeStruct(g.shape, g.dtype) for g in gathered], 3 * nw, copies,
                  aliases={w: w for w in range(nw)})


def _pair_exchange(ps, by):
    nw = len(ps)

    def copies(srcs, outs, send_sems, recv_sems):
        x, y, c = _place()
        return [_rcopy(_half(srcs[w], 1 - c, by[w], (slice(None),) * (ps[w].ndim - 2)), outs[w], send_sems.at[w], recv_sems.at[w],
                       (x, y, 1 - c)) for w in range(nw)]

    return _Stage(ps, [jax.ShapeDtypeStruct(p.shape[:-2] + _half_shape(p.shape[-2:], b), p.dtype) for p, b in zip(ps, by)],
                  nw, copies)


def _sum_blocks(half_shape, by):
    rh, ch = half_shape
    lanes = -(-ch // LANES) * LANES
    bt = _pick(rh, max(16, (3 << 18) // lanes // 16 * 16), 16)
    if by == ROWS:
        nb = rh // bt
        return (bt, ch), nb, (lambda i: (i, 0)), (lambda i, c: (c * nb + i, 0))
    if rh % bt == 0 and bt % 16 == 0:
        return (bt, ch), rh // bt, (lambda i: (i, 0)), (lambda i, c: (i, c))
    bc = _pick(ch, max(LANES, (5 << 18) // rh // LANES * LANES), LANES)
    nb = ch // bc
    return (rh, bc), nb, (lambda i: (0, i)), (lambda i, c: (0, c * nb + i))


def _pair_sum(p, got, c_idx, name, by=ROWS):
    lead, hs = got.shape[0], got.shape[1:]
    blk, nb, pos, pos_whole = _sum_blocks(hs, by)

    def body(c_ref, a, b, of, ob):
        s = a[...] + b[...]
        of[...] = s
        ob[...] = s.astype(BF16)

    spec = pl.BlockSpec((None,) + blk, lambda j, i, c_ref: (j,) + pos(i))
    return pl.pallas_call(
        body, name=name,
        grid_spec=pltpu.PrefetchScalarGridSpec(
            num_scalar_prefetch=1, grid=(lead, nb),
            in_specs=[pl.BlockSpec((None,) + blk, lambda j, i, c_ref: (j,) + pos_whole(i, c_ref[0])), spec],
            out_specs=[spec, spec]),
        out_shape=[jax.ShapeDtypeStruct((lead,) + hs, F32), jax.ShapeDtypeStruct((lead,) + hs, BF16)],
        compiler_params=_params(("parallel", "parallel")),
    )(c_idx, p, got)


def _chip_scatter(qbs):
    nw = len(qbs)

    def copies(srcs, outs, send_sems, recv_sems):
        x, y, c = _place()
        return [_rcopy(srcs[w].at[2 * px + py], outs[w].at[k], send_sems.at[3 * w + k], recv_sems.at[3 * w + k], (px, py, c))
                for w in range(nw) for k, (px, py) in enumerate(_other_chips(x, y))]

    return _Stage(qbs, [jax.ShapeDtypeStruct((3,) + q.shape[1:], q.dtype) for q in qbs], 3 * nw, copies)


def _final_sum(qf, got, me_idx, name, by=ROWS):
    hs = qf.shape[1:]
    blk, nb, pos, _ = _sum_blocks(hs, by)

    def body(me_ref, a, b, o):
        o[...] = ((a[...] + b[0].astype(F32)) + b[1].astype(F32)) + b[2].astype(F32)

    return pl.pallas_call(
        body, name=name,
        grid_spec=pltpu.PrefetchScalarGridSpec(
            num_scalar_prefetch=1, grid=(nb,),
            in_specs=[pl.BlockSpec((None,) + blk, lambda i, me_ref: (me_ref[0],) + pos(i)),
                      pl.BlockSpec((3,) + blk, lambda i, me_ref: (0,) + pos(i))],
            out_specs=pl.BlockSpec(blk, lambda i, me_ref: pos(i))),
        out_shape=jax.ShapeDtypeStruct(hs, F32),
        compiler_params=_params(("parallel",)),
    )(me_idx, qf, got)


def _pair_allgather(halves, by):
    nw = len(halves)
    whole = [(2 * h.shape[0], h.shape[1]) if b == ROWS else h.shape for h, b in zip(halves, by)]

    def copies(srcs, outs, send_sems, recv_sems):
        x, y, c = _place()
        there = lambda w: _half(outs[w], c, ROWS) if by[w] == ROWS else outs[w]
        return [_rcopy(srcs[w], there(w), send_sems.at[w], recv_sems.at[w], (x, y, 1 - c)) for w in range(nw)]

    return _Stage(halves, [jax.ShapeDtypeStruct(s, h.dtype) for s, h in zip(whole, halves)], nw, copies)


class _SemaphoreWindow:
    def __init__(self, ref, off):
        self.ref, self.off = ref, off

    @property
    def at(self):
        return self

    def __getitem__(self, i):
        return self.ref.at[self.off + i]


def _both(a, b):
    na, ma = len(a.inputs), len(a.out_shapes)

    def copies(ins, outs, send_sems, recv_sems):
        return (a.copies(ins[:na], outs[:ma], send_sems, recv_sems) +
                b.copies(ins[na:], outs[ma:], _SemaphoreWindow(send_sems, a.n_sems), _SemaphoreWindow(recv_sems, a.n_sems)))

    return _Stage(a.inputs + b.inputs, a.out_shapes + b.out_shapes, a.n_sems + b.n_sems, copies,
                  aliases={**a.aliases, **{na + i: ma + o for i, o in b.aliases.items()}})


def _small_exchange(items, out_shapes, finish, name):
    n = len(items)
    offs, rows = [], 0
    for it in items:
        offs.append(rows)
        rows += it.shape[0]
    rows = -(-rows // SUBLANES) * SUBLANES
    width = -(-max(it.shape[1] for it in items) // LANES) * LANES
    VMEM = pl.BlockSpec(memory_space=pltpu.VMEM)

    def body(*refs):
        ins, outs = refs[:n], refs[n:n + len(out_shapes)]
        buf, send_sems, recv_sems = refs[n + len(out_shapes):]
        x, y, c = _place()
        me = 4 * x + 2 * y + c
        flip = lambda v, f: (1 - v) if f else v
        peers = [(flip(x, r >> 2 & 1), flip(y, r >> 1 & 1), flip(c, r & 1)) for r in range(1, 8)]
        buf[me] = jnp.zeros((rows, width), F32)
        for it, off, ref in zip(items, offs, ins):
            buf[me, off:off + it.shape[0], 0:it.shape[1]] = ref[...]
        cps = [_rcopy(buf.at[me], buf.at[me], send_sems.at[k], recv_sems.at[k], dev) for k, dev in enumerate(peers)]
        for cp in cps:
            cp.start()
        for k, (px, py, pc) in enumerate(peers):
            slot = buf.at[4 * px + 2 * py + pc]
            _rcopy(slot, slot, send_sems.at[k], recv_sems.at[k], (px, py, pc)).wait_recv()
        for cp in cps:
            cp.wait_send()
        finish(buf, offs, outs)

    return pl.pallas_call(
        body, name=name, in_specs=[VMEM] * n, out_specs=[VMEM] * len(out_shapes),
        out_shape=[jax.ShapeDtypeStruct(s, F32) for s in out_shapes],
        scratch_shapes=[pltpu.VMEM((8, rows, width), F32), pltpu.SemaphoreType.DMA((7,)), pltpu.SemaphoreType.DMA((7,))],
        compiler_params=pltpu.CompilerParams(vmem_limit_bytes=VMEM_LIMIT_BYTES),
    )(*items)


def _allreduce_small(items, name):
    def finish(buf, offs, outs):
        for it, off, out in zip(items, offs, outs):
            region = lambda d: buf[d, off:off + it.shape[0], 0:it.shape[1]]
            s = region(0)
            for d in range(1, 8):
                s = s + region(d)
            out[...] = s
    return _small_exchange(items, [it.shape for it in items], finish, name)


def _allgather_small_shards(items, name):
    def finish(buf, offs, outs):
        for it, off, out in zip(items, offs, outs):
            r, c = it.shape
            for j in range(4):
                out[:, j * c:(j + 1) * c] = buf[2 * j, off:off + r, 0:c]
    return _small_exchange(items, [(it.shape[0], 4 * it.shape[1]) for it in items], finish, name)


def _split_w_in(wt, D):
    pad = jnp.zeros((ZS - 3 * LOWRANK, wt.shape[1]), wt.dtype)
    big = jnp.concatenate([wt[:3 * D], wt[3 * D + 16:6 * D + 16], wt[6 * D + 16:7 * D + 16], wt[7 * D + 48:]], axis=0)
    small = jnp.concatenate([wt[3 * D:3 * D + 16], wt[7 * D + 16:7 * D + 48], pad], axis=0)
    return big, small


def _join_w_in(ga, gb, gs, D):
    return jnp.concatenate([ga[:3 * D], gs[:16], ga[3 * D:], gb[:D], gs[16:48], gb[D:]], axis=0)


def kernel(x, p, g_mix, w_in, gla_w2, gla_b, gla_norm, dn_conv, dn_a_log, dn_dt_bias, dn_norm, w_out, g_mlp, w_up, w_down, g_ple, w_ple_gate, w_ple_proj, g_final, loss_target, m_g_mix, m_w_in, m_gla_w2, m_gla_b, m_gla_norm, m_dn_conv, m_dn_a_log, m_dn_dt_bias, m_dn_norm, m_w_out, m_g_mlp, m_w_up, m_w_down, m_g_ple, m_w_ple_gate, m_w_ple_proj, m_g_final, v_g_mix, v_w_in, v_gla_w2, v_gla_b, v_gla_norm, v_dn_conv, v_dn_a_log, v_dn_dt_bias, v_dn_norm, v_w_out, v_g_mlp, v_w_up, v_w_down, v_g_ple, v_w_ple_gate, v_w_ple_proj, v_g_final):
    wts = dict(zip(WEIGHTS, [g_mix, w_in, gla_w2, gla_b, gla_norm, dn_conv, dn_a_log, dn_dt_bias, dn_norm, w_out, g_mlp,
                             w_up, w_down, g_ple, w_ple_gate, w_ple_proj, g_final]))
    mom = dict(zip(WEIGHTS, [m_g_mix, m_w_in, m_gla_w2, m_gla_b, m_gla_norm, m_dn_conv, m_dn_a_log, m_dn_dt_bias, m_dn_norm,
                             m_w_out, m_g_mlp, m_w_up, m_w_down, m_g_ple, m_w_ple_gate, m_w_ple_proj, m_g_final]))
    var = dict(zip(WEIGHTS, [v_g_mix, v_w_in, v_gla_w2, v_gla_b, v_gla_norm, v_dn_conv, v_dn_a_log, v_dn_dt_bias, v_dn_norm,
                             v_w_out, v_g_mlp, v_w_up, v_w_down, v_g_ple, v_w_ple_gate, v_w_ple_proj, v_g_final]))
    Bl, S, D = x.shape
    T = Bl * S
    PLE = p.shape[-1]
    dn_d, gla_dk = D // DN_HEADS, D // (2 * GLA_HEADS)
    ix, iy, ic = _place()
    j_me = 2 * ix + iy
    as2d = lambda a: a.reshape(a.shape[-2], a.shape[-1]) if a.ndim > 1 else a.reshape(1, -1)
    c_idx, me_idx = ic.reshape(1).astype(jnp.int32), j_me.reshape(1).astype(jnp.int32)

    rows_first = lambda a: jnp.transpose(a, (2, 0, 1))
    cols_last = lambda a: jnp.transpose(a, (1, 2, 0))
    w_in_t, m_in_t, v_in_t = rows_first(w_in), rows_first(m_w_in), rows_first(v_w_in)
    n_in = w_in_t.shape[0]
    shard2d = {n: as2d(wts[n]) for n, _ in BIG[1:]}
    bf16_shards = [w_in_t.astype(BF16).reshape(n_in, D)] + [shard2d[n].astype(BF16) for n, _ in BIG[1:]]
    split = [COLS] + [ROWS] * (len(BIG) - 1)
    own_slot = lambda g, s: lax.dynamic_update_slice(g, s[None], (j_me, 0, 0))
    xt = x.reshape(T, D)
    (w_in_near,) = _run_stage(_gather_neighbours(bf16_shards[:1], split[:1]), "allgather_w_in_neighbours")
    (w_in_ici,) = _run_stage(_gather_relay([w_in_near]), "allgather_w_in_relay")
    h, w_in_all = _rmsnorm_fwd(xt, g_mix, "rms1_fwd", stage=_gather_pass([w_in_ici], split[:1]))
    w_in_slots = own_slot(w_in_all, bf16_shards[0])
    w_big, w_small = _split_w_in(w_in_slots.reshape(4 * n_in, D), D)

    w2_full, conv_full = _allgather_small_shards([as2d(gla_w2), as2d(dn_conv)], "allgather_small_weights")
    w2pad = jnp.pad(w2_full, ((0, ZS - LOWRANK), (0, 0)))
    w2h = jnp.swapaxes(w2pad.reshape(ZS, GLA_HEADS, gla_dk), 0, 1)
    gbh = gla_b.reshape(GLA_HEADS, 1, gla_dk)
    alog_w = jnp.broadcast_to(dn_a_log.reshape(DN_HEADS, 1, 1), (DN_HEADS, 1, dn_d))
    dtb_w = jnp.broadcast_to(dn_dt_bias.reshape(DN_HEADS, 1, 1), (DN_HEADS, 1, dn_d))

    tgt = loss_target.reshape(T, D)
    pt = p.reshape(T, PLE)
    seq = lambda t: t.reshape(Bl, S, t.shape[-1])
    tok = lambda t: t.reshape(T, t.shape[-1])
    first, second = [1, 2, 5], [3, 4]
    sh, sp = (lambda idx: [bf16_shards[i] for i in idx]), (lambda idx: [split[i] for i in idx])
    z_big, *first_ici = _matmul(h, w_big, 'nt', [F32], "proj_in", stage=_gather_ici(sh(first), sp(first)))
    (z_small,) = _matmul(h, w_small, 'nt', [F32], "proj_in_narrow")
    o_gla, st_all, *first_all = _gla_fwd(seq(z_big), seq(z_small), w2h, gbh, Bl, S, D, stage=_gather_pass(first_ici, sp(first)))
    acts = [_conv_fwd(z_big, conv_full, grp, Bl, S, D) for grp in range(3)]
    o_dn, s_all, *second_ici = _dn_fwd(seq(acts[0]), seq(acts[1]), seq(acts[2]), seq(z_small), alog_w, dtb_w, Bl, S, D,
                                       stage=_gather_ici(sh(second), sp(second)))
    mixed, *second_all = _merge_fwd(tok(o_gla), tok(o_dn), z_big, gla_norm, dn_norm, D,
                                    stage=_gather_pass(second_ici, sp(second)))
    slots = {BIG[i][0]: own_slot(g, bf16_shards[i]) for i, g in zip(first + second, first_all + second_all)}
    rows_joined = lambda t: t.reshape(4 * t.shape[1], t.shape[2])
    w_out_f, w_down_f, w_pg_f = rows_joined(slots['w_out']), rows_joined(slots['w_down']), rows_joined(slots['w_ple_gate'])
    w_up_s, w_pp_s = slots['w_up'], slots['w_ple_proj']
    add_norm = lambda r, e, g: (lambda x_new: (x_new, _rms(x_new, g)))(e + r)
    x1, h2 = _matmul(mixed, w_out_f, 'nn', [F32, BF16], "proj_out", epilogue=add_norm, extras=(xt, g_mlp), bm=512, bn=D)
    u, act = _matmul(h2, w_up_s, 'nn', [F32, BF16], "mlp_up", b_slots=True,
                     epilogue=lambda r: (r, jnp.square(jnp.maximum(r, 0.0))))
    x2, h3 = _matmul(act, w_down_f, 'nn', [F32, BF16], "mlp_down", epilogue=add_norm, extras=(x1, g_ple), bm=512, bn=D)
    (pp,) = _matmul(pt, w_pp_s, 'nn', [F32], "ple_proj", b_slots=True)
    gp, x3 = _matmul(h3, w_pg_f, 'nn', [F32, F32], "ple_gate",
                     epilogue=lambda r, e, q: (r, e + _sigmoid(r) * q), extras=(x2, pp), bm=512)
    dx3, loss_tile, d_g_final = _loss_fwd_bwd(x3, g_final.reshape(1, D), tgt, "loss")

    d_gp, d_pp = _ple_bwd(dx3, gp, pp, "ple_bwd")
    (g_pp,) = _matmul(pt, d_pp, 'tn', [F32], "ple_proj_dw", out_slots=True)
    (g_pg,) = _matmul(h3, d_gp, 'tn', [F32], "ple_gate_dw")
    (dh3,) = _matmul(d_gp, w_pg_f, 'nt', [F32], "ple_gate_dx")
    dx2, dx2b, d_g_ple = _rmsnorm_bwd_add(x2, g_ple, dh3, dx3, "rms3_bwd")
    (g_down,) = _matmul(act, dx2b, 'tn', [F32], "mlp_down_dw")
    (du,) = _matmul(dx2b, w_down_f, 'nt', [BF16], "mlp_down_dx",
                    epilogue=lambda r, e: (r * 2.0 * jnp.maximum(e, 0.0),), extras=(u,))
    (g_up,) = _matmul(h2, du, 'tn', [F32], "mlp_up_dw", out_slots=True)
    by_rows = lambda g: g.reshape(4, g.shape[0] // 4, g.shape[1])
    send_mlp = [g_up, by_rows(g_down), by_rows(g_pg), g_pp]
    dh2, *sib_mlp = _matmul(du, w_up_s, 'nt', [F32], "mlp_up_dx", b_slots=True, stage=_pair_exchange(send_mlp, split[2:]))
    dx1, dx1b, d_g_mlp = _rmsnorm_bwd_add(x1, g_mlp, dh2, dx2, "rms2_bwd")
    (g_out,) = _matmul(mixed, dx1b, 'tn', [F32], "proj_out_dw")
    dmix, sib_out = _matmul(dx1b, w_out_f, 'nt', [F32], "proj_out_dx", stage=_pair_exchange([by_rows(g_out)], split[1:2]))
    rest = [n for n, _ in BIG[1:]]
    send_rest, sib_rest = [by_rows(g_out)] + send_mlp, [sib_out] + sib_mlp
    sums_rest = [_pair_sum(s, f, c_idx, f"grad_pair_sum_{n}") for n, s, f in zip(rest, send_rest, sib_rest)]
    d_ogla, d_gg, d_odn, d_dz, d_ga, d_gb, d_gla_norm, d_dn_norm = _merge_bwd(
        tok(o_gla), tok(o_dn), z_big, gla_norm, dn_norm, dmix, D)
    d_q, d_k, d_v, dzs_gla, d_w2h, d_gbh = _gla_bwd(seq(z_big), seq(z_small), w2h, gbh, st_all, seq(d_ogla), Bl, S, D)
    d_qa, d_ka, d_va, d_zs, d_alog_w, d_dtb_w, *chips_rest = _dn_bwd(
        seq(acts[0]), seq(acts[1]), seq(acts[2]), seq(z_small), alog_w, dtb_w, s_all, seq(d_odn), dzs_gla, Bl, S, D,
        stage=_chip_scatter([b for _, b in sums_rest]))
    conv_b = [_conv_bwd(z_big, conv_full, tok(g), grp, Bl, S, D) for grp, g in enumerate([d_qa, d_ka, d_va])]
    dz_big = jnp.concatenate([tok(d_q), tok(d_k), tok(d_v), d_gg, conv_b[0][0], conv_b[1][0], conv_b[2][0], d_dz, d_ga,
                              d_gb], axis=1)
    dz_small = tok(d_zs)
    cut = 6 * D
    (d_w_a,) = _matmul(dz_big, h, 'tn', [F32], "proj_in_dw_a", m_cols=(0, cut))
    d_w_b, sib_a = _matmul(dz_big, h, 'tn', [F32], "proj_in_dw_b", m_cols=(cut, 3 * D), stage=_pair_exchange([d_w_a], [COLS]))
    halves_rest = [_final_sum(f, got, me_idx, f"grad_final_sum_{n}") for n, (f, _), got in zip(rest, sums_rest, chips_rest)]
    d_w_small, sib_b = _matmul(dz_small, h, 'tn', [F32], "proj_in_narrow_dw", stage=_pair_exchange([d_w_b], [COLS]))
    (sib_s,) = _run_stage(_pair_exchange([d_w_small], [COLS]), "grad_pair_exchange_narrow")
    parts = [_pair_sum(mine[None], theirs[None], c_idx, f"grad_pair_sum_w_in_{tag}", COLS)
             for tag, mine, theirs in (("a", d_w_a, sib_a), ("b", d_w_b, sib_b), ("narrow", d_w_small, sib_s))]
    joined = lambda k: _join_w_in(parts[0][k][0], parts[1][k][0], parts[2][k][0], D).reshape(4, n_in, D // 2)
    sum_in_f32, sum_in_bf16 = joined(0), joined(1)
    dh_a, chips_in, *pair_rest = _matmul(dz_big, w_big, 'nn', [F32], "proj_in_dx",
                                         stage=_both(_chip_scatter([sum_in_bf16]), _pair_allgather(halves_rest, split[1:])))
    half_in = _final_sum(sum_in_f32, chips_in, me_idx, "grad_final_sum_w_in", split[0])
    (dh,) = _matmul(dz_small, w_small, 'nn', [F32], "proj_in_narrow_dx", epilogue=lambda r, e: (e + r,), extras=(dh_a,))
    grad_x, _, d_g_mix = _rmsnorm_bwd_add(xt, g_mix, dh, dx1, "rms1_bwd")
    (pair_in,) = _run_stage(_pair_allgather([half_in], split[:1]), "grad_pair_allgather_w_in")
    reduced = {n: lax.dynamic_update_slice(o, hlf, (ic * hlf.shape[0], 0)) for n, o, hlf in zip(rest, pair_rest, halves_rest)}
    south = ic == 0
    g_in_t = jnp.concatenate([jnp.where(south, half_in, pair_in), jnp.where(south, pair_in, half_in)],
                             axis=1).reshape(n_in, 1, D)

    d_w2 = jnp.swapaxes(d_w2h, 0, 1).reshape(ZS, D // 2)[:LOWRANK]
    small_grads = {'g_mix': d_g_mix, 'gla_w2': d_w2, 'gla_b': d_gbh.reshape(1, D // 2), 'gla_norm': d_gla_norm,
                   'dn_a_log': d_alog_w[:, 0, 0].reshape(1, DN_HEADS), 'dn_dt_bias': d_dtb_w[:, 0, 0].reshape(1, DN_HEADS),
                   'dn_norm': d_dn_norm, 'g_mlp': d_g_mlp, 'g_ple': d_g_ple, 'g_final': d_g_final}
    names = [n for n in SMALL if n != 'dn_conv']
    total = _allreduce_small([small_grads[n] for n in names] + [cb[1] for cb in conv_b] + [loss_tile[:1]],
                             "allreduce_small_grads")
    gsmall = dict(zip(names, total[:len(names)]))
    loss = total[-1][0, 0]
    my_cols = lambda g: lax.dynamic_slice_in_dim(g, j_me * (g.shape[1] // 4), g.shape[1] // 4, axis=1)
    gsmall['gla_w2'] = my_cols(gsmall['gla_w2'])
    gsmall['dn_conv'] = my_cols(jnp.concatenate(total[len(names):len(names) + 3], axis=1))

    g_o, d_o, m_o, v_o = {}, {}, {}, {}
    d_in_t, nm_in_t, nv_in_t, g_out_t = _adamw(w_in_t, g_in_t, m_in_t, v_in_t, "adamw_w_in", with_grad=True)
    g_o['w_in'], d_o['w_in'], m_o['w_in'], v_o['w_in'] = [cols_last(t) for t in (g_out_t, d_in_t, nm_in_t, nv_in_t)]
    for n, _ in BIG[1:]:
        shp = wts[n].shape
        d2, nm2, nv2 = _adamw(shard2d[n], reduced[n], as2d(mom[n]), as2d(var[n]), f"adamw_{n}")
        g_o[n], d_o[n], m_o[n], v_o[n] = reduced[n].reshape(shp), d2.reshape(shp), nm2.reshape(shp), nv2.reshape(shp)
    ds, nms, nvs = _adamw_small([as2d(wts[n]) for n in SMALL], [as2d(gsmall[n]) for n in SMALL],
                                [as2d(mom[n]) for n in SMALL], [as2d(var[n]) for n in SMALL])
    for n, dd, mm, vv in zip(SMALL, ds, nms, nvs):
        shp = wts[n].shape
        g_o[n], d_o[n], m_o[n], v_o[n] = gsmall[n].reshape(shp), dd.reshape(shp), mm.reshape(shp), vv.reshape(shp)

    return (loss, grad_x.reshape(Bl, S, D), *[g_o[n] for n in WEIGHTS], *[d_o[n] for n in WEIGHTS],
            *[m_o[n] for n in WEIGHTS], *[v_o[n] for n in WEIGHTS])
```

```python
import functools

import jax
import jax.numpy as jnp
from jax import lax
from jax.experimental import pallas as pl
from jax.experimental.pallas import tpu as pltpu

F32 = jnp.float32
BF16 = jnp.bfloat16

CHUNK = 64
GLA_HEADS = 4
DN_HEADS = 16
LOWRANK = 16
GLA_TAU = 16.0
DN_CONV = 4
EPS = 1e-6
ZS = 128
A_LANE, B_LANE = LOWRANK, LOWRANK + DN_HEADS
ADAM_LR, ADAM_B1, ADAM_B2, ADAM_EPS, ADAM_WD, ADAM_STEP = 0.001, 0.9, 0.999, 1e-08, 0.01, 10

V7X_VMEM_BYTES = 64 * 1024 * 1024
VMEM_LIMIT_BYTES = V7X_VMEM_BYTES - 8 * 1024 * 1024
LANES = 128
SUBLANES = 8
MESH = pl.DeviceIdType.MESH
DN_HEADS_PER_STEP = 16
GLA_HEADS_PER_STEP = 4

WEIGHTS = ['g_mix', 'w_in', 'gla_w2', 'gla_b', 'gla_norm', 'dn_conv', 'dn_a_log', 'dn_dt_bias', 'dn_norm', 'w_out',
           'g_mlp', 'w_up', 'w_down', 'g_ple', 'w_ple_gate', 'w_ple_proj', 'g_final']
BIG = [('w_in', 1), ('w_out', 0), ('w_up', 1), ('w_down', 0), ('w_ple_gate', 0), ('w_ple_proj', 1)]
SMALL = [n for n in WEIGHTS if n not in dict(BIG)]

_NN, _NT, _TN = 'nn', 'nt', 'tn'


def _params(sem=None):
    return pltpu.CompilerParams(dimension_semantics=sem, vmem_limit_bytes=VMEM_LIMIT_BYTES)


def _dot(a, b, form, precision=None):
    o = a.ndim - 2
    contract = {_NN: ((1 + o,), (o,)), _NT: ((1 + o,), (1 + o,)), _TN: ((o,), (o,))}[form]
    batch = ((0,), (0,)) if o else ((), ())
    return lax.dot_general(a, b, (contract, batch), precision=precision, preferred_element_type=F32)


def _make_mm(cast, precision):
    def raw(a, b, dims):
        return _dot(cast(a), cast(b), dims, precision)

    @jax.custom_vjp
    def nn(a, b):
        return raw(a, b, _NN)
    nn.defvjp(lambda a, b: (raw(a, b, _NN), (a, b)), lambda r, g: (raw(g, r[1], _NT), raw(r[0], g, _TN)))

    @jax.custom_vjp
    def nt(a, b):
        return raw(a, b, _NT)
    nt.defvjp(lambda a, b: (raw(a, b, _NT), (a, b)), lambda r, g: (raw(g, r[1], _NN), raw(g, r[0], _TN)))

    @jax.custom_vjp
    def tn(a, b):
        return raw(a, b, _TN)
    tn.defvjp(lambda a, b: (raw(a, b, _TN), (a, b)), lambda r, g: (raw(r[1], g, _NT), raw(r[0], g, _NN)))
    return nn, nt, tn


_bnn, _bnt, _btn = _make_mm(lambda t: t.astype(BF16), None)


def _iota2(n, axis):
    return lax.broadcasted_iota(jnp.int32, (n, n), axis)


def _lower(n, strict=False):
    return (_iota2(n, 0) > _iota2(n, 1)) if strict else (_iota2(n, 0) >= _iota2(n, 1))


def _tri_times(tri, x):
    tri = tri.astype(F32)
    if x.ndim == 3:
        tri = jnp.broadcast_to(tri, (x.shape[0],) + tri.shape)
    return _dot(tri, x, _NN, lax.Precision.HIGH)


@jax.custom_vjp
def _cumsum_rows(x):
    return _tri_times(_lower(x.shape[-2]), x)


def _cumsum_rows_bwd(_, g):
    n = g.shape[-2]
    return (_tri_times(_iota2(n, 0) <= _iota2(n, 1), g),)


_cumsum_rows.defvjp(lambda x: (_cumsum_rows(x), None), _cumsum_rows_bwd)


def _tri_inv_impl(a):
    n = a.shape[-1]
    eye = (_iota2(n, 0) == _iota2(n, 1)).astype(F32)
    p = eye - a
    ak = a
    k = 2
    while k < n:
        ak = _dot(ak.astype(BF16), ak.astype(BF16), _NN)
        p = p + _dot(p.astype(BF16), ak.astype(BF16), _NN)
        k *= 2
    return p


@jax.custom_vjp
def _tri_inv(a):
    return _tri_inv_impl(a)


def _tri_inv_fwd(a):
    t = _tri_inv_impl(a)
    return t, t


def _tri_inv_bwd(t, g):
    tb = t.astype(BF16)
    tg = _dot(tb, g.astype(BF16), _TN)
    return (-_dot(tg.astype(BF16), tb, _NT),)


_tri_inv.defvjp(_tri_inv_fwd, _tri_inv_bwd)


def _shift_rows(x, s, down):
    n = x.shape[0]
    r = lax.broadcasted_iota(jnp.int32, x.shape, 0)
    if down:
        return jnp.where(r >= s, pltpu.roll(x, s, 0), 0.0)
    return jnp.where(r < n - s, pltpu.roll(x, n - s, 0), 0.0)


def _make_shift(s):
    @jax.custom_vjp
    def f(x):
        return _shift_rows(x, s, True)
    f.defvjp(lambda x: (_shift_rows(x, s, True), None), lambda _, g: (_shift_rows(g, s, False),))
    return f


def _sigmoid(x):
    return jax.nn.sigmoid(x)


def _silu(x):
    return x * jax.nn.sigmoid(x)


def _softplus(x):
    return jnp.maximum(x, 0.0) + jnp.log1p(jnp.exp(-jnp.abs(x)))


def _log_sigmoid(x):
    return -_softplus(-x)


def _rms(x, g):
    return x * lax.rsqrt(jnp.mean(x * x, axis=-1, keepdims=True) + EPS) * g


def _gla_chunk(q, k, v, zs, w2, gb, st, *, scale):
    c = q.shape[-2]
    logf = _log_sigmoid(_bnn(zs, w2) + gb) * (1.0 / GLA_TAU)
    bcum = _cumsum_rows(logf)
    b_last = jnp.sum(logf, axis=-2, keepdims=True)
    q_in = (q * scale) * jnp.exp(bcum)
    k_in = k * jnp.exp(-bcum)
    a = jnp.where(_lower(c), _bnt(q_in, k_in), 0.0)
    o = _bnn(a, v) + _bnt(q_in, st)
    k_dec = k * jnp.exp(b_last - bcum)
    st_new = st * jnp.exp(b_last) + _btn(v, k_dec)
    return o, st_new


def _dn_chunk(q, k, v, aw, bw, alog, dtb, s):
    c = q.shape[-2]
    incl, strict = _lower(c), _lower(c, True)
    g_w = -jnp.exp(alog) * _softplus(aw + dtb)
    beta_w = _sigmoid(bw)
    gcum_w = _cumsum_rows(g_w)
    lane0 = lax.broadcasted_iota(jnp.int32, gcum_w.shape, gcum_w.ndim - 1) == 0
    gcol = jnp.sum(jnp.where(lane0, gcum_w, 0.0), axis=-1, keepdims=True)
    d1 = jnp.broadcast_to(gcol, gcol.shape[:-1] + (c,))
    diff = jnp.where(incl, d1 - jnp.swapaxes(d1, -1, -2), 0.0)
    decay = jnp.where(incl, jnp.exp(diff), 0.0)
    k_beta = k * beta_w
    a = jnp.where(strict, _bnt(k_beta, k) * decay, 0.0)
    t = _tri_inv(a)
    egc = jnp.exp(gcum_w)
    u = _bnn(t, v * beta_w)
    w = _bnn(t, k_beta * egc)
    attn = jnp.where(incl, _bnt(q, k) * decay, 0.0)
    q_dec = q * egc
    g_last = jnp.sum(g_w, axis=-2, keepdims=True)
    k_dec = k * jnp.exp(g_last - gcum_w)
    v_new = u - _bnn(w, s)
    o = _bnn(q_dec, s) + _bnn(attn, v_new)
    s_new = s * jnp.exp(g_last) + _btn(k_dec, v_new)
    return o, s_new


def _conv_act(x, wrows, *, l2, scale):
    taps = len(wrows)
    y = None
    for j in range(taps):
        s = taps - 1 - j
        xs = x if s == 0 else _make_shift(s)(x)
        y = wrows[j] * xs if y is None else y + wrows[j] * xs
    y = _silu(y)
    if l2:
        y = y * lax.rsqrt(jnp.sum(y * y, axis=-1, keepdims=True) + EPS) * scale
    return y


def _merge_math(og, gg, od, dz, ga, gb, gn, dn):
    nsub = len(og)
    dv = nsub * og[0].shape[1]
    ssq = jnp.sum(og[0] * og[0], axis=-1, keepdims=True)
    for s in range(1, nsub):
        ssq = ssq + jnp.sum(og[s] * og[s], axis=-1, keepdims=True)
    r = lax.rsqrt(ssq * (1.0 / dv) + EPS)
    outs = []
    for s in range(nsub):
        a = og[s] * r * gn[s] * _silu(gg[s])
        b = _rms(od[s], dn) * _silu(dz[s])
        outs.append(_sigmoid(ga[s]) * a + _sigmoid(gb[s]) * b)
    return outs


def _pick(n, target, mult):
    best = None
    for d in range(mult, min(n, target) + 1, mult):
        if n % d == 0:
            best = d
    return best if best is not None else n


class _Stage:
    def __init__(self, inputs, out_shapes, n_sems, copies, aliases=None):
        self.inputs, self.out_shapes, self.n_sems, self.copies = list(inputs), list(out_shapes), n_sems, copies
        self.aliases = aliases or {}

    @property
    def sems(self):
        return [pltpu.SemaphoreType.DMA((self.n_sems,)), pltpu.SemaphoreType.DMA((self.n_sems,))]


def _host_stage(body, stage, n_in, n_out, grid):
    ci, co = len(stage.inputs), len(stage.out_shapes)

    def wrapped(*refs):
        ins, cins = refs[:n_in], refs[n_in:n_in + ci]
        outs, couts = refs[n_in + ci:n_in + ci + n_out], refs[n_in + ci + n_out:n_in + ci + n_out + co]
        scratch, sems = refs[n_in + ci + n_out + co:-2], refs[-2:]
        ids = [pl.program_id(d) for d in range(len(grid))]
        first, last = ids[0] == 0, ids[0] == grid[0] - 1
        for i, g in zip(ids[1:], grid[1:]):
            first, last = first & (i == 0), last & (i == g - 1)

        @pl.when(first)
        def _():
            for cp in stage.copies(cins, couts, *sems):
                cp.start()

        body(*ins, *outs, *scratch)

        @pl.when(last)
        def _():
            for cp in stage.copies(cins, couts, *sems):
                cp.wait()

    return wrapped


def _call(body, name, grid, in_specs, out_specs, out_shape, scratch, semantics, args, stage=None):
    if stage is None:
        return pl.pallas_call(body, name=name, grid=grid, in_specs=list(in_specs), out_specs=list(out_specs),
                              out_shape=list(out_shape), scratch_shapes=list(scratch), compiler_params=_params(semantics))(*args)
    n_in, n_out = len(in_specs), len(out_specs)
    return pl.pallas_call(
        _host_stage(body, stage, n_in, n_out, grid), name=name, grid=grid,
        in_specs=list(in_specs) + [ANY] * len(stage.inputs),
        out_specs=list(out_specs) + [ANY] * len(stage.out_shapes), out_shape=list(out_shape) + stage.out_shapes,
        scratch_shapes=list(scratch) + stage.sems,
        input_output_aliases={n_in + i: n_out + o for i, o in stage.aliases.items()},
        compiler_params=_params(("arbitrary",) * len(grid)),
    )(*args, *stage.inputs)


def _run_stage(stage, name):
    ci = len(stage.inputs)

    def body(*refs):
        cps = stage.copies(refs[:ci], refs[ci:-2], *refs[-2:])
        for cp in cps:
            cp.start()
        for cp in cps:
            cp.wait()

    return pl.pallas_call(body, name=name, in_specs=[ANY] * ci, out_specs=[ANY] * len(stage.out_shapes),
                          out_shape=stage.out_shapes, scratch_shapes=stage.sems,
                          input_output_aliases=dict(stage.aliases))(*stage.inputs)


def _matmul(a, b, form, out_dtypes, name, epilogue=None, extras=(), bm=1024, bn=1024, bk=2048,
            b_slots=False, out_slots=False, stage=None, m_cols=None):
    ns, c = (b.shape[0], b.shape[2]) if b_slots else (1, None)
    b2 = b.shape[1:] if b_slots else b.shape
    if form == 'nn':
        (M, K), (K2, N) = a.shape, (b2[0], b2[1] * ns)
    elif form == 'nt':
        (M, K), (N, K2) = a.shape, (b2[0], b2[1] * ns)
    else:
        (K, M), (K2, N) = a.shape, b2
    m0 = 0
    if m_cols is not None:
        m0, M = m_cols
    assert K == K2 and not (b_slots and form == 'tn') and (m_cols is None or form == 'tn'), (a.shape, b.shape, form)
    bm, bn, bk = _pick(M, bm, SUBLANES), _pick(N, bn, LANES), _pick(K, bk, LANES)
    assert m0 % bm == 0
    if b_slots:
        bn, bk = (_pick(c, bn, LANES), bk) if form == 'nn' else (bn, _pick(c, bk, LANES))
    if out_slots:
        oc = N // 4
        bn = _pick(oc, bn, LANES)
    nk = K // bk
    a_spec = pl.BlockSpec((bk, bm), lambda i, j, k: (k, m0 // bm + i)) if form == 'tn' else pl.BlockSpec((bm, bk), lambda i, j, k: (i, k))
    if b_slots and form == 'nn':
        per = c // bn
        b_spec = pl.BlockSpec((None, bk, bn), lambda i, j, k: (j // per, k, j % per))
    elif b_slots:
        per = c // bk
        b_spec = pl.BlockSpec((None, bn, bk), lambda i, j, k: (k // per, j, k % per))
    elif form == 'nt':
        b_spec = pl.BlockSpec((bn, bk), lambda i, j, k: (j, k))
    else:
        b_spec = pl.BlockSpec((bk, bn), lambda i, j, k: (k, j))
    o_spec = pl.BlockSpec((bm, bn), lambda i, j, k: (i, j))
    if out_slots:
        oper = oc // bn
        out_spec = pl.BlockSpec((None, bm, bn), lambda i, j, k: (j // oper, i, j % oper))
        out_shape = [jax.ShapeDtypeStruct((4, M, oc), d) for d in out_dtypes]
    else:
        out_spec = o_spec
        out_shape = [jax.ShapeDtypeStruct((M, N), d) for d in out_dtypes]
    ne, no = len(extras), len(out_dtypes)

    def finish(r, extra_refs, out_refs):
        outs = (r,) if epilogue is None else epilogue(r, *[e[...] for e in extra_refs])
        for ref, o in zip(out_refs, outs):
            ref[...] = o.astype(ref.dtype)

    def body_one(a_ref, b_ref, *rest):
        finish(_dot(a_ref[...].astype(BF16), b_ref[...].astype(BF16), form), rest[:ne], rest[ne:ne + no])

    def body_acc(a_ref, b_ref, *rest):
        extra_refs, out_refs, acc = rest[:ne], rest[ne:ne + no], rest[ne + no]
        k = pl.program_id(2)
        part = _dot(a_ref[...].astype(BF16), b_ref[...].astype(BF16), form)

        @pl.when(k == 0)
        def _():
            acc[...] = part

        @pl.when((k > 0) & (k < nk - 1))
        def _():
            acc[...] += part

        @pl.when(k == nk - 1)
        def _():
            finish(acc[...] + part, extra_refs, out_refs)

    row_spec = pl.BlockSpec((1, bn), lambda i, j, k: (0, j))
    extra_specs = [o_spec if e.shape[0] == M else row_spec for e in extras]
    return _call(body_one if nk == 1 else body_acc, name, (M // bm, N // bn, nk), [a_spec, b_spec] + extra_specs,
                 [out_spec] * no, out_shape, [] if nk == 1 else [pltpu.VMEM((bm, bn), F32)],
                 ("parallel", "parallel", "arbitrary"), (a, b, *extras), stage)


def _rowwise(fn, rows, consts, row_outs, acc_outs, name, bt=256, stage=None):
    T = rows[0].shape[0]
    bt = _pick(T, bt, SUBLANES)
    nr, nc, no, na = len(rows), len(consts), len(row_outs), len(acc_outs)

    def body(*refs):
        r_in, c_in = refs[:nr], refs[nr:nr + nc]
        r_out, a_out = refs[nr + nc:nr + nc + no], refs[nr + nc + no:]
        ro, ao = fn([r[...] for r in r_in], [c[...] for c in c_in])
        for ref, o in zip(r_out, ro):
            ref[...] = o.astype(ref.dtype)
        if na:
            @pl.when(pl.program_id(0) == 0)
            def _():
                for ref in a_out:
                    ref[...] = jnp.zeros_like(ref)
            for ref, o in zip(a_out, ao):
                ref[...] += o

    whole = lambda shp: pl.BlockSpec(shp, lambda i: (0,) * len(shp))
    return _call(
        body, name, (T // bt,),
        [pl.BlockSpec((bt, r.shape[1]), lambda i: (i, 0)) for r in rows] + [whole(c.shape) for c in consts],
        [pl.BlockSpec((bt, w), lambda i: (i, 0)) for w, _ in row_outs] + [whole(s) for s in acc_outs],
        [jax.ShapeDtypeStruct((T, w), d) for w, d in row_outs] + [jax.ShapeDtypeStruct(s, F32) for s in acc_outs],
        [], ("arbitrary",), (*rows, *consts), stage)


def _rmsnorm_fwd(x, g, name, stage=None):
    return _rowwise(lambda r, c: ([_rms(r[0], c[0])], []), [x], [g], [(x.shape[1], BF16)], [], name, stage=stage)


def _rmsnorm_bwd_add(x, g, dh, dres, name):
    D = x.shape[1]

    def fn(r, c):
        _, vjp = jax.vjp(_rms, r[0], c[0])
        dx, dg = vjp(r[1])
        dx = dx + r[2]
        return [dx, dx], [dg]
    return _rowwise(fn, [x, dh, dres], [g], [(D, F32), (D, BF16)], [(1, D)], name)


def _loss_fwd_bwd(x3, g, target, name):
    D = x3.shape[1]

    def fn(r, c):
        def row_loss(x, gain):
            err = _rms(x, gain) - r[1]
            return 0.5 * jnp.mean(err * err, axis=-1, keepdims=True)
        lrow, vjp = jax.vjp(row_loss, r[0], c[0])
        dx, dg = vjp(jnp.ones_like(lrow))
        tile = jnp.broadcast_to(jnp.sum(lrow, axis=0, keepdims=True), (SUBLANES, LANES))
        return [dx], [tile, dg]
    return _rowwise(fn, [x3, target], [g], [(D, F32)], [(SUBLANES, LANES), (1, D)], name)


def _ple_bwd(dx3, gp, pp, name):
    D = dx3.shape[1]

    def fn(r, c):
        s = _sigmoid(r[1])
        return [r[0] * r[2] * s * (1.0 - s), r[0] * s], []
    return _rowwise(fn, [dx3, gp, pp], [], [(D, BF16), (D, BF16)], [], name)


def _adamw_math(w, g, m, v):
    nm = ADAM_B1 * m + (1.0 - ADAM_B1) * g
    nv = ADAM_B2 * v + (1.0 - ADAM_B2) * (g * g)
    m_hat = nm / (1.0 - ADAM_B1 ** ADAM_STEP)
    v_hat = nv / (1.0 - ADAM_B2 ** ADAM_STEP)
    return -ADAM_LR * (m_hat / (jnp.sqrt(v_hat) + ADAM_EPS) + ADAM_WD * w), nm, nv


def _adamw(w, g, m, v, name, with_grad=False):
    R, C = w.shape[0], w.shape[-1]
    lanes = -(-C // LANES) * LANES
    if w.ndim == 2:
        bt = _pick(R, max(SUBLANES, (1 << 18) // lanes // SUBLANES * SUBLANES), SUBLANES)
        spec = pl.BlockSpec((bt, C), lambda i: (i, 0))
    else:
        bt = _pick(R, max(1, (1 << 18) // lanes), 1)
        spec = pl.BlockSpec((bt, 1, C), lambda i: (i, 0, 0))

    def body(w_ref, g_ref, m_ref, v_ref, d_ref, nm_ref, nv_ref, *g_out):
        d_ref[...], nm_ref[...], nv_ref[...] = _adamw_math(w_ref[...], g_ref[...], m_ref[...], v_ref[...])
        for ref in g_out:
            ref[...] = g_ref[...]

    n_out = 4 if with_grad else 3
    return pl.pallas_call(
        body, name=name, grid=(R // bt,), in_specs=[spec] * 4, out_specs=[spec] * n_out,
        out_shape=[jax.ShapeDtypeStruct(w.shape, F32)] * n_out, compiler_params=_params(("parallel",)),
    )(w, g, m, v)


def _adamw_small(ws, gs, ms, vs):
    n = len(ws)

    def body(*refs):
        for i in range(n):
            d, nm, nv = _adamw_math(refs[i][...], refs[n + i][...], refs[2 * n + i][...], refs[3 * n + i][...])
            refs[4 * n + i][...], refs[5 * n + i][...], refs[6 * n + i][...] = d, nm, nv

    VMEM = pl.BlockSpec(memory_space=pltpu.VMEM)
    shapes = [jax.ShapeDtypeStruct(w.shape, F32) for w in ws]
    outs = pl.pallas_call(body, name="adamw_small", in_specs=[VMEM] * (4 * n), out_specs=[VMEM] * (3 * n),
                          out_shape=shapes * 3)(*ws, *gs, *ms, *vs)
    return outs[:n], outs[n:2 * n], outs[2 * n:]


def _gla_fwd(z_big, z_small, w2h, gbh, Bl, S, D, stage=None):
    NC, dk, dv, HB = S // CHUNK, D // (2 * GLA_HEADS), D // GLA_HEADS, GLA_HEADS_PER_STEP
    HG = GLA_HEADS // HB
    chains = [(hh, bb) for hh in range(HB) for bb in range(Bl)]
    G = len(chains)
    fn = functools.partial(_gla_chunk, scale=dk ** -0.5)

    def body(q, k, v, z, w2, gb, o_ref, stall_ref, st):
        n, g = pl.program_id(0), pl.program_id(1)

        @pl.when(n == 0)
        def _():
            st[g] = jnp.zeros((G, dv, dk), F32)
        s0 = st[g]
        stall_ref[...] = s0.reshape(HB, Bl, dv, dk)
        qk = lambda r: jnp.stack([r[bb, :, hh * dk:(hh + 1) * dk] for hh, bb in chains])
        o, s_new = fn(qk(q), qk(k), jnp.stack([v[bb, :, hh * dv:(hh + 1) * dv] for hh, bb in chains]),
                      jnp.stack([z[bb] for _, bb in chains]), jnp.stack([w2[hh] for hh, _ in chains]),
                      jnp.stack([gb[hh] for hh, _ in chains]), s0)
        for i, (hh, bb) in enumerate(chains):
            o_ref[bb, :, hh * dv:(hh + 1) * dv] = o[i]
        st[g] = s_new

    return _call(
        body, "gla_fwd", (NC, HG),
        [pl.BlockSpec((Bl, CHUNK, HB * dk), lambda n, g: (0, n, g)),
         pl.BlockSpec((Bl, CHUNK, HB * dk), lambda n, g: (0, n, HG + g)),
         pl.BlockSpec((Bl, CHUNK, HB * dv), lambda n, g: (0, n, HG + g)),
         pl.BlockSpec((Bl, CHUNK, ZS), lambda n, g: (0, n, 0)),
         pl.BlockSpec((HB, ZS, dk), lambda n, g: (g, 0, 0)),
         pl.BlockSpec((HB, 1, dk), lambda n, g: (g, 0, 0))],
        [pl.BlockSpec((Bl, CHUNK, HB * dv), lambda n, g: (0, n, g)),
         pl.BlockSpec((HB, Bl, None, dv, dk), lambda n, g: (g, 0, n, 0, 0))],
        [jax.ShapeDtypeStruct((Bl, S, D), F32), jax.ShapeDtypeStruct((GLA_HEADS, Bl, NC, dv, dk), F32)],
        [pltpu.VMEM((HG, G, dv, dk), F32)], ("arbitrary", "arbitrary"), (z_big, z_big, z_big, z_small, w2h, gbh), stage)


def _gla_bwd(z_big, z_small, w2h, gbh, st_all, do, Bl, S, D):
    NC, dk, dv, HB = S // CHUNK, D // (2 * GLA_HEADS), D // GLA_HEADS, GLA_HEADS_PER_STEP
    HG = GLA_HEADS // HB
    chains = [(hh, bb) for hh in range(HB) for bb in range(Bl)]
    G = len(chains)
    fn = functools.partial(_gla_chunk, scale=dk ** -0.5)

    def body(q, k, v, z, w2, gb, st0, do_ref, dq_ref, dk_ref, dv_ref, dzs_ref, dw2_ref, dgb_ref, dst):
        n, g = pl.program_id(0), pl.program_id(1)

        @pl.when(n == 0)
        def _():
            dst[g] = jnp.zeros((G, dv, dk), F32)

        @pl.when((n == 0) & (g == 0))
        def _():
            dw2_ref[...] = jnp.zeros_like(dw2_ref)
            dgb_ref[...] = jnp.zeros_like(dgb_ref)

        qk = lambda r: jnp.stack([r[bb, :, hh * dk:(hh + 1) * dk] for hh, bb in chains])
        vv = lambda r: jnp.stack([r[bb, :, hh * dv:(hh + 1) * dv] for hh, bb in chains])
        _, vjp = jax.vjp(fn, qk(q), qk(k), vv(v), jnp.stack([z[bb] for _, bb in chains]),
                         jnp.stack([w2[hh] for hh, _ in chains]), jnp.stack([gb[hh] for hh, _ in chains]),
                         st0[...].reshape(G, dv, dk))
        dq, dkk, dvv, dzs, dw2, dgb, dst0 = vjp((vv(do_ref), dst[g]))
        for i, (hh, bb) in enumerate(chains):
            dq_ref[bb, :, hh * dk:(hh + 1) * dk] = dq[i].astype(dq_ref.dtype)
            dk_ref[bb, :, hh * dk:(hh + 1) * dk] = dkk[i].astype(dk_ref.dtype)
            dv_ref[bb, :, hh * dv:(hh + 1) * dv] = dvv[i].astype(dv_ref.dtype)
            dw2_ref[g * HB + hh] += dw2[i]
            dgb_ref[g * HB + hh] += dgb[i]
        for bb in range(Bl):
            tot = sum(dzs[i] for i, (_, b2) in enumerate(chains) if b2 == bb)

            @pl.when(g == 0)
            def _():
                dzs_ref[bb] = tot

            @pl.when(g > 0)
            def _():
                dzs_ref[bb] += tot
        dst[g] = dst0

    rn = lambda n: NC - 1 - n
    return pl.pallas_call(
        body, name="gla_bwd", grid=(NC, HG),
        in_specs=[pl.BlockSpec((Bl, CHUNK, HB * dk), lambda n, g: (0, rn(n), g)),
                  pl.BlockSpec((Bl, CHUNK, HB * dk), lambda n, g: (0, rn(n), HG + g)),
                  pl.BlockSpec((Bl, CHUNK, HB * dv), lambda n, g: (0, rn(n), HG + g)),
                  pl.BlockSpec((Bl, CHUNK, ZS), lambda n, g: (0, rn(n), 0)),
                  pl.BlockSpec((HB, ZS, dk), lambda n, g: (g, 0, 0)),
                  pl.BlockSpec((HB, 1, dk), lambda n, g: (g, 0, 0)),
                  pl.BlockSpec((HB, Bl, None, dv, dk), lambda n, g: (g, 0, rn(n), 0, 0)),
                  pl.BlockSpec((Bl, CHUNK, HB * dv), lambda n, g: (0, rn(n), g))],
        out_specs=[pl.BlockSpec((Bl, CHUNK, HB * dk), lambda n, g: (0, rn(n), g)),
                   pl.BlockSpec((Bl, CHUNK, HB * dk), lambda n, g: (0, rn(n), g)),
                   pl.BlockSpec((Bl, CHUNK, HB * dv), lambda n, g: (0, rn(n), g)),
                   pl.BlockSpec((Bl, CHUNK, ZS), lambda n, g: (0, rn(n), 0)),
                   pl.BlockSpec((GLA_HEADS, ZS, dk), lambda n, g: (0, 0, 0)),
                   pl.BlockSpec((GLA_HEADS, 1, dk), lambda n, g: (0, 0, 0))],
        out_shape=[jax.ShapeDtypeStruct((Bl, S, D // 2), BF16), jax.ShapeDtypeStruct((Bl, S, D // 2), BF16),
                   jax.ShapeDtypeStruct((Bl, S, D), BF16), jax.ShapeDtypeStruct((Bl, S, ZS), F32),
                   jax.ShapeDtypeStruct((GLA_HEADS, ZS, dk), F32), jax.ShapeDtypeStruct((GLA_HEADS, 1, dk), F32)],
        scratch_shapes=[pltpu.VMEM((HG, G, dv, dk), F32)],
        compiler_params=_params(("arbitrary", "arbitrary")),
    )(z_big, z_big, z_big, z_small, w2h, gbh, st_all, do)


def _conv_fwd(z_big, conv_w, grp, Bl, S, D):
    d = D // DN_HEADS
    l2, scale = grp < 2, (d ** -0.5 if grp == 0 else 1.0)
    x_blk0 = (3 * D + grp * D) // d

    def body(x_ref, w_ref, o_ref):
        wrows = [w_ref[j:j + 1, :] for j in range(DN_CONV)]
        o_ref[...] = _conv_act(x_ref[...], wrows, l2=l2, scale=scale)

    return pl.pallas_call(
        body, name=f"conv_fwd{grp}", grid=(Bl, DN_HEADS),
        in_specs=[pl.BlockSpec((S, d), lambda b, j: (b, x_blk0 + j)),
                  pl.BlockSpec((DN_CONV, d), lambda b, j: (0, grp * DN_HEADS + j))],
        out_specs=pl.BlockSpec((S, d), lambda b, j: (b, j)),
        out_shape=jax.ShapeDtypeStruct((Bl * S, D), F32),
        compiler_params=_params(("parallel", "parallel")),
    )(z_big, conv_w)


def _conv_bwd(z_big, conv_w, dact, grp, Bl, S, D):
    d = D // DN_HEADS
    l2, scale = grp < 2, (d ** -0.5 if grp == 0 else 1.0)
    x_blk0 = (3 * D + grp * D) // d

    def body(x_ref, w_ref, g_ref, dx_ref, dw_ref):
        @pl.when(pl.program_id(1) == 0)
        def _():
            dw_ref[...] = jnp.zeros_like(dw_ref)
        wrows = [w_ref[j:j + 1, :] for j in range(DN_CONV)]
        _, vjp = jax.vjp(lambda x, wr: _conv_act(x, wr, l2=l2, scale=scale), x_ref[...], wrows)
        dx, dwr = vjp(g_ref[...])
        dx_ref[...] = dx.astype(dx_ref.dtype)
        for j in range(DN_CONV):
            dw_ref[j:j + 1, :] += dwr[j]

    return pl.pallas_call(
        body, name=f"conv_bwd{grp}", grid=(DN_HEADS, Bl),
        in_specs=[pl.BlockSpec((S, d), lambda j, b: (b, x_blk0 + j)),
                  pl.BlockSpec((DN_CONV, d), lambda j, b: (0, grp * DN_HEADS + j)),
                  pl.BlockSpec((S, d), lambda j, b: (b, j))],
        out_specs=[pl.BlockSpec((S, d), lambda j, b: (b, j)), pl.BlockSpec((DN_CONV, d), lambda j, b: (0, j))],
        out_shape=[jax.ShapeDtypeStruct((Bl * S, D), BF16), jax.ShapeDtypeStruct((DN_CONV, D), F32)],
        compiler_params=_params(("arbitrary", "arbitrary")),
    )(z_big, conv_w, dact)


def _lane_column(zb, lane, width):
    pick = lax.broadcasted_iota(jnp.int32, zb.shape, 1) == lane
    return jnp.broadcast_to(jnp.sum(jnp.where(pick, zb, 0.0), axis=-1, keepdims=True), (zb.shape[0], width))


def _dn_fwd(qa, ka, va, z_small, alog, dtb, Bl, S, D, stage=None):
    NC, d, HB = S // CHUNK, D // DN_HEADS, DN_HEADS_PER_STEP
    HG = DN_HEADS // HB
    chains = [(hh, bb) for hh in range(HB) for bb in range(Bl)]
    G = len(chains)

    def body(q, k, v, z, al, dt, o_ref, sall_ref, st):
        n, g = pl.program_id(0), pl.program_id(1)

        @pl.when(n == 0)
        def _():
            st[g] = jnp.zeros((G, d, d), F32)
        tok_in = lambda r: jnp.stack([r[bb, :, hh * d:(hh + 1) * d] for hh, bb in chains])
        head_in = lambda r: jnp.stack([r[hh] for hh, _ in chains])
        gate_in = lambda lane0: jnp.stack([_lane_column(z[bb], lane0 + g * HB + hh, d) for hh, bb in chains])
        s0 = st[g]
        sall_ref[...] = s0.reshape(HB, Bl, d, d)
        o, s_new = _dn_chunk(tok_in(q), tok_in(k), tok_in(v), gate_in(A_LANE), gate_in(B_LANE), head_in(al), head_in(dt), s0)
        for i, (hh, bb) in enumerate(chains):
            o_ref[bb, :, hh * d:(hh + 1) * d] = o[i]
        st[g] = s_new

    tok = pl.BlockSpec((Bl, CHUNK, HB * d), lambda n, g: (0, n, g))
    per_head = pl.BlockSpec((HB, 1, d), lambda n, g: (g, 0, 0))
    return _call(
        body, "dn_fwd", (NC, HG),
        [tok, tok, tok, pl.BlockSpec((Bl, CHUNK, ZS), lambda n, g: (0, n, 0)), per_head, per_head],
        [tok, pl.BlockSpec((HB, Bl, None, d, d), lambda n, g: (g, 0, n, 0, 0))],
        [jax.ShapeDtypeStruct((Bl, S, D), F32), jax.ShapeDtypeStruct((DN_HEADS, Bl, NC, d, d), F32)],
        [pltpu.VMEM((HG, G, d, d), F32)], ("arbitrary", "arbitrary"), (qa, ka, va, z_small, alog, dtb), stage)


def _dn_bwd(qa, ka, va, z_small, alog, dtb, s_all, do, dzs_gla, Bl, S, D, stage=None):
    NC, d, HB = S // CHUNK, D // DN_HEADS, DN_HEADS_PER_STEP
    HG = DN_HEADS // HB
    chains = [(hh, bb) for hh in range(HB) for bb in range(Bl)]
    G = len(chains)

    def lanesum(t):
        return jnp.sum(t, axis=-1, keepdims=True)

    def body(q, k, v, z, al, dt, s0_ref, do_ref, dzg_ref, dq_ref, dk_ref, dv_ref, dzs_ref, dal_ref, ddt_ref, dst):
        n, g = pl.program_id(0), pl.program_id(1)

        @pl.when(n == 0)
        def _():
            dst[g] = jnp.zeros((G, d, d), F32)

        @pl.when((n == 0) & (g == 0))
        def _():
            dal_ref[...] = jnp.zeros_like(dal_ref)
            ddt_ref[...] = jnp.zeros_like(ddt_ref)

        tok_in = lambda r: jnp.stack([r[bb, :, hh * d:(hh + 1) * d] for hh, bb in chains])
        head_in = lambda r: jnp.stack([r[hh] for hh, _ in chains])
        gate_in = lambda lane0: jnp.stack([_lane_column(z[bb], lane0 + g * HB + hh, d) for hh, bb in chains])
        _, vjp = jax.vjp(_dn_chunk, tok_in(q), tok_in(k), tok_in(v), gate_in(A_LANE), gate_in(B_LANE), head_in(al),
                         head_in(dt), s0_ref[...].reshape(G, d, d))
        dq, dkk, dvv, da, db, dal, ddt, ds0 = vjp((tok_in(do_ref), dst[g]))
        da, db = lanesum(da), lanesum(db)
        dal = jnp.broadcast_to(lanesum(dal), (G, 1, d))
        ddt = jnp.broadcast_to(lanesum(ddt), (G, 1, d))
        lane = lax.broadcasted_iota(jnp.int32, (CHUNK, ZS), 1)
        for bb in range(Bl):
            part = jnp.zeros((CHUNK, ZS), F32)
            for i, (hh, b2) in enumerate(chains):
                if b2 == bb:
                    h = g * HB + hh
                    part = part + jnp.where(lane == A_LANE + h, da[i], 0.0) + jnp.where(lane == B_LANE + h, db[i], 0.0)

            @pl.when(g == 0)
            def _():
                dzs_ref[bb] = jnp.where(lane < LOWRANK, dzg_ref[bb], 0.0) + part

            @pl.when(g > 0)
            def _():
                dzs_ref[bb] += part
        for i, (hh, bb) in enumerate(chains):
            cols = slice(hh * d, (hh + 1) * d)
            dq_ref[bb, :, cols] = dq[i]
            dk_ref[bb, :, cols] = dkk[i]
            dv_ref[bb, :, cols] = dvv[i]
            dal_ref[g * HB + hh] += dal[i]
            ddt_ref[g * HB + hh] += ddt[i]
        dst[g] = ds0

    rn = lambda n: NC - 1 - n
    tok = pl.BlockSpec((Bl, CHUNK, HB * d), lambda n, g: (0, rn(n), g))
    zsb = pl.BlockSpec((Bl, CHUNK, ZS), lambda n, g: (0, rn(n), 0))
    per_head = pl.BlockSpec((HB, 1, d), lambda n, g: (g, 0, 0))
    all_heads = pl.BlockSpec((DN_HEADS, 1, d), lambda n, g: (0, 0, 0))
    tok_shape = jax.ShapeDtypeStruct((Bl, S, D), F32)
    head_shape = jax.ShapeDtypeStruct((DN_HEADS, 1, d), F32)
    return _call(
        body, "dn_bwd", (NC, HG),
        [tok, tok, tok, zsb, per_head, per_head,
         pl.BlockSpec((HB, Bl, None, d, d), lambda n, g: (g, 0, rn(n), 0, 0)), tok, zsb],
        [tok, tok, tok, zsb, all_heads, all_heads],
        [tok_shape, tok_shape, tok_shape, jax.ShapeDtypeStruct((Bl, S, ZS), F32), head_shape, head_shape],
        [pltpu.VMEM((HG, G, d, d), F32)], ("arbitrary", "arbitrary"),
        (qa, ka, va, z_small, alog, dtb, s_all, do, dzs_gla), stage)


def _merge_specs(D, bt):
    dv, w = D // GLA_HEADS, D // DN_HEADS
    col = lambda off: pl.BlockSpec((bt, dv), lambda i, h: (i, off // dv + h))
    return dv, w, col


def _merge_load(refs, nsub, w):
    return [[r[:, s * w:(s + 1) * w] for s in range(nsub)] for r in refs]


def _merge_fwd(o_gla, o_dn, z_big, gla_norm, dn_norm, D, bt=256, stage=None):
    T = o_gla.shape[0]
    bt = _pick(T, bt, SUBLANES)
    dv, w, col = _merge_specs(D, bt)
    nsub = dv // w

    def body(og, gg, od, dz, ga, gb, gn, dn, out):
        ogl, ggl, odl, dzl, gal, gbl = _merge_load([og, gg, od, dz, ga, gb], nsub, w)
        gnl = [gn[:, s * w:(s + 1) * w] for s in range(nsub)]
        outs = _merge_math(ogl, ggl, odl, dzl, gal, gbl, gnl, dn[...])
        for s in range(nsub):
            out[:, s * w:(s + 1) * w] = outs[s].astype(out.dtype)

    return _call(
        body, "merge_fwd", (T // bt, GLA_HEADS),
        [col(0), col(2 * D), col(0), col(6 * D), col(7 * D), col(8 * D),
         pl.BlockSpec((1, dv), lambda i, h: (0, 0)), pl.BlockSpec((1, w), lambda i, h: (0, 0))],
        [col(0)], [jax.ShapeDtypeStruct((T, D), BF16)], [], ("parallel", "parallel"),
        (o_gla, z_big, o_dn, z_big, z_big, z_big, gla_norm, dn_norm), stage)


def _merge_bwd(o_gla, o_dn, z_big, gla_norm, dn_norm, dmix, D, bt=256):
    T = o_gla.shape[0]
    bt = _pick(T, bt, SUBLANES)
    dv, w, col = _merge_specs(D, bt)
    nsub = dv // w

    def body(og, gg, od, dz, ga, gb, gn, dn, dm, dog, dgg, dod, ddz, dga, dgb, dgn, ddn):
        @pl.when((pl.program_id(0) == 0) & (pl.program_id(1) == 0))
        def _():
            dgn[...] = jnp.zeros_like(dgn)
            ddn[...] = jnp.zeros_like(ddn)

        ogl, ggl, odl, dzl, gal, gbl, dml = _merge_load([og, gg, od, dz, ga, gb, dm], nsub, w)
        gnl = [gn[:, s * w:(s + 1) * w] for s in range(nsub)]
        _, vjp = jax.vjp(_merge_math, ogl, ggl, odl, dzl, gal, gbl, gnl, dn[...])
        g_og, g_gg, g_od, g_dz, g_ga, g_gb, g_gn, g_dn = vjp(dml)
        for s in range(nsub):
            sl = slice(s * w, (s + 1) * w)
            dog[:, sl] = g_og[s]
            dgg[:, sl] = g_gg[s].astype(dgg.dtype)
            dod[:, sl] = g_od[s]
            ddz[:, sl] = g_dz[s].astype(ddz.dtype)
            dga[:, sl] = g_ga[s].astype(dga.dtype)
            dgb[:, sl] = g_gb[s].astype(dgb.dtype)
            dgn[:, sl] += g_gn[s]
        ddn[...] += g_dn

    f32s, bf16s = jax.ShapeDtypeStruct((T, D), F32), jax.ShapeDtypeStruct((T, D), BF16)
    return pl.pallas_call(
        body, name="merge_bwd", grid=(T // bt, GLA_HEADS),
        in_specs=[col(0), col(2 * D), col(0), col(6 * D), col(7 * D), col(8 * D),
                  pl.BlockSpec((1, dv), lambda i, h: (0, 0)), pl.BlockSpec((1, w), lambda i, h: (0, 0)), col(0)],
        out_specs=[col(0)] * 6 + [pl.BlockSpec((1, dv), lambda i, h: (0, 0)), pl.BlockSpec((1, w), lambda i, h: (0, 0))],
        out_shape=[f32s, bf16s, f32s, bf16s, bf16s, bf16s,
                   jax.ShapeDtypeStruct((1, dv), F32), jax.ShapeDtypeStruct((1, w), F32)],
        compiler_params=_params(("arbitrary", "arbitrary")),
    )(o_gla, z_big, o_dn, z_big, z_big, z_big, gla_norm, dn_norm, dmix)


def _place():
    return lax.axis_index("x"), lax.axis_index("y"), lax.axis_index("c")


def _other_chips(x, y):
    return [(1 - x, y), (x, 1 - y), (1 - x, 1 - y)]


def _rcopy(src, dst, send_sem, recv_sem, dev):
    return pltpu.make_async_remote_copy(src_ref=src, dst_ref=dst, send_sem=send_sem, recv_sem=recv_sem,
                                        device_id=dev, device_id_type=MESH)


ANY = pl.BlockSpec(memory_space=pl.ANY)


ROWS, COLS = 'rows', 'cols'


def _half(ref, hc, by, lead=()):
    shape = ref.shape[len(lead):]
    if by == ROWS:
        rh = shape[0] // 2
        idx = (pl.ds(pl.multiple_of(hc * rh, 16), rh),) + (slice(None),) * (len(shape) - 1)
    else:
        ch = shape[-1] // 2
        idx = (slice(None),) * (len(shape) - 1) + (pl.ds(pl.multiple_of(hc * ch, LANES), ch),)
    return ref.at[(*lead, *idx)]


def _half_shape(shape, by):
    return (shape[0] // 2,) + tuple(shape[1:]) if by == ROWS else tuple(shape[:-1]) + (shape[-1] // 2,)


def _gather_ici(shards, by):
    nw = len(shards)

    def copies(srcs, outs, send_sems, recv_sems):
        x, y, c = _place()
        return [_rcopy(_half(srcs[w], c, by[w]), _half(outs[w], c, by[w], (2 * x + y,)),
                       send_sems.at[3 * w + k], recv_sems.at[3 * w + k], (px, py, c))
                for w in range(nw) for k, (px, py) in enumerate(_other_chips(x, y))]

    return _Stage(shards, [jax.ShapeDtypeStruct((4,) + s.shape, s.dtype) for s in shards], 3 * nw, copies)


def _gather_neighbours(shards, by):
    nw = len(shards)

    def copies(srcs, outs, send_sems, recv_sems):
        x, y, c = _place()
        return [_rcopy(_half(srcs[w], c, by[w]), _half(outs[w], c, by[w], (2 * x + y,)),
                       send_sems.at[2 * w + k], recv_sems.at[2 * w + k], (px, py, c))
                for w in range(nw) for k, (px, py) in enumerate(_other_chips(x, y)[:2])]

    return _Stage(shards, [jax.ShapeDtypeStruct((4,) + s.shape, s.dtype) for s in shards], 2 * nw, copies)


def _gather_relay(gathered):
    nw = len(gathered)

    def copies(srcs, outs, send_sems, recv_sems):
        x, y, c = _place()
        cps = []
        for w in range(nw):
            _, n, cols = gathered[w].shape
            cut, ch = n // 2 // 16 * 16, cols // 2
            lanes = pl.ds(pl.multiple_of(c * ch, LANES), ch)
            via = [(2 * (1 - x) + y, pl.ds(0, cut), (x, 1 - y, c)),
                   (2 * x + (1 - y), pl.ds(cut, n - cut), (1 - x, y, c))]
            for k, (slot, rows, dev) in enumerate(via):
                cps.append(_rcopy(srcs[w].at[slot, rows, lanes], outs[w].at[slot, rows, lanes],
                                  send_sems.at[2 * w + k], recv_sems.at[2 * w + k], dev))
        return cps

    return _Stage(gathered, [jax.ShapeDtypeStruct(g.shape, g.dtype) for g in gathered], 2 * nw, copies,
                  aliases={w: w for w in range(nw)})


def _gather_pass(gathered, by):
    nw = len(gathered)

    def copies(srcs, outs, send_sems, recv_sems):
        x, y, c = _place()
        cps = []
        for w in range(nw):
            for k, (px, py) in enumerate(_other_chips(x, y)):
                slot = (2 * px + py,)
                cps.append(_rcopy(_half(srcs[w], c, by[w], slot), _half(outs[w], c, by[w], slot),
                                  send_sems.at[3 * w + k], recv_sems.at[3 * w + k], (x, y, 1 - c)))
        return cps

    return _Stage(gathered, [jax.ShapeDtypeStruct(g.shape, g.dtype) for g in gathered], 3 * nw, copies,
                  aliases={w: w for w in range(nw)})


def _pair_exchange(ps, by):
    nw = len(ps)

    def copies(srcs, outs, send_sems, recv_sems):
        x, y, c = _place()
        return [_rcopy(_half(srcs[w], 1 - c, by[w], (slice(None),) * (ps[w].ndim - 2)), outs[w], send_sems.at[w], recv_sems.at[w],
                       (x, y, 1 - c)) for w in range(nw)]

    return _Stage(ps, [jax.ShapeDtypeStruct(p.shape[:-2] + _half_shape(p.shape[-2:], b), p.dtype) for p, b in zip(ps, by)],
                  nw, copies)


def _sum_blocks(half_shape, by):
    rh, ch = half_shape
    lanes = -(-ch // LANES) * LANES
    bt = _pick(rh, max(16, (3 << 18) // lanes // 16 * 16), 16)
    if by == ROWS:
        nb = rh // bt
        return (bt, ch), nb, (lambda i: (i, 0)), (lambda i, c: (c * nb + i, 0))
    if rh % bt == 0 and bt % 16 == 0:
        return (bt, ch), rh // bt, (lambda i: (i, 0)), (lambda i, c: (i, c))
    bc = _pick(ch, max(LANES, (5 << 18) // rh // LANES * LANES), LANES)
    nb = ch // bc
    return (rh, bc), nb, (lambda i: (0, i)), (lambda i, c: (0, c * nb + i))


def _pair_sum(p, got, c_idx, name, by=ROWS):
    lead, hs = got.shape[0], got.shape[1:]
    blk, nb, pos, pos_whole = _sum_blocks(hs, by)

    def body(c_ref, a, b, of, ob):
        s = a[...] + b[...]
        of[...] = s
        ob[...] = s.astype(BF16)

    spec = pl.BlockSpec((None,) + blk, lambda j, i, c_ref: (j,) + pos(i))
    return pl.pallas_call(
        body, name=name,
        grid_spec=pltpu.PrefetchScalarGridSpec(
            num_scalar_prefetch=1, grid=(lead, nb),
            in_specs=[pl.BlockSpec((None,) + blk, lambda j, i, c_ref: (j,) + pos_whole(i, c_ref[0])), spec],
            out_specs=[spec, spec]),
        out_shape=[jax.ShapeDtypeStruct((lead,) + hs, F32), jax.ShapeDtypeStruct((lead,) + hs, BF16)],
        compiler_params=_params(("parallel", "parallel")),
    )(c_idx, p, got)


def _chip_scatter(qbs):
    nw = len(qbs)

    def copies(srcs, outs, send_sems, recv_sems):
        x, y, c = _place()
        return [_rcopy(srcs[w].at[2 * px + py], outs[w].at[k], send_sems.at[3 * w + k], recv_sems.at[3 * w + k], (px, py, c))
                for w in range(nw) for k, (px, py) in enumerate(_other_chips(x, y))]

    return _Stage(qbs, [jax.ShapeDtypeStruct((3,) + q.shape[1:], q.dtype) for q in qbs], 3 * nw, copies)


def _final_sum(qf, got, me_idx, name, by=ROWS):
    hs = qf.shape[1:]
    blk, nb, pos, _ = _sum_blocks(hs, by)

    def body(me_ref, a, b, o):
        o[...] = ((a[...] + b[0].astype(F32)) + b[1].astype(F32)) + b[2].astype(F32)

    return pl.pallas_call(
        body, name=name,
        grid_spec=pltpu.PrefetchScalarGridSpec(
            num_scalar_prefetch=1, grid=(nb,),
            in_specs=[pl.BlockSpec((None,) + blk, lambda i, me_ref: (me_ref[0],) + pos(i)),
                      pl.BlockSpec((3,) + blk, lambda i, me_ref: (0,) + pos(i))],
            out_specs=pl.BlockSpec(blk, lambda i, me_ref: pos(i))),
        out_shape=jax.ShapeDtypeStruct(hs, F32),
        compiler_params=_params(("parallel",)),
    )(me_idx, qf, got)


def _pair_allgather(halves, by):
    nw = len(halves)
    whole = [(2 * h.shape[0], h.shape[1]) if b == ROWS else h.shape for h, b in zip(halves, by)]

    def copies(srcs, outs, send_sems, recv_sems):
        x, y, c = _place()
        there = lambda w: _half(outs[w], c, ROWS) if by[w] == ROWS else outs[w]
        return [_rcopy(srcs[w], there(w), send_sems.at[w], recv_sems.at[w], (x, y, 1 - c)) for w in range(nw)]

    return _Stage(halves, [jax.ShapeDtypeStruct(s, h.dtype) for s, h in zip(whole, halves)], nw, copies)


class _SemaphoreWindow:
    def __init__(self, ref, off):
        self.ref, self.off = ref, off

    @property
    def at(self):
        return self

    def __getitem__(self, i):
        return self.ref.at[self.off + i]


def _both(a, b):
    na, ma = len(a.inputs), len(a.out_shapes)

    def copies(ins, outs, send_sems, recv_sems):
        return (a.copies(ins[:na], outs[:ma], send_sems, recv_sems) +
                b.copies(ins[na:], outs[ma:], _SemaphoreWindow(send_sems, a.n_sems), _SemaphoreWindow(recv_sems, a.n_sems)))

    return _Stage(a.inputs + b.inputs, a.out_shapes + b.out_shapes, a.n_sems + b.n_sems, copies,
                  aliases={**a.aliases, **{na + i: ma + o for i, o in b.aliases.items()}})


def _small_exchange(items, out_shapes, finish, name):
    n = len(items)
    offs, rows = [], 0
    for it in items:
        offs.append(rows)
        rows += it.shape[0]
    rows = -(-rows // SUBLANES) * SUBLANES
    width = -(-max(it.shape[1] for it in items) // LANES) * LANES
    VMEM = pl.BlockSpec(memory_space=pltpu.VMEM)

    def body(*refs):
        ins, outs = refs[:n], refs[n:n + len(out_shapes)]
        buf, send_sems, recv_sems = refs[n + len(out_shapes):]
        x, y, c = _place()
        me = 4 * x + 2 * y + c
        flip = lambda v, f: (1 - v) if f else v
        peers = [(flip(x, r >> 2 & 1), flip(y, r >> 1 & 1), flip(c, r & 1)) for r in range(1, 8)]
        buf[me] = jnp.zeros((rows, width), F32)
        for it, off, ref in zip(items, offs, ins):
            buf[me, off:off + it.shape[0], 0:it.shape[1]] = ref[...]
        cps = [_rcopy(buf.at[me], buf.at[me], send_sems.at[k], recv_sems.at[k], dev) for k, dev in enumerate(peers)]
        for cp in cps:
            cp.start()
        for k, (px, py, pc) in enumerate(peers):
            slot = buf.at[4 * px + 2 * py + pc]
            _rcopy(slot, slot, send_sems.at[k], recv_sems.at[k], (px, py, pc)).wait_recv()
        for cp in cps:
            cp.wait_send()
        finish(buf, offs, outs)

    return pl.pallas_call(
        body, name=name, in_specs=[VMEM] * n, out_specs=[VMEM] * len(out_shapes),
        out_shape=[jax.ShapeDtypeStruct(s, F32) for s in out_shapes],
        scratch_shapes=[pltpu.VMEM((8, rows, width), F32), pltpu.SemaphoreType.DMA((7,)), pltpu.SemaphoreType.DMA((7,))],
        compiler_params=pltpu.CompilerParams(vmem_limit_bytes=VMEM_LIMIT_BYTES),
    )(*items)


def _allreduce_small(items, name):
    def finish(buf, offs, outs):
        for it, off, out in zip(items, offs, outs):
            region = lambda d: buf[d, off:off + it.shape[0], 0:it.shape[1]]
            s = region(0)
            for d in range(1, 8):
                s = s + region(d)
            out[...] = s
    return _small_exchange(items, [it.shape for it in items], finish, name)


def _allgather_small_shards(items, name):
    def finish(buf, offs, outs):
        for it, off, out in zip(items, offs, outs):
            r, c = it.shape
            for j in range(4):
                out[:, j * c:(j + 1) * c] = buf[2 * j, off:off + r, 0:c]
    return _small_exchange(items, [(it.shape[0], 4 * it.shape[1]) for it in items], finish, name)


def _split_w_in(wt, D):
    pad = jnp.zeros((ZS - 3 * LOWRANK, wt.shape[1]), wt.dtype)
    big = jnp.concatenate([wt[:3 * D], wt[3 * D + 16:6 * D + 16], wt[6 * D + 16:7 * D + 16], wt[7 * D + 48:]], axis=0)
    small = jnp.concatenate([wt[3 * D:3 * D + 16], wt[7 * D + 16:7 * D + 48], pad], axis=0)
    return big, small


def _join_w_in(ga, gb, gs, D):
    return jnp.concatenate([ga[:3 * D], gs[:16], ga[3 * D:], gb[:D], gs[16:48], gb[D:]], axis=0)


def kernel(x, p, g_mix, w_in, gla_w2, gla_b, gla_norm, dn_conv, dn_a_log, dn_dt_bias, dn_norm, w_out, g_mlp, w_up, w_down, g_ple, w_ple_gate, w_ple_proj, g_final, loss_target, m_g_mix, m_w_in, m_gla_w2, m_gla_b, m_gla_norm, m_dn_conv, m_dn_a_log, m_dn_dt_bias, m_dn_norm, m_w_out, m_g_mlp, m_w_up, m_w_down, m_g_ple, m_w_ple_gate, m_w_ple_proj, m_g_final, v_g_mix, v_w_in, v_gla_w2, v_gla_b, v_gla_norm, v_dn_conv, v_dn_a_log, v_dn_dt_bias, v_dn_norm, v_w_out, v_g_mlp, v_w_up, v_w_down, v_g_ple, v_w_ple_gate, v_w_ple_proj, v_g_final):
    wts = dict(zip(WEIGHTS, [g_mix, w_in, gla_w2, gla_b, gla_norm, dn_conv, dn_a_log, dn_dt_bias, dn_norm, w_out, g_mlp,
                             w_up, w_down, g_ple, w_ple_gate, w_ple_proj, g_final]))
    mom = dict(zip(WEIGHTS, [m_g_mix, m_w_in, m_gla_w2, m_gla_b, m_gla_norm, m_dn_conv, m_dn_a_log, m_dn_dt_bias, m_dn_norm,
                             m_w_out, m_g_mlp, m_w_up, m_w_down, m_g_ple, m_w_ple_gate, m_w_ple_proj, m_g_final]))
    var = dict(zip(WEIGHTS, [v_g_mix, v_w_in, v_gla_w2, v_gla_b, v_gla_norm, v_dn_conv, v_dn_a_log, v_dn_dt_bias, v_dn_norm,
                             v_w_out, v_g_mlp, v_w_up, v_w_down, v_g_ple, v_w_ple_gate, v_w_ple_proj, v_g_final]))
    Bl, S, D = x.shape
    T = Bl * S
    PLE = p.shape[-1]
    dn_d, gla_dk = D // DN_HEADS, D // (2 * GLA_HEADS)
    ix, iy, ic = _place()
    j_me = 2 * ix + iy
    as2d = lambda a: a.reshape(a.shape[-2], a.shape[-1]) if a.ndim > 1 else a.reshape(1, -1)
    c_idx, me_idx = ic.reshape(1).astype(jnp.int32), j_me.reshape(1).astype(jnp.int32)

    rows_first = lambda a: jnp.transpose(a, (2, 0, 1))
    cols_last = lambda a: jnp.transpose(a, (1, 2, 0))
    w_in_t, m_in_t, v_in_t = rows_first(w_in), rows_first(m_w_in), rows_first(v_w_in)
    n_in = w_in_t.shape[0]
    shard2d = {n: as2d(wts[n]) for n, _ in BIG[1:]}
    bf16_shards = [w_in_t.astype(BF16).reshape(n_in, D)] + [shard2d[n].astype(BF16) for n, _ in BIG[1:]]
    split = [COLS] + [ROWS] * (len(BIG) - 1)
    own_slot = lambda g, s: lax.dynamic_update_slice(g, s[None], (j_me, 0, 0))
    xt = x.reshape(T, D)
    (w_in_near,) = _run_stage(_gather_neighbours(bf16_shards[:1], split[:1]), "allgather_w_in_neighbours")
    (w_in_ici,) = _run_stage(_gather_relay([w_in_near]), "allgather_w_in_relay")
    h, w_in_all = _rmsnorm_fwd(xt, g_mix, "rms1_fwd", stage=_gather_pass([w_in_ici], split[:1]))
    w_in_slots = own_slot(w_in_all, bf16_shards[0])
    w_big, w_small = _split_w_in(w_in_slots.reshape(4 * n_in, D), D)

    w2_full, conv_full = _allgather_small_shards([as2d(gla_w2), as2d(dn_conv)], "allgather_small_weights")
    w2pad = jnp.pad(w2_full, ((0, ZS - LOWRANK), (0, 0)))
    w2h = jnp.swapaxes(w2pad.reshape(ZS, GLA_HEADS, gla_dk), 0, 1)
    gbh = gla_b.reshape(GLA_HEADS, 1, gla_dk)
    alog_w = jnp.broadcast_to(dn_a_log.reshape(DN_HEADS, 1, 1), (DN_HEADS, 1, dn_d))
    dtb_w = jnp.broadcast_to(dn_dt_bias.reshape(DN_HEADS, 1, 1), (DN_HEADS, 1, dn_d))

    tgt = loss_target.reshape(T, D)
    pt = p.reshape(T, PLE)
    seq = lambda t: t.reshape(Bl, S, t.shape[-1])
    tok = lambda t: t.reshape(T, t.shape[-1])
    first, second = [1, 2, 5], [3, 4]
    sh, sp = (lambda idx: [bf16_shards[i] for i in idx]), (lambda idx: [split[i] for i in idx])
    z_big, *first_ici = _matmul(h, w_big, 'nt', [F32], "proj_in", stage=_gather_ici(sh(first), sp(first)))
    (z_small,) = _matmul(h, w_small, 'nt', [F32], "proj_in_narrow")
    o_gla, st_all, *first_all = _gla_fwd(seq(z_big), seq(z_small), w2h, gbh, Bl, S, D, stage=_gather_pass(first_ici, sp(first)))
    acts = [_conv_fwd(z_big, conv_full, grp, Bl, S, D) for grp in range(3)]
    o_dn, s_all, *second_ici = _dn_fwd(seq(acts[0]), seq(acts[1]), seq(acts[2]), seq(z_small), alog_w, dtb_w, Bl, S, D,
                                       stage=_gather_ici(sh(second), sp(second)))
    mixed, *second_all = _merge_fwd(tok(o_gla), tok(o_dn), z_big, gla_norm, dn_norm, D,
                                    stage=_gather_pass(second_ici, sp(second)))
    slots = {BIG[i][0]: own_slot(g, bf16_shards[i]) for i, g in zip(first + second, first_all + second_all)}
    rows_joined = lambda t: t.reshape(4 * t.shape[1], t.shape[2])
    w_out_f, w_down_f, w_pg_f = rows_joined(slots['w_out']), rows_joined(slots['w_down']), rows_joined(slots['w_ple_gate'])
    w_up_s, w_pp_s = slots['w_up'], slots['w_ple_proj']
    add_norm = lambda r, e, g: (lambda x_new: (x_new, _rms(x_new, g)))(e + r)
    x1, h2 = _matmul(mixed, w_out_f, 'nn', [F32, BF16], "proj_out", epilogue=add_norm, extras=(xt, g_mlp), bm=512, bn=D)
    u, act = _matmul(h2, w_up_s, 'nn', [F32, BF16], "mlp_up", b_slots=True,
                     epilogue=lambda r: (r, jnp.square(jnp.maximum(r, 0.0))))
    x2, h3 = _matmul(act, w_down_f, 'nn', [F32, BF16], "mlp_down", epilogue=add_norm, extras=(x1, g_ple), bm=512, bn=D)
    (pp,) = _matmul(pt, w_pp_s, 'nn', [F32], "ple_proj", b_slots=True)
    gp, x3 = _matmul(h3, w_pg_f, 'nn', [F32, F32], "ple_gate",
                     epilogue=lambda r, e, q: (r, e + _sigmoid(r) * q), extras=(x2, pp), bm=512)
    dx3, loss_tile, d_g_final = _loss_fwd_bwd(x3, g_final.reshape(1, D), tgt, "loss")

    d_gp, d_pp = _ple_bwd(dx3, gp, pp, "ple_bwd")
    (g_pp,) = _matmul(pt, d_pp, 'tn', [F32], "ple_proj_dw", out_slots=True)
    (g_pg,) = _matmul(h3, d_gp, 'tn', [F32], "ple_gate_dw")
    (dh3,) = _matmul(d_gp, w_pg_f, 'nt', [F32], "ple_gate_dx")
    dx2, dx2b, d_g_ple = _rmsnorm_bwd_add(x2, g_ple, dh3, dx3, "rms3_bwd")
    (g_down,) = _matmul(act, dx2b, 'tn', [F32], "mlp_down_dw")
    (du,) = _matmul(dx2b, w_down_f, 'nt', [BF16], "mlp_down_dx",
                    epilogue=lambda r, e: (r * 2.0 * jnp.maximum(e, 0.0),), extras=(u,))
    (g_up,) = _matmul(h2, du, 'tn', [F32], "mlp_up_dw", out_slots=True)
    by_rows = lambda g: g.reshape(4, g.shape[0] // 4, g.shape[1])
    send_mlp = [g_up, by_rows(g_down), by_rows(g_pg), g_pp]
    dh2, *sib_mlp = _matmul(du, w_up_s, 'nt', [F32], "mlp_up_dx", b_slots=True, stage=_pair_exchange(send_mlp, split[2:]))
    dx1, dx1b, d_g_mlp = _rmsnorm_bwd_add(x1, g_mlp, dh2, dx2, "rms2_bwd")
    (g_out,) = _matmul(mixed, dx1b, 'tn', [F32], "proj_out_dw")
    dmix, sib_out = _matmul(dx1b, w_out_f, 'nt', [F32], "proj_out_dx", stage=_pair_exchange([by_rows(g_out)], split[1:2]))
    rest = [n for n, _ in BIG[1:]]
    send_rest, sib_rest = [by_rows(g_out)] + send_mlp, [sib_out] + sib_mlp
    sums_rest = [_pair_sum(s, f, c_idx, f"grad_pair_sum_{n}") for n, s, f in zip(rest, send_rest, sib_rest)]
    d_ogla, d_gg, d_odn, d_dz, d_ga, d_gb, d_gla_norm, d_dn_norm = _merge_bwd(
        tok(o_gla), tok(o_dn), z_big, gla_norm, dn_norm, dmix, D)
    d_q, d_k, d_v, dzs_gla, d_w2h, d_gbh = _gla_bwd(seq(z_big), seq(z_small), w2h, gbh, st_all, seq(d_ogla), Bl, S, D)
    d_qa, d_ka, d_va, d_zs, d_alog_w, d_dtb_w, *chips_rest = _dn_bwd(
        seq(acts[0]), seq(acts[1]), seq(acts[2]), seq(z_small), alog_w, dtb_w, s_all, seq(d_odn), dzs_gla, Bl, S, D,
        stage=_chip_scatter([b for _, b in sums_rest]))
    conv_b = [_conv_bwd(z_big, conv_full, tok(g), grp, Bl, S, D) for grp, g in enumerate([d_qa, d_ka, d_va])]
    dz_big = jnp.concatenate([tok(d_q), tok(d_k), tok(d_v), d_gg, conv_b[0][0], conv_b[1][0], conv_b[2][0], d_dz, d_ga,
                              d_gb], axis=1)
    dz_small = tok(d_zs)
    cut = 6 * D
    (d_w_a,) = _matmul(dz_big, h, 'tn', [F32], "proj_in_dw_a", m_cols=(0, cut))
    d_w_b, sib_a = _matmul(dz_big, h, 'tn', [F32], "proj_in_dw_b", m_cols=(cut, 3 * D), stage=_pair_exchange([d_w_a], [COLS]))
    halves_rest = [_final_sum(f, got, me_idx, f"grad_final_sum_{n}") for n, (f, _), got in zip(rest, sums_rest, chips_rest)]
    d_w_small, sib_b = _matmul(dz_small, h, 'tn', [F32], "proj_in_narrow_dw", stage=_pair_exchange([d_w_b], [COLS]))
    (sib_s,) = _run_stage(_pair_exchange([d_w_small], [COLS]), "grad_pair_exchange_narrow")
    parts = [_pair_sum(mine[None], theirs[None], c_idx, f"grad_pair_sum_w_in_{tag}", COLS)
             for tag, mine, theirs in (("a", d_w_a, sib_a), ("b", d_w_b, sib_b), ("narrow", d_w_small, sib_s))]
    joined = lambda k: _join_w_in(parts[0][k][0], parts[1][k][0], parts[2][k][0], D).reshape(4, n_in, D // 2)
    sum_in_f32, sum_in_bf16 = joined(0), joined(1)
    dh_a, chips_in, *pair_rest = _matmul(dz_big, w_big, 'nn', [F32], "proj_in_dx",
                                         stage=_both(_chip_scatter([sum_in_bf16]), _pair_allgather(halves_rest, split[1:])))
    half_in = _final_sum(sum_in_f32, chips_in, me_idx, "grad_final_sum_w_in", split[0])
    (dh,) = _matmul(dz_small, w_small, 'nn', [F32], "proj_in_narrow_dx", epilogue=lambda r, e: (e + r,), extras=(dh_a,))
    grad_x, _, d_g_mix = _rmsnorm_bwd_add(xt, g_mix, dh, dx1, "rms1_bwd")
    (pair_in,) = _run_stage(_pair_allgather([half_in], split[:1]), "grad_pair_allgather_w_in")
    reduced = {n: lax.dynamic_update_slice(o, hlf, (ic * hlf.shape[0], 0)) for n, o, hlf in zip(rest, pair_rest, halves_rest)}
    south = ic == 0
    g_in_t = jnp.concatenate([jnp.where(south, half_in, pair_in), jnp.where(south, pair_in, half_in)],
                             axis=1).reshape(n_in, 1, D)

    d_w2 = jnp.swapaxes(d_w2h, 0, 1).reshape(ZS, D // 2)[:LOWRANK]
    small_grads = {'g_mix': d_g_mix, 'gla_w2': d_w2, 'gla_b': d_gbh.reshape(1, D // 2), 'gla_norm': d_gla_norm,
                   'dn_a_log': d_alog_w[:, 0, 0].reshape(1, DN_HEADS), 'dn_dt_bias': d_dtb_w[:, 0, 0].reshape(1, DN_HEADS),
                   'dn_norm': d_dn_norm, 'g_mlp': d_g_mlp, 'g_ple': d_g_ple, 'g_final': d_g_final}
    names = [n for n in SMALL if n != 'dn_conv']
    total = _allreduce_small([small_grads[n] for n in names] + [cb[1] for cb in conv_b] + [loss_tile[:1]],
                             "allreduce_small_grads")
    gsmall = dict(zip(names, total[:len(names)]))
    loss = total[-1][0, 0]
    my_cols = lambda g: lax.dynamic_slice_in_dim(g, j_me * (g.shape[1] // 4), g.shape[1] // 4, axis=1)
    gsmall['gla_w2'] = my_cols(gsmall['gla_w2'])
    gsmall['dn_conv'] = my_cols(jnp.concatenate(total[len(names):len(names) + 3], axis=1))

    g_o, d_o, m_o, v_o = {}, {}, {}, {}
    d_in_t, nm_in_t, nv_in_t, g_out_t = _adamw(w_in_t, g_in_t, m_in_t, v_in_t, "adamw_w_in", with_grad=True)
    g_o['w_in'], d_o['w_in'], m_o['w_in'], v_o['w_in'] = [cols_last(t) for t in (g_out_t, d_in_t, nm_in_t, nv_in_t)]
    for n, _ in BIG[1:]:
        shp = wts[n].shape
        d2, nm2, nv2 = _adamw(shard2d[n], reduced[n], as2d(mom[n]), as2d(var[n]), f"adamw_{n}")
        g_o[n], d_o[n], m_o[n], v_o[n] = reduced[n].reshape(shp), d2.reshape(shp), nm2.reshape(shp), nv2.reshape(shp)
    ds, nms, nvs = _adamw_small([as2d(wts[n]) for n in SMALL], [as2d(gsmall[n]) for n in SMALL],
                                [as2d(mom[n]) for n in SMALL], [as2d(var[n]) for n in SMALL])
    for n, dd, mm, vv in zip(SMALL, ds, nms, nvs):
        shp = wts[n].shape
        g_o[n], d_o[n], m_o[n], v_o[n] = gsmall[n].reshape(shp), dd.reshape(shp), mm.reshape(shp), vv.reshape(shp)

    return (loss, grad_x.reshape(Bl, S, D), *[g_o[n] for n in WEIGHTS], *[d_o[n] for n in WEIGHTS],
            *[m_o[n] for n in WEIGHTS], *[v_o[n] for n in WEIGHTS])
```

```python
import functools

import jax
import jax.numpy as jnp
from jax import lax
from jax.experimental import pallas as pl
from jax.experimental.pallas import tpu as pltpu

F32 = jnp.float32
BF16 = jnp.bfloat16

CHUNK = 64
GLA_HEADS = 4
DN_HEADS = 16
LOWRANK = 16
GLA_TAU = 16.0
DN_CONV = 4
EPS = 1e-6
ZS = 128
A_LANE, B_LANE = LOWRANK, LOWRANK + DN_HEADS
ADAM_LR, ADAM_B1, ADAM_B2, ADAM_EPS, ADAM_WD, ADAM_STEP = 0.001, 0.9, 0.999, 1e-08, 0.01, 10

V7X_VMEM_BYTES = 64 * 1024 * 1024
VMEM_LIMIT_BYTES = V7X_VMEM_BYTES - 8 * 1024 * 1024
LANES = 128
SUBLANES = 8
MESH = pl.DeviceIdType.MESH
DN_HEADS_PER_STEP = 16
GLA_HEADS_PER_STEP = 4

WEIGHTS = ['g_mix', 'w_in', 'gla_w2', 'gla_b', 'gla_norm', 'dn_conv', 'dn_a_log', 'dn_dt_bias', 'dn_norm', 'w_out',
           'g_mlp', 'w_up', 'w_down', 'g_ple', 'w_ple_gate', 'w_ple_proj', 'g_final']
BIG = [('w_in', 1), ('w_out', 0), ('w_up', 1), ('w_down', 0), ('w_ple_gate', 0), ('w_ple_proj', 1)]
SMALL = [n for n in WEIGHTS if n not in dict(BIG)]

_NN, _NT, _TN = 'nn', 'nt', 'tn'


def _params(sem=None):
    return pltpu.CompilerParams(dimension_semantics=sem, vmem_limit_bytes=VMEM_LIMIT_BYTES)


def _dot(a, b, form, precision=None):
    o = a.ndim - 2
    contract = {_NN: ((1 + o,), (o,)), _NT: ((1 + o,), (1 + o,)), _TN: ((o,), (o,))}[form]
    batch = ((0,), (0,)) if o else ((), ())
    return lax.dot_general(a, b, (contract, batch), precision=precision, preferred_element_type=F32)


def _make_mm(cast, precision):
    def raw(a, b, dims):
        return _dot(cast(a), cast(b), dims, precision)

    @jax.custom_vjp
    def nn(a, b):
        return raw(a, b, _NN)
    nn.defvjp(lambda a, b: (raw(a, b, _NN), (a, b)), lambda r, g: (raw(g, r[1], _NT), raw(r[0], g, _TN)))

    @jax.custom_vjp
    def nt(a, b):
        return raw(a, b, _NT)
    nt.defvjp(lambda a, b: (raw(a, b, _NT), (a, b)), lambda r, g: (raw(g, r[1], _NN), raw(g, r[0], _TN)))

    @jax.custom_vjp
    def tn(a, b):
        return raw(a, b, _TN)
    tn.defvjp(lambda a, b: (raw(a, b, _TN), (a, b)), lambda r, g: (raw(r[1], g, _NT), raw(r[0], g, _NN)))
    return nn, nt, tn


_bnn, _bnt, _btn = _make_mm(lambda t: t.astype(BF16), None)


def _iota2(n, axis):
    return lax.broadcasted_iota(jnp.int32, (n, n), axis)


def _lower(n, strict=False):
    return (_iota2(n, 0) > _iota2(n, 1)) if strict else (_iota2(n, 0) >= _iota2(n, 1))


def _tri_times(tri, x):
    tri = tri.astype(F32)
    if x.ndim == 3:
        tri = jnp.broadcast_to(tri, (x.shape[0],) + tri.shape)
    return _dot(tri, x, _NN, lax.Precision.HIGH)


@jax.custom_vjp
def _cumsum_rows(x):
    return _tri_times(_lower(x.shape[-2]), x)


def _cumsum_rows_bwd(_, g):
    n = g.shape[-2]
    return (_tri_times(_iota2(n, 0) <= _iota2(n, 1), g),)


_cumsum_rows.defvjp(lambda x: (_cumsum_rows(x), None), _cumsum_rows_bwd)


def _tri_inv_impl(a):
    n = a.shape[-1]
    eye = (_iota2(n, 0) == _iota2(n, 1)).astype(F32)
    p = eye - a
    ak = a
    k = 2
    while k < n:
        ak = _dot(ak.astype(BF16), ak.astype(BF16), _NN)
        p = p + _dot(p.astype(BF16), ak.astype(BF16), _NN)
        k *= 2
    return p


@jax.custom_vjp
def _tri_inv(a):
    return _tri_inv_impl(a)


def _tri_inv_fwd(a):
    t = _tri_inv_impl(a)
    return t, t


def _tri_inv_bwd(t, g):
    tb = t.astype(BF16)
    tg = _dot(tb, g.astype(BF16), _TN)
    return (-_dot(tg.astype(BF16), tb, _NT),)


_tri_inv.defvjp(_tri_inv_fwd, _tri_inv_bwd)


def _shift_rows(x, s, down):
    n = x.shape[0]
    r = lax.broadcasted_iota(jnp.int32, x.shape, 0)
    if down:
        return jnp.where(r >= s, pltpu.roll(x, s, 0), 0.0)
    return jnp.where(r < n - s, pltpu.roll(x, n - s, 0), 0.0)


def _make_shift(s):
    @jax.custom_vjp
    def f(x):
        return _shift_rows(x, s, True)
    f.defvjp(lambda x: (_shift_rows(x, s, True), None), lambda _, g: (_shift_rows(g, s, False),))
    return f


def _sigmoid(x):
    return jax.nn.sigmoid(x)


def _silu(x):
    return x * jax.nn.sigmoid(x)


def _softplus(x):
    return jnp.maximum(x, 0.0) + jnp.log1p(jnp.exp(-jnp.abs(x)))


def _log_sigmoid(x):
    return -_softplus(-x)


def _rms(x, g):
    return x * lax.rsqrt(jnp.mean(x * x, axis=-1, keepdims=True) + EPS) * g


def _gla_chunk(q, k, v, zs, w2, gb, st, *, scale):
    c = q.shape[-2]
    logf = _log_sigmoid(_bnn(zs, w2) + gb) * (1.0 / GLA_TAU)
    bcum = _cumsum_rows(logf)
    b_last = jnp.sum(logf, axis=-2, keepdims=True)
    q_in = (q * scale) * jnp.exp(bcum)
    k_in = k * jnp.exp(-bcum)
    a = jnp.where(_lower(c), _bnt(q_in, k_in), 0.0)
    o = _bnn(a, v) + _bnt(q_in, st)
    k_dec = k * jnp.exp(b_last - bcum)
    st_new = st * jnp.exp(b_last) + _btn(v, k_dec)
    return o, st_new


def _dn_chunk(q, k, v, aw, bw, alog, dtb, s):
    c = q.shape[-2]
    incl, strict = _lower(c), _lower(c, True)
    g_w = -jnp.exp(alog) * _softplus(aw + dtb)
    beta_w = _sigmoid(bw)
    gcum_w = _cumsum_rows(g_w)
    lane0 = lax.broadcasted_iota(jnp.int32, gcum_w.shape, gcum_w.ndim - 1) == 0
    gcol = jnp.sum(jnp.where(lane0, gcum_w, 0.0), axis=-1, keepdims=True)
    d1 = jnp.broadcast_to(gcol, gcol.shape[:-1] + (c,))
    diff = jnp.where(incl, d1 - jnp.swapaxes(d1, -1, -2), 0.0)
    decay = jnp.where(incl, jnp.exp(diff), 0.0)
    k_beta = k * beta_w
    a = jnp.where(strict, _bnt(k_beta, k) * decay, 0.0)
    t = _tri_inv(a)
    egc = jnp.exp(gcum_w)
    u = _bnn(t, v * beta_w)
    w = _bnn(t, k_beta * egc)
    attn = jnp.where(incl, _bnt(q, k) * decay, 0.0)
    q_dec = q * egc
    g_last = jnp.sum(g_w, axis=-2, keepdims=True)
    k_dec = k * jnp.exp(g_last - gcum_w)
    v_new = u - _bnn(w, s)
    o = _bnn(q_dec, s) + _bnn(attn, v_new)
    s_new = s * jnp.exp(g_last) + _btn(k_dec, v_new)
    return o, s_new


def _conv_act(x, wrows, *, l2, scale):
    taps = len(wrows)
    y = None
    for j in range(taps):
        s = taps - 1 - j
        xs = x if s == 0 else _make_shift(s)(x)
        y = wrows[j] * xs if y is None else y + wrows[j] * xs
    y = _silu(y)
    if l2:
        y = y * lax.rsqrt(jnp.sum(y * y, axis=-1, keepdims=True) + EPS) * scale
    return y


def _merge_math(og, gg, od, dz, ga, gb, gn, dn):
    nsub = len(og)
    dv = nsub * og[0].shape[1]
    ssq = jnp.sum(og[0] * og[0], axis=-1, keepdims=True)
    for s in range(1, nsub):
        ssq = ssq + jnp.sum(og[s] * og[s], axis=-1, keepdims=True)
    r = lax.rsqrt(ssq * (1.0 / dv) + EPS)
    outs = []
    for s in range(nsub):
        a = og[s] * r * gn[s] * _silu(gg[s])
        b = _rms(od[s], dn) * _silu(dz[s])
        outs.append(_sigmoid(ga[s]) * a + _sigmoid(gb[s]) * b)
    return outs


def _pick(n, target, mult):
    best = None
    for d in range(mult, min(n, target) + 1, mult):
        if n % d == 0:
            best = d
    return best if best is not None else n


class _Stage:
    def __init__(self, inputs, out_shapes, n_sems, copies, aliases=None):
        self.inputs, self.out_shapes, self.n_sems, self.copies = list(inputs), list(out_shapes), n_sems, copies
        self.aliases = aliases or {}

    @property
    def sems(self):
        return [pltpu.SemaphoreType.DMA((self.n_sems,)), pltpu.SemaphoreType.DMA((self.n_sems,))]


def _host_stage(body, stage, n_in, n_out, grid):
    ci, co = len(stage.inputs), len(stage.out_shapes)

    def wrapped(*refs):
        ins, cins = refs[:n_in], refs[n_in:n_in + ci]
        outs, couts = refs[n_in + ci:n_in + ci + n_out], refs[n_in + ci + n_out:n_in + ci + n_out + co]
        scratch, sems = refs[n_in + ci + n_out + co:-2], refs[-2:]
        ids = [pl.program_id(d) for d in range(len(grid))]
        first, last = ids[0] == 0, ids[0] == grid[0] - 1
        for i, g in zip(ids[1:], grid[1:]):
            first, last = first & (i == 0), last & (i == g - 1)

        @pl.when(first)
        def _():
            for cp in stage.copies(cins, couts, *sems):
                cp.start()

        body(*ins, *outs, *scratch)

        @pl.when(last)
        def _():
            for cp in stage.copies(cins, couts, *sems):
                cp.wait()

    return wrapped


def _call(body, name, grid, in_specs, out_specs, out_shape, scratch, semantics, args, stage=None):
    if stage is None:
        return pl.pallas_call(body, name=name, grid=grid, in_specs=list(in_specs), out_specs=list(out_specs),
                              out_shape=list(out_shape), scratch_shapes=list(scratch), compiler_params=_params(semantics))(*args)
    n_in, n_out = len(in_specs), len(out_specs)
    return pl.pallas_call(
        _host_stage(body, stage, n_in, n_out, grid), name=name, grid=grid,
        in_specs=list(in_specs) + [ANY] * len(stage.inputs),
        out_specs=list(out_specs) + [ANY] * len(stage.out_shapes), out_shape=list(out_shape) + stage.out_shapes,
        scratch_shapes=list(scratch) + stage.sems,
        input_output_aliases={n_in + i: n_out + o for i, o in stage.aliases.items()},
        compiler_params=_params(("arbitrary",) * len(grid)),
    )(*args, *stage.inputs)


def _run_stage(stage, name):
    ci = len(stage.inputs)

    def body(*refs):
        cps = stage.copies(refs[:ci], refs[ci:-2], *refs[-2:])
        for cp in cps:
            cp.start()
        for cp in cps:
            cp.wait()

    return pl.pallas_call(body, name=name, in_specs=[ANY] * ci, out_specs=[ANY] * len(stage.out_shapes),
                          out_shape=stage.out_shapes, scratch_shapes=stage.sems,
                          input_output_aliases=dict(stage.aliases))(*stage.inputs)


def _matmul(a, b, form, out_dtypes, name, epilogue=None, extras=(), bm=1024, bn=1024, bk=2048,
            b_slots=False, out_slots=False, stage=None, m_cols=None):
    ns, c = (b.shape[0], b.shape[2]) if b_slots else (1, None)
    b2 = b.shape[1:] if b_slots else b.shape
    if form == 'nn':
        (M, K), (K2, N) = a.shape, (b2[0], b2[1] * ns)
    elif form == 'nt':
        (M, K), (N, K2) = a.shape, (b2[0], b2[1] * ns)
    else:
        (K, M), (K2, N) = a.shape, b2
    m0 = 0
    if m_cols is not None:
        m0, M = m_cols
    assert K == K2 and not (b_slots and form == 'tn') and (m_cols is None or form == 'tn'), (a.shape, b.shape, form)
    bm, bn, bk = _pick(M, bm, SUBLANES), _pick(N, bn, LANES), _pick(K, bk, LANES)
    assert m0 % bm == 0
    if b_slots:
        bn, bk = (_pick(c, bn, LANES), bk) if form == 'nn' else (bn, _pick(c, bk, LANES))
    if out_slots:
        oc = N // 4
        bn = _pick(oc, bn, LANES)
    nk = K // bk
    a_spec = pl.BlockSpec((bk, bm), lambda i, j, k: (k, m0 // bm + i)) if form == 'tn' else pl.BlockSpec((bm, bk), lambda i, j, k: (i, k))
    if b_slots and form == 'nn':
        per = c // bn
        b_spec = pl.BlockSpec((None, bk, bn), lambda i, j, k: (j // per, k, j % per))
    elif b_slots:
        per = c // bk
        b_spec = pl.BlockSpec((None, bn, bk), lambda i, j, k: (k // per, j, k % per))
    elif form == 'nt':
        b_spec = pl.BlockSpec((bn, bk), lambda i, j, k: (j, k))
    else:
        b_spec = pl.BlockSpec((bk, bn), lambda i, j, k: (k, j))
    o_spec = pl.BlockSpec((bm, bn), lambda i, j, k: (i, j))
    if out_slots:
        oper = oc // bn
        out_spec = pl.BlockSpec((None, bm, bn), lambda i, j, k: (j // oper, i, j % oper))
        out_shape = [jax.ShapeDtypeStruct((4, M, oc), d) for d in out_dtypes]
    else:
        out_spec = o_spec
        out_shape = [jax.ShapeDtypeStruct((M, N), d) for d in out_dtypes]
    ne, no = len(extras), len(out_dtypes)

    def finish(r, extra_refs, out_refs):
        outs = (r,) if epilogue is None else epilogue(r, *[e[...] for e in extra_refs])
        for ref, o in zip(out_refs, outs):
            ref[...] = o.astype(ref.dtype)

    def body_one(a_ref, b_ref, *rest):
        finish(_dot(a_ref[...].astype(BF16), b_ref[...].astype(BF16), form), rest[:ne], rest[ne:ne + no])

    def body_acc(a_ref, b_ref, *rest):
        extra_refs, out_refs, acc = rest[:ne], rest[ne:ne + no], rest[ne + no]
        k = pl.program_id(2)
        part = _dot(a_ref[...].astype(BF16), b_ref[...].astype(BF16), form)

        @pl.when(k == 0)
        def _():
            acc[...] = part

        @pl.when((k > 0) & (k < nk - 1))
        def _():
            acc[...] += part

        @pl.when(k == nk - 1)
        def _():
            finish(acc[...] + part, extra_refs, out_refs)

    row_spec = pl.BlockSpec((1, bn), lambda i, j, k: (0, j))
    extra_specs = [o_spec if e.shape[0] == M else row_spec for e in extras]
    return _call(body_one if nk == 1 else body_acc, name, (M // bm, N // bn, nk), [a_spec, b_spec] + extra_specs,
                 [out_spec] * no, out_shape, [] if nk == 1 else [pltpu.VMEM((bm, bn), F32)],
                 ("parallel", "parallel", "arbitrary"), (a, b, *extras), stage)


def _rowwise(fn, rows, consts, row_outs, acc_outs, name, bt=256, stage=None):
    T = rows[0].shape[0]
    bt = _pick(T, bt, SUBLANES)
    nr, nc, no, na = len(rows), len(consts), len(row_outs), len(acc_outs)

    def body(*refs):
        r_in, c_in = refs[:nr], refs[nr:nr + nc]
        r_out, a_out = refs[nr + nc:nr + nc + no], refs[nr + nc + no:]
        ro, ao = fn([r[...] for r in r_in], [c[...] for c in c_in])
        for ref, o in zip(r_out, ro):
            ref[...] = o.astype(ref.dtype)
        if na:
            @pl.when(pl.program_id(0) == 0)
            def _():
                for ref in a_out:
                    ref[...] = jnp.zeros_like(ref)
            for ref, o in zip(a_out, ao):
                ref[...] += o

    whole = lambda shp: pl.BlockSpec(shp, lambda i: (0,) * len(shp))
    return _call(
        body, name, (T // bt,),
        [pl.BlockSpec((bt, r.shape[1]), lambda i: (i, 0)) for r in rows] + [whole(c.shape) for c in consts],
        [pl.BlockSpec((bt, w), lambda i: (i, 0)) for w, _ in row_outs] + [whole(s) for s in acc_outs],
        [jax.ShapeDtypeStruct((T, w), d) for w, d in row_outs] + [jax.ShapeDtypeStruct(s, F32) for s in acc_outs],
        [], ("arbitrary",), (*rows, *consts), stage)


def _rmsnorm_fwd(x, g, name, stage=None):
    return _rowwise(lambda r, c: ([_rms(r[0], c[0])], []), [x], [g], [(x.shape[1], BF16)], [], name, stage=stage)


def _rmsnorm_bwd_add(x, g, dh, dres, name):
    D = x.shape[1]

    def fn(r, c):
        _, vjp = jax.vjp(_rms, r[0], c[0])
        dx, dg = vjp(r[1])
        dx = dx + r[2]
        return [dx, dx], [dg]
    return _rowwise(fn, [x, dh, dres], [g], [(D, F32), (D, BF16)], [(1, D)], name)


def _loss_fwd_bwd(x3, g, target, gp, pp, name):
    D = x3.shape[1]

    def fn(r, c):
        def row_loss(x, gain):
            err = _rms(x, gain) - r[1]
            return 0.5 * jnp.mean(err * err, axis=-1, keepdims=True)
        lrow, vjp = jax.vjp(row_loss, r[0], c[0])
        dx, dg = vjp(jnp.ones_like(lrow))
        tile = jnp.broadcast_to(jnp.sum(lrow, axis=0, keepdims=True), (SUBLANES, LANES))
        s = _sigmoid(r[2])
        return [dx, dx * r[3] * s * (1.0 - s), dx * s], [tile, dg]
    return _rowwise(fn, [x3, target, gp, pp], [g], [(D, F32), (D, BF16), (D, BF16)], [(SUBLANES, LANES), (1, D)], name)


def _ple_bwd(dx3, gp, pp, name):
    D = dx3.shape[1]

    def fn(r, c):
        s = _sigmoid(r[1])
        return [r[0] * r[2] * s * (1.0 - s), r[0] * s], []
    return _rowwise(fn, [dx3, gp, pp], [], [(D, BF16), (D, BF16)], [], name)


def _adamw_math(w, g, m, v):
    nm = ADAM_B1 * m + (1.0 - ADAM_B1) * g
    nv = ADAM_B2 * v + (1.0 - ADAM_B2) * (g * g)
    m_hat = nm / (1.0 - ADAM_B1 ** ADAM_STEP)
    v_hat = nv / (1.0 - ADAM_B2 ** ADAM_STEP)
    return -ADAM_LR * (m_hat / (jnp.sqrt(v_hat) + ADAM_EPS) + ADAM_WD * w), nm, nv


def _adamw(w, g, m, v, name, with_grad=False):
    R, C = w.shape[0], w.shape[-1]
    lanes = -(-C // LANES) * LANES
    if w.ndim == 2:
        bt = _pick(R, max(SUBLANES, (1 << 18) // lanes // SUBLANES * SUBLANES), SUBLANES)
        spec = pl.BlockSpec((bt, C), lambda i: (i, 0))
    else:
        bt = _pick(R, max(1, (1 << 18) // lanes), 1)
        spec = pl.BlockSpec((bt, 1, C), lambda i: (i, 0, 0))

    def body(w_ref, g_ref, m_ref, v_ref, d_ref, nm_ref, nv_ref, *g_out):
        d_ref[...], nm_ref[...], nv_ref[...] = _adamw_math(w_ref[...], g_ref[...], m_ref[...], v_ref[...])
        for ref in g_out:
            ref[...] = g_ref[...]

    n_out = 4 if with_grad else 3
    return pl.pallas_call(
        body, name=name, grid=(R // bt,), in_specs=[spec] * 4, out_specs=[spec] * n_out,
        out_shape=[jax.ShapeDtypeStruct(w.shape, F32)] * n_out, compiler_params=_params(("parallel",)),
    )(w, g, m, v)


def _adamw_small(ws, gs, ms, vs):
    n = len(ws)

    def body(*refs):
        for i in range(n):
            d, nm, nv = _adamw_math(refs[i][...], refs[n + i][...], refs[2 * n + i][...], refs[3 * n + i][...])
            refs[4 * n + i][...], refs[5 * n + i][...], refs[6 * n + i][...] = d, nm, nv

    VMEM = pl.BlockSpec(memory_space=pltpu.VMEM)
    shapes = [jax.ShapeDtypeStruct(w.shape, F32) for w in ws]
    outs = pl.pallas_call(body, name="adamw_small", in_specs=[VMEM] * (4 * n), out_specs=[VMEM] * (3 * n),
                          out_shape=shapes * 3)(*ws, *gs, *ms, *vs)
    return outs[:n], outs[n:2 * n], outs[2 * n:]


def _gla_fwd(z_big, z_small, w2h, gbh, Bl, S, D, stage=None):
    NC, dk, dv, HB = S // CHUNK, D // (2 * GLA_HEADS), D // GLA_HEADS, GLA_HEADS_PER_STEP
    HG = GLA_HEADS // HB
    chains = [(hh, bb) for hh in range(HB) for bb in range(Bl)]
    G = len(chains)
    fn = functools.partial(_gla_chunk, scale=dk ** -0.5)

    def body(q, k, v, z, w2, gb, o_ref, stall_ref, st):
        n, g = pl.program_id(0), pl.program_id(1)

        @pl.when(n == 0)
        def _():
            st[g] = jnp.zeros((G, dv, dk), F32)
        s0 = st[g]
        stall_ref[...] = s0.reshape(HB, Bl, dv, dk)
        qk = lambda r: jnp.stack([r[bb, :, hh * dk:(hh + 1) * dk] for hh, bb in chains])
        o, s_new = fn(qk(q), qk(k), jnp.stack([v[bb, :, hh * dv:(hh + 1) * dv] for hh, bb in chains]),
                      jnp.stack([z[bb] for _, bb in chains]), jnp.stack([w2[hh] for hh, _ in chains]),
                      jnp.stack([gb[hh] for hh, _ in chains]), s0)
        for i, (hh, bb) in enumerate(chains):
            o_ref[bb, :, hh * dv:(hh + 1) * dv] = o[i]
        st[g] = s_new

    return _call(
        body, "gla_fwd", (NC, HG),
        [pl.BlockSpec((Bl, CHUNK, HB * dk), lambda n, g: (0, n, g)),
         pl.BlockSpec((Bl, CHUNK, HB * dk), lambda n, g: (0, n, HG + g)),
         pl.BlockSpec((Bl, CHUNK, HB * dv), lambda n, g: (0, n, HG + g)),
         pl.BlockSpec((Bl, CHUNK, ZS), lambda n, g: (0, n, 0)),
         pl.BlockSpec((HB, ZS, dk), lambda n, g: (g, 0, 0)),
         pl.BlockSpec((HB, 1, dk), lambda n, g: (g, 0, 0))],
        [pl.BlockSpec((Bl, CHUNK, HB * dv), lambda n, g: (0, n, g)),
         pl.BlockSpec((HB, Bl, None, dv, dk), lambda n, g: (g, 0, n, 0, 0))],
        [jax.ShapeDtypeStruct((Bl, S, D), F32), jax.ShapeDtypeStruct((GLA_HEADS, Bl, NC, dv, dk), F32)],
        [pltpu.VMEM((HG, G, dv, dk), F32)], ("arbitrary", "arbitrary"), (z_big, z_big, z_big, z_small, w2h, gbh), stage)


def _gla_bwd(z_big, z_small, w2h, gbh, st_all, do, Bl, S, D):
    NC, dk, dv, HB = S // CHUNK, D // (2 * GLA_HEADS), D // GLA_HEADS, GLA_HEADS_PER_STEP
    HG = GLA_HEADS // HB
    chains = [(hh, bb) for hh in range(HB) for bb in range(Bl)]
    G = len(chains)
    fn = functools.partial(_gla_chunk, scale=dk ** -0.5)

    def body(q, k, v, z, w2, gb, st0, do_ref, dq_ref, dk_ref, dv_ref, dzs_ref, dw2_ref, dgb_ref, dst):
        n, g = pl.program_id(0), pl.program_id(1)

        @pl.when(n == 0)
        def _():
            dst[g] = jnp.zeros((G, dv, dk), F32)

        @pl.when((n == 0) & (g == 0))
        def _():
            dw2_ref[...] = jnp.zeros_like(dw2_ref)
            dgb_ref[...] = jnp.zeros_like(dgb_ref)

        qk = lambda r: jnp.stack([r[bb, :, hh * dk:(hh + 1) * dk] for hh, bb in chains])
        vv = lambda r: jnp.stack([r[bb, :, hh * dv:(hh + 1) * dv] for hh, bb in chains])
        _, vjp = jax.vjp(fn, qk(q), qk(k), vv(v), jnp.stack([z[bb] for _, bb in chains]),
                         jnp.stack([w2[hh] for hh, _ in chains]), jnp.stack([gb[hh] for hh, _ in chains]),
                         st0[...].reshape(G, dv, dk))
        dq, dkk, dvv, dzs, dw2, dgb, dst0 = vjp((vv(do_ref), dst[g]))
        for i, (hh, bb) in enumerate(chains):
            dq_ref[bb, :, hh * dk:(hh + 1) * dk] = dq[i].astype(dq_ref.dtype)
            dk_ref[bb, :, hh * dk:(hh + 1) * dk] = dkk[i].astype(dk_ref.dtype)
            dv_ref[bb, :, hh * dv:(hh + 1) * dv] = dvv[i].astype(dv_ref.dtype)
            dw2_ref[g * HB + hh] += dw2[i]
            dgb_ref[g * HB + hh] += dgb[i]
        for bb in range(Bl):
            tot = sum(dzs[i] for i, (_, b2) in enumerate(chains) if b2 == bb)

            @pl.when(g == 0)
            def _():
                dzs_ref[bb] = tot

            @pl.when(g > 0)
            def _():
                dzs_ref[bb] += tot
        dst[g] = dst0

    rn = lambda n: NC - 1 - n
    return pl.pallas_call(
        body, name="gla_bwd", grid=(NC, HG),
        in_specs=[pl.BlockSpec((Bl, CHUNK, HB * dk), lambda n, g: (0, rn(n), g)),
                  pl.BlockSpec((Bl, CHUNK, HB * dk), lambda n, g: (0, rn(n), HG + g)),
                  pl.BlockSpec((Bl, CHUNK, HB * dv), lambda n, g: (0, rn(n), HG + g)),
                  pl.BlockSpec((Bl, CHUNK, ZS), lambda n, g: (0, rn(n), 0)),
                  pl.BlockSpec((HB, ZS, dk), lambda n, g: (g, 0, 0)),
                  pl.BlockSpec((HB, 1, dk), lambda n, g: (g, 0, 0)),
                  pl.BlockSpec((HB, Bl, None, dv, dk), lambda n, g: (g, 0, rn(n), 0, 0)),
                  pl.BlockSpec((Bl, CHUNK, HB * dv), lambda n, g: (0, rn(n), g))],
        out_specs=[pl.BlockSpec((Bl, CHUNK, HB * dk), lambda n, g: (0, rn(n), g)),
                   pl.BlockSpec((Bl, CHUNK, HB * dk), lambda n, g: (0, rn(n), g)),
                   pl.BlockSpec((Bl, CHUNK, HB * dv), lambda n, g: (0, rn(n), g)),
                   pl.BlockSpec((Bl, CHUNK, ZS), lambda n, g: (0, rn(n), 0)),
                   pl.BlockSpec((GLA_HEADS, ZS, dk), lambda n, g: (0, 0, 0)),
                   pl.BlockSpec((GLA_HEADS, 1, dk), lambda n, g: (0, 0, 0))],
        out_shape=[jax.ShapeDtypeStruct((Bl, S, D // 2), BF16), jax.ShapeDtypeStruct((Bl, S, D // 2), BF16),
                   jax.ShapeDtypeStruct((Bl, S, D), BF16), jax.ShapeDtypeStruct((Bl, S, ZS), F32),
                   jax.ShapeDtypeStruct((GLA_HEADS, ZS, dk), F32), jax.ShapeDtypeStruct((GLA_HEADS, 1, dk), F32)],
        scratch_shapes=[pltpu.VMEM((HG, G, dv, dk), F32)],
        compiler_params=_params(("arbitrary", "arbitrary")),
    )(z_big, z_big, z_big, z_small, w2h, gbh, st_all, do)


def _conv_fwd(z_big, conv_w, grp, Bl, S, D):
    d = D // DN_HEADS
    l2, scale = grp < 2, (d ** -0.5 if grp == 0 else 1.0)
    x_blk0 = (3 * D + grp * D) // d

    def body(x_ref, w_ref, o_ref):
        wrows = [w_ref[j:j + 1, :] for j in range(DN_CONV)]
        o_ref[...] = _conv_act(x_ref[...], wrows, l2=l2, scale=scale)

    return pl.pallas_call(
        body, name=f"conv_fwd{grp}", grid=(Bl, DN_HEADS),
        in_specs=[pl.BlockSpec((S, d), lambda b, j: (b, x_blk0 + j)),
                  pl.BlockSpec((DN_CONV, d), lambda b, j: (0, grp * DN_HEADS + j))],
        out_specs=pl.BlockSpec((S, d), lambda b, j: (b, j)),
        out_shape=jax.ShapeDtypeStruct((Bl * S, D), F32),
        compiler_params=_params(("parallel", "parallel")),
    )(z_big, conv_w)


def _conv_bwd(z_big, conv_w, dact, grp, Bl, S, D):
    d = D // DN_HEADS
    l2, scale = grp < 2, (d ** -0.5 if grp == 0 else 1.0)
    x_blk0 = (3 * D + grp * D) // d

    def body(x_ref, w_ref, g_ref, dx_ref, dw_ref):
        @pl.when(pl.program_id(1) == 0)
        def _():
            dw_ref[...] = jnp.zeros_like(dw_ref)
        wrows = [w_ref[j:j + 1, :] for j in range(DN_CONV)]
        _, vjp = jax.vjp(lambda x, wr: _conv_act(x, wr, l2=l2, scale=scale), x_ref[...], wrows)
        dx, dwr = vjp(g_ref[...])
        dx_ref[...] = dx.astype(dx_ref.dtype)
        for j in range(DN_CONV):
            dw_ref[j:j + 1, :] += dwr[j]

    return pl.pallas_call(
        body, name=f"conv_bwd{grp}", grid=(DN_HEADS, Bl),
        in_specs=[pl.BlockSpec((S, d), lambda j, b: (b, x_blk0 + j)),
                  pl.BlockSpec((DN_CONV, d), lambda j, b: (0, grp * DN_HEADS + j)),
                  pl.BlockSpec((S, d), lambda j, b: (b, j))],
        out_specs=[pl.BlockSpec((S, d), lambda j, b: (b, j)), pl.BlockSpec((DN_CONV, d), lambda j, b: (0, j))],
        out_shape=[jax.ShapeDtypeStruct((Bl * S, D), BF16), jax.ShapeDtypeStruct((DN_CONV, D), F32)],
        compiler_params=_params(("arbitrary", "arbitrary")),
    )(z_big, conv_w, dact)


def _lane_column(zb, lane, width):
    pick = lax.broadcasted_iota(jnp.int32, zb.shape, 1) == lane
    return jnp.broadcast_to(jnp.sum(jnp.where(pick, zb, 0.0), axis=-1, keepdims=True), (zb.shape[0], width))


def _dn_fwd(qa, ka, va, z_small, alog, dtb, Bl, S, D, stage=None):
    NC, d, HB = S // CHUNK, D // DN_HEADS, DN_HEADS_PER_STEP
    HG = DN_HEADS // HB
    chains = [(hh, bb) for hh in range(HB) for bb in range(Bl)]
    G = len(chains)

    def body(q, k, v, z, al, dt, o_ref, sall_ref, st):
        n, g = pl.program_id(0), pl.program_id(1)

        @pl.when(n == 0)
        def _():
            st[g] = jnp.zeros((G, d, d), F32)
        tok_in = lambda r: jnp.stack([r[bb, :, hh * d:(hh + 1) * d] for hh, bb in chains])
        head_in = lambda r: jnp.stack([r[hh] for hh, _ in chains])
        gate_in = lambda lane0: jnp.stack([_lane_column(z[bb], lane0 + g * HB + hh, d) for hh, bb in chains])
        s0 = st[g]
        sall_ref[...] = s0.reshape(HB, Bl, d, d)
        o, s_new = _dn_chunk(tok_in(q), tok_in(k), tok_in(v), gate_in(A_LANE), gate_in(B_LANE), head_in(al), head_in(dt), s0)
        for i, (hh, bb) in enumerate(chains):
            o_ref[bb, :, hh * d:(hh + 1) * d] = o[i]
        st[g] = s_new

    tok = pl.BlockSpec((Bl, CHUNK, HB * d), lambda n, g: (0, n, g))
    per_head = pl.BlockSpec((HB, 1, d), lambda n, g: (g, 0, 0))
    return _call(
        body, "dn_fwd", (NC, HG),
        [tok, tok, tok, pl.BlockSpec((Bl, CHUNK, ZS), lambda n, g: (0, n, 0)), per_head, per_head],
        [tok, pl.BlockSpec((HB, Bl, None, d, d), lambda n, g: (g, 0, n, 0, 0))],
        [jax.ShapeDtypeStruct((Bl, S, D), F32), jax.ShapeDtypeStruct((DN_HEADS, Bl, NC, d, d), F32)],
        [pltpu.VMEM((HG, G, d, d), F32)], ("arbitrary", "arbitrary"), (qa, ka, va, z_small, alog, dtb), stage)


def _dn_bwd(qa, ka, va, z_small, alog, dtb, s_all, do, dzs_gla, Bl, S, D, stage=None):
    NC, d, HB = S // CHUNK, D // DN_HEADS, DN_HEADS_PER_STEP
    HG = DN_HEADS // HB
    chains = [(hh, bb) for hh in range(HB) for bb in range(Bl)]
    G = len(chains)

    def lanesum(t):
        return jnp.sum(t, axis=-1, keepdims=True)

    def body(q, k, v, z, al, dt, s0_ref, do_ref, dzg_ref, dq_ref, dk_ref, dv_ref, dzs_ref, dal_ref, ddt_ref, dst):
        n, g = pl.program_id(0), pl.program_id(1)

        @pl.when(n == 0)
        def _():
            dst[g] = jnp.zeros((G, d, d), F32)

        @pl.when((n == 0) & (g == 0))
        def _():
            dal_ref[...] = jnp.zeros_like(dal_ref)
            ddt_ref[...] = jnp.zeros_like(ddt_ref)

        tok_in = lambda r: jnp.stack([r[bb, :, hh * d:(hh + 1) * d] for hh, bb in chains])
        head_in = lambda r: jnp.stack([r[hh] for hh, _ in chains])
        gate_in = lambda lane0: jnp.stack([_lane_column(z[bb], lane0 + g * HB + hh, d) for hh, bb in chains])
        _, vjp = jax.vjp(_dn_chunk, tok_in(q), tok_in(k), tok_in(v), gate_in(A_LANE), gate_in(B_LANE), head_in(al),
                         head_in(dt), s0_ref[...].reshape(G, d, d))
        dq, dkk, dvv, da, db, dal, ddt, ds0 = vjp((tok_in(do_ref), dst[g]))
        da, db = lanesum(da), lanesum(db)
        dal = jnp.broadcast_to(lanesum(dal), (G, 1, d))
        ddt = jnp.broadcast_to(lanesum(ddt), (G, 1, d))
        lane = lax.broadcasted_iota(jnp.int32, (CHUNK, ZS), 1)
        for bb in range(Bl):
            part = jnp.zeros((CHUNK, ZS), F32)
            for i, (hh, b2) in enumerate(chains):
                if b2 == bb:
                    h = g * HB + hh
                    part = part + jnp.where(lane == A_LANE + h, da[i], 0.0) + jnp.where(lane == B_LANE + h, db[i], 0.0)

            @pl.when(g == 0)
            def _():
                dzs_ref[bb] = jnp.where(lane < LOWRANK, dzg_ref[bb], 0.0) + part

            @pl.when(g > 0)
            def _():
                dzs_ref[bb] += part
        for i, (hh, bb) in enumerate(chains):
            cols = slice(hh * d, (hh + 1) * d)
            dq_ref[bb, :, cols] = dq[i]
            dk_ref[bb, :, cols] = dkk[i]
            dv_ref[bb, :, cols] = dvv[i]
            dal_ref[g * HB + hh] += dal[i]
            ddt_ref[g * HB + hh] += ddt[i]
        dst[g] = ds0

    rn = lambda n: NC - 1 - n
    tok = pl.BlockSpec((Bl, CHUNK, HB * d), lambda n, g: (0, rn(n), g))
    zsb = pl.BlockSpec((Bl, CHUNK, ZS), lambda n, g: (0, rn(n), 0))
    per_head = pl.BlockSpec((HB, 1, d), lambda n, g: (g, 0, 0))
    all_heads = pl.BlockSpec((DN_HEADS, 1, d), lambda n, g: (0, 0, 0))
    tok_shape = jax.ShapeDtypeStruct((Bl, S, D), F32)
    head_shape = jax.ShapeDtypeStruct((DN_HEADS, 1, d), F32)
    return _call(
        body, "dn_bwd", (NC, HG),
        [tok, tok, tok, zsb, per_head, per_head,
         pl.BlockSpec((HB, Bl, None, d, d), lambda n, g: (g, 0, rn(n), 0, 0)), tok, zsb],
        [tok, tok, tok, zsb, all_heads, all_heads],
        [tok_shape, tok_shape, tok_shape, jax.ShapeDtypeStruct((Bl, S, ZS), F32), head_shape, head_shape],
        [pltpu.VMEM((HG, G, d, d), F32)], ("arbitrary", "arbitrary"),
        (qa, ka, va, z_small, alog, dtb, s_all, do, dzs_gla), stage)


def _merge_specs(D, bt):
    dv, w = D // GLA_HEADS, D // DN_HEADS
    col = lambda off: pl.BlockSpec((bt, dv), lambda i, h: (i, off // dv + h))
    return dv, w, col


def _merge_load(refs, nsub, w):
    return [[r[:, s * w:(s + 1) * w] for s in range(nsub)] for r in refs]


def _merge_fwd(o_gla, o_dn, z_big, gla_norm, dn_norm, D, bt=256, stage=None):
    T = o_gla.shape[0]
    bt = _pick(T, bt, SUBLANES)
    dv, w, col = _merge_specs(D, bt)
    nsub = dv // w

    def body(og, gg, od, dz, ga, gb, gn, dn, out):
        ogl, ggl, odl, dzl, gal, gbl = _merge_load([og, gg, od, dz, ga, gb], nsub, w)
        gnl = [gn[:, s * w:(s + 1) * w] for s in range(nsub)]
        outs = _merge_math(ogl, ggl, odl, dzl, gal, gbl, gnl, dn[...])
        for s in range(nsub):
            out[:, s * w:(s + 1) * w] = outs[s].astype(out.dtype)

    return _call(
        body, "merge_fwd", (T // bt, GLA_HEADS),
        [col(0), col(2 * D), col(0), col(6 * D), col(7 * D), col(8 * D),
         pl.BlockSpec((1, dv), lambda i, h: (0, 0)), pl.BlockSpec((1, w), lambda i, h: (0, 0))],
        [col(0)], [jax.ShapeDtypeStruct((T, D), BF16)], [], ("parallel", "parallel"),
        (o_gla, z_big, o_dn, z_big, z_big, z_big, gla_norm, dn_norm), stage)


def _merge_bwd(o_gla, o_dn, z_big, gla_norm, dn_norm, dmix, D, bt=256):
    T = o_gla.shape[0]
    bt = _pick(T, bt, SUBLANES)
    dv, w, col = _merge_specs(D, bt)
    nsub = dv // w

    def body(og, gg, od, dz, ga, gb, gn, dn, dm, dog, dgg, dod, ddz, dga, dgb, dgn, ddn):
        @pl.when((pl.program_id(0) == 0) & (pl.program_id(1) == 0))
        def _():
            dgn[...] = jnp.zeros_like(dgn)
            ddn[...] = jnp.zeros_like(ddn)

        ogl, ggl, odl, dzl, gal, gbl, dml = _merge_load([og, gg, od, dz, ga, gb, dm], nsub, w)
        gnl = [gn[:, s * w:(s + 1) * w] for s in range(nsub)]
        _, vjp = jax.vjp(_merge_math, ogl, ggl, odl, dzl, gal, gbl, gnl, dn[...])
        g_og, g_gg, g_od, g_dz, g_ga, g_gb, g_gn, g_dn = vjp(dml)
        for s in range(nsub):
            sl = slice(s * w, (s + 1) * w)
            dog[:, sl] = g_og[s]
            dgg[:, sl] = g_gg[s].astype(dgg.dtype)
            dod[:, sl] = g_od[s]
            ddz[:, sl] = g_dz[s].astype(ddz.dtype)
            dga[:, sl] = g_ga[s].astype(dga.dtype)
            dgb[:, sl] = g_gb[s].astype(dgb.dtype)
            dgn[:, sl] += g_gn[s]
        ddn[...] += g_dn

    f32s, bf16s = jax.ShapeDtypeStruct((T, D), F32), jax.ShapeDtypeStruct((T, D), BF16)
    return pl.pallas_call(
        body, name="merge_bwd", grid=(T // bt, GLA_HEADS),
        in_specs=[col(0), col(2 * D), col(0), col(6 * D), col(7 * D), col(8 * D),
                  pl.BlockSpec((1, dv), lambda i, h: (0, 0)), pl.BlockSpec((1, w), lambda i, h: (0, 0)), col(0)],
        out_specs=[col(0)] * 6 + [pl.BlockSpec((1, dv), lambda i, h: (0, 0)), pl.BlockSpec((1, w), lambda i, h: (0, 0))],
        out_shape=[f32s, bf16s, f32s, bf16s, bf16s, bf16s,
                   jax.ShapeDtypeStruct((1, dv), F32), jax.ShapeDtypeStruct((1, w), F32)],
        compiler_params=_params(("arbitrary", "arbitrary")),
    )(o_gla, z_big, o_dn, z_big, z_big, z_big, gla_norm, dn_norm, dmix)


def _place():
    return lax.axis_index("x"), lax.axis_index("y"), lax.axis_index("c")


def _other_chips(x, y):
    return [(1 - x, y), (x, 1 - y), (1 - x, 1 - y)]


def _rcopy(src, dst, send_sem, recv_sem, dev):
    return pltpu.make_async_remote_copy(src_ref=src, dst_ref=dst, send_sem=send_sem, recv_sem=recv_sem,
                                        device_id=dev, device_id_type=MESH)


ANY = pl.BlockSpec(memory_space=pl.ANY)


ROWS, COLS = 'rows', 'cols'


def _half(ref, hc, by, lead=()):
    shape = ref.shape[len(lead):]
    if by == ROWS:
        rh = shape[0] // 2
        idx = (pl.ds(pl.multiple_of(hc * rh, 16), rh),) + (slice(None),) * (len(shape) - 1)
    else:
        ch = shape[-1] // 2
        idx = (slice(None),) * (len(shape) - 1) + (pl.ds(pl.multiple_of(hc * ch, LANES), ch),)
    return ref.at[(*lead, *idx)]


def _half_shape(shape, by):
    return (shape[0] // 2,) + tuple(shape[1:]) if by == ROWS else tuple(shape[:-1]) + (shape[-1] // 2,)


def _gather_ici(shards, by):
    nw = len(shards)

    def copies(srcs, outs, send_sems, recv_sems):
        x, y, c = _place()
        return [_rcopy(_half(srcs[w], c, by[w]), _half(outs[w], c, by[w], (2 * x + y,)),
                       send_sems.at[3 * w + k], recv_sems.at[3 * w + k], (px, py, c))
                for w in range(nw) for k, (px, py) in enumerate(_other_chips(x, y))]

    return _Stage(shards, [jax.ShapeDtypeStruct((4,) + s.shape, s.dtype) for s in shards], 3 * nw, copies)


def _gather_neighbours(shards, by):
    nw = len(shards)

    def copies(srcs, outs, send_sems, recv_sems):
        x, y, c = _place()
        return [_rcopy(_half(srcs[w], c, by[w]), _half(outs[w], c, by[w], (2 * x + y,)),
                       send_sems.at[2 * w + k], recv_sems.at[2 * w + k], (px, py, c))
                for w in range(nw) for k, (px, py) in enumerate(_other_chips(x, y)[:2])]

    return _Stage(shards, [jax.ShapeDtypeStruct((4,) + s.shape, s.dtype) for s in shards], 2 * nw, copies)


def _gather_relay(gathered):
    nw = len(gathered)

    def copies(srcs, outs, send_sems, recv_sems):
        x, y, c = _place()
        cps = []
        for w in range(nw):
            _, n, cols = gathered[w].shape
            cut, ch = n // 2 // 16 * 16, cols // 2
            lanes = pl.ds(pl.multiple_of(c * ch, LANES), ch)
            via = [(2 * (1 - x) + y, pl.ds(0, cut), (x, 1 - y, c)),
                   (2 * x + (1 - y), pl.ds(cut, n - cut), (1 - x, y, c))]
            for k, (slot, rows, dev) in enumerate(via):
                cps.append(_rcopy(srcs[w].at[slot, rows, lanes], outs[w].at[slot, rows, lanes],
                                  send_sems.at[2 * w + k], recv_sems.at[2 * w + k], dev))
        return cps

    return _Stage(gathered, [jax.ShapeDtypeStruct(g.shape, g.dtype) for g in gathered], 2 * nw, copies,
                  aliases={w: w for w in range(nw)})


def _gather_pass(gathered, by):
    nw = len(gathered)

    def copies(srcs, outs, send_sems, recv_sems):
        x, y, c = _place()
        cps = []
        for w in range(nw):
            for k, (px, py) in enumerate(_other_chips(x, y)):
                slot = (2 * px + py,)
                cps.append(_rcopy(_half(srcs[w], c, by[w], slot), _half(outs[w], c, by[w], slot),
                                  send_sems.at[3 * w + k], recv_sems.at[3 * w + k], (x, y, 1 - c)))
        return cps

    return _Stage(gathered, [jax.ShapeDtypeStruct(g.shape, g.dtype) for g in gathered], 3 * nw, copies,
                  aliases={w: w for w in range(nw)})


def _pair_exchange(ps, by):
    nw = len(ps)

    def copies(srcs, outs, send_sems, recv_sems):
        x, y, c = _place()
        return [_rcopy(_half(srcs[w], 1 - c, by[w], (slice(None),) * (ps[w].ndim - 2)), outs[w], send_sems.at[w], recv_sems.at[w],
                       (x, y, 1 - c)) for w in range(nw)]

    return _Stage(ps, [jax.ShapeDtypeStruct(p.shape[:-2] + _half_shape(p.shape[-2:], b), p.dtype) for p, b in zip(ps, by)],
                  nw, copies)


def _sum_blocks(half_shape, by):
    rh, ch = half_shape
    lanes = -(-ch // LANES) * LANES
    bt = _pick(rh, max(16, (3 << 18) // lanes // 16 * 16), 16)
    if by == ROWS:
        nb = rh // bt
        return (bt, ch), nb, (lambda i: (i, 0)), (lambda i, c: (c * nb + i, 0))
    if rh % bt == 0 and bt % 16 == 0:
        return (bt, ch), rh // bt, (lambda i: (i, 0)), (lambda i, c: (i, c))
    bc = _pick(ch, max(LANES, (5 << 18) // rh // LANES * LANES), LANES)
    nb = ch // bc
    return (rh, bc), nb, (lambda i: (0, i)), (lambda i, c: (0, c * nb + i))


def _pair_sum(p, got, c_idx, name, by=ROWS):
    lead, hs = got.shape[0], got.shape[1:]
    blk, nb, pos, pos_whole = _sum_blocks(hs, by)

    def body(c_ref, a, b, of, ob):
        s = a[...] + b[...]
        of[...] = s
        ob[...] = s.astype(BF16)

    spec = pl.BlockSpec((None,) + blk, lambda j, i, c_ref: (j,) + pos(i))
    return pl.pallas_call(
        body, name=name,
        grid_spec=pltpu.PrefetchScalarGridSpec(
            num_scalar_prefetch=1, grid=(lead, nb),
            in_specs=[pl.BlockSpec((None,) + blk, lambda j, i, c_ref: (j,) + pos_whole(i, c_ref[0])), spec],
            out_specs=[spec, spec]),
        out_shape=[jax.ShapeDtypeStruct((lead,) + hs, F32), jax.ShapeDtypeStruct((lead,) + hs, BF16)],
        compiler_params=_params(("parallel", "parallel")),
    )(c_idx, p, got)


def _chip_scatter(qbs):
    nw = len(qbs)

    def copies(srcs, outs, send_sems, recv_sems):
        x, y, c = _place()
        return [_rcopy(srcs[w].at[2 * px + py], outs[w].at[k], send_sems.at[3 * w + k], recv_sems.at[3 * w + k], (px, py, c))
                for w in range(nw) for k, (px, py) in enumerate(_other_chips(x, y))]

    return _Stage(qbs, [jax.ShapeDtypeStruct((3,) + q.shape[1:], q.dtype) for q in qbs], 3 * nw, copies)


def _final_sum(qf, got, me_idx, name, by=ROWS):
    hs = qf.shape[1:]
    blk, nb, pos, _ = _sum_blocks(hs, by)

    def body(me_ref, a, b, o):
        o[...] = ((a[...] + b[0].astype(F32)) + b[1].astype(F32)) + b[2].astype(F32)

    return pl.pallas_call(
        body, name=name,
        grid_spec=pltpu.PrefetchScalarGridSpec(
            num_scalar_prefetch=1, grid=(nb,),
            in_specs=[pl.BlockSpec((None,) + blk, lambda i, me_ref: (me_ref[0],) + pos(i)),
                      pl.BlockSpec((3,) + blk, lambda i, me_ref: (0,) + pos(i))],
            out_specs=pl.BlockSpec(blk, lambda i, me_ref: pos(i))),
        out_shape=jax.ShapeDtypeStruct(hs, F32),
        compiler_params=_params(("parallel",)),
    )(me_idx, qf, got)


def _pair_allgather(halves, by):
    nw = len(halves)
    whole = [(2 * h.shape[0], h.shape[1]) if b == ROWS else h.shape for h, b in zip(halves, by)]

    def copies(srcs, outs, send_sems, recv_sems):
        x, y, c = _place()
        there = lambda w: _half(outs[w], c, ROWS) if by[w] == ROWS else outs[w]
        return [_rcopy(srcs[w], there(w), send_sems.at[w], recv_sems.at[w], (x, y, 1 - c)) for w in range(nw)]

    return _Stage(halves, [jax.ShapeDtypeStruct(s, h.dtype) for s, h in zip(whole, halves)], nw, copies)


class _SemaphoreWindow:
    def __init__(self, ref, off):
        self.ref, self.off = ref, off

    @property
    def at(self):
        return self

    def __getitem__(self, i):
        return self.ref.at[self.off + i]


def _both(a, b):
    na, ma = len(a.inputs), len(a.out_shapes)

    def copies(ins, outs, send_sems, recv_sems):
        return (a.copies(ins[:na], outs[:ma], send_sems, recv_sems) +
                b.copies(ins[na:], outs[ma:], _SemaphoreWindow(send_sems, a.n_sems), _SemaphoreWindow(recv_sems, a.n_sems)))

    return _Stage(a.inputs + b.inputs, a.out_shapes + b.out_shapes, a.n_sems + b.n_sems, copies,
                  aliases={**a.aliases, **{na + i: ma + o for i, o in b.aliases.items()}})


def _small_exchange(items, out_shapes, finish, name):
    n = len(items)
    offs, rows = [], 0
    for it in items:
        offs.append(rows)
        rows += it.shape[0]
    rows = -(-rows // SUBLANES) * SUBLANES
    width = -(-max(it.shape[1] for it in items) // LANES) * LANES
    VMEM = pl.BlockSpec(memory_space=pltpu.VMEM)

    def body(*refs):
        ins, outs = refs[:n], refs[n:n + len(out_shapes)]
        buf, send_sems, recv_sems = refs[n + len(out_shapes):]
        x, y, c = _place()
        me = 4 * x + 2 * y + c
        flip = lambda v, f: (1 - v) if f else v
        peers = [(flip(x, r >> 2 & 1), flip(y, r >> 1 & 1), flip(c, r & 1)) for r in range(1, 8)]
        buf[me] = jnp.zeros((rows, width), F32)
        for it, off, ref in zip(items, offs, ins):
            buf[me, off:off + it.shape[0], 0:it.shape[1]] = ref[...]
        cps = [_rcopy(buf.at[me], buf.at[me], send_sems.at[k], recv_sems.at[k], dev) for k, dev in enumerate(peers)]
        for cp in cps:
            cp.start()
        for k, (px, py, pc) in enumerate(peers):
            slot = buf.at[4 * px + 2 * py + pc]
            _rcopy(slot, slot, send_sems.at[k], recv_sems.at[k], (px, py, pc)).wait_recv()
        for cp in cps:
            cp.wait_send()
        finish(buf, offs, outs)

    return pl.pallas_call(
        body, name=name, in_specs=[VMEM] * n, out_specs=[VMEM] * len(out_shapes),
        out_shape=[jax.ShapeDtypeStruct(s, F32) for s in out_shapes],
        scratch_shapes=[pltpu.VMEM((8, rows, width), F32), pltpu.SemaphoreType.DMA((7,)), pltpu.SemaphoreType.DMA((7,))],
        compiler_params=pltpu.CompilerParams(vmem_limit_bytes=VMEM_LIMIT_BYTES),
    )(*items)


def _allreduce_small(items, name):
    def finish(buf, offs, outs):
        for it, off, out in zip(items, offs, outs):
            region = lambda d: buf[d, off:off + it.shape[0], 0:it.shape[1]]
            s = region(0)
            for d in range(1, 8):
                s = s + region(d)
            out[...] = s
    return _small_exchange(items, [it.shape for it in items], finish, name)


def _allgather_small_shards(items, name):
    def finish(buf, offs, outs):
        for it, off, out in zip(items, offs, outs):
            r, c = it.shape
            for j in range(4):
                out[:, j * c:(j + 1) * c] = buf[2 * j, off:off + r, 0:c]
    return _small_exchange(items, [(it.shape[0], 4 * it.shape[1]) for it in items], finish, name)


def _split_w_in(wt, D):
    pad = jnp.zeros((ZS - 3 * LOWRANK, wt.shape[1]), wt.dtype)
    big = jnp.concatenate([wt[:3 * D], wt[3 * D + 16:6 * D + 16], wt[6 * D + 16:7 * D + 16], wt[7 * D + 48:]], axis=0)
    small = jnp.concatenate([wt[3 * D:3 * D + 16], wt[7 * D + 16:7 * D + 48], pad], axis=0)
    return big, small


def _join_w_in(ga, gb, gs, D):
    return jnp.concatenate([ga[:3 * D], gs[:16], ga[3 * D:], gb[:D], gs[16:48], gb[D:]], axis=0)


def kernel(x, p, g_mix, w_in, gla_w2, gla_b, gla_norm, dn_conv, dn_a_log, dn_dt_bias, dn_norm, w_out, g_mlp, w_up, w_down, g_ple, w_ple_gate, w_ple_proj, g_final, loss_target, m_g_mix, m_w_in, m_gla_w2, m_gla_b, m_gla_norm, m_dn_conv, m_dn_a_log, m_dn_dt_bias, m_dn_norm, m_w_out, m_g_mlp, m_w_up, m_w_down, m_g_ple, m_w_ple_gate, m_w_ple_proj, m_g_final, v_g_mix, v_w_in, v_gla_w2, v_gla_b, v_gla_norm, v_dn_conv, v_dn_a_log, v_dn_dt_bias, v_dn_norm, v_w_out, v_g_mlp, v_w_up, v_w_down, v_g_ple, v_w_ple_gate, v_w_ple_proj, v_g_final):
    wts = dict(zip(WEIGHTS, [g_mix, w_in, gla_w2, gla_b, gla_norm, dn_conv, dn_a_log, dn_dt_bias, dn_norm, w_out, g_mlp,
                             w_up, w_down, g_ple, w_ple_gate, w_ple_proj, g_final]))
    mom = dict(zip(WEIGHTS, [m_g_mix, m_w_in, m_gla_w2, m_gla_b, m_gla_norm, m_dn_conv, m_dn_a_log, m_dn_dt_bias, m_dn_norm,
                             m_w_out, m_g_mlp, m_w_up, m_w_down, m_g_ple, m_w_ple_gate, m_w_ple_proj, m_g_final]))
    var = dict(zip(WEIGHTS, [v_g_mix, v_w_in, v_gla_w2, v_gla_b, v_gla_norm, v_dn_conv, v_dn_a_log, v_dn_dt_bias, v_dn_norm,
                             v_w_out, v_g_mlp, v_w_up, v_w_down, v_g_ple, v_w_ple_gate, v_w_ple_proj, v_g_final]))
    Bl, S, D = x.shape
    T = Bl * S
    PLE = p.shape[-1]
    dn_d, gla_dk = D // DN_HEADS, D // (2 * GLA_HEADS)
    ix, iy, ic = _place()
    j_me = 2 * ix + iy
    as2d = lambda a: a.reshape(a.shape[-2], a.shape[-1]) if a.ndim > 1 else a.reshape(1, -1)
    c_idx, me_idx = ic.reshape(1).astype(jnp.int32), j_me.reshape(1).astype(jnp.int32)

    rows_first = lambda a: jnp.transpose(a, (2, 0, 1))
    cols_last = lambda a: jnp.transpose(a, (1, 2, 0))
    w_in_t, m_in_t, v_in_t = rows_first(w_in), rows_first(m_w_in), rows_first(v_w_in)
    n_in = w_in_t.shape[0]
    shard2d = {n: as2d(wts[n]) for n, _ in BIG[1:]}
    bf16_shards = [w_in_t.astype(BF16).reshape(n_in, D)] + [shard2d[n].astype(BF16) for n, _ in BIG[1:]]
    split = [COLS] + [ROWS] * (len(BIG) - 1)
    own_slot = lambda g, s: lax.dynamic_update_slice(g, s[None], (j_me, 0, 0))
    xt = x.reshape(T, D)
    (w_in_near,) = _run_stage(_gather_neighbours(bf16_shards[:1], split[:1]), "allgather_w_in_neighbours")
    (w_in_ici,) = _run_stage(_gather_relay([w_in_near]), "allgather_w_in_relay")
    h, w_in_all = _rmsnorm_fwd(xt, g_mix, "rms1_fwd", stage=_gather_pass([w_in_ici], split[:1]))
    w_in_slots = own_slot(w_in_all, bf16_shards[0])
    w_big, w_small = _split_w_in(w_in_slots.reshape(4 * n_in, D), D)

    w2_full, conv_full = _allgather_small_shards([as2d(gla_w2), as2d(dn_conv)], "allgather_small_weights")
    w2pad = jnp.pad(w2_full, ((0, ZS - LOWRANK), (0, 0)))
    w2h = jnp.swapaxes(w2pad.reshape(ZS, GLA_HEADS, gla_dk), 0, 1)
    gbh = gla_b.reshape(GLA_HEADS, 1, gla_dk)
    alog_w = jnp.broadcast_to(dn_a_log.reshape(DN_HEADS, 1, 1), (DN_HEADS, 1, dn_d))
    dtb_w = jnp.broadcast_to(dn_dt_bias.reshape(DN_HEADS, 1, 1), (DN_HEADS, 1, dn_d))

    tgt = loss_target.reshape(T, D)
    pt = p.reshape(T, PLE)
    seq = lambda t: t.reshape(Bl, S, t.shape[-1])
    tok = lambda t: t.reshape(T, t.shape[-1])
    first, second = [1, 2, 5], [3, 4]
    sh, sp = (lambda idx: [bf16_shards[i] for i in idx]), (lambda idx: [split[i] for i in idx])
    z_big, *first_ici = _matmul(h, w_big, 'nt', [F32], "proj_in", stage=_gather_ici(sh(first), sp(first)))
    (z_small,) = _matmul(h, w_small, 'nt', [F32], "proj_in_narrow")
    o_gla, st_all, *first_all = _gla_fwd(seq(z_big), seq(z_small), w2h, gbh, Bl, S, D, stage=_gather_pass(first_ici, sp(first)))
    acts = [_conv_fwd(z_big, conv_full, grp, Bl, S, D) for grp in range(3)]
    o_dn, s_all, *second_ici = _dn_fwd(seq(acts[0]), seq(acts[1]), seq(acts[2]), seq(z_small), alog_w, dtb_w, Bl, S, D,
                                       stage=_gather_ici(sh(second), sp(second)))
    mixed, *second_all = _merge_fwd(tok(o_gla), tok(o_dn), z_big, gla_norm, dn_norm, D,
                                    stage=_gather_pass(second_ici, sp(second)))
    slots = {BIG[i][0]: own_slot(g, bf16_shards[i]) for i, g in zip(first + second, first_all + second_all)}
    rows_joined = lambda t: t.reshape(4 * t.shape[1], t.shape[2])
    w_out_f, w_down_f, w_pg_f = rows_joined(slots['w_out']), rows_joined(slots['w_down']), rows_joined(slots['w_ple_gate'])
    w_up_s, w_pp_s = slots['w_up'], slots['w_ple_proj']
    add_norm = lambda r, e, g: (lambda x_new: (x_new, _rms(x_new, g)))(e + r)
    x1, h2 = _matmul(mixed, w_out_f, 'nn', [F32, BF16], "proj_out", epilogue=add_norm, extras=(xt, g_mlp), bm=512, bn=D)
    u, act = _matmul(h2, w_up_s, 'nn', [F32, BF16], "mlp_up", b_slots=True,
                     epilogue=lambda r: (r, jnp.square(jnp.maximum(r, 0.0))))
    x2, h3 = _matmul(act, w_down_f, 'nn', [F32, BF16], "mlp_down", epilogue=add_norm, extras=(x1, g_ple), bm=512, bn=D)
    (pp,) = _matmul(pt, w_pp_s, 'nn', [F32], "ple_proj", b_slots=True)
    gp, x3 = _matmul(h3, w_pg_f, 'nn', [F32, F32], "ple_gate",
                     epilogue=lambda r, e, q: (r, e + _sigmoid(r) * q), extras=(x2, pp), bm=512)
    dx3, d_gp, d_pp, loss_tile, d_g_final = _loss_fwd_bwd(x3, g_final.reshape(1, D), tgt, gp, pp, "loss")

    (g_pp,) = _matmul(pt, d_pp, 'tn', [F32], "ple_proj_dw", out_slots=True)
    (g_pg,) = _matmul(h3, d_gp, 'tn', [F32], "ple_gate_dw")
    (dh3,) = _matmul(d_gp, w_pg_f, 'nt', [F32], "ple_gate_dx")
    dx2, dx2b, d_g_ple = _rmsnorm_bwd_add(x2, g_ple, dh3, dx3, "rms3_bwd")
    (g_down,) = _matmul(act, dx2b, 'tn', [F32], "mlp_down_dw")
    (du,) = _matmul(dx2b, w_down_f, 'nt', [BF16], "mlp_down_dx",
                    epilogue=lambda r, e: (r * 2.0 * jnp.maximum(e, 0.0),), extras=(u,))
    (g_up,) = _matmul(h2, du, 'tn', [F32], "mlp_up_dw", out_slots=True)
    by_rows = lambda g: g.reshape(4, g.shape[0] // 4, g.shape[1])
    send_mlp = [g_up, by_rows(g_down), by_rows(g_pg), g_pp]
    dh2, *sib_mlp = _matmul(du, w_up_s, 'nt', [F32], "mlp_up_dx", b_slots=True, stage=_pair_exchange(send_mlp, split[2:]))
    dx1, dx1b, d_g_mlp = _rmsnorm_bwd_add(x1, g_mlp, dh2, dx2, "rms2_bwd")
    (g_out,) = _matmul(mixed, dx1b, 'tn', [F32], "proj_out_dw")
    dmix, sib_out = _matmul(dx1b, w_out_f, 'nt', [F32], "proj_out_dx", stage=_pair_exchange([by_rows(g_out)], split[1:2]))
    rest = [n for n, _ in BIG[1:]]
    send_rest, sib_rest = [by_rows(g_out)] + send_mlp, [sib_out] + sib_mlp
    sums_rest = [_pair_sum(s, f, c_idx, f"grad_pair_sum_{n}") for n, s, f in zip(rest, send_rest, sib_rest)]
    d_ogla, d_gg, d_odn, d_dz, d_ga, d_gb, d_gla_norm, d_dn_norm = _merge_bwd(
        tok(o_gla), tok(o_dn), z_big, gla_norm, dn_norm, dmix, D)
    d_q, d_k, d_v, dzs_gla, d_w2h, d_gbh = _gla_bwd(seq(z_big), seq(z_small), w2h, gbh, st_all, seq(d_ogla), Bl, S, D)
    d_qa, d_ka, d_va, d_zs, d_alog_w, d_dtb_w, *chips_rest = _dn_bwd(
        seq(acts[0]), seq(acts[1]), seq(acts[2]), seq(z_small), alog_w, dtb_w, s_all, seq(d_odn), dzs_gla, Bl, S, D,
        stage=_chip_scatter([b for _, b in sums_rest]))
    conv_b = [_conv_bwd(z_big, conv_full, tok(g), grp, Bl, S, D) for grp, g in enumerate([d_qa, d_ka, d_va])]
    dz_big = jnp.concatenate([tok(d_q), tok(d_k), tok(d_v), d_gg, conv_b[0][0], conv_b[1][0], conv_b[2][0], d_dz, d_ga,
                              d_gb], axis=1)
    dz_small = tok(d_zs)
    cut = 6 * D
    (d_w_a,) = _matmul(dz_big, h, 'tn', [F32], "proj_in_dw_a", m_cols=(0, cut))
    d_w_b, sib_a = _matmul(dz_big, h, 'tn', [F32], "proj_in_dw_b", m_cols=(cut, 3 * D), stage=_pair_exchange([d_w_a], [COLS]))
    halves_rest = [_final_sum(f, got, me_idx, f"grad_final_sum_{n}") for n, (f, _), got in zip(rest, sums_rest, chips_rest)]
    d_w_small, sib_b = _matmul(dz_small, h, 'tn', [F32], "proj_in_narrow_dw", stage=_pair_exchange([d_w_b], [COLS]))
    (sib_s,) = _run_stage(_pair_exchange([d_w_small], [COLS]), "grad_pair_exchange_narrow")
    parts = [_pair_sum(mine[None], theirs[None], c_idx, f"grad_pair_sum_w_in_{tag}", COLS)
             for tag, mine, theirs in (("a", d_w_a, sib_a), ("b", d_w_b, sib_b), ("narrow", d_w_small, sib_s))]
    joined = lambda k: _join_w_in(parts[0][k][0], parts[1][k][0], parts[2][k][0], D).reshape(4, n_in, D // 2)
    sum_in_f32, sum_in_bf16 = joined(0), joined(1)
    dh_a, chips_in, *pair_rest = _matmul(dz_big, w_big, 'nn', [F32], "proj_in_dx",
                                         stage=_both(_chip_scatter([sum_in_bf16]), _pair_allgather(halves_rest, split[1:])))
    half_in = _final_sum(sum_in_f32, chips_in, me_idx, "grad_final_sum_w_in", split[0])
    (dh,) = _matmul(dz_small, w_small, 'nn', [F32], "proj_in_narrow_dx", epilogue=lambda r, e: (e + r,), extras=(dh_a,))
    grad_x, _, d_g_mix = _rmsnorm_bwd_add(xt, g_mix, dh, dx1, "rms1_bwd")
    (pair_in,) = _run_stage(_pair_allgather([half_in], split[:1]), "grad_pair_allgather_w_in")
    reduced = {n: lax.dynamic_update_slice(o, hlf, (ic * hlf.shape[0], 0)) for n, o, hlf in zip(rest, pair_rest, halves_rest)}
    south = ic == 0
    g_in_t = jnp.concatenate([jnp.where(south, half_in, pair_in), jnp.where(south, pair_in, half_in)],
                             axis=1).reshape(n_in, 1, D)

    d_w2 = jnp.swapaxes(d_w2h, 0, 1).reshape(ZS, D // 2)[:LOWRANK]
    small_grads = {'g_mix': d_g_mix, 'gla_w2': d_w2, 'gla_b': d_gbh.reshape(1, D // 2), 'gla_norm': d_gla_norm,
                   'dn_a_log': d_alog_w[:, 0, 0].reshape(1, DN_HEADS), 'dn_dt_bias': d_dtb_w[:, 0, 0].reshape(1, DN_HEADS),
                   'dn_norm': d_dn_norm, 'g_mlp': d_g_mlp, 'g_ple': d_g_ple, 'g_final': d_g_final}
    names = [n for n in SMALL if n != 'dn_conv']
    total = _allreduce_small([small_grads[n] for n in names] + [cb[1] for cb in conv_b] + [loss_tile[:1]],
                             "allreduce_small_grads")
    gsmall = dict(zip(names, total[:len(names)]))
    loss = total[-1][0, 0]
    my_cols = lambda g: lax.dynamic_slice_in_dim(g, j_me * (g.shape[1] // 4), g.shape[1] // 4, axis=1)
    gsmall['gla_w2'] = my_cols(gsmall['gla_w2'])
    gsmall['dn_conv'] = my_cols(jnp.concatenate(total[len(names):len(names) + 3], axis=1))

    g_o, d_o, m_o, v_o = {}, {}, {}, {}
    d_in_t, nm_in_t, nv_in_t, g_out_t = _adamw(w_in_t, g_in_t, m_in_t, v_in_t, "adamw_w_in", with_grad=True)
    g_o['w_in'], d_o['w_in'], m_o['w_in'], v_o['w_in'] = [cols_last(t) for t in (g_out_t, d_in_t, nm_in_t, nv_in_t)]
    for n, _ in BIG[1:]:
        shp = wts[n].shape
        d2, nm2, nv2 = _adamw(shard2d[n], reduced[n], as2d(mom[n]), as2d(var[n]), f"adamw_{n}")
        g_o[n], d_o[n], m_o[n], v_o[n] = reduced[n].reshape(shp), d2.reshape(shp), nm2.reshape(shp), nv2.reshape(shp)
    ds, nms, nvs = _adamw_small([as2d(wts[n]) for n in SMALL], [as2d(gsmall[n]) for n in SMALL],
                                [as2d(mom[n]) for n in SMALL], [as2d(var[n]) for n in SMALL])
    for n, dd, mm, vv in zip(SMALL, ds, nms, nvs):
        shp = wts[n].shape
        g_o[n], d_o[n], m_o[n], v_o[n] = gsmall[n].reshape(shp), dd.reshape(shp), mm.reshape(shp), vv.reshape(shp)

    return (loss, grad_x.reshape(Bl, S, D), *[g_o[n] for n in WEIGHTS], *[d_o[n] for n in WEIGHTS],
            *[m_o[n] for n in WEIGHTS], *[v_o[n] for n in WEIGHTS])
```
